```python
import jax
import jax.numpy as jnp
from jax import lax
import numpy as np

D_MODEL = 1024
BATCH = 4
SEQ = 4096
DEPTH = 4

CTX_LEN = 256
GRID_W = 64
HEAD_DIM = 64
CONV_WIDTH = 256
CONV_K = 3
RET_HEADS = 4
RET_WIDTH = RET_HEADS * HEAD_DIM
RET_CHUNK = 128
ATTN_HEADS = 8
ATTN_KV_HEADS = 2
ATTN_GROUP = ATTN_HEADS // ATTN_KV_HEADS
ATTN_WIDTH = ATTN_HEADS * HEAD_DIM
KV_WIDTH = ATTN_KV_HEADS * HEAD_DIM
WINDOW = 128
BLOCK = 128
MIX_WIDTH = CONV_WIDTH + RET_WIDTH + ATTN_WIDTH
SPLIT_SIZES = (CONV_WIDTH,) * 3 + (RET_WIDTH,) * 5 + (ATTN_WIDTH, KV_WIDTH, KV_WIDTH)
SPLIT_IDX = tuple(sum(SPLIT_SIZES[:i + 1]) for i in range(len(SPLIT_SIZES) - 1))
IN_WIDTH = sum(SPLIT_SIZES)
N_EXPERTS = 16
CAPACITY_FACTOR = 2
D_EXPERT = 1024
ROPE_BASE = 10000.0
EPS = 1e-6
NEG_INF = -1e30

kernel_name = 'hybrid_parallel_group_dit'


def rms_norm(x, g):
    xf = x.astype(jnp.float32)
    y = xf * lax.rsqrt(jnp.mean(xf * xf, axis=-1, keepdims=True) + EPS)
    return (y * g.astype(jnp.float32)).astype(x.dtype)


def head_norm(o):
    of = o.astype(jnp.float32)
    mu = jnp.mean(of, axis=-1, keepdims=True)
    var = jnp.mean(jnp.square(of - mu), axis=-1, keepdims=True)
    return ((of - mu) * lax.rsqrt(var + EPS)).astype(o.dtype)


def modulate(h, shift, scale):
    return h * (1.0 + scale) + shift


def axial_rope_tables(n_tokens):
    rows = n_tokens // GRID_W
    row = jnp.repeat(jnp.arange(rows, dtype=jnp.float32), GRID_W)
    col = jnp.tile(jnp.arange(GRID_W, dtype=jnp.float32), rows)
    axis_dim = HEAD_DIM // 2
    inv_freq = ROPE_BASE ** (-jnp.arange(0, axis_dim, 2, dtype=jnp.float32) / axis_dim)
    ang = jnp.stack([row[:, None] * inv_freq, col[:, None] * inv_freq], axis=1)
    return jnp.cos(ang), jnp.sin(ang)


def apply_axial_rope(x, cos, sin):
    shp = x.shape
    xr = x.astype(jnp.float32).reshape(shp[:-1] + (2, 2, HEAD_DIM // 4))
    x1, x2 = xr[..., 0, :], xr[..., 1, :]
    out = jnp.stack([x1 * cos - x2 * sin, x2 * cos + x1 * sin], axis=-2)
    return out.reshape(shp).astype(x.dtype)


def to_heads(t, n_heads):
    b, n, _ = t.shape
    return t.reshape(b, n, n_heads, HEAD_DIM).transpose(0, 2, 1, 3)


def from_heads(t):
    b, h, n, d = t.shape
    return t.transpose(0, 2, 1, 3).reshape(b, n, h * d)


def gated_short_conv(gate_b, gate_c, u, conv_w):
    z = jnp.pad(gate_c * u, ((0, 0), (1, 1), (0, 0)))
    conv = z[:, :-2] * conv_w[0] + z[:, 1:-1] * conv_w[1] + z[:, 2:] * conv_w[2]
    return gate_b * conv


def retention_states(k, v, log_g, s0):
    b, h, n_tok, d = k.shape
    n_chunk = n_tok // RET_CHUNK
    kc = k.reshape(b, h, n_chunk, RET_CHUNK, d)
    vc = v.reshape(b, h, n_chunk, RET_CHUNK, d)
    j = jnp.arange(RET_CHUNK, dtype=jnp.float32)
    w_in = jnp.exp(log_g[:, None] * (RET_CHUNK - 1 - j))
    u = jnp.einsum('bhnld,bhnle,hl->bhnde', kc, vc, w_in).astype(jnp.float32)
    chunk_decay = jnp.exp(log_g * RET_CHUNK)[:, None, None]

    def step(s, u_n):
        return chunk_decay * s + u_n, s

    s_fin, s_prev = lax.scan(step, s0, jnp.moveaxis(u, 2, 0))
    return jnp.moveaxis(s_prev, 0, 2), s_fin


def retention_readout(q, k, v, log_g, s_prev):
    b, h, n_tok, d = q.shape
    n_chunk = n_tok // RET_CHUNK
    qc = q.reshape(b, h, n_chunk, RET_CHUNK, d)
    kc = k.reshape(b, h, n_chunk, RET_CHUNK, d)
    vc = v.reshape(b, h, n_chunk, RET_CHUNK, d)
    i = jnp.arange(RET_CHUNK, dtype=jnp.float32)
    diff = i[:, None] - i[None, :]
    dmask = jnp.where(diff >= 0, jnp.exp(log_g[:, None, None] * jnp.maximum(diff, 0.0)), 0.0)
    scores = jnp.einsum('bhnid,bhnjd->bhnij', qc, kc) * dmask[:, None]
    o = jnp.einsum('bhnij,bhnje->bhnie', scores, vc)
    q_dec = jnp.exp(log_g[:, None] * (i + 1.0))
    o = o + jnp.einsum('bhnid,bhnde,hi->bhnie', qc, s_prev, q_dec)
    return o.reshape(b, h, n_tok, d).astype(q.dtype)


def window_context_attention(q, k, v, k_ctx, v_ctx, sink):
    b, h, g, n_tok, d = q.shape
    nb = n_tok // BLOCK
    scale = d ** -0.5
    qb = q.reshape(b, h, g, nb, BLOCK, d)
    pad = ((0, 0), (0, 0), (BLOCK, BLOCK), (0, 0))
    kp = jnp.pad(k, pad).reshape(b, h, nb + 2, BLOCK, d)
    vp = jnp.pad(v, pad).reshape(b, h, nb + 2, BLOCK, d)
    kw = jnp.concatenate([kp[:, :, :-2], kp[:, :, 1:-1], kp[:, :, 2:]], axis=3)
    vw = jnp.concatenate([vp[:, :, :-2], vp[:, :, 1:-1], vp[:, :, 2:]], axis=3)
    s_win = jnp.einsum('bhgnqd,bhnkd->bhgnqk', qb, kw).astype(jnp.float32) * scale
    q_pos = jnp.arange(n_tok).reshape(nb, BLOCK)
    k_pos = (jnp.arange(nb) * BLOCK - BLOCK)[:, None] + jnp.arange(3 * BLOCK)[None, :]
    rel = q_pos[:, :, None] - k_pos[:, None, :]
    valid = (jnp.abs(rel) <= WINDOW) & (k_pos[:, None, :] >= 0) & (k_pos[:, None, :] < n_tok)
    s_win = jnp.where(valid, s_win, NEG_INF)
    s_ctx = jnp.einsum('bhgnqd,bhkd->bhgnqk', qb, k_ctx).astype(jnp.float32) * scale
    s_sink = jnp.broadcast_to(sink.astype(jnp.float32)[None, :, :, None, None, None], s_ctx.shape[:-1] + (1,))
    p = jax.nn.softmax(jnp.concatenate([s_win, s_ctx, s_sink], axis=-1), axis=-1).astype(v.dtype)
    n_win = 3 * BLOCK
    n_ctx = k_ctx.shape[2]
    o = (jnp.einsum('bhgnqk,bhnkd->bhgnqd', p[..., :n_win], vw)
         + jnp.einsum('bhgnqk,bhkd->bhgnqd', p[..., n_win:n_win + n_ctx], v_ctx))
    return o.reshape(b, h, g, n_tok, d)


def context_attention(q, k, v, sink):
    scale = q.shape[-1] ** -0.5
    s = jnp.einsum('bhgqd,bhkd->bhgqk', q, k).astype(jnp.float32) * scale
    s_sink = jnp.broadcast_to(sink.astype(jnp.float32)[None, :, :, None, None], s.shape[:-1] + (1,))
    p = jax.nn.softmax(jnp.concatenate([s, s_sink], axis=-1), axis=-1)[..., :-1].astype(v.dtype)
    return jnp.einsum('bhgqk,bhkd->bhgqd', p, v)


def expert_choice_ffn(h, w_router, w_gate, w_up, w_down):
    n_tok, d_model = h.shape[1], h.shape[2]
    cap = CAPACITY_FACTOR * n_tok // N_EXPERTS
    aff = jax.nn.softmax((h @ w_router).astype(jnp.float32), axis=-1)
    gate, idx = lax.top_k(jnp.swapaxes(aff, 1, 2), cap)
    xs = jax.vmap(lambda hb, ib: hb[ib])(h, idx)
    a = jnp.einsum('becd,edf->becf', xs, w_gate)
    u = jnp.einsum('becd,edf->becf', xs, w_up)
    y = jnp.einsum('becf,efd->becd', jax.nn.silu(a) * u, w_down) * gate[..., None].astype(h.dtype)

    def scatter(ib, yb):
        return jnp.zeros((n_tok, d_model), yb.dtype).at[ib.reshape(-1)].add(yb.reshape(-1, d_model))

    return jax.vmap(scatter)(idx, y)


def token_mixer(h_lat, h_ctx, w_in, conv_w, ret_logit, sink, w_out, cos, sin, need_ctx):
    b = h_lat.shape[0]
    (cb_l, cc_l, cx_l, rq_l, rk_l, rv_l, rgf_l, rgb_l, aq_l, ak_l, av_l) = jnp.split(h_lat @ w_in, SPLIT_IDX, axis=-1)
    (cb_c, cc_c, cx_c, rq_c, rk_c, rv_c, rgf_c, rgb_c, aq_c, ak_c, av_c) = jnp.split(h_ctx @ w_in, SPLIT_IDX, axis=-1)
    k_scale = HEAD_DIM ** -0.5

    conv_l = gated_short_conv(cb_l, cc_l, cx_l, conv_w)

    q_l = apply_axial_rope(to_heads(rq_l, RET_HEADS), cos, sin)
    k_l = apply_axial_rope(to_heads(rk_l, RET_HEADS) * k_scale, cos, sin)
    v_l = to_heads(rv_l, RET_HEADS)
    k_c = to_heads(rk_c, RET_HEADS) * k_scale
    v_c = to_heads(rv_c, RET_HEADS)
    q_c = to_heads(rq_c, RET_HEADS) if need_ctx else None
    gates_l = (rgf_l, rgb_l)
    gates_c = (rgf_c, rgb_c)
    s_zero = jnp.zeros((b, RET_HEADS, HEAD_DIM, HEAD_DIM), jnp.float32)
    ret_l = 0.0
    ret_c = 0.0
    for d in range(2):
        log_g = jax.nn.log_sigmoid(ret_logit[d].astype(jnp.float32))
        fl = (lambda t: jnp.flip(t, axis=2)) if d == 1 else (lambda t: t)
        sp_c, s_c = retention_states(fl(k_c), fl(v_c), log_g, s_zero)
        sp_l, _ = retention_states(fl(k_l), fl(v_l), log_g, s_c)
        o_l = fl(retention_readout(fl(q_l), fl(k_l), fl(v_l), log_g, sp_l))
        ret_l = ret_l + from_heads(head_norm(o_l)) * jax.nn.silu(gates_l[d])
        if need_ctx:
            o_c = fl(retention_readout(fl(q_c), fl(k_c), fl(v_c), log_g, sp_c))
            ret_c = ret_c + from_heads(head_norm(o_c)) * jax.nn.silu(gates_c[d])

    n_tok = h_lat.shape[1]
    qa_l = apply_axial_rope(to_heads(aq_l, ATTN_HEADS), cos, sin).reshape(b, ATTN_KV_HEADS, ATTN_GROUP, n_tok, HEAD_DIM)
    ka_l = apply_axial_rope(to_heads(ak_l, ATTN_KV_HEADS), cos, sin)
    va_l = to_heads(av_l, ATTN_KV_HEADS)
    ka_c = to_heads(ak_c, ATTN_KV_HEADS)
    va_c = to_heads(av_c, ATTN_KV_HEADS)
    sink_hg = sink.reshape(ATTN_KV_HEADS, ATTN_GROUP)
    attn_l = window_context_attention(qa_l, ka_l, va_l, ka_c, va_c, sink_hg)
    attn_l = from_heads(attn_l.reshape(b, ATTN_HEADS, n_tok, HEAD_DIM))
    y_l = jnp.concatenate([conv_l, ret_l, attn_l], axis=-1) @ w_out
    if not need_ctx:
        return y_l, None

    n_ctx = h_ctx.shape[1]
    conv_c = gated_short_conv(cb_c, cc_c, cx_c, conv_w)
    qa_c = to_heads(aq_c, ATTN_HEADS).reshape(b, ATTN_KV_HEADS, ATTN_GROUP, n_ctx, HEAD_DIM)
    attn_c = context_attention(qa_c, ka_c, va_c, sink_hg)
    attn_c = from_heads(attn_c.reshape(b, ATTN_HEADS, n_ctx, HEAD_DIM))
    y_c = jnp.concatenate([conv_c, ret_c, attn_c], axis=-1) @ w_out
    return y_l, y_c


def setup_inputs(seed: int = 0) -> dict:
    key = jax.random.key(seed)
    ks = jax.random.split(key, 18)
    f32 = jnp.float32

    def nrm(k, shape, s):
        return jax.random.normal(k, shape, f32) * s

    ret_base = jnp.log(jnp.exp2(5.0 + jnp.arange(RET_HEADS, dtype=f32)) - 1.0)
    return {
        'x': nrm(ks[0], (BATCH, SEQ, D_MODEL), 1.0),
        'c': nrm(ks[1], (BATCH, D_MODEL), 1.0),
        'ctx': nrm(ks[2], (BATCH, CTX_LEN, D_MODEL), 1.0),
        'c_ctx': nrm(ks[3], (D_MODEL,), 1.0),
        'w_mod': nrm(ks[4], (DEPTH, D_MODEL, 6 * D_MODEL), 0.5 * D_MODEL ** -0.5),
        'b_mod': nrm(ks[5], (DEPTH, 6 * D_MODEL), 0.02),
        'norm1_g': 1.0 + nrm(ks[6], (DEPTH, D_MODEL), 0.02),
        'norm2_g': 1.0 + nrm(ks[7], (DEPTH, D_MODEL), 0.02),
        'w_in': nrm(ks[8], (DEPTH, D_MODEL, IN_WIDTH), D_MODEL ** -0.5),
        'conv_w': nrm(ks[9], (DEPTH, CONV_K, CONV_WIDTH), CONV_K ** -0.5),
        'ret_decay_logit': ret_base + nrm(ks[10], (DEPTH, 2, RET_HEADS), 0.1),
        'attn_sink': nrm(ks[11], (DEPTH, ATTN_HEADS), 0.5),
        'w_out': nrm(ks[12], (DEPTH, MIX_WIDTH, D_MODEL), MIX_WIDTH ** -0.5),
        'w_router': nrm(ks[13], (DEPTH, D_MODEL, N_EXPERTS), D_MODEL ** -0.5),
        'w_gate': nrm(ks[14], (DEPTH, N_EXPERTS, D_MODEL, D_EXPERT), D_MODEL ** -0.5),
        'w_up': nrm(ks[15], (DEPTH, N_EXPERTS, D_MODEL, D_EXPERT), D_MODEL ** -0.5),
        'w_down': nrm(ks[16], (DEPTH, N_EXPERTS, D_EXPERT, D_MODEL), D_EXPERT ** -0.5),
        'final_g': 1.0 + nrm(ks[17], (D_MODEL,), 0.02),
    }


def reference(x, c, ctx, c_ctx, w_mod, b_mod, norm1_g, norm2_g, w_in, conv_w, ret_decay_logit,
              attn_sink, w_out, w_router, w_gate, w_up, w_down, final_g):
    cos, sin = axial_rope_tables(x.shape[1])
    silu_c = jax.nn.silu(c)
    silu_cc = jax.nn.silu(c_ctx)
    for l in range(DEPTH):
        need_ctx = l < DEPTH - 1
        mod_l = (silu_c @ w_mod[l] + b_mod[l])[:, None, :]
        mod_c = (silu_cc @ w_mod[l] + b_mod[l])[None, None, :]
        sh1, sc1, g1, sh2, sc2, g2 = jnp.split(mod_l, 6, axis=-1)
        csh1, csc1, cg1, csh2, csc2, cg2 = jnp.split(mod_c, 6, axis=-1)

        h_l = modulate(rms_norm(x, norm1_g[l]), sh1, sc1)
        h_c = modulate(rms_norm(ctx, norm1_g[l]), csh1, csc1)
        y_l, y_c = token_mixer(h_l, h_c, w_in[l], conv_w[l], ret_decay_logit[l], attn_sink[l],
                               w_out[l], cos, sin, need_ctx)
        x = x + g1 * y_l
        h_l = modulate(rms_norm(x, norm2_g[l]), sh2, sc2)
        x = x + g2 * expert_choice_ffn(h_l, w_router[l], w_gate[l], w_up[l], w_down[l])
        if need_ctx:
            ctx = ctx + cg1 * y_c
            h_c = modulate(rms_norm(ctx, norm2_g[l]), csh2, csc2)
            ctx = ctx + cg2 * expert_choice_ffn(h_c, w_router[l], w_gate[l], w_up[l], w_down[l])
    return rms_norm(x, final_g)
```

```python
import functools

import jax
import jax.numpy as jnp
from jax import lax
from jax.experimental import pallas as pl
from jax.experimental.pallas import tpu as pltpu

F32 = jnp.float32
BF16 = jnp.bfloat16
I32 = jnp.int32
HIGHEST = lax.Precision.HIGHEST

HEAD_DIM = 64
CONV_WIDTH = 256
RET_HEADS = 4
RET_WIDTH = RET_HEADS * HEAD_DIM
ATTN_HEADS = 8
ATTN_KV_HEADS = 2
ATTN_GROUP = ATTN_HEADS // ATTN_KV_HEADS
ATTN_WIDTH = ATTN_HEADS * HEAD_DIM
KV_WIDTH = ATTN_KV_HEADS * HEAD_DIM
CHUNK = 128
GRID_W = 64
N_EXPERTS = 16
CAPACITY_FACTOR = 2
ROPE_BASE = 10000.0
EPS = 1e-6
NEG_INF = -1e30
QK_SCALE = HEAD_DIM ** -0.5

LANES = 128
SUBLANES = 8
VMEM_LIMIT_BYTES = 56 * 1024 * 1024

O_CB = 0
O_CC = O_CB + CONV_WIDTH
O_CX = O_CC + CONV_WIDTH
O_RQ = O_CX + CONV_WIDTH
O_RK = O_RQ + RET_WIDTH
O_RV = O_RK + RET_WIDTH
O_GF = O_RV + RET_WIDTH
O_GB = O_GF + RET_WIDTH
O_AQ = O_GB + RET_WIDTH
O_AK = O_AQ + ATTN_WIDTH
O_AV = O_AK + KV_WIDTH
IN_WIDTH = O_AV + KV_WIDTH

M_SH1, M_SC1, M_G1, M_SH2, M_SC2, M_G2 = range(6)
M_CTX = 6
M_ROWS = 16

TOKEN_TILE = 256


def _params(*sem):
    return pltpu.CompilerParams(dimension_semantics=sem, vmem_limit_bytes=VMEM_LIMIT_BYTES)


def _dot(a, b):
    return jnp.dot(a, b, preferred_element_type=F32)


def _dot_nt(a, b):
    return lax.dot_general(a, b, (((1,), (1,)), ((), ())), preferred_element_type=F32)


def _silu(v):
    return v * jax.nn.sigmoid(v)


def _mod_kernel(c_ref, w_ref, b_ref, o_ref):
    s = _silu(c_ref[...])
    o_ref[0] = jnp.dot(s, w_ref[0], precision=HIGHEST, preferred_element_type=F32) + b_ref[0]


def _mod_vectors(c_rows, w_mod, b_mod):
    depth, d_model, width = w_mod.shape
    tn = 1536
    return pl.pallas_call(
        _mod_kernel,
        grid=(depth, width // tn),
        in_specs=[
            pl.BlockSpec((SUBLANES, d_model), lambda l, n: (0, 0)),
            pl.BlockSpec((1, d_model, tn), lambda l, n: (l, 0, n)),
            pl.BlockSpec((1, 1, tn), lambda l, n: (l, 0, n)),
        ],
        out_specs=pl.BlockSpec((1, SUBLANES, tn), lambda l, n: (l, 0, n)),
        out_shape=jax.ShapeDtypeStruct((depth, SUBLANES, width), F32),
        compiler_params=_params("parallel", "parallel"),
        name="mod_vectors",
    )(c_rows, w_mod, b_mod.reshape(depth, 1, width))


def _log_sigmoid(v):
    return -jnp.log(1.0 + jnp.exp(-v))


def _tables_kernel(lgl_ref, lgr_ref, qdec_ref, win_ref, cd_ref, dm_ref):
    pos = lax.broadcasted_iota(I32, (CHUNK, RET_WIDTH), 0).astype(F32)
    ri = (lax.broadcasted_iota(I32, (RET_HEADS * CHUNK, CHUNK), 0) & (CHUNK - 1)).astype(F32)
    rj = lax.broadcasted_iota(I32, (RET_HEADS * CHUNK, CHUNK), 1).astype(F32)
    for d in range(2):
        lg = _log_sigmoid(lgl_ref[0, d])
        lg1 = lg[0:1, :]
        if d == 0:
            qdec_ref[0, d] = jnp.exp(lg1 * (pos + 1.0))
            win_ref[0, d] = jnp.exp(lg1 * (CHUNK - 1.0 - pos))
            diff = ri - rj
        else:
            qdec_ref[0, d] = jnp.exp(lg1 * (CHUNK - pos))
            win_ref[0, d] = jnp.exp(lg1 * pos)
            diff = rj - ri
        cd_ref[0, d] = jnp.exp(lg * float(CHUNK))
        lr = _log_sigmoid(lgr_ref[0, d])
        dm_ref[0, d] = jnp.where(diff >= 0.0, jnp.exp(lr * jnp.maximum(diff, 0.0)), 0.0)


def _decay_tables(ret_decay_logit):
    depth = ret_decay_logit.shape[0]
    lg = ret_decay_logit.astype(F32)
    lgl = jnp.broadcast_to(jnp.repeat(lg, HEAD_DIM, axis=-1)[:, :, None, :], (depth, 2, SUBLANES, RET_WIDTH))
    lgr = jnp.broadcast_to(jnp.repeat(lg, CHUNK, axis=-1)[:, :, :, None], (depth, 2, RET_HEADS * CHUNK, CHUNK))
    return pl.pallas_call(
        _tables_kernel,
        grid=(depth,),
        in_specs=[
            pl.BlockSpec((1, 2, SUBLANES, RET_WIDTH), lambda l: (l, 0, 0, 0)),
            pl.BlockSpec((1, 2, RET_HEADS * CHUNK, CHUNK), lambda l: (l, 0, 0, 0)),
        ],
        out_specs=[
            pl.BlockSpec((1, 2, CHUNK, RET_WIDTH), lambda l: (l, 0, 0, 0)),
            pl.BlockSpec((1, 2, CHUNK, RET_WIDTH), lambda l: (l, 0, 0, 0)),
            pl.BlockSpec((1, 2, SUBLANES, RET_WIDTH), lambda l: (l, 0, 0, 0)),
            pl.BlockSpec((1, 2, RET_HEADS * CHUNK, CHUNK), lambda l: (l, 0, 0, 0)),
        ],
        out_shape=[
            jax.ShapeDtypeStruct((depth, 2, CHUNK, RET_WIDTH), F32),
            jax.ShapeDtypeStruct((depth, 2, CHUNK, RET_WIDTH), F32),
            jax.ShapeDtypeStruct((depth, 2, SUBLANES, RET_WIDTH), F32),
            jax.ShapeDtypeStruct((depth, 2, RET_HEADS * CHUNK, CHUNK), F32),
        ],
        compiler_params=_params("parallel"),
        name="decay_tables",
    )(lgl, lgr)


def _rope(v, cs, sn):
    lane = lax.broadcasted_iota(I32, (1, LANES), 1)
    first = (lane & 31) < 16
    outs = []
    for g in range(v.shape[1] // LANES):
        vg = v[:, g * LANES:(g + 1) * LANES]
        sw = jnp.where(first, pltpu.roll(vg, LANES - 16, 1), pltpu.roll(vg, 16, 1))
        outs.append(vg * cs + sw * sn)
    return outs[0] if len(outs) == 1 else jnp.concatenate(outs, axis=1)


def _swap_halves(v):
    return jnp.concatenate([v[:, HEAD_DIM:], v[:, :HEAD_DIM]], axis=1)


def _from_gather_layout(f_ref, tm):
    return jnp.concatenate(
        [f_ref[0, pl.ds(q, tm, stride=SUBLANES), :] for q in range(SUBLANES)], axis=1)


def _rms(v):
    return v * lax.rsqrt(jnp.mean(v * v, axis=-1, keepdims=True) + EPS)


def _inproj_kernel(has_ffn, n_ctx_tiles, *refs):
    if has_ffn:
        x_ref, f_ref, mod_ref, g_ref, w_ref, cs_ref, sn_ref = refs[:7]
        outs = refs[7:]
        xo_ref = outs[0]
        outs = outs[1:]
    else:
        x_ref, mod_ref, g_ref, w_ref, cs_ref, sn_ref = refs[:6]
        outs = refs[6:]
    cb_ref, z_ref, rq_ref, rk_ref, rv_ref, gf_ref, gb_ref, aq_ref, ak_ref, aks_ref, av_ref, avs_ref = outs
    tm = x_ref.shape[1]
    is_ctx = pl.program_id(1) < n_ctx_tiles
    m = mod_ref[0]

    def row(r):
        return jnp.where(is_ctx, m[M_CTX + r:M_CTX + r + 1, :], m[r:r + 1, :])

    x = x_ref[0]
    if has_ffn:
        x = x + row(M_G2) * _from_gather_layout(f_ref, tm)
        xo_ref[0] = x
    h = (_rms(x) * g_ref[...]) * (1.0 + row(M_SC1)) + row(M_SH1)
    h = h.astype(BF16)
    cs = cs_ref[...]
    sn = sn_ref[...]

    def proj(a, b):
        return _dot(h, w_ref[:, a:b])

    cb_ref[0] = proj(O_CB, O_CC)
    z_ref[0] = proj(O_CC, O_CX) * proj(O_CX, O_RQ)
    rq_ref[0] = _rope(proj(O_RQ, O_RK), cs, sn).astype(BF16)
    rk_ref[0] = _rope(proj(O_RK, O_RV) * QK_SCALE, cs, sn).astype(BF16)
    rv_ref[0] = proj(O_RV, O_GF).astype(BF16)
    gf_ref[0] = _silu(proj(O_GF, O_GB))
    gb_ref[0] = _silu(proj(O_GB, O_AQ))
    aq_ref[0] = (_rope(proj(O_AQ, O_AK), cs, sn) * QK_SCALE).astype(BF16)
    ak = _rope(proj(O_AK, O_AV), cs, sn)
    ak_ref[0] = ak.astype(BF16)
    aks_ref[0] = _swap_halves(ak).astype(BF16)
    av = proj(O_AV, IN_WIDTH)
    av_ref[0] = av.astype(BF16)
    avs_ref[0] = _swap_halves(av).astype(BF16)


def _inproj(x, ffn, mrows, norm_g, w_in_bf, layer, rope_cs, rope_sn, n_ctx):
    b, tu, d = x.shape
    tm = TOKEN_TILE
    has_ffn = ffn is not None
    tok = lambda width: pl.BlockSpec((1, tm, width), lambda i, j: (i, j, 0))
    in_specs = [tok(d)]
    args = [x]
    if has_ffn:
        in_specs.append(pl.BlockSpec((1, tm * SUBLANES, LANES), lambda i, j: (i, j, 0)))
        args.append(ffn)
    in_specs += [
        pl.BlockSpec((1, M_ROWS, d), lambda i, j: (i, 0, 0)),
        pl.BlockSpec((1, d), lambda i, j: (0, 0)),
        pl.BlockSpec((None, d, IN_WIDTH), lambda i, j: (layer, 0, 0)),
        pl.BlockSpec((tm, LANES), lambda i, j: (j, 0)),
        pl.BlockSpec((tm, LANES), lambda i, j: (j, 0)),
    ]
    args += [mrows, norm_g.reshape(1, d), w_in_bf, rope_cs, rope_sn]
    widths = [(CONV_WIDTH, F32), (CONV_WIDTH, F32), (RET_WIDTH, BF16), (RET_WIDTH, BF16), (RET_WIDTH, BF16),
              (RET_WIDTH, F32), (RET_WIDTH, F32), (ATTN_WIDTH, BF16), (KV_WIDTH, BF16), (KV_WIDTH, BF16),
              (KV_WIDTH, BF16), (KV_WIDTH, BF16)]
    out_specs = [tok(w) for w, _ in widths]
    out_shape = [jax.ShapeDtypeStruct((b, tu, w), dt) for w, dt in widths]
    if has_ffn:
        out_specs = [tok(d)] + out_specs
        out_shape = [jax.ShapeDtypeStruct((b, tu, d), F32)] + out_shape
    res = pl.pallas_call(
        functools.partial(_inproj_kernel, has_ffn, n_ctx // tm),
        grid=(b, tu // tm),
        in_specs=in_specs,
        out_specs=out_specs,
        out_shape=out_shape,
        compiler_params=_params("parallel", "parallel"),
        name="inproj",
    )(*args)
    if has_ffn:
        return res[0], res[1:]
    return x, res


def _head_block_mask(n):
    r = lax.broadcasted_iota(I32, (n, n), 0) // HEAD_DIM
    c = lax.broadcasted_iota(I32, (n, n), 1) // HEAD_DIM
    return r == c


def _states_kernel(n_ctx_chunks, n_chunks, rk_ref, rv_ref, win_ref, cd_ref, sp_ref, s_scr):
    same_head = _head_block_mask(RET_WIDTH)
    for d in range(2):
        s_scr[...] = jnp.zeros_like(s_scr)
        win = win_ref[d]
        cd = cd_ref[d][0:1, :]

        def body(i, carry, d=d, win=win, cd=cd):
            if d == 0:
                c = i
            else:
                c = jnp.where(i < n_ctx_chunks, n_ctx_chunks - 1 - i, n_chunks + n_ctx_chunks - 1 - i)
            off = pl.multiple_of(c * CHUNK, CHUNK)
            kw = rk_ref[0, pl.ds(off, CHUNK), :].astype(F32) * win
            v = rv_ref[0, pl.ds(off, CHUNK), :]
            u = _dot(kw.T.astype(BF16), v)
            s = s_scr[...]
            sp_ref[0, d, pl.ds(c, 1)] = s.astype(BF16)[None]
            s_scr[...] = s * cd + jnp.where(same_head, u, 0.0)
            return carry

        lax.fori_loop(0, n_chunks, body, 0)


def _ret_states(rk, rv, win, cd, n_ctx):
    b, tu, w = rk.shape
    n_chunks = tu // CHUNK
    return pl.pallas_call(
        functools.partial(_states_kernel, n_ctx // CHUNK, n_chunks),
        grid=(b,),
        in_specs=[
            pl.BlockSpec((1, tu, w), lambda i: (i, 0, 0)),
            pl.BlockSpec((1, tu, w), lambda i: (i, 0, 0)),
            pl.BlockSpec((2, CHUNK, w), lambda i: (0, 0, 0)),
            pl.BlockSpec((2, SUBLANES, w), lambda i: (0, 0, 0)),
        ],
        out_specs=pl.BlockSpec((1, 2, n_chunks, w, w), lambda i: (i, 0, 0, 0, 0)),
        out_shape=jax.ShapeDtypeStruct((b, 2, n_chunks, w, w), BF16),
        scratch_shapes=[pltpu.VMEM((w, w), F32)],
        compiler_params=_params("parallel"),
        name="ret_states",
    )(rk, rv, win, cd)


def _mixer_kernel(n_ctx_chunks, n_chunks,
                  cb_ref, zp_ref, zc_ref, zn_ref, cw_ref,
                  rq_ref, rk_ref, rv_ref, gf_ref, gb_ref, sp_ref, dm_ref, qdec_ref,
                  aq_ref, akp_ref, akc_ref, akn_ref, akx_ref, aksp_ref, aksc_ref, aksn_ref, aksx_ref,
                  avp_ref, avc_ref, avn_ref, avx_ref, avsp_ref, avsc_ref, avsn_ref, avsx_ref,
                  sink_ref, mix_ref):
    c = pl.program_id(1)
    is_lat = c >= n_ctx_chunks

    z = zc_ref[0]
    row = lax.broadcasted_iota(I32, (CHUNK, 1), 0)
    has_prev = jnp.logical_and(c != 0, c != n_ctx_chunks)
    has_next = jnp.logical_and(c != n_ctx_chunks - 1, c != n_chunks - 1)
    z_before = jnp.where(row == 0, jnp.where(has_prev, zp_ref[0, CHUNK - 1:CHUNK, :], 0.0),
                         pltpu.roll(z, 1, 0))
    z_after = jnp.where(row == CHUNK - 1, jnp.where(has_next, zn_ref[0, 0:1, :], 0.0),
                        pltpu.roll(z, CHUNK - 1, 0))
    cw = cw_ref[...]
    conv = cb_ref[0] * (z_before * cw[0:1, :] + z * cw[1:2, :] + z_after * cw[2:3, :])

    q = rq_ref[0]
    k = rk_ref[0]
    v = rv_ref[0]
    lane_head = lax.broadcasted_iota(I32, (1, RET_WIDTH), 1) // HEAD_DIM
    qz = jnp.zeros_like(q)
    qs = jnp.concatenate([jnp.where(lane_head == hh, q, qz) for hh in range(RET_HEADS)], axis=0)
    scores = _dot_nt(qs, k)
    qf = q.astype(F32)
    gsum = jnp.where(_head_block_mask(RET_WIDTH), 1.0 / HEAD_DIM, 0.0).astype(F32)
    gates = (gf_ref[0], gb_ref[0])
    ret = jnp.zeros((CHUNK, RET_WIDTH), F32)
    for d in range(2):
        p = (scores * dm_ref[d]).astype(BF16)
        o_all = _dot(p, v)
        o = _dot((qf * qdec_ref[d]).astype(BF16), sp_ref[0, d, 0])
        for hh in range(RET_HEADS):
            o = o + jnp.where(lane_head == hh, o_all[hh * CHUNK:(hh + 1) * CHUNK, :], 0.0)
        mu = jnp.dot(o, gsum, precision=HIGHEST, preferred_element_type=F32)
        dl = o - mu
        var = jnp.dot(dl * dl, gsum, precision=HIGHEST, preferred_element_type=F32)
        ret = ret + dl * lax.rsqrt(var + EPS) * gates[d]

    qa = aq_ref[0]
    keys = (jnp.concatenate([akp_ref[0], akc_ref[0], akn_ref[0], akx_ref[0]], axis=0),
            jnp.concatenate([aksp_ref[0], aksc_ref[0], aksn_ref[0], aksx_ref[0]], axis=0))
    vals = (jnp.concatenate([avp_ref[0], avc_ref[0], avn_ref[0], avx_ref[0]], axis=0),
            jnp.concatenate([avsp_ref[0], avsc_ref[0], avsn_ref[0], avsx_ref[0]], axis=0))
    n_ctx = akx_ref.shape[1]
    iq = lax.broadcasted_iota(I32, (2 * CHUNK, CHUNK), 0) & (CHUNK - 1)
    ik = lax.broadcasted_iota(I32, (2 * CHUNK, CHUNK), 1)
    ok_prev = jnp.logical_and(jnp.logical_and(is_lat, c - 1 >= n_ctx_chunks), ik >= iq)
    ok_cur = jnp.logical_and(is_lat, ik >= 0)
    ok_next = jnp.logical_and(jnp.logical_and(is_lat, c + 1 <= n_chunks - 1), ik <= iq)
    ok_ctx = jnp.ones((2 * CHUNK, n_ctx), jnp.bool_)
    valid = jnp.concatenate([ok_prev, ok_cur, ok_next, ok_ctx], axis=1)
    half = lax.broadcasted_iota(I32, (1, LANES), 1) // HEAD_DIM
    qa_z = jnp.zeros((CHUNK, LANES), BF16)
    cols = [None] * (ATTN_HEADS // 2)
    for hk in range(ATTN_KV_HEADS):
        outs = []
        for par in range(2):
            ja, jb = 2 * hk, 2 * hk + 1
            qst = jnp.concatenate([
                jnp.where(half == par, qa[:, ja * LANES:(ja + 1) * LANES], qa_z),
                jnp.where(half == par, qa[:, jb * LANES:(jb + 1) * LANES], qa_z)], axis=0)
            sel = 0 if par == hk else 1
            s = jnp.where(valid, _dot_nt(qst, keys[sel]), NEG_INF)
            ha, hb = ATTN_GROUP * hk + par, ATTN_GROUP * hk + par + 2
            snk = jnp.concatenate([sink_ref[ha * CHUNK:(ha + 1) * CHUNK, 0:1],
                                   sink_ref[hb * CHUNK:(hb + 1) * CHUNK, 0:1]], axis=0)
            mx = jnp.maximum(jnp.max(s, axis=1, keepdims=True), snk)
            e = jnp.exp(s - mx)
            den = jnp.sum(e, axis=1, keepdims=True) + jnp.exp(snk - mx)
            outs.append(_dot((e / den).astype(BF16), vals[sel]))
        cols[2 * hk] = jnp.where(half == 0, outs[0][:CHUNK], outs[1][:CHUNK])
        cols[2 * hk + 1] = jnp.where(half == 0, outs[0][CHUNK:], outs[1][CHUNK:])

    mix_ref[0] = jnp.concatenate([conv, ret] + cols, axis=1).astype(BF16)


def _mixer(proj, sp, tabs, conv_w, sink_rows, n_ctx):
    cb, z, rq, rk, rv, gf, gb, aq, ak, aks, av, avs = proj
    dm, qdec = tabs
    b, tu, _ = cb.shape
    n_chunks = tu // CHUNK
    last = n_chunks - 1
    cur = lambda w: pl.BlockSpec((1, CHUNK, w), lambda i, c: (i, c, 0))
    prev = lambda w: pl.BlockSpec((1, CHUNK, w), lambda i, c: (i, jnp.maximum(c - 1, 0), 0))
    nxt = lambda w: pl.BlockSpec((1, CHUNK, w), lambda i, c: (i, jnp.minimum(c + 1, last), 0))
    ctx = lambda w: pl.BlockSpec((1, n_ctx, w), lambda i, c: (i, 0, 0))
    full = lambda shape: pl.BlockSpec(shape, lambda i, c: (0,) * len(shape))
    kv4 = lambda: [prev(KV_WIDTH), cur(KV_WIDTH), nxt(KV_WIDTH), ctx(KV_WIDTH)]
    in_specs = ([cur(CONV_WIDTH), prev(CONV_WIDTH), cur(CONV_WIDTH), nxt(CONV_WIDTH), full(conv_w.shape)]
                + [cur(RET_WIDTH)] * 5
                + [pl.BlockSpec((1, 2, 1, RET_WIDTH, RET_WIDTH), lambda i, c: (i, 0, c, 0, 0)),
                   full(dm.shape), full(qdec.shape)]
                + [cur(ATTN_WIDTH)] + kv4() + kv4() + kv4() + kv4()
                + [full(sink_rows.shape)])
    args = ([cb, z, z, z, conv_w, rq, rk, rv, gf, gb, sp, dm, qdec, aq]
            + [ak] * 4 + [aks] * 4 + [av] * 4 + [avs] * 4 + [sink_rows])
    return pl.pallas_call(
        functools.partial(_mixer_kernel, n_ctx // CHUNK, n_chunks),
        grid=(b, n_chunks),
        in_specs=in_specs,
        out_specs=pl.BlockSpec((1, CHUNK, CONV_WIDTH + RET_WIDTH + ATTN_WIDTH), lambda i, c: (i, c, 0)),
        out_shape=jax.ShapeDtypeStruct((b, tu, CONV_WIDTH + RET_WIDTH + ATTN_WIDTH), BF16),
        compiler_params=_params("parallel", "parallel"),
        name="mixer",
    )(*args)


def _outproj_kernel(n_ctx_tiles, mix_ref, x_ref, mod_ref, g_ref, w_ref, wr_ref, xo_ref, hg_ref, aff_ref):
    tm = x_ref.shape[1]
    is_ctx = pl.program_id(1) < n_ctx_tiles
    m = mod_ref[0]

    def row(r):
        return jnp.where(is_ctx, m[M_CTX + r:M_CTX + r + 1, :], m[r:r + 1, :])

    x = x_ref[0] + row(M_G1) * _dot(mix_ref[0], w_ref[...])
    xo_ref[0] = x
    h = (_rms(x) * g_ref[...]) * (1.0 + row(M_SC2)) + row(M_SH2)
    for q in range(SUBLANES):
        hg_ref[0, pl.ds(q, tm, stride=SUBLANES), :] = h[:, q * LANES:(q + 1) * LANES]
    logits = lax.dot_general(wr_ref[...], h, (((1,), (1,)), ((), ())),
                             precision=HIGHEST, preferred_element_type=F32)
    e = jnp.exp(logits - jnp.max(logits, axis=0, keepdims=True))
    aff_ref[0] = e / jnp.sum(e, axis=0, keepdims=True)


def _outproj(mix, x, mrows, norm_g, w_out_bf, layer, w_router_t, n_ctx):
    b, tu, d = x.shape
    tm = TOKEN_TILE
    n_exp = w_router_t.shape[0]
    return pl.pallas_call(
        functools.partial(_outproj_kernel, n_ctx // tm),
        grid=(b, tu // tm),
        in_specs=[
            pl.BlockSpec((1, tm, mix.shape[2]), lambda i, j: (i, j, 0)),
            pl.BlockSpec((1, tm, d), lambda i, j: (i, j, 0)),
            pl.BlockSpec((1, M_ROWS, d), lambda i, j: (i, 0, 0)),
            pl.BlockSpec((1, d), lambda i, j: (0, 0)),
            pl.BlockSpec((None,) + w_out_bf.shape[1:], lambda i, j: (layer, 0, 0)),
            pl.BlockSpec((n_exp, d), lambda i, j: (0, 0)),
        ],
        out_specs=[
            pl.BlockSpec((1, tm, d), lambda i, j: (i, j, 0)),
            pl.BlockSpec((1, tm * SUBLANES, LANES), lambda i, j: (i, j, 0)),
            pl.BlockSpec((1, n_exp, tm), lambda i, j: (i, 0, j)),
        ],
        out_shape=[
            jax.ShapeDtypeStruct((b, tu, d), F32),
            jax.ShapeDtypeStruct((b, tu * SUBLANES, LANES), F32),
            jax.ShapeDtypeStruct((b, n_exp, tu), F32),
        ],
        compiler_params=_params("parallel", "parallel"),
        name="outproj",
    )(mix, x, mrows, norm_g.reshape(1, d), w_out_bf, w_router_t)


def _cumsum_lanes(mask_bf):
    n = mask_bf.shape[1]
    upper = (lax.broadcasted_iota(I32, (LANES, LANES), 0) <= lax.broadcasted_iota(I32, (LANES, LANES), 1))
    upper = jnp.where(upper, 1.0, 0.0).astype(BF16)
    off = jnp.zeros((mask_bf.shape[0], 1), F32)
    outs = []
    for jb in range(n // LANES):
        blk = _dot(mask_bf[:, jb * LANES:(jb + 1) * LANES], upper) + off
        off = blk[:, LANES - 1:LANES]
        outs.append(blk)
    return jnp.concatenate(outs, axis=1)


def _route_kernel(streams, aff_ref, idx_ref, gate_ref, cum_scr, sa_scr):
    a = aff_ref[0]
    n_exp = a.shape[0]
    for lo, n, k, slot0 in streams:
        v = a[:, lo:lo + n]

        def search(i, t, v=v, k=k):
            cand = t | jnp.left_shift(jnp.int32(1), 30 - i)
            cnt = jnp.sum(jnp.where(v >= pltpu.bitcast(cand, F32), 1, 0), axis=1, keepdims=True)
            return jnp.where(cnt >= k, cand, t)

        thr = lax.fori_loop(0, 31, search, jnp.zeros((n_exp, 1), I32))
        above = pltpu.bitcast(thr + 1, F32)
        gt = v >= above
        eq = jnp.logical_and(v >= pltpu.bitcast(thr, F32), v < above)
        need = (k - jnp.sum(jnp.where(gt, 1, 0), axis=1, keepdims=True)).astype(F32)
        cums = _cumsum_lanes(jnp.concatenate([jnp.where(gt, 1.0, 0.0), jnp.where(eq, 1.0, 0.0)], axis=0).astype(BF16))
        cum_gt, cum_eq = cums[:n_exp], cums[n_exp:]
        sel = jnp.logical_or(gt, jnp.logical_and(eq, cum_eq <= need))
        cum = cum_gt + jnp.minimum(cum_eq, need)
        sa = jnp.where(sel, v, 0.0)
        for e in range(n_exp):
            cum_scr[e, :, :n] = cum[e:e + 1, :]
            sa_scr[e, :, :n] = sa[e:e + 1, :]

        gs = min(k, LANES)

        def per_expert(e, carry, lo=lo, n=n, k=k, slot0=slot0, gs=gs):
            for sg in range(k // gs):
                slot = (lax.broadcasted_iota(I32, (gs, 1), 0) + sg * gs).astype(F32)

                def blk(tb, acc, slot=slot):
                    cnt, gacc = acc
                    off = pl.multiple_of(tb * LANES, LANES)
                    crow = cum_scr[e, :, pl.ds(off, LANES)]
                    srow = sa_scr[e, :, pl.ds(off, LANES)]
                    cnt = cnt + jnp.where(crow <= slot, 1.0, 0.0)
                    gacc = gacc + jnp.where(crow == slot + 1.0, srow, 0.0)
                    return cnt, gacc

                cnt, gacc = lax.fori_loop(0, n // LANES, blk,
                                          (jnp.zeros((gs, LANES), F32), jnp.zeros((gs, LANES), F32)))
                tok = jnp.sum(cnt, axis=1, keepdims=True).astype(I32) + lo
                a0 = slot0 + sg * gs
                idx_ref[0, pl.ds(e, 1), a0:a0 + gs, :] = tok[None]
                gate_ref[0, pl.ds(e, 1), a0:a0 + gs, :] = jnp.sum(gacc, axis=1, keepdims=True)[None]
            return carry

        lax.fori_loop(0, n_exp, per_expert, 0)


def _route(aff, n_ctx):
    b, n_exp, tu = aff.shape
    n_lat = tu - n_ctx
    cap_l = CAPACITY_FACTOR * n_lat // n_exp
    cap_c = CAPACITY_FACTOR * n_ctx // n_exp
    slots = cap_l + cap_c
    streams = ((n_ctx, n_lat, cap_l, 0), (0, n_ctx, cap_c, cap_l))
    return pl.pallas_call(
        functools.partial(_route_kernel, streams),
        grid=(b,),
        in_specs=[pl.BlockSpec((1, n_exp, tu), lambda i: (i, 0, 0))],
        out_specs=[pl.BlockSpec((1, n_exp, slots, 1), lambda i: (i, 0, 0, 0))] * 2,
        out_shape=[jax.ShapeDtypeStruct((b, n_exp, slots, 1), I32),
                   jax.ShapeDtypeStruct((b, n_exp, slots, 1), F32)],
        scratch_shapes=[pltpu.VMEM((n_exp, 1, n_lat), F32), pltpu.VMEM((n_exp, 1, n_lat), F32)],
        compiler_params=_params("parallel"),
        name="route",
    )(aff)


def _slot_pitch(slots):
    return slots + SUBLANES


def _gather_kernel(slots, pitch, idx_ref, h_ref, o_ref):
    for mi in range(slots):
        r = pl.multiple_of(idx_ref[0, 0, mi] * SUBLANES, SUBLANES)
        o_ref[0, 0, pl.ds(mi, SUBLANES, stride=pitch), :] = h_ref[0, pl.ds(r, SUBLANES), :]


def _gather(hg, idx_rows, n_exp, slots):
    b = hg.shape[0]
    pitch = _slot_pitch(slots)
    return pl.pallas_call(
        functools.partial(_gather_kernel, slots, pitch),
        grid=(b, n_exp),
        in_specs=[
            pl.BlockSpec((1, 1, slots), lambda i, e: (i * n_exp + e, 0, 0), memory_space=pltpu.SMEM),
            pl.BlockSpec((1,) + hg.shape[1:], lambda i, e: (i, 0, 0)),
        ],
        out_specs=pl.BlockSpec((1, 1, SUBLANES * pitch, LANES), lambda i, e: (i, e, 0, 0)),
        out_shape=jax.ShapeDtypeStruct((b, n_exp, SUBLANES * pitch, LANES), F32),
        compiler_params=_params("parallel", "arbitrary"),
        name="gather",
    )(idx_rows, hg)


def _ffn_kernel(slots, pitch, xs_ref, wg_ref, wu_ref, wd_ref, gate_ref, y_ref, wg_bf, wu_bf, wd_bf):
    @pl.when(pl.program_id(1) == 0)
    def _():
        wg_bf[...] = wg_ref[0].astype(BF16)
        wu_bf[...] = wu_ref[0].astype(BF16)
        wd_bf[...] = wd_ref[0].astype(BF16)

    x = jnp.concatenate([xs_ref[0, 0, q * pitch:q * pitch + slots, :] for q in range(SUBLANES)], axis=1)
    x = x.astype(BF16)
    a = _dot(x, wg_bf[...])
    u = _dot(x, wu_bf[...])
    y = _dot((_silu(a) * u).astype(BF16), wd_bf[...]) * gate_ref[0, 0]
    for q in range(SUBLANES):
        y_ref[0, 0, q * pitch:q * pitch + slots, :] = y[:, q * LANES:(q + 1) * LANES]


def _ffn(xs, w_gate, w_up, w_down, layer, gate, slots):
    b, n_exp = xs.shape[:2]
    pitch = _slot_pitch(slots)
    d, f = w_gate.shape[2:]
    slot_tile = pl.BlockSpec((1, 1, SUBLANES * pitch, LANES), lambda e, i: (i, e, 0, 0))
    return pl.pallas_call(
        functools.partial(_ffn_kernel, slots, pitch),
        grid=(n_exp, b),
        in_specs=[
            slot_tile,
            pl.BlockSpec((None, 1, d, f), lambda e, i: (layer, e, 0, 0)),
            pl.BlockSpec((None, 1, d, f), lambda e, i: (layer, e, 0, 0)),
            pl.BlockSpec((None, 1, f, d), lambda e, i: (layer, e, 0, 0)),
            pl.BlockSpec((1, 1, slots, 1), lambda e, i: (i, e, 0, 0)),
        ],
        out_specs=slot_tile,
        out_shape=jax.ShapeDtypeStruct(xs.shape, F32),
        scratch_shapes=[pltpu.VMEM((d, f), BF16), pltpu.VMEM((d, f), BF16), pltpu.VMEM((f, d), BF16)],
        compiler_params=_params("arbitrary", "arbitrary"),
        name="ffn",
    )(xs, w_gate, w_up, w_down, gate)


SCATTER_BATCH = 8


def _scatter_kernel(slots, pitch, idx_ref, y_ref, o_ref):
    @pl.when(pl.program_id(1) == 0)
    def _():
        o_ref[...] = jnp.zeros_like(o_ref)

    for m0 in range(0, slots, SCATTER_BATCH):
        rows = [pl.multiple_of(idx_ref[0, 0, m0 + u] * SUBLANES, SUBLANES) for u in range(SCATTER_BATCH)]
        vals = [o_ref[0, pl.ds(rows[u], SUBLANES), :] + y_ref[0, 0, pl.ds(m0 + u, SUBLANES, stride=pitch), :]
                for u in range(SCATTER_BATCH)]
        for u in range(SCATTER_BATCH):
            o_ref[0, pl.ds(rows[u], SUBLANES), :] = vals[u]


def _scatter(y, idx_rows, tu, slots):
    b, n_exp = y.shape[:2]
    pitch = _slot_pitch(slots)
    return pl.pallas_call(
        functools.partial(_scatter_kernel, slots, pitch),
        grid=(b, n_exp),
        in_specs=[
            pl.BlockSpec((1, 1, slots), lambda i, e: (i * n_exp + e, 0, 0), memory_space=pltpu.SMEM),
            pl.BlockSpec((1, 1, SUBLANES * pitch, LANES), lambda i, e: (i, e, 0, 0)),
        ],
        out_specs=pl.BlockSpec((1, tu * SUBLANES, LANES), lambda i, e: (i, 0, 0)),
        out_shape=jax.ShapeDtypeStruct((b, tu * SUBLANES, LANES), F32),
        compiler_params=_params("parallel", "arbitrary"),
        name="scatter",
    )(idx_rows, y)


def _final_kernel(x_ref, f_ref, mod_ref, g_ref, o_ref):
    tm = x_ref.shape[1]
    x = x_ref[0] + mod_ref[0][M_G2:M_G2 + 1, :] * _from_gather_layout(f_ref, tm)
    o_ref[0] = _rms(x) * g_ref[...]


def _final(x, ffn, mrows, final_g, n_ctx):
    b, tu, d = x.shape
    tm = TOKEN_TILE
    skip = n_ctx // tm
    return pl.pallas_call(
        _final_kernel,
        grid=(b, (tu - n_ctx) // tm),
        in_specs=[
            pl.BlockSpec((1, tm, d), lambda i, j: (i, j + skip, 0)),
            pl.BlockSpec((1, tm * SUBLANES, LANES), lambda i, j: (i, j + skip, 0)),
            pl.BlockSpec((1, M_ROWS, d), lambda i, j: (i, 0, 0)),
            pl.BlockSpec((1, d), lambda i, j: (0, 0)),
        ],
        out_specs=pl.BlockSpec((1, tm, d), lambda i, j: (i, j, 0)),
        out_shape=jax.ShapeDtypeStruct((b, tu - n_ctx, d), F32),
        compiler_params=_params("parallel", "parallel"),
        name="final_norm",
    )(x, ffn, mrows, final_g.reshape(1, d))


def _rope_tables(n_lat, n_ctx):
    rows = n_lat // GRID_W
    rowp = jnp.repeat(jnp.arange(rows, dtype=F32), GRID_W)
    colp = jnp.tile(jnp.arange(GRID_W, dtype=F32), rows)
    axis_dim = HEAD_DIM // 2
    inv_freq = ROPE_BASE ** (-jnp.arange(0, axis_dim, 2, dtype=F32) / axis_dim)
    ar = rowp[:, None] * inv_freq
    ac = colp[:, None] * inv_freq
    cs = jnp.concatenate([jnp.cos(ar), jnp.cos(ar), jnp.cos(ac), jnp.cos(ac)], axis=1)
    sn = jnp.concatenate([-jnp.sin(ar), jnp.sin(ar), -jnp.sin(ac), jnp.sin(ac)], axis=1)
    reps = LANES // HEAD_DIM
    cs = jnp.concatenate([jnp.ones((n_ctx, LANES), F32), jnp.tile(cs, (1, reps))], axis=0)
    sn = jnp.concatenate([jnp.zeros((n_ctx, LANES), F32), jnp.tile(sn, (1, reps))], axis=0)
    return cs, sn


def kernel(x, c, ctx, c_ctx, w_mod, b_mod, norm1_g, norm2_g, w_in, conv_w, ret_decay_logit, attn_sink,
           w_out, w_router, w_gate, w_up, w_down, final_g):
    b, n_lat, d = x.shape
    n_ctx = ctx.shape[1]
    depth = w_in.shape[0]
    n_exp = w_router.shape[2]
    tu = n_ctx + n_lat
    assert w_in.shape[2] == IN_WIDTH and n_ctx % TOKEN_TILE == 0 and n_lat % TOKEN_TILE == 0
    assert b + 1 <= SUBLANES and n_exp == N_EXPERTS

    c_rows = jnp.concatenate([c, c_ctx[None], jnp.zeros((SUBLANES - b - 1, d), F32)], axis=0)
    mods = _mod_vectors(c_rows, w_mod, b_mod).reshape(depth, SUBLANES, 6, d)
    qdec, win, cd, dm = _decay_tables(ret_decay_logit)
    rope_cs, rope_sn = _rope_tables(n_lat, n_ctx)
    sink_rows = jnp.broadcast_to(jnp.repeat(attn_sink.astype(F32), CHUNK, axis=1)[:, :, None],
                                 (depth, ATTN_HEADS * CHUNK, LANES))
    w_in_bf = w_in.astype(BF16)
    w_out_bf = w_out.astype(BF16)
    w_router_t = jnp.swapaxes(w_router, 1, 2)

    xu = jnp.concatenate([ctx, x], axis=1)
    ffn = None
    mrows = None
    cap = CAPACITY_FACTOR * n_lat // n_exp + CAPACITY_FACTOR * n_ctx // n_exp
    for l in range(depth):
        prev_mrows = mrows
        mrows = jnp.concatenate([mods[l, :b], jnp.broadcast_to(mods[l, b][None], (b, 6, d)),
                                 jnp.zeros((b, M_ROWS - 12, d), F32)], axis=1)
        xu, proj = _inproj(xu, ffn, mrows if ffn is None else _with_prev_g2(mrows, prev_mrows),
                           norm1_g[l], w_in_bf, l, rope_cs, rope_sn, n_ctx)
        sp = _ret_states(proj[3], proj[4], win[l], cd[l], n_ctx)
        mix = _mixer(proj, sp, (dm[l], qdec[l]), conv_w[l], sink_rows[l], n_ctx)
        xu, hg, aff = _outproj(mix, xu, mrows, norm2_g[l], w_out_bf, l, w_router_t[l], n_ctx)
        idx, gate = _route(aff, n_ctx)
        idx_rows = idx.reshape(b * n_exp, 1, cap)
        xs = _gather(hg, idx_rows, n_exp, cap)
        y = _ffn(xs, w_gate, w_up, w_down, l, gate, cap)
        ffn = _scatter(y, idx_rows, tu, cap)
    return _final(xu, ffn, mrows, final_g, n_ctx)


def _with_prev_g2(mrows, prev_mrows):
    out = mrows.at[:, M_G2].set(prev_mrows[:, M_G2])
    return out.at[:, M_CTX + M_G2].set(prev_mrows[:, M_CTX + M_G2])
```

```python
import functools

import jax
import jax.numpy as jnp
from jax import lax
from jax.experimental import pallas as pl
from jax.experimental.pallas import tpu as pltpu

F32 = jnp.float32
BF16 = jnp.bfloat16
I32 = jnp.int32
HIGHEST = lax.Precision.HIGHEST

HEAD_DIM = 64
CONV_WIDTH = 256
RET_HEADS = 4
RET_WIDTH = RET_HEADS * HEAD_DIM
ATTN_HEADS = 8
ATTN_KV_HEADS = 2
ATTN_GROUP = ATTN_HEADS // ATTN_KV_HEADS
ATTN_WIDTH = ATTN_HEADS * HEAD_DIM
KV_WIDTH = ATTN_KV_HEADS * HEAD_DIM
CHUNK = 128
GRID_W = 64
N_EXPERTS = 16
CAPACITY_FACTOR = 2
ROPE_BASE = 10000.0
EPS = 1e-6
NEG_INF = -1e30
QK_SCALE = HEAD_DIM ** -0.5
LOG2E = 1.4426950408889634

LANES = 128
SUBLANES = 8
VMEM_LIMIT_BYTES = 56 * 1024 * 1024

O_CB = 0
O_CC = O_CB + CONV_WIDTH
O_CX = O_CC + CONV_WIDTH
O_RQ = O_CX + CONV_WIDTH
O_RK = O_RQ + RET_WIDTH
O_RV = O_RK + RET_WIDTH
O_GF = O_RV + RET_WIDTH
O_GB = O_GF + RET_WIDTH
O_AQ = O_GB + RET_WIDTH
O_AK = O_AQ + ATTN_WIDTH
O_AV = O_AK + KV_WIDTH
IN_WIDTH = O_AV + KV_WIDTH

M_SH1, M_SC1, M_G1, M_SH2, M_SC2, M_G2 = range(6)
M_CTX = 6
M_ROWS = 16

TOKEN_TILE = 256


def _params(*sem, vmem=VMEM_LIMIT_BYTES):
    return pltpu.CompilerParams(dimension_semantics=sem, vmem_limit_bytes=vmem)


def _dot(a, b):
    return jnp.dot(a, b, preferred_element_type=F32)


def _dot_nt(a, b):
    return lax.dot_general(a, b, (((1,), (1,)), ((), ())), preferred_element_type=F32)


def _silu(v):
    return v * jax.nn.sigmoid(v)


def _mod_kernel(c_ref, w_ref, b_ref, o_ref):
    s = _silu(c_ref[...])
    o_ref[0] = jnp.dot(s, w_ref[0], precision=HIGHEST, preferred_element_type=F32) + b_ref[0]


def _mod_vectors(c_rows, w_mod, b_mod):
    depth, d_model, width = w_mod.shape
    tn = 1536
    return pl.pallas_call(
        _mod_kernel,
        grid=(depth, width // tn),
        in_specs=[
            pl.BlockSpec((SUBLANES, d_model), lambda l, n: (0, 0)),
            pl.BlockSpec((1, d_model, tn), lambda l, n: (l, 0, n)),
            pl.BlockSpec((1, 1, tn), lambda l, n: (l, 0, n)),
        ],
        out_specs=pl.BlockSpec((1, SUBLANES, tn), lambda l, n: (l, 0, n)),
        out_shape=jax.ShapeDtypeStruct((depth, SUBLANES, width), F32),
        compiler_params=_params("parallel", "parallel"),
        name="mod_vectors",
    )(c_rows, w_mod, b_mod.reshape(depth, 1, width))


def _log_sigmoid(v):
    return -jnp.log(1.0 + jnp.exp(-v))


def _tables_kernel(lgl_ref, lgr_ref, qdec_ref, win_ref, cd_ref, dm_ref):
    pos = lax.broadcasted_iota(I32, (CHUNK, RET_WIDTH), 0).astype(F32)
    ri = (lax.broadcasted_iota(I32, (RET_HEADS * CHUNK, CHUNK), 0) & (CHUNK - 1)).astype(F32)
    rj = lax.broadcasted_iota(I32, (RET_HEADS * CHUNK, CHUNK), 1).astype(F32)
    for d in range(2):
        lg = _log_sigmoid(lgl_ref[0, d])
        lg1 = lg[0:1, :]
        if d == 0:
            qdec_ref[0, d] = jnp.exp(lg1 * (pos + 1.0))
            win_ref[0, d] = jnp.exp(lg1 * (CHUNK - 1.0 - pos))
            diff = ri - rj
        else:
            qdec_ref[0, d] = jnp.exp(lg1 * (CHUNK - pos))
            win_ref[0, d] = jnp.exp(lg1 * pos)
            diff = rj - ri
        cd_ref[0, d] = jnp.exp(lg * float(CHUNK))
        lr = _log_sigmoid(lgr_ref[0, d])
        dm_ref[0, d] = jnp.where(diff >= 0.0, jnp.exp(lr * jnp.maximum(diff, 0.0)), 0.0)


def _decay_tables(ret_decay_logit):
    depth = ret_decay_logit.shape[0]
    lg = ret_decay_logit.astype(F32)
    lgl = jnp.broadcast_to(jnp.repeat(lg, HEAD_DIM, axis=-1)[:, :, None, :], (depth, 2, SUBLANES, RET_WIDTH))
    lgr = jnp.broadcast_to(jnp.repeat(lg, CHUNK, axis=-1)[:, :, :, None], (depth, 2, RET_HEADS * CHUNK, CHUNK))
    return pl.pallas_call(
        _tables_kernel,
        grid=(depth,),
        in_specs=[
            pl.BlockSpec((1, 2, SUBLANES, RET_WIDTH), lambda l: (l, 0, 0, 0)),
            pl.BlockSpec((1, 2, RET_HEADS * CHUNK, CHUNK), lambda l: (l, 0, 0, 0)),
        ],
        out_specs=[
            pl.BlockSpec((1, 2, CHUNK, RET_WIDTH), lambda l: (l, 0, 0, 0)),
            pl.BlockSpec((1, 2, CHUNK, RET_WIDTH), lambda l: (l, 0, 0, 0)),
            pl.BlockSpec((1, 2, SUBLANES, RET_WIDTH), lambda l: (l, 0, 0, 0)),
            pl.BlockSpec((1, 2, RET_HEADS * CHUNK, CHUNK), lambda l: (l, 0, 0, 0)),
        ],
        out_shape=[
            jax.ShapeDtypeStruct((depth, 2, CHUNK, RET_WIDTH), F32),
            jax.ShapeDtypeStruct((depth, 2, CHUNK, RET_WIDTH), F32),
            jax.ShapeDtypeStruct((depth, 2, SUBLANES, RET_WIDTH), F32),
            jax.ShapeDtypeStruct((depth, 2, RET_HEADS * CHUNK, CHUNK), F32),
        ],
        compiler_params=_params("parallel"),
        name="decay_tables",
    )(lgl, lgr)


def _rope(v, cs, sn):
    lane = lax.broadcasted_iota(I32, (1, LANES), 1)
    first = (lane & 31) < 16
    outs = []
    for g in range(v.shape[1] // LANES):
        vg = v[:, g * LANES:(g + 1) * LANES]
        sw = jnp.where(first, pltpu.roll(vg, LANES - 16, 1), pltpu.roll(vg, 16, 1))
        outs.append(vg * cs + sw * sn)
    return outs[0] if len(outs) == 1 else jnp.concatenate(outs, axis=1)


def _swap_halves(v):
    return jnp.concatenate([v[:, HEAD_DIM:], v[:, :HEAD_DIM]], axis=1)


def _from_gather_layout(f_ref, tm):
    return jnp.concatenate(
        [f_ref[0, pl.ds(q, tm, stride=SUBLANES), :] for q in range(SUBLANES)], axis=1)


def _rms(v):
    return v * lax.rsqrt(jnp.mean(v * v, axis=-1, keepdims=True) + EPS)


def _inproj_kernel(has_ffn, n_ctx_tiles, *refs):
    if has_ffn:
        x_ref, f_ref, mod_ref, g_ref, w_ref, cs_ref, sn_ref = refs[:7]
        outs = refs[7:]
        xo_ref = outs[0]
        outs = outs[1:]
    else:
        x_ref, mod_ref, g_ref, w_ref, cs_ref, sn_ref = refs[:6]
        outs = refs[6:]
    cz_ref, rqkv_ref, gates_ref, aq_ref, kv_ref = outs
    tm = x_ref.shape[1]
    is_ctx = pl.program_id(1) < n_ctx_tiles
    m = mod_ref[0]

    def row(r):
        return jnp.where(is_ctx, m[M_CTX + r:M_CTX + r + 1, :], m[r:r + 1, :])

    x = x_ref[0]
    if has_ffn:
        x = x + row(M_G2) * _from_gather_layout(f_ref, tm)
        xo_ref[0] = x
    h = (_rms(x) * g_ref[...]) * (1.0 + row(M_SC1)) + row(M_SH1)
    h = h.astype(BF16)
    cs = cs_ref[...]
    sn = sn_ref[...]

    def proj(a, b):
        return _dot(h, w_ref[:, a:b])

    cz_ref[0, :, 0:CONV_WIDTH] = proj(O_CB, O_CC)
    cz_ref[0, :, CONV_WIDTH:] = proj(O_CC, O_CX) * proj(O_CX, O_RQ)
    rqkv_ref[0, :, 0:RET_WIDTH] = _rope(proj(O_RQ, O_RK), cs, sn).astype(BF16)
    rqkv_ref[0, :, RET_WIDTH:2 * RET_WIDTH] = _rope(proj(O_RK, O_RV) * QK_SCALE, cs, sn).astype(BF16)
    rqkv_ref[0, :, 2 * RET_WIDTH:] = proj(O_RV, O_GF).astype(BF16)
    gates_ref[0, :, 0:RET_WIDTH] = _silu(proj(O_GF, O_GB))
    gates_ref[0, :, RET_WIDTH:] = _silu(proj(O_GB, O_AQ))
    aq_ref[0] = (_rope(proj(O_AQ, O_AK), cs, sn) * (QK_SCALE * LOG2E)).astype(BF16)
    ak = _rope(proj(O_AK, O_AV), cs, sn)
    av = proj(O_AV, IN_WIDTH)
    kv_ref[0, :, 0:KV_WIDTH] = ak.astype(BF16)
    kv_ref[0, :, KV_WIDTH:2 * KV_WIDTH] = _swap_halves(ak).astype(BF16)
    kv_ref[0, :, 2 * KV_WIDTH:3 * KV_WIDTH] = av.astype(BF16)
    kv_ref[0, :, 3 * KV_WIDTH:] = _swap_halves(av).astype(BF16)


def _inproj(x, ffn, mrows, norm_g, w_in_bf, layer, rope_cs, rope_sn, n_ctx):
    b, tu, d = x.shape
    tm = TOKEN_TILE
    has_ffn = ffn is not None
    tok = lambda width: pl.BlockSpec((1, tm, width), lambda i, j: (i, j, 0))
    in_specs = [tok(d)]
    args = [x]
    if has_ffn:
        in_specs.append(pl.BlockSpec((1, tm * SUBLANES, LANES), lambda i, j: (i, j, 0)))
        args.append(ffn)
    in_specs += [
        pl.BlockSpec((1, M_ROWS, d), lambda i, j: (i, 0, 0)),
        pl.BlockSpec((1, d), lambda i, j: (0, 0)),
        pl.BlockSpec((None, d, IN_WIDTH), lambda i, j: (layer, 0, 0)),
        pl.BlockSpec((tm, LANES), lambda i, j: (j, 0)),
        pl.BlockSpec((tm, LANES), lambda i, j: (j, 0)),
    ]
    args += [mrows, norm_g.reshape(1, d), w_in_bf, rope_cs, rope_sn]
    widths = [(2 * CONV_WIDTH, F32), (3 * RET_WIDTH, BF16), (2 * RET_WIDTH, F32), (ATTN_WIDTH, BF16),
              (4 * KV_WIDTH, BF16)]
    out_specs = [tok(w) for w, _ in widths]
    out_shape = [jax.ShapeDtypeStruct((b, tu, w), dt) for w, dt in widths]
    if has_ffn:
        out_specs = [tok(d)] + out_specs
        out_shape = [jax.ShapeDtypeStruct((b, tu, d), F32)] + out_shape
    res = pl.pallas_call(
        functools.partial(_inproj_kernel, has_ffn, n_ctx // tm),
        grid=(b, tu // tm),
        in_specs=in_specs,
        out_specs=out_specs,
        out_shape=out_shape,
        compiler_params=_params("parallel", "parallel"),
        name="inproj",
    )(*args)
    if has_ffn:
        return res[0], res[1:]
    return x, res


def _head_block_mask(n):
    r = lax.broadcasted_iota(I32, (n, n), 0) // HEAD_DIM
    c = lax.broadcasted_iota(I32, (n, n), 1) // HEAD_DIM
    return r == c


def _states_kernel(n_ctx_chunks, n_chunks, rqkv_ref, win_ref, cd_ref, sp_ref, s_scr):
    same_head = _head_block_mask(RET_WIDTH)
    for d in range(2):
        s_scr[...] = jnp.zeros_like(s_scr)
        win = win_ref[d]
        cd = cd_ref[d][0:1, :]

        def body(i, carry, d=d, win=win, cd=cd):
            if d == 0:
                c = i
            else:
                c = jnp.where(i < n_ctx_chunks, n_ctx_chunks - 1 - i, n_chunks + n_ctx_chunks - 1 - i)
            off = pl.multiple_of(c * CHUNK, CHUNK)
            kw = rqkv_ref[0, pl.ds(off, CHUNK), RET_WIDTH:2 * RET_WIDTH].astype(F32) * win
            v = rqkv_ref[0, pl.ds(off, CHUNK), 2 * RET_WIDTH:]
            u = _dot(kw.T.astype(BF16), v)
            s = s_scr[...]
            sp_ref[0, d, pl.ds(c, 1)] = s.astype(BF16)[None]
            s_scr[...] = s * cd + jnp.where(same_head, u, 0.0)
            return carry

        lax.fori_loop(0, n_chunks, body, 0)


def _ret_states(rqkv, win, cd, n_ctx):
    b, tu, _ = rqkv.shape
    w = RET_WIDTH
    n_chunks = tu // CHUNK
    return pl.pallas_call(
        functools.partial(_states_kernel, n_ctx // CHUNK, n_chunks),
        grid=(b,),
        in_specs=[
            pl.BlockSpec((1, tu, 3 * w), lambda i: (i, 0, 0)),
            pl.BlockSpec((2, CHUNK, w), lambda i: (0, 0, 0)),
            pl.BlockSpec((2, SUBLANES, w), lambda i: (0, 0, 0)),
        ],
        out_specs=pl.BlockSpec((1, 2, n_chunks, w, w), lambda i: (i, 0, 0, 0, 0)),
        out_shape=jax.ShapeDtypeStruct((b, 2, n_chunks, w, w), BF16),
        scratch_shapes=[pltpu.VMEM((w, w), F32)],
        compiler_params=_params("parallel"),
        name="ret_states",
    )(rqkv, win, cd)


def _group_mean(t, ones_bd):
    hi = t.astype(BF16)
    lo = (t - hi.astype(F32)).astype(BF16)
    s = _dot(jnp.concatenate([hi, lo], axis=0), ones_bd)
    n = t.shape[0]
    return (s[:n] + s[n:]) * (1.0 / HEAD_DIM)


def _mixer_kernel(n_ctx_chunks, n_chunks,
                  cz_ref, czp_ref, czn_ref, cw_ref, rqkv_ref, gates_ref, sp_ref, dm_ref, qdec_ref,
                  aq_ref, kvp_ref, kvc_ref, kvn_ref, kvx_ref, sink_ref, mix_ref):
    c = pl.program_id(1)
    is_lat = c >= n_ctx_chunks

    z = cz_ref[0, :, CONV_WIDTH:]
    row = lax.broadcasted_iota(I32, (CHUNK, 1), 0)
    has_prev = jnp.logical_and(c != 0, c != n_ctx_chunks)
    has_next = jnp.logical_and(c != n_ctx_chunks - 1, c != n_chunks - 1)
    z_last = czp_ref[0, SUBLANES - 1:SUBLANES, CONV_WIDTH:]
    z_first = czn_ref[0, 0:1, CONV_WIDTH:]
    z_before = jnp.where(row == 0, jnp.where(has_prev, z_last, 0.0), pltpu.roll(z, 1, 0))
    z_after = jnp.where(row == CHUNK - 1, jnp.where(has_next, z_first, 0.0), pltpu.roll(z, CHUNK - 1, 0))
    cw = cw_ref[...]
    conv = cz_ref[0, :, :CONV_WIDTH] * (z_before * cw[0:1, :] + z * cw[1:2, :] + z_after * cw[2:3, :])

    q = rqkv_ref[0, :, 0:RET_WIDTH]
    k = rqkv_ref[0, :, RET_WIDTH:2 * RET_WIDTH]
    v = rqkv_ref[0, :, 2 * RET_WIDTH:]
    lane_head = lax.broadcasted_iota(I32, (1, RET_WIDTH), 1) // HEAD_DIM
    qz = jnp.zeros_like(q)
    qs = jnp.concatenate([jnp.where(lane_head == hh, q, qz) for hh in range(RET_HEADS)], axis=0)
    scores = _dot_nt(qs, k)
    qf = q.astype(F32)
    outs = []
    for d in range(2):
        p = (scores * dm_ref[d]).astype(BF16)
        o_all = _dot(p, v)
        o = _dot((qf * qdec_ref[d]).astype(BF16), sp_ref[0, d, 0])
        for hh in range(RET_HEADS):
            o = o + jnp.where(lane_head == hh, o_all[hh * CHUNK:(hh + 1) * CHUNK, :], 0.0)
        outs.append(o)
    o2 = jnp.concatenate(outs, axis=0)
    ones_bd = jnp.where(_head_block_mask(RET_WIDTH), 1.0, 0.0).astype(BF16)
    dl = o2 - _group_mean(o2, ones_bd)
    on = dl * lax.rsqrt(_group_mean(dl * dl, ones_bd) + EPS)
    ret = on[:CHUNK] * gates_ref[0, :, 0:RET_WIDTH] + on[CHUNK:] * gates_ref[0, :, RET_WIDTH:]

    qa = aq_ref[0]
    kvs = [kvp_ref[0], kvc_ref[0], kvn_ref[0], kvx_ref[0]]
    keys = tuple(jnp.concatenate([t[:, i * KV_WIDTH:(i + 1) * KV_WIDTH] for t in kvs], axis=0) for i in (0, 1))
    vals = tuple(jnp.concatenate([t[:, i * KV_WIDTH:(i + 1) * KV_WIDTH] for t in kvs], axis=0) for i in (2, 3))
    n_ctx = kvx_ref.shape[1]
    iq = lax.broadcasted_iota(I32, (2 * CHUNK, CHUNK), 0) & (CHUNK - 1)
    ik = lax.broadcasted_iota(I32, (2 * CHUNK, CHUNK), 1)
    ok_prev = jnp.logical_and(jnp.logical_and(is_lat, c - 1 >= n_ctx_chunks), ik >= iq)
    ok_cur = jnp.logical_and(is_lat, ik >= 0)
    ok_next = jnp.logical_and(jnp.logical_and(is_lat, c + 1 <= n_chunks - 1), ik <= iq)
    ok_ctx = jnp.ones((2 * CHUNK, n_ctx), jnp.bool_)
    valid = jnp.concatenate([ok_prev, ok_cur, ok_next, ok_ctx], axis=1)
    half = lax.broadcasted_iota(I32, (1, LANES), 1) // HEAD_DIM
    qa_z = jnp.zeros((CHUNK, LANES), BF16)
    cols = [None] * (ATTN_HEADS // 2)
    for hk in range(ATTN_KV_HEADS):
        outs = []
        for par in range(2):
            ja, jb = 2 * hk, 2 * hk + 1
            qst = jnp.concatenate([
                jnp.where(half == par, qa[:, ja * LANES:(ja + 1) * LANES], qa_z),
                jnp.where(half == par, qa[:, jb * LANES:(jb + 1) * LANES], qa_z)], axis=0)
            sel = 0 if par == hk else 1
            s = jnp.where(valid, _dot_nt(qst, keys[sel]), NEG_INF)
            ha, hb = ATTN_GROUP * hk + par, ATTN_GROUP * hk + par + 2
            snk = jnp.concatenate([sink_ref[ha * CHUNK:(ha + 1) * CHUNK, 0:1],
                                   sink_ref[hb * CHUNK:(hb + 1) * CHUNK, 0:1]], axis=0) * LOG2E
            mx = jnp.maximum(jnp.max(s, axis=1, keepdims=True), snk)
            e = jnp.exp2(s - mx)
            den = jnp.sum(e, axis=1, keepdims=True) + jnp.exp2(snk - mx)
            outs.append(_dot(e.astype(BF16), vals[sel]) * (1.0 / den))
        cols[2 * hk] = jnp.where(half == 0, outs[0][:CHUNK], outs[1][:CHUNK])
        cols[2 * hk + 1] = jnp.where(half == 0, outs[0][CHUNK:], outs[1][CHUNK:])

    mix_ref[0] = jnp.concatenate([conv, ret] + cols, axis=1).astype(BF16)


def _mixer(proj, sp, tabs, conv_w, sink_rows, n_ctx):
    cz, rqkv, gates, aq, kv = proj
    dm, qdec = tabs
    b, tu, _ = cz.shape
    n_chunks = tu // CHUNK
    last = n_chunks - 1
    per8 = CHUNK // SUBLANES
    cur = lambda w: pl.BlockSpec((1, CHUNK, w), lambda i, c: (i, c, 0))
    prev = lambda w: pl.BlockSpec((1, CHUNK, w), lambda i, c: (i, jnp.maximum(c - 1, 0), 0))
    nxt = lambda w: pl.BlockSpec((1, CHUNK, w), lambda i, c: (i, jnp.minimum(c + 1, last), 0))
    full = lambda shape: pl.BlockSpec(shape, lambda i, c: (0,) * len(shape))
    wcz, wkv = cz.shape[2], kv.shape[2]
    in_specs = [
        cur(wcz),
        pl.BlockSpec((1, SUBLANES, wcz), lambda i, c: (i, jnp.maximum(c * per8 - 1, 0), 0)),
        pl.BlockSpec((1, SUBLANES, wcz), lambda i, c: (i, jnp.minimum((c + 1) * per8, tu // SUBLANES - 1), 0)),
        full(conv_w.shape),
        cur(rqkv.shape[2]), cur(gates.shape[2]),
        pl.BlockSpec((1, 2, 1, RET_WIDTH, RET_WIDTH), lambda i, c: (i, 0, c, 0, 0)),
        full(dm.shape), full(qdec.shape),
        cur(ATTN_WIDTH), prev(wkv), cur(wkv), nxt(wkv),
        pl.BlockSpec((1, n_ctx, wkv), lambda i, c: (i, 0, 0)),
        full(sink_rows.shape),
    ]
    args = [cz, cz, cz, conv_w, rqkv, gates, sp, dm, qdec, aq, kv, kv, kv, kv, sink_rows]
    width = CONV_WIDTH + RET_WIDTH + ATTN_WIDTH
    return pl.pallas_call(
        functools.partial(_mixer_kernel, n_ctx // CHUNK, n_chunks),
        grid=(b, n_chunks),
        in_specs=in_specs,
        out_specs=pl.BlockSpec((1, CHUNK, width), lambda i, c: (i, c, 0)),
        out_shape=jax.ShapeDtypeStruct((b, tu, width), BF16),
        compiler_params=_params("parallel", "parallel"),
        name="mixer",
    )(*args)


def _outproj_kernel(n_ctx_tiles, mix_ref, x_ref, mod_ref, g_ref, w_ref, wrh_ref, wrl_ref, xo_ref, hg_ref, aff_ref):
    tm = x_ref.shape[1]
    is_ctx = pl.program_id(1) < n_ctx_tiles
    m = mod_ref[0]

    def row(r):
        return jnp.where(is_ctx, m[M_CTX + r:M_CTX + r + 1, :], m[r:r + 1, :])

    x = x_ref[0] + row(M_G1) * _dot(mix_ref[0], w_ref[...])
    xo_ref[0] = x
    h = (_rms(x) * g_ref[...]) * (1.0 + row(M_SC2)) + row(M_SH2)
    for q in range(SUBLANES):
        hg_ref[0, pl.ds(q, tm, stride=SUBLANES), :] = h[:, q * LANES:(q + 1) * LANES]
    h_hi = h.astype(BF16)
    h_lo = (h - h_hi.astype(F32)).astype(BF16)
    logits = _dot(h_hi, wrh_ref[...]) + (_dot(h_lo, wrh_ref[...]) + _dot(h_hi, wrl_ref[...]))
    n_exp = aff_ref.shape[1]
    lane = lax.broadcasted_iota(I32, (1, LANES), 1)
    logits = jnp.where(lane < n_exp, logits, NEG_INF)
    e = jnp.exp(logits - jnp.max(logits, axis=1, keepdims=True))
    aff = e / jnp.sum(e, axis=1, keepdims=True)
    aff_ref[0] = aff.T[:n_exp, :]


def _outproj(mix, x, mrows, norm_g, w_out_bf, layer, w_router, n_ctx):
    b, tu, d = x.shape
    tm = TOKEN_TILE
    n_exp = w_router.shape[1]
    wr = jnp.pad(w_router, ((0, 0), (0, LANES - n_exp)))
    wr_hi = wr.astype(BF16)
    wr_lo = (wr - wr_hi.astype(F32)).astype(BF16)
    return pl.pallas_call(
        functools.partial(_outproj_kernel, n_ctx // tm),
        grid=(b, tu // tm),
        in_specs=[
            pl.BlockSpec((1, tm, mix.shape[2]), lambda i, j: (i, j, 0)),
            pl.BlockSpec((1, tm, d), lambda i, j: (i, j, 0)),
            pl.BlockSpec((1, M_ROWS, d), lambda i, j: (i, 0, 0)),
            pl.BlockSpec((1, d), lambda i, j: (0, 0)),
            pl.BlockSpec((None,) + w_out_bf.shape[1:], lambda i, j: (layer, 0, 0)),
            pl.BlockSpec((d, LANES), lambda i, j: (0, 0)),
            pl.BlockSpec((d, LANES), lambda i, j: (0, 0)),
        ],
        out_specs=[
            pl.BlockSpec((1, tm, d), lambda i, j: (i, j, 0)),
            pl.BlockSpec((1, tm * SUBLANES, LANES), lambda i, j: (i, j, 0)),
            pl.BlockSpec((1, n_exp, tm), lambda i, j: (i, 0, j)),
        ],
        out_shape=[
            jax.ShapeDtypeStruct((b, tu, d), F32),
            jax.ShapeDtypeStruct((b, tu * SUBLANES, LANES), F32),
            jax.ShapeDtypeStruct((b, n_exp, tu), F32),
        ],
        compiler_params=_params("parallel", "parallel"),
        name="outproj",
    )(mix, x, mrows, norm_g.reshape(1, d), w_out_bf, wr_hi, wr_lo)


SLOT_GROUP = 256


def _route_kernel(streams, aff_ref, idx_ref, gate_ref, inc_scr, sel_scr, ce_scr, cnt_scr):
    a = aff_ref[0]
    n_exp = a.shape[0]
    lane = lax.broadcasted_iota(I32, (1, LANES), 1)
    lane_f = lane.astype(F32)
    tri_r = lax.broadcasted_iota(I32, (LANES, LANES), 0)
    tri_c = lax.broadcasted_iota(I32, (LANES, LANES), 1)
    upper = jnp.where(tri_r <= tri_c, 1.0, 0.0).astype(BF16)
    inc_scr[...] = jnp.zeros_like(inc_scr)
    sel_scr[...] = jnp.zeros_like(sel_scr)
    for lo, n, k, slot0 in streams:
        v = a[:, lo:lo + n]

        def search(i, t, v=v, k=k):
            cand = t | jnp.left_shift(jnp.int32(1), 30 - i)
            cnt = jnp.sum(jnp.where(v >= pltpu.bitcast(cand, F32), 1, 0), axis=1, keepdims=True)
            return jnp.where(cnt >= k, cand, t)

        thr = lax.fori_loop(0, 31, search, jnp.zeros((n_exp, 1), I32))
        above = pltpu.bitcast(thr + 1, F32)
        floor = pltpu.bitcast(thr, F32)
        n_gt = jnp.sum(jnp.where(v >= above, 1, 0), axis=1, keepdims=True)
        need = (k - n_gt).astype(F32)

        seen_eq = jnp.zeros((n_exp, 1), F32)
        counts = jnp.zeros((n_exp, LANES), F32)
        for j in range(n // LANES):
            vb = v[:, j * LANES:(j + 1) * LANES]
            gt = vb >= above
            eq = jnp.logical_and(vb >= floor, vb < above)
            both = jnp.concatenate([jnp.where(gt, 1.0, 0.0), jnp.where(eq, 1.0, 0.0)], axis=0).astype(BF16)
            pref = _dot(both, upper)
            rank_eq = pref[n_exp:] + seen_eq
            inc = pref[:n_exp] + jnp.minimum(rank_eq, need) - jnp.minimum(seen_eq, need)
            sel = jnp.logical_or(gt, jnp.logical_and(eq, rank_eq <= need))
            seen_eq = rank_eq[:, LANES - 1:LANES]
            counts = counts + jnp.where(lane == j, inc[:, LANES - 1:LANES], 0.0)
            inc_scr[pl.ds(j, n_exp, stride=LANES), :] = inc
            sel_scr[pl.ds(j, n_exp, stride=LANES), :] = jnp.where(sel, vb, 0.0)
        through = _dot(counts.astype(BF16), upper)
        for e in range(n_exp):
            ce_scr[e] = through[e:e + 1, :]
            cnt_scr[e] = counts[e:e + 1, :]

        gs = min(k, SLOT_GROUP)

        def per_expert(e, carry, lo=lo, k=k, slot0=slot0, gs=gs):
            through_e = ce_scr[e]
            rhs = jnp.where(tri_r == 0, 1.0, jnp.where(tri_r == 1, cnt_scr[e], 0.0)).astype(BF16)
            rows_e = pl.ds(pl.multiple_of(e * LANES, LANES), LANES)
            tab = jnp.concatenate([inc_scr[rows_e, :], sel_scr[rows_e, :]], axis=1)
            for sg in range(k // gs):
                slot = (lax.broadcasted_iota(I32, (gs, 1), 0) + sg * gs).astype(F32)
                before = jnp.where(through_e <= slot, 1.0, 0.0).astype(BF16)
                r = _dot_nt(before, rhs)
                blk = r[:, 0:1]
                base = r[:, 1:2]
                onehot = jnp.where(lane_f == blk, 1.0, 0.0)
                row = jnp.dot(onehot, tab, precision=HIGHEST, preferred_element_type=F32)
                inc, aff = row[:, :LANES], row[:, LANES:]
                local = slot - base
                pos = _dot_nt(jnp.where(inc <= local, 1.0, 0.0).astype(BF16), rhs)[:, 0:1]
                tok = (blk * LANES + pos).astype(I32) + lo
                gate = jnp.sum(jnp.where(inc == local + 1.0, aff, 0.0), axis=1, keepdims=True)
                a0 = slot0 + sg * gs
                idx_ref[0, pl.ds(e, 1), a0:a0 + gs, :] = tok[None]
                gate_ref[0, pl.ds(e, 1), a0:a0 + gs, :] = gate[None]
            return carry

        lax.fori_loop(0, n_exp, per_expert, 0)


def _route(aff, n_ctx):
    b, n_exp, tu = aff.shape
    n_lat = tu - n_ctx
    cap_l = CAPACITY_FACTOR * n_lat // n_exp
    cap_c = CAPACITY_FACTOR * n_ctx // n_exp
    slots = cap_l + cap_c
    assert n_lat // LANES <= LANES and n_ctx // LANES <= LANES
    streams = ((n_ctx, n_lat, cap_l, 0), (0, n_ctx, cap_c, cap_l))
    return pl.pallas_call(
        functools.partial(_route_kernel, streams),
        grid=(b,),
        in_specs=[pl.BlockSpec((1, n_exp, tu), lambda i: (i, 0, 0))],
        out_specs=[pl.BlockSpec((1, n_exp, slots, 1), lambda i: (i, 0, 0, 0))] * 2,
        out_shape=[jax.ShapeDtypeStruct((b, n_exp, slots, 1), I32),
                   jax.ShapeDtypeStruct((b, n_exp, slots, 1), F32)],
        scratch_shapes=[pltpu.VMEM((n_exp * LANES, LANES), F32), pltpu.VMEM((n_exp * LANES, LANES), F32),
                        pltpu.VMEM((n_exp, 1, LANES), F32), pltpu.VMEM((n_exp, 1, LANES), F32)],
        compiler_params=_params("parallel"),
        name="route",
    )(aff)


def _slot_pitch(slots):
    return slots + SUBLANES


def _zero_slot_padding(ref, lead, slots, pitch):
    for q in range(SUBLANES):
        ref[lead + (pl.ds(q * pitch + slots, pitch - slots), slice(None))] = jnp.zeros((pitch - slots, LANES), ref.dtype)


def _gather_kernel(slots, pitch, idx_ref, h_ref, o_ref):
    _zero_slot_padding(o_ref, (0, 0), slots, pitch)
    for mi in range(slots):
        r = pl.multiple_of(idx_ref[0, 0, mi] * SUBLANES, SUBLANES)
        o_ref[0, 0, pl.ds(mi, SUBLANES, stride=pitch), :] = h_ref[0, pl.ds(r, SUBLANES), :]


def _gather(hg, idx_rows, n_exp, slots):
    b = hg.shape[0]
    pitch = _slot_pitch(slots)
    return pl.pallas_call(
        functools.partial(_gather_kernel, slots, pitch),
        grid=(b, n_exp),
        in_specs=[
            pl.BlockSpec((1, 1, slots), lambda i, e: (i * n_exp + e, 0, 0), memory_space=pltpu.SMEM),
            pl.BlockSpec((1,) + hg.shape[1:], lambda i, e: (i, 0, 0)),
        ],
        out_specs=pl.BlockSpec((1, 1, SUBLANES * pitch, LANES), lambda i, e: (i, e, 0, 0)),
        out_shape=jax.ShapeDtypeStruct((b, n_exp, SUBLANES * pitch, LANES), F32),
        compiler_params=_params("parallel", "arbitrary"),
        name="gather",
    )(idx_rows, hg)


FFN_SAMPLES = 2
FFN_VMEM_LIMIT_BYTES = 60 * 1024 * 1024


def _ffn_kernel(slots, pitch, xs_ref, wg0_ref, wg1_ref, wu0_ref, wu1_ref, wd0_ref, wd1_ref, gate_ref, y_ref,
                wg_bf, wu_bf, wd_bf):
    @pl.when(pl.program_id(1) == 0)
    def _():
        half = wg_bf.shape[0] // 2
        for dst, lo_ref, hi_ref in ((wg_bf, wg0_ref, wg1_ref), (wu_bf, wu0_ref, wu1_ref), (wd_bf, wd0_ref, wd1_ref)):
            dst[:half, :] = lo_ref[0].astype(BF16)
            dst[half:, :] = hi_ref[0].astype(BF16)

    n = xs_ref.shape[0]
    x = jnp.concatenate(
        [jnp.concatenate([xs_ref[s, 0, q * pitch:q * pitch + slots, :] for q in range(SUBLANES)], axis=1)
         for s in range(n)], axis=0).astype(BF16)
    gate = jnp.concatenate([gate_ref[s, 0] for s in range(n)], axis=0)
    a = _dot(x, wg_bf[...])
    u = _dot(x, wu_bf[...])
    y = _dot((_silu(a) * u).astype(BF16), wd_bf[...]) * gate
    for s in range(n):
        _zero_slot_padding(y_ref, (s, 0), slots, pitch)
        for q in range(SUBLANES):
            y_ref[s, 0, q * pitch:q * pitch + slots, :] = y[s * slots:(s + 1) * slots, q * LANES:(q + 1) * LANES]


def _ffn(xs, w_gate, w_up, w_down, layer, gate, slots):
    b, n_exp = xs.shape[:2]
    pitch = _slot_pitch(slots)
    d, f = w_gate.shape[2:]
    ns = FFN_SAMPLES
    steps = b // ns
    assert steps == 2 and b % ns == 0
    slot_tile = pl.BlockSpec((ns, 1, SUBLANES * pitch, LANES), lambda e, i: (i, e, 0, 0))

    def half(rows, cols, q):
        return pl.BlockSpec((None, 1, rows // 2, cols),
                            lambda e, i: (layer, jnp.minimum(e + (i > q).astype(jnp.int32), n_exp - 1), q, 0))

    return pl.pallas_call(
        functools.partial(_ffn_kernel, slots, pitch),
        grid=(n_exp, steps),
        in_specs=[
            slot_tile,
            half(d, f, 0), half(d, f, 1), half(d, f, 0), half(d, f, 1), half(f, d, 0), half(f, d, 1),
            pl.BlockSpec((ns, 1, slots, 1), lambda e, i: (i, e, 0, 0)),
        ],
        out_specs=slot_tile,
        out_shape=jax.ShapeDtypeStruct(xs.shape, F32),
        scratch_shapes=[pltpu.VMEM((d, f), BF16), pltpu.VMEM((d, f), BF16), pltpu.VMEM((f, d), BF16)],
        compiler_params=_params("arbitrary", "arbitrary", vmem=FFN_VMEM_LIMIT_BYTES),
        name="ffn",
    )(xs, w_gate, w_gate, w_up, w_up, w_down, w_down, gate)


SCATTER_BATCH = 8


def _scatter_kernel(slots, pitch, idx_ref, y_ref, o_ref):
    @pl.when(pl.program_id(1) == 0)
    def _():
        o_ref[...] = jnp.zeros_like(o_ref)

    for m0 in range(0, slots, SCATTER_BATCH):
        rows = [pl.multiple_of(idx_ref[0, 0, m0 + u] * SUBLANES, SUBLANES) for u in range(SCATTER_BATCH)]
        vals = [o_ref[0, pl.ds(rows[u], SUBLANES), :] + y_ref[0, 0, pl.ds(m0 + u, SUBLANES, stride=pitch), :]
                for u in range(SCATTER_BATCH)]
        for u in range(SCATTER_BATCH):
            o_ref[0, pl.ds(rows[u], SUBLANES), :] = vals[u]


def _scatter(y, idx_rows, tu, slots):
    b, n_exp = y.shape[:2]
    pitch = _slot_pitch(slots)
    return pl.pallas_call(
        functools.partial(_scatter_kernel, slots, pitch),
        grid=(b, n_exp),
        in_specs=[
            pl.BlockSpec((1, 1, slots), lambda i, e: (i * n_exp + e, 0, 0), memory_space=pltpu.SMEM),
            pl.BlockSpec((1, 1, SUBLANES * pitch, LANES), lambda i, e: (i, e, 0, 0)),
        ],
        out_specs=pl.BlockSpec((1, tu * SUBLANES, LANES), lambda i, e: (i, 0, 0)),
        out_shape=jax.ShapeDtypeStruct((b, tu * SUBLANES, LANES), F32),
        compiler_params=_params("parallel", "arbitrary"),
        name="scatter",
    )(idx_rows, y)


def _final_kernel(x_ref, f_ref, mod_ref, g_ref, o_ref):
    tm = x_ref.shape[1]
    x = x_ref[0] + mod_ref[0][M_G2:M_G2 + 1, :] * _from_gather_layout(f_ref, tm)
    o_ref[0] = _rms(x) * g_ref[...]


def _final(x, ffn, mrows, final_g, n_ctx):
    b, tu, d = x.shape
    tm = TOKEN_TILE
    skip = n_ctx // tm
    return pl.pallas_call(
        _final_kernel,
        grid=(b, (tu - n_ctx) // tm),
        in_specs=[
            pl.BlockSpec((1, tm, d), lambda i, j: (i, j + skip, 0)),
            pl.BlockSpec((1, tm * SUBLANES, LANES), lambda i, j: (i, j + skip, 0)),
            pl.BlockSpec((1, M_ROWS, d), lambda i, j: (i, 0, 0)),
            pl.BlockSpec((1, d), lambda i, j: (0, 0)),
        ],
        out_specs=pl.BlockSpec((1, tm, d), lambda i, j: (i, j, 0)),
        out_shape=jax.ShapeDtypeStruct((b, tu - n_ctx, d), F32),
        compiler_params=_params("parallel", "parallel"),
        name="final_norm",
    )(x, ffn, mrows, final_g.reshape(1, d))


def _rope_tables(n_lat, n_ctx):
    rows = n_lat // GRID_W
    rowp = jnp.repeat(jnp.arange(rows, dtype=F32), GRID_W)
    colp = jnp.tile(jnp.arange(GRID_W, dtype=F32), rows)
    axis_dim = HEAD_DIM // 2
    inv_freq = ROPE_BASE ** (-jnp.arange(0, axis_dim, 2, dtype=F32) / axis_dim)
    ar = rowp[:, None] * inv_freq
    ac = colp[:, None] * inv_freq
    cs = jnp.concatenate([jnp.cos(ar), jnp.cos(ar), jnp.cos(ac), jnp.cos(ac)], axis=1)
    sn = jnp.concatenate([-jnp.sin(ar), jnp.sin(ar), -jnp.sin(ac), jnp.sin(ac)], axis=1)
    reps = LANES // HEAD_DIM
    cs = jnp.concatenate([jnp.ones((n_ctx, LANES), F32), jnp.tile(cs, (1, reps))], axis=0)
    sn = jnp.concatenate([jnp.zeros((n_ctx, LANES), F32), jnp.tile(sn, (1, reps))], axis=0)
    return cs, sn


def kernel(x, c, ctx, c_ctx, w_mod, b_mod, norm1_g, norm2_g, w_in, conv_w, ret_decay_logit, attn_sink,
           w_out, w_router, w_gate, w_up, w_down, final_g):
    b, n_lat, d = x.shape
    n_ctx = ctx.shape[1]
    depth = w_in.shape[0]
    n_exp = w_router.shape[2]
    tu = n_ctx + n_lat
    assert w_in.shape[2] == IN_WIDTH and n_ctx % TOKEN_TILE == 0 and n_lat % TOKEN_TILE == 0
    assert b + 1 <= SUBLANES and n_exp == N_EXPERTS

    c_rows = jnp.concatenate([c, c_ctx[None], jnp.zeros((SUBLANES - b - 1, d), F32)], axis=0)
    mods = _mod_vectors(c_rows, w_mod, b_mod).reshape(depth, SUBLANES, 6, d)
    qdec, win, cd, dm = _decay_tables(ret_decay_logit)
    rope_cs, rope_sn = _rope_tables(n_lat, n_ctx)
    sink_rows = jnp.broadcast_to(jnp.repeat(attn_sink.astype(F32), CHUNK, axis=1)[:, :, None],
                                 (depth, ATTN_HEADS * CHUNK, LANES))
    w_in_bf = w_in.astype(BF16)
    w_out_bf = w_out.astype(BF16)

    xu = jnp.concatenate([ctx, x], axis=1)
    ffn = None
    mrows = None
    cap = CAPACITY_FACTOR * n_lat // n_exp + CAPACITY_FACTOR * n_ctx // n_exp
    for l in range(depth):
        prev_mrows = mrows
        mrows = jnp.concatenate([mods[l, :b], jnp.broadcast_to(mods[l, b][None], (b, 6, d)),
                                 jnp.zeros((b, M_ROWS - 12, d), F32)], axis=1)
        xu, proj = _inproj(xu, ffn, mrows if ffn is None else _with_prev_g2(mrows, prev_mrows),
                           norm1_g[l], w_in_bf, l, rope_cs, rope_sn, n_ctx)
        sp = _ret_states(proj[1], win[l], cd[l], n_ctx)
        mix = _mixer(proj, sp, (dm[l], qdec[l]), conv_w[l], sink_rows[l], n_ctx)
        xu, hg, aff = _outproj(mix, xu, mrows, norm2_g[l], w_out_bf, l, w_router[l], n_ctx)
        idx, gate = _route(aff, n_ctx)
        idx_rows = idx.reshape(b * n_exp, 1, cap)
        xs = _gather(hg, idx_rows, n_exp, cap)
        y = _ffn(xs, w_gate, w_up, w_down, l, gate, cap)
        ffn = _scatter(y, idx_rows, tu, cap)
    return _final(xu, ffn, mrows, final_g, n_ctx)


def _with_prev_g2(mrows, prev_mrows):
    out = mrows.at[:, M_G2].set(prev_mrows[:, M_G2])
    return out.at[:, M_CTX + M_G2].set(prev_mrows[:, M_CTX + M_G2])
```

```python
import functools

import jax
import jax.numpy as jnp
from jax import lax
from jax.experimental import pallas as pl
from jax.experimental.pallas import tpu as pltpu

F32 = jnp.float32
BF16 = jnp.bfloat16
I32 = jnp.int32
HIGHEST = lax.Precision.HIGHEST

HEAD_DIM = 64
CONV_WIDTH = 256
RET_HEADS = 4
RET_WIDTH = RET_HEADS * HEAD_DIM
ATTN_HEADS = 8
ATTN_KV_HEADS = 2
ATTN_GROUP = ATTN_HEADS // ATTN_KV_HEADS
ATTN_WIDTH = ATTN_HEADS * HEAD_DIM
KV_WIDTH = ATTN_KV_HEADS * HEAD_DIM
CHUNK = 128
GRID_W = 64
N_EXPERTS = 16
CAPACITY_FACTOR = 2
ROPE_BASE = 10000.0
EPS = 1e-6
NEG_INF = -1e30
QK_SCALE = HEAD_DIM ** -0.5
LOG2E = 1.4426950408889634

LANES = 128
SUBLANES = 8
VMEM_LIMIT_BYTES = 56 * 1024 * 1024

O_CB = 0
O_CC = O_CB + CONV_WIDTH
O_CX = O_CC + CONV_WIDTH
O_RQ = O_CX + CONV_WIDTH
O_RK = O_RQ + RET_WIDTH
O_RV = O_RK + RET_WIDTH
O_GF = O_RV + RET_WIDTH
O_GB = O_GF + RET_WIDTH
O_AQ = O_GB + RET_WIDTH
O_AK = O_AQ + ATTN_WIDTH
O_AV = O_AK + KV_WIDTH
IN_WIDTH = O_AV + KV_WIDTH

M_SH1, M_SC1, M_G1, M_SH2, M_SC2, M_G2 = range(6)
M_CTX = 6
M_ROWS = 16

INPROJ_TILE = 544
OUTPROJ_TILE = 256
FINAL_TILE = 256


def _params(*sem, vmem=VMEM_LIMIT_BYTES):
    return pltpu.CompilerParams(dimension_semantics=sem, vmem_limit_bytes=vmem)


def _dot(a, b):
    return jnp.dot(a, b, preferred_element_type=F32)


def _dot_nt(a, b):
    return lax.dot_general(a, b, (((1,), (1,)), ((), ())), preferred_element_type=F32)


def _silu(v):
    return v * jax.nn.sigmoid(v)


def _mod_kernel(c_ref, w_ref, b_ref, o_ref):
    s = _silu(c_ref[...])
    o_ref[0] = jnp.dot(s, w_ref[0], precision=HIGHEST, preferred_element_type=F32) + b_ref[0]


def _mod_vectors(c_rows, w_mod, b_mod):
    depth, d_model, width = w_mod.shape
    tn = 1536
    return pl.pallas_call(
        _mod_kernel,
        grid=(depth, width // tn),
        in_specs=[
            pl.BlockSpec((SUBLANES, d_model), lambda l, n: (0, 0)),
            pl.BlockSpec((1, d_model, tn), lambda l, n: (l, 0, n)),
            pl.BlockSpec((1, 1, tn), lambda l, n: (l, 0, n)),
        ],
        out_specs=pl.BlockSpec((1, SUBLANES, tn), lambda l, n: (l, 0, n)),
        out_shape=jax.ShapeDtypeStruct((depth, SUBLANES, width), F32),
        compiler_params=_params("parallel", "parallel"),
        name="mod_vectors",
    )(c_rows, w_mod, b_mod.reshape(depth, 1, width))


def _log_sigmoid(v):
    return -jnp.log(1.0 + jnp.exp(-v))


def _tables_kernel(lgl_ref, lgr_ref, qdec_ref, win_ref, cd_ref, dm_ref):
    pos = lax.broadcasted_iota(I32, (CHUNK, RET_WIDTH), 0).astype(F32)
    ri = (lax.broadcasted_iota(I32, (RET_HEADS * CHUNK, CHUNK), 0) & (CHUNK - 1)).astype(F32)
    rj = lax.broadcasted_iota(I32, (RET_HEADS * CHUNK, CHUNK), 1).astype(F32)
    for d in range(2):
        lg = _log_sigmoid(lgl_ref[0, d])
        lg1 = lg[0:1, :]
        if d == 0:
            qdec_ref[0, d] = jnp.exp(lg1 * (pos + 1.0))
            win_ref[0, d] = jnp.exp(lg1 * (CHUNK - 1.0 - pos))
            diff = ri - rj
        else:
            qdec_ref[0, d] = jnp.exp(lg1 * (CHUNK - pos))
            win_ref[0, d] = jnp.exp(lg1 * pos)
            diff = rj - ri
        cd_ref[0, d] = jnp.exp(lg * float(CHUNK))
        lr = _log_sigmoid(lgr_ref[0, d])
        dm_ref[0, d] = jnp.where(diff >= 0.0, jnp.exp(lr * jnp.maximum(diff, 0.0)), 0.0)


def _decay_tables(ret_decay_logit):
    depth = ret_decay_logit.shape[0]
    lg = ret_decay_logit.astype(F32)
    lgl = jnp.broadcast_to(jnp.repeat(lg, HEAD_DIM, axis=-1)[:, :, None, :], (depth, 2, SUBLANES, RET_WIDTH))
    lgr = jnp.broadcast_to(jnp.repeat(lg, CHUNK, axis=-1)[:, :, :, None], (depth, 2, RET_HEADS * CHUNK, CHUNK))
    return pl.pallas_call(
        _tables_kernel,
        grid=(depth,),
        in_specs=[
            pl.BlockSpec((1, 2, SUBLANES, RET_WIDTH), lambda l: (l, 0, 0, 0)),
            pl.BlockSpec((1, 2, RET_HEADS * CHUNK, CHUNK), lambda l: (l, 0, 0, 0)),
        ],
        out_specs=[
            pl.BlockSpec((1, 2, CHUNK, RET_WIDTH), lambda l: (l, 0, 0, 0)),
            pl.BlockSpec((1, 2, CHUNK, RET_WIDTH), lambda l: (l, 0, 0, 0)),
            pl.BlockSpec((1, 2, SUBLANES, RET_WIDTH), lambda l: (l, 0, 0, 0)),
            pl.BlockSpec((1, 2, RET_HEADS * CHUNK, CHUNK), lambda l: (l, 0, 0, 0)),
        ],
        out_shape=[
            jax.ShapeDtypeStruct((depth, 2, CHUNK, RET_WIDTH), F32),
            jax.ShapeDtypeStruct((depth, 2, CHUNK, RET_WIDTH), F32),
            jax.ShapeDtypeStruct((depth, 2, SUBLANES, RET_WIDTH), F32),
            jax.ShapeDtypeStruct((depth, 2, RET_HEADS * CHUNK, CHUNK), F32),
        ],
        compiler_params=_params("parallel"),
        name="decay_tables",
    )(lgl, lgr)


def _rope(v, cs, sn):
    lane = lax.broadcasted_iota(I32, (1, LANES), 1)
    first = (lane & 31) < 16
    outs = []
    for g in range(v.shape[1] // LANES):
        vg = v[:, g * LANES:(g + 1) * LANES]
        sw = jnp.where(first, pltpu.roll(vg, LANES - 16, 1), pltpu.roll(vg, 16, 1))
        outs.append(vg * cs + sw * sn)
    return outs[0] if len(outs) == 1 else jnp.concatenate(outs, axis=1)


def _swap_halves(v):
    return jnp.concatenate([v[:, HEAD_DIM:], v[:, :HEAD_DIM]], axis=1)


def _from_gather_layout(f_ref, tm):
    return jnp.concatenate(
        [f_ref[0, pl.ds(q, tm, stride=SUBLANES), :] for q in range(SUBLANES)], axis=1)


def _rms(v):
    return v * lax.rsqrt(jnp.mean(v * v, axis=-1, keepdims=True) + EPS)


def _mod_rows(mod_ref, tm, n_ctx):
    m = mod_ref[0]
    if n_ctx % tm == 0:
        is_ctx = pl.program_id(1) < n_ctx // tm
    else:
        is_ctx = pl.program_id(1) * tm + lax.broadcasted_iota(I32, (tm, 1), 0) < n_ctx

    def row(r):
        return jnp.where(is_ctx, m[M_CTX + r:M_CTX + r + 1, :], m[r:r + 1, :])

    return row


def _inproj_kernel(has_ffn, n_ctx, *refs):
    if has_ffn:
        x_ref, f_ref, mod_ref, g_ref, w_ref, cs_ref, sn_ref = refs[:7]
        outs = refs[7:]
        xo_ref = outs[0]
        outs = outs[1:]
    else:
        x_ref, mod_ref, g_ref, w_ref, cs_ref, sn_ref = refs[:6]
        outs = refs[6:]
    cz_ref, rqkv_ref, gates_ref, aq_ref, kv_ref = outs
    tm = x_ref.shape[1]
    row = _mod_rows(mod_ref, tm, n_ctx)

    x = x_ref[0]
    if has_ffn:
        x = x + row(M_G2) * _from_gather_layout(f_ref, tm)
        xo_ref[0] = x
    h = (_rms(x) * g_ref[...]) * (1.0 + row(M_SC1)) + row(M_SH1)
    h = h.astype(BF16)
    cs = cs_ref[...]
    sn = sn_ref[...]

    def proj(a, b):
        return _dot(h, w_ref[:, a:b])

    cz_ref[0, :, 0:CONV_WIDTH] = proj(O_CB, O_CC)
    cz_ref[0, :, CONV_WIDTH:] = proj(O_CC, O_CX) * proj(O_CX, O_RQ)
    rqkv_ref[0, :, 0:RET_WIDTH] = _rope(proj(O_RQ, O_RK), cs, sn).astype(BF16)
    rqkv_ref[0, :, RET_WIDTH:2 * RET_WIDTH] = _rope(proj(O_RK, O_RV) * QK_SCALE, cs, sn).astype(BF16)
    rqkv_ref[0, :, 2 * RET_WIDTH:] = proj(O_RV, O_GF).astype(BF16)
    gates_ref[0, :, 0:RET_WIDTH] = _silu(proj(O_GF, O_GB))
    gates_ref[0, :, RET_WIDTH:] = _silu(proj(O_GB, O_AQ))
    aq_ref[0] = (_rope(proj(O_AQ, O_AK), cs, sn) * (QK_SCALE * LOG2E)).astype(BF16)
    ak = _rope(proj(O_AK, O_AV), cs, sn)
    av = proj(O_AV, IN_WIDTH)
    kv_ref[0, :, 0:KV_WIDTH] = ak.astype(BF16)
    kv_ref[0, :, KV_WIDTH:2 * KV_WIDTH] = _swap_halves(ak).astype(BF16)
    kv_ref[0, :, 2 * KV_WIDTH:3 * KV_WIDTH] = av.astype(BF16)
    kv_ref[0, :, 3 * KV_WIDTH:] = _swap_halves(av).astype(BF16)


def _inproj(x, ffn, mrows, norm_g, w_in_bf, layer, rope_cs, rope_sn, n_ctx):
    b, tu, d = x.shape
    tm = INPROJ_TILE
    has_ffn = ffn is not None
    tok = lambda width: pl.BlockSpec((1, tm, width), lambda i, j: (i, j, 0))
    in_specs = [tok(d)]
    args = [x]
    if has_ffn:
        in_specs.append(pl.BlockSpec((1, tm * SUBLANES, LANES), lambda i, j: (i, j, 0)))
        args.append(ffn)
    in_specs += [
        pl.BlockSpec((1, M_ROWS, d), lambda i, j: (i, 0, 0)),
        pl.BlockSpec((1, d), lambda i, j: (0, 0)),
        pl.BlockSpec((None, d, IN_WIDTH), lambda i, j: (layer, 0, 0)),
        pl.BlockSpec((tm, LANES), lambda i, j: (j, 0)),
        pl.BlockSpec((tm, LANES), lambda i, j: (j, 0)),
    ]
    args += [mrows, norm_g.reshape(1, d), w_in_bf, rope_cs, rope_sn]
    widths = [(2 * CONV_WIDTH, F32), (3 * RET_WIDTH, BF16), (2 * RET_WIDTH, F32), (ATTN_WIDTH, BF16),
              (4 * KV_WIDTH, BF16)]
    out_specs = [tok(w) for w, _ in widths]
    out_shape = [jax.ShapeDtypeStruct((b, tu, w), dt) for w, dt in widths]
    if has_ffn:
        out_specs = [tok(d)] + out_specs
        out_shape = [jax.ShapeDtypeStruct((b, tu, d), F32)] + out_shape
    res = pl.pallas_call(
        functools.partial(_inproj_kernel, has_ffn, n_ctx),
        grid=(b, tu // tm),
        in_specs=in_specs,
        out_specs=out_specs,
        out_shape=out_shape,
        compiler_params=_params("parallel", "parallel"),
        name="inproj",
    )(*args)
    if has_ffn:
        return res[0], res[1:]
    return x, res


def _head_block_mask(n):
    r = lax.broadcasted_iota(I32, (n, n), 0) // HEAD_DIM
    c = lax.broadcasted_iota(I32, (n, n), 1) // HEAD_DIM
    return r == c


def _states_kernel(n_ctx_chunks, n_chunks, rqkv_ref, win_ref, cd_ref, sp_ref, s_scr):
    same_head = _head_block_mask(RET_WIDTH)
    s_scr[...] = jnp.zeros_like(s_scr)

    def body(i, carry):
        for d in range(2):
            if d == 0:
                c = i
            else:
                c = jnp.where(i < n_ctx_chunks, n_ctx_chunks - 1 - i, n_chunks + n_ctx_chunks - 1 - i)
            off = pl.multiple_of(c * CHUNK, CHUNK)
            kw = rqkv_ref[0, pl.ds(off, CHUNK), RET_WIDTH:2 * RET_WIDTH].astype(F32) * win_ref[d]
            v = rqkv_ref[0, pl.ds(off, CHUNK), 2 * RET_WIDTH:]
            u = _dot(kw.T.astype(BF16), v)
            s = s_scr[d]
            sp_ref[0, d, pl.ds(c, 1)] = s.astype(BF16)[None]
            s_scr[d] = s * cd_ref[d][0:1, :] + jnp.where(same_head, u, 0.0)
        return carry

    lax.fori_loop(0, n_chunks, body, 0)


def _ret_states(rqkv, win, cd, n_ctx):
    b, tu, _ = rqkv.shape
    w = RET_WIDTH
    n_chunks = tu // CHUNK
    return pl.pallas_call(
        functools.partial(_states_kernel, n_ctx // CHUNK, n_chunks),
        grid=(b,),
        in_specs=[
            pl.BlockSpec((1, tu, 3 * w), lambda i: (i, 0, 0)),
            pl.BlockSpec((2, CHUNK, w), lambda i: (0, 0, 0)),
            pl.BlockSpec((2, SUBLANES, w), lambda i: (0, 0, 0)),
        ],
        out_specs=pl.BlockSpec((1, 2, n_chunks, w, w), lambda i: (i, 0, 0, 0, 0)),
        out_shape=jax.ShapeDtypeStruct((b, 2, n_chunks, w, w), BF16),
        scratch_shapes=[pltpu.VMEM((2, w, w), F32)],
        compiler_params=_params("parallel"),
        name="ret_states",
    )(rqkv, win, cd)


def _group_mean(t, ones_bd):
    hi = t.astype(BF16)
    lo = (t - hi.astype(F32)).astype(BF16)
    s = _dot(jnp.concatenate([hi, lo], axis=0), ones_bd)
    n = t.shape[0]
    return (s[:n] + s[n:]) * (1.0 / HEAD_DIM)


def _mix_chunk(c, n_ctx_chunks, n_chunks, cz, z_last, z_first, cw, rqkv, gates, sp, dm_ref, qdec_ref,
               qa, kvs, sink_ref):
    is_lat = c >= n_ctx_chunks

    z = cz[:, CONV_WIDTH:]
    row = lax.broadcasted_iota(I32, (CHUNK, 1), 0)
    has_prev = jnp.logical_and(c != 0, c != n_ctx_chunks)
    has_next = jnp.logical_and(c != n_ctx_chunks - 1, c != n_chunks - 1)
    z_before = jnp.where(row == 0, jnp.where(has_prev, z_last, 0.0), pltpu.roll(z, 1, 0))
    z_after = jnp.where(row == CHUNK - 1, jnp.where(has_next, z_first, 0.0), pltpu.roll(z, CHUNK - 1, 0))
    conv = cz[:, :CONV_WIDTH] * (z_before * cw[0:1, :] + z * cw[1:2, :] + z_after * cw[2:3, :])

    q = rqkv[:, 0:RET_WIDTH]
    k = rqkv[:, RET_WIDTH:2 * RET_WIDTH]
    v = rqkv[:, 2 * RET_WIDTH:]
    lane_head = lax.broadcasted_iota(I32, (1, RET_WIDTH), 1) // HEAD_DIM
    qz = jnp.zeros_like(q)
    qs = jnp.concatenate([jnp.where(lane_head == hh, q, qz) for hh in range(RET_HEADS)], axis=0)
    scores = _dot_nt(qs, k)
    qf = q.astype(F32)
    outs = []
    for d in range(2):
        p = (scores * dm_ref[d]).astype(BF16)
        o_all = _dot(p, v)
        o = _dot((qf * qdec_ref[d]).astype(BF16), sp[d])
        for hh in range(RET_HEADS):
            o = o + jnp.where(lane_head == hh, o_all[hh * CHUNK:(hh + 1) * CHUNK, :], 0.0)
        outs.append(o)
    o2 = jnp.concatenate(outs, axis=0)
    ones_bd = jnp.where(_head_block_mask(RET_WIDTH), 1.0, 0.0).astype(BF16)
    dl = o2 - _group_mean(o2, ones_bd)
    on = dl * lax.rsqrt(_group_mean(dl * dl, ones_bd) + EPS)
    ret = on[:CHUNK] * gates[:, 0:RET_WIDTH] + on[CHUNK:] * gates[:, RET_WIDTH:]

    keys = tuple(jnp.concatenate([t[:, i * KV_WIDTH:(i + 1) * KV_WIDTH] for t in kvs], axis=0) for i in (0, 1))
    vals = tuple(jnp.concatenate([t[:, i * KV_WIDTH:(i + 1) * KV_WIDTH] for t in kvs], axis=0) for i in (2, 3))
    n_ctx = kvs[3].shape[0]
    iq = lax.broadcasted_iota(I32, (2 * CHUNK, CHUNK), 0) & (CHUNK - 1)
    ik = lax.broadcasted_iota(I32, (2 * CHUNK, CHUNK), 1)
    ok_prev = jnp.logical_and(jnp.logical_and(is_lat, c - 1 >= n_ctx_chunks), ik >= iq)
    ok_cur = jnp.logical_and(is_lat, ik >= 0)
    ok_next = jnp.logical_and(jnp.logical_and(is_lat, c + 1 <= n_chunks - 1), ik <= iq)
    ok_ctx = jnp.ones((2 * CHUNK, n_ctx), jnp.bool_)
    valid = jnp.concatenate([ok_prev, ok_cur, ok_next, ok_ctx], axis=1)
    half = lax.broadcasted_iota(I32, (1, LANES), 1) // HEAD_DIM
    qa_z = jnp.zeros((CHUNK, LANES), BF16)
    cols = [None] * (ATTN_HEADS // 2)
    for hk in range(ATTN_KV_HEADS):
        outs = []
        for par in range(2):
            ja, jb = 2 * hk, 2 * hk + 1
            qst = jnp.concatenate([
                jnp.where(half == par, qa[:, ja * LANES:(ja + 1) * LANES], qa_z),
                jnp.where(half == par, qa[:, jb * LANES:(jb + 1) * LANES], qa_z)], axis=0)
            sel = 0 if par == hk else 1
            s = jnp.where(valid, _dot_nt(qst, keys[sel]), NEG_INF)
            ha, hb = ATTN_GROUP * hk + par, ATTN_GROUP * hk + par + 2
            snk = jnp.concatenate([sink_ref[ha * CHUNK:(ha + 1) * CHUNK, 0:1],
                                   sink_ref[hb * CHUNK:(hb + 1) * CHUNK, 0:1]], axis=0) * LOG2E
            mx = jnp.maximum(jnp.max(s, axis=1, keepdims=True), snk)
            e = jnp.exp2(s - mx)
            den = jnp.sum(e, axis=1, keepdims=True) + jnp.exp2(snk - mx)
            outs.append(_dot(e.astype(BF16), vals[sel]) * (1.0 / den))
        cols[2 * hk] = jnp.where(half == 0, outs[0][:CHUNK], outs[1][:CHUNK])
        cols[2 * hk + 1] = jnp.where(half == 0, outs[0][CHUNK:], outs[1][CHUNK:])

    return jnp.concatenate([conv, ret] + cols, axis=1)


MIX_CHUNKS = 2


def _mixer_kernel(n_ctx_chunks, n_chunks,
                  cz_ref, czp_ref, czn_ref, cw_ref, rqkv_ref, gates_ref, sp_ref, dm_ref, qdec_ref,
                  aq_ref, kvp_ref, kvc_ref, kvn_ref, kvx_ref, sink_ref, mix_ref):
    j = pl.program_id(1)
    cw = cw_ref[...]
    kvx = kvx_ref[0]
    for s in range(MIX_CHUNKS):
        lo, hi = s * CHUNK, (s + 1) * CHUNK
        if s == 0:
            z_last = czp_ref[0, SUBLANES - 1:SUBLANES, CONV_WIDTH:]
            kv_prev = kvp_ref[0]
        else:
            z_last = cz_ref[0, lo - 1:lo, CONV_WIDTH:]
            kv_prev = kvc_ref[0, lo - CHUNK:lo, :]
        if s == MIX_CHUNKS - 1:
            z_first = czn_ref[0, 0:1, CONV_WIDTH:]
            kv_next = kvn_ref[0]
        else:
            z_first = cz_ref[0, hi:hi + 1, CONV_WIDTH:]
            kv_next = kvc_ref[0, hi:hi + CHUNK, :]
        mix = _mix_chunk(j * MIX_CHUNKS + s, n_ctx_chunks, n_chunks, cz_ref[0, lo:hi, :], z_last, z_first, cw,
                         rqkv_ref[0, lo:hi, :], gates_ref[0, lo:hi, :], (sp_ref[0, 0, s], sp_ref[0, 1, s]),
                         dm_ref, qdec_ref, aq_ref[0, lo:hi, :], [kv_prev, kvc_ref[0, lo:hi, :], kv_next, kvx],
                         sink_ref)
        mix_ref[0, lo:hi, :] = mix.astype(BF16)


def _mixer(proj, sp, tabs, conv_w, sink_rows, n_ctx):
    cz, rqkv, gates, aq, kv = proj
    dm, qdec = tabs
    b, tu, _ = cz.shape
    n_chunks = tu // CHUNK
    mc = MIX_CHUNKS
    rows = mc * CHUNK
    assert n_chunks % mc == 0 and (n_ctx // CHUNK) % mc == 0
    last = n_chunks - 1
    per8 = rows // SUBLANES
    cur = lambda w: pl.BlockSpec((1, rows, w), lambda i, j: (i, j, 0))
    full = lambda shape: pl.BlockSpec(shape, lambda i, j: (0,) * len(shape))
    wcz, wkv = cz.shape[2], kv.shape[2]
    in_specs = [
        cur(wcz),
        pl.BlockSpec((1, SUBLANES, wcz), lambda i, j: (i, jnp.maximum(j * per8 - 1, 0), 0)),
        pl.BlockSpec((1, SUBLANES, wcz), lambda i, j: (i, jnp.minimum((j + 1) * per8, tu // SUBLANES - 1), 0)),
        full(conv_w.shape),
        cur(rqkv.shape[2]), cur(gates.shape[2]),
        pl.BlockSpec((1, 2, mc, RET_WIDTH, RET_WIDTH), lambda i, j: (i, 0, j, 0, 0)),
        full(dm.shape), full(qdec.shape),
        cur(ATTN_WIDTH),
        pl.BlockSpec((1, CHUNK, wkv), lambda i, j: (i, jnp.maximum(j * mc - 1, 0), 0)),
        cur(wkv),
        pl.BlockSpec((1, CHUNK, wkv), lambda i, j: (i, jnp.minimum((j + 1) * mc, last), 0)),
        pl.BlockSpec((1, n_ctx, wkv), lambda i, j: (i, 0, 0)),
        full(sink_rows.shape),
    ]
    args = [cz, cz, cz, conv_w, rqkv, gates, sp, dm, qdec, aq, kv, kv, kv, kv, sink_rows]
    width = CONV_WIDTH + RET_WIDTH + ATTN_WIDTH
    return pl.pallas_call(
        functools.partial(_mixer_kernel, n_ctx // CHUNK, n_chunks),
        grid=(b, n_chunks // mc),
        in_specs=in_specs,
        out_specs=pl.BlockSpec((1, rows, width), lambda i, j: (i, j, 0)),
        out_shape=jax.ShapeDtypeStruct((b, tu, width), BF16),
        compiler_params=_params("parallel", "parallel"),
        name="mixer",
    )(*args)


def _outproj_kernel(n_ctx, n_exp, mix_ref, x_ref, mod_ref, g_ref, w_ref, wrh_ref, wrl_ref, xo_ref, hg_ref, aff_ref):
    tm = x_ref.shape[1]
    row = _mod_rows(mod_ref, tm, n_ctx)

    x = x_ref[0] + row(M_G1) * _dot(mix_ref[0], w_ref[...])
    xo_ref[0] = x
    h = (_rms(x) * g_ref[...]) * (1.0 + row(M_SC2)) + row(M_SH2)
    for q in range(SUBLANES):
        hg_ref[0, pl.ds(q, tm, stride=SUBLANES), :] = h[:, q * LANES:(q + 1) * LANES]
    h_hi = h.astype(BF16)
    h_lo = (h - h_hi.astype(F32)).astype(BF16)
    logits = _dot(h_hi, wrh_ref[...]) + (_dot(h_lo, wrh_ref[...]) + _dot(h_hi, wrl_ref[...]))
    lane = lax.broadcasted_iota(I32, (1, LANES), 1)
    logits = jnp.where(lane < n_exp, logits, NEG_INF)
    e = jnp.exp(logits - jnp.max(logits, axis=1, keepdims=True))
    aff_ref[0] = e / jnp.sum(e, axis=1, keepdims=True)


def _outproj(mix, x, mrows, norm_g, w_out_bf, layer, w_router, n_ctx):
    b, tu, d = x.shape
    tm = OUTPROJ_TILE
    n_exp = w_router.shape[1]
    wr = jnp.pad(w_router, ((0, 0), (0, LANES - n_exp)))
    wr_hi = wr.astype(BF16)
    wr_lo = (wr - wr_hi.astype(F32)).astype(BF16)
    return pl.pallas_call(
        functools.partial(_outproj_kernel, n_ctx, n_exp),
        grid=(b, tu // tm),
        in_specs=[
            pl.BlockSpec((1, tm, mix.shape[2]), lambda i, j: (i, j, 0)),
            pl.BlockSpec((1, tm, d), lambda i, j: (i, j, 0)),
            pl.BlockSpec((1, M_ROWS, d), lambda i, j: (i, 0, 0)),
            pl.BlockSpec((1, d), lambda i, j: (0, 0)),
            pl.BlockSpec((None,) + w_out_bf.shape[1:], lambda i, j: (layer, 0, 0)),
            pl.BlockSpec((d, LANES), lambda i, j: (0, 0)),
            pl.BlockSpec((d, LANES), lambda i, j: (0, 0)),
        ],
        out_specs=[
            pl.BlockSpec((1, tm, d), lambda i, j: (i, j, 0)),
            pl.BlockSpec((1, tm * SUBLANES, LANES), lambda i, j: (i, j, 0)),
            pl.BlockSpec((1, tm, LANES), lambda i, j: (i, j, 0)),
        ],
        out_shape=[
            jax.ShapeDtypeStruct((b, tu, d), F32),
            jax.ShapeDtypeStruct((b, tu * SUBLANES, LANES), F32),
            jax.ShapeDtypeStruct((b, tu, LANES), F32),
        ],
        compiler_params=_params("parallel", "parallel"),
        name="outproj",
    )(mix, x, mrows, norm_g.reshape(1, d), w_out_bf, wr_hi, wr_lo)


ROUTE_TABLES = 4


def _route_stream_tables(v, k, tab_scr, ce_scr, cnt_scr, stream, thr):
    n_exp, n = v.shape
    lane = lax.broadcasted_iota(I32, (1, LANES), 1)
    tri_r = lax.broadcasted_iota(I32, (LANES, LANES), 0)
    tri_c = lax.broadcasted_iota(I32, (LANES, LANES), 1)
    upper = jnp.where(tri_r <= tri_c, 1.0, 0.0).astype(BF16)
    above = pltpu.bitcast(thr + 1, F32)
    floor = pltpu.bitcast(thr, F32)
    n_gt = jnp.sum(jnp.where(v >= above, 1, 0), axis=1, keepdims=True)
    need = (k - n_gt).astype(F32)
    seen_eq = jnp.zeros((n_exp, 1), F32)
    counts = jnp.zeros((n_exp, LANES), F32)
    for j in range(n // LANES):
        vb = v[:, j * LANES:(j + 1) * LANES]
        gt = vb >= above
        eq = jnp.logical_and(vb >= floor, vb < above)
        both = jnp.concatenate([jnp.where(gt, 1.0, 0.0), jnp.where(eq, 1.0, 0.0)], axis=0).astype(BF16)
        pref = _dot(both, upper)
        rank_eq = pref[n_exp:] + seen_eq
        inc = pref[:n_exp] + jnp.minimum(rank_eq, need) - jnp.minimum(seen_eq, need)
        sel = jnp.logical_or(gt, jnp.logical_and(eq, rank_eq <= need))
        seen_eq = rank_eq[:, LANES - 1:LANES]
        counts = counts + jnp.where(lane == j, inc[:, LANES - 1:LANES], 0.0)
        a0 = jnp.where(sel, vb, 0.0)
        t0 = a0.astype(BF16).astype(F32)
        t1 = (a0 - t0).astype(BF16).astype(F32)
        t2 = a0 - t0 - t1
        for t, val in enumerate((inc, t0, t1, t2)):
            tab_scr[stream * ROUTE_TABLES + t, pl.ds(j, n_exp, stride=LANES), :] = val
    through = _dot(counts.astype(BF16), upper)
    for e in range(n_exp):
        ce_scr[stream, e] = through[e:e + 1, :]
        cnt_scr[stream, e] = counts[e:e + 1, :]


def _route_slots(e, stream, k, lo, tab_scr, ce_scr, cnt_scr):
    lane_f = lax.broadcasted_iota(I32, (1, LANES), 1).astype(F32)
    slot = lax.broadcasted_iota(I32, (k, LANES), 0).astype(F32)
    ones_rows = jnp.ones((LANES, LANES), BF16)
    cnt_rows = jnp.broadcast_to(cnt_scr[stream, e], (LANES, LANES)).astype(BF16)
    rows_e = pl.ds(pl.multiple_of(e * LANES, LANES), LANES)
    tab = jnp.concatenate([tab_scr[stream * ROUTE_TABLES + t, rows_e, :] for t in range(ROUTE_TABLES)],
                          axis=1).astype(BF16)
    before = jnp.where(ce_scr[stream, e] <= slot, 1.0, 0.0).astype(BF16)
    blk = _dot_nt(before, ones_rows)
    base = _dot_nt(before, cnt_rows)
    row = _dot(jnp.where(lane_f == blk, 1.0, 0.0).astype(BF16), tab)
    inc = row[:, :LANES]
    aff = row[:, LANES:2 * LANES] + row[:, 2 * LANES:3 * LANES] + row[:, 3 * LANES:]
    local = slot - base
    pos = _dot_nt(jnp.where(inc <= local, 1.0, 0.0).astype(BF16), ones_rows)
    tok = (blk * LANES + pos).astype(I32) + lo
    gate = jnp.sum(jnp.where(inc == local + 1.0, aff, 0.0), axis=1, keepdims=True)
    return tok[:, 0:1], gate


def _route_kernel(n_exp, streams, aff_ref, idx_ref, gate_ref, tab_scr, ce_scr, cnt_scr):
    a = aff_ref[0].T[:n_exp, :]
    tab_scr[...] = jnp.zeros_like(tab_scr)
    vs = [a[:, lo:lo + n] for lo, n, _, _ in streams]

    def search(i, ts):
        out = []
        for t, v, (_, _, k, _) in zip(ts, vs, streams):
            cand = t | jnp.left_shift(jnp.int32(1), 30 - i)
            cnt = jnp.sum(jnp.where(v >= pltpu.bitcast(cand, F32), 1, 0), axis=1, keepdims=True)
            out.append(jnp.where(cnt >= k, cand, t))
        return tuple(out)

    thrs = lax.fori_loop(0, 31, search, tuple(jnp.zeros((n_exp, 1), I32) for _ in streams))
    for st, (v, (_, _, k, _)) in enumerate(zip(vs, streams)):
        _route_stream_tables(v, k, tab_scr, ce_scr, cnt_scr, st, thrs[st])

    def per_expert(e, carry):
        for st, (lo, _, k, slot0) in enumerate(streams):
            tok, gate = _route_slots(e, st, k, lo, tab_scr, ce_scr, cnt_scr)
            idx_ref[0, pl.ds(e, 1), slot0:slot0 + k, :] = tok[None]
            gate_ref[0, pl.ds(e, 1), slot0:slot0 + k, :] = gate[None]
        return carry

    lax.fori_loop(0, n_exp, per_expert, 0)


def _route(aff, n_exp, n_ctx):
    b, tu, _ = aff.shape
    n_lat = tu - n_ctx
    cap_l = CAPACITY_FACTOR * n_lat // n_exp
    cap_c = CAPACITY_FACTOR * n_ctx // n_exp
    slots = cap_l + cap_c
    assert n_lat // LANES <= LANES and n_ctx // LANES <= LANES
    streams = ((n_ctx, n_lat, cap_l, 0), (0, n_ctx, cap_c, cap_l))
    return pl.pallas_call(
        functools.partial(_route_kernel, n_exp, streams),
        grid=(b,),
        in_specs=[pl.BlockSpec((1, tu, LANES), lambda i: (i, 0, 0))],
        out_specs=[pl.BlockSpec((1, n_exp, slots, 1), lambda i: (i, 0, 0, 0))] * 2,
        out_shape=[jax.ShapeDtypeStruct((b, n_exp, slots, 1), I32),
                   jax.ShapeDtypeStruct((b, n_exp, slots, 1), F32)],
        scratch_shapes=[pltpu.VMEM((len(streams) * ROUTE_TABLES, n_exp * LANES, LANES), F32),
                        pltpu.VMEM((len(streams), n_exp, 1, LANES), F32),
                        pltpu.VMEM((len(streams), n_exp, 1, LANES), F32)],
        compiler_params=_params("parallel"),
        name="route",
    )(aff)


def _slot_pitch(slots):
    return slots + SUBLANES


def _zero_slot_padding(ref, lead, slots, pitch):
    for q in range(SUBLANES):
        ref[lead + (pl.ds(q * pitch + slots, pitch - slots), slice(None))] = jnp.zeros((pitch - slots, LANES), ref.dtype)


def _gather_kernel(slots, pitch, idx_ref, h_ref, o_ref):
    _zero_slot_padding(o_ref, (0, 0), slots, pitch)
    for mi in range(slots):
        r = pl.multiple_of(idx_ref[0, 0, mi] * SUBLANES, SUBLANES)
        o_ref[0, 0, pl.ds(mi, SUBLANES, stride=pitch), :] = h_ref[0, pl.ds(r, SUBLANES), :]


def _gather(hg, idx_rows, n_exp, slots):
    b = hg.shape[0]
    pitch = _slot_pitch(slots)
    return pl.pallas_call(
        functools.partial(_gather_kernel, slots, pitch),
        grid=(b, n_exp),
        in_specs=[
            pl.BlockSpec((1, 1, slots), lambda i, e: (i * n_exp + e, 0, 0), memory_space=pltpu.SMEM),
            pl.BlockSpec((1,) + hg.shape[1:], lambda i, e: (i, 0, 0)),
        ],
        out_specs=pl.BlockSpec((1, 1, SUBLANES * pitch, LANES), lambda i, e: (i, e, 0, 0)),
        out_shape=jax.ShapeDtypeStruct((b, n_exp, SUBLANES * pitch, LANES), F32),
        compiler_params=_params("parallel", "arbitrary"),
        name="gather",
    )(idx_rows, hg)


FFN_SAMPLES = 2
FFN_VMEM_LIMIT_BYTES = 60 * 1024 * 1024


def _ffn_kernel(slots, pitch, xs_ref, wg0_ref, wg1_ref, wu0_ref, wu1_ref, wd0_ref, wd1_ref, gate_ref, y_ref,
                wg_bf, wu_bf, wd_bf):
    @pl.when(pl.program_id(1) == 0)
    def _():
        half = wg_bf.shape[0] // 2
        for dst, lo_ref, hi_ref in ((wg_bf, wg0_ref, wg1_ref), (wu_bf, wu0_ref, wu1_ref), (wd_bf, wd0_ref, wd1_ref)):
            dst[:half, :] = lo_ref[0].astype(BF16)
            dst[half:, :] = hi_ref[0].astype(BF16)

    n = xs_ref.shape[0]
    x = jnp.concatenate(
        [jnp.concatenate([xs_ref[s, 0, q * pitch:q * pitch + slots, :] for q in range(SUBLANES)], axis=1)
         for s in range(n)], axis=0).astype(BF16)
    gate = jnp.concatenate([gate_ref[s, 0] for s in range(n)], axis=0)
    a = _dot(x, wg_bf[...])
    u = _dot(x, wu_bf[...])
    y = _dot((_silu(a) * u).astype(BF16), wd_bf[...]) * gate
    for s in range(n):
        _zero_slot_padding(y_ref, (s, 0), slots, pitch)
        for q in range(SUBLANES):
            y_ref[s, 0, q * pitch:q * pitch + slots, :] = y[s * slots:(s + 1) * slots, q * LANES:(q + 1) * LANES]


def _ffn(xs, w_gate, w_up, w_down, layer, gate, slots):
    b, n_exp = xs.shape[:2]
    pitch = _slot_pitch(slots)
    d, f = w_gate.shape[2:]
    ns = FFN_SAMPLES
    steps = b // ns
    assert steps == 2 and b % ns == 0
    slot_tile = pl.BlockSpec((ns, 1, SUBLANES * pitch, LANES), lambda e, i: (i, e, 0, 0))

    def half(rows, cols, q):
        return pl.BlockSpec((None, 1, rows // 2, cols),
                            lambda e, i: (layer, jnp.minimum(e + (i > q).astype(jnp.int32), n_exp - 1), q, 0))

    return pl.pallas_call(
        functools.partial(_ffn_kernel, slots, pitch),
        grid=(n_exp, steps),
        in_specs=[
            slot_tile,
            half(d, f, 0), half(d, f, 1), half(d, f, 0), half(d, f, 1), half(f, d, 0), half(f, d, 1),
            pl.BlockSpec((ns, 1, slots, 1), lambda e, i: (i, e, 0, 0)),
        ],
        out_specs=slot_tile,
        out_shape=jax.ShapeDtypeStruct(xs.shape, F32),
        scratch_shapes=[pltpu.VMEM((d, f), BF16), pltpu.VMEM((d, f), BF16), pltpu.VMEM((f, d), BF16)],
        compiler_params=_params("arbitrary", "arbitrary", vmem=FFN_VMEM_LIMIT_BYTES),
        name="ffn",
    )(xs, w_gate, w_gate, w_up, w_up, w_down, w_down, gate)


SCATTER_BATCH = 16


def _scatter_kernel(slots, pitch, idx_ref, y_ref, o_ref):
    @pl.when(pl.program_id(1) == 0)
    def _():
        o_ref[...] = jnp.zeros_like(o_ref)

    for m0 in range(0, slots, SCATTER_BATCH):
        rows = [pl.multiple_of(idx_ref[0, 0, m0 + u] * SUBLANES, SUBLANES) for u in range(SCATTER_BATCH)]
        vals = [o_ref[0, pl.ds(rows[u], SUBLANES), :] + y_ref[0, 0, pl.ds(m0 + u, SUBLANES, stride=pitch), :]
                for u in range(SCATTER_BATCH)]
        for u in range(SCATTER_BATCH):
            o_ref[0, pl.ds(rows[u], SUBLANES), :] = vals[u]


def _scatter(y, idx_rows, tu, slots):
    b, n_exp = y.shape[:2]
    pitch = _slot_pitch(slots)
    return pl.pallas_call(
        functools.partial(_scatter_kernel, slots, pitch),
        grid=(b, n_exp),
        in_specs=[
            pl.BlockSpec((1, 1, slots), lambda i, e: (i * n_exp + e, 0, 0), memory_space=pltpu.SMEM),
            pl.BlockSpec((1, 1, SUBLANES * pitch, LANES), lambda i, e: (i, e, 0, 0)),
        ],
        out_specs=pl.BlockSpec((1, tu * SUBLANES, LANES), lambda i, e: (i, 0, 0)),
        out_shape=jax.ShapeDtypeStruct((b, tu * SUBLANES, LANES), F32),
        compiler_params=_params("parallel", "arbitrary"),
        name="scatter",
    )(idx_rows, y)


def _final_kernel(x_ref, f_ref, mod_ref, g_ref, o_ref):
    tm = x_ref.shape[1]
    x = x_ref[0] + mod_ref[0][M_G2:M_G2 + 1, :] * _from_gather_layout(f_ref, tm)
    o_ref[0] = _rms(x) * g_ref[...]


def _final(x, ffn, mrows, final_g, n_ctx):
    b, tu, d = x.shape
    tm = FINAL_TILE
    skip = n_ctx // tm
    return pl.pallas_call(
        _final_kernel,
        grid=(b, (tu - n_ctx) // tm),
        in_specs=[
            pl.BlockSpec((1, tm, d), lambda i, j: (i, j + skip, 0)),
            pl.BlockSpec((1, tm * SUBLANES, LANES), lambda i, j: (i, j + skip, 0)),
            pl.BlockSpec((1, M_ROWS, d), lambda i, j: (i, 0, 0)),
            pl.BlockSpec((1, d), lambda i, j: (0, 0)),
        ],
        out_specs=pl.BlockSpec((1, tm, d), lambda i, j: (i, j, 0)),
        out_shape=jax.ShapeDtypeStruct((b, tu - n_ctx, d), F32),
        compiler_params=_params("parallel", "parallel"),
        name="final_norm",
    )(x, ffn, mrows, final_g.reshape(1, d))


def _rope_tables(n_lat, n_ctx):
    rows = n_lat // GRID_W
    rowp = jnp.repeat(jnp.arange(rows, dtype=F32), GRID_W)
    colp = jnp.tile(jnp.arange(GRID_W, dtype=F32), rows)
    axis_dim = HEAD_DIM // 2
    inv_freq = ROPE_BASE ** (-jnp.arange(0, axis_dim, 2, dtype=F32) / axis_dim)
    ar = rowp[:, None] * inv_freq
    ac = colp[:, None] * inv_freq
    cs = jnp.concatenate([jnp.cos(ar), jnp.cos(ar), jnp.cos(ac), jnp.cos(ac)], axis=1)
    sn = jnp.concatenate([-jnp.sin(ar), jnp.sin(ar), -jnp.sin(ac), jnp.sin(ac)], axis=1)
    reps = LANES // HEAD_DIM
    cs = jnp.concatenate([jnp.ones((n_ctx, LANES), F32), jnp.tile(cs, (1, reps))], axis=0)
    sn = jnp.concatenate([jnp.zeros((n_ctx, LANES), F32), jnp.tile(sn, (1, reps))], axis=0)
    return cs, sn


def kernel(x, c, ctx, c_ctx, w_mod, b_mod, norm1_g, norm2_g, w_in, conv_w, ret_decay_logit, attn_sink,
           w_out, w_router, w_gate, w_up, w_down, final_g):
    b, n_lat, d = x.shape
    n_ctx = ctx.shape[1]
    depth = w_in.shape[0]
    n_exp = w_router.shape[2]
    tu = n_ctx + n_lat
    assert w_in.shape[2] == IN_WIDTH and tu % INPROJ_TILE == 0 and tu % OUTPROJ_TILE == 0
    assert n_ctx % FINAL_TILE == 0 and n_lat % FINAL_TILE == 0
    assert b + 1 <= SUBLANES and n_exp == N_EXPERTS

    c_rows = jnp.concatenate([c, c_ctx[None], jnp.zeros((SUBLANES - b - 1, d), F32)], axis=0)
    mods = _mod_vectors(c_rows, w_mod, b_mod).reshape(depth, SUBLANES, 6, d)
    qdec, win, cd, dm = _decay_tables(ret_decay_logit)
    rope_cs, rope_sn = _rope_tables(n_lat, n_ctx)
    sink_rows = jnp.broadcast_to(jnp.repeat(attn_sink.astype(F32), CHUNK, axis=1)[:, :, None],
                                 (depth, ATTN_HEADS * CHUNK, LANES))
    w_in_bf = w_in.astype(BF16)
    w_out_bf = w_out.astype(BF16)

    xu = jnp.concatenate([ctx, x], axis=1)
    ffn = None
    mrows = None
    cap = CAPACITY_FACTOR * n_lat // n_exp + CAPACITY_FACTOR * n_ctx // n_exp
    for l in range(depth):
        prev_mrows = mrows
        mrows = jnp.concatenate([mods[l, :b], jnp.broadcast_to(mods[l, b][None], (b, 6, d)),
                                 jnp.zeros((b, M_ROWS - 12, d), F32)], axis=1)
        xu, proj = _inproj(xu, ffn, mrows if ffn is None else _with_prev_g2(mrows, prev_mrows),
                           norm1_g[l], w_in_bf, l, rope_cs, rope_sn, n_ctx)
        sp = _ret_states(proj[1], win[l], cd[l], n_ctx)
        mix = _mixer(proj, sp, (dm[l], qdec[l]), conv_w[l], sink_rows[l], n_ctx)
        xu, hg, aff = _outproj(mix, xu, mrows, norm2_g[l], w_out_bf, l, w_router[l], n_ctx)
        idx, gate = _route(aff, n_exp, n_ctx)
        idx_rows = idx.reshape(b * n_exp, 1, cap)
        xs = _gather(hg, idx_rows, n_exp, cap)
        y = _ffn(xs, w_gate, w_up, w_down, l, gate, cap)
        ffn = _scatter(y, idx_rows, tu, cap)
    return _final(xu, ffn, mrows, final_g, n_ctx)


def _with_prev_g2(mrows, prev_mrows):
    out = mrows.at[:, M_G2].set(prev_mrows[:, M_G2])
    return out.at[:, M_CTX + M_G2].set(prev_mrows[:, M_CTX + M_G2])
```

```python
import functools

import jax
import jax.numpy as jnp
from jax import lax
from jax.experimental import pallas as pl
from jax.experimental.pallas import tpu as pltpu

F32 = jnp.float32
BF16 = jnp.bfloat16
I32 = jnp.int32
HIGHEST = lax.Precision.HIGHEST

HEAD_DIM = 64
CONV_WIDTH = 256
RET_HEADS = 4
RET_WIDTH = RET_HEADS * HEAD_DIM
ATTN_HEADS = 8
ATTN_KV_HEADS = 2
ATTN_GROUP = ATTN_HEADS // ATTN_KV_HEADS
ATTN_WIDTH = ATTN_HEADS * HEAD_DIM
KV_WIDTH = ATTN_KV_HEADS * HEAD_DIM
CHUNK = 128
GRID_W = 64
N_EXPERTS = 16
CAPACITY_FACTOR = 2
ROPE_BASE = 10000.0
EPS = 1e-6
NEG_INF = -1e30
QK_SCALE = HEAD_DIM ** -0.5
LOG2E = 1.4426950408889634

LANES = 128
SUBLANES = 8
VMEM_LIMIT_BYTES = 56 * 1024 * 1024

O_CB = 0
O_CC = O_CB + CONV_WIDTH
O_CX = O_CC + CONV_WIDTH
O_RQ = O_CX + CONV_WIDTH
O_RK = O_RQ + RET_WIDTH
O_RV = O_RK + RET_WIDTH
O_GF = O_RV + RET_WIDTH
O_GB = O_GF + RET_WIDTH
O_AQ = O_GB + RET_WIDTH
O_AK = O_AQ + ATTN_WIDTH
O_AV = O_AK + KV_WIDTH
IN_WIDTH = O_AV + KV_WIDTH

M_SH1, M_SC1, M_G1, M_SH2, M_SC2, M_G2 = range(6)
M_CTX = 6
M_ROWS = 16

INPROJ_TILE = 544
OUTPROJ_TILE = 256
FINAL_TILE = 256


def _params(*sem, vmem=VMEM_LIMIT_BYTES):
    return pltpu.CompilerParams(dimension_semantics=sem, vmem_limit_bytes=vmem)


def _dot(a, b):
    return jnp.dot(a, b, preferred_element_type=F32)


def _dot_nt(a, b):
    return lax.dot_general(a, b, (((1,), (1,)), ((), ())), preferred_element_type=F32)


def _silu(v):
    return v * jax.nn.sigmoid(v)


def _mod_kernel(c_ref, w_ref, b_ref, o_ref):
    s = _silu(c_ref[...])
    o_ref[0] = jnp.dot(s, w_ref[0], precision=HIGHEST, preferred_element_type=F32) + b_ref[0]


def _mod_vectors(c_rows, w_mod, b_mod):
    depth, d_model, width = w_mod.shape
    tn = 1536
    return pl.pallas_call(
        _mod_kernel,
        grid=(depth, width // tn),
        in_specs=[
            pl.BlockSpec((SUBLANES, d_model), lambda l, n: (0, 0)),
            pl.BlockSpec((1, d_model, tn), lambda l, n: (l, 0, n)),
            pl.BlockSpec((1, 1, tn), lambda l, n: (l, 0, n)),
        ],
        out_specs=pl.BlockSpec((1, SUBLANES, tn), lambda l, n: (l, 0, n)),
        out_shape=jax.ShapeDtypeStruct((depth, SUBLANES, width), F32),
        compiler_params=_params("parallel", "parallel"),
        name="mod_vectors",
    )(c_rows, w_mod, b_mod.reshape(depth, 1, width))


def _log_sigmoid(v):
    return -jnp.log(1.0 + jnp.exp(-v))


def _tables_kernel(lgl_ref, lgr_ref, qdec_ref, win_ref, cd_ref, dm_ref):
    pos = lax.broadcasted_iota(I32, (CHUNK, RET_WIDTH), 0).astype(F32)
    ri = (lax.broadcasted_iota(I32, (RET_HEADS * CHUNK, CHUNK), 0) & (CHUNK - 1)).astype(F32)
    rj = lax.broadcasted_iota(I32, (RET_HEADS * CHUNK, CHUNK), 1).astype(F32)
    for d in range(2):
        lg = _log_sigmoid(lgl_ref[0, d])
        lg1 = lg[0:1, :]
        if d == 0:
            qdec_ref[0, d] = jnp.exp(lg1 * (pos + 1.0))
            win_ref[0, d] = jnp.exp(lg1 * (CHUNK - 1.0 - pos))
            diff = ri - rj
        else:
            qdec_ref[0, d] = jnp.exp(lg1 * (CHUNK - pos))
            win_ref[0, d] = jnp.exp(lg1 * pos)
            diff = rj - ri
        cd_ref[0, d] = jnp.exp(lg * float(CHUNK))
        lr = _log_sigmoid(lgr_ref[0, d])
        dm_ref[0, d] = jnp.where(diff >= 0.0, jnp.exp(lr * jnp.maximum(diff, 0.0)), 0.0)


def _decay_tables(ret_decay_logit):
    depth = ret_decay_logit.shape[0]
    lg = ret_decay_logit.astype(F32)
    lgl = jnp.broadcast_to(jnp.repeat(lg, HEAD_DIM, axis=-1)[:, :, None, :], (depth, 2, SUBLANES, RET_WIDTH))
    lgr = jnp.broadcast_to(jnp.repeat(lg, CHUNK, axis=-1)[:, :, :, None], (depth, 2, RET_HEADS * CHUNK, CHUNK))
    return pl.pallas_call(
        _tables_kernel,
        grid=(depth,),
        in_specs=[
            pl.BlockSpec((1, 2, SUBLANES, RET_WIDTH), lambda l: (l, 0, 0, 0)),
            pl.BlockSpec((1, 2, RET_HEADS * CHUNK, CHUNK), lambda l: (l, 0, 0, 0)),
        ],
        out_specs=[
            pl.BlockSpec((1, 2, CHUNK, RET_WIDTH), lambda l: (l, 0, 0, 0)),
            pl.BlockSpec((1, 2, CHUNK, RET_WIDTH), lambda l: (l, 0, 0, 0)),
            pl.BlockSpec((1, 2, SUBLANES, RET_WIDTH), lambda l: (l, 0, 0, 0)),
            pl.BlockSpec((1, 2, RET_HEADS * CHUNK, CHUNK), lambda l: (l, 0, 0, 0)),
        ],
        out_shape=[
            jax.ShapeDtypeStruct((depth, 2, CHUNK, RET_WIDTH), F32),
            jax.ShapeDtypeStruct((depth, 2, CHUNK, RET_WIDTH), F32),
            jax.ShapeDtypeStruct((depth, 2, SUBLANES, RET_WIDTH), F32),
            jax.ShapeDtypeStruct((depth, 2, RET_HEADS * CHUNK, CHUNK), F32),
        ],
        compiler_params=_params("parallel"),
        name="decay_tables",
    )(lgl, lgr)


def _rope(v, cs, sn):
    lane = lax.broadcasted_iota(I32, (1, LANES), 1)
    first = (lane & 31) < 16
    outs = []
    for g in range(v.shape[1] // LANES):
        vg = v[:, g * LANES:(g + 1) * LANES]
        sw = jnp.where(first, pltpu.roll(vg, LANES - 16, 1), pltpu.roll(vg, 16, 1))
        outs.append(vg * cs + sw * sn)
    return outs[0] if len(outs) == 1 else jnp.concatenate(outs, axis=1)


def _swap_halves(v):
    return jnp.concatenate([v[:, HEAD_DIM:], v[:, :HEAD_DIM]], axis=1)


def _from_gather_layout(f_ref, tm):
    return jnp.concatenate(
        [f_ref[0, pl.ds(q, tm, stride=SUBLANES), :] for q in range(SUBLANES)], axis=1)


def _rms(v):
    return v * lax.rsqrt(jnp.mean(v * v, axis=-1, keepdims=True) + EPS)


def _mod_rows(mod_ref, tm, n_ctx):
    m = mod_ref[0]
    if n_ctx % tm == 0:
        is_ctx = pl.program_id(1) < n_ctx // tm
    else:
        is_ctx = pl.program_id(1) * tm + lax.broadcasted_iota(I32, (tm, 1), 0) < n_ctx

    def row(r):
        return jnp.where(is_ctx, m[M_CTX + r:M_CTX + r + 1, :], m[r:r + 1, :])

    return row


def _inproj_kernel(has_ffn, n_ctx, *refs):
    if has_ffn:
        x_ref, f_ref, mod_ref, g_ref, w_ref, cs_ref, sn_ref = refs[:7]
        outs = refs[7:]
        xo_ref = outs[0]
        outs = outs[1:]
    else:
        x_ref, mod_ref, g_ref, w_ref, cs_ref, sn_ref = refs[:6]
        outs = refs[6:]
    cz_ref, rqkv_ref, gates_ref, aq_ref, kv_ref = outs
    tm = x_ref.shape[1]
    row = _mod_rows(mod_ref, tm, n_ctx)

    x = x_ref[0]
    if has_ffn:
        x = x + row(M_G2) * _from_gather_layout(f_ref, tm)
        xo_ref[0] = x
    h = (_rms(x) * g_ref[...]) * (1.0 + row(M_SC1)) + row(M_SH1)
    h = h.astype(BF16)
    cs = cs_ref[...]
    sn = sn_ref[...]

    def proj(a, b):
        return _dot(h, w_ref[:, a:b])

    cz_ref[0, :, 0:CONV_WIDTH] = proj(O_CB, O_CC)
    cz_ref[0, :, CONV_WIDTH:] = proj(O_CC, O_CX) * proj(O_CX, O_RQ)
    rqkv_ref[0, :, 0:RET_WIDTH] = _rope(proj(O_RQ, O_RK), cs, sn).astype(BF16)
    rqkv_ref[0, :, RET_WIDTH:2 * RET_WIDTH] = _rope(proj(O_RK, O_RV) * QK_SCALE, cs, sn).astype(BF16)
    rqkv_ref[0, :, 2 * RET_WIDTH:] = proj(O_RV, O_GF).astype(BF16)
    gates_ref[0, :, 0:RET_WIDTH] = _silu(proj(O_GF, O_GB))
    gates_ref[0, :, RET_WIDTH:] = _silu(proj(O_GB, O_AQ))
    aq_ref[0] = (_rope(proj(O_AQ, O_AK), cs, sn) * (QK_SCALE * LOG2E)).astype(BF16)
    ak = _rope(proj(O_AK, O_AV), cs, sn)
    av = proj(O_AV, IN_WIDTH)
    kv_ref[0, :, 0:KV_WIDTH] = ak.astype(BF16)
    kv_ref[0, :, KV_WIDTH:2 * KV_WIDTH] = _swap_halves(ak).astype(BF16)
    kv_ref[0, :, 2 * KV_WIDTH:3 * KV_WIDTH] = av.astype(BF16)
    kv_ref[0, :, 3 * KV_WIDTH:] = _swap_halves(av).astype(BF16)


def _inproj(x, ffn, mrows, norm_g, w_in_bf, layer, rope_cs, rope_sn, n_ctx):
    b, tu, d = x.shape
    tm = INPROJ_TILE
    has_ffn = ffn is not None
    tok = lambda width: pl.BlockSpec((1, tm, width), lambda i, j: (i, j, 0))
    in_specs = [tok(d)]
    args = [x]
    if has_ffn:
        in_specs.append(pl.BlockSpec((1, tm * SUBLANES, LANES), lambda i, j: (i, j, 0)))
        args.append(ffn)
    in_specs += [
        pl.BlockSpec((1, M_ROWS, d), lambda i, j: (i, 0, 0)),
        pl.BlockSpec((1, d), lambda i, j: (0, 0)),
        pl.BlockSpec((None, d, IN_WIDTH), lambda i, j: (layer, 0, 0)),
        pl.BlockSpec((tm, LANES), lambda i, j: (j, 0)),
        pl.BlockSpec((tm, LANES), lambda i, j: (j, 0)),
    ]
    args += [mrows, norm_g.reshape(1, d), w_in_bf, rope_cs, rope_sn]
    widths = [(2 * CONV_WIDTH, F32), (3 * RET_WIDTH, BF16), (2 * RET_WIDTH, F32), (ATTN_WIDTH, BF16),
              (4 * KV_WIDTH, BF16)]
    out_specs = [tok(w) for w, _ in widths]
    out_shape = [jax.ShapeDtypeStruct((b, tu, w), dt) for w, dt in widths]
    if has_ffn:
        out_specs = [tok(d)] + out_specs
        out_shape = [jax.ShapeDtypeStruct((b, tu, d), F32)] + out_shape
    res = pl.pallas_call(
        functools.partial(_inproj_kernel, has_ffn, n_ctx),
        grid=(b, tu // tm),
        in_specs=in_specs,
        out_specs=out_specs,
        out_shape=out_shape,
        compiler_params=_params("parallel", "parallel"),
        name="inproj",
    )(*args)
    if has_ffn:
        return res[0], res[1:]
    return x, res


def _head_block_mask(n):
    r = lax.broadcasted_iota(I32, (n, n), 0) // HEAD_DIM
    c = lax.broadcasted_iota(I32, (n, n), 1) // HEAD_DIM
    return r == c


def _states_kernel(n_ctx_chunks, n_chunks, rqkv_ref, win_ref, cd_ref, sp_ref, s_scr):
    same_head = _head_block_mask(RET_WIDTH)
    s_scr[...] = jnp.zeros_like(s_scr)

    def body(i, carry):
        for d in range(2):
            if d == 0:
                c = i
            else:
                c = jnp.where(i < n_ctx_chunks, n_ctx_chunks - 1 - i, n_chunks + n_ctx_chunks - 1 - i)
            off = pl.multiple_of(c * CHUNK, CHUNK)
            kw = rqkv_ref[0, pl.ds(off, CHUNK), RET_WIDTH:2 * RET_WIDTH].astype(F32) * win_ref[d]
            v = rqkv_ref[0, pl.ds(off, CHUNK), 2 * RET_WIDTH:]
            u = _dot(kw.T.astype(BF16), v)
            s = s_scr[d]
            sp_ref[0, d, pl.ds(c, 1)] = s.astype(BF16)[None]
            s_scr[d] = s * cd_ref[d][0:1, :] + jnp.where(same_head, u, 0.0)
        return carry

    lax.fori_loop(0, n_chunks, body, 0)


def _ret_states(rqkv, win, cd, n_ctx):
    b, tu, _ = rqkv.shape
    w = RET_WIDTH
    n_chunks = tu // CHUNK
    return pl.pallas_call(
        functools.partial(_states_kernel, n_ctx // CHUNK, n_chunks),
        grid=(b,),
        in_specs=[
            pl.BlockSpec((1, tu, 3 * w), lambda i: (i, 0, 0)),
            pl.BlockSpec((2, CHUNK, w), lambda i: (0, 0, 0)),
            pl.BlockSpec((2, SUBLANES, w), lambda i: (0, 0, 0)),
        ],
        out_specs=pl.BlockSpec((1, 2, n_chunks, w, w), lambda i: (i, 0, 0, 0, 0)),
        out_shape=jax.ShapeDtypeStruct((b, 2, n_chunks, w, w), BF16),
        scratch_shapes=[pltpu.VMEM((2, w, w), F32)],
        compiler_params=_params("parallel"),
        name="ret_states",
    )(rqkv, win, cd)


def _group_mean(t, ones_bd):
    hi = t.astype(BF16)
    lo = (t - hi.astype(F32)).astype(BF16)
    s = _dot(jnp.concatenate([hi, lo], axis=0), ones_bd)
    n = t.shape[0]
    return (s[:n] + s[n:]) * (1.0 / HEAD_DIM)


def _mix_chunk(c, n_ctx_chunks, n_chunks, cz, z_last, z_first, cw, rqkv, gates, sp, dm_ref, qdec_ref,
               qa, kvs, sink_ref):
    is_lat = c >= n_ctx_chunks

    z = cz[:, CONV_WIDTH:]
    row = lax.broadcasted_iota(I32, (CHUNK, 1), 0)
    has_prev = jnp.logical_and(c != 0, c != n_ctx_chunks)
    has_next = jnp.logical_and(c != n_ctx_chunks - 1, c != n_chunks - 1)
    z_before = jnp.where(row == 0, jnp.where(has_prev, z_last, 0.0), pltpu.roll(z, 1, 0))
    z_after = jnp.where(row == CHUNK - 1, jnp.where(has_next, z_first, 0.0), pltpu.roll(z, CHUNK - 1, 0))
    conv = cz[:, :CONV_WIDTH] * (z_before * cw[0:1, :] + z * cw[1:2, :] + z_after * cw[2:3, :])

    q = rqkv[:, 0:RET_WIDTH]
    k = rqkv[:, RET_WIDTH:2 * RET_WIDTH]
    v = rqkv[:, 2 * RET_WIDTH:]
    lane_head = lax.broadcasted_iota(I32, (1, RET_WIDTH), 1) // HEAD_DIM
    qz = jnp.zeros_like(q)
    qs = jnp.concatenate([jnp.where(lane_head == hh, q, qz) for hh in range(RET_HEADS)], axis=0)
    scores = _dot_nt(qs, k)
    qf = q.astype(F32)
    outs = []
    for d in range(2):
        p = (scores * dm_ref[d]).astype(BF16)
        o_all = _dot(p, v)
        o = _dot((qf * qdec_ref[d]).astype(BF16), sp[d])
        for hh in range(RET_HEADS):
            o = o + jnp.where(lane_head == hh, o_all[hh * CHUNK:(hh + 1) * CHUNK, :], 0.0)
        outs.append(o)
    o2 = jnp.concatenate(outs, axis=0)
    ones_bd = jnp.where(_head_block_mask(RET_WIDTH), 1.0, 0.0).astype(BF16)
    dl = o2 - _group_mean(o2, ones_bd)
    on = dl * lax.rsqrt(_group_mean(dl * dl, ones_bd) + EPS)
    ret = on[:CHUNK] * gates[:, 0:RET_WIDTH] + on[CHUNK:] * gates[:, RET_WIDTH:]

    keys = tuple(jnp.concatenate([t[:, i * KV_WIDTH:(i + 1) * KV_WIDTH] for t in kvs], axis=0) for i in (0, 1))
    vals = tuple(jnp.concatenate([t[:, i * KV_WIDTH:(i + 1) * KV_WIDTH] for t in kvs], axis=0) for i in (2, 3))
    n_ctx = kvs[3].shape[0]
    rows = 2 * CHUNK
    iq = lax.broadcasted_iota(I32, (rows, CHUNK), 0) & (CHUNK - 1)
    ik = lax.broadcasted_iota(I32, (rows, CHUNK), 1)
    ok_prev = jnp.logical_and(jnp.logical_and(is_lat, c - 1 >= n_ctx_chunks), ik >= iq)
    ok_cur = jnp.logical_and(is_lat, ik >= 0)
    ok_next = jnp.logical_and(jnp.logical_and(is_lat, c + 1 <= n_chunks - 1), ik <= iq)
    ok_ctx = jnp.ones((rows, n_ctx), jnp.bool_)
    valid = jnp.concatenate([ok_prev, ok_cur, ok_next, ok_ctx], axis=1)
    half = lax.broadcasted_iota(I32, (1, LANES), 1) // HEAD_DIM
    qa_z = jnp.zeros((CHUNK, LANES), BF16)
    cols = [None] * (ATTN_HEADS // 2)
    for hk in range(ATTN_KV_HEADS):
        outs = []
        for par in range(2):
            ja, jb = 2 * hk, 2 * hk + 1
            qst = jnp.concatenate([
                jnp.where(half == par, qa[:, ja * LANES:(ja + 1) * LANES], qa_z),
                jnp.where(half == par, qa[:, jb * LANES:(jb + 1) * LANES], qa_z)], axis=0)
            sel = 0 if par == hk else 1
            s = jnp.where(valid, _dot_nt(qst, keys[sel]), NEG_INF)
            ha, hb = ATTN_GROUP * hk + par, ATTN_GROUP * hk + par + 2
            snk = jnp.concatenate([sink_ref[ha * CHUNK:(ha + 1) * CHUNK, 0:1],
                                   sink_ref[hb * CHUNK:(hb + 1) * CHUNK, 0:1]], axis=0) * LOG2E
            mx = jnp.maximum(jnp.max(s, axis=1, keepdims=True), snk)
            e = jnp.exp2(s - mx)
            den = jnp.sum(e, axis=1, keepdims=True) + jnp.exp2(snk - mx)
            outs.append(_dot(e.astype(BF16), vals[sel]) * (1.0 / den))
        cols[2 * hk] = jnp.where(half == 0, outs[0][:CHUNK], outs[1][:CHUNK])
        cols[2 * hk + 1] = jnp.where(half == 0, outs[0][CHUNK:], outs[1][CHUNK:])

    return jnp.concatenate([conv, ret] + cols, axis=1)


MIX_CHUNKS = 2


def _mixer_kernel(n_ctx_chunks, n_chunks,
                  cz_ref, czp_ref, czn_ref, cw_ref, rqkv_ref, gates_ref, sp_ref, dm_ref, qdec_ref,
                  aq_ref, kvp_ref, kvc_ref, kvn_ref, kvx_ref, sink_ref, mix_ref):
    j = pl.program_id(1)
    cw = cw_ref[...]
    kvx = kvx_ref[0]
    for s in range(MIX_CHUNKS):
        lo, hi = s * CHUNK, (s + 1) * CHUNK
        if s == 0:
            z_last = czp_ref[0, SUBLANES - 1:SUBLANES, CONV_WIDTH:]
            kv_prev = kvp_ref[0]
        else:
            z_last = cz_ref[0, lo - 1:lo, CONV_WIDTH:]
            kv_prev = kvc_ref[0, lo - CHUNK:lo, :]
        if s == MIX_CHUNKS - 1:
            z_first = czn_ref[0, 0:1, CONV_WIDTH:]
            kv_next = kvn_ref[0]
        else:
            z_first = cz_ref[0, hi:hi + 1, CONV_WIDTH:]
            kv_next = kvc_ref[0, hi:hi + CHUNK, :]
        mix = _mix_chunk(j * MIX_CHUNKS + s, n_ctx_chunks, n_chunks, cz_ref[0, lo:hi, :], z_last, z_first, cw,
                         rqkv_ref[0, lo:hi, :], gates_ref[0, lo:hi, :], (sp_ref[0, 0, s], sp_ref[0, 1, s]),
                         dm_ref, qdec_ref, aq_ref[0, lo:hi, :], [kv_prev, kvc_ref[0, lo:hi, :], kv_next, kvx],
                         sink_ref)
        mix_ref[0, lo:hi, :] = mix.astype(BF16)


def _mixer(proj, sp, tabs, conv_w, sink_rows, n_ctx):
    cz, rqkv, gates, aq, kv = proj
    dm, qdec = tabs
    b, tu, _ = cz.shape
    n_chunks = tu // CHUNK
    mc = MIX_CHUNKS
    rows = mc * CHUNK
    assert n_chunks % mc == 0 and (n_ctx // CHUNK) % mc == 0
    last = n_chunks - 1
    per8 = rows // SUBLANES
    cur = lambda w: pl.BlockSpec((1, rows, w), lambda i, j: (i, j, 0))
    full = lambda shape: pl.BlockSpec(shape, lambda i, j: (0,) * len(shape))
    wcz, wkv = cz.shape[2], kv.shape[2]
    in_specs = [
        cur(wcz),
        pl.BlockSpec((1, SUBLANES, wcz), lambda i, j: (i, jnp.maximum(j * per8 - 1, 0), 0)),
        pl.BlockSpec((1, SUBLANES, wcz), lambda i, j: (i, jnp.minimum((j + 1) * per8, tu // SUBLANES - 1), 0)),
        full(conv_w.shape),
        cur(rqkv.shape[2]), cur(gates.shape[2]),
        pl.BlockSpec((1, 2, mc, RET_WIDTH, RET_WIDTH), lambda i, j: (i, 0, j, 0, 0)),
        full(dm.shape), full(qdec.shape),
        cur(ATTN_WIDTH),
        pl.BlockSpec((1, CHUNK, wkv), lambda i, j: (i, jnp.maximum(j * mc - 1, 0), 0)),
        cur(wkv),
        pl.BlockSpec((1, CHUNK, wkv), lambda i, j: (i, jnp.minimum((j + 1) * mc, last), 0)),
        pl.BlockSpec((1, n_ctx, wkv), lambda i, j: (i, 0, 0)),
        full(sink_rows.shape),
    ]
    args = [cz, cz, cz, conv_w, rqkv, gates, sp, dm, qdec, aq, kv, kv, kv, kv, sink_rows]
    width = CONV_WIDTH + RET_WIDTH + ATTN_WIDTH
    return pl.pallas_call(
        functools.partial(_mixer_kernel, n_ctx // CHUNK, n_chunks),
        grid=(b, n_chunks // mc),
        in_specs=in_specs,
        out_specs=pl.BlockSpec((1, rows, width), lambda i, j: (i, j, 0)),
        out_shape=jax.ShapeDtypeStruct((b, tu, width), BF16),
        compiler_params=_params("parallel", "parallel"),
        name="mixer",
    )(*args)


def _outproj_kernel(n_ctx, n_exp, mix_ref, x_ref, mod_ref, g_ref, w_ref, wrh_ref, wrl_ref, xo_ref, hg_ref, aff_ref):
    tm = x_ref.shape[1]
    row = _mod_rows(mod_ref, tm, n_ctx)

    x = x_ref[0] + row(M_G1) * _dot(mix_ref[0], w_ref[...])
    xo_ref[0] = x
    h = (_rms(x) * g_ref[...]) * (1.0 + row(M_SC2)) + row(M_SH2)
    for q in range(SUBLANES):
        hg_ref[0, pl.ds(q, tm, stride=SUBLANES), :] = h[:, q * LANES:(q + 1) * LANES]
    h_hi = h.astype(BF16)
    h_lo = (h - h_hi.astype(F32)).astype(BF16)
    logits = _dot(h_hi, wrh_ref[...]) + (_dot(h_lo, wrh_ref[...]) + _dot(h_hi, wrl_ref[...]))
    lane = lax.broadcasted_iota(I32, (1, LANES), 1)
    logits = jnp.where(lane < n_exp, logits, NEG_INF)
    e = jnp.exp(logits - jnp.max(logits, axis=1, keepdims=True))
    aff_ref[0] = e / jnp.sum(e, axis=1, keepdims=True)


def _outproj(mix, x, mrows, norm_g, w_out_bf, layer, w_router, n_ctx):
    b, tu, d = x.shape
    tm = OUTPROJ_TILE
    n_exp = w_router.shape[1]
    wr = jnp.pad(w_router, ((0, 0), (0, LANES - n_exp)))
    wr_hi = wr.astype(BF16)
    wr_lo = (wr - wr_hi.astype(F32)).astype(BF16)
    return pl.pallas_call(
        functools.partial(_outproj_kernel, n_ctx, n_exp),
        grid=(b, tu // tm),
        in_specs=[
            pl.BlockSpec((1, tm, mix.shape[2]), lambda i, j: (i, j, 0)),
            pl.BlockSpec((1, tm, d), lambda i, j: (i, j, 0)),
            pl.BlockSpec((1, M_ROWS, d), lambda i, j: (i, 0, 0)),
            pl.BlockSpec((1, d), lambda i, j: (0, 0)),
            pl.BlockSpec((None,) + w_out_bf.shape[1:], lambda i, j: (layer, 0, 0)),
            pl.BlockSpec((d, LANES), lambda i, j: (0, 0)),
            pl.BlockSpec((d, LANES), lambda i, j: (0, 0)),
        ],
        out_specs=[
            pl.BlockSpec((1, tm, d), lambda i, j: (i, j, 0)),
            pl.BlockSpec((1, tm * SUBLANES, LANES), lambda i, j: (i, j, 0)),
            pl.BlockSpec((1, tm, LANES), lambda i, j: (i, j, 0)),
        ],
        out_shape=[
            jax.ShapeDtypeStruct((b, tu, d), F32),
            jax.ShapeDtypeStruct((b, tu * SUBLANES, LANES), F32),
            jax.ShapeDtypeStruct((b, tu, LANES), F32),
        ],
        compiler_params=_params("parallel", "parallel"),
        name="outproj",
    )(mix, x, mrows, norm_g.reshape(1, d), w_out_bf, wr_hi, wr_lo)


ROUTE_TABLES = 4


def _route_stream_tables(v, k, tab_scr, ce_scr, cnt_scr, stream, thr):
    n_exp, n = v.shape
    lane = lax.broadcasted_iota(I32, (1, LANES), 1)
    tri_r = lax.broadcasted_iota(I32, (LANES, LANES), 0)
    tri_c = lax.broadcasted_iota(I32, (LANES, LANES), 1)
    upper = jnp.where(tri_r <= tri_c, 1.0, 0.0).astype(BF16)
    above = pltpu.bitcast(thr + 1, F32)
    floor = pltpu.bitcast(thr, F32)
    n_gt = jnp.sum(jnp.where(v >= above, 1, 0), axis=1, keepdims=True)
    need = (k - n_gt).astype(F32)
    seen_eq = jnp.zeros((n_exp, 1), F32)
    counts = jnp.zeros((n_exp, LANES), F32)
    for j in range(n // LANES):
        vb = v[:, j * LANES:(j + 1) * LANES]
        gt = vb >= above
        eq = jnp.logical_and(vb >= floor, vb < above)
        both = jnp.concatenate([jnp.where(gt, 1.0, 0.0), jnp.where(eq, 1.0, 0.0)], axis=0).astype(BF16)
        pref = _dot(both, upper)
        rank_eq = pref[n_exp:] + seen_eq
        inc = pref[:n_exp] + jnp.minimum(rank_eq, need) - jnp.minimum(seen_eq, need)
        sel = jnp.logical_or(gt, jnp.logical_and(eq, rank_eq <= need))
        seen_eq = rank_eq[:, LANES - 1:LANES]
        counts = counts + jnp.where(lane == j, inc[:, LANES - 1:LANES], 0.0)
        a0 = jnp.where(sel, vb, 0.0)
        t0 = a0.astype(BF16).astype(F32)
        t1 = (a0 - t0).astype(BF16).astype(F32)
        t2 = a0 - t0 - t1
        for t, val in enumerate((inc, t0, t1, t2)):
            tab_scr[stream * ROUTE_TABLES + t, pl.ds(j, n_exp, stride=LANES), :] = val
    through = _dot(counts.astype(BF16), upper)
    for e in range(n_exp):
        ce_scr[stream, e] = through[e:e + 1, :]
        cnt_scr[stream, e] = counts[e:e + 1, :]


def _route_slots(e, stream, k, lo, tab_scr, ce_scr, cnt_scr):
    lane_f = lax.broadcasted_iota(I32, (1, LANES), 1).astype(F32)
    slot = lax.broadcasted_iota(I32, (k, LANES), 0).astype(F32)
    ones_rows = jnp.ones((LANES, LANES), BF16)
    cnt_rows = jnp.broadcast_to(cnt_scr[stream, e], (LANES, LANES)).astype(BF16)
    rows_e = pl.ds(pl.multiple_of(e * LANES, LANES), LANES)
    tab = jnp.concatenate([tab_scr[stream * ROUTE_TABLES + t, rows_e, :] for t in range(ROUTE_TABLES)],
                          axis=1).astype(BF16)
    before = jnp.where(ce_scr[stream, e] <= slot, 1.0, 0.0).astype(BF16)
    blk = _dot_nt(before, ones_rows)
    base = _dot_nt(before, cnt_rows)
    row = _dot(jnp.where(lane_f == blk, 1.0, 0.0).astype(BF16), tab)
    inc = row[:, :LANES]
    aff = row[:, LANES:2 * LANES] + row[:, 2 * LANES:3 * LANES] + row[:, 3 * LANES:]
    local = slot - base
    pos = _dot_nt(jnp.where(inc <= local, 1.0, 0.0).astype(BF16), ones_rows)
    tok = blk * LANES + pos + float(lo)
    gate = jnp.sum(jnp.where(inc == local + 1.0, aff, 0.0), axis=1, keepdims=True)
    return tok, gate


def _route_kernel(n_exp, streams, aff_ref, idx_ref, gate_ref, tab_scr, ce_scr, cnt_scr):
    a = aff_ref[0].T[:n_exp, :]
    tab_scr[...] = jnp.zeros_like(tab_scr)
    vs = [a[:, lo:lo + n] for lo, n, _, _ in streams]

    def search(i, ts):
        out = []
        for t, v, (_, _, k, _) in zip(ts, vs, streams):
            cand = t | jnp.left_shift(jnp.int32(1), 30 - i)
            cnt = jnp.sum(jnp.where(v >= pltpu.bitcast(cand, F32), 1, 0), axis=1, keepdims=True)
            out.append(jnp.where(cnt >= k, cand, t))
        return tuple(out)

    thrs = lax.fori_loop(0, 31, search, tuple(jnp.zeros((n_exp, 1), I32) for _ in streams))
    for st, (v, (_, _, k, _)) in enumerate(zip(vs, streams)):
        _route_stream_tables(v, k, tab_scr, ce_scr, cnt_scr, st, thrs[st])

    slots = idx_ref.shape[3]

    def per_expert(e, carry):
        toks = []
        for st, (lo, _, k, slot0) in enumerate(streams):
            assert slot0 == sum(t.shape[0] for t in toks)
            tok, gate = _route_slots(e, st, k, lo, tab_scr, ce_scr, cnt_scr)
            toks.append(tok)
            gate_ref[0, pl.ds(e, 1), slot0:slot0 + k, :] = gate[None]
        toks.append(jnp.zeros((-slots % LANES, LANES), F32))
        row = jnp.concatenate(toks, axis=0).T[0:1, :slots]
        idx_ref[0, pl.ds(e, 1)] = row.astype(I32)[None]
        return carry

    lax.fori_loop(0, n_exp, per_expert, 0)


def _route(aff, n_exp, n_ctx):
    b, tu, _ = aff.shape
    n_lat = tu - n_ctx
    cap_l = CAPACITY_FACTOR * n_lat // n_exp
    cap_c = CAPACITY_FACTOR * n_ctx // n_exp
    slots = cap_l + cap_c
    assert n_lat // LANES <= LANES and n_ctx // LANES <= LANES
    streams = ((n_ctx, n_lat, cap_l, 0), (0, n_ctx, cap_c, cap_l))
    return pl.pallas_call(
        functools.partial(_route_kernel, n_exp, streams),
        grid=(b,),
        in_specs=[pl.BlockSpec((1, tu, LANES), lambda i: (i, 0, 0))],
        out_specs=[pl.BlockSpec((1, n_exp, 1, slots), lambda i: (i, 0, 0, 0)),
                   pl.BlockSpec((1, n_exp, slots, 1), lambda i: (i, 0, 0, 0))],
        out_shape=[jax.ShapeDtypeStruct((b, n_exp, 1, slots), I32),
                   jax.ShapeDtypeStruct((b, n_exp, slots, 1), F32)],
        scratch_shapes=[pltpu.VMEM((len(streams) * ROUTE_TABLES, n_exp * LANES, LANES), F32),
                        pltpu.VMEM((len(streams), n_exp, 1, LANES), F32),
                        pltpu.VMEM((len(streams), n_exp, 1, LANES), F32)],
        compiler_params=_params("parallel"),
        name="route",
    )(aff)


def _slot_pitch(slots):
    return slots + SUBLANES


def _zero_slot_padding(ref, lead, slots, pitch):
    for q in range(SUBLANES):
        ref[lead + (pl.ds(q * pitch + slots, pitch - slots), slice(None))] = jnp.zeros((pitch - slots, LANES), ref.dtype)


BF16_ROWS = 2 * SUBLANES


def _slot_pitch_bf16(slots):
    return -(-slots // BF16_ROWS) * BF16_ROWS + BF16_ROWS


def _gather_kernel(slots, pitch, pitch_out, idx_ref, h_ref, o_ref, tile_scr):
    for mi in range(slots):
        r = pl.multiple_of(idx_ref[0, 0, mi] * SUBLANES, SUBLANES)
        tile_scr[pl.ds(mi, SUBLANES, stride=pitch), :] = h_ref[0, pl.ds(r, SUBLANES), :]
    _zero_slot_padding(o_ref, (0, 0), slots, pitch_out)
    for q in range(SUBLANES):
        o_ref[0, 0, q * pitch_out:q * pitch_out + slots, :] = tile_scr[q * pitch:q * pitch + slots, :].astype(BF16)


def _gather(hg, idx_rows, n_exp, slots):
    b = hg.shape[0]
    pitch = _slot_pitch(slots)
    pitch_out = _slot_pitch_bf16(slots)
    return pl.pallas_call(
        functools.partial(_gather_kernel, slots, pitch, pitch_out),
        grid=(b, n_exp),
        in_specs=[
            pl.BlockSpec((1, 1, slots), lambda i, e: (i * n_exp + e, 0, 0), memory_space=pltpu.SMEM),
            pl.BlockSpec((1,) + hg.shape[1:], lambda i, e: (i, 0, 0)),
        ],
        out_specs=pl.BlockSpec((1, 1, SUBLANES * pitch_out, LANES), lambda i, e: (i, e, 0, 0)),
        out_shape=jax.ShapeDtypeStruct((b, n_exp, SUBLANES * pitch_out, LANES), BF16),
        scratch_shapes=[pltpu.VMEM((SUBLANES * pitch, LANES), F32)],
        compiler_params=_params("parallel", "arbitrary"),
        name="gather",
    )(idx_rows, hg)


FFN_SAMPLES = 2
FFN_VMEM_LIMIT_BYTES = 60 * 1024 * 1024


def _ffn_kernel(slots, pitch_in, pitch, xs_ref, wg0_ref, wg1_ref, wu0_ref, wu1_ref, wd0_ref, wd1_ref, gate_ref, y_ref,
                wg_bf, wu_bf, wd_bf):
    @pl.when(pl.program_id(1) == 0)
    def _():
        half = wg_bf.shape[0] // 2
        for dst, lo_ref, hi_ref in ((wg_bf, wg0_ref, wg1_ref), (wu_bf, wu0_ref, wu1_ref), (wd_bf, wd0_ref, wd1_ref)):
            dst[:half, :] = lo_ref[0].astype(BF16)
            dst[half:, :] = hi_ref[0].astype(BF16)

    n = xs_ref.shape[0]
    x = jnp.concatenate(
        [jnp.concatenate([xs_ref[s, 0, q * pitch_in:q * pitch_in + slots, :] for q in range(SUBLANES)], axis=1)
         for s in range(n)], axis=0)
    gate = jnp.concatenate([gate_ref[s, 0] for s in range(n)], axis=0)
    a = _dot(x, wg_bf[...])
    u = _dot(x, wu_bf[...])
    y = _dot((_silu(a) * u).astype(BF16), wd_bf[...]) * gate
    for s in range(n):
        _zero_slot_padding(y_ref, (s, 0), slots, pitch)
        for q in range(SUBLANES):
            y_ref[s, 0, q * pitch:q * pitch + slots, :] = y[s * slots:(s + 1) * slots, q * LANES:(q + 1) * LANES]


def _ffn(xs, w_gate, w_up, w_down, layer, gate, slots):
    b, n_exp = xs.shape[:2]
    pitch = _slot_pitch(slots)
    pitch_in = _slot_pitch_bf16(slots)
    d, f = w_gate.shape[2:]
    ns = FFN_SAMPLES
    steps = b // ns
    assert steps == 2 and b % ns == 0
    slot_tile = lambda p: pl.BlockSpec((ns, 1, SUBLANES * p, LANES), lambda e, i: (i, e, 0, 0))

    def half(rows, cols, q):
        return pl.BlockSpec((None, 1, rows // 2, cols),
                            lambda e, i: (layer, jnp.minimum(e + (i > q).astype(jnp.int32), n_exp - 1), q, 0))

    return pl.pallas_call(
        functools.partial(_ffn_kernel, slots, pitch_in, pitch),
        grid=(n_exp, steps),
        in_specs=[
            slot_tile(pitch_in),
            half(d, f, 0), half(d, f, 1), half(d, f, 0), half(d, f, 1), half(f, d, 0), half(f, d, 1),
            pl.BlockSpec((ns, 1, slots, 1), lambda e, i: (i, e, 0, 0)),
        ],
        out_specs=slot_tile(pitch),
        out_shape=jax.ShapeDtypeStruct((b, n_exp, SUBLANES * pitch, LANES), F32),
        scratch_shapes=[pltpu.VMEM((d, f), BF16), pltpu.VMEM((d, f), BF16), pltpu.VMEM((f, d), BF16)],
        compiler_params=_params("arbitrary", "arbitrary", vmem=FFN_VMEM_LIMIT_BYTES),
        name="ffn",
    )(xs, w_gate, w_gate, w_up, w_up, w_down, w_down, gate)


SCATTER_BATCH = 16


def _scatter_kernel(slots, pitch, idx_ref, y_ref, o_ref):
    @pl.when(pl.program_id(1) == 0)
    def _():
        o_ref[...] = jnp.zeros_like(o_ref)

    for m0 in range(0, slots, SCATTER_BATCH):
        rows = [pl.multiple_of(idx_ref[0, 0, m0 + u] * SUBLANES, SUBLANES) for u in range(SCATTER_BATCH)]
        vals = [o_ref[0, pl.ds(rows[u], SUBLANES), :] + y_ref[0, 0, pl.ds(m0 + u, SUBLANES, stride=pitch), :]
                for u in range(SCATTER_BATCH)]
        for u in range(SCATTER_BATCH):
            o_ref[0, pl.ds(rows[u], SUBLANES), :] = vals[u]


def _scatter(y, idx_rows, tu, slots):
    b, n_exp = y.shape[:2]
    pitch = _slot_pitch(slots)
    return pl.pallas_call(
        functools.partial(_scatter_kernel, slots, pitch),
        grid=(b, n_exp),
        in_specs=[
            pl.BlockSpec((1, 1, slots), lambda i, e: (i * n_exp + e, 0, 0), memory_space=pltpu.SMEM),
            pl.BlockSpec((1, 1, SUBLANES * pitch, LANES), lambda i, e: (i, e, 0, 0)),
        ],
        out_specs=pl.BlockSpec((1, tu * SUBLANES, LANES), lambda i, e: (i, 0, 0)),
        out_shape=jax.ShapeDtypeStruct((b, tu * SUBLANES, LANES), F32),
        compiler_params=_params("parallel", "arbitrary"),
        name="scatter",
    )(idx_rows, y)


def _final_kernel(x_ref, f_ref, mod_ref, g_ref, o_ref):
    tm = x_ref.shape[1]
    x = x_ref[0] + mod_ref[0][M_G2:M_G2 + 1, :] * _from_gather_layout(f_ref, tm)
    o_ref[0] = _rms(x) * g_ref[...]


def _final(x, ffn, mrows, final_g, n_ctx):
    b, tu, d = x.shape
    tm = FINAL_TILE
    skip = n_ctx // tm
    return pl.pallas_call(
        _final_kernel,
        grid=(b, (tu - n_ctx) // tm),
        in_specs=[
            pl.BlockSpec((1, tm, d), lambda i, j: (i, j + skip, 0)),
            pl.BlockSpec((1, tm * SUBLANES, LANES), lambda i, j: (i, j + skip, 0)),
            pl.BlockSpec((1, M_ROWS, d), lambda i, j: (i, 0, 0)),
            pl.BlockSpec((1, d), lambda i, j: (0, 0)),
        ],
        out_specs=pl.BlockSpec((1, tm, d), lambda i, j: (i, j, 0)),
        out_shape=jax.ShapeDtypeStruct((b, tu - n_ctx, d), F32),
        compiler_params=_params("parallel", "parallel"),
        name="final_norm",
    )(x, ffn, mrows, final_g.reshape(1, d))


def _rope_tables(n_lat, n_ctx):
    rows = n_lat // GRID_W
    rowp = jnp.repeat(jnp.arange(rows, dtype=F32), GRID_W)
    colp = jnp.tile(jnp.arange(GRID_W, dtype=F32), rows)
    axis_dim = HEAD_DIM // 2
    inv_freq = ROPE_BASE ** (-jnp.arange(0, axis_dim, 2, dtype=F32) / axis_dim)
    ar = rowp[:, None] * inv_freq
    ac = colp[:, None] * inv_freq
    cs = jnp.concatenate([jnp.cos(ar), jnp.cos(ar), jnp.cos(ac), jnp.cos(ac)], axis=1)
    sn = jnp.concatenate([-jnp.sin(ar), jnp.sin(ar), -jnp.sin(ac), jnp.sin(ac)], axis=1)
    reps = LANES // HEAD_DIM
    cs = jnp.concatenate([jnp.ones((n_ctx, LANES), F32), jnp.tile(cs, (1, reps))], axis=0)
    sn = jnp.concatenate([jnp.zeros((n_ctx, LANES), F32), jnp.tile(sn, (1, reps))], axis=0)
    return cs, sn


def kernel(x, c, ctx, c_ctx, w_mod, b_mod, norm1_g, norm2_g, w_in, conv_w, ret_decay_logit, attn_sink,
           w_out, w_router, w_gate, w_up, w_down, final_g):
    b, n_lat, d = x.shape
    n_ctx = ctx.shape[1]
    depth = w_in.shape[0]
    n_exp = w_router.shape[2]
    tu = n_ctx + n_lat
    assert w_in.shape[2] == IN_WIDTH and tu % INPROJ_TILE == 0 and tu % OUTPROJ_TILE == 0
    assert n_ctx % FINAL_TILE == 0 and n_lat % FINAL_TILE == 0
    assert b + 1 <= SUBLANES and n_exp == N_EXPERTS

    c_rows = jnp.concatenate([c, c_ctx[None], jnp.zeros((SUBLANES - b - 1, d), F32)], axis=0)
    mods = _mod_vectors(c_rows, w_mod, b_mod).reshape(depth, SUBLANES, 6, d)
    qdec, win, cd, dm = _decay_tables(ret_decay_logit)
    rope_cs, rope_sn = _rope_tables(n_lat, n_ctx)
    sink_rows = jnp.broadcast_to(jnp.repeat(attn_sink.astype(F32), CHUNK, axis=1)[:, :, None],
                                 (depth, ATTN_HEADS * CHUNK, LANES))
    w_in_bf = w_in.astype(BF16)
    w_out_bf = w_out.astype(BF16)

    xu = jnp.concatenate([ctx, x], axis=1)
    ffn = None
    mrows = None
    cap = CAPACITY_FACTOR * n_lat // n_exp + CAPACITY_FACTOR * n_ctx // n_exp
    for l in range(depth):
        prev_mrows = mrows
        mrows = jnp.concatenate([mods[l, :b], jnp.broadcast_to(mods[l, b][None], (b, 6, d)),
                                 jnp.zeros((b, M_ROWS - 12, d), F32)], axis=1)
        xu, proj = _inproj(xu, ffn, mrows if ffn is None else _with_prev_g2(mrows, prev_mrows),
                           norm1_g[l], w_in_bf, l, rope_cs, rope_sn, n_ctx)
        sp = _ret_states(proj[1], win[l], cd[l], n_ctx)
        mix = _mixer(proj, sp, (dm[l], qdec[l]), conv_w[l], sink_rows[l], n_ctx)
        xu, hg, aff = _outproj(mix, xu, mrows, norm2_g[l], w_out_bf, l, w_router[l], n_ctx)
        idx, gate = _route(aff, n_exp, n_ctx)
        idx_rows = idx.reshape(b * n_exp, 1, cap)
        xs = _gather(hg, idx_rows, n_exp, cap)
        y = _ffn(xs, w_gate, w_up, w_down, l, gate, cap)
        ffn = _scatter(y, idx_rows, tu, cap)
    return _final(xu, ffn, mrows, final_g, n_ctx)


def _with_prev_g2(mrows, prev_mrows):
    out = mrows.at[:, M_G2].set(prev_mrows[:, M_G2])
    return out.at[:, M_CTX + M_G2].set(prev_mrows[:, M_CTX + M_G2])
```

```python
import functools

import jax
import jax.numpy as jnp
from jax import lax
from jax.experimental import pallas as pl
from jax.experimental.pallas import tpu as pltpu

F32 = jnp.float32
BF16 = jnp.bfloat16
I32 = jnp.int32
HIGHEST = lax.Precision.HIGHEST

HEAD_DIM = 64
CONV_WIDTH = 256
RET_HEADS = 4
RET_WIDTH = RET_HEADS * HEAD_DIM
ATTN_HEADS = 8
ATTN_KV_HEADS = 2
ATTN_GROUP = ATTN_HEADS // ATTN_KV_HEADS
ATTN_WIDTH = ATTN_HEADS * HEAD_DIM
KV_WIDTH = ATTN_KV_HEADS * HEAD_DIM
CHUNK = 128
GRID_W = 64
N_EXPERTS = 16
CAPACITY_FACTOR = 2
ROPE_BASE = 10000.0
EPS = 1e-6
NEG_INF = -1e30
QK_SCALE = HEAD_DIM ** -0.5
LOG2E = 1.4426950408889634

LANES = 128
SUBLANES = 8
VMEM_LIMIT_BYTES = 56 * 1024 * 1024

O_CB = 0
O_CC = O_CB + CONV_WIDTH
O_CX = O_CC + CONV_WIDTH
O_RQ = O_CX + CONV_WIDTH
O_RK = O_RQ + RET_WIDTH
O_RV = O_RK + RET_WIDTH
O_GF = O_RV + RET_WIDTH
O_GB = O_GF + RET_WIDTH
O_AQ = O_GB + RET_WIDTH
O_AK = O_AQ + ATTN_WIDTH
O_AV = O_AK + KV_WIDTH
IN_WIDTH = O_AV + KV_WIDTH

M_SH1, M_SC1, M_G1, M_SH2, M_SC2, M_G2 = range(6)
M_CTX = 6
M_ROWS = 16

INPROJ_TILE = 544
FINAL_TILE = 256


def _params(*sem, vmem=VMEM_LIMIT_BYTES):
    return pltpu.CompilerParams(dimension_semantics=sem, vmem_limit_bytes=vmem)


def _dot(a, b):
    return jnp.dot(a, b, preferred_element_type=F32)


def _dot_nt(a, b):
    return lax.dot_general(a, b, (((1,), (1,)), ((), ())), preferred_element_type=F32)


def _silu(v):
    return v * jax.nn.sigmoid(v)


def _mod_kernel(c_ref, w_ref, b_ref, o_ref):
    s = _silu(c_ref[...])
    o_ref[0] = jnp.dot(s, w_ref[0], precision=HIGHEST, preferred_element_type=F32) + b_ref[0]


def _mod_vectors(c_rows, w_mod, b_mod):
    depth, d_model, width = w_mod.shape
    tn = 1536
    return pl.pallas_call(
        _mod_kernel,
        grid=(depth, width // tn),
        in_specs=[
            pl.BlockSpec((SUBLANES, d_model), lambda l, n: (0, 0)),
            pl.BlockSpec((1, d_model, tn), lambda l, n: (l, 0, n)),
            pl.BlockSpec((1, 1, tn), lambda l, n: (l, 0, n)),
        ],
        out_specs=pl.BlockSpec((1, SUBLANES, tn), lambda l, n: (l, 0, n)),
        out_shape=jax.ShapeDtypeStruct((depth, SUBLANES, width), F32),
        compiler_params=_params("parallel", "parallel"),
        name="mod_vectors",
    )(c_rows, w_mod, b_mod.reshape(depth, 1, width))


def _log_sigmoid(v):
    return -jnp.log(1.0 + jnp.exp(-v))


def _tables_kernel(lgl_ref, lgr_ref, qdec_ref, win_ref, cd_ref, dm_ref):
    pos = lax.broadcasted_iota(I32, (CHUNK, RET_WIDTH), 0).astype(F32)
    ri = (lax.broadcasted_iota(I32, (RET_HEADS * CHUNK, CHUNK), 0) & (CHUNK - 1)).astype(F32)
    rj = lax.broadcasted_iota(I32, (RET_HEADS * CHUNK, CHUNK), 1).astype(F32)
    for d in range(2):
        lg = _log_sigmoid(lgl_ref[0, d])
        lg1 = lg[0:1, :]
        if d == 0:
            qdec_ref[0, d] = jnp.exp(lg1 * (pos + 1.0))
            win_ref[0, d] = jnp.exp(lg1 * (CHUNK - 1.0 - pos))
            diff = ri - rj
        else:
            qdec_ref[0, d] = jnp.exp(lg1 * (CHUNK - pos))
            win_ref[0, d] = jnp.exp(lg1 * pos)
            diff = rj - ri
        cd_ref[0, d] = jnp.exp(lg * float(CHUNK))
        lr = _log_sigmoid(lgr_ref[0, d])
        dm_ref[0, d] = jnp.where(diff >= 0.0, jnp.exp(lr * jnp.maximum(diff, 0.0)), 0.0)


def _decay_tables(ret_decay_logit):
    depth = ret_decay_logit.shape[0]
    lg = ret_decay_logit.astype(F32)
    lgl = jnp.broadcast_to(jnp.repeat(lg, HEAD_DIM, axis=-1)[:, :, None, :], (depth, 2, SUBLANES, RET_WIDTH))
    lgr = jnp.broadcast_to(jnp.repeat(lg, CHUNK, axis=-1)[:, :, :, None], (depth, 2, RET_HEADS * CHUNK, CHUNK))
    return pl.pallas_call(
        _tables_kernel,
        grid=(depth,),
        in_specs=[
            pl.BlockSpec((1, 2, SUBLANES, RET_WIDTH), lambda l: (l, 0, 0, 0)),
            pl.BlockSpec((1, 2, RET_HEADS * CHUNK, CHUNK), lambda l: (l, 0, 0, 0)),
        ],
        out_specs=[
            pl.BlockSpec((1, 2, CHUNK, RET_WIDTH), lambda l: (l, 0, 0, 0)),
            pl.BlockSpec((1, 2, CHUNK, RET_WIDTH), lambda l: (l, 0, 0, 0)),
            pl.BlockSpec((1, 2, SUBLANES, RET_WIDTH), lambda l: (l, 0, 0, 0)),
            pl.BlockSpec((1, 2, RET_HEADS * CHUNK, CHUNK), lambda l: (l, 0, 0, 0)),
        ],
        out_shape=[
            jax.ShapeDtypeStruct((depth, 2, CHUNK, RET_WIDTH), F32),
            jax.ShapeDtypeStruct((depth, 2, CHUNK, RET_WIDTH), F32),
            jax.ShapeDtypeStruct((depth, 2, SUBLANES, RET_WIDTH), F32),
            jax.ShapeDtypeStruct((depth, 2, RET_HEADS * CHUNK, CHUNK), F32),
        ],
        compiler_params=_params("parallel"),
        name="decay_tables",
    )(lgl, lgr)


def _rope(v, cs, sn):
    lane = lax.broadcasted_iota(I32, (1, LANES), 1)
    first = (lane & 31) < 16
    outs = []
    for g in range(v.shape[1] // LANES):
        vg = v[:, g * LANES:(g + 1) * LANES]
        sw = jnp.where(first, pltpu.roll(vg, LANES - 16, 1), pltpu.roll(vg, 16, 1))
        outs.append(vg * cs + sw * sn)
    return outs[0] if len(outs) == 1 else jnp.concatenate(outs, axis=1)


def _swap_halves(v):
    return jnp.concatenate([v[:, HEAD_DIM:], v[:, :HEAD_DIM]], axis=1)


def _from_gather_layout(f_ref, tm):
    return jnp.concatenate(
        [f_ref[0, pl.ds(q, tm, stride=SUBLANES), :] for q in range(SUBLANES)], axis=1)


def _rms(v):
    return v * lax.rsqrt(jnp.mean(v * v, axis=-1, keepdims=True) + EPS)


def _mod_rows(mod_ref, tm, n_ctx):
    m = mod_ref[0]
    if n_ctx % tm == 0:
        is_ctx = pl.program_id(1) < n_ctx // tm
    else:
        is_ctx = pl.program_id(1) * tm + lax.broadcasted_iota(I32, (tm, 1), 0) < n_ctx

    def row(r):
        return jnp.where(is_ctx, m[M_CTX + r:M_CTX + r + 1, :], m[r:r + 1, :])

    return row


def _inproj_kernel(has_ffn, n_ctx, *refs):
    if has_ffn:
        x_ref, f_ref, mod_ref, g_ref, w_ref, cs_ref, sn_ref = refs[:7]
        outs = refs[7:]
        xo_ref = outs[0]
        outs = outs[1:]
    else:
        x_ref, mod_ref, g_ref, w_ref, cs_ref, sn_ref = refs[:6]
        outs = refs[6:]
    cz_ref, rqkv_ref, gates_ref, aq_ref, kv_ref = outs
    tm = x_ref.shape[1]
    row = _mod_rows(mod_ref, tm, n_ctx)

    x = x_ref[0]
    if has_ffn:
        x = x + row(M_G2) * _from_gather_layout(f_ref, tm)
        xo_ref[0] = x
    h = (_rms(x) * g_ref[...]) * (1.0 + row(M_SC1)) + row(M_SH1)
    h = h.astype(BF16)
    cs = cs_ref[...]
    sn = sn_ref[...]

    def proj(a, b):
        return _dot(h, w_ref[:, a:b])

    cz_ref[0, :, 0:CONV_WIDTH] = proj(O_CB, O_CC)
    cz_ref[0, :, CONV_WIDTH:] = proj(O_CC, O_CX) * proj(O_CX, O_RQ)
    rqkv_ref[0, :, 0:RET_WIDTH] = _rope(proj(O_RQ, O_RK), cs, sn).astype(BF16)
    rqkv_ref[0, :, RET_WIDTH:2 * RET_WIDTH] = _rope(proj(O_RK, O_RV) * QK_SCALE, cs, sn).astype(BF16)
    rqkv_ref[0, :, 2 * RET_WIDTH:] = proj(O_RV, O_GF).astype(BF16)
    gates_ref[0, :, 0:RET_WIDTH] = _silu(proj(O_GF, O_GB))
    gates_ref[0, :, RET_WIDTH:] = _silu(proj(O_GB, O_AQ))
    aq_ref[0] = (_rope(proj(O_AQ, O_AK), cs, sn) * (QK_SCALE * LOG2E)).astype(BF16)
    ak = _rope(proj(O_AK, O_AV), cs, sn)
    av = proj(O_AV, IN_WIDTH)
    kv_ref[0, :, 0:KV_WIDTH] = ak.astype(BF16)
    kv_ref[0, :, KV_WIDTH:2 * KV_WIDTH] = _swap_halves(ak).astype(BF16)
    kv_ref[0, :, 2 * KV_WIDTH:3 * KV_WIDTH] = av.astype(BF16)
    kv_ref[0, :, 3 * KV_WIDTH:] = _swap_halves(av).astype(BF16)


def _inproj(x, ffn, mrows, norm_g, w_in_bf, layer, rope_cs, rope_sn, n_ctx):
    b, tu, d = x.shape
    tm = INPROJ_TILE
    has_ffn = ffn is not None
    tok = lambda width: pl.BlockSpec((1, tm, width), lambda i, j: (i, j, 0))
    in_specs = [tok(d)]
    args = [x]
    if has_ffn:
        in_specs.append(pl.BlockSpec((1, tm * SUBLANES, LANES), lambda i, j: (i, j, 0)))
        args.append(ffn)
    in_specs += [
        pl.BlockSpec((1, M_ROWS, d), lambda i, j: (i, 0, 0)),
        pl.BlockSpec((1, d), lambda i, j: (0, 0)),
        pl.BlockSpec((None, d, IN_WIDTH), lambda i, j: (layer, 0, 0)),
        pl.BlockSpec((tm, LANES), lambda i, j: (j, 0)),
        pl.BlockSpec((tm, LANES), lambda i, j: (j, 0)),
    ]
    args += [mrows, norm_g.reshape(1, d), w_in_bf, rope_cs, rope_sn]
    widths = [(2 * CONV_WIDTH, F32), (3 * RET_WIDTH, BF16), (2 * RET_WIDTH, F32), (ATTN_WIDTH, BF16),
              (4 * KV_WIDTH, BF16)]
    out_specs = [tok(w) for w, _ in widths]
    out_shape = [jax.ShapeDtypeStruct((b, tu, w), dt) for w, dt in widths]
    if has_ffn:
        out_specs = [tok(d)] + out_specs
        out_shape = [jax.ShapeDtypeStruct((b, tu, d), F32)] + out_shape
    res = pl.pallas_call(
        functools.partial(_inproj_kernel, has_ffn, n_ctx),
        grid=(b, tu // tm),
        in_specs=in_specs,
        out_specs=out_specs,
        out_shape=out_shape,
        compiler_params=_params("parallel", "parallel"),
        name="inproj",
    )(*args)
    if has_ffn:
        return res[0], res[1:]
    return x, res


def _head_block_mask(n):
    r = lax.broadcasted_iota(I32, (n, n), 0) // HEAD_DIM
    c = lax.broadcasted_iota(I32, (n, n), 1) // HEAD_DIM
    return r == c


def _states_kernel(n_ctx_chunks, n_chunks, rqkv_ref, win_ref, cd_ref, sp_ref, s_scr):
    same_head = _head_block_mask(RET_WIDTH)
    s_scr[...] = jnp.zeros_like(s_scr)

    def body(i, carry):
        for d in range(2):
            if d == 0:
                c = i
            else:
                c = jnp.where(i < n_ctx_chunks, n_ctx_chunks - 1 - i, n_chunks + n_ctx_chunks - 1 - i)
            off = pl.multiple_of(c * CHUNK, CHUNK)
            kw = rqkv_ref[0, pl.ds(off, CHUNK), RET_WIDTH:2 * RET_WIDTH].astype(F32) * win_ref[d]
            v = rqkv_ref[0, pl.ds(off, CHUNK), 2 * RET_WIDTH:]
            u = _dot(kw.T.astype(BF16), v)
            s = s_scr[d]
            sp_ref[0, d, pl.ds(c, 1)] = s.astype(BF16)[None]
            s_scr[d] = s * cd_ref[d][0:1, :] + jnp.where(same_head, u, 0.0)
        return carry

    lax.fori_loop(0, n_chunks, body, 0)


def _ret_states(rqkv, win, cd, n_ctx):
    b, tu, _ = rqkv.shape
    w = RET_WIDTH
    n_chunks = tu // CHUNK
    return pl.pallas_call(
        functools.partial(_states_kernel, n_ctx // CHUNK, n_chunks),
        grid=(b,),
        in_specs=[
            pl.BlockSpec((1, tu, 3 * w), lambda i: (i, 0, 0)),
            pl.BlockSpec((2, CHUNK, w), lambda i: (0, 0, 0)),
            pl.BlockSpec((2, SUBLANES, w), lambda i: (0, 0, 0)),
        ],
        out_specs=pl.BlockSpec((1, 2, n_chunks, w, w), lambda i: (i, 0, 0, 0, 0)),
        out_shape=jax.ShapeDtypeStruct((b, 2, n_chunks, w, w), BF16),
        scratch_shapes=[pltpu.VMEM((2, w, w), F32)],
        compiler_params=_params("parallel"),
        name="ret_states",
    )(rqkv, win, cd)


def _group_mean(t, ones_bd):
    hi = t.astype(BF16)
    lo = (t - hi.astype(F32)).astype(BF16)
    s = _dot(jnp.concatenate([hi, lo], axis=0), ones_bd)
    n = t.shape[0]
    return (s[:n] + s[n:]) * (1.0 / HEAD_DIM)


def _mix_chunk(c, n_ctx_chunks, n_chunks, cz, z_last, z_first, cw, rqkv, gates, sp, dm_ref, qdec_ref,
               qa, kvs, sink_ref):
    is_lat = c >= n_ctx_chunks

    z = cz[:, CONV_WIDTH:]
    row = lax.broadcasted_iota(I32, (CHUNK, 1), 0)
    has_prev = jnp.logical_and(c != 0, c != n_ctx_chunks)
    has_next = jnp.logical_and(c != n_ctx_chunks - 1, c != n_chunks - 1)
    z_before = jnp.where(row == 0, jnp.where(has_prev, z_last, 0.0), pltpu.roll(z, 1, 0))
    z_after = jnp.where(row == CHUNK - 1, jnp.where(has_next, z_first, 0.0), pltpu.roll(z, CHUNK - 1, 0))
    conv = cz[:, :CONV_WIDTH] * (z_before * cw[0:1, :] + z * cw[1:2, :] + z_after * cw[2:3, :])

    q = rqkv[:, 0:RET_WIDTH]
    k = rqkv[:, RET_WIDTH:2 * RET_WIDTH]
    v = rqkv[:, 2 * RET_WIDTH:]
    lane_head = lax.broadcasted_iota(I32, (1, RET_WIDTH), 1) // HEAD_DIM
    qz = jnp.zeros_like(q)
    qs = jnp.concatenate([jnp.where(lane_head == hh, q, qz) for hh in range(RET_HEADS)], axis=0)
    scores = _dot_nt(qs, k)
    qf = q.astype(F32)
    outs = []
    for d in range(2):
        p = (scores * dm_ref[d]).astype(BF16)
        o_all = _dot(p, v)
        o = _dot((qf * qdec_ref[d]).astype(BF16), sp[d])
        for hh in range(RET_HEADS):
            o = o + jnp.where(lane_head == hh, o_all[hh * CHUNK:(hh + 1) * CHUNK, :], 0.0)
        outs.append(o)
    o2 = jnp.concatenate(outs, axis=0)
    ones_bd = jnp.where(_head_block_mask(RET_WIDTH), 1.0, 0.0).astype(BF16)
    dl = o2 - _group_mean(o2, ones_bd)
    on = dl * lax.rsqrt(_group_mean(dl * dl, ones_bd) + EPS)
    ret = on[:CHUNK] * gates[:, 0:RET_WIDTH] + on[CHUNK:] * gates[:, RET_WIDTH:]

    keys = tuple(jnp.concatenate([t[:, i * KV_WIDTH:(i + 1) * KV_WIDTH] for t in kvs], axis=0) for i in (0, 1))
    vals = tuple(jnp.concatenate([t[:, i * KV_WIDTH:(i + 1) * KV_WIDTH] for t in kvs], axis=0) for i in (2, 3))
    n_ctx = kvs[3].shape[0]
    rows = 2 * CHUNK
    iq = lax.broadcasted_iota(I32, (rows, CHUNK), 0) & (CHUNK - 1)
    ik = lax.broadcasted_iota(I32, (rows, CHUNK), 1)
    ok_prev = jnp.logical_and(jnp.logical_and(is_lat, c - 1 >= n_ctx_chunks), ik >= iq)
    ok_cur = jnp.logical_and(is_lat, ik >= 0)
    ok_next = jnp.logical_and(jnp.logical_and(is_lat, c + 1 <= n_chunks - 1), ik <= iq)
    ok_ctx = jnp.ones((rows, n_ctx), jnp.bool_)
    valid = jnp.concatenate([ok_prev, ok_cur, ok_next, ok_ctx], axis=1)
    half = lax.broadcasted_iota(I32, (1, LANES), 1) // HEAD_DIM
    qa_z = jnp.zeros((CHUNK, LANES), BF16)
    cols = [None] * (ATTN_HEADS // 2)
    for hk in range(ATTN_KV_HEADS):
        outs = []
        for par in range(2):
            ja, jb = 2 * hk, 2 * hk + 1
            qst = jnp.concatenate([
                jnp.where(half == par, qa[:, ja * LANES:(ja + 1) * LANES], qa_z),
                jnp.where(half == par, qa[:, jb * LANES:(jb + 1) * LANES], qa_z)], axis=0)
            sel = 0 if par == hk else 1
            s = jnp.where(valid, _dot_nt(qst, keys[sel]), NEG_INF)
            ha, hb = ATTN_GROUP * hk + par, ATTN_GROUP * hk + par + 2
            snk = jnp.concatenate([sink_ref[ha * CHUNK:(ha + 1) * CHUNK, 0:1],
                                   sink_ref[hb * CHUNK:(hb + 1) * CHUNK, 0:1]], axis=0) * LOG2E
            mx = jnp.maximum(jnp.max(s, axis=1, keepdims=True), snk)
            e = jnp.exp2(s - mx)
            den = jnp.sum(e, axis=1, keepdims=True) + jnp.exp2(snk - mx)
            outs.append(_dot(e.astype(BF16), vals[sel]) * (1.0 / den))
        cols[2 * hk] = jnp.where(half == 0, outs[0][:CHUNK], outs[1][:CHUNK])
        cols[2 * hk + 1] = jnp.where(half == 0, outs[0][CHUNK:], outs[1][CHUNK:])

    return jnp.concatenate([conv, ret] + cols, axis=1)


MIX_CHUNKS = 2


def _mixer_kernel(n_ctx_chunks, n_chunks, n_exp,
                  cz_ref, czp_ref, czn_ref, cw_ref, rqkv_ref, gates_ref, sp_ref, dm_ref, qdec_ref,
                  aq_ref, kvp_ref, kvc_ref, kvn_ref, kvx_ref, sink_ref,
                  x_ref, mod_ref, g_ref, w_ref, wrh_ref, wrl_ref, xo_ref, hg_ref, aff_ref):
    j = pl.program_id(1)
    cw = cw_ref[...]
    kvx = kvx_ref[0]
    mixes = []
    for s in range(MIX_CHUNKS):
        lo, hi = s * CHUNK, (s + 1) * CHUNK
        if s == 0:
            z_last = czp_ref[0, SUBLANES - 1:SUBLANES, CONV_WIDTH:]
            kv_prev = kvp_ref[0]
        else:
            z_last = cz_ref[0, lo - 1:lo, CONV_WIDTH:]
            kv_prev = kvc_ref[0, lo - CHUNK:lo, :]
        if s == MIX_CHUNKS - 1:
            z_first = czn_ref[0, 0:1, CONV_WIDTH:]
            kv_next = kvn_ref[0]
        else:
            z_first = cz_ref[0, hi:hi + 1, CONV_WIDTH:]
            kv_next = kvc_ref[0, hi:hi + CHUNK, :]
        mix = _mix_chunk(j * MIX_CHUNKS + s, n_ctx_chunks, n_chunks, cz_ref[0, lo:hi, :], z_last, z_first, cw,
                         rqkv_ref[0, lo:hi, :], gates_ref[0, lo:hi, :], (sp_ref[0, 0, s], sp_ref[0, 1, s]),
                         dm_ref, qdec_ref, aq_ref[0, lo:hi, :], [kv_prev, kvc_ref[0, lo:hi, :], kv_next, kvx],
                         sink_ref)
        mixes.append(mix.astype(BF16))
    _outproj_tail(n_ctx_chunks * CHUNK, n_exp, jnp.concatenate(mixes, axis=0), x_ref, mod_ref, g_ref, w_ref,
                  wrh_ref, wrl_ref, xo_ref, hg_ref, aff_ref)


def _mixer(proj, sp, tabs, conv_w, sink_rows, x, mrows, norm_g, w_out_bf, layer, w_router, n_ctx):
    cz, rqkv, gates, aq, kv = proj
    dm, qdec = tabs
    b, tu, _ = cz.shape
    d = x.shape[2]
    n_exp = w_router.shape[1]
    wr = jnp.pad(w_router, ((0, 0), (0, LANES - n_exp)))
    wr_hi = wr.astype(BF16)
    wr_lo = (wr - wr_hi.astype(F32)).astype(BF16)
    n_chunks = tu // CHUNK
    mc = MIX_CHUNKS
    rows = mc * CHUNK
    assert n_chunks % mc == 0 and (n_ctx // CHUNK) % mc == 0
    last = n_chunks - 1
    per8 = rows // SUBLANES
    cur = lambda w: pl.BlockSpec((1, rows, w), lambda i, j: (i, j, 0))
    full = lambda shape: pl.BlockSpec(shape, lambda i, j: (0,) * len(shape))
    wcz, wkv = cz.shape[2], kv.shape[2]
    in_specs = [
        cur(wcz),
        pl.BlockSpec((1, SUBLANES, wcz), lambda i, j: (i, jnp.maximum(j * per8 - 1, 0), 0)),
        pl.BlockSpec((1, SUBLANES, wcz), lambda i, j: (i, jnp.minimum((j + 1) * per8, tu // SUBLANES - 1), 0)),
        full(conv_w.shape),
        cur(rqkv.shape[2]), cur(gates.shape[2]),
        pl.BlockSpec((1, 2, mc, RET_WIDTH, RET_WIDTH), lambda i, j: (i, 0, j, 0, 0)),
        full(dm.shape), full(qdec.shape),
        cur(ATTN_WIDTH),
        pl.BlockSpec((1, CHUNK, wkv), lambda i, j: (i, jnp.maximum(j * mc - 1, 0), 0)),
        cur(wkv),
        pl.BlockSpec((1, CHUNK, wkv), lambda i, j: (i, jnp.minimum((j + 1) * mc, last), 0)),
        pl.BlockSpec((1, n_ctx, wkv), lambda i, j: (i, 0, 0)),
        full(sink_rows.shape),
        cur(d),
        pl.BlockSpec((1, M_ROWS, d), lambda i, j: (i, 0, 0)),
        full((1, d)),
        pl.BlockSpec((None,) + w_out_bf.shape[1:], lambda i, j: (layer, 0, 0)),
        full((d, LANES)), full((d, LANES)),
    ]
    args = [cz, cz, cz, conv_w, rqkv, gates, sp, dm, qdec, aq, kv, kv, kv, kv, sink_rows,
            x, mrows, norm_g.reshape(1, d), w_out_bf, wr_hi, wr_lo]
    return pl.pallas_call(
        functools.partial(_mixer_kernel, n_ctx // CHUNK, n_chunks, n_exp),
        grid=(b, n_chunks // mc),
        in_specs=in_specs,
        out_specs=[
            cur(d),
            pl.BlockSpec((1, rows * SUBLANES, LANES), lambda i, j: (i, j, 0)),
            cur(LANES),
        ],
        out_shape=[
            jax.ShapeDtypeStruct((b, tu, d), F32),
            jax.ShapeDtypeStruct((b, tu * SUBLANES, LANES), F32),
            jax.ShapeDtypeStruct((b, tu, LANES), F32),
        ],
        compiler_params=_params("parallel", "parallel"),
        name="mixer",
    )(*args)


def _outproj_tail(n_ctx, n_exp, mix, x_ref, mod_ref, g_ref, w_ref, wrh_ref, wrl_ref, xo_ref, hg_ref, aff_ref):
    tm = x_ref.shape[1]
    row = _mod_rows(mod_ref, tm, n_ctx)

    x = x_ref[0] + row(M_G1) * _dot(mix, w_ref[...])
    xo_ref[0] = x
    h = (_rms(x) * g_ref[...]) * (1.0 + row(M_SC2)) + row(M_SH2)
    for q in range(SUBLANES):
        hg_ref[0, pl.ds(q, tm, stride=SUBLANES), :] = h[:, q * LANES:(q + 1) * LANES]
    h_hi = h.astype(BF16)
    h_lo = (h - h_hi.astype(F32)).astype(BF16)
    logits = _dot(h_hi, wrh_ref[...]) + (_dot(h_lo, wrh_ref[...]) + _dot(h_hi, wrl_ref[...]))
    lane = lax.broadcasted_iota(I32, (1, LANES), 1)
    logits = jnp.where(lane < n_exp, logits, NEG_INF)
    e = jnp.exp(logits - jnp.max(logits, axis=1, keepdims=True))
    aff_ref[0] = e / jnp.sum(e, axis=1, keepdims=True)


ROUTE_TABLES = 4


def _route_stream_tables(v, k, tab_scr, ce_scr, cnt_scr, stream, thr):
    n_exp, n = v.shape
    lane = lax.broadcasted_iota(I32, (1, LANES), 1)
    tri_r = lax.broadcasted_iota(I32, (LANES, LANES), 0)
    tri_c = lax.broadcasted_iota(I32, (LANES, LANES), 1)
    upper = jnp.where(tri_r <= tri_c, 1.0, 0.0).astype(BF16)
    above = pltpu.bitcast(thr + 1, F32)
    floor = pltpu.bitcast(thr, F32)
    n_gt = jnp.sum(jnp.where(v >= above, 1, 0), axis=1, keepdims=True)
    need = (k - n_gt).astype(F32)
    seen_eq = jnp.zeros((n_exp, 1), F32)
    counts = jnp.zeros((n_exp, LANES), F32)
    for j in range(n // LANES):
        vb = v[:, j * LANES:(j + 1) * LANES]
        gt = vb >= above
        eq = jnp.logical_and(vb >= floor, vb < above)
        both = jnp.concatenate([jnp.where(gt, 1.0, 0.0), jnp.where(eq, 1.0, 0.0)], axis=0).astype(BF16)
        pref = _dot(both, upper)
        rank_eq = pref[n_exp:] + seen_eq
        inc = pref[:n_exp] + jnp.minimum(rank_eq, need) - jnp.minimum(seen_eq, need)
        sel = jnp.logical_or(gt, jnp.logical_and(eq, rank_eq <= need))
        seen_eq = rank_eq[:, LANES - 1:LANES]
        counts = counts + jnp.where(lane == j, inc[:, LANES - 1:LANES], 0.0)
        a0 = jnp.where(sel, vb, 0.0)
        t0 = a0.astype(BF16).astype(F32)
        t1 = (a0 - t0).astype(BF16).astype(F32)
        t2 = a0 - t0 - t1
        for t, val in enumerate((inc, t0, t1, t2)):
            tab_scr[stream * ROUTE_TABLES + t, pl.ds(j, n_exp, stride=LANES), :] = val
    through = _dot(counts.astype(BF16), upper)
    for e in range(n_exp):
        ce_scr[stream, e] = through[e:e + 1, :]
        cnt_scr[stream, e] = counts[e:e + 1, :]


def _route_slots(e, stream, k, lo, tab_scr, ce_scr, cnt_scr):
    lane_f = lax.broadcasted_iota(I32, (1, LANES), 1).astype(F32)
    slot = lax.broadcasted_iota(I32, (k, LANES), 0).astype(F32)
    ones_rows = jnp.ones((LANES, LANES), BF16)
    cnt_rows = jnp.broadcast_to(cnt_scr[stream, e], (LANES, LANES)).astype(BF16)
    rows_e = pl.ds(pl.multiple_of(e * LANES, LANES), LANES)
    tab = jnp.concatenate([tab_scr[stream * ROUTE_TABLES + t, rows_e, :] for t in range(ROUTE_TABLES)],
                          axis=1).astype(BF16)
    before = jnp.where(ce_scr[stream, e] <= slot, 1.0, 0.0).astype(BF16)
    blk = _dot_nt(before, ones_rows)
    base = _dot_nt(before, cnt_rows)
    row = _dot(jnp.where(lane_f == blk, 1.0, 0.0).astype(BF16), tab)
    inc = row[:, :LANES]
    aff = row[:, LANES:2 * LANES] + row[:, 2 * LANES:3 * LANES] + row[:, 3 * LANES:]
    local = slot - base
    pos = _dot_nt(jnp.where(inc <= local, 1.0, 0.0).astype(BF16), ones_rows)
    tok = blk * LANES + pos + float(lo)
    gate = jnp.sum(jnp.where(inc == local + 1.0, aff, 0.0), axis=1, keepdims=True)
    return tok, gate


def _route_kernel(n_exp, streams, aff_ref, idx_ref, gate_ref, tab_scr, ce_scr, cnt_scr):
    a = aff_ref[0].T[:n_exp, :]
    tab_scr[...] = jnp.zeros_like(tab_scr)
    vs = [a[:, lo:lo + n] for lo, n, _, _ in streams]

    def search(i, ts):
        out = []
        for t, v, (_, _, k, _) in zip(ts, vs, streams):
            cand = t | jnp.left_shift(jnp.int32(1), 30 - i)
            cnt = jnp.sum(jnp.where(v >= pltpu.bitcast(cand, F32), 1, 0), axis=1, keepdims=True)
            out.append(jnp.where(cnt >= k, cand, t))
        return tuple(out)

    thrs = lax.fori_loop(0, 31, search, tuple(jnp.zeros((n_exp, 1), I32) for _ in streams))
    for st, (v, (_, _, k, _)) in enumerate(zip(vs, streams)):
        _route_stream_tables(v, k, tab_scr, ce_scr, cnt_scr, st, thrs[st])

    slots = idx_ref.shape[3]

    def per_expert(e, carry):
        toks = []
        for st, (lo, _, k, slot0) in enumerate(streams):
            assert slot0 == sum(t.shape[0] for t in toks)
            tok, gate = _route_slots(e, st, k, lo, tab_scr, ce_scr, cnt_scr)
            toks.append(tok)
            gate_ref[0, pl.ds(e, 1), slot0:slot0 + k, :] = gate[None]
        toks.append(jnp.zeros((-slots % LANES, LANES), F32))
        row = jnp.concatenate(toks, axis=0).T[0:1, :slots]
        idx_ref[0, pl.ds(e, 1)] = row.astype(I32)[None]
        return carry

    lax.fori_loop(0, n_exp, per_expert, 0)


def _route(aff, n_exp, n_ctx):
    b, tu, _ = aff.shape
    n_lat = tu - n_ctx
    cap_l = CAPACITY_FACTOR * n_lat // n_exp
    cap_c = CAPACITY_FACTOR * n_ctx // n_exp
    slots = cap_l + cap_c
    assert n_lat // LANES <= LANES and n_ctx // LANES <= LANES
    streams = ((n_ctx, n_lat, cap_l, 0), (0, n_ctx, cap_c, cap_l))
    return pl.pallas_call(
        functools.partial(_route_kernel, n_exp, streams),
        grid=(b,),
        in_specs=[pl.BlockSpec((1, tu, LANES), lambda i: (i, 0, 0))],
        out_specs=[pl.BlockSpec((1, n_exp, 1, slots), lambda i: (i, 0, 0, 0)),
                   pl.BlockSpec((1, n_exp, slots, 1), lambda i: (i, 0, 0, 0))],
        out_shape=[jax.ShapeDtypeStruct((b, n_exp, 1, slots), I32),
                   jax.ShapeDtypeStruct((b, n_exp, slots, 1), F32)],
        scratch_shapes=[pltpu.VMEM((len(streams) * ROUTE_TABLES, n_exp * LANES, LANES), F32),
                        pltpu.VMEM((len(streams), n_exp, 1, LANES), F32),
                        pltpu.VMEM((len(streams), n_exp, 1, LANES), F32)],
        compiler_params=_params("parallel"),
        name="route",
    )(aff)


def _slot_pitch(slots):
    return slots + SUBLANES


def _zero_slot_padding(ref, lead, slots, pitch):
    for q in range(SUBLANES):
        ref[lead + (pl.ds(q * pitch + slots, pitch - slots), slice(None))] = jnp.zeros((pitch - slots, LANES), ref.dtype)


BF16_ROWS = 2 * SUBLANES


def _slot_pitch_bf16(slots):
    return -(-slots // BF16_ROWS) * BF16_ROWS + BF16_ROWS


def _gather_kernel(slots, pitch, pitch_out, idx_ref, h_ref, o_ref, tile_scr):
    for mi in range(slots):
        r = pl.multiple_of(idx_ref[0, 0, mi] * SUBLANES, SUBLANES)
        tile_scr[pl.ds(mi, SUBLANES, stride=pitch), :] = h_ref[0, pl.ds(r, SUBLANES), :]
    _zero_slot_padding(o_ref, (0, 0), slots, pitch_out)
    for q in range(SUBLANES):
        o_ref[0, 0, q * pitch_out:q * pitch_out + slots, :] = tile_scr[q * pitch:q * pitch + slots, :].astype(BF16)


def _gather(hg, idx_rows, n_exp, slots):
    b = hg.shape[0]
    pitch = _slot_pitch(slots)
    pitch_out = _slot_pitch_bf16(slots)
    return pl.pallas_call(
        functools.partial(_gather_kernel, slots, pitch, pitch_out),
        grid=(b, n_exp),
        in_specs=[
            pl.BlockSpec((1, 1, slots), lambda i, e: (i * n_exp + e, 0, 0), memory_space=pltpu.SMEM),
            pl.BlockSpec((1,) + hg.shape[1:], lambda i, e: (i, 0, 0)),
        ],
        out_specs=pl.BlockSpec((1, 1, SUBLANES * pitch_out, LANES), lambda i, e: (i, e, 0, 0)),
        out_shape=jax.ShapeDtypeStruct((b, n_exp, SUBLANES * pitch_out, LANES), BF16),
        scratch_shapes=[pltpu.VMEM((SUBLANES * pitch, LANES), F32)],
        compiler_params=_params("parallel", "arbitrary"),
        name="gather",
    )(idx_rows, hg)


FFN_SAMPLES = 2
FFN_VMEM_LIMIT_BYTES = 60 * 1024 * 1024


def _ffn_kernel(slots, pitch_in, pitch, xs_ref, wg0_ref, wg1_ref, wu0_ref, wu1_ref, wd0_ref, wd1_ref, gate_ref, y_ref,
                wg_bf, wu_bf, wd_bf):
    @pl.when(pl.program_id(1) == 0)
    def _():
        half = wg_bf.shape[0] // 2
        for dst, lo_ref, hi_ref in ((wg_bf, wg0_ref, wg1_ref), (wu_bf, wu0_ref, wu1_ref), (wd_bf, wd0_ref, wd1_ref)):
            dst[:half, :] = lo_ref[0].astype(BF16)
            dst[half:, :] = hi_ref[0].astype(BF16)

    n = xs_ref.shape[0]
    x = jnp.concatenate(
        [jnp.concatenate([xs_ref[s, 0, q * pitch_in:q * pitch_in + slots, :] for q in range(SUBLANES)], axis=1)
         for s in range(n)], axis=0)
    gate = jnp.concatenate([gate_ref[s, 0] for s in range(n)], axis=0)
    a = _dot(x, wg_bf[...])
    u = _dot(x, wu_bf[...])
    y = _dot((_silu(a) * u).astype(BF16), wd_bf[...]) * gate
    for s in range(n):
        _zero_slot_padding(y_ref, (s, 0), slots, pitch)
        for q in range(SUBLANES):
            y_ref[s, 0, q * pitch:q * pitch + slots, :] = y[s * slots:(s + 1) * slots, q * LANES:(q + 1) * LANES]


def _ffn(xs, w_gate, w_up, w_down, layer, gate, slots):
    b, n_exp = xs.shape[:2]
    pitch = _slot_pitch(slots)
    pitch_in = _slot_pitch_bf16(slots)
    d, f = w_gate.shape[2:]
    ns = FFN_SAMPLES
    steps = b // ns
    assert steps == 2 and b % ns == 0
    slot_tile = lambda p: pl.BlockSpec((ns, 1, SUBLANES * p, LANES), lambda e, i: (i, e, 0, 0))

    def half(rows, cols, q):
        return pl.BlockSpec((None, 1, rows // 2, cols),
                            lambda e, i: (layer, jnp.minimum(e + (i > q).astype(jnp.int32), n_exp - 1), q, 0))

    return pl.pallas_call(
        functools.partial(_ffn_kernel, slots, pitch_in, pitch),
        grid=(n_exp, steps),
        in_specs=[
            slot_tile(pitch_in),
            half(d, f, 0), half(d, f, 1), half(d, f, 0), half(d, f, 1), half(f, d, 0), half(f, d, 1),
            pl.BlockSpec((ns, 1, slots, 1), lambda e, i: (i, e, 0, 0)),
        ],
        out_specs=slot_tile(pitch),
        out_shape=jax.ShapeDtypeStruct((b, n_exp, SUBLANES * pitch, LANES), F32),
        scratch_shapes=[pltpu.VMEM((d, f), BF16), pltpu.VMEM((d, f), BF16), pltpu.VMEM((f, d), BF16)],
        compiler_params=_params("arbitrary", "arbitrary", vmem=FFN_VMEM_LIMIT_BYTES),
        name="ffn",
    )(xs, w_gate, w_gate, w_up, w_up, w_down, w_down, gate)


SCATTER_BATCH = 16


def _scatter_kernel(slots, pitch, idx_ref, y_ref, o_ref):
    @pl.when(pl.program_id(1) == 0)
    def _():
        o_ref[...] = jnp.zeros_like(o_ref)

    for m0 in range(0, slots, SCATTER_BATCH):
        rows = [pl.multiple_of(idx_ref[0, 0, m0 + u] * SUBLANES, SUBLANES) for u in range(SCATTER_BATCH)]
        vals = [o_ref[0, pl.ds(rows[u], SUBLANES), :] + y_ref[0, 0, pl.ds(m0 + u, SUBLANES, stride=pitch), :]
                for u in range(SCATTER_BATCH)]
        for u in range(SCATTER_BATCH):
            o_ref[0, pl.ds(rows[u], SUBLANES), :] = vals[u]


def _scatter(y, idx_rows, tu, slots):
    b, n_exp = y.shape[:2]
    pitch = _slot_pitch(slots)
    return pl.pallas_call(
        functools.partial(_scatter_kernel, slots, pitch),
        grid=(b, n_exp),
        in_specs=[
            pl.BlockSpec((1, 1, slots), lambda i, e: (i * n_exp + e, 0, 0), memory_space=pltpu.SMEM),
            pl.BlockSpec((1, 1, SUBLANES * pitch, LANES), lambda i, e: (i, e, 0, 0)),
        ],
        out_specs=pl.BlockSpec((1, tu * SUBLANES, LANES), lambda i, e: (i, 0, 0)),
        out_shape=jax.ShapeDtypeStruct((b, tu * SUBLANES, LANES), F32),
        compiler_params=_params("parallel", "arbitrary"),
        name="scatter",
    )(idx_rows, y)


def _final_kernel(x_ref, f_ref, mod_ref, g_ref, o_ref):
    tm = x_ref.shape[1]
    x = x_ref[0] + mod_ref[0][M_G2:M_G2 + 1, :] * _from_gather_layout(f_ref, tm)
    o_ref[0] = _rms(x) * g_ref[...]


def _final(x, ffn, mrows, final_g, n_ctx):
    b, tu, d = x.shape
    tm = FINAL_TILE
    skip = n_ctx // tm
    return pl.pallas_call(
        _final_kernel,
        grid=(b, (tu - n_ctx) // tm),
        in_specs=[
            pl.BlockSpec((1, tm, d), lambda i, j: (i, j + skip, 0)),
            pl.BlockSpec((1, tm * SUBLANES, LANES), lambda i, j: (i, j + skip, 0)),
            pl.BlockSpec((1, M_ROWS, d), lambda i, j: (i, 0, 0)),
            pl.BlockSpec((1, d), lambda i, j: (0, 0)),
        ],
        out_specs=pl.BlockSpec((1, tm, d), lambda i, j: (i, j, 0)),
        out_shape=jax.ShapeDtypeStruct((b, tu - n_ctx, d), F32),
        compiler_params=_params("parallel", "parallel"),
        name="final_norm",
    )(x, ffn, mrows, final_g.reshape(1, d))


def _rope_tables(n_lat, n_ctx):
    rows = n_lat // GRID_W
    rowp = jnp.repeat(jnp.arange(rows, dtype=F32), GRID_W)
    colp = jnp.tile(jnp.arange(GRID_W, dtype=F32), rows)
    axis_dim = HEAD_DIM // 2
    inv_freq = ROPE_BASE ** (-jnp.arange(0, axis_dim, 2, dtype=F32) / axis_dim)
    ar = rowp[:, None] * inv_freq
    ac = colp[:, None] * inv_freq
    cs = jnp.concatenate([jnp.cos(ar), jnp.cos(ar), jnp.cos(ac), jnp.cos(ac)], axis=1)
    sn = jnp.concatenate([-jnp.sin(ar), jnp.sin(ar), -jnp.sin(ac), jnp.sin(ac)], axis=1)
    reps = LANES // HEAD_DIM
    cs = jnp.concatenate([jnp.ones((n_ctx, LANES), F32), jnp.tile(cs, (1, reps))], axis=0)
    sn = jnp.concatenate([jnp.zeros((n_ctx, LANES), F32), jnp.tile(sn, (1, reps))], axis=0)
    return cs, sn


def kernel(x, c, ctx, c_ctx, w_mod, b_mod, norm1_g, norm2_g, w_in, conv_w, ret_decay_logit, attn_sink,
           w_out, w_router, w_gate, w_up, w_down, final_g):
    b, n_lat, d = x.shape
    n_ctx = ctx.shape[1]
    depth = w_in.shape[0]
    n_exp = w_router.shape[2]
    tu = n_ctx + n_lat
    assert w_in.shape[2] == IN_WIDTH and tu % INPROJ_TILE == 0
    assert n_ctx % FINAL_TILE == 0 and n_lat % FINAL_TILE == 0
    assert b + 1 <= SUBLANES and n_exp == N_EXPERTS

    c_rows = jnp.concatenate([c, c_ctx[None], jnp.zeros((SUBLANES - b - 1, d), F32)], axis=0)
    mods = _mod_vectors(c_rows, w_mod, b_mod).reshape(depth, SUBLANES, 6, d)
    qdec, win, cd, dm = _decay_tables(ret_decay_logit)
    rope_cs, rope_sn = _rope_tables(n_lat, n_ctx)
    sink_rows = jnp.broadcast_to(jnp.repeat(attn_sink.astype(F32), CHUNK, axis=1)[:, :, None],
                                 (depth, ATTN_HEADS * CHUNK, LANES))
    w_in_bf = w_in.astype(BF16)
    w_out_bf = w_out.astype(BF16)

    xu = jnp.concatenate([ctx, x], axis=1)
    ffn = None
    mrows = None
    cap = CAPACITY_FACTOR * n_lat // n_exp + CAPACITY_FACTOR * n_ctx // n_exp
    for l in range(depth):
        prev_mrows = mrows
        mrows = jnp.concatenate([mods[l, :b], jnp.broadcast_to(mods[l, b][None], (b, 6, d)),
                                 jnp.zeros((b, M_ROWS - 12, d), F32)], axis=1)
        xu, proj = _inproj(xu, ffn, mrows if ffn is None else _with_prev_g2(mrows, prev_mrows),
                           norm1_g[l], w_in_bf, l, rope_cs, rope_sn, n_ctx)
        sp = _ret_states(proj[1], win[l], cd[l], n_ctx)
        xu, hg, aff = _mixer(proj, sp, (dm[l], qdec[l]), conv_w[l], sink_rows[l],
                             xu, mrows, norm2_g[l], w_out_bf, l, w_router[l], n_ctx)
        idx, gate = _route(aff, n_exp, n_ctx)
        idx_rows = idx.reshape(b * n_exp, 1, cap)
        xs = _gather(hg, idx_rows, n_exp, cap)
        y = _ffn(xs, w_gate, w_up, w_down, l, gate, cap)
        ffn = _scatter(y, idx_rows, tu, cap)
    return _final(xu, ffn, mrows, final_g, n_ctx)


def _with_prev_g2(mrows, prev_mrows):
    out = mrows.at[:, M_G2].set(prev_mrows[:, M_G2])
    return out.at[:, M_CTX + M_G2].set(prev_mrows[:, M_CTX + M_G2])
```

```python
import functools

import jax
import jax.numpy as jnp
from jax import lax
from jax.experimental import pallas as pl
from jax.experimental.pallas import tpu as pltpu

F32 = jnp.float32
BF16 = jnp.bfloat16
I32 = jnp.int32
HIGHEST = lax.Precision.HIGHEST

HEAD_DIM = 64
CONV_WIDTH = 256
RET_HEADS = 4
RET_WIDTH = RET_HEADS * HEAD_DIM
ATTN_HEADS = 8
ATTN_KV_HEADS = 2
ATTN_GROUP = ATTN_HEADS // ATTN_KV_HEADS
ATTN_WIDTH = ATTN_HEADS * HEAD_DIM
KV_WIDTH = ATTN_KV_HEADS * HEAD_DIM
CHUNK = 128
GRID_W = 64
N_EXPERTS = 16
CAPACITY_FACTOR = 2
ROPE_BASE = 10000.0
EPS = 1e-6
NEG_INF = -1e30
QK_SCALE = HEAD_DIM ** -0.5
LOG2E = 1.4426950408889634

LANES = 128
SUBLANES = 8
VMEM_LIMIT_BYTES = 56 * 1024 * 1024

O_CB = 0
O_CC = O_CB + CONV_WIDTH
O_CX = O_CC + CONV_WIDTH
O_RQ = O_CX + CONV_WIDTH
O_RK = O_RQ + RET_WIDTH
O_RV = O_RK + RET_WIDTH
O_GF = O_RV + RET_WIDTH
O_GB = O_GF + RET_WIDTH
O_AQ = O_GB + RET_WIDTH
O_AK = O_AQ + ATTN_WIDTH
O_AV = O_AK + KV_WIDTH
IN_WIDTH = O_AV + KV_WIDTH

M_SH1, M_SC1, M_G1, M_SH2, M_SC2, M_G2 = range(6)
M_CTX = 6
M_ROWS = 16

INPROJ_TILE = 544
FINAL_TILE = 256


def _params(*sem, vmem=VMEM_LIMIT_BYTES):
    return pltpu.CompilerParams(dimension_semantics=sem, vmem_limit_bytes=vmem)


def _dot(a, b):
    return jnp.dot(a, b, preferred_element_type=F32)


def _dot_nt(a, b):
    return lax.dot_general(a, b, (((1,), (1,)), ((), ())), preferred_element_type=F32)


def _silu(v):
    return v * jax.nn.sigmoid(v)


def _mod_kernel(c_ref, w_ref, b_ref, o_ref):
    s = _silu(c_ref[...])
    o_ref[0] = jnp.dot(s, w_ref[0], precision=HIGHEST, preferred_element_type=F32) + b_ref[0]


def _mod_vectors(c_rows, w_mod, b_mod):
    depth, d_model, width = w_mod.shape
    tn = 1536
    return pl.pallas_call(
        _mod_kernel,
        grid=(depth, width // tn),
        in_specs=[
            pl.BlockSpec((SUBLANES, d_model), lambda l, n: (0, 0)),
            pl.BlockSpec((1, d_model, tn), lambda l, n: (l, 0, n)),
            pl.BlockSpec((1, 1, tn), lambda l, n: (l, 0, n)),
        ],
        out_specs=pl.BlockSpec((1, SUBLANES, tn), lambda l, n: (l, 0, n)),
        out_shape=jax.ShapeDtypeStruct((depth, SUBLANES, width), F32),
        compiler_params=_params("parallel", "parallel"),
        name="mod_vectors",
    )(c_rows, w_mod, b_mod.reshape(depth, 1, width))


def _log_sigmoid(v):
    return -jnp.log(1.0 + jnp.exp(-v))


def _tables_kernel(lgl_ref, lgr_ref, qdec_ref, win_ref, cd_ref, dm_ref):
    pos = lax.broadcasted_iota(I32, (CHUNK, RET_WIDTH), 0).astype(F32)
    ri = lax.broadcasted_iota(I32, (CHUNK, RET_HEADS * CHUNK), 0).astype(F32)
    rj = (lax.broadcasted_iota(I32, (CHUNK, RET_HEADS * CHUNK), 1) & (CHUNK - 1)).astype(F32)
    for d in range(2):
        lg = _log_sigmoid(lgl_ref[0, d])
        lg1 = lg[0:1, :]
        if d == 0:
            qdec_ref[0, d] = jnp.exp(lg1 * (pos + 1.0))
            win_ref[0, d] = jnp.exp(lg1 * (CHUNK - 1.0 - pos))
            diff = ri - rj
        else:
            qdec_ref[0, d] = jnp.exp(lg1 * (CHUNK - pos))
            win_ref[0, d] = jnp.exp(lg1 * pos)
            diff = rj - ri
        cd_ref[0, d] = jnp.exp(lg * float(CHUNK))
        lr = _log_sigmoid(lgr_ref[0, d])
        dm_ref[0, d] = jnp.where(diff >= 0.0, jnp.exp(lr * jnp.maximum(diff, 0.0)), 0.0)


def _decay_tables(ret_decay_logit):
    depth = ret_decay_logit.shape[0]
    lg = ret_decay_logit.astype(F32)
    lgl = jnp.broadcast_to(jnp.repeat(lg, HEAD_DIM, axis=-1)[:, :, None, :], (depth, 2, SUBLANES, RET_WIDTH))
    lgr = jnp.broadcast_to(jnp.repeat(lg, CHUNK, axis=-1)[:, :, None, :], (depth, 2, CHUNK, RET_HEADS * CHUNK))
    return pl.pallas_call(
        _tables_kernel,
        grid=(depth,),
        in_specs=[
            pl.BlockSpec((1, 2, SUBLANES, RET_WIDTH), lambda l: (l, 0, 0, 0)),
            pl.BlockSpec((1, 2, CHUNK, RET_HEADS * CHUNK), lambda l: (l, 0, 0, 0)),
        ],
        out_specs=[
            pl.BlockSpec((1, 2, CHUNK, RET_WIDTH), lambda l: (l, 0, 0, 0)),
            pl.BlockSpec((1, 2, CHUNK, RET_WIDTH), lambda l: (l, 0, 0, 0)),
            pl.BlockSpec((1, 2, SUBLANES, RET_WIDTH), lambda l: (l, 0, 0, 0)),
            pl.BlockSpec((1, 2, CHUNK, RET_HEADS * CHUNK), lambda l: (l, 0, 0, 0)),
        ],
        out_shape=[
            jax.ShapeDtypeStruct((depth, 2, CHUNK, RET_WIDTH), F32),
            jax.ShapeDtypeStruct((depth, 2, CHUNK, RET_WIDTH), F32),
            jax.ShapeDtypeStruct((depth, 2, SUBLANES, RET_WIDTH), F32),
            jax.ShapeDtypeStruct((depth, 2, CHUNK, RET_HEADS * CHUNK), F32),
        ],
        compiler_params=_params("parallel"),
        name="decay_tables",
    )(lgl, lgr)


def _rope(v, cs, sn):
    lane = lax.broadcasted_iota(I32, (1, LANES), 1)
    first = (lane & 31) < 16
    outs = []
    for g in range(v.shape[1] // LANES):
        vg = v[:, g * LANES:(g + 1) * LANES]
        sw = jnp.where(first, pltpu.roll(vg, LANES - 16, 1), pltpu.roll(vg, 16, 1))
        outs.append(vg * cs + sw * sn)
    return outs[0] if len(outs) == 1 else jnp.concatenate(outs, axis=1)


def _swap_halves(v):
    return jnp.concatenate([v[:, HEAD_DIM:], v[:, :HEAD_DIM]], axis=1)


def _from_gather_layout(f_ref, tm):
    return jnp.concatenate(
        [f_ref[0, pl.ds(q, tm, stride=SUBLANES), :] for q in range(SUBLANES)], axis=1)


def _rms(v):
    return v * lax.rsqrt(jnp.mean(v * v, axis=-1, keepdims=True) + EPS)


def _mod_rows(mod_ref, tm, n_ctx):
    m = mod_ref[0]
    if n_ctx % tm == 0:
        is_ctx = pl.program_id(1) < n_ctx // tm
    else:
        is_ctx = pl.program_id(1) * tm + lax.broadcasted_iota(I32, (tm, 1), 0) < n_ctx

    def row(r):
        return jnp.where(is_ctx, m[M_CTX + r:M_CTX + r + 1, :], m[r:r + 1, :])

    return row


def _inproj_kernel(has_ffn, n_ctx, *refs):
    if has_ffn:
        x_ref, f_ref, mod_ref, g_ref, w_ref, cs_ref, sn_ref = refs[:7]
        outs = refs[7:]
        xo_ref = outs[0]
        outs = outs[1:]
    else:
        x_ref, mod_ref, g_ref, w_ref, cs_ref, sn_ref = refs[:6]
        outs = refs[6:]
    cz_ref, rqkv_ref, gates_ref, aq_ref, kv_ref = outs
    tm = x_ref.shape[1]
    row = _mod_rows(mod_ref, tm, n_ctx)

    x = x_ref[0]
    if has_ffn:
        x = x + row(M_G2) * _from_gather_layout(f_ref, tm)
        xo_ref[0] = x
    h = (_rms(x) * g_ref[...]) * (1.0 + row(M_SC1)) + row(M_SH1)
    h = h.astype(BF16)
    cs = cs_ref[...]
    sn = sn_ref[...]

    def proj(a, b):
        return _dot(h, w_ref[:, a:b])

    cz_ref[0, :, 0:CONV_WIDTH] = proj(O_CB, O_CC)
    cz_ref[0, :, CONV_WIDTH:] = proj(O_CC, O_CX) * proj(O_CX, O_RQ)
    rqkv_ref[0, :, 0:RET_WIDTH] = _rope(proj(O_RQ, O_RK), cs, sn).astype(BF16)
    rqkv_ref[0, :, RET_WIDTH:2 * RET_WIDTH] = _rope(proj(O_RK, O_RV) * QK_SCALE, cs, sn).astype(BF16)
    rqkv_ref[0, :, 2 * RET_WIDTH:] = proj(O_RV, O_GF).astype(BF16)
    gates_ref[0, :, 0:RET_WIDTH] = _silu(proj(O_GF, O_GB))
    gates_ref[0, :, RET_WIDTH:] = _silu(proj(O_GB, O_AQ))
    aq_ref[0] = (_rope(proj(O_AQ, O_AK), cs, sn) * (QK_SCALE * LOG2E)).astype(BF16)
    ak = _rope(proj(O_AK, O_AV), cs, sn)
    av = proj(O_AV, IN_WIDTH)
    kv_ref[0, :, 0:KV_WIDTH] = ak.astype(BF16)
    kv_ref[0, :, KV_WIDTH:2 * KV_WIDTH] = _swap_halves(ak).astype(BF16)
    kv_ref[0, :, 2 * KV_WIDTH:] = av.astype(BF16)


def _inproj(x, ffn, mrows, norm_g, w_in_bf, layer, rope_cs, rope_sn, n_ctx):
    b, tu, d = x.shape
    tm = INPROJ_TILE
    has_ffn = ffn is not None
    tok = lambda width: pl.BlockSpec((1, tm, width), lambda i, j: (i, j, 0))
    in_specs = [tok(d)]
    args = [x]
    if has_ffn:
        in_specs.append(pl.BlockSpec((1, tm * SUBLANES, LANES), lambda i, j: (i, j, 0)))
        args.append(ffn)
    in_specs += [
        pl.BlockSpec((1, M_ROWS, d), lambda i, j: (i, 0, 0)),
        pl.BlockSpec((1, d), lambda i, j: (0, 0)),
        pl.BlockSpec((None, d, IN_WIDTH), lambda i, j: (layer, 0, 0)),
        pl.BlockSpec((tm, LANES), lambda i, j: (j, 0)),
        pl.BlockSpec((tm, LANES), lambda i, j: (j, 0)),
    ]
    args += [mrows, norm_g.reshape(1, d), w_in_bf, rope_cs, rope_sn]
    widths = [(2 * CONV_WIDTH, F32), (3 * RET_WIDTH, BF16), (2 * RET_WIDTH, F32), (ATTN_WIDTH, BF16),
              (3 * KV_WIDTH, BF16)]
    out_specs = [tok(w) for w, _ in widths]
    out_shape = [jax.ShapeDtypeStruct((b, tu, w), dt) for w, dt in widths]
    if has_ffn:
        out_specs = [tok(d)] + out_specs
        out_shape = [jax.ShapeDtypeStruct((b, tu, d), F32)] + out_shape
    res = pl.pallas_call(
        functools.partial(_inproj_kernel, has_ffn, n_ctx),
        grid=(b, tu // tm),
        in_specs=in_specs,
        out_specs=out_specs,
        out_shape=out_shape,
        compiler_params=_params("parallel", "parallel"),
        name="inproj",
    )(*args)
    if has_ffn:
        return res[0], res[1:]
    return x, res


def _head_block_mask(n):
    r = lax.broadcasted_iota(I32, (n, n), 0) // HEAD_DIM
    c = lax.broadcasted_iota(I32, (n, n), 1) // HEAD_DIM
    return r == c


def _states_kernel(n_ctx_chunks, n_chunks, rqkv_ref, win_ref, cd_ref, sp_ref, s_scr):
    same_head = _head_block_mask(RET_WIDTH)
    s_scr[...] = jnp.zeros_like(s_scr)

    def body(i, carry):
        for d in range(2):
            if d == 0:
                c = i
            else:
                c = jnp.where(i < n_ctx_chunks, n_ctx_chunks - 1 - i, n_chunks + n_ctx_chunks - 1 - i)
            off = pl.multiple_of(c * CHUNK, CHUNK)
            kw = rqkv_ref[0, pl.ds(off, CHUNK), RET_WIDTH:2 * RET_WIDTH].astype(F32) * win_ref[d]
            v = rqkv_ref[0, pl.ds(off, CHUNK), 2 * RET_WIDTH:]
            u = _dot(kw.T.astype(BF16), v)
            s = s_scr[d]
            sp_ref[0, d, pl.ds(c, 1)] = s.astype(BF16)[None]
            s_scr[d] = s * cd_ref[d][0:1, :] + jnp.where(same_head, u, 0.0)
        return carry

    lax.fori_loop(0, n_chunks, body, 0)


def _ret_states(rqkv, win, cd, n_ctx):
    b, tu, _ = rqkv.shape
    w = RET_WIDTH
    n_chunks = tu // CHUNK
    return pl.pallas_call(
        functools.partial(_states_kernel, n_ctx // CHUNK, n_chunks),
        grid=(b,),
        in_specs=[
            pl.BlockSpec((1, tu, 3 * w), lambda i: (i, 0, 0)),
            pl.BlockSpec((2, CHUNK, w), lambda i: (0, 0, 0)),
            pl.BlockSpec((2, SUBLANES, w), lambda i: (0, 0, 0)),
        ],
        out_specs=pl.BlockSpec((1, 2, n_chunks, w, w), lambda i: (i, 0, 0, 0, 0)),
        out_shape=jax.ShapeDtypeStruct((b, 2, n_chunks, w, w), BF16),
        scratch_shapes=[pltpu.VMEM((2, w, w), F32)],
        compiler_params=_params("parallel"),
        name="ret_states",
    )(rqkv, win, cd)


def _group_mean(t, ones_bd):
    hi = t.astype(BF16)
    lo = (t - hi.astype(F32)).astype(BF16)
    s = _dot(jnp.concatenate([hi, lo], axis=0), ones_bd)
    n = t.shape[0]
    return (s[:n] + s[n:]) * (1.0 / HEAD_DIM)


def _mix_chunk(c, n_ctx_chunks, n_chunks, cz, z_last, z_first, cw, rqkv, gates, sp, dm_ref, qdec_ref,
               qa, kvs, sink_ref):
    is_lat = c >= n_ctx_chunks

    z = cz[:, CONV_WIDTH:]
    row = lax.broadcasted_iota(I32, (CHUNK, 1), 0)
    has_prev = jnp.logical_and(c != 0, c != n_ctx_chunks)
    has_next = jnp.logical_and(c != n_ctx_chunks - 1, c != n_chunks - 1)
    z_before = jnp.where(row == 0, jnp.where(has_prev, z_last, 0.0), pltpu.roll(z, 1, 0))
    z_after = jnp.where(row == CHUNK - 1, jnp.where(has_next, z_first, 0.0), pltpu.roll(z, CHUNK - 1, 0))
    conv = cz[:, :CONV_WIDTH] * (z_before * cw[0:1, :] + z * cw[1:2, :] + z_after * cw[2:3, :])

    q = rqkv[:, 0:RET_WIDTH]
    k = rqkv[:, RET_WIDTH:2 * RET_WIDTH]
    v = rqkv[:, 2 * RET_WIDTH:]
    lane_head = lax.broadcasted_iota(I32, (1, RET_WIDTH), 1) // HEAD_DIM
    kz = jnp.zeros_like(k)
    k_heads = jnp.concatenate([jnp.where(lane_head == hh, k, kz) for hh in range(RET_HEADS)], axis=0)
    v_heads = jnp.concatenate([jnp.where(lane_head == hh, v, kz) for hh in range(RET_HEADS)], axis=0)
    scores = _dot_nt(q, k_heads)
    qf = q.astype(F32)
    outs = []
    for d in range(2):
        p = (scores * dm_ref[d]).astype(BF16)
        lhs = jnp.concatenate([p, (qf * qdec_ref[d]).astype(BF16)], axis=1)
        o = _dot(lhs, jnp.concatenate([v_heads, sp[d]], axis=0))
        outs.append(o)
    o2 = jnp.concatenate(outs, axis=0)
    ones_bd = jnp.where(_head_block_mask(RET_WIDTH), 1.0, 0.0).astype(BF16)
    dl = o2 - _group_mean(o2, ones_bd)
    on = dl * lax.rsqrt(_group_mean(dl * dl, ones_bd) + EPS)
    ret = on[:CHUNK] * gates[:, 0:RET_WIDTH] + on[CHUNK:] * gates[:, RET_WIDTH:]

    keys, v_t = kvs
    n_ctx = keys[0].shape[0] - 3 * CHUNK
    cols2 = 2 * CHUNK
    ik = lax.broadcasted_iota(I32, (CHUNK, cols2), 0)
    iq = lax.broadcasted_iota(I32, (CHUNK, cols2), 1) & (CHUNK - 1)
    ok_prev = jnp.logical_and(jnp.logical_and(is_lat, c - 1 >= n_ctx_chunks), ik >= iq)
    ok_cur = jnp.logical_and(is_lat, ik >= 0)
    ok_next = jnp.logical_and(jnp.logical_and(is_lat, c + 1 <= n_chunks - 1), ik <= iq)
    ok_ctx = jnp.ones((n_ctx, cols2), jnp.bool_)
    valid = jnp.concatenate([ok_prev, ok_cur, ok_next, ok_ctx], axis=0)
    half = lax.broadcasted_iota(I32, (1, LANES), 1) // HEAD_DIM
    qa_z = jnp.zeros((CHUNK, LANES), BF16)
    cols = [None] * (ATTN_HEADS // 2)
    for hk in range(ATTN_KV_HEADS):
        outs = []
        for par in range(2):
            ja, jb = 2 * hk, 2 * hk + 1
            qst = jnp.concatenate([
                jnp.where(half == par, qa[:, ja * LANES:(ja + 1) * LANES], qa_z),
                jnp.where(half == par, qa[:, jb * LANES:(jb + 1) * LANES], qa_z)], axis=0)
            sel = 0 if par == hk else 1
            s = jnp.where(valid, _dot_nt(keys[sel], qst), NEG_INF)
            ha, hb = ATTN_GROUP * hk + par, ATTN_GROUP * hk + par + 2
            snk = jnp.concatenate([sink_ref[ha * CHUNK:ha * CHUNK + 1, :],
                                   sink_ref[hb * CHUNK:hb * CHUNK + 1, :]], axis=1) * LOG2E
            mx = jnp.maximum(jnp.max(s, axis=0, keepdims=True), snk)
            e = jnp.exp2(s - mx)
            den = jnp.sum(e, axis=0, keepdims=True) + jnp.exp2(snk - mx)
            o_t = _dot(v_t[hk * HEAD_DIM:(hk + 1) * HEAD_DIM, :], e.astype(BF16))
            outs.append(o_t * (1.0 / den))
        cols[2 * hk] = jnp.concatenate([outs[0][:, :CHUNK], outs[1][:, :CHUNK]], axis=0).T
        cols[2 * hk + 1] = jnp.concatenate([outs[0][:, CHUNK:], outs[1][:, CHUNK:]], axis=0).T

    return jnp.concatenate([conv, ret] + cols, axis=1)


MIX_CHUNKS = 2


def _mixer_kernel(n_ctx_chunks, n_chunks, n_exp,
                  cz_ref, czp_ref, czn_ref, cw_ref, rqkv_ref, gates_ref, sp_ref, dm_ref, qdec_ref,
                  aq_ref, kvp_ref, kvc_ref, kvn_ref, kvx_ref, sink_ref,
                  x_ref, mod_ref, g_ref, w_ref, wrh_ref, wrl_ref, xo_ref, hg_ref, aff_ref):
    j = pl.program_id(1)
    cw = cw_ref[...]

    def values_t(blk):
        return blk[:, 2 * KV_WIDTH:].astype(F32).T.astype(BF16)

    kv_blocks = ([kvp_ref[0]] + [kvc_ref[0, s * CHUNK:(s + 1) * CHUNK, :] for s in range(MIX_CHUNKS)] + [kvn_ref[0]])
    kv_ctx = [kvx_ref[0, t * CHUNK:(t + 1) * CHUNK, :] for t in range(kvx_ref.shape[1] // CHUNK)]
    vt_blocks = [values_t(blk) for blk in kv_blocks]
    vt_ctx = [values_t(blk) for blk in kv_ctx]
    mixes = []
    for s in range(MIX_CHUNKS):
        lo, hi = s * CHUNK, (s + 1) * CHUNK
        if s == 0:
            z_last = czp_ref[0, SUBLANES - 1:SUBLANES, CONV_WIDTH:]
        else:
            z_last = cz_ref[0, lo - 1:lo, CONV_WIDTH:]
        if s == MIX_CHUNKS - 1:
            z_first = czn_ref[0, 0:1, CONV_WIDTH:]
        else:
            z_first = cz_ref[0, hi:hi + 1, CONV_WIDTH:]
        window = kv_blocks[s:s + 3] + kv_ctx
        keys = tuple(jnp.concatenate([blk[:, i * KV_WIDTH:(i + 1) * KV_WIDTH] for blk in window], axis=0)
                     for i in (0, 1))
        v_t = jnp.concatenate(vt_blocks[s:s + 3] + vt_ctx, axis=1)
        mix = _mix_chunk(j * MIX_CHUNKS + s, n_ctx_chunks, n_chunks, cz_ref[0, lo:hi, :], z_last, z_first, cw,
                         rqkv_ref[0, lo:hi, :], gates_ref[0, lo:hi, :], (sp_ref[0, 0, s], sp_ref[0, 1, s]),
                         dm_ref, qdec_ref, aq_ref[0, lo:hi, :], (keys, v_t), sink_ref)
        mixes.append(mix.astype(BF16))
    _outproj_tail(n_ctx_chunks * CHUNK, n_exp, jnp.concatenate(mixes, axis=0), x_ref, mod_ref, g_ref, w_ref,
                  wrh_ref, wrl_ref, xo_ref, hg_ref, aff_ref)


def _mixer(proj, sp, tabs, conv_w, sink_rows, x, mrows, norm_g, w_out_bf, layer, w_router, n_ctx):
    cz, rqkv, gates, aq, kv = proj
    dm, qdec = tabs
    b, tu, _ = cz.shape
    d = x.shape[2]
    n_exp = w_router.shape[1]
    wr = jnp.pad(w_router, ((0, 0), (0, LANES - n_exp)))
    wr_hi = wr.astype(BF16)
    wr_lo = (wr - wr_hi.astype(F32)).astype(BF16)
    n_chunks = tu // CHUNK
    mc = MIX_CHUNKS
    rows = mc * CHUNK
    assert n_chunks % mc == 0 and (n_ctx // CHUNK) % mc == 0
    last = n_chunks - 1
    per8 = rows // SUBLANES
    cur = lambda w: pl.BlockSpec((1, rows, w), lambda i, j: (i, j, 0))
    full = lambda shape: pl.BlockSpec(shape, lambda i, j: (0,) * len(shape))
    wcz, wkv = cz.shape[2], kv.shape[2]
    in_specs = [
        cur(wcz),
        pl.BlockSpec((1, SUBLANES, wcz), lambda i, j: (i, jnp.maximum(j * per8 - 1, 0), 0)),
        pl.BlockSpec((1, SUBLANES, wcz), lambda i, j: (i, jnp.minimum((j + 1) * per8, tu // SUBLANES - 1), 0)),
        full(conv_w.shape),
        cur(rqkv.shape[2]), cur(gates.shape[2]),
        pl.BlockSpec((1, 2, mc, RET_WIDTH, RET_WIDTH), lambda i, j: (i, 0, j, 0, 0)),
        full(dm.shape), full(qdec.shape),
        cur(ATTN_WIDTH),
        pl.BlockSpec((1, CHUNK, wkv), lambda i, j: (i, jnp.maximum(j * mc - 1, 0), 0)),
        cur(wkv),
        pl.BlockSpec((1, CHUNK, wkv), lambda i, j: (i, jnp.minimum((j + 1) * mc, last), 0)),
        pl.BlockSpec((1, n_ctx, wkv), lambda i, j: (i, 0, 0)),
        full(sink_rows.shape),
        cur(d),
        pl.BlockSpec((1, M_ROWS, d), lambda i, j: (i, 0, 0)),
        full((1, d)),
        pl.BlockSpec((None,) + w_out_bf.shape[1:], lambda i, j: (layer, 0, 0)),
        full((d, LANES)), full((d, LANES)),
    ]
    args = [cz, cz, cz, conv_w, rqkv, gates, sp, dm, qdec, aq, kv, kv, kv, kv, sink_rows,
            x, mrows, norm_g.reshape(1, d), w_out_bf, wr_hi, wr_lo]
    return pl.pallas_call(
        functools.partial(_mixer_kernel, n_ctx // CHUNK, n_chunks, n_exp),
        grid=(b, n_chunks // mc),
        in_specs=in_specs,
        out_specs=[
            cur(d),
            pl.BlockSpec((1, rows * SUBLANES, LANES), lambda i, j: (i, j, 0)),
            cur(LANES),
        ],
        out_shape=[
            jax.ShapeDtypeStruct((b, tu, d), F32),
            jax.ShapeDtypeStruct((b, tu * SUBLANES, LANES), F32),
            jax.ShapeDtypeStruct((b, tu, LANES), F32),
        ],
        compiler_params=_params("parallel", "parallel"),
        name="mixer",
    )(*args)


def _outproj_tail(n_ctx, n_exp, mix, x_ref, mod_ref, g_ref, w_ref, wrh_ref, wrl_ref, xo_ref, hg_ref, aff_ref):
    tm = x_ref.shape[1]
    row = _mod_rows(mod_ref, tm, n_ctx)

    x = x_ref[0] + row(M_G1) * _dot(mix, w_ref[...])
    xo_ref[0] = x
    h = (_rms(x) * g_ref[...]) * (1.0 + row(M_SC2)) + row(M_SH2)
    for q in range(SUBLANES):
        hg_ref[0, pl.ds(q, tm, stride=SUBLANES), :] = h[:, q * LANES:(q + 1) * LANES]
    h_hi = h.astype(BF16)
    h_lo = (h - h_hi.astype(F32)).astype(BF16)
    logits = _dot(h_hi, wrh_ref[...]) + (_dot(h_lo, wrh_ref[...]) + _dot(h_hi, wrl_ref[...]))
    lane = lax.broadcasted_iota(I32, (1, LANES), 1)
    logits = jnp.where(lane < n_exp, logits, NEG_INF)
    e = jnp.exp(logits - jnp.max(logits, axis=1, keepdims=True))
    aff_ref[0] = e / jnp.sum(e, axis=1, keepdims=True)


ROUTE_TABLES = 4


def _route_stream_tables(v, k, tab_scr, ce_scr, cnt_scr, stream, thr):
    n_exp, n = v.shape
    lane = lax.broadcasted_iota(I32, (1, LANES), 1)
    tri_r = lax.broadcasted_iota(I32, (LANES, LANES), 0)
    tri_c = lax.broadcasted_iota(I32, (LANES, LANES), 1)
    upper = jnp.where(tri_r <= tri_c, 1.0, 0.0).astype(BF16)
    above = pltpu.bitcast(thr + 1, F32)
    floor = pltpu.bitcast(thr, F32)
    n_gt = jnp.sum(jnp.where(v >= above, 1, 0), axis=1, keepdims=True)
    need = (k - n_gt).astype(F32)
    seen_eq = jnp.zeros((n_exp, 1), F32)
    counts = jnp.zeros((n_exp, LANES), F32)
    for j in range(n // LANES):
        vb = v[:, j * LANES:(j + 1) * LANES]
        gt = vb >= above
        eq = jnp.logical_and(vb >= floor, vb < above)
        both = jnp.concatenate([jnp.where(gt, 1.0, 0.0), jnp.where(eq, 1.0, 0.0)], axis=0).astype(BF16)
        pref = _dot(both, upper)
        rank_eq = pref[n_exp:] + seen_eq
        inc = pref[:n_exp] + jnp.minimum(rank_eq, need) - jnp.minimum(seen_eq, need)
        sel = jnp.logical_or(gt, jnp.logical_and(eq, rank_eq <= need))
        seen_eq = rank_eq[:, LANES - 1:LANES]
        counts = counts + jnp.where(lane == j, inc[:, LANES - 1:LANES], 0.0)
        a0 = jnp.where(sel, vb, 0.0)
        t0 = a0.astype(BF16).astype(F32)
        t1 = (a0 - t0).astype(BF16).astype(F32)
        t2 = a0 - t0 - t1
        for t, val in enumerate((inc, t0, t1, t2)):
            tab_scr[stream * ROUTE_TABLES + t, pl.ds(j, n_exp, stride=LANES), :] = val
    through = _dot(counts.astype(BF16), upper)
    for e in range(n_exp):
        ce_scr[stream, e] = through[e:e + 1, :]
        cnt_scr[stream, e] = counts[e:e + 1, :]


def _route_slots(e, stream, k, lo, tab_scr, ce_scr, cnt_scr):
    lane_f = lax.broadcasted_iota(I32, (1, LANES), 1).astype(F32)
    slot = lax.broadcasted_iota(I32, (k, LANES), 0).astype(F32)
    ones_rows = jnp.ones((LANES, LANES), BF16)
    cnt_rows = jnp.broadcast_to(cnt_scr[stream, e], (LANES, LANES)).astype(BF16)
    rows_e = pl.ds(pl.multiple_of(e * LANES, LANES), LANES)
    tab = jnp.concatenate([tab_scr[stream * ROUTE_TABLES + t, rows_e, :] for t in range(ROUTE_TABLES)],
                          axis=1).astype(BF16)
    before = jnp.where(ce_scr[stream, e] <= slot, 1.0, 0.0).astype(BF16)
    blk = _dot_nt(before, ones_rows)
    base = _dot_nt(before, cnt_rows)
    row = _dot(jnp.where(lane_f == blk, 1.0, 0.0).astype(BF16), tab)
    inc = row[:, :LANES]
    aff = row[:, LANES:2 * LANES] + row[:, 2 * LANES:3 * LANES] + row[:, 3 * LANES:]
    local = slot - base
    pos = _dot_nt(jnp.where(inc <= local, 1.0, 0.0).astype(BF16), ones_rows)
    tok = blk * LANES + pos + float(lo)
    gate = jnp.sum(jnp.where(inc == local + 1.0, aff, 0.0), axis=1, keepdims=True)
    return tok, gate


def _route_kernel(n_exp, streams, aff_ref, idx_ref, gate_ref, tab_scr, ce_scr, cnt_scr):
    a = aff_ref[0].T[:n_exp, :]
    tab_scr[...] = jnp.zeros_like(tab_scr)
    vs = [a[:, lo:lo + n] for lo, n, _, _ in streams]

    def search(i, ts):
        out = []
        for t, v, (_, _, k, _) in zip(ts, vs, streams):
            cand = t | jnp.left_shift(jnp.int32(1), 30 - i)
            cnt = jnp.sum(jnp.where(v >= pltpu.bitcast(cand, F32), 1, 0), axis=1, keepdims=True)
            out.append(jnp.where(cnt >= k, cand, t))
        return tuple(out)

    thrs = lax.fori_loop(0, 31, search, tuple(jnp.zeros((n_exp, 1), I32) for _ in streams))
    for st, (v, (_, _, k, _)) in enumerate(zip(vs, streams)):
        _route_stream_tables(v, k, tab_scr, ce_scr, cnt_scr, st, thrs[st])

    slots = idx_ref.shape[3]

    def per_expert(e, carry):
        toks = []
        for st, (lo, _, k, slot0) in enumerate(streams):
            assert slot0 == sum(t.shape[0] for t in toks)
            tok, gate = _route_slots(e, st, k, lo, tab_scr, ce_scr, cnt_scr)
            toks.append(tok)
            gate_ref[0, pl.ds(e, 1), slot0:slot0 + k, :] = gate[None]
        toks.append(jnp.zeros((-slots % LANES, LANES), F32))
        row = jnp.concatenate(toks, axis=0).T[0:1, :slots]
        idx_ref[0, pl.ds(e, 1)] = row.astype(I32)[None]
        return carry

    lax.fori_loop(0, n_exp, per_expert, 0)


def _route(aff, n_exp, n_ctx):
    b, tu, _ = aff.shape
    n_lat = tu - n_ctx
    cap_l = CAPACITY_FACTOR * n_lat // n_exp
    cap_c = CAPACITY_FACTOR * n_ctx // n_exp
    slots = cap_l + cap_c
    assert n_lat // LANES <= LANES and n_ctx // LANES <= LANES
    streams = ((n_ctx, n_lat, cap_l, 0), (0, n_ctx, cap_c, cap_l))
    return pl.pallas_call(
        functools.partial(_route_kernel, n_exp, streams),
        grid=(b,),
        in_specs=[pl.BlockSpec((1, tu, LANES), lambda i: (i, 0, 0))],
        out_specs=[pl.BlockSpec((1, n_exp, 1, slots), lambda i: (i, 0, 0, 0)),
                   pl.BlockSpec((1, n_exp, slots, 1), lambda i: (i, 0, 0, 0))],
        out_shape=[jax.ShapeDtypeStruct((b, n_exp, 1, slots), I32),
                   jax.ShapeDtypeStruct((b, n_exp, slots, 1), F32)],
        scratch_shapes=[pltpu.VMEM((len(streams) * ROUTE_TABLES, n_exp * LANES, LANES), F32),
                        pltpu.VMEM((len(streams), n_exp, 1, LANES), F32),
                        pltpu.VMEM((len(streams), n_exp, 1, LANES), F32)],
        compiler_params=_params("parallel"),
        name="route",
    )(aff)


def _slot_pitch(slots):
    return slots + SUBLANES


def _zero_slot_padding(ref, lead, slots, pitch):
    for q in range(SUBLANES):
        ref[lead + (pl.ds(q * pitch + slots, pitch - slots), slice(None))] = jnp.zeros((pitch - slots, LANES), ref.dtype)


BF16_ROWS = 2 * SUBLANES


def _slot_pitch_bf16(slots):
    return -(-slots // BF16_ROWS) * BF16_ROWS + BF16_ROWS


def _gather_kernel(slots, pitch, pitch_out, idx_ref, h_ref, o_ref, tile_scr):
    for mi in range(slots):
        r = pl.multiple_of(idx_ref[0, 0, mi] * SUBLANES, SUBLANES)
        tile_scr[pl.ds(mi, SUBLANES, stride=pitch), :] = h_ref[0, pl.ds(r, SUBLANES), :]
    _zero_slot_padding(o_ref, (0, 0), slots, pitch_out)
    for q in range(SUBLANES):
        o_ref[0, 0, q * pitch_out:q * pitch_out + slots, :] = tile_scr[q * pitch:q * pitch + slots, :].astype(BF16)


def _gather(hg, idx_rows, n_exp, slots):
    b = hg.shape[0]
    pitch = _slot_pitch(slots)
    pitch_out = _slot_pitch_bf16(slots)
    return pl.pallas_call(
        functools.partial(_gather_kernel, slots, pitch, pitch_out),
        grid=(b, n_exp),
        in_specs=[
            pl.BlockSpec((1, 1, slots), lambda i, e: (i * n_exp + e, 0, 0), memory_space=pltpu.SMEM),
            pl.BlockSpec((1,) + hg.shape[1:], lambda i, e: (i, 0, 0)),
        ],
        out_specs=pl.BlockSpec((1, 1, SUBLANES * pitch_out, LANES), lambda i, e: (i, e, 0, 0)),
        out_shape=jax.ShapeDtypeStruct((b, n_exp, SUBLANES * pitch_out, LANES), BF16),
        scratch_shapes=[pltpu.VMEM((SUBLANES * pitch, LANES), F32)],
        compiler_params=_params("parallel", "arbitrary"),
        name="gather",
    )(idx_rows, hg)


FFN_SAMPLES = 2
FFN_VMEM_LIMIT_BYTES = 60 * 1024 * 1024


def _ffn_kernel(slots, pitch_in, pitch, xs_ref, wg0_ref, wg1_ref, wu0_ref, wu1_ref, wd0_ref, wd1_ref, gate_ref, y_ref,
                wg_bf, wu_bf, wd_bf):
    @pl.when(pl.program_id(1) == 0)
    def _():
        half = wg_bf.shape[0] // 2
        for dst, lo_ref, hi_ref in ((wg_bf, wg0_ref, wg1_ref), (wu_bf, wu0_ref, wu1_ref), (wd_bf, wd0_ref, wd1_ref)):
            dst[:half, :] = lo_ref[0].astype(BF16)
            dst[half:, :] = hi_ref[0].astype(BF16)

    n = xs_ref.shape[0]
    x = jnp.concatenate(
        [jnp.concatenate([xs_ref[s, 0, q * pitch_in:q * pitch_in + slots, :] for q in range(SUBLANES)], axis=1)
         for s in range(n)], axis=0)
    gate = jnp.concatenate([gate_ref[s, 0] for s in range(n)], axis=0)
    a = _dot(x, wg_bf[...])
    u = _dot(x, wu_bf[...])
    y = _dot((_silu(a) * u).astype(BF16), wd_bf[...]) * gate
    for s in range(n):
        _zero_slot_padding(y_ref, (s, 0), slots, pitch)
        for q in range(SUBLANES):
            y_ref[s, 0, q * pitch:q * pitch + slots, :] = y[s * slots:(s + 1) * slots, q * LANES:(q + 1) * LANES]


def _ffn(xs, w_gate, w_up, w_down, layer, gate, slots):
    b, n_exp = xs.shape[:2]
    pitch = _slot_pitch(slots)
    pitch_in = _slot_pitch_bf16(slots)
    d, f = w_gate.shape[2:]
    ns = FFN_SAMPLES
    steps = b // ns
    assert steps == 2 and b % ns == 0
    slot_tile = lambda p: pl.BlockSpec((ns, 1, SUBLANES * p, LANES), lambda e, i: (i, e, 0, 0))

    def half(rows, cols, q):
        return pl.BlockSpec((None, 1, rows // 2, cols),
                            lambda e, i: (layer, jnp.minimum(e + (i > q).astype(jnp.int32), n_exp - 1), q, 0))

    return pl.pallas_call(
        functools.partial(_ffn_kernel, slots, pitch_in, pitch),
        grid=(n_exp, steps),
        in_specs=[
            slot_tile(pitch_in),
            half(d, f, 0), half(d, f, 1), half(d, f, 0), half(d, f, 1), half(f, d, 0), half(f, d, 1),
            pl.BlockSpec((ns, 1, slots, 1), lambda e, i: (i, e, 0, 0)),
        ],
        out_specs=slot_tile(pitch),
        out_shape=jax.ShapeDtypeStruct((b, n_exp, SUBLANES * pitch, LANES), F32),
        scratch_shapes=[pltpu.VMEM((d, f), BF16), pltpu.VMEM((d, f), BF16), pltpu.VMEM((f, d), BF16)],
        compiler_params=_params("arbitrary", "arbitrary", vmem=FFN_VMEM_LIMIT_BYTES),
        name="ffn",
    )(xs, w_gate, w_gate, w_up, w_up, w_down, w_down, gate)


SCATTER_BATCH = 16


def _scatter_kernel(slots, pitch, idx_ref, y_ref, o_ref):
    @pl.when(pl.program_id(1) == 0)
    def _():
        o_ref[...] = jnp.zeros_like(o_ref)

    for m0 in range(0, slots, SCATTER_BATCH):
        rows = [pl.multiple_of(idx_ref[0, 0, m0 + u] * SUBLANES, SUBLANES) for u in range(SCATTER_BATCH)]
        vals = [o_ref[0, pl.ds(rows[u], SUBLANES), :] + y_ref[0, 0, pl.ds(m0 + u, SUBLANES, stride=pitch), :]
                for u in range(SCATTER_BATCH)]
        for u in range(SCATTER_BATCH):
            o_ref[0, pl.ds(rows[u], SUBLANES), :] = vals[u]


def _scatter(y, idx_rows, tu, slots):
    b, n_exp = y.shape[:2]
    pitch = _slot_pitch(slots)
    return pl.pallas_call(
        functools.partial(_scatter_kernel, slots, pitch),
        grid=(b, n_exp),
        in_specs=[
            pl.BlockSpec((1, 1, slots), lambda i, e: (i * n_exp + e, 0, 0), memory_space=pltpu.SMEM),
            pl.BlockSpec((1, 1, SUBLANES * pitch, LANES), lambda i, e: (i, e, 0, 0)),
        ],
        out_specs=pl.BlockSpec((1, tu * SUBLANES, LANES), lambda i, e: (i, 0, 0)),
        out_shape=jax.ShapeDtypeStruct((b, tu * SUBLANES, LANES), F32),
        compiler_params=_params("parallel", "arbitrary"),
        name="scatter",
    )(idx_rows, y)


def _final_kernel(x_ref, f_ref, mod_ref, g_ref, o_ref):
    tm = x_ref.shape[1]
    x = x_ref[0] + mod_ref[0][M_G2:M_G2 + 1, :] * _from_gather_layout(f_ref, tm)
    o_ref[0] = _rms(x) * g_ref[...]


def _final(x, ffn, mrows, final_g, n_ctx):
    b, tu, d = x.shape
    tm = FINAL_TILE
    skip = n_ctx // tm
    return pl.pallas_call(
        _final_kernel,
        grid=(b, (tu - n_ctx) // tm),
        in_specs=[
            pl.BlockSpec((1, tm, d), lambda i, j: (i, j + skip, 0)),
            pl.BlockSpec((1, tm * SUBLANES, LANES), lambda i, j: (i, j + skip, 0)),
            pl.BlockSpec((1, M_ROWS, d), lambda i, j: (i, 0, 0)),
            pl.BlockSpec((1, d), lambda i, j: (0, 0)),
        ],
        out_specs=pl.BlockSpec((1, tm, d), lambda i, j: (i, j, 0)),
        out_shape=jax.ShapeDtypeStruct((b, tu - n_ctx, d), F32),
        compiler_params=_params("parallel", "parallel"),
        name="final_norm",
    )(x, ffn, mrows, final_g.reshape(1, d))


def _rope_tables(n_lat, n_ctx):
    rows = n_lat // GRID_W
    rowp = jnp.repeat(jnp.arange(rows, dtype=F32), GRID_W)
    colp = jnp.tile(jnp.arange(GRID_W, dtype=F32), rows)
    axis_dim = HEAD_DIM // 2
    inv_freq = ROPE_BASE ** (-jnp.arange(0, axis_dim, 2, dtype=F32) / axis_dim)
    ar = rowp[:, None] * inv_freq
    ac = colp[:, None] * inv_freq
    cs = jnp.concatenate([jnp.cos(ar), jnp.cos(ar), jnp.cos(ac), jnp.cos(ac)], axis=1)
    sn = jnp.concatenate([-jnp.sin(ar), jnp.sin(ar), -jnp.sin(ac), jnp.sin(ac)], axis=1)
    reps = LANES // HEAD_DIM
    cs = jnp.concatenate([jnp.ones((n_ctx, LANES), F32), jnp.tile(cs, (1, reps))], axis=0)
    sn = jnp.concatenate([jnp.zeros((n_ctx, LANES), F32), jnp.tile(sn, (1, reps))], axis=0)
    return cs, sn


def kernel(x, c, ctx, c_ctx, w_mod, b_mod, norm1_g, norm2_g, w_in, conv_w, ret_decay_logit, attn_sink,
           w_out, w_router, w_gate, w_up, w_down, final_g):
    b, n_lat, d = x.shape
    n_ctx = ctx.shape[1]
    depth = w_in.shape[0]
    n_exp = w_router.shape[2]
    tu = n_ctx + n_lat
    assert w_in.shape[2] == IN_WIDTH and tu % INPROJ_TILE == 0
    assert n_ctx % FINAL_TILE == 0 and n_lat % FINAL_TILE == 0
    assert b + 1 <= SUBLANES and n_exp == N_EXPERTS

    c_rows = jnp.concatenate([c, c_ctx[None], jnp.zeros((SUBLANES - b - 1, d), F32)], axis=0)
    mods = _mod_vectors(c_rows, w_mod, b_mod).reshape(depth, SUBLANES, 6, d)
    qdec, win, cd, dm = _decay_tables(ret_decay_logit)
    rope_cs, rope_sn = _rope_tables(n_lat, n_ctx)
    sink_rows = jnp.broadcast_to(jnp.repeat(attn_sink.astype(F32), CHUNK, axis=1)[:, :, None],
                                 (depth, ATTN_HEADS * CHUNK, LANES))
    w_in_bf = w_in.astype(BF16)
    w_out_bf = w_out.astype(BF16)

    xu = jnp.concatenate([ctx, x], axis=1)
    ffn = None
    mrows = None
    cap = CAPACITY_FACTOR * n_lat // n_exp + CAPACITY_FACTOR * n_ctx // n_exp
    for l in range(depth):
        prev_mrows = mrows
        mrows = jnp.concatenate([mods[l, :b], jnp.broadcast_to(mods[l, b][None], (b, 6, d)),
                                 jnp.zeros((b, M_ROWS - 12, d), F32)], axis=1)
        xu, proj = _inproj(xu, ffn, mrows if ffn is None else _with_prev_g2(mrows, prev_mrows),
                           norm1_g[l], w_in_bf, l, rope_cs, rope_sn, n_ctx)
        sp = _ret_states(proj[1], win[l], cd[l], n_ctx)
        xu, hg, aff = _mixer(proj, sp, (dm[l], qdec[l]), conv_w[l], sink_rows[l],
                             xu, mrows, norm2_g[l], w_out_bf, l, w_router[l], n_ctx)
        idx, gate = _route(aff, n_exp, n_ctx)
        idx_rows = idx.reshape(b * n_exp, 1, cap)
        xs = _gather(hg, idx_rows, n_exp, cap)
        y = _ffn(xs, w_gate, w_up, w_down, l, gate, cap)
        ffn = _scatter(y, idx_rows, tu, cap)
    return _final(xu, ffn, mrows, final_g, n_ctx)


def _with_prev_g2(mrows, prev_mrows):
    out = mrows.at[:, M_G2].set(prev_mrows[:, M_G2])
    return out.at[:, M_CTX + M_G2].set(prev_mrows[:, M_CTX + M_G2])
```

```python
import functools

import jax
import jax.numpy as jnp
from jax import lax
from jax.experimental import pallas as pl
from jax.experimental.pallas import tpu as pltpu

F32 = jnp.float32
BF16 = jnp.bfloat16
I32 = jnp.int32
HIGHEST = lax.Precision.HIGHEST

HEAD_DIM = 64
CONV_WIDTH = 256
RET_HEADS = 4
RET_WIDTH = RET_HEADS * HEAD_DIM
ATTN_HEADS = 8
ATTN_KV_HEADS = 2
ATTN_GROUP = ATTN_HEADS // ATTN_KV_HEADS
ATTN_WIDTH = ATTN_HEADS * HEAD_DIM
KV_WIDTH = ATTN_KV_HEADS * HEAD_DIM
CHUNK = 128
GRID_W = 64
N_EXPERTS = 16
CAPACITY_FACTOR = 2
ROPE_BASE = 10000.0
EPS = 1e-6
NEG_INF = -1e30
QK_SCALE = HEAD_DIM ** -0.5
LOG2E = 1.4426950408889634

LANES = 128
SUBLANES = 8
VMEM_LIMIT_BYTES = 56 * 1024 * 1024

O_CB = 0
O_CC = O_CB + CONV_WIDTH
O_CX = O_CC + CONV_WIDTH
O_RQ = O_CX + CONV_WIDTH
O_RK = O_RQ + RET_WIDTH
O_RV = O_RK + RET_WIDTH
O_GF = O_RV + RET_WIDTH
O_GB = O_GF + RET_WIDTH
O_AQ = O_GB + RET_WIDTH
O_AK = O_AQ + ATTN_WIDTH
O_AV = O_AK + KV_WIDTH
IN_WIDTH = O_AV + KV_WIDTH

M_SH1, M_SC1, M_G1, M_SH2, M_SC2, M_G2 = range(6)
M_CTX = 6
M_ROWS = 16

INPROJ_TILE = 544
FINAL_TILE = 256


def _params(*sem, vmem=VMEM_LIMIT_BYTES):
    return pltpu.CompilerParams(dimension_semantics=sem, vmem_limit_bytes=vmem)


def _dot(a, b):
    return jnp.dot(a, b, preferred_element_type=F32)


def _dot_nt(a, b):
    return lax.dot_general(a, b, (((1,), (1,)), ((), ())), preferred_element_type=F32)


def _silu(v):
    return v * jax.nn.sigmoid(v)


def _mod_kernel(c_ref, w_ref, b_ref, o_ref):
    s = _silu(c_ref[...])
    o_ref[0] = jnp.dot(s, w_ref[0], precision=HIGHEST, preferred_element_type=F32) + b_ref[0]


def _mod_vectors(c_rows, w_mod, b_mod):
    depth, d_model, width = w_mod.shape
    tn = 1536
    return pl.pallas_call(
        _mod_kernel,
        grid=(depth, width // tn),
        in_specs=[
            pl.BlockSpec((SUBLANES, d_model), lambda l, n: (0, 0)),
            pl.BlockSpec((1, d_model, tn), lambda l, n: (l, 0, n)),
            pl.BlockSpec((1, 1, tn), lambda l, n: (l, 0, n)),
        ],
        out_specs=pl.BlockSpec((1, SUBLANES, tn), lambda l, n: (l, 0, n)),
        out_shape=jax.ShapeDtypeStruct((depth, SUBLANES, width), F32),
        compiler_params=_params("parallel", "parallel"),
        name="mod_vectors",
    )(c_rows, w_mod, b_mod.reshape(depth, 1, width))


def _log_sigmoid(v):
    return -jnp.log(1.0 + jnp.exp(-v))


def _tables_kernel(lgl_ref, lgr_ref, qdec_ref, win_ref, cd_ref, dm_ref):
    pos = lax.broadcasted_iota(I32, (CHUNK, RET_WIDTH), 0).astype(F32)
    ri = lax.broadcasted_iota(I32, (CHUNK, RET_HEADS * CHUNK), 0).astype(F32)
    rj = (lax.broadcasted_iota(I32, (CHUNK, RET_HEADS * CHUNK), 1) & (CHUNK - 1)).astype(F32)
    for d in range(2):
        lg = _log_sigmoid(lgl_ref[0, d])
        lg1 = lg[0:1, :]
        if d == 0:
            qdec_ref[0, d] = jnp.exp(lg1 * (pos + 1.0))
            win_ref[0, d] = jnp.exp(lg1 * (CHUNK - 1.0 - pos))
            diff = ri - rj
        else:
            qdec_ref[0, d] = jnp.exp(lg1 * (CHUNK - pos))
            win_ref[0, d] = jnp.exp(lg1 * pos)
            diff = rj - ri
        cd_ref[0, d] = jnp.exp(lg * float(CHUNK))
        lr = _log_sigmoid(lgr_ref[0, d])
        dm_ref[0, d] = jnp.where(diff >= 0.0, jnp.exp(lr * jnp.maximum(diff, 0.0)), 0.0)


def _decay_tables(ret_decay_logit):
    depth = ret_decay_logit.shape[0]
    lg = ret_decay_logit.astype(F32)
    lgl = jnp.broadcast_to(jnp.repeat(lg, HEAD_DIM, axis=-1)[:, :, None, :], (depth, 2, SUBLANES, RET_WIDTH))
    lgr = jnp.broadcast_to(jnp.repeat(lg, CHUNK, axis=-1)[:, :, None, :], (depth, 2, CHUNK, RET_HEADS * CHUNK))
    return pl.pallas_call(
        _tables_kernel,
        grid=(depth,),
        in_specs=[
            pl.BlockSpec((1, 2, SUBLANES, RET_WIDTH), lambda l: (l, 0, 0, 0)),
            pl.BlockSpec((1, 2, CHUNK, RET_HEADS * CHUNK), lambda l: (l, 0, 0, 0)),
        ],
        out_specs=[
            pl.BlockSpec((1, 2, CHUNK, RET_WIDTH), lambda l: (l, 0, 0, 0)),
            pl.BlockSpec((1, 2, CHUNK, RET_WIDTH), lambda l: (l, 0, 0, 0)),
            pl.BlockSpec((1, 2, SUBLANES, RET_WIDTH), lambda l: (l, 0, 0, 0)),
            pl.BlockSpec((1, 2, CHUNK, RET_HEADS * CHUNK), lambda l: (l, 0, 0, 0)),
        ],
        out_shape=[
            jax.ShapeDtypeStruct((depth, 2, CHUNK, RET_WIDTH), F32),
            jax.ShapeDtypeStruct((depth, 2, CHUNK, RET_WIDTH), F32),
            jax.ShapeDtypeStruct((depth, 2, SUBLANES, RET_WIDTH), F32),
            jax.ShapeDtypeStruct((depth, 2, CHUNK, RET_HEADS * CHUNK), F32),
        ],
        compiler_params=_params("parallel"),
        name="decay_tables",
    )(lgl, lgr)


def _rope(v, cs, sn):
    lane = lax.broadcasted_iota(I32, (1, LANES), 1)
    first = (lane & 31) < 16
    outs = []
    for g in range(v.shape[1] // LANES):
        vg = v[:, g * LANES:(g + 1) * LANES]
        sw = jnp.where(first, pltpu.roll(vg, LANES - 16, 1), pltpu.roll(vg, 16, 1))
        outs.append(vg * cs + sw * sn)
    return outs[0] if len(outs) == 1 else jnp.concatenate(outs, axis=1)


def _swap_halves(v):
    return jnp.concatenate([v[:, HEAD_DIM:], v[:, :HEAD_DIM]], axis=1)


def _from_gather_layout(f_ref, tm):
    return jnp.concatenate(
        [f_ref[0, pl.ds(q, tm, stride=SUBLANES), :] for q in range(SUBLANES)], axis=1)


def _rms(v):
    return v * lax.rsqrt(jnp.mean(v * v, axis=-1, keepdims=True) + EPS)


def _mod_rows(mod_ref, tm, n_ctx, tile):
    m = mod_ref[0]
    if n_ctx % tm == 0:
        is_ctx = tile < n_ctx // tm
    else:
        is_ctx = tile * tm + lax.broadcasted_iota(I32, (tm, 1), 0) < n_ctx

    def row(r):
        return jnp.where(is_ctx, m[M_CTX + r:M_CTX + r + 1, :], m[r:r + 1, :])

    return row


def _inproj_kernel(has_ffn, n_ctx, *refs):
    if has_ffn:
        x_ref, f_ref, mod_ref, g_ref, w_ref, cs_ref, sn_ref = refs[:7]
        outs = refs[7:]
        xo_ref = outs[0]
        outs = outs[1:]
    else:
        x_ref, mod_ref, g_ref, w_ref, cs_ref, sn_ref = refs[:6]
        outs = refs[6:]
    cz_ref, rqkv_ref, gates_ref, aq_ref, kv_ref = outs
    tm = x_ref.shape[1]
    row = _mod_rows(mod_ref, tm, n_ctx, pl.program_id(1))

    x = x_ref[0]
    if has_ffn:
        x = x + row(M_G2) * _from_gather_layout(f_ref, tm)
        xo_ref[0] = x
    h = (_rms(x) * g_ref[...]) * (1.0 + row(M_SC1)) + row(M_SH1)
    h = h.astype(BF16)
    cs = cs_ref[...]
    sn = sn_ref[...]

    def proj(a, b):
        return _dot(h, w_ref[:, a:b])

    cz_ref[0, :, 0:CONV_WIDTH] = proj(O_CB, O_CC)
    cz_ref[0, :, CONV_WIDTH:] = proj(O_CC, O_CX) * proj(O_CX, O_RQ)
    rqkv_ref[0, :, 0:RET_WIDTH] = _rope(proj(O_RQ, O_RK), cs, sn).astype(BF16)
    rqkv_ref[0, :, RET_WIDTH:2 * RET_WIDTH] = _rope(proj(O_RK, O_RV) * QK_SCALE, cs, sn).astype(BF16)
    rqkv_ref[0, :, 2 * RET_WIDTH:] = proj(O_RV, O_GF).astype(BF16)
    gates_ref[0, :, 0:RET_WIDTH] = _silu(proj(O_GF, O_GB))
    gates_ref[0, :, RET_WIDTH:] = _silu(proj(O_GB, O_AQ))
    aq_ref[0] = (_rope(proj(O_AQ, O_AK), cs, sn) * (QK_SCALE * LOG2E)).astype(BF16)
    ak = _rope(proj(O_AK, O_AV), cs, sn)
    av = proj(O_AV, IN_WIDTH)
    kv_ref[0, :, 0:KV_WIDTH] = ak.astype(BF16)
    kv_ref[0, :, KV_WIDTH:2 * KV_WIDTH] = _swap_halves(ak).astype(BF16)
    kv_ref[0, :, 2 * KV_WIDTH:] = av.astype(BF16)


def _inproj(x, ffn, mrows, norm_g, w_in_bf, layer, rope_cs, rope_sn, n_ctx):
    b, tu, d = x.shape
    tm = INPROJ_TILE
    has_ffn = ffn is not None
    tok = lambda width: pl.BlockSpec((1, tm, width), lambda i, j: (i, j, 0))
    in_specs = [tok(d)]
    args = [x]
    if has_ffn:
        in_specs.append(pl.BlockSpec((1, tm * SUBLANES, LANES), lambda i, j: (i, j, 0)))
        args.append(ffn)
    in_specs += [
        pl.BlockSpec((1, M_ROWS, d), lambda i, j: (i, 0, 0)),
        pl.BlockSpec((1, d), lambda i, j: (0, 0)),
        pl.BlockSpec((None, d, IN_WIDTH), lambda i, j: (layer, 0, 0)),
        pl.BlockSpec((tm, LANES), lambda i, j: (j, 0)),
        pl.BlockSpec((tm, LANES), lambda i, j: (j, 0)),
    ]
    args += [mrows, norm_g.reshape(1, d), w_in_bf, rope_cs, rope_sn]
    widths = [(2 * CONV_WIDTH, F32), (3 * RET_WIDTH, BF16), (2 * RET_WIDTH, F32), (ATTN_WIDTH, BF16),
              (3 * KV_WIDTH, BF16)]
    out_specs = [tok(w) for w, _ in widths]
    out_shape = [jax.ShapeDtypeStruct((b, tu, w), dt) for w, dt in widths]
    if has_ffn:
        out_specs = [tok(d)] + out_specs
        out_shape = [jax.ShapeDtypeStruct((b, tu, d), F32)] + out_shape
    res = pl.pallas_call(
        functools.partial(_inproj_kernel, has_ffn, n_ctx),
        grid=(b, tu // tm),
        in_specs=in_specs,
        out_specs=out_specs,
        out_shape=out_shape,
        compiler_params=_params("parallel", "parallel"),
        name="inproj",
    )(*args)
    if has_ffn:
        return res[0], res[1:]
    return x, res


def _head_block_mask(n):
    r = lax.broadcasted_iota(I32, (n, n), 0) // HEAD_DIM
    c = lax.broadcasted_iota(I32, (n, n), 1) // HEAD_DIM
    return r == c


def _states_kernel(n_ctx_chunks, n_chunks, rqkv_ref, win_ref, cd_ref, sp_ref, s_scr):
    same_head = _head_block_mask(RET_WIDTH)
    s_scr[...] = jnp.zeros_like(s_scr)

    def body(i, carry):
        for d in range(2):
            if d == 0:
                c = i
            else:
                c = jnp.where(i < n_ctx_chunks, n_ctx_chunks - 1 - i, n_chunks + n_ctx_chunks - 1 - i)
            off = pl.multiple_of(c * CHUNK, CHUNK)
            kw = rqkv_ref[0, pl.ds(off, CHUNK), RET_WIDTH:2 * RET_WIDTH].astype(F32) * win_ref[d]
            v = rqkv_ref[0, pl.ds(off, CHUNK), 2 * RET_WIDTH:]
            u = _dot(kw.T.astype(BF16), v)
            s = s_scr[d]
            sp_ref[0, d, pl.ds(c, 1)] = s.astype(BF16)[None]
            s_scr[d] = s * cd_ref[d][0:1, :] + jnp.where(same_head, u, 0.0)
        return carry

    lax.fori_loop(0, n_chunks, body, 0)


def _ret_states(rqkv, win, cd, n_ctx):
    b, tu, _ = rqkv.shape
    w = RET_WIDTH
    n_chunks = tu // CHUNK
    return pl.pallas_call(
        functools.partial(_states_kernel, n_ctx // CHUNK, n_chunks),
        grid=(b,),
        in_specs=[
            pl.BlockSpec((1, tu, 3 * w), lambda i: (i, 0, 0)),
            pl.BlockSpec((2, CHUNK, w), lambda i: (0, 0, 0)),
            pl.BlockSpec((2, SUBLANES, w), lambda i: (0, 0, 0)),
        ],
        out_specs=pl.BlockSpec((1, 2, n_chunks, w, w), lambda i: (i, 0, 0, 0, 0)),
        out_shape=jax.ShapeDtypeStruct((b, 2, n_chunks, w, w), BF16),
        scratch_shapes=[pltpu.VMEM((2, w, w), F32)],
        compiler_params=_params("parallel"),
        name="ret_states",
    )(rqkv, win, cd)


def _group_mean(t, ones_bd):
    hi = t.astype(BF16)
    lo = (t - hi.astype(F32)).astype(BF16)
    s = _dot(jnp.concatenate([hi, lo], axis=0), ones_bd)
    n = t.shape[0]
    return (s[:n] + s[n:]) * (1.0 / HEAD_DIM)


def _mix_chunk(c, n_ctx_chunks, n_chunks, cz, z_last, z_first, cw, rqkv, gates, sp, dm_ref, qdec_ref,
               qa, kvs, sink_ref):
    is_lat = c >= n_ctx_chunks

    z = cz[:, CONV_WIDTH:]
    row = lax.broadcasted_iota(I32, (CHUNK, 1), 0)
    has_prev = jnp.logical_and(c != 0, c != n_ctx_chunks)
    has_next = jnp.logical_and(c != n_ctx_chunks - 1, c != n_chunks - 1)
    z_before = jnp.where(row == 0, jnp.where(has_prev, z_last, 0.0), pltpu.roll(z, 1, 0))
    z_after = jnp.where(row == CHUNK - 1, jnp.where(has_next, z_first, 0.0), pltpu.roll(z, CHUNK - 1, 0))
    conv = cz[:, :CONV_WIDTH] * (z_before * cw[0:1, :] + z * cw[1:2, :] + z_after * cw[2:3, :])

    q = rqkv[:, 0:RET_WIDTH]
    k = rqkv[:, RET_WIDTH:2 * RET_WIDTH]
    v = rqkv[:, 2 * RET_WIDTH:]
    lane_head = lax.broadcasted_iota(I32, (1, RET_WIDTH), 1) // HEAD_DIM
    kz = jnp.zeros_like(k)
    k_heads = jnp.concatenate([jnp.where(lane_head == hh, k, kz) for hh in range(RET_HEADS)], axis=0)
    v_heads = jnp.concatenate([jnp.where(lane_head == hh, v, kz) for hh in range(RET_HEADS)], axis=0)
    scores = _dot_nt(q, k_heads)

    keys, v_t = kvs
    half = lax.broadcasted_iota(I32, (1, LANES), 1) // HEAD_DIM
    qa_z = jnp.zeros((CHUNK, LANES), BF16)
    combos = [(hk, par) for hk in range(ATTN_KV_HEADS) for par in range(2)]
    raw = {}
    for hk, par in combos:
        ja, jb = 2 * hk, 2 * hk + 1
        qst = jnp.concatenate([
            jnp.where(half == par, qa[:, ja * LANES:(ja + 1) * LANES], qa_z),
            jnp.where(half == par, qa[:, jb * LANES:(jb + 1) * LANES], qa_z)], axis=0)
        sel = 0 if par == hk else 1
        raw[hk, par] = _dot_nt(keys[sel], qst)
    yield

    qf = q.astype(F32)
    outs = []
    for d in range(2):
        p = (scores * dm_ref[d]).astype(BF16)
        lhs = jnp.concatenate([p, (qf * qdec_ref[d]).astype(BF16)], axis=1)
        o = _dot(lhs, jnp.concatenate([v_heads, sp[d]], axis=0))
        outs.append(o)
    o2 = jnp.concatenate(outs, axis=0)
    yield

    n_ctx = keys[0].shape[0] - 3 * CHUNK
    cols2 = 2 * CHUNK
    ik = lax.broadcasted_iota(I32, (CHUNK, cols2), 0)
    iq = lax.broadcasted_iota(I32, (CHUNK, cols2), 1) & (CHUNK - 1)
    ok_prev = jnp.logical_and(jnp.logical_and(is_lat, c - 1 >= n_ctx_chunks), ik >= iq)
    ok_cur = jnp.logical_and(is_lat, ik >= 0)
    ok_next = jnp.logical_and(jnp.logical_and(is_lat, c + 1 <= n_chunks - 1), ik <= iq)
    ok_ctx = jnp.ones((n_ctx, cols2), jnp.bool_)
    valid = jnp.concatenate([ok_prev, ok_cur, ok_next, ok_ctx], axis=0)
    att = {}
    for n, (hk, par) in enumerate(combos):
        s = jnp.where(valid, raw[hk, par], NEG_INF)
        ha, hb = ATTN_GROUP * hk + par, ATTN_GROUP * hk + par + 2
        snk = jnp.concatenate([sink_ref[ha * CHUNK:ha * CHUNK + 1, :],
                               sink_ref[hb * CHUNK:hb * CHUNK + 1, :]], axis=1) * LOG2E
        mx = jnp.maximum(jnp.max(s, axis=0, keepdims=True), snk)
        e = jnp.exp2(s - mx)
        den = jnp.sum(e, axis=0, keepdims=True) + jnp.exp2(snk - mx)
        o_t = _dot(v_t[hk * HEAD_DIM:(hk + 1) * HEAD_DIM, :], e.astype(BF16))
        att[hk, par] = o_t * (1.0 / den)
        if n == 0:
            ones_bd = jnp.where(_head_block_mask(RET_WIDTH), 1.0, 0.0).astype(BF16)
            dl = o2 - _group_mean(o2, ones_bd)
            on = dl * lax.rsqrt(_group_mean(dl * dl, ones_bd) + EPS)
            ret = on[:CHUNK] * gates[:, 0:RET_WIDTH] + on[CHUNK:] * gates[:, RET_WIDTH:]
        yield

    cols = []
    for hk in range(ATTN_KV_HEADS):
        cols.append(jnp.concatenate([att[hk, 0][:, :CHUNK], att[hk, 1][:, :CHUNK]], axis=0).T)
        cols.append(jnp.concatenate([att[hk, 0][:, CHUNK:], att[hk, 1][:, CHUNK:]], axis=0).T)
    return jnp.concatenate([conv, ret] + cols, axis=1)


def _round_robin(gens):
    results = [None] * len(gens)
    active = list(range(len(gens)))
    while active:
        for i in list(active):
            try:
                next(gens[i])
            except StopIteration as done:
                results[i] = done.value
                active.remove(i)
    return results


MIX_CHUNKS = 2


def _mixer_kernel(n_ctx_chunks, n_chunks, n_exp,
                  cz_ref, czp_ref, czn_ref, cw_ref, rqkv_ref, gates_ref, sp_ref, dm_ref, qdec_ref,
                  aq_ref, kvp_ref, kvc_ref, kvn_ref, kvx_ref, sink_ref,
                  x_ref, mod_ref, g_ref, w_ref, wrh_ref, wrl_ref, xo_ref, hg_ref, aff_ref):
    j = pl.program_id(1)
    cw = cw_ref[...]

    def values_t(blk):
        return blk[:, 2 * KV_WIDTH:].astype(F32).T.astype(BF16)

    kv_blocks = ([kvp_ref[0]] + [kvc_ref[0, s * CHUNK:(s + 1) * CHUNK, :] for s in range(MIX_CHUNKS)] + [kvn_ref[0]])
    kv_ctx = [kvx_ref[0, t * CHUNK:(t + 1) * CHUNK, :] for t in range(kvx_ref.shape[1] // CHUNK)]
    vt_blocks = [values_t(blk) for blk in kv_blocks]
    vt_ctx = [values_t(blk) for blk in kv_ctx]
    gens = []
    for s in range(MIX_CHUNKS):
        lo, hi = s * CHUNK, (s + 1) * CHUNK
        if s == 0:
            z_last = czp_ref[0, SUBLANES - 1:SUBLANES, CONV_WIDTH:]
        else:
            z_last = cz_ref[0, lo - 1:lo, CONV_WIDTH:]
        if s == MIX_CHUNKS - 1:
            z_first = czn_ref[0, 0:1, CONV_WIDTH:]
        else:
            z_first = cz_ref[0, hi:hi + 1, CONV_WIDTH:]
        window = kv_blocks[s:s + 3] + kv_ctx
        keys = tuple(jnp.concatenate([blk[:, i * KV_WIDTH:(i + 1) * KV_WIDTH] for blk in window], axis=0)
                     for i in (0, 1))
        v_t = jnp.concatenate(vt_blocks[s:s + 3] + vt_ctx, axis=1)
        gens.append(_mix_chunk(j * MIX_CHUNKS + s, n_ctx_chunks, n_chunks, cz_ref[0, lo:hi, :], z_last, z_first,
                               cw, rqkv_ref[0, lo:hi, :], gates_ref[0, lo:hi, :],
                               (sp_ref[0, 0, s], sp_ref[0, 1, s]), dm_ref, qdec_ref, aq_ref[0, lo:hi, :],
                               (keys, v_t), sink_ref))
    mixes = [mix.astype(BF16) for mix in _round_robin(gens)]
    _outproj_tail(n_ctx_chunks * CHUNK, n_exp, j, jnp.concatenate(mixes, axis=0), x_ref, mod_ref, g_ref, w_ref,
                  wrh_ref, wrl_ref, xo_ref, hg_ref, aff_ref)


def _mixer(proj, sp, tabs, conv_w, sink_rows, x, mrows, norm_g, w_out_bf, layer, w_router, n_ctx):
    cz, rqkv, gates, aq, kv = proj
    dm, qdec = tabs
    b, tu, _ = cz.shape
    d = x.shape[2]
    n_exp = w_router.shape[1]
    wr = jnp.pad(w_router, ((0, 0), (0, LANES - n_exp)))
    wr_hi = wr.astype(BF16)
    wr_lo = (wr - wr_hi.astype(F32)).astype(BF16)
    n_chunks = tu // CHUNK
    mc = MIX_CHUNKS
    rows = mc * CHUNK
    assert n_chunks % mc == 0 and (n_ctx // CHUNK) % mc == 0
    last = n_chunks - 1
    per8 = rows // SUBLANES
    cur = lambda w: pl.BlockSpec((1, rows, w), lambda i, j: (i, j, 0))
    full = lambda shape: pl.BlockSpec(shape, lambda i, j: (0,) * len(shape))
    wcz, wkv = cz.shape[2], kv.shape[2]
    in_specs = [
        cur(wcz),
        pl.BlockSpec((1, SUBLANES, wcz), lambda i, j: (i, jnp.maximum(j * per8 - 1, 0), 0)),
        pl.BlockSpec((1, SUBLANES, wcz), lambda i, j: (i, jnp.minimum((j + 1) * per8, tu // SUBLANES - 1), 0)),
        full(conv_w.shape),
        cur(rqkv.shape[2]), cur(gates.shape[2]),
        pl.BlockSpec((1, 2, mc, RET_WIDTH, RET_WIDTH), lambda i, j: (i, 0, j, 0, 0)),
        full(dm.shape), full(qdec.shape),
        cur(ATTN_WIDTH),
        pl.BlockSpec((1, CHUNK, wkv), lambda i, j: (i, jnp.maximum(j * mc - 1, 0), 0)),
        cur(wkv),
        pl.BlockSpec((1, CHUNK, wkv), lambda i, j: (i, jnp.minimum((j + 1) * mc, last), 0)),
        pl.BlockSpec((1, n_ctx, wkv), lambda i, j: (i, 0, 0)),
        full(sink_rows.shape),
        cur(d),
        pl.BlockSpec((1, M_ROWS, d), lambda i, j: (i, 0, 0)),
        full((1, d)),
        pl.BlockSpec((None,) + w_out_bf.shape[1:], lambda i, j: (layer, 0, 0)),
        full((d, LANES)), full((d, LANES)),
    ]
    args = [cz, cz, cz, conv_w, rqkv, gates, sp, dm, qdec, aq, kv, kv, kv, kv, sink_rows,
            x, mrows, norm_g.reshape(1, d), w_out_bf, wr_hi, wr_lo]
    return pl.pallas_call(
        functools.partial(_mixer_kernel, n_ctx // CHUNK, n_chunks, n_exp),
        grid=(b, n_chunks // mc),
        in_specs=in_specs,
        out_specs=[
            cur(d),
            pl.BlockSpec((1, rows * SUBLANES, LANES), lambda i, j: (i, j, 0)),
            cur(LANES),
        ],
        out_shape=[
            jax.ShapeDtypeStruct((b, tu, d), F32),
            jax.ShapeDtypeStruct((b, tu * SUBLANES, LANES), F32),
            jax.ShapeDtypeStruct((b, tu, LANES), F32),
        ],
        compiler_params=_params("parallel", "parallel"),
        name="mixer",
    )(*args)


def _outproj_tail(n_ctx, n_exp, tile, mix, x_ref, mod_ref, g_ref, w_ref, wrh_ref, wrl_ref, xo_ref, hg_ref, aff_ref):
    tm = x_ref.shape[1]
    row = _mod_rows(mod_ref, tm, n_ctx, tile)

    x = x_ref[0] + row(M_G1) * _dot(mix, w_ref[...])
    xo_ref[0] = x
    h = (_rms(x) * g_ref[...]) * (1.0 + row(M_SC2)) + row(M_SH2)
    for q in range(SUBLANES):
        hg_ref[0, pl.ds(q, tm, stride=SUBLANES), :] = h[:, q * LANES:(q + 1) * LANES]
    h_hi = h.astype(BF16)
    h_lo = (h - h_hi.astype(F32)).astype(BF16)
    logits = _dot(h_hi, wrh_ref[...]) + (_dot(h_lo, wrh_ref[...]) + _dot(h_hi, wrl_ref[...]))
    lane = lax.broadcasted_iota(I32, (1, LANES), 1)
    logits = jnp.where(lane < n_exp, logits, NEG_INF)
    e = jnp.exp(logits - jnp.max(logits, axis=1, keepdims=True))
    aff_ref[0] = e / jnp.sum(e, axis=1, keepdims=True)


ROUTE_TABLES = 4


def _route_stream_tables(v, k, tab_scr, ce_scr, cnt_scr, stream, thr):
    n_exp, n = v.shape
    lane = lax.broadcasted_iota(I32, (1, LANES), 1)
    tri_r = lax.broadcasted_iota(I32, (LANES, LANES), 0)
    tri_c = lax.broadcasted_iota(I32, (LANES, LANES), 1)
    upper = jnp.where(tri_r <= tri_c, 1.0, 0.0).astype(BF16)
    above = pltpu.bitcast(thr + 1, F32)
    floor = pltpu.bitcast(thr, F32)
    n_gt = jnp.sum(jnp.where(v >= above, 1, 0), axis=1, keepdims=True)
    need = (k - n_gt).astype(F32)
    seen_eq = jnp.zeros((n_exp, 1), F32)
    counts = jnp.zeros((n_exp, LANES), F32)
    for j in range(n // LANES):
        vb = v[:, j * LANES:(j + 1) * LANES]
        gt = vb >= above
        eq = jnp.logical_and(vb >= floor, vb < above)
        both = jnp.concatenate([jnp.where(gt, 1.0, 0.0), jnp.where(eq, 1.0, 0.0)], axis=0).astype(BF16)
        pref = _dot(both, upper)
        rank_eq = pref[n_exp:] + seen_eq
        inc = pref[:n_exp] + jnp.minimum(rank_eq, need) - jnp.minimum(seen_eq, need)
        sel = jnp.logical_or(gt, jnp.logical_and(eq, rank_eq <= need))
        seen_eq = rank_eq[:, LANES - 1:LANES]
        counts = counts + jnp.where(lane == j, inc[:, LANES - 1:LANES], 0.0)
        a0 = jnp.where(sel, vb, 0.0)
        t0 = a0.astype(BF16).astype(F32)
        t1 = (a0 - t0).astype(BF16).astype(F32)
        t2 = a0 - t0 - t1
        for t, val in enumerate((inc, t0, t1, t2)):
            tab_scr[stream * ROUTE_TABLES + t, pl.ds(j, n_exp, stride=LANES), :] = val
    through = _dot(counts.astype(BF16), upper)
    for e in range(n_exp):
        ce_scr[stream, e] = through[e:e + 1, :]
        cnt_scr[stream, e] = counts[e:e + 1, :]


def _route_slots(e, stream, k, lo, tab_scr, ce_scr, cnt_scr):
    lane_f = lax.broadcasted_iota(I32, (1, LANES), 1).astype(F32)
    slot = lax.broadcasted_iota(I32, (k, LANES), 0).astype(F32)
    ones_rows = jnp.ones((LANES, LANES), BF16)
    cnt_rows = jnp.broadcast_to(cnt_scr[stream, e], (LANES, LANES)).astype(BF16)
    rows_e = pl.ds(pl.multiple_of(e * LANES, LANES), LANES)
    tab = jnp.concatenate([tab_scr[stream * ROUTE_TABLES + t, rows_e, :] for t in range(ROUTE_TABLES)],
                          axis=1).astype(BF16)
    before = jnp.where(ce_scr[stream, e] <= slot, 1.0, 0.0).astype(BF16)
    blk = _dot_nt(before, ones_rows)
    base = _dot_nt(before, cnt_rows)
    row = _dot(jnp.where(lane_f == blk, 1.0, 0.0).astype(BF16), tab)
    inc = row[:, :LANES]
    aff = row[:, LANES:2 * LANES] + row[:, 2 * LANES:3 * LANES] + row[:, 3 * LANES:]
    local = slot - base
    pos = _dot_nt(jnp.where(inc <= local, 1.0, 0.0).astype(BF16), ones_rows)
    tok = blk * LANES + pos + float(lo)
    gate = jnp.sum(jnp.where(inc == local + 1.0, aff, 0.0), axis=1, keepdims=True)
    return tok, gate


def _route_kernel(n_exp, streams, aff_ref, idx_ref, gate_ref, tab_scr, ce_scr, cnt_scr):
    a = aff_ref[0].T[:n_exp, :]
    tab_scr[...] = jnp.zeros_like(tab_scr)
    vs = [a[:, lo:lo + n] for lo, n, _, _ in streams]

    def search(i, ts):
        out = []
        for t, v, (_, _, k, _) in zip(ts, vs, streams):
            cand = t | jnp.left_shift(jnp.int32(1), 30 - i)
            cnt = jnp.sum(jnp.where(v >= pltpu.bitcast(cand, F32), 1, 0), axis=1, keepdims=True)
            out.append(jnp.where(cnt >= k, cand, t))
        return tuple(out)

    thrs = lax.fori_loop(0, 31, search, tuple(jnp.zeros((n_exp, 1), I32) for _ in streams))
    for st, (v, (_, _, k, _)) in enumerate(zip(vs, streams)):
        _route_stream_tables(v, k, tab_scr, ce_scr, cnt_scr, st, thrs[st])

    slots = idx_ref.shape[3]

    def per_expert(e, carry):
        toks = []
        for st, (lo, _, k, slot0) in enumerate(streams):
            assert slot0 == sum(t.shape[0] for t in toks)
            tok, gate = _route_slots(e, st, k, lo, tab_scr, ce_scr, cnt_scr)
            toks.append(tok)
            gate_ref[0, pl.ds(e, 1), slot0:slot0 + k, :] = gate[None]
        toks.append(jnp.zeros((-slots % LANES, LANES), F32))
        row = jnp.concatenate(toks, axis=0).T[0:1, :slots]
        idx_ref[0, pl.ds(e, 1)] = row.astype(I32)[None]
        return carry

    lax.fori_loop(0, n_exp, per_expert, 0)


def _route(aff, n_exp, n_ctx):
    b, tu, _ = aff.shape
    n_lat = tu - n_ctx
    cap_l = CAPACITY_FACTOR * n_lat // n_exp
    cap_c = CAPACITY_FACTOR * n_ctx // n_exp
    slots = cap_l + cap_c
    assert n_lat // LANES <= LANES and n_ctx // LANES <= LANES
    streams = ((n_ctx, n_lat, cap_l, 0), (0, n_ctx, cap_c, cap_l))
    return pl.pallas_call(
        functools.partial(_route_kernel, n_exp, streams),
        grid=(b,),
        in_specs=[pl.BlockSpec((1, tu, LANES), lambda i: (i, 0, 0))],
        out_specs=[pl.BlockSpec((1, n_exp, 1, slots), lambda i: (i, 0, 0, 0)),
                   pl.BlockSpec((1, n_exp, slots, 1), lambda i: (i, 0, 0, 0))],
        out_shape=[jax.ShapeDtypeStruct((b, n_exp, 1, slots), I32),
                   jax.ShapeDtypeStruct((b, n_exp, slots, 1), F32)],
        scratch_shapes=[pltpu.VMEM((len(streams) * ROUTE_TABLES, n_exp * LANES, LANES), F32),
                        pltpu.VMEM((len(streams), n_exp, 1, LANES), F32),
                        pltpu.VMEM((len(streams), n_exp, 1, LANES), F32)],
        compiler_params=_params("parallel"),
        name="route",
    )(aff)


def _slot_pitch(slots):
    return slots + SUBLANES


def _zero_slot_padding(ref, lead, slots, pitch):
    for q in range(SUBLANES):
        ref[lead + (pl.ds(q * pitch + slots, pitch - slots), slice(None))] = jnp.zeros((pitch - slots, LANES), ref.dtype)


BF16_ROWS = 2 * SUBLANES


def _slot_pitch_bf16(slots):
    return -(-slots // BF16_ROWS) * BF16_ROWS + BF16_ROWS


def _gather_kernel(slots, pitch, pitch_out, idx_ref, h_ref, o_ref, tile_scr):
    for mi in range(slots):
        r = pl.multiple_of(idx_ref[0, 0, mi] * SUBLANES, SUBLANES)
        tile_scr[pl.ds(mi, SUBLANES, stride=pitch), :] = h_ref[0, pl.ds(r, SUBLANES), :]
    _zero_slot_padding(o_ref, (0, 0), slots, pitch_out)
    for q in range(SUBLANES):
        o_ref[0, 0, q * pitch_out:q * pitch_out + slots, :] = tile_scr[q * pitch:q * pitch + slots, :].astype(BF16)


def _gather(hg, idx_rows, n_exp, slots):
    b = hg.shape[0]
    pitch = _slot_pitch(slots)
    pitch_out = _slot_pitch_bf16(slots)
    return pl.pallas_call(
        functools.partial(_gather_kernel, slots, pitch, pitch_out),
        grid=(b, n_exp),
        in_specs=[
            pl.BlockSpec((1, 1, slots), lambda i, e: (i * n_exp + e, 0, 0), memory_space=pltpu.SMEM),
            pl.BlockSpec((1,) + hg.shape[1:], lambda i, e: (i, 0, 0)),
        ],
        out_specs=pl.BlockSpec((1, 1, SUBLANES * pitch_out, LANES), lambda i, e: (i, e, 0, 0)),
        out_shape=jax.ShapeDtypeStruct((b, n_exp, SUBLANES * pitch_out, LANES), BF16),
        scratch_shapes=[pltpu.VMEM((SUBLANES * pitch, LANES), F32)],
        compiler_params=_params("parallel", "arbitrary"),
        name="gather",
    )(idx_rows, hg)


FFN_SAMPLES = 2
FFN_VMEM_LIMIT_BYTES = 60 * 1024 * 1024


def _ffn_kernel(slots, pitch_in, pitch, xs_ref, wg0_ref, wg1_ref, wu0_ref, wu1_ref, wd0_ref, wd1_ref, gate_ref, y_ref,
                wg_bf, wu_bf, wd_bf):
    @pl.when(pl.program_id(1) == 0)
    def _():
        half = wg_bf.shape[0] // 2
        for dst, lo_ref, hi_ref in ((wg_bf, wg0_ref, wg1_ref), (wu_bf, wu0_ref, wu1_ref), (wd_bf, wd0_ref, wd1_ref)):
            dst[:half, :] = lo_ref[0].astype(BF16)
            dst[half:, :] = hi_ref[0].astype(BF16)

    n = xs_ref.shape[0]
    x = jnp.concatenate(
        [jnp.concatenate([xs_ref[s, 0, q * pitch_in:q * pitch_in + slots, :] for q in range(SUBLANES)], axis=1)
         for s in range(n)], axis=0)
    gate = jnp.concatenate([gate_ref[s, 0] for s in range(n)], axis=0)
    a = _dot(x, wg_bf[...])
    u = _dot(x, wu_bf[...])
    y = _dot((_silu(a) * u).astype(BF16), wd_bf[...]) * gate
    for s in range(n):
        _zero_slot_padding(y_ref, (s, 0), slots, pitch)
        for q in range(SUBLANES):
            y_ref[s, 0, q * pitch:q * pitch + slots, :] = y[s * slots:(s + 1) * slots, q * LANES:(q + 1) * LANES]


def _ffn(xs, w_gate, w_up, w_down, layer, gate, slots):
    b, n_exp = xs.shape[:2]
    pitch = _slot_pitch(slots)
    pitch_in = _slot_pitch_bf16(slots)
    d, f = w_gate.shape[2:]
    ns = FFN_SAMPLES
    steps = b // ns
    assert steps == 2 and b % ns == 0
    slot_tile = lambda p: pl.BlockSpec((ns, 1, SUBLANES * p, LANES), lambda e, i: (i, e, 0, 0))

    def half(rows, cols, q):
        return pl.BlockSpec((None, 1, rows // 2, cols),
                            lambda e, i: (layer, jnp.minimum(e + (i > q).astype(jnp.int32), n_exp - 1), q, 0))

    return pl.pallas_call(
        functools.partial(_ffn_kernel, slots, pitch_in, pitch),
        grid=(n_exp, steps),
        in_specs=[
            slot_tile(pitch_in),
            half(d, f, 0), half(d, f, 1), half(d, f, 0), half(d, f, 1), half(f, d, 0), half(f, d, 1),
            pl.BlockSpec((ns, 1, slots, 1), lambda e, i: (i, e, 0, 0)),
        ],
        out_specs=slot_tile(pitch),
        out_shape=jax.ShapeDtypeStruct((b, n_exp, SUBLANES * pitch, LANES), F32),
        scratch_shapes=[pltpu.VMEM((d, f), BF16), pltpu.VMEM((d, f), BF16), pltpu.VMEM((f, d), BF16)],
        compiler_params=_params("arbitrary", "arbitrary", vmem=FFN_VMEM_LIMIT_BYTES),
        name="ffn",
    )(xs, w_gate, w_gate, w_up, w_up, w_down, w_down, gate)


SCATTER_BATCH = 16


def _scatter_kernel(slots, pitch, idx_ref, y_ref, o_ref):
    @pl.when(pl.program_id(1) == 0)
    def _():
        o_ref[...] = jnp.zeros_like(o_ref)

    for m0 in range(0, slots, SCATTER_BATCH):
        rows = [pl.multiple_of(idx_ref[0, 0, m0 + u] * SUBLANES, SUBLANES) for u in range(SCATTER_BATCH)]
        vals = [o_ref[0, pl.ds(rows[u], SUBLANES), :] + y_ref[0, 0, pl.ds(m0 + u, SUBLANES, stride=pitch), :]
                for u in range(SCATTER_BATCH)]
        for u in range(SCATTER_BATCH):
            o_ref[0, pl.ds(rows[u], SUBLANES), :] = vals[u]


def _scatter(y, idx_rows, tu, slots):
    b, n_exp = y.shape[:2]
    pitch = _slot_pitch(slots)
    return pl.pallas_call(
        functools.partial(_scatter_kernel, slots, pitch),
        grid=(b, n_exp),
        in_specs=[
            pl.BlockSpec((1, 1, slots), lambda i, e: (i * n_exp + e, 0, 0), memory_space=pltpu.SMEM),
            pl.BlockSpec((1, 1, SUBLANES * pitch, LANES), lambda i, e: (i, e, 0, 0)),
        ],
        out_specs=pl.BlockSpec((1, tu * SUBLANES, LANES), lambda i, e: (i, 0, 0)),
        out_shape=jax.ShapeDtypeStruct((b, tu * SUBLANES, LANES), F32),
        compiler_params=_params("parallel", "arbitrary"),
        name="scatter",
    )(idx_rows, y)


def _final_kernel(x_ref, f_ref, mod_ref, g_ref, o_ref):
    tm = x_ref.shape[1]
    x = x_ref[0] + mod_ref[0][M_G2:M_G2 + 1, :] * _from_gather_layout(f_ref, tm)
    o_ref[0] = _rms(x) * g_ref[...]


def _final(x, ffn, mrows, final_g, n_ctx):
    b, tu, d = x.shape
    tm = FINAL_TILE
    skip = n_ctx // tm
    return pl.pallas_call(
        _final_kernel,
        grid=(b, (tu - n_ctx) // tm),
        in_specs=[
            pl.BlockSpec((1, tm, d), lambda i, j: (i, j + skip, 0)),
            pl.BlockSpec((1, tm * SUBLANES, LANES), lambda i, j: (i, j + skip, 0)),
            pl.BlockSpec((1, M_ROWS, d), lambda i, j: (i, 0, 0)),
            pl.BlockSpec((1, d), lambda i, j: (0, 0)),
        ],
        out_specs=pl.BlockSpec((1, tm, d), lambda i, j: (i, j, 0)),
        out_shape=jax.ShapeDtypeStruct((b, tu - n_ctx, d), F32),
        compiler_params=_params("parallel", "parallel"),
        name="final_norm",
    )(x, ffn, mrows, final_g.reshape(1, d))


def _rope_tables(n_lat, n_ctx):
    rows = n_lat // GRID_W
    rowp = jnp.repeat(jnp.arange(rows, dtype=F32), GRID_W)
    colp = jnp.tile(jnp.arange(GRID_W, dtype=F32), rows)
    axis_dim = HEAD_DIM // 2
    inv_freq = ROPE_BASE ** (-jnp.arange(0, axis_dim, 2, dtype=F32) / axis_dim)
    ar = rowp[:, None] * inv_freq
    ac = colp[:, None] * inv_freq
    cs = jnp.concatenate([jnp.cos(ar), jnp.cos(ar), jnp.cos(ac), jnp.cos(ac)], axis=1)
    sn = jnp.concatenate([-jnp.sin(ar), jnp.sin(ar), -jnp.sin(ac), jnp.sin(ac)], axis=1)
    reps = LANES // HEAD_DIM
    cs = jnp.concatenate([jnp.ones((n_ctx, LANES), F32), jnp.tile(cs, (1, reps))], axis=0)
    sn = jnp.concatenate([jnp.zeros((n_ctx, LANES), F32), jnp.tile(sn, (1, reps))], axis=0)
    return cs, sn


def kernel(x, c, ctx, c_ctx, w_mod, b_mod, norm1_g, norm2_g, w_in, conv_w, ret_decay_logit, attn_sink,
           w_out, w_router, w_gate, w_up, w_down, final_g):
    b, n_lat, d = x.shape
    n_ctx = ctx.shape[1]
    depth = w_in.shape[0]
    n_exp = w_router.shape[2]
    tu = n_ctx + n_lat
    assert w_in.shape[2] == IN_WIDTH and tu % INPROJ_TILE == 0
    assert n_ctx % FINAL_TILE == 0 and n_lat % FINAL_TILE == 0
    assert b + 1 <= SUBLANES and n_exp == N_EXPERTS

    c_rows = jnp.concatenate([c, c_ctx[None], jnp.zeros((SUBLANES - b - 1, d), F32)], axis=0)
    mods = _mod_vectors(c_rows, w_mod, b_mod).reshape(depth, SUBLANES, 6, d)
    qdec, win, cd, dm = _decay_tables(ret_decay_logit)
    rope_cs, rope_sn = _rope_tables(n_lat, n_ctx)
    sink_rows = jnp.broadcast_to(jnp.repeat(attn_sink.astype(F32), CHUNK, axis=1)[:, :, None],
                                 (depth, ATTN_HEADS * CHUNK, LANES))
    w_in_bf = w_in.astype(BF16)
    w_out_bf = w_out.astype(BF16)

    xu = jnp.concatenate([ctx, x], axis=1)
    ffn = None
    mrows = None
    cap = CAPACITY_FACTOR * n_lat // n_exp + CAPACITY_FACTOR * n_ctx // n_exp
    for l in range(depth):
        prev_mrows = mrows
        mrows = jnp.concatenate([mods[l, :b], jnp.broadcast_to(mods[l, b][None], (b, 6, d)),
                                 jnp.zeros((b, M_ROWS - 12, d), F32)], axis=1)
        xu, proj = _inproj(xu, ffn, mrows if ffn is None else _with_prev_g2(mrows, prev_mrows),
                           norm1_g[l], w_in_bf, l, rope_cs, rope_sn, n_ctx)
        sp = _ret_states(proj[1], win[l], cd[l], n_ctx)
        xu, hg, aff = _mixer(proj, sp, (dm[l], qdec[l]), conv_w[l], sink_rows[l],
                             xu, mrows, norm2_g[l], w_out_bf, l, w_router[l], n_ctx)
        idx, gate = _route(aff, n_exp, n_ctx)
        idx_rows = idx.reshape(b * n_exp, 1, cap)
        xs = _gather(hg, idx_rows, n_exp, cap)
        y = _ffn(xs, w_gate, w_up, w_down, l, gate, cap)
        ffn = _scatter(y, idx_rows, tu, cap)
    return _final(xu, ffn, mrows, final_g, n_ctx)


def _with_prev_g2(mrows, prev_mrows):
    out = mrows.at[:, M_G2].set(prev_mrows[:, M_G2])
    return out.at[:, M_CTX + M_G2].set(prev_mrows[:, M_CTX + M_G2])
```

```python
import functools

import jax
import jax.numpy as jnp
from jax import lax
from jax.experimental import pallas as pl
from jax.experimental.pallas import tpu as pltpu

F32 = jnp.float32
BF16 = jnp.bfloat16
I32 = jnp.int32
HIGHEST = lax.Precision.HIGHEST

HEAD_DIM = 64
CONV_WIDTH = 256
RET_HEADS = 4
RET_WIDTH = RET_HEADS * HEAD_DIM
ATTN_HEADS = 8
ATTN_KV_HEADS = 2
ATTN_GROUP = ATTN_HEADS // ATTN_KV_HEADS
ATTN_WIDTH = ATTN_HEADS * HEAD_DIM
KV_WIDTH = ATTN_KV_HEADS * HEAD_DIM
CHUNK = 128
GRID_W = 64
N_EXPERTS = 16
CAPACITY_FACTOR = 2
ROPE_BASE = 10000.0
EPS = 1e-6
NEG_INF = -1e30
QK_SCALE = HEAD_DIM ** -0.5
LOG2E = 1.4426950408889634

LANES = 128
SUBLANES = 8
VMEM_LIMIT_BYTES = 56 * 1024 * 1024

O_CB = 0
O_CC = O_CB + CONV_WIDTH
O_CX = O_CC + CONV_WIDTH
O_RQ = O_CX + CONV_WIDTH
O_RK = O_RQ + RET_WIDTH
O_RV = O_RK + RET_WIDTH
O_GF = O_RV + RET_WIDTH
O_GB = O_GF + RET_WIDTH
O_AQ = O_GB + RET_WIDTH
O_AK = O_AQ + ATTN_WIDTH
O_AV = O_AK + KV_WIDTH
IN_WIDTH = O_AV + KV_WIDTH

M_SH1, M_SC1, M_G1, M_SH2, M_SC2, M_G2 = range(6)
M_CTX = 6
M_ROWS = 16

INPROJ_TILE = 544
FINAL_TILE = 256


def _params(*sem, vmem=VMEM_LIMIT_BYTES):
    return pltpu.CompilerParams(dimension_semantics=sem, vmem_limit_bytes=vmem)


def _dot(a, b):
    return jnp.dot(a, b, preferred_element_type=F32)


def _dot_nt(a, b):
    return lax.dot_general(a, b, (((1,), (1,)), ((), ())), preferred_element_type=F32)


def _silu(v):
    return v * jax.nn.sigmoid(v)


def _mod_kernel(c_ref, w_ref, b_ref, o_ref):
    s = _silu(c_ref[...])
    o_ref[0] = jnp.dot(s, w_ref[0], precision=HIGHEST, preferred_element_type=F32) + b_ref[0]


def _mod_vectors(c_rows, w_mod, b_mod):
    depth, d_model, width = w_mod.shape
    tn = 1536
    return pl.pallas_call(
        _mod_kernel,
        grid=(depth, width // tn),
        in_specs=[
            pl.BlockSpec((SUBLANES, d_model), lambda l, n: (0, 0)),
            pl.BlockSpec((1, d_model, tn), lambda l, n: (l, 0, n)),
            pl.BlockSpec((1, 1, tn), lambda l, n: (l, 0, n)),
        ],
        out_specs=pl.BlockSpec((1, SUBLANES, tn), lambda l, n: (l, 0, n)),
        out_shape=jax.ShapeDtypeStruct((depth, SUBLANES, width), F32),
        compiler_params=_params("parallel", "parallel"),
        name="mod_vectors",
    )(c_rows, w_mod, b_mod.reshape(depth, 1, width))


def _log_sigmoid(v):
    return -jnp.log(1.0 + jnp.exp(-v))


def _tables_kernel(lgl_ref, lgr_ref, qdec_ref, win_ref, cd_ref, dm_ref):
    pos = lax.broadcasted_iota(I32, (CHUNK, RET_WIDTH), 0).astype(F32)
    ri = lax.broadcasted_iota(I32, (CHUNK, RET_HEADS * CHUNK), 0).astype(F32)
    rj = (lax.broadcasted_iota(I32, (CHUNK, RET_HEADS * CHUNK), 1) & (CHUNK - 1)).astype(F32)
    for d in range(2):
        lg = _log_sigmoid(lgl_ref[0, d])
        lg1 = lg[0:1, :]
        if d == 0:
            qdec_ref[0, d] = jnp.exp(lg1 * (pos + 1.0))
            win_ref[0, d] = jnp.exp(lg1 * (CHUNK - 1.0 - pos))
            diff = ri - rj
        else:
            qdec_ref[0, d] = jnp.exp(lg1 * (CHUNK - pos))
            win_ref[0, d] = jnp.exp(lg1 * pos)
            diff = rj - ri
        cd_ref[0, d] = jnp.exp(lg * float(CHUNK))
        lr = _log_sigmoid(lgr_ref[0, d])
        dm_ref[0, d] = jnp.where(diff >= 0.0, jnp.exp(lr * jnp.maximum(diff, 0.0)), 0.0)


def _decay_tables(ret_decay_logit):
    depth = ret_decay_logit.shape[0]
    lg = ret_decay_logit.astype(F32)
    lgl = jnp.broadcast_to(jnp.repeat(lg, HEAD_DIM, axis=-1)[:, :, None, :], (depth, 2, SUBLANES, RET_WIDTH))
    lgr = jnp.broadcast_to(jnp.repeat(lg, CHUNK, axis=-1)[:, :, None, :], (depth, 2, CHUNK, RET_HEADS * CHUNK))
    return pl.pallas_call(
        _tables_kernel,
        grid=(depth,),
        in_specs=[
            pl.BlockSpec((1, 2, SUBLANES, RET_WIDTH), lambda l: (l, 0, 0, 0)),
            pl.BlockSpec((1, 2, CHUNK, RET_HEADS * CHUNK), lambda l: (l, 0, 0, 0)),
        ],
        out_specs=[
            pl.BlockSpec((1, 2, CHUNK, RET_WIDTH), lambda l: (l, 0, 0, 0)),
            pl.BlockSpec((1, 2, CHUNK, RET_WIDTH), lambda l: (l, 0, 0, 0)),
            pl.BlockSpec((1, 2, SUBLANES, RET_WIDTH), lambda l: (l, 0, 0, 0)),
            pl.BlockSpec((1, 2, CHUNK, RET_HEADS * CHUNK), lambda l: (l, 0, 0, 0)),
        ],
        out_shape=[
            jax.ShapeDtypeStruct((depth, 2, CHUNK, RET_WIDTH), F32),
            jax.ShapeDtypeStruct((depth, 2, CHUNK, RET_WIDTH), F32),
            jax.ShapeDtypeStruct((depth, 2, SUBLANES, RET_WIDTH), F32),
            jax.ShapeDtypeStruct((depth, 2, CHUNK, RET_HEADS * CHUNK), F32),
        ],
        compiler_params=_params("parallel"),
        name="decay_tables",
    )(lgl, lgr)


def _rope(v, cs, sn):
    lane = lax.broadcasted_iota(I32, (1, LANES), 1)
    first = (lane & 31) < 16
    outs = []
    for g in range(v.shape[1] // LANES):
        vg = v[:, g * LANES:(g + 1) * LANES]
        sw = jnp.where(first, pltpu.roll(vg, LANES - 16, 1), pltpu.roll(vg, 16, 1))
        outs.append(vg * cs + sw * sn)
    return outs[0] if len(outs) == 1 else jnp.concatenate(outs, axis=1)


def _swap_halves(v):
    return jnp.concatenate([v[:, HEAD_DIM:], v[:, :HEAD_DIM]], axis=1)


def _from_gather_layout(f_ref, tm):
    return jnp.concatenate(
        [f_ref[0, pl.ds(q, tm, stride=SUBLANES), :] for q in range(SUBLANES)], axis=1)


def _rms(v):
    return v * lax.rsqrt(jnp.mean(v * v, axis=-1, keepdims=True) + EPS)


def _mod_rows(mod_ref, tm, n_ctx, tile):
    m = mod_ref[0]
    if n_ctx % tm == 0:
        is_ctx = tile < n_ctx // tm
    else:
        is_ctx = tile * tm + lax.broadcasted_iota(I32, (tm, 1), 0) < n_ctx

    def row(r):
        return jnp.where(is_ctx, m[M_CTX + r:M_CTX + r + 1, :], m[r:r + 1, :])

    return row


def _inproj_kernel(has_ffn, n_ctx, *refs):
    if has_ffn:
        x_ref, f_ref, mod_ref, g_ref, w_ref, cs_ref, sn_ref = refs[:7]
        outs = refs[7:]
        xo_ref = outs[0]
        outs = outs[1:]
    else:
        x_ref, mod_ref, g_ref, w_ref, cs_ref, sn_ref = refs[:6]
        outs = refs[6:]
    cz_ref, rqkv_ref, gates_ref, aq_ref, kv_ref = outs
    tm = x_ref.shape[1]
    row = _mod_rows(mod_ref, tm, n_ctx, pl.program_id(1))

    x = x_ref[0]
    if has_ffn:
        x = x + row(M_G2) * _from_gather_layout(f_ref, tm)
        xo_ref[0] = x
    h = (_rms(x) * g_ref[...]) * (1.0 + row(M_SC1)) + row(M_SH1)
    h = h.astype(BF16)
    cs = cs_ref[...]
    sn = sn_ref[...]

    def proj(a, b):
        return _dot(h, w_ref[:, a:b])

    cz_ref[0, :, 0:CONV_WIDTH] = proj(O_CB, O_CC)
    cz_ref[0, :, CONV_WIDTH:] = proj(O_CC, O_CX) * proj(O_CX, O_RQ)
    rqkv_ref[0, :, 0:RET_WIDTH] = _rope(proj(O_RQ, O_RK), cs, sn).astype(BF16)
    rqkv_ref[0, :, RET_WIDTH:2 * RET_WIDTH] = _rope(proj(O_RK, O_RV) * QK_SCALE, cs, sn).astype(BF16)
    rqkv_ref[0, :, 2 * RET_WIDTH:] = proj(O_RV, O_GF).astype(BF16)
    gates_ref[0, :, 0:RET_WIDTH] = _silu(proj(O_GF, O_GB))
    gates_ref[0, :, RET_WIDTH:] = _silu(proj(O_GB, O_AQ))
    aq_ref[0] = (_rope(proj(O_AQ, O_AK), cs, sn) * (QK_SCALE * LOG2E)).astype(BF16)
    ak = _rope(proj(O_AK, O_AV), cs, sn)
    av = proj(O_AV, IN_WIDTH)
    kv_ref[0, :, 0:KV_WIDTH] = ak.astype(BF16)
    kv_ref[0, :, KV_WIDTH:2 * KV_WIDTH] = _swap_halves(ak).astype(BF16)
    kv_ref[0, :, 2 * KV_WIDTH:] = av.astype(BF16)


def _inproj(x, ffn, mrows, norm_g, w_in_bf, layer, rope_cs, rope_sn, n_ctx):
    b, tu, d = x.shape
    tm = INPROJ_TILE
    has_ffn = ffn is not None
    tok = lambda width: pl.BlockSpec((1, tm, width), lambda i, j: (i, j, 0))
    in_specs = [tok(d)]
    args = [x]
    if has_ffn:
        in_specs.append(pl.BlockSpec((1, tm * SUBLANES, LANES), lambda i, j: (i, j, 0)))
        args.append(ffn)
    in_specs += [
        pl.BlockSpec((1, M_ROWS, d), lambda i, j: (i, 0, 0)),
        pl.BlockSpec((1, d), lambda i, j: (0, 0)),
        pl.BlockSpec((None, d, IN_WIDTH), lambda i, j: (layer, 0, 0)),
        pl.BlockSpec((tm, LANES), lambda i, j: (j, 0)),
        pl.BlockSpec((tm, LANES), lambda i, j: (j, 0)),
    ]
    args += [mrows, norm_g.reshape(1, d), w_in_bf, rope_cs, rope_sn]
    widths = [(2 * CONV_WIDTH, F32), (3 * RET_WIDTH, BF16), (2 * RET_WIDTH, F32), (ATTN_WIDTH, BF16),
              (3 * KV_WIDTH, BF16)]
    out_specs = [tok(w) for w, _ in widths]
    out_shape = [jax.ShapeDtypeStruct((b, tu, w), dt) for w, dt in widths]
    if has_ffn:
        out_specs = [tok(d)] + out_specs
        out_shape = [jax.ShapeDtypeStruct((b, tu, d), F32)] + out_shape
    res = pl.pallas_call(
        functools.partial(_inproj_kernel, has_ffn, n_ctx),
        grid=(b, tu // tm),
        in_specs=in_specs,
        out_specs=out_specs,
        out_shape=out_shape,
        compiler_params=_params("parallel", "parallel"),
        name="inproj",
    )(*args)
    if has_ffn:
        return res[0], res[1:]
    return x, res


def _head_block_mask(n):
    r = lax.broadcasted_iota(I32, (n, n), 0) // HEAD_DIM
    c = lax.broadcasted_iota(I32, (n, n), 1) // HEAD_DIM
    return r == c


def _states_kernel(n_ctx_chunks, n_chunks, rqkv_ref, win_ref, cd_ref, sp_ref, s_scr):
    same_head = _head_block_mask(RET_WIDTH)
    s_scr[...] = jnp.zeros_like(s_scr)

    def body(i, carry):
        for d in range(2):
            if d == 0:
                c = i
            else:
                c = jnp.where(i < n_ctx_chunks, n_ctx_chunks - 1 - i, n_chunks + n_ctx_chunks - 1 - i)
            off = pl.multiple_of(c * CHUNK, CHUNK)
            kw = rqkv_ref[0, pl.ds(off, CHUNK), RET_WIDTH:2 * RET_WIDTH].astype(F32) * win_ref[d]
            v = rqkv_ref[0, pl.ds(off, CHUNK), 2 * RET_WIDTH:]
            u = _dot(kw.T.astype(BF16), v)
            s = s_scr[d]
            sp_ref[0, d, pl.ds(c, 1)] = s.astype(BF16)[None]
            s_scr[d] = s * cd_ref[d][0:1, :] + jnp.where(same_head, u, 0.0)
        return carry

    lax.fori_loop(0, n_chunks, body, 0)


def _ret_states(rqkv, win, cd, n_ctx):
    b, tu, _ = rqkv.shape
    w = RET_WIDTH
    n_chunks = tu // CHUNK
    return pl.pallas_call(
        functools.partial(_states_kernel, n_ctx // CHUNK, n_chunks),
        grid=(b,),
        in_specs=[
            pl.BlockSpec((1, tu, 3 * w), lambda i: (i, 0, 0)),
            pl.BlockSpec((2, CHUNK, w), lambda i: (0, 0, 0)),
            pl.BlockSpec((2, SUBLANES, w), lambda i: (0, 0, 0)),
        ],
        out_specs=pl.BlockSpec((1, 2, n_chunks, w, w), lambda i: (i, 0, 0, 0, 0)),
        out_shape=jax.ShapeDtypeStruct((b, 2, n_chunks, w, w), BF16),
        scratch_shapes=[pltpu.VMEM((2, w, w), F32)],
        compiler_params=_params("parallel"),
        name="ret_states",
    )(rqkv, win, cd)


def _group_mean(t, ones_bd):
    hi = t.astype(BF16)
    lo = (t - hi.astype(F32)).astype(BF16)
    s = _dot(jnp.concatenate([hi, lo], axis=0), ones_bd)
    n = t.shape[0]
    return (s[:n] + s[n:]) * (1.0 / HEAD_DIM)


def _mix_chunk(c, n_ctx_chunks, n_chunks, cz, z_last, z_first, cw, rqkv, gates, sp, dm_ref, qdec_ref,
               qa, kvs, sink_ref):
    is_lat = c >= n_ctx_chunks

    z = cz[:, CONV_WIDTH:]
    row = lax.broadcasted_iota(I32, (CHUNK, 1), 0)
    has_prev = jnp.logical_and(c != 0, c != n_ctx_chunks)
    has_next = jnp.logical_and(c != n_ctx_chunks - 1, c != n_chunks - 1)
    z_before = jnp.where(row == 0, jnp.where(has_prev, z_last, 0.0), pltpu.roll(z, 1, 0))
    z_after = jnp.where(row == CHUNK - 1, jnp.where(has_next, z_first, 0.0), pltpu.roll(z, CHUNK - 1, 0))
    conv = cz[:, :CONV_WIDTH] * (z_before * cw[0:1, :] + z * cw[1:2, :] + z_after * cw[2:3, :])

    q = rqkv[:, 0:RET_WIDTH]
    k = rqkv[:, RET_WIDTH:2 * RET_WIDTH]
    v = rqkv[:, 2 * RET_WIDTH:]
    lane_head = lax.broadcasted_iota(I32, (1, RET_WIDTH), 1) // HEAD_DIM
    kz = jnp.zeros_like(k)
    k_heads = jnp.concatenate([jnp.where(lane_head == hh, k, kz) for hh in range(RET_HEADS)], axis=0)
    v_heads = jnp.concatenate([jnp.where(lane_head == hh, v, kz) for hh in range(RET_HEADS)], axis=0)
    scores = _dot_nt(q, k_heads)

    keys, v_t = kvs
    half = lax.broadcasted_iota(I32, (1, LANES), 1) // HEAD_DIM
    qa_z = jnp.zeros((CHUNK, LANES), BF16)
    combos = [(hk, par) for hk in range(ATTN_KV_HEADS) for par in range(2)]
    raw = {}
    for hk, par in combos:
        ja, jb = 2 * hk, 2 * hk + 1
        qst = jnp.concatenate([
            jnp.where(half == par, qa[:, ja * LANES:(ja + 1) * LANES], qa_z),
            jnp.where(half == par, qa[:, jb * LANES:(jb + 1) * LANES], qa_z)], axis=0)
        sel = 0 if par == hk else 1
        raw[hk, par] = _dot_nt(keys[sel], qst)
    yield

    qf = q.astype(F32)
    outs = []
    for d in range(2):
        p = (scores * dm_ref[d]).astype(BF16)
        lhs = jnp.concatenate([p, (qf * qdec_ref[d]).astype(BF16)], axis=1)
        o = _dot(lhs, jnp.concatenate([v_heads, sp[d]], axis=0))
        outs.append(o)
    o2 = jnp.concatenate(outs, axis=0)
    yield

    n_ctx = keys[0].shape[0] - 3 * CHUNK
    cols2 = 2 * CHUNK
    ik = lax.broadcasted_iota(I32, (CHUNK, cols2), 0)
    iq = lax.broadcasted_iota(I32, (CHUNK, cols2), 1) & (CHUNK - 1)
    ok_prev = jnp.logical_and(jnp.logical_and(is_lat, c - 1 >= n_ctx_chunks), ik >= iq)
    ok_cur = jnp.logical_and(is_lat, ik >= 0)
    ok_next = jnp.logical_and(jnp.logical_and(is_lat, c + 1 <= n_chunks - 1), ik <= iq)
    ok_ctx = jnp.ones((n_ctx, cols2), jnp.bool_)
    valid = jnp.concatenate([ok_prev, ok_cur, ok_next, ok_ctx], axis=0)
    att = {}
    for n, (hk, par) in enumerate(combos):
        s = jnp.where(valid, raw[hk, par], NEG_INF)
        ha, hb = ATTN_GROUP * hk + par, ATTN_GROUP * hk + par + 2
        snk = jnp.concatenate([sink_ref[ha * CHUNK:ha * CHUNK + 1, :],
                               sink_ref[hb * CHUNK:hb * CHUNK + 1, :]], axis=1) * LOG2E
        mx = jnp.maximum(jnp.max(s, axis=0, keepdims=True), snk)
        e = jnp.exp2(s - mx)
        den = jnp.sum(e, axis=0, keepdims=True) + jnp.exp2(snk - mx)
        o_t = _dot(v_t[hk * HEAD_DIM:(hk + 1) * HEAD_DIM, :], e.astype(BF16))
        att[hk, par] = o_t * (1.0 / den)
        if n == 0:
            ones_bd = jnp.where(_head_block_mask(RET_WIDTH), 1.0, 0.0).astype(BF16)
            dl = o2 - _group_mean(o2, ones_bd)
            on = dl * lax.rsqrt(_group_mean(dl * dl, ones_bd) + EPS)
            ret = on[:CHUNK] * gates[:, 0:RET_WIDTH] + on[CHUNK:] * gates[:, RET_WIDTH:]
        yield

    cols = []
    for hk in range(ATTN_KV_HEADS):
        cols.append(jnp.concatenate([att[hk, 0][:, :CHUNK], att[hk, 1][:, :CHUNK]], axis=0).T)
        cols.append(jnp.concatenate([att[hk, 0][:, CHUNK:], att[hk, 1][:, CHUNK:]], axis=0).T)
    return jnp.concatenate([conv, ret] + cols, axis=1)


def _round_robin(gens):
    results = [None] * len(gens)
    active = list(range(len(gens)))
    while active:
        for i in list(active):
            try:
                next(gens[i])
            except StopIteration as done:
                results[i] = done.value
                active.remove(i)
    return results


MIX_CHUNKS = 2


def _mixer_kernel(n_ctx_chunks, n_chunks, n_exp,
                  cz_ref, czp_ref, czn_ref, cw_ref, rqkv_ref, gates_ref, sp_ref, dm_ref, qdec_ref,
                  aq_ref, kvp_ref, kvc_ref, kvn_ref, kvx_ref, sink_ref,
                  x_ref, mod_ref, g_ref, w_ref, wrh_ref, wrl_ref, xo_ref, hg_ref, aff_ref, mix_scr):
    t = pl.program_id(1)
    n_steps = pl.num_programs(1) - 1

    @pl.when(t == 0)
    def _():
        mix_scr[...] = jnp.zeros_like(mix_scr)

    tail = _outproj_tail(n_ctx_chunks * CHUNK, n_exp, jnp.maximum(t - 1, 0), mix_scr[...], x_ref, mod_ref, g_ref,
                         w_ref, wrh_ref, wrl_ref, xo_ref, hg_ref, aff_ref)
    j = jnp.minimum(t, n_steps - 1)
    cw = cw_ref[...]

    def values_t(blk):
        return blk[:, 2 * KV_WIDTH:].astype(F32).T.astype(BF16)

    kv_blocks = ([kvp_ref[0]] + [kvc_ref[0, s * CHUNK:(s + 1) * CHUNK, :] for s in range(MIX_CHUNKS)] + [kvn_ref[0]])
    kv_ctx = [kvx_ref[0, t * CHUNK:(t + 1) * CHUNK, :] for t in range(kvx_ref.shape[1] // CHUNK)]
    vt_blocks = [values_t(blk) for blk in kv_blocks]
    vt_ctx = [values_t(blk) for blk in kv_ctx]
    gens = []
    for s in range(MIX_CHUNKS):
        lo, hi = s * CHUNK, (s + 1) * CHUNK
        if s == 0:
            z_last = czp_ref[0, SUBLANES - 1:SUBLANES, CONV_WIDTH:]
        else:
            z_last = cz_ref[0, lo - 1:lo, CONV_WIDTH:]
        if s == MIX_CHUNKS - 1:
            z_first = czn_ref[0, 0:1, CONV_WIDTH:]
        else:
            z_first = cz_ref[0, hi:hi + 1, CONV_WIDTH:]
        window = kv_blocks[s:s + 3] + kv_ctx
        keys = tuple(jnp.concatenate([blk[:, i * KV_WIDTH:(i + 1) * KV_WIDTH] for blk in window], axis=0)
                     for i in (0, 1))
        v_t = jnp.concatenate(vt_blocks[s:s + 3] + vt_ctx, axis=1)
        gens.append(_mix_chunk(j * MIX_CHUNKS + s, n_ctx_chunks, n_chunks, cz_ref[0, lo:hi, :], z_last, z_first,
                               cw, rqkv_ref[0, lo:hi, :], gates_ref[0, lo:hi, :],
                               (sp_ref[0, 0, s], sp_ref[0, 1, s]), dm_ref, qdec_ref, aq_ref[0, lo:hi, :],
                               (keys, v_t), sink_ref))
    mixes = [mix.astype(BF16) for mix in _round_robin(gens + [tail])[:MIX_CHUNKS]]
    mix_scr[...] = jnp.concatenate(mixes, axis=0)


def _mixer(proj, sp, tabs, conv_w, sink_rows, x, mrows, norm_g, w_out_bf, layer, w_router, n_ctx):
    cz, rqkv, gates, aq, kv = proj
    dm, qdec = tabs
    b, tu, _ = cz.shape
    d = x.shape[2]
    n_exp = w_router.shape[1]
    wr = jnp.pad(w_router, ((0, 0), (0, LANES - n_exp)))
    wr_hi = wr.astype(BF16)
    wr_lo = (wr - wr_hi.astype(F32)).astype(BF16)
    n_chunks = tu // CHUNK
    mc = MIX_CHUNKS
    rows = mc * CHUNK
    assert n_chunks % mc == 0 and (n_ctx // CHUNK) % mc == 0
    last = n_chunks - 1
    n_steps = n_chunks // mc
    per8 = rows // SUBLANES
    mixed = lambda t: jnp.minimum(t, n_steps - 1)
    projected = lambda t: jnp.maximum(t - 1, 0)
    cur = lambda w: pl.BlockSpec((1, rows, w), lambda i, t: (i, mixed(t), 0))
    out = lambda r, w: pl.BlockSpec((1, r, w), lambda i, t: (i, projected(t), 0))
    full = lambda shape: pl.BlockSpec(shape, lambda i, t: (0,) * len(shape))
    wcz, wkv = cz.shape[2], kv.shape[2]
    in_specs = [
        cur(wcz),
        pl.BlockSpec((1, SUBLANES, wcz), lambda i, t: (i, jnp.maximum(mixed(t) * per8 - 1, 0), 0)),
        pl.BlockSpec((1, SUBLANES, wcz),
                     lambda i, t: (i, jnp.minimum((mixed(t) + 1) * per8, tu // SUBLANES - 1), 0)),
        full(conv_w.shape),
        cur(rqkv.shape[2]), cur(gates.shape[2]),
        pl.BlockSpec((1, 2, mc, RET_WIDTH, RET_WIDTH), lambda i, t: (i, 0, mixed(t), 0, 0)),
        full(dm.shape), full(qdec.shape),
        cur(ATTN_WIDTH),
        pl.BlockSpec((1, CHUNK, wkv), lambda i, t: (i, jnp.maximum(mixed(t) * mc - 1, 0), 0)),
        cur(wkv),
        pl.BlockSpec((1, CHUNK, wkv), lambda i, t: (i, jnp.minimum((mixed(t) + 1) * mc, last), 0)),
        pl.BlockSpec((1, n_ctx, wkv), lambda i, t: (i, 0, 0)),
        full(sink_rows.shape),
        out(rows, d),
        pl.BlockSpec((1, M_ROWS, d), lambda i, t: (i, 0, 0)),
        full((1, d)),
        pl.BlockSpec((None,) + w_out_bf.shape[1:], lambda i, t: (layer, 0, 0)),
        full((d, LANES)), full((d, LANES)),
    ]
    args = [cz, cz, cz, conv_w, rqkv, gates, sp, dm, qdec, aq, kv, kv, kv, kv, sink_rows,
            x, mrows, norm_g.reshape(1, d), w_out_bf, wr_hi, wr_lo]
    return pl.pallas_call(
        functools.partial(_mixer_kernel, n_ctx // CHUNK, n_chunks, n_exp),
        grid=(b, n_steps + 1),
        in_specs=in_specs,
        out_specs=[out(rows, d), out(rows * SUBLANES, LANES), out(rows, LANES)],
        out_shape=[
            jax.ShapeDtypeStruct((b, tu, d), F32),
            jax.ShapeDtypeStruct((b, tu * SUBLANES, LANES), F32),
            jax.ShapeDtypeStruct((b, tu, LANES), F32),
        ],
        scratch_shapes=[pltpu.VMEM((rows, CONV_WIDTH + RET_WIDTH + ATTN_WIDTH), BF16)],
        compiler_params=_params("parallel", "arbitrary"),
        name="mixer",
    )(*args)


def _outproj_tail(n_ctx, n_exp, tile, mix, x_ref, mod_ref, g_ref, w_ref, wrh_ref, wrl_ref, xo_ref, hg_ref, aff_ref):
    tm = x_ref.shape[1]
    row = _mod_rows(mod_ref, tm, n_ctx, tile)

    y = _dot(mix, w_ref[...])
    yield
    x = x_ref[0] + row(M_G1) * y
    xo_ref[0] = x
    h = (_rms(x) * g_ref[...]) * (1.0 + row(M_SC2)) + row(M_SH2)
    for q in range(SUBLANES):
        hg_ref[0, pl.ds(q, tm, stride=SUBLANES), :] = h[:, q * LANES:(q + 1) * LANES]
    yield
    h_hi = h.astype(BF16)
    h_lo = (h - h_hi.astype(F32)).astype(BF16)
    logits = _dot(h_hi, wrh_ref[...]) + (_dot(h_lo, wrh_ref[...]) + _dot(h_hi, wrl_ref[...]))
    yield
    lane = lax.broadcasted_iota(I32, (1, LANES), 1)
    logits = jnp.where(lane < n_exp, logits, NEG_INF)
    e = jnp.exp(logits - jnp.max(logits, axis=1, keepdims=True))
    aff_ref[0] = e / jnp.sum(e, axis=1, keepdims=True)


ROUTE_TABLES = 4


def _route_stream_tables(v, k, tab_scr, ce_scr, cnt_scr, stream, thr):
    n_exp, n = v.shape
    lane = lax.broadcasted_iota(I32, (1, LANES), 1)
    tri_r = lax.broadcasted_iota(I32, (LANES, LANES), 0)
    tri_c = lax.broadcasted_iota(I32, (LANES, LANES), 1)
    upper = jnp.where(tri_r <= tri_c, 1.0, 0.0).astype(BF16)
    above = pltpu.bitcast(thr + 1, F32)
    floor = pltpu.bitcast(thr, F32)
    n_gt = jnp.sum(jnp.where(v >= above, 1, 0), axis=1, keepdims=True)
    need = (k - n_gt).astype(F32)
    seen_eq = jnp.zeros((n_exp, 1), F32)
    counts = jnp.zeros((n_exp, LANES), F32)
    for j in range(n // LANES):
        vb = v[:, j * LANES:(j + 1) * LANES]
        gt = vb >= above
        eq = jnp.logical_and(vb >= floor, vb < above)
        both = jnp.concatenate([jnp.where(gt, 1.0, 0.0), jnp.where(eq, 1.0, 0.0)], axis=0).astype(BF16)
        pref = _dot(both, upper)
        rank_eq = pref[n_exp:] + seen_eq
        inc = pref[:n_exp] + jnp.minimum(rank_eq, need) - jnp.minimum(seen_eq, need)
        sel = jnp.logical_or(gt, jnp.logical_and(eq, rank_eq <= need))
        seen_eq = rank_eq[:, LANES - 1:LANES]
        counts = counts + jnp.where(lane == j, inc[:, LANES - 1:LANES], 0.0)
        a0 = jnp.where(sel, vb, 0.0)
        t0 = a0.astype(BF16).astype(F32)
        t1 = (a0 - t0).astype(BF16).astype(F32)
        t2 = a0 - t0 - t1
        for t, val in enumerate((inc, t0, t1, t2)):
            tab_scr[stream * ROUTE_TABLES + t, pl.ds(j, n_exp, stride=LANES), :] = val
    through = _dot(counts.astype(BF16), upper)
    for e in range(n_exp):
        ce_scr[stream, e] = through[e:e + 1, :]
        cnt_scr[stream, e] = counts[e:e + 1, :]


def _route_slots(e, stream, k, lo, tab_scr, ce_scr, cnt_scr):
    lane_f = lax.broadcasted_iota(I32, (1, LANES), 1).astype(F32)
    slot = lax.broadcasted_iota(I32, (k, LANES), 0).astype(F32)
    ones_rows = jnp.ones((LANES, LANES), BF16)
    cnt_rows = jnp.broadcast_to(cnt_scr[stream, e], (LANES, LANES)).astype(BF16)
    rows_e = pl.ds(pl.multiple_of(e * LANES, LANES), LANES)
    tab = jnp.concatenate([tab_scr[stream * ROUTE_TABLES + t, rows_e, :] for t in range(ROUTE_TABLES)],
                          axis=1).astype(BF16)
    before = jnp.where(ce_scr[stream, e] <= slot, 1.0, 0.0).astype(BF16)
    blk = _dot_nt(before, ones_rows)
    base = _dot_nt(before, cnt_rows)
    row = _dot(jnp.where(lane_f == blk, 1.0, 0.0).astype(BF16), tab)
    inc = row[:, :LANES]
    aff = row[:, LANES:2 * LANES] + row[:, 2 * LANES:3 * LANES] + row[:, 3 * LANES:]
    local = slot - base
    pos = _dot_nt(jnp.where(inc <= local, 1.0, 0.0).astype(BF16), ones_rows)
    tok = blk * LANES + pos + float(lo)
    gate = jnp.sum(jnp.where(inc == local + 1.0, aff, 0.0), axis=1, keepdims=True)
    return tok, gate


def _route_kernel(n_exp, streams, aff_ref, idx_ref, gate_ref, tab_scr, ce_scr, cnt_scr):
    a = aff_ref[0].T[:n_exp, :]
    tab_scr[...] = jnp.zeros_like(tab_scr)
    vs = [a[:, lo:lo + n] for lo, n, _, _ in streams]

    def search(i, ts):
        out = []
        for t, v, (_, _, k, _) in zip(ts, vs, streams):
            cand = t | jnp.left_shift(jnp.int32(1), 30 - i)
            cnt = jnp.sum(jnp.where(v >= pltpu.bitcast(cand, F32), 1, 0), axis=1, keepdims=True)
            out.append(jnp.where(cnt >= k, cand, t))
        return tuple(out)

    thrs = lax.fori_loop(0, 31, search, tuple(jnp.zeros((n_exp, 1), I32) for _ in streams))
    for st, (v, (_, _, k, _)) in enumerate(zip(vs, streams)):
        _route_stream_tables(v, k, tab_scr, ce_scr, cnt_scr, st, thrs[st])

    slots = idx_ref.shape[3]

    def per_expert(e, carry):
        toks = []
        for st, (lo, _, k, slot0) in enumerate(streams):
            assert slot0 == sum(t.shape[0] for t in toks)
            tok, gate = _route_slots(e, st, k, lo, tab_scr, ce_scr, cnt_scr)
            toks.append(tok)
            gate_ref[0, pl.ds(e, 1), slot0:slot0 + k, :] = gate[None]
        toks.append(jnp.zeros((-slots % LANES, LANES), F32))
        row = jnp.concatenate(toks, axis=0).T[0:1, :slots]
        idx_ref[0, pl.ds(e, 1)] = row.astype(I32)[None]
        return carry

    lax.fori_loop(0, n_exp, per_expert, 0)


def _route(aff, n_exp, n_ctx):
    b, tu, _ = aff.shape
    n_lat = tu - n_ctx
    cap_l = CAPACITY_FACTOR * n_lat // n_exp
    cap_c = CAPACITY_FACTOR * n_ctx // n_exp
    slots = cap_l + cap_c
    assert n_lat // LANES <= LANES and n_ctx // LANES <= LANES
    streams = ((n_ctx, n_lat, cap_l, 0), (0, n_ctx, cap_c, cap_l))
    return pl.pallas_call(
        functools.partial(_route_kernel, n_exp, streams),
        grid=(b,),
        in_specs=[pl.BlockSpec((1, tu, LANES), lambda i: (i, 0, 0))],
        out_specs=[pl.BlockSpec((1, n_exp, 1, slots), lambda i: (i, 0, 0, 0)),
                   pl.BlockSpec((1, n_exp, slots, 1), lambda i: (i, 0, 0, 0))],
        out_shape=[jax.ShapeDtypeStruct((b, n_exp, 1, slots), I32),
                   jax.ShapeDtypeStruct((b, n_exp, slots, 1), F32)],
        scratch_shapes=[pltpu.VMEM((len(streams) * ROUTE_TABLES, n_exp * LANES, LANES), F32),
                        pltpu.VMEM((len(streams), n_exp, 1, LANES), F32),
                        pltpu.VMEM((len(streams), n_exp, 1, LANES), F32)],
        compiler_params=_params("parallel"),
        name="route",
    )(aff)


def _slot_pitch(slots):
    return slots + SUBLANES


def _zero_slot_padding(ref, lead, slots, pitch):
    for q in range(SUBLANES):
        ref[lead + (pl.ds(q * pitch + slots, pitch - slots), slice(None))] = jnp.zeros((pitch - slots, LANES), ref.dtype)


BF16_ROWS = 2 * SUBLANES


def _slot_pitch_bf16(slots):
    return -(-slots // BF16_ROWS) * BF16_ROWS + BF16_ROWS


def _gather_kernel(slots, pitch, pitch_out, idx_ref, h_ref, o_ref, tile_scr):
    for mi in range(slots):
        r = pl.multiple_of(idx_ref[0, 0, mi] * SUBLANES, SUBLANES)
        tile_scr[pl.ds(mi, SUBLANES, stride=pitch), :] = h_ref[0, pl.ds(r, SUBLANES), :]
    _zero_slot_padding(o_ref, (0, 0), slots, pitch_out)
    for q in range(SUBLANES):
        o_ref[0, 0, q * pitch_out:q * pitch_out + slots, :] = tile_scr[q * pitch:q * pitch + slots, :].astype(BF16)


def _gather(hg, idx_rows, n_exp, slots):
    b = hg.shape[0]
    pitch = _slot_pitch(slots)
    pitch_out = _slot_pitch_bf16(slots)
    return pl.pallas_call(
        functools.partial(_gather_kernel, slots, pitch, pitch_out),
        grid=(b, n_exp),
        in_specs=[
            pl.BlockSpec((1, 1, slots), lambda i, e: (i * n_exp + e, 0, 0), memory_space=pltpu.SMEM),
            pl.BlockSpec((1,) + hg.shape[1:], lambda i, e: (i, 0, 0)),
        ],
        out_specs=pl.BlockSpec((1, 1, SUBLANES * pitch_out, LANES), lambda i, e: (i, e, 0, 0)),
        out_shape=jax.ShapeDtypeStruct((b, n_exp, SUBLANES * pitch_out, LANES), BF16),
        scratch_shapes=[pltpu.VMEM((SUBLANES * pitch, LANES), F32)],
        compiler_params=_params("parallel", "arbitrary"),
        name="gather",
    )(idx_rows, hg)


FFN_SAMPLES = 2
FFN_VMEM_LIMIT_BYTES = 60 * 1024 * 1024


def _ffn_kernel(slots, pitch_in, pitch, xs_ref, wg0_ref, wg1_ref, wu0_ref, wu1_ref, wd0_ref, wd1_ref, gate_ref, y_ref,
                wg_bf, wu_bf, wd_bf):
    @pl.when(pl.program_id(1) == 0)
    def _():
        half = wg_bf.shape[0] // 2
        for dst, lo_ref, hi_ref in ((wg_bf, wg0_ref, wg1_ref), (wu_bf, wu0_ref, wu1_ref), (wd_bf, wd0_ref, wd1_ref)):
            dst[:half, :] = lo_ref[0].astype(BF16)
            dst[half:, :] = hi_ref[0].astype(BF16)

    n = xs_ref.shape[0]
    x = jnp.concatenate(
        [jnp.concatenate([xs_ref[s, 0, q * pitch_in:q * pitch_in + slots, :] for q in range(SUBLANES)], axis=1)
         for s in range(n)], axis=0)
    gate = jnp.concatenate([gate_ref[s, 0] for s in range(n)], axis=0)
    a = _dot(x, wg_bf[...])
    u = _dot(x, wu_bf[...])
    y = _dot((_silu(a) * u).astype(BF16), wd_bf[...]) * gate
    for s in range(n):
        _zero_slot_padding(y_ref, (s, 0), slots, pitch)
        for q in range(SUBLANES):
            y_ref[s, 0, q * pitch:q * pitch + slots, :] = y[s * slots:(s + 1) * slots, q * LANES:(q + 1) * LANES]


def _ffn(xs, w_gate, w_up, w_down, layer, gate, slots):
    b, n_exp = xs.shape[:2]
    pitch = _slot_pitch(slots)
    pitch_in = _slot_pitch_bf16(slots)
    d, f = w_gate.shape[2:]
    ns = FFN_SAMPLES
    steps = b // ns
    assert steps == 2 and b % ns == 0
    slot_tile = lambda p: pl.BlockSpec((ns, 1, SUBLANES * p, LANES), lambda e, i: (i, e, 0, 0))

    def half(rows, cols, q):
        return pl.BlockSpec((None, 1, rows // 2, cols),
                            lambda e, i: (layer, jnp.minimum(e + (i > q).astype(jnp.int32), n_exp - 1), q, 0))

    return pl.pallas_call(
        functools.partial(_ffn_kernel, slots, pitch_in, pitch),
        grid=(n_exp, steps),
        in_specs=[
            slot_tile(pitch_in),
            half(d, f, 0), half(d, f, 1), half(d, f, 0), half(d, f, 1), half(f, d, 0), half(f, d, 1),
            pl.BlockSpec((ns, 1, slots, 1), lambda e, i: (i, e, 0, 0)),
        ],
        out_specs=slot_tile(pitch),
        out_shape=jax.ShapeDtypeStruct((b, n_exp, SUBLANES * pitch, LANES), F32),
        scratch_shapes=[pltpu.VMEM((d, f), BF16), pltpu.VMEM((d, f), BF16), pltpu.VMEM((f, d), BF16)],
        compiler_params=_params("arbitrary", "arbitrary", vmem=FFN_VMEM_LIMIT_BYTES),
        name="ffn",
    )(xs, w_gate, w_gate, w_up, w_up, w_down, w_down, gate)


SCATTER_BATCH = 16


def _scatter_kernel(slots, pitch, idx_ref, y_ref, o_ref):
    @pl.when(pl.program_id(1) == 0)
    def _():
        o_ref[...] = jnp.zeros_like(o_ref)

    for m0 in range(0, slots, SCATTER_BATCH):
        rows = [pl.multiple_of(idx_ref[0, 0, m0 + u] * SUBLANES, SUBLANES) for u in range(SCATTER_BATCH)]
        vals = [o_ref[0, pl.ds(rows[u], SUBLANES), :] + y_ref[0, 0, pl.ds(m0 + u, SUBLANES, stride=pitch), :]
                for u in range(SCATTER_BATCH)]
        for u in range(SCATTER_BATCH):
            o_ref[0, pl.ds(rows[u], SUBLANES), :] = vals[u]


def _scatter(y, idx_rows, tu, slots):
    b, n_exp = y.shape[:2]
    pitch = _slot_pitch(slots)
    return pl.pallas_call(
        functools.partial(_scatter_kernel, slots, pitch),
        grid=(b, n_exp),
        in_specs=[
            pl.BlockSpec((1, 1, slots), lambda i, e: (i * n_exp + e, 0, 0), memory_space=pltpu.SMEM),
            pl.BlockSpec((1, 1, SUBLANES * pitch, LANES), lambda i, e: (i, e, 0, 0)),
        ],
        out_specs=pl.BlockSpec((1, tu * SUBLANES, LANES), lambda i, e: (i, 0, 0)),
        out_shape=jax.ShapeDtypeStruct((b, tu * SUBLANES, LANES), F32),
        compiler_params=_params("parallel", "arbitrary"),
        name="scatter",
    )(idx_rows, y)


def _final_kernel(x_ref, f_ref, mod_ref, g_ref, o_ref):
    tm = x_ref.shape[1]
    x = x_ref[0] + mod_ref[0][M_G2:M_G2 + 1, :] * _from_gather_layout(f_ref, tm)
    o_ref[0] = _rms(x) * g_ref[...]


def _final(x, ffn, mrows, final_g, n_ctx):
    b, tu, d = x.shape
    tm = FINAL_TILE
    skip = n_ctx // tm
    return pl.pallas_call(
        _final_kernel,
        grid=(b, (tu - n_ctx) // tm),
        in_specs=[
            pl.BlockSpec((1, tm, d), lambda i, j: (i, j + skip, 0)),
            pl.BlockSpec((1, tm * SUBLANES, LANES), lambda i, j: (i, j + skip, 0)),
            pl.BlockSpec((1, M_ROWS, d), lambda i, j: (i, 0, 0)),
            pl.BlockSpec((1, d), lambda i, j: (0, 0)),
        ],
        out_specs=pl.BlockSpec((1, tm, d), lambda i, j: (i, j, 0)),
        out_shape=jax.ShapeDtypeStruct((b, tu - n_ctx, d), F32),
        compiler_params=_params("parallel", "parallel"),
        name="final_norm",
    )(x, ffn, mrows, final_g.reshape(1, d))


def _rope_tables(n_lat, n_ctx):
    rows = n_lat // GRID_W
    rowp = jnp.repeat(jnp.arange(rows, dtype=F32), GRID_W)
    colp = jnp.tile(jnp.arange(GRID_W, dtype=F32), rows)
    axis_dim = HEAD_DIM // 2
    inv_freq = ROPE_BASE ** (-jnp.arange(0, axis_dim, 2, dtype=F32) / axis_dim)
    ar = rowp[:, None] * inv_freq
    ac = colp[:, None] * inv_freq
    cs = jnp.concatenate([jnp.cos(ar), jnp.cos(ar), jnp.cos(ac), jnp.cos(ac)], axis=1)
    sn = jnp.concatenate([-jnp.sin(ar), jnp.sin(ar), -jnp.sin(ac), jnp.sin(ac)], axis=1)
    reps = LANES // HEAD_DIM
    cs = jnp.concatenate([jnp.ones((n_ctx, LANES), F32), jnp.tile(cs, (1, reps))], axis=0)
    sn = jnp.concatenate([jnp.zeros((n_ctx, LANES), F32), jnp.tile(sn, (1, reps))], axis=0)
    return cs, sn


def kernel(x, c, ctx, c_ctx, w_mod, b_mod, norm1_g, norm2_g, w_in, conv_w, ret_decay_logit, attn_sink,
           w_out, w_router, w_gate, w_up, w_down, final_g):
    b, n_lat, d = x.shape
    n_ctx = ctx.shape[1]
    depth = w_in.shape[0]
    n_exp = w_router.shape[2]
    tu = n_ctx + n_lat
    assert w_in.shape[2] == IN_WIDTH and tu % INPROJ_TILE == 0
    assert n_ctx % FINAL_TILE == 0 and n_lat % FINAL_TILE == 0
    assert b + 1 <= SUBLANES and n_exp == N_EXPERTS

    c_rows = jnp.concatenate([c, c_ctx[None], jnp.zeros((SUBLANES - b - 1, d), F32)], axis=0)
    mods = _mod_vectors(c_rows, w_mod, b_mod).reshape(depth, SUBLANES, 6, d)
    qdec, win, cd, dm = _decay_tables(ret_decay_logit)
    rope_cs, rope_sn = _rope_tables(n_lat, n_ctx)
    sink_rows = jnp.broadcast_to(jnp.repeat(attn_sink.astype(F32), CHUNK, axis=1)[:, :, None],
                                 (depth, ATTN_HEADS * CHUNK, LANES))
    w_in_bf = w_in.astype(BF16)
    w_out_bf = w_out.astype(BF16)

    xu = jnp.concatenate([ctx, x], axis=1)
    ffn = None
    mrows = None
    cap = CAPACITY_FACTOR * n_lat // n_exp + CAPACITY_FACTOR * n_ctx // n_exp
    for l in range(depth):
        prev_mrows = mrows
        mrows = jnp.concatenate([mods[l, :b], jnp.broadcast_to(mods[l, b][None], (b, 6, d)),
                                 jnp.zeros((b, M_ROWS - 12, d), F32)], axis=1)
        xu, proj = _inproj(xu, ffn, mrows if ffn is None else _with_prev_g2(mrows, prev_mrows),
                           norm1_g[l], w_in_bf, l, rope_cs, rope_sn, n_ctx)
        sp = _ret_states(proj[1], win[l], cd[l], n_ctx)
        xu, hg, aff = _mixer(proj, sp, (dm[l], qdec[l]), conv_w[l], sink_rows[l],
                             xu, mrows, norm2_g[l], w_out_bf, l, w_router[l], n_ctx)
        idx, gate = _route(aff, n_exp, n_ctx)
        idx_rows = idx.reshape(b * n_exp, 1, cap)
        xs = _gather(hg, idx_rows, n_exp, cap)
        y = _ffn(xs, w_gate, w_up, w_down, l, gate, cap)
        ffn = _scatter(y, idx_rows, tu, cap)
    return _final(xu, ffn, mrows, final_g, n_ctx)


def _with_prev_g2(mrows, prev_mrows):
    out = mrows.at[:, M_G2].set(prev_mrows[:, M_G2])
    return out.at[:, M_CTX + M_G2].set(prev_mrows[:, M_CTX + M_G2])
```

```python
import functools

import jax
import jax.numpy as jnp
from jax import lax
from jax.experimental import pallas as pl
from jax.experimental.pallas import tpu as pltpu

F32 = jnp.float32
BF16 = jnp.bfloat16
I32 = jnp.int32
HIGHEST = lax.Precision.HIGHEST

HEAD_DIM = 64
CONV_WIDTH = 256
RET_HEADS = 4
RET_WIDTH = RET_HEADS * HEAD_DIM
ATTN_HEADS = 8
ATTN_KV_HEADS = 2
ATTN_GROUP = ATTN_HEADS // ATTN_KV_HEADS
ATTN_WIDTH = ATTN_HEADS * HEAD_DIM
KV_WIDTH = ATTN_KV_HEADS * HEAD_DIM
CHUNK = 128
GRID_W = 64
N_EXPERTS = 16
CAPACITY_FACTOR = 2
ROPE_BASE = 10000.0
EPS = 1e-6
NEG_INF = -1e30
QK_SCALE = HEAD_DIM ** -0.5
LOG2E = 1.4426950408889634

LANES = 128
SUBLANES = 8
VMEM_LIMIT_BYTES = 56 * 1024 * 1024

O_CB = 0
O_CC = O_CB + CONV_WIDTH
O_CX = O_CC + CONV_WIDTH
O_RQ = O_CX + CONV_WIDTH
O_RK = O_RQ + RET_WIDTH
O_RV = O_RK + RET_WIDTH
O_GF = O_RV + RET_WIDTH
O_GB = O_GF + RET_WIDTH
O_AQ = O_GB + RET_WIDTH
O_AK = O_AQ + ATTN_WIDTH
O_AV = O_AK + KV_WIDTH
IN_WIDTH = O_AV + KV_WIDTH

M_SH1, M_SC1, M_G1, M_SH2, M_SC2, M_G2 = range(6)
M_CTX = 6
M_ROWS = 16

INPROJ_TILE = 544
FINAL_TILE = 256


def _params(*sem, vmem=VMEM_LIMIT_BYTES):
    return pltpu.CompilerParams(dimension_semantics=sem, vmem_limit_bytes=vmem)


def _dot(a, b):
    return jnp.dot(a, b, preferred_element_type=F32)


def _dot_nt(a, b):
    return lax.dot_general(a, b, (((1,), (1,)), ((), ())), preferred_element_type=F32)


def _silu(v):
    return v * jax.nn.sigmoid(v)


def _mod_kernel(c_ref, w_ref, b_ref, o_ref):
    s = _silu(c_ref[...])
    o_ref[0] = jnp.dot(s, w_ref[0], precision=HIGHEST, preferred_element_type=F32) + b_ref[0]


def _mod_vectors(c_rows, w_mod, b_mod):
    depth, d_model, width = w_mod.shape
    tn = 1536
    return pl.pallas_call(
        _mod_kernel,
        grid=(depth, width // tn),
        in_specs=[
            pl.BlockSpec((SUBLANES, d_model), lambda l, n: (0, 0)),
            pl.BlockSpec((1, d_model, tn), lambda l, n: (l, 0, n)),
            pl.BlockSpec((1, 1, tn), lambda l, n: (l, 0, n)),
        ],
        out_specs=pl.BlockSpec((1, SUBLANES, tn), lambda l, n: (l, 0, n)),
        out_shape=jax.ShapeDtypeStruct((depth, SUBLANES, width), F32),
        compiler_params=_params("parallel", "parallel"),
        name="mod_vectors",
    )(c_rows, w_mod, b_mod.reshape(depth, 1, width))


def _log_sigmoid(v):
    return -jnp.log(1.0 + jnp.exp(-v))


def _tables_kernel(lgl_ref, lgr_ref, qdec_ref, win_ref, cd_ref, dm_ref):
    pos = lax.broadcasted_iota(I32, (CHUNK, RET_WIDTH), 0).astype(F32)
    ri = lax.broadcasted_iota(I32, (CHUNK, RET_HEADS * CHUNK), 0).astype(F32)
    rj = (lax.broadcasted_iota(I32, (CHUNK, RET_HEADS * CHUNK), 1) & (CHUNK - 1)).astype(F32)
    for d in range(2):
        lg = _log_sigmoid(lgl_ref[0, d])
        lg1 = lg[0:1, :]
        if d == 0:
            qdec_ref[0, d] = jnp.exp(lg1 * (pos + 1.0))
            win_ref[0, d] = jnp.exp(lg1 * (CHUNK - 1.0 - pos))
            diff = ri - rj
        else:
            qdec_ref[0, d] = jnp.exp(lg1 * (CHUNK - pos))
            win_ref[0, d] = jnp.exp(lg1 * pos)
            diff = rj - ri
        cd_ref[0, d] = jnp.exp(lg * float(CHUNK))
        lr = _log_sigmoid(lgr_ref[0, d])
        dm_ref[0, d] = jnp.where(diff >= 0.0, jnp.exp(lr * jnp.maximum(diff, 0.0)), 0.0)


def _decay_tables(ret_decay_logit):
    depth = ret_decay_logit.shape[0]
    lg = ret_decay_logit.astype(F32)
    lgl = jnp.broadcast_to(jnp.repeat(lg, HEAD_DIM, axis=-1)[:, :, None, :], (depth, 2, SUBLANES, RET_WIDTH))
    lgr = jnp.broadcast_to(jnp.repeat(lg, CHUNK, axis=-1)[:, :, None, :], (depth, 2, CHUNK, RET_HEADS * CHUNK))
    return pl.pallas_call(
        _tables_kernel,
        grid=(depth,),
        in_specs=[
            pl.BlockSpec((1, 2, SUBLANES, RET_WIDTH), lambda l: (l, 0, 0, 0)),
            pl.BlockSpec((1, 2, CHUNK, RET_HEADS * CHUNK), lambda l: (l, 0, 0, 0)),
        ],
        out_specs=[
            pl.BlockSpec((1, 2, CHUNK, RET_WIDTH), lambda l: (l, 0, 0, 0)),
            pl.BlockSpec((1, 2, CHUNK, RET_WIDTH), lambda l: (l, 0, 0, 0)),
            pl.BlockSpec((1, 2, SUBLANES, RET_WIDTH), lambda l: (l, 0, 0, 0)),
            pl.BlockSpec((1, 2, CHUNK, RET_HEADS * CHUNK), lambda l: (l, 0, 0, 0)),
        ],
        out_shape=[
            jax.ShapeDtypeStruct((depth, 2, CHUNK, RET_WIDTH), F32),
            jax.ShapeDtypeStruct((depth, 2, CHUNK, RET_WIDTH), F32),
            jax.ShapeDtypeStruct((depth, 2, SUBLANES, RET_WIDTH), F32),
            jax.ShapeDtypeStruct((depth, 2, CHUNK, RET_HEADS * CHUNK), F32),
        ],
        compiler_params=_params("parallel"),
        name="decay_tables",
    )(lgl, lgr)


def _rope(v, cs, sn):
    lane = lax.broadcasted_iota(I32, (1, LANES), 1)
    first = (lane & 31) < 16
    outs = []
    for g in range(v.shape[1] // LANES):
        vg = v[:, g * LANES:(g + 1) * LANES]
        sw = jnp.where(first, pltpu.roll(vg, LANES - 16, 1), pltpu.roll(vg, 16, 1))
        outs.append(vg * cs + sw * sn)
    return outs[0] if len(outs) == 1 else jnp.concatenate(outs, axis=1)


def _swap_halves(v):
    return jnp.concatenate([v[:, HEAD_DIM:], v[:, :HEAD_DIM]], axis=1)


def _from_gather_layout(f_ref, tm):
    return jnp.concatenate(
        [f_ref[0, pl.ds(q, tm, stride=SUBLANES), :] for q in range(SUBLANES)], axis=1)


def _rms(v):
    return v * lax.rsqrt(jnp.mean(v * v, axis=-1, keepdims=True) + EPS)


def _mod_rows(mod_ref, tm, n_ctx, tile):
    m = mod_ref[0]
    if n_ctx % tm == 0:
        is_ctx = tile < n_ctx // tm
    else:
        is_ctx = tile * tm + lax.broadcasted_iota(I32, (tm, 1), 0) < n_ctx

    def row(r):
        return jnp.where(is_ctx, m[M_CTX + r:M_CTX + r + 1, :], m[r:r + 1, :])

    return row


def _inproj_kernel(has_ffn, n_ctx, *refs):
    if has_ffn:
        x_ref, f_ref, mod_ref, g_ref, w_ref, cs_ref, sn_ref = refs[:7]
        outs = refs[7:]
        xo_ref = outs[0]
        outs = outs[1:]
    else:
        x_ref, mod_ref, g_ref, w_ref, cs_ref, sn_ref = refs[:6]
        outs = refs[6:]
    cz_ref, rqkv_ref, gates_ref, aq_ref, kv_ref = outs
    tm = x_ref.shape[1]
    row = _mod_rows(mod_ref, tm, n_ctx, pl.program_id(1))

    x = x_ref[0]
    if has_ffn:
        x = x + row(M_G2) * _from_gather_layout(f_ref, tm)
        xo_ref[0] = x
    h = (_rms(x) * g_ref[...]) * (1.0 + row(M_SC1)) + row(M_SH1)
    h = h.astype(BF16)
    cs = cs_ref[...]
    sn = sn_ref[...]

    def proj(a, b):
        return _dot(h, w_ref[:, a:b])

    cz_ref[0, :, 0:CONV_WIDTH] = proj(O_CB, O_CC)
    cz_ref[0, :, CONV_WIDTH:] = proj(O_CC, O_CX) * proj(O_CX, O_RQ)
    rqkv_ref[0, :, 0:RET_WIDTH] = _rope(proj(O_RQ, O_RK), cs, sn).astype(BF16)
    rqkv_ref[0, :, RET_WIDTH:2 * RET_WIDTH] = _rope(proj(O_RK, O_RV) * QK_SCALE, cs, sn).astype(BF16)
    rqkv_ref[0, :, 2 * RET_WIDTH:] = proj(O_RV, O_GF).astype(BF16)
    gates_ref[0, :, 0:RET_WIDTH] = _silu(proj(O_GF, O_GB))
    gates_ref[0, :, RET_WIDTH:] = _silu(proj(O_GB, O_AQ))
    aq_ref[0] = (_rope(proj(O_AQ, O_AK), cs, sn) * (QK_SCALE * LOG2E)).astype(BF16)
    ak = _rope(proj(O_AK, O_AV), cs, sn)
    av = proj(O_AV, IN_WIDTH)
    kv_ref[0, :, 0:KV_WIDTH] = ak.astype(BF16)
    kv_ref[0, :, KV_WIDTH:2 * KV_WIDTH] = _swap_halves(ak).astype(BF16)
    kv_ref[0, :, 2 * KV_WIDTH:] = av.astype(BF16)


def _inproj(x, ffn, mrows, norm_g, w_in_bf, layer, rope_cs, rope_sn, n_ctx):
    b, tu, d = x.shape
    tm = INPROJ_TILE
    has_ffn = ffn is not None
    tok = lambda width: pl.BlockSpec((1, tm, width), lambda i, j: (i, j, 0))
    in_specs = [tok(d)]
    args = [x]
    if has_ffn:
        in_specs.append(pl.BlockSpec((1, tm * SUBLANES, LANES), lambda i, j: (i, j, 0)))
        args.append(ffn)
    in_specs += [
        pl.BlockSpec((1, M_ROWS, d), lambda i, j: (i, 0, 0)),
        pl.BlockSpec((1, d), lambda i, j: (0, 0)),
        pl.BlockSpec((None, d, IN_WIDTH), lambda i, j: (layer, 0, 0)),
        pl.BlockSpec((tm, LANES), lambda i, j: (j, 0)),
        pl.BlockSpec((tm, LANES), lambda i, j: (j, 0)),
    ]
    args += [mrows, norm_g.reshape(1, d), w_in_bf, rope_cs, rope_sn]
    widths = [(2 * CONV_WIDTH, F32), (3 * RET_WIDTH, BF16), (2 * RET_WIDTH, F32), (ATTN_WIDTH, BF16),
              (3 * KV_WIDTH, BF16)]
    out_specs = [tok(w) for w, _ in widths]
    out_shape = [jax.ShapeDtypeStruct((b, tu, w), dt) for w, dt in widths]
    if has_ffn:
        out_specs = [tok(d)] + out_specs
        out_shape = [jax.ShapeDtypeStruct((b, tu, d), F32)] + out_shape
    res = pl.pallas_call(
        functools.partial(_inproj_kernel, has_ffn, n_ctx),
        grid=(b, tu // tm),
        in_specs=in_specs,
        out_specs=out_specs,
        out_shape=out_shape,
        compiler_params=_params("parallel", "parallel"),
        name="inproj",
    )(*args)
    if has_ffn:
        return res[0], res[1:]
    return x, res


def _head_block_mask(n):
    r = lax.broadcasted_iota(I32, (n, n), 0) // HEAD_DIM
    c = lax.broadcasted_iota(I32, (n, n), 1) // HEAD_DIM
    return r == c


STATE_UNROLL = 2


def _states_kernel(n_ctx_chunks, n_chunks, rqkv_ref, win_ref, cd_ref, sp_ref, s_scr):
    same_head = _head_block_mask(RET_WIDTH)
    s_scr[...] = jnp.zeros_like(s_scr)

    def chunk_update(d, pos):
        if d == 0:
            c = pos
        else:
            c = jnp.where(pos < n_ctx_chunks, n_ctx_chunks - 1 - pos, n_chunks + n_ctx_chunks - 1 - pos)
        off = pl.multiple_of(c * CHUNK, CHUNK)
        kw = rqkv_ref[0, pl.ds(off, CHUNK), RET_WIDTH:2 * RET_WIDTH].astype(F32) * win_ref[d]
        v = rqkv_ref[0, pl.ds(off, CHUNK), 2 * RET_WIDTH:]
        kw_t = kw.T.astype(BF16)
        yield
        u = _dot(kw_t, v)
        yield
        return c, jnp.where(same_head, u, 0.0)

    def body(i, carry):
        todo = [(d, i * STATE_UNROLL + n) for n in range(STATE_UNROLL) for d in range(2)]
        done = _round_robin([chunk_update(d, pos) for d, pos in todo])
        for (d, _), (c, u) in zip(todo, done):
            s = s_scr[d]
            sp_ref[0, d, pl.ds(c, 1)] = s.astype(BF16)[None]
            s_scr[d] = s * cd_ref[d][0:1, :] + u
        return carry

    assert n_chunks % STATE_UNROLL == 0
    lax.fori_loop(0, n_chunks // STATE_UNROLL, body, 0)


def _ret_states(rqkv, win, cd, n_ctx):
    b, tu, _ = rqkv.shape
    w = RET_WIDTH
    n_chunks = tu // CHUNK
    return pl.pallas_call(
        functools.partial(_states_kernel, n_ctx // CHUNK, n_chunks),
        grid=(b,),
        in_specs=[
            pl.BlockSpec((1, tu, 3 * w), lambda i: (i, 0, 0)),
            pl.BlockSpec((2, CHUNK, w), lambda i: (0, 0, 0)),
            pl.BlockSpec((2, SUBLANES, w), lambda i: (0, 0, 0)),
        ],
        out_specs=pl.BlockSpec((1, 2, n_chunks, w, w), lambda i: (i, 0, 0, 0, 0)),
        out_shape=jax.ShapeDtypeStruct((b, 2, n_chunks, w, w), BF16),
        scratch_shapes=[pltpu.VMEM((2, w, w), F32)],
        compiler_params=_params("parallel"),
        name="ret_states",
    )(rqkv, win, cd)


def _group_mean(t, ones_bd):
    hi = t.astype(BF16)
    lo = (t - hi.astype(F32)).astype(BF16)
    s = _dot(jnp.concatenate([hi, lo], axis=0), ones_bd)
    n = t.shape[0]
    return (s[:n] + s[n:]) * (1.0 / HEAD_DIM)


def _mix_chunk(c, n_ctx_chunks, n_chunks, cz, z_last, z_first, cw, rqkv, gates, sp, dm_ref, qdec_ref,
               qa, kvs, sink_ref):
    is_lat = c >= n_ctx_chunks

    z = cz[:, CONV_WIDTH:]
    row = lax.broadcasted_iota(I32, (CHUNK, 1), 0)
    has_prev = jnp.logical_and(c != 0, c != n_ctx_chunks)
    has_next = jnp.logical_and(c != n_ctx_chunks - 1, c != n_chunks - 1)
    z_before = jnp.where(row == 0, jnp.where(has_prev, z_last, 0.0), pltpu.roll(z, 1, 0))
    z_after = jnp.where(row == CHUNK - 1, jnp.where(has_next, z_first, 0.0), pltpu.roll(z, CHUNK - 1, 0))
    conv = cz[:, :CONV_WIDTH] * (z_before * cw[0:1, :] + z * cw[1:2, :] + z_after * cw[2:3, :])

    q = rqkv[:, 0:RET_WIDTH]
    k = rqkv[:, RET_WIDTH:2 * RET_WIDTH]
    v = rqkv[:, 2 * RET_WIDTH:]
    lane_head = lax.broadcasted_iota(I32, (1, RET_WIDTH), 1) // HEAD_DIM
    kz = jnp.zeros_like(k)
    k_heads = jnp.concatenate([jnp.where(lane_head == hh, k, kz) for hh in range(RET_HEADS)], axis=0)
    v_heads = jnp.concatenate([jnp.where(lane_head == hh, v, kz) for hh in range(RET_HEADS)], axis=0)
    scores = _dot_nt(q, k_heads)

    keys, v_t = kvs
    half = lax.broadcasted_iota(I32, (1, LANES), 1) // HEAD_DIM
    qa_z = jnp.zeros((CHUNK, LANES), BF16)
    combos = [(hk, par) for hk in range(ATTN_KV_HEADS) for par in range(2)]
    raw = {}
    for hk, par in combos:
        ja, jb = 2 * hk, 2 * hk + 1
        qst = jnp.concatenate([
            jnp.where(half == par, qa[:, ja * LANES:(ja + 1) * LANES], qa_z),
            jnp.where(half == par, qa[:, jb * LANES:(jb + 1) * LANES], qa_z)], axis=0)
        sel = 0 if par == hk else 1
        raw[hk, par] = _dot_nt(keys[sel], qst)
    yield

    qf = q.astype(F32)
    outs = []
    for d in range(2):
        p = (scores * dm_ref[d]).astype(BF16)
        lhs = jnp.concatenate([p, (qf * qdec_ref[d]).astype(BF16)], axis=1)
        o = _dot(lhs, jnp.concatenate([v_heads, sp[d]], axis=0))
        outs.append(o)
    o2 = jnp.concatenate(outs, axis=0)
    yield

    n_ctx = keys[0].shape[0] - 3 * CHUNK
    cols2 = 2 * CHUNK
    ik = lax.broadcasted_iota(I32, (CHUNK, cols2), 0)
    iq = lax.broadcasted_iota(I32, (CHUNK, cols2), 1) & (CHUNK - 1)
    ok_prev = jnp.logical_and(jnp.logical_and(is_lat, c - 1 >= n_ctx_chunks), ik >= iq)
    ok_cur = jnp.logical_and(is_lat, ik >= 0)
    ok_next = jnp.logical_and(jnp.logical_and(is_lat, c + 1 <= n_chunks - 1), ik <= iq)
    ok_ctx = jnp.ones((n_ctx, cols2), jnp.bool_)
    valid = jnp.concatenate([ok_prev, ok_cur, ok_next, ok_ctx], axis=0)
    att = {}
    for n, (hk, par) in enumerate(combos):
        s = jnp.where(valid, raw[hk, par], NEG_INF)
        ha, hb = ATTN_GROUP * hk + par, ATTN_GROUP * hk + par + 2
        snk = jnp.concatenate([sink_ref[ha * CHUNK:ha * CHUNK + 1, :],
                               sink_ref[hb * CHUNK:hb * CHUNK + 1, :]], axis=1) * LOG2E
        mx = jnp.maximum(jnp.max(s, axis=0, keepdims=True), snk)
        e = jnp.exp2(s - mx)
        den = jnp.sum(e, axis=0, keepdims=True) + jnp.exp2(snk - mx)
        o_t = _dot(v_t[hk * HEAD_DIM:(hk + 1) * HEAD_DIM, :], e.astype(BF16))
        att[hk, par] = o_t * (1.0 / den)
        if n == 0:
            ones_bd = jnp.where(_head_block_mask(RET_WIDTH), 1.0, 0.0).astype(BF16)
            dl = o2 - _group_mean(o2, ones_bd)
            on = dl * lax.rsqrt(_group_mean(dl * dl, ones_bd) + EPS)
            ret = on[:CHUNK] * gates[:, 0:RET_WIDTH] + on[CHUNK:] * gates[:, RET_WIDTH:]
        yield

    cols = []
    for hk in range(ATTN_KV_HEADS):
        cols.append(jnp.concatenate([att[hk, 0][:, :CHUNK], att[hk, 1][:, :CHUNK]], axis=0).T)
        cols.append(jnp.concatenate([att[hk, 0][:, CHUNK:], att[hk, 1][:, CHUNK:]], axis=0).T)
    return jnp.concatenate([conv, ret] + cols, axis=1)


def _round_robin(gens):
    results = [None] * len(gens)
    active = list(range(len(gens)))
    while active:
        for i in list(active):
            try:
                next(gens[i])
            except StopIteration as done:
                results[i] = done.value
                active.remove(i)
    return results


MIX_CHUNKS = 2


def _mixer_kernel(n_ctx_chunks, n_chunks, n_exp,
                  cz_ref, czp_ref, czn_ref, cw_ref, rqkv_ref, gates_ref, sp_ref, dm_ref, qdec_ref,
                  aq_ref, kvp_ref, kvc_ref, kvn_ref, kvx_ref, sink_ref,
                  x_ref, mod_ref, g_ref, w_ref, wrh_ref, wrl_ref, xo_ref, hg_ref, aff_ref, mix_scr):
    t = pl.program_id(1)
    n_steps = pl.num_programs(1) - 1

    @pl.when(t == 0)
    def _():
        mix_scr[...] = jnp.zeros_like(mix_scr)

    tail = _outproj_tail(n_ctx_chunks * CHUNK, n_exp, jnp.maximum(t - 1, 0), mix_scr[...], x_ref, mod_ref, g_ref,
                         w_ref, wrh_ref, wrl_ref, xo_ref, hg_ref, aff_ref)
    j = jnp.minimum(t, n_steps - 1)
    cw = cw_ref[...]

    def values_t(blk):
        return blk[:, 2 * KV_WIDTH:].astype(F32).T.astype(BF16)

    kv_blocks = ([kvp_ref[0]] + [kvc_ref[0, s * CHUNK:(s + 1) * CHUNK, :] for s in range(MIX_CHUNKS)] + [kvn_ref[0]])
    kv_ctx = [kvx_ref[0, t * CHUNK:(t + 1) * CHUNK, :] for t in range(kvx_ref.shape[1] // CHUNK)]
    vt_blocks = [values_t(blk) for blk in kv_blocks]
    vt_ctx = [values_t(blk) for blk in kv_ctx]
    gens = []
    for s in range(MIX_CHUNKS):
        lo, hi = s * CHUNK, (s + 1) * CHUNK
        if s == 0:
            z_last = czp_ref[0, SUBLANES - 1:SUBLANES, CONV_WIDTH:]
        else:
            z_last = cz_ref[0, lo - 1:lo, CONV_WIDTH:]
        if s == MIX_CHUNKS - 1:
            z_first = czn_ref[0, 0:1, CONV_WIDTH:]
        else:
            z_first = cz_ref[0, hi:hi + 1, CONV_WIDTH:]
        window = kv_blocks[s:s + 3] + kv_ctx
        keys = tuple(jnp.concatenate([blk[:, i * KV_WIDTH:(i + 1) * KV_WIDTH] for blk in window], axis=0)
                     for i in (0, 1))
        v_t = jnp.concatenate(vt_blocks[s:s + 3] + vt_ctx, axis=1)
        gens.append(_mix_chunk(j * MIX_CHUNKS + s, n_ctx_chunks, n_chunks, cz_ref[0, lo:hi, :], z_last, z_first,
                               cw, rqkv_ref[0, lo:hi, :], gates_ref[0, lo:hi, :],
                               (sp_ref[0, 0, s], sp_ref[0, 1, s]), dm_ref, qdec_ref, aq_ref[0, lo:hi, :],
                               (keys, v_t), sink_ref))
    mixes = [mix.astype(BF16) for mix in _round_robin(gens + [tail])[:MIX_CHUNKS]]
    mix_scr[...] = jnp.concatenate(mixes, axis=0)


def _mixer(proj, sp, tabs, conv_w, sink_rows, x, mrows, norm_g, w_out_bf, layer, w_router, n_ctx):
    cz, rqkv, gates, aq, kv = proj
    dm, qdec = tabs
    b, tu, _ = cz.shape
    d = x.shape[2]
    n_exp = w_router.shape[1]
    wr = jnp.pad(w_router, ((0, 0), (0, LANES - n_exp)))
    wr_hi = wr.astype(BF16)
    wr_lo = (wr - wr_hi.astype(F32)).astype(BF16)
    n_chunks = tu // CHUNK
    mc = MIX_CHUNKS
    rows = mc * CHUNK
    assert n_chunks % mc == 0 and (n_ctx // CHUNK) % mc == 0
    last = n_chunks - 1
    n_steps = n_chunks // mc
    per8 = rows // SUBLANES
    mixed = lambda t: jnp.minimum(t, n_steps - 1)
    projected = lambda t: jnp.maximum(t - 1, 0)
    cur = lambda w: pl.BlockSpec((1, rows, w), lambda i, t: (i, mixed(t), 0))
    out = lambda r, w: pl.BlockSpec((1, r, w), lambda i, t: (i, projected(t), 0))
    full = lambda shape: pl.BlockSpec(shape, lambda i, t: (0,) * len(shape))
    wcz, wkv = cz.shape[2], kv.shape[2]
    in_specs = [
        cur(wcz),
        pl.BlockSpec((1, SUBLANES, wcz), lambda i, t: (i, jnp.maximum(mixed(t) * per8 - 1, 0), 0)),
        pl.BlockSpec((1, SUBLANES, wcz),
                     lambda i, t: (i, jnp.minimum((mixed(t) + 1) * per8, tu // SUBLANES - 1), 0)),
        full(conv_w.shape),
        cur(rqkv.shape[2]), cur(gates.shape[2]),
        pl.BlockSpec((1, 2, mc, RET_WIDTH, RET_WIDTH), lambda i, t: (i, 0, mixed(t), 0, 0)),
        full(dm.shape), full(qdec.shape),
        cur(ATTN_WIDTH),
        pl.BlockSpec((1, CHUNK, wkv), lambda i, t: (i, jnp.maximum(mixed(t) * mc - 1, 0), 0)),
        cur(wkv),
        pl.BlockSpec((1, CHUNK, wkv), lambda i, t: (i, jnp.minimum((mixed(t) + 1) * mc, last), 0)),
        pl.BlockSpec((1, n_ctx, wkv), lambda i, t: (i, 0, 0)),
        full(sink_rows.shape),
        out(rows, d),
        pl.BlockSpec((1, M_ROWS, d), lambda i, t: (i, 0, 0)),
        full((1, d)),
        pl.BlockSpec((None,) + w_out_bf.shape[1:], lambda i, t: (layer, 0, 0)),
        full((d, LANES)), full((d, LANES)),
    ]
    args = [cz, cz, cz, conv_w, rqkv, gates, sp, dm, qdec, aq, kv, kv, kv, kv, sink_rows,
            x, mrows, norm_g.reshape(1, d), w_out_bf, wr_hi, wr_lo]
    return pl.pallas_call(
        functools.partial(_mixer_kernel, n_ctx // CHUNK, n_chunks, n_exp),
        grid=(b, n_steps + 1),
        in_specs=in_specs,
        out_specs=[out(rows, d), out(rows * SUBLANES, LANES), out(rows, LANES)],
        out_shape=[
            jax.ShapeDtypeStruct((b, tu, d), F32),
            jax.ShapeDtypeStruct((b, tu * SUBLANES, LANES), F32),
            jax.ShapeDtypeStruct((b, tu, LANES), F32),
        ],
        scratch_shapes=[pltpu.VMEM((rows, CONV_WIDTH + RET_WIDTH + ATTN_WIDTH), BF16)],
        compiler_params=_params("parallel", "arbitrary"),
        name="mixer",
    )(*args)


def _outproj_tail(n_ctx, n_exp, tile, mix, x_ref, mod_ref, g_ref, w_ref, wrh_ref, wrl_ref, xo_ref, hg_ref, aff_ref):
    tm = x_ref.shape[1]
    row = _mod_rows(mod_ref, tm, n_ctx, tile)

    y = _dot(mix, w_ref[...])
    yield
    x = x_ref[0] + row(M_G1) * y
    xo_ref[0] = x
    h = (_rms(x) * g_ref[...]) * (1.0 + row(M_SC2)) + row(M_SH2)
    for q in range(SUBLANES):
        hg_ref[0, pl.ds(q, tm, stride=SUBLANES), :] = h[:, q * LANES:(q + 1) * LANES]
    yield
    h_hi = h.astype(BF16)
    h_lo = (h - h_hi.astype(F32)).astype(BF16)
    logits = _dot(h_hi, wrh_ref[...]) + (_dot(h_lo, wrh_ref[...]) + _dot(h_hi, wrl_ref[...]))
    yield
    lane = lax.broadcasted_iota(I32, (1, LANES), 1)
    logits = jnp.where(lane < n_exp, logits, NEG_INF)
    e = jnp.exp(logits - jnp.max(logits, axis=1, keepdims=True))
    aff_ref[0] = e / jnp.sum(e, axis=1, keepdims=True)


ROUTE_TABLES = 4
ROUTE_GROUP = 4


def _route_stream_tables(v, k, tab_scr, ce_scr, cnt_scr, stream, thr):
    n_exp, n = v.shape
    lane = lax.broadcasted_iota(I32, (1, LANES), 1)
    tri_r = lax.broadcasted_iota(I32, (LANES, LANES), 0)
    tri_c = lax.broadcasted_iota(I32, (LANES, LANES), 1)
    upper = jnp.where(tri_r <= tri_c, 1.0, 0.0).astype(BF16)
    above = pltpu.bitcast(thr + 1, F32)
    floor = pltpu.bitcast(thr, F32)
    n_gt = jnp.sum(jnp.where(v >= above, 1, 0), axis=1, keepdims=True)
    need = (k - n_gt).astype(F32)
    seen_eq = jnp.zeros((n_exp, 1), F32)
    counts = jnp.zeros((n_exp, LANES), F32)
    for j in range(n // LANES):
        vb = v[:, j * LANES:(j + 1) * LANES]
        gt = vb >= above
        eq = jnp.logical_and(vb >= floor, vb < above)
        both = jnp.concatenate([jnp.where(gt, 1.0, 0.0), jnp.where(eq, 1.0, 0.0)], axis=0).astype(BF16)
        pref = _dot(both, upper)
        rank_eq = pref[n_exp:] + seen_eq
        inc = pref[:n_exp] + jnp.minimum(rank_eq, need) - jnp.minimum(seen_eq, need)
        sel = jnp.logical_or(gt, jnp.logical_and(eq, rank_eq <= need))
        seen_eq = rank_eq[:, LANES - 1:LANES]
        counts = counts + jnp.where(lane == j, inc[:, LANES - 1:LANES], 0.0)
        a0 = jnp.where(sel, vb, 0.0)
        t0 = a0.astype(BF16).astype(F32)
        t1 = (a0 - t0).astype(BF16).astype(F32)
        t2 = a0 - t0 - t1
        for t, val in enumerate((inc, t0, t1, t2)):
            tab_scr[stream * ROUTE_TABLES + t, pl.ds(j, n_exp, stride=LANES), :] = val
    through = _dot(counts.astype(BF16), upper)
    for e in range(n_exp):
        ce_scr[stream, e] = through[e:e + 1, :]
        cnt_scr[stream, e] = counts[e:e + 1, :]


def _route_slots(e, stream, k, lo, tab_scr, ce_scr, cnt_scr):
    lane_f = lax.broadcasted_iota(I32, (1, LANES), 1).astype(F32)
    slot = lax.broadcasted_iota(I32, (k, LANES), 0).astype(F32)
    ones_rows = jnp.ones((LANES, LANES), BF16)
    cnt_rows = jnp.broadcast_to(cnt_scr[stream, e], (LANES, LANES)).astype(BF16)
    rows_e = pl.ds(pl.multiple_of(e * LANES, LANES), LANES)
    tab = jnp.concatenate([tab_scr[stream * ROUTE_TABLES + t, rows_e, :] for t in range(ROUTE_TABLES)],
                          axis=1).astype(BF16)
    before = jnp.where(ce_scr[stream, e] <= slot, 1.0, 0.0).astype(BF16)
    blk = _dot_nt(before, ones_rows)
    base = _dot_nt(before, cnt_rows)
    yield
    row = _dot(jnp.where(lane_f == blk, 1.0, 0.0).astype(BF16), tab)
    yield
    inc = row[:, :LANES]
    aff = row[:, LANES:2 * LANES] + row[:, 2 * LANES:3 * LANES] + row[:, 3 * LANES:]
    local = slot - base
    pos = _dot_nt(jnp.where(inc <= local, 1.0, 0.0).astype(BF16), ones_rows)
    yield
    tok = blk * LANES + pos + float(lo)
    gate = jnp.sum(jnp.where(inc == local + 1.0, aff, 0.0), axis=1, keepdims=True)
    return tok, gate


def _route_kernel(n_exp, streams, aff_ref, idx_ref, gate_ref, tab_scr, ce_scr, cnt_scr):
    a = aff_ref[0].T[:n_exp, :]
    tab_scr[...] = jnp.zeros_like(tab_scr)
    vs = [a[:, lo:lo + n] for lo, n, _, _ in streams]

    def search(i, ts):
        out = []
        for t, v, (_, _, k, _) in zip(ts, vs, streams):
            cand = t | jnp.left_shift(jnp.int32(1), 30 - i)
            cnt = jnp.sum(jnp.where(v >= pltpu.bitcast(cand, F32), 1, 0), axis=1, keepdims=True)
            out.append(jnp.where(cnt >= k, cand, t))
        return tuple(out)

    thrs = lax.fori_loop(0, 31, search, tuple(jnp.zeros((n_exp, 1), I32) for _ in streams))
    for st, (v, (_, _, k, _)) in enumerate(zip(vs, streams)):
        _route_stream_tables(v, k, tab_scr, ce_scr, cnt_scr, st, thrs[st])

    slots = idx_ref.shape[3]

    assert [s[3] for s in streams] == [sum(s[2] for s in streams[:n]) for n in range(len(streams))]

    def expert_group(g, carry):
        experts = [g * ROUTE_GROUP + u for u in range(ROUTE_GROUP)]
        found = _round_robin([_route_slots(e, st, k, lo, tab_scr, ce_scr, cnt_scr)
                              for e in experts for st, (lo, _, k, _) in enumerate(streams)])
        for u, e in enumerate(experts):
            toks = []
            for st, (_, _, k, slot0) in enumerate(streams):
                tok, gate = found[u * len(streams) + st]
                toks.append(tok)
                gate_ref[0, pl.ds(e, 1), slot0:slot0 + k, :] = gate[None]
            toks.append(jnp.zeros((-slots % LANES, LANES), F32))
            row = jnp.concatenate(toks, axis=0).T[0:1, :slots]
            idx_ref[0, pl.ds(e, 1)] = row.astype(I32)[None]
        return carry

    assert n_exp % ROUTE_GROUP == 0
    lax.fori_loop(0, n_exp // ROUTE_GROUP, expert_group, 0)


def _route(aff, n_exp, n_ctx):
    b, tu, _ = aff.shape
    n_lat = tu - n_ctx
    cap_l = CAPACITY_FACTOR * n_lat // n_exp
    cap_c = CAPACITY_FACTOR * n_ctx // n_exp
    slots = cap_l + cap_c
    assert n_lat // LANES <= LANES and n_ctx // LANES <= LANES
    streams = ((n_ctx, n_lat, cap_l, 0), (0, n_ctx, cap_c, cap_l))
    return pl.pallas_call(
        functools.partial(_route_kernel, n_exp, streams),
        grid=(b,),
        in_specs=[pl.BlockSpec((1, tu, LANES), lambda i: (i, 0, 0))],
        out_specs=[pl.BlockSpec((1, n_exp, 1, slots), lambda i: (i, 0, 0, 0)),
                   pl.BlockSpec((1, n_exp, slots, 1), lambda i: (i, 0, 0, 0))],
        out_shape=[jax.ShapeDtypeStruct((b, n_exp, 1, slots), I32),
                   jax.ShapeDtypeStruct((b, n_exp, slots, 1), F32)],
        scratch_shapes=[pltpu.VMEM((len(streams) * ROUTE_TABLES, n_exp * LANES, LANES), F32),
                        pltpu.VMEM((len(streams), n_exp, 1, LANES), F32),
                        pltpu.VMEM((len(streams), n_exp, 1, LANES), F32)],
        compiler_params=_params("parallel"),
        name="route",
    )(aff)


def _slot_pitch(slots):
    return slots + SUBLANES


def _zero_slot_padding(ref, lead, slots, pitch):
    for q in range(SUBLANES):
        ref[lead + (pl.ds(q * pitch + slots, pitch - slots), slice(None))] = jnp.zeros((pitch - slots, LANES), ref.dtype)


BF16_ROWS = 2 * SUBLANES


def _slot_pitch_bf16(slots):
    return -(-slots // BF16_ROWS) * BF16_ROWS + BF16_ROWS


def _gather_kernel(slots, pitch, pitch_out, idx_ref, h_ref, o_ref, tile_scr):
    for mi in range(slots):
        r = pl.multiple_of(idx_ref[0, 0, mi] * SUBLANES, SUBLANES)
        tile_scr[pl.ds(mi, SUBLANES, stride=pitch), :] = h_ref[0, pl.ds(r, SUBLANES), :]
    _zero_slot_padding(o_ref, (0, 0), slots, pitch_out)
    for q in range(SUBLANES):
        o_ref[0, 0, q * pitch_out:q * pitch_out + slots, :] = tile_scr[q * pitch:q * pitch + slots, :].astype(BF16)


def _gather(hg, idx_rows, n_exp, slots):
    b = hg.shape[0]
    pitch = _slot_pitch(slots)
    pitch_out = _slot_pitch_bf16(slots)
    return pl.pallas_call(
        functools.partial(_gather_kernel, slots, pitch, pitch_out),
        grid=(b, n_exp),
        in_specs=[
            pl.BlockSpec((1, 1, slots), lambda i, e: (i * n_exp + e, 0, 0), memory_space=pltpu.SMEM),
            pl.BlockSpec((1,) + hg.shape[1:], lambda i, e: (i, 0, 0)),
        ],
        out_specs=pl.BlockSpec((1, 1, SUBLANES * pitch_out, LANES), lambda i, e: (i, e, 0, 0)),
        out_shape=jax.ShapeDtypeStruct((b, n_exp, SUBLANES * pitch_out, LANES), BF16),
        scratch_shapes=[pltpu.VMEM((SUBLANES * pitch, LANES), F32)],
        compiler_params=_params("parallel", "arbitrary"),
        name="gather",
    )(idx_rows, hg)


FFN_SAMPLES = 2
FFN_VMEM_LIMIT_BYTES = 60 * 1024 * 1024


def _ffn_kernel(slots, pitch_in, pitch, xs_ref, wg0_ref, wg1_ref, wu0_ref, wu1_ref, wd0_ref, wd1_ref, gate_ref, y_ref,
                wg_bf, wu_bf, wd_bf):
    @pl.when(pl.program_id(1) == 0)
    def _():
        half = wg_bf.shape[0] // 2
        for dst, lo_ref, hi_ref in ((wg_bf, wg0_ref, wg1_ref), (wu_bf, wu0_ref, wu1_ref), (wd_bf, wd0_ref, wd1_ref)):
            dst[:half, :] = lo_ref[0].astype(BF16)
            dst[half:, :] = hi_ref[0].astype(BF16)

    n = xs_ref.shape[0]
    x = jnp.concatenate(
        [jnp.concatenate([xs_ref[s, 0, q * pitch_in:q * pitch_in + slots, :] for q in range(SUBLANES)], axis=1)
         for s in range(n)], axis=0)
    gate = jnp.concatenate([gate_ref[s, 0] for s in range(n)], axis=0)
    a = _dot(x, wg_bf[...])
    u = _dot(x, wu_bf[...])
    y = _dot((_silu(a) * u).astype(BF16), wd_bf[...]) * gate
    for s in range(n):
        _zero_slot_padding(y_ref, (s, 0), slots, pitch)
        for q in range(SUBLANES):
            y_ref[s, 0, q * pitch:q * pitch + slots, :] = y[s * slots:(s + 1) * slots, q * LANES:(q + 1) * LANES]


def _ffn(xs, w_gate, w_up, w_down, layer, gate, slots):
    b, n_exp = xs.shape[:2]
    pitch = _slot_pitch(slots)
    pitch_in = _slot_pitch_bf16(slots)
    d, f = w_gate.shape[2:]
    ns = FFN_SAMPLES
    steps = b // ns
    assert steps == 2 and b % ns == 0
    slot_tile = lambda p: pl.BlockSpec((ns, 1, SUBLANES * p, LANES), lambda e, i: (i, e, 0, 0))

    def half(rows, cols, q):
        return pl.BlockSpec((None, 1, rows // 2, cols),
                            lambda e, i: (layer, jnp.minimum(e + (i > q).astype(jnp.int32), n_exp - 1), q, 0))

    return pl.pallas_call(
        functools.partial(_ffn_kernel, slots, pitch_in, pitch),
        grid=(n_exp, steps),
        in_specs=[
            slot_tile(pitch_in),
            half(d, f, 0), half(d, f, 1), half(d, f, 0), half(d, f, 1), half(f, d, 0), half(f, d, 1),
            pl.BlockSpec((ns, 1, slots, 1), lambda e, i: (i, e, 0, 0)),
        ],
        out_specs=slot_tile(pitch),
        out_shape=jax.ShapeDtypeStruct((b, n_exp, SUBLANES * pitch, LANES), F32),
        scratch_shapes=[pltpu.VMEM((d, f), BF16), pltpu.VMEM((d, f), BF16), pltpu.VMEM((f, d), BF16)],
        compiler_params=_params("arbitrary", "arbitrary", vmem=FFN_VMEM_LIMIT_BYTES),
        name="ffn",
    )(xs, w_gate, w_gate, w_up, w_up, w_down, w_down, gate)


SCATTER_BATCH = 16


def _scatter_kernel(slots, pitch, idx_ref, y_ref, o_ref):
    @pl.when(pl.program_id(1) == 0)
    def _():
        o_ref[...] = jnp.zeros_like(o_ref)

    for m0 in range(0, slots, SCATTER_BATCH):
        rows = [pl.multiple_of(idx_ref[0, 0, m0 + u] * SUBLANES, SUBLANES) for u in range(SCATTER_BATCH)]
        vals = [o_ref[0, pl.ds(rows[u], SUBLANES), :] + y_ref[0, 0, pl.ds(m0 + u, SUBLANES, stride=pitch), :]
                for u in range(SCATTER_BATCH)]
        for u in range(SCATTER_BATCH):
            o_ref[0, pl.ds(rows[u], SUBLANES), :] = vals[u]


def _scatter(y, idx_rows, tu, slots):
    b, n_exp = y.shape[:2]
    pitch = _slot_pitch(slots)
    return pl.pallas_call(
        functools.partial(_scatter_kernel, slots, pitch),
        grid=(b, n_exp),
        in_specs=[
            pl.BlockSpec((1, 1, slots), lambda i, e: (i * n_exp + e, 0, 0), memory_space=pltpu.SMEM),
            pl.BlockSpec((1, 1, SUBLANES * pitch, LANES), lambda i, e: (i, e, 0, 0)),
        ],
        out_specs=pl.BlockSpec((1, tu * SUBLANES, LANES), lambda i, e: (i, 0, 0)),
        out_shape=jax.ShapeDtypeStruct((b, tu * SUBLANES, LANES), F32),
        compiler_params=_params("parallel", "arbitrary"),
        name="scatter",
    )(idx_rows, y)


def _final_kernel(x_ref, f_ref, mod_ref, g_ref, o_ref):
    tm = x_ref.shape[1]
    x = x_ref[0] + mod_ref[0][M_G2:M_G2 + 1, :] * _from_gather_layout(f_ref, tm)
    o_ref[0] = _rms(x) * g_ref[...]


def _final(x, ffn, mrows, final_g, n_ctx):
    b, tu, d = x.shape
    tm = FINAL_TILE
    skip = n_ctx // tm
    return pl.pallas_call(
        _final_kernel,
        grid=(b, (tu - n_ctx) // tm),
        in_specs=[
            pl.BlockSpec((1, tm, d), lambda i, j: (i, j + skip, 0)),
            pl.BlockSpec((1, tm * SUBLANES, LANES), lambda i, j: (i, j + skip, 0)),
            pl.BlockSpec((1, M_ROWS, d), lambda i, j: (i, 0, 0)),
            pl.BlockSpec((1, d), lambda i, j: (0, 0)),
        ],
        out_specs=pl.BlockSpec((1, tm, d), lambda i, j: (i, j, 0)),
        out_shape=jax.ShapeDtypeStruct((b, tu - n_ctx, d), F32),
        compiler_params=_params("parallel", "parallel"),
        name="final_norm",
    )(x, ffn, mrows, final_g.reshape(1, d))


def _rope_tables(n_lat, n_ctx):
    rows = n_lat // GRID_W
    rowp = jnp.repeat(jnp.arange(rows, dtype=F32), GRID_W)
    colp = jnp.tile(jnp.arange(GRID_W, dtype=F32), rows)
    axis_dim = HEAD_DIM // 2
    inv_freq = ROPE_BASE ** (-jnp.arange(0, axis_dim, 2, dtype=F32) / axis_dim)
    ar = rowp[:, None] * inv_freq
    ac = colp[:, None] * inv_freq
    cs = jnp.concatenate([jnp.cos(ar), jnp.cos(ar), jnp.cos(ac), jnp.cos(ac)], axis=1)
    sn = jnp.concatenate([-jnp.sin(ar), jnp.sin(ar), -jnp.sin(ac), jnp.sin(ac)], axis=1)
    reps = LANES // HEAD_DIM
    cs = jnp.concatenate([jnp.ones((n_ctx, LANES), F32), jnp.tile(cs, (1, reps))], axis=0)
    sn = jnp.concatenate([jnp.zeros((n_ctx, LANES), F32), jnp.tile(sn, (1, reps))], axis=0)
    return cs, sn


def kernel(x, c, ctx, c_ctx, w_mod, b_mod, norm1_g, norm2_g, w_in, conv_w, ret_decay_logit, attn_sink,
           w_out, w_router, w_gate, w_up, w_down, final_g):
    b, n_lat, d = x.shape
    n_ctx = ctx.shape[1]
    depth = w_in.shape[0]
    n_exp = w_router.shape[2]
    tu = n_ctx + n_lat
    assert w_in.shape[2] == IN_WIDTH and tu % INPROJ_TILE == 0
    assert n_ctx % FINAL_TILE == 0 and n_lat % FINAL_TILE == 0
    assert b + 1 <= SUBLANES and n_exp == N_EXPERTS

    c_rows = jnp.concatenate([c, c_ctx[None], jnp.zeros((SUBLANES - b - 1, d), F32)], axis=0)
    mods = _mod_vectors(c_rows, w_mod, b_mod).reshape(depth, SUBLANES, 6, d)
    qdec, win, cd, dm = _decay_tables(ret_decay_logit)
    rope_cs, rope_sn = _rope_tables(n_lat, n_ctx)
    sink_rows = jnp.broadcast_to(jnp.repeat(attn_sink.astype(F32), CHUNK, axis=1)[:, :, None],
                                 (depth, ATTN_HEADS * CHUNK, LANES))
    w_in_bf = w_in.astype(BF16)
    w_out_bf = w_out.astype(BF16)

    xu = jnp.concatenate([ctx, x], axis=1)
    ffn = None
    mrows = None
    cap = CAPACITY_FACTOR * n_lat // n_exp + CAPACITY_FACTOR * n_ctx // n_exp
    for l in range(depth):
        prev_mrows = mrows
        mrows = jnp.concatenate([mods[l, :b], jnp.broadcast_to(mods[l, b][None], (b, 6, d)),
                                 jnp.zeros((b, M_ROWS - 12, d), F32)], axis=1)
        xu, proj = _inproj(xu, ffn, mrows if ffn is None else _with_prev_g2(mrows, prev_mrows),
                           norm1_g[l], w_in_bf, l, rope_cs, rope_sn, n_ctx)
        sp = _ret_states(proj[1], win[l], cd[l], n_ctx)
        xu, hg, aff = _mixer(proj, sp, (dm[l], qdec[l]), conv_w[l], sink_rows[l],
                             xu, mrows, norm2_g[l], w_out_bf, l, w_router[l], n_ctx)
        idx, gate = _route(aff, n_exp, n_ctx)
        idx_rows = idx.reshape(b * n_exp, 1, cap)
        xs = _gather(hg, idx_rows, n_exp, cap)
        y = _ffn(xs, w_gate, w_up, w_down, l, gate, cap)
        ffn = _scatter(y, idx_rows, tu, cap)
    return _final(xu, ffn, mrows, final_g, n_ctx)


def _with_prev_g2(mrows, prev_mrows):
    out = mrows.at[:, M_G2].set(prev_mrows[:, M_G2])
    return out.at[:, M_CTX + M_G2].set(prev_mrows[:, M_CTX + M_G2])
```

```python
import functools

import jax
import jax.numpy as jnp
from jax import lax
from jax.experimental import pallas as pl
from jax.experimental.pallas import tpu as pltpu

F32 = jnp.float32
BF16 = jnp.bfloat16
I32 = jnp.int32
HIGHEST = lax.Precision.HIGHEST

HEAD_DIM = 64
CONV_WIDTH = 256
RET_HEADS = 4
RET_WIDTH = RET_HEADS * HEAD_DIM
ATTN_HEADS = 8
ATTN_KV_HEADS = 2
ATTN_GROUP = ATTN_HEADS // ATTN_KV_HEADS
ATTN_WIDTH = ATTN_HEADS * HEAD_DIM
KV_WIDTH = ATTN_KV_HEADS * HEAD_DIM
CHUNK = 128
GRID_W = 64
N_EXPERTS = 16
CAPACITY_FACTOR = 2
ROPE_BASE = 10000.0
EPS = 1e-6
NEG_INF = -1e30
QK_SCALE = HEAD_DIM ** -0.5
LOG2E = 1.4426950408889634

LANES = 128
SUBLANES = 8
VMEM_LIMIT_BYTES = 56 * 1024 * 1024

O_CB = 0
O_CC = O_CB + CONV_WIDTH
O_CX = O_CC + CONV_WIDTH
O_RQ = O_CX + CONV_WIDTH
O_RK = O_RQ + RET_WIDTH
O_RV = O_RK + RET_WIDTH
O_GF = O_RV + RET_WIDTH
O_GB = O_GF + RET_WIDTH
O_AQ = O_GB + RET_WIDTH
O_AK = O_AQ + ATTN_WIDTH
O_AV = O_AK + KV_WIDTH
IN_WIDTH = O_AV + KV_WIDTH

M_SH1, M_SC1, M_G1, M_SH2, M_SC2, M_G2 = range(6)
M_CTX = 6
M_ROWS = 16

INPROJ_TILE = 544
FINAL_TILE = 256


def _params(*sem, vmem=VMEM_LIMIT_BYTES):
    return pltpu.CompilerParams(dimension_semantics=sem, vmem_limit_bytes=vmem)


def _dot(a, b):
    return jnp.dot(a, b, preferred_element_type=F32)


def _dot_nt(a, b):
    return lax.dot_general(a, b, (((1,), (1,)), ((), ())), preferred_element_type=F32)


def _silu(v):
    return v * jax.nn.sigmoid(v)


def _mod_kernel(c_ref, w_ref, b_ref, o_ref):
    s = _silu(c_ref[...])
    o_ref[0] = jnp.dot(s, w_ref[0], precision=HIGHEST, preferred_element_type=F32) + b_ref[0]


def _mod_vectors(c_rows, w_mod, b_mod):
    depth, d_model, width = w_mod.shape
    tn = 1536
    return pl.pallas_call(
        _mod_kernel,
        grid=(depth, width // tn),
        in_specs=[
            pl.BlockSpec((SUBLANES, d_model), lambda l, n: (0, 0)),
            pl.BlockSpec((1, d_model, tn), lambda l, n: (l, 0, n)),
            pl.BlockSpec((1, 1, tn), lambda l, n: (l, 0, n)),
        ],
        out_specs=pl.BlockSpec((1, SUBLANES, tn), lambda l, n: (l, 0, n)),
        out_shape=jax.ShapeDtypeStruct((depth, SUBLANES, width), F32),
        compiler_params=_params("parallel", "parallel"),
        name="mod_vectors",
    )(c_rows, w_mod, b_mod.reshape(depth, 1, width))


def _log_sigmoid(v):
    return -jnp.log(1.0 + jnp.exp(-v))


def _tables_kernel(lgl_ref, lgr_ref, qdec_ref, win_ref, cd_ref, dm_ref):
    pos = lax.broadcasted_iota(I32, (CHUNK, RET_WIDTH), 0).astype(F32)
    ri = lax.broadcasted_iota(I32, (CHUNK, RET_HEADS * CHUNK), 0).astype(F32)
    rj = (lax.broadcasted_iota(I32, (CHUNK, RET_HEADS * CHUNK), 1) & (CHUNK - 1)).astype(F32)
    for d in range(2):
        lg = _log_sigmoid(lgl_ref[0, d])
        lg1 = lg[0:1, :]
        if d == 0:
            qdec_ref[0, d] = jnp.exp(lg1 * (pos + 1.0))
            win_ref[0, d] = jnp.exp(lg1 * (CHUNK - 1.0 - pos))
            diff = ri - rj
        else:
            qdec_ref[0, d] = jnp.exp(lg1 * (CHUNK - pos))
            win_ref[0, d] = jnp.exp(lg1 * pos)
            diff = rj - ri
        cd_ref[0, d] = jnp.exp(lg * float(CHUNK))
        lr = _log_sigmoid(lgr_ref[0, d])
        dm_ref[0, d] = jnp.where(diff >= 0.0, jnp.exp(lr * jnp.maximum(diff, 0.0)), 0.0)


def _decay_tables(ret_decay_logit):
    depth = ret_decay_logit.shape[0]
    lg = ret_decay_logit.astype(F32)
    lgl = jnp.broadcast_to(jnp.repeat(lg, HEAD_DIM, axis=-1)[:, :, None, :], (depth, 2, SUBLANES, RET_WIDTH))
    lgr = jnp.broadcast_to(jnp.repeat(lg, CHUNK, axis=-1)[:, :, None, :], (depth, 2, CHUNK, RET_HEADS * CHUNK))
    return pl.pallas_call(
        _tables_kernel,
        grid=(depth,),
        in_specs=[
            pl.BlockSpec((1, 2, SUBLANES, RET_WIDTH), lambda l: (l, 0, 0, 0)),
            pl.BlockSpec((1, 2, CHUNK, RET_HEADS * CHUNK), lambda l: (l, 0, 0, 0)),
        ],
        out_specs=[
            pl.BlockSpec((1, 2, CHUNK, RET_WIDTH), lambda l: (l, 0, 0, 0)),
            pl.BlockSpec((1, 2, CHUNK, RET_WIDTH), lambda l: (l, 0, 0, 0)),
            pl.BlockSpec((1, 2, SUBLANES, RET_WIDTH), lambda l: (l, 0, 0, 0)),
            pl.BlockSpec((1, 2, CHUNK, RET_HEADS * CHUNK), lambda l: (l, 0, 0, 0)),
        ],
        out_shape=[
            jax.ShapeDtypeStruct((depth, 2, CHUNK, RET_WIDTH), F32),
            jax.ShapeDtypeStruct((depth, 2, CHUNK, RET_WIDTH), F32),
            jax.ShapeDtypeStruct((depth, 2, SUBLANES, RET_WIDTH), F32),
            jax.ShapeDtypeStruct((depth, 2, CHUNK, RET_HEADS * CHUNK), F32),
        ],
        compiler_params=_params("parallel"),
        name="decay_tables",
    )(lgl, lgr)


def _rope(v, cs, sn):
    lane = lax.broadcasted_iota(I32, (1, LANES), 1)
    first = (lane & 31) < 16
    outs = []
    for g in range(v.shape[1] // LANES):
        vg = v[:, g * LANES:(g + 1) * LANES]
        sw = jnp.where(first, pltpu.roll(vg, LANES - 16, 1), pltpu.roll(vg, 16, 1))
        outs.append(vg * cs + sw * sn)
    return outs[0] if len(outs) == 1 else jnp.concatenate(outs, axis=1)


def _swap_halves(v):
    return jnp.concatenate([v[:, HEAD_DIM:], v[:, :HEAD_DIM]], axis=1)


def _from_gather_layout(f_ref, tm):
    return jnp.concatenate(
        [f_ref[0, pl.ds(q, tm, stride=SUBLANES), :] for q in range(SUBLANES)], axis=1)


def _rms(v):
    return v * lax.rsqrt(jnp.mean(v * v, axis=-1, keepdims=True) + EPS)


def _mod_rows(mod_ref, tm, n_ctx, tile):
    m = mod_ref[0]
    if n_ctx % tm == 0:
        is_ctx = tile < n_ctx // tm
    else:
        is_ctx = tile * tm + lax.broadcasted_iota(I32, (tm, 1), 0) < n_ctx

    def row(r):
        return jnp.where(is_ctx, m[M_CTX + r:M_CTX + r + 1, :], m[r:r + 1, :])

    return row


def _inproj_kernel(has_ffn, n_ctx, *refs):
    if has_ffn:
        x_ref, f_ref, mod_ref, g_ref, w_ref, cs_ref, sn_ref = refs[:7]
        outs = refs[7:]
        xo_ref = outs[0]
        outs = outs[1:]
    else:
        x_ref, mod_ref, g_ref, w_ref, cs_ref, sn_ref = refs[:6]
        outs = refs[6:]
    cz_ref, rqkv_ref, gates_ref, aq_ref, kv_ref = outs
    tm = x_ref.shape[1]
    row = _mod_rows(mod_ref, tm, n_ctx, pl.program_id(1))

    x = x_ref[0]
    if has_ffn:
        x = x + row(M_G2) * _from_gather_layout(f_ref, tm)
        xo_ref[0] = x
    h = (_rms(x) * g_ref[...]) * (1.0 + row(M_SC1)) + row(M_SH1)
    h = h.astype(BF16)
    cs = cs_ref[...]
    sn = sn_ref[...]

    def proj(a, b):
        return _dot(h, w_ref[:, a:b])

    cz_ref[0, :, 0:CONV_WIDTH] = proj(O_CB, O_CC)
    cz_ref[0, :, CONV_WIDTH:] = proj(O_CC, O_CX) * proj(O_CX, O_RQ)
    rqkv_ref[0, :, 0:RET_WIDTH] = _rope(proj(O_RQ, O_RK), cs, sn).astype(BF16)
    rqkv_ref[0, :, RET_WIDTH:2 * RET_WIDTH] = _rope(proj(O_RK, O_RV) * QK_SCALE, cs, sn).astype(BF16)
    rqkv_ref[0, :, 2 * RET_WIDTH:] = proj(O_RV, O_GF).astype(BF16)
    gates_ref[0, :, 0:RET_WIDTH] = _silu(proj(O_GF, O_GB))
    gates_ref[0, :, RET_WIDTH:] = _silu(proj(O_GB, O_AQ))
    aq_ref[0] = (_rope(proj(O_AQ, O_AK), cs, sn) * (QK_SCALE * LOG2E)).astype(BF16)
    ak = _rope(proj(O_AK, O_AV), cs, sn)
    av = proj(O_AV, IN_WIDTH)
    kv_ref[0, :, 0:KV_WIDTH] = ak.astype(BF16)
    kv_ref[0, :, KV_WIDTH:2 * KV_WIDTH] = _swap_halves(ak).astype(BF16)
    kv_ref[0, :, 2 * KV_WIDTH:] = av.astype(BF16)


def _inproj(x, ffn, mrows, norm_g, w_in_bf, layer, rope_cs, rope_sn, n_ctx):
    b, tu, d = x.shape
    tm = INPROJ_TILE
    has_ffn = ffn is not None
    tok = lambda width: pl.BlockSpec((1, tm, width), lambda i, j: (i, j, 0))
    in_specs = [tok(d)]
    args = [x]
    if has_ffn:
        in_specs.append(pl.BlockSpec((1, tm * SUBLANES, LANES), lambda i, j: (i, j, 0)))
        args.append(ffn)
    in_specs += [
        pl.BlockSpec((1, M_ROWS, d), lambda i, j: (i, 0, 0)),
        pl.BlockSpec((1, d), lambda i, j: (0, 0)),
        pl.BlockSpec((None, d, IN_WIDTH), lambda i, j: (layer, 0, 0)),
        pl.BlockSpec((tm, LANES), lambda i, j: (j, 0)),
        pl.BlockSpec((tm, LANES), lambda i, j: (j, 0)),
    ]
    args += [mrows, norm_g.reshape(1, d), w_in_bf, rope_cs, rope_sn]
    widths = [(2 * CONV_WIDTH, F32), (3 * RET_WIDTH, BF16), (2 * RET_WIDTH, F32), (ATTN_WIDTH, BF16),
              (3 * KV_WIDTH, BF16)]
    out_specs = [tok(w) for w, _ in widths]
    out_shape = [jax.ShapeDtypeStruct((b, tu, w), dt) for w, dt in widths]
    if has_ffn:
        out_specs = [tok(d)] + out_specs
        out_shape = [jax.ShapeDtypeStruct((b, tu, d), F32)] + out_shape
    res = pl.pallas_call(
        functools.partial(_inproj_kernel, has_ffn, n_ctx),
        grid=(b, tu // tm),
        in_specs=in_specs,
        out_specs=out_specs,
        out_shape=out_shape,
        compiler_params=_params("parallel", "parallel"),
        name="inproj",
    )(*args)
    if has_ffn:
        return res[0], res[1:]
    return x, res


def _head_block_mask(n):
    r = lax.broadcasted_iota(I32, (n, n), 0) // HEAD_DIM
    c = lax.broadcasted_iota(I32, (n, n), 1) // HEAD_DIM
    return r == c


STATE_UNROLL = 2


def _states_kernel(n_ctx_chunks, n_chunks, rqkv_ref, win_ref, cd_ref, sp_ref, s_scr):
    same_head = _head_block_mask(RET_WIDTH)
    s_scr[...] = jnp.zeros_like(s_scr)

    def chunk_update(d, pos):
        if d == 0:
            c = pos
        else:
            c = jnp.where(pos < n_ctx_chunks, n_ctx_chunks - 1 - pos, n_chunks + n_ctx_chunks - 1 - pos)
        off = pl.multiple_of(c * CHUNK, CHUNK)
        kw = rqkv_ref[0, pl.ds(off, CHUNK), RET_WIDTH:2 * RET_WIDTH].astype(F32) * win_ref[d]
        v = rqkv_ref[0, pl.ds(off, CHUNK), 2 * RET_WIDTH:]
        kw_t = kw.T.astype(BF16)
        yield
        u = _dot(kw_t, v)
        yield
        return c, jnp.where(same_head, u, 0.0)

    def body(i, carry):
        todo = [(d, i * STATE_UNROLL + n) for n in range(STATE_UNROLL) for d in range(2)]
        done = _round_robin([chunk_update(d, pos) for d, pos in todo])
        for (d, _), (c, u) in zip(todo, done):
            s = s_scr[d]
            sp_ref[0, d, pl.ds(c, 1)] = s.astype(BF16)[None]
            s_scr[d] = s * cd_ref[d][0:1, :] + u
        return carry

    assert n_chunks % STATE_UNROLL == 0
    lax.fori_loop(0, n_chunks // STATE_UNROLL, body, 0)


def _ret_states(rqkv, win, cd, n_ctx):
    b, tu, _ = rqkv.shape
    w = RET_WIDTH
    n_chunks = tu // CHUNK
    return pl.pallas_call(
        functools.partial(_states_kernel, n_ctx // CHUNK, n_chunks),
        grid=(b,),
        in_specs=[
            pl.BlockSpec((1, tu, 3 * w), lambda i: (i, 0, 0)),
            pl.BlockSpec((2, CHUNK, w), lambda i: (0, 0, 0)),
            pl.BlockSpec((2, SUBLANES, w), lambda i: (0, 0, 0)),
        ],
        out_specs=pl.BlockSpec((1, 2, n_chunks, w, w), lambda i: (i, 0, 0, 0, 0)),
        out_shape=jax.ShapeDtypeStruct((b, 2, n_chunks, w, w), BF16),
        scratch_shapes=[pltpu.VMEM((2, w, w), F32)],
        compiler_params=_params("parallel"),
        name="ret_states",
    )(rqkv, win, cd)


def _group_mean(t, ones_bd):
    hi = t.astype(BF16)
    lo = (t - hi.astype(F32)).astype(BF16)
    s = _dot(jnp.concatenate([hi, lo], axis=0), ones_bd)
    n = t.shape[0]
    return (s[:n] + s[n:]) * (1.0 / HEAD_DIM)


def _mix_chunk(c, n_ctx_chunks, n_chunks, cz, z_last, z_first, cw, rqkv, gates, sp, dm_ref, qdec_ref,
               qa, kvs, sink_ref):
    is_lat = c >= n_ctx_chunks

    z = cz[:, CONV_WIDTH:]
    row = lax.broadcasted_iota(I32, (CHUNK, 1), 0)
    has_prev = jnp.logical_and(c != 0, c != n_ctx_chunks)
    has_next = jnp.logical_and(c != n_ctx_chunks - 1, c != n_chunks - 1)
    z_before = jnp.where(row == 0, jnp.where(has_prev, z_last, 0.0), pltpu.roll(z, 1, 0))
    z_after = jnp.where(row == CHUNK - 1, jnp.where(has_next, z_first, 0.0), pltpu.roll(z, CHUNK - 1, 0))
    conv = cz[:, :CONV_WIDTH] * (z_before * cw[0:1, :] + z * cw[1:2, :] + z_after * cw[2:3, :])

    q = rqkv[:, 0:RET_WIDTH]
    k = rqkv[:, RET_WIDTH:2 * RET_WIDTH]
    v = rqkv[:, 2 * RET_WIDTH:]
    lane_head = lax.broadcasted_iota(I32, (1, RET_WIDTH), 1) // HEAD_DIM
    kz = jnp.zeros_like(k)
    k_heads = jnp.concatenate([jnp.where(lane_head == hh, k, kz) for hh in range(RET_HEADS)], axis=0)
    v_heads = jnp.concatenate([jnp.where(lane_head == hh, v, kz) for hh in range(RET_HEADS)], axis=0)
    scores = _dot_nt(q, k_heads)

    keys, v_t = kvs
    half = lax.broadcasted_iota(I32, (1, LANES), 1) // HEAD_DIM
    qa_z = jnp.zeros((CHUNK, LANES), BF16)
    combos = [(hk, par) for hk in range(ATTN_KV_HEADS) for par in range(2)]
    raw = {}
    for hk, par in combos:
        ja, jb = 2 * hk, 2 * hk + 1
        qst = jnp.concatenate([
            jnp.where(half == par, qa[:, ja * LANES:(ja + 1) * LANES], qa_z),
            jnp.where(half == par, qa[:, jb * LANES:(jb + 1) * LANES], qa_z)], axis=0)
        sel = 0 if par == hk else 1
        raw[hk, par] = _dot_nt(keys[sel], qst)
    yield

    qf = q.astype(F32)
    outs = []
    for d in range(2):
        p = (scores * dm_ref[d]).astype(BF16)
        lhs = jnp.concatenate([p, (qf * qdec_ref[d]).astype(BF16)], axis=1)
        o = _dot(lhs, jnp.concatenate([v_heads, sp[d]], axis=0))
        outs.append(o)
    o2 = jnp.concatenate(outs, axis=0)
    yield

    n_ctx = keys[0].shape[0] - 3 * CHUNK
    cols2 = 2 * CHUNK
    ik = lax.broadcasted_iota(I32, (CHUNK, cols2), 0)
    iq = lax.broadcasted_iota(I32, (CHUNK, cols2), 1) & (CHUNK - 1)
    ok_prev = jnp.logical_and(jnp.logical_and(is_lat, c - 1 >= n_ctx_chunks), ik >= iq)
    ok_cur = jnp.logical_and(is_lat, ik >= 0)
    ok_next = jnp.logical_and(jnp.logical_and(is_lat, c + 1 <= n_chunks - 1), ik <= iq)
    ok_ctx = jnp.ones((n_ctx, cols2), jnp.bool_)
    valid = jnp.concatenate([ok_prev, ok_cur, ok_next, ok_ctx], axis=0)
    att = {}
    for n, (hk, par) in enumerate(combos):
        s = jnp.where(valid, raw[hk, par], NEG_INF)
        ha, hb = ATTN_GROUP * hk + par, ATTN_GROUP * hk + par + 2
        snk = jnp.concatenate([sink_ref[ha * CHUNK:ha * CHUNK + 1, :],
                               sink_ref[hb * CHUNK:hb * CHUNK + 1, :]], axis=1) * LOG2E
        mx = jnp.maximum(jnp.max(s, axis=0, keepdims=True), snk)
        e = jnp.exp2(s - mx)
        den = jnp.sum(e, axis=0, keepdims=True) + jnp.exp2(snk - mx)
        o_t = _dot(v_t[hk * HEAD_DIM:(hk + 1) * HEAD_DIM, :], e.astype(BF16))
        att[hk, par] = o_t * (1.0 / den)
        if n == 0:
            ones_bd = jnp.where(_head_block_mask(RET_WIDTH), 1.0, 0.0).astype(BF16)
            dl = o2 - _group_mean(o2, ones_bd)
            on = dl * lax.rsqrt(_group_mean(dl * dl, ones_bd) + EPS)
            ret = on[:CHUNK] * gates[:, 0:RET_WIDTH] + on[CHUNK:] * gates[:, RET_WIDTH:]
        yield

    cols = []
    for hk in range(ATTN_KV_HEADS):
        cols.append(jnp.concatenate([att[hk, 0][:, :CHUNK], att[hk, 1][:, :CHUNK]], axis=0).T)
        cols.append(jnp.concatenate([att[hk, 0][:, CHUNK:], att[hk, 1][:, CHUNK:]], axis=0).T)
    return jnp.concatenate([conv, ret] + cols, axis=1)


def _round_robin(gens):
    results = [None] * len(gens)
    active = list(range(len(gens)))
    while active:
        for i in list(active):
            try:
                next(gens[i])
            except StopIteration as done:
                results[i] = done.value
                active.remove(i)
    return results


MIX_CHUNKS = 2


def _mixer_kernel(n_ctx_chunks, n_chunks, n_exp,
                  cz_ref, czp_ref, czn_ref, cw_ref, rqkv_ref, gates_ref, sp_ref, dm_ref, qdec_ref,
                  aq_ref, kvp_ref, kvc_ref, kvn_ref, kvx_ref, sink_ref,
                  x_ref, mod_ref, g_ref, w_ref, wrh_ref, wrl_ref, xo_ref, hg_ref, aff_ref, mix_scr):
    t = pl.program_id(1)
    n_steps = pl.num_programs(1) - 1

    @pl.when(t == 0)
    def _():
        mix_scr[...] = jnp.zeros_like(mix_scr)

    tail = _outproj_tail(n_ctx_chunks * CHUNK, n_exp, jnp.maximum(t - 1, 0), mix_scr[...], x_ref, mod_ref, g_ref,
                         w_ref, wrh_ref, wrl_ref, xo_ref, hg_ref, aff_ref)
    j = jnp.minimum(t, n_steps - 1)
    cw = cw_ref[...]

    def values_t(blk):
        return blk[:, 2 * KV_WIDTH:].astype(F32).T.astype(BF16)

    kv_blocks = ([kvp_ref[0]] + [kvc_ref[0, s * CHUNK:(s + 1) * CHUNK, :] for s in range(MIX_CHUNKS)] + [kvn_ref[0]])
    kv_ctx = [kvx_ref[0, t * CHUNK:(t + 1) * CHUNK, :] for t in range(kvx_ref.shape[1] // CHUNK)]
    vt_blocks = [values_t(blk) for blk in kv_blocks]
    vt_ctx = [values_t(blk) for blk in kv_ctx]
    gens = []
    for s in range(MIX_CHUNKS):
        lo, hi = s * CHUNK, (s + 1) * CHUNK
        if s == 0:
            z_last = czp_ref[0, SUBLANES - 1:SUBLANES, CONV_WIDTH:]
        else:
            z_last = cz_ref[0, lo - 1:lo, CONV_WIDTH:]
        if s == MIX_CHUNKS - 1:
            z_first = czn_ref[0, 0:1, CONV_WIDTH:]
        else:
            z_first = cz_ref[0, hi:hi + 1, CONV_WIDTH:]
        window = kv_blocks[s:s + 3] + kv_ctx
        keys = tuple(jnp.concatenate([blk[:, i * KV_WIDTH:(i + 1) * KV_WIDTH] for blk in window], axis=0)
                     for i in (0, 1))
        v_t = jnp.concatenate(vt_blocks[s:s + 3] + vt_ctx, axis=1)
        gens.append(_mix_chunk(j * MIX_CHUNKS + s, n_ctx_chunks, n_chunks, cz_ref[0, lo:hi, :], z_last, z_first,
                               cw, rqkv_ref[0, lo:hi, :], gates_ref[0, lo:hi, :],
                               (sp_ref[0, 0, s], sp_ref[0, 1, s]), dm_ref, qdec_ref, aq_ref[0, lo:hi, :],
                               (keys, v_t), sink_ref))
    mixes = [mix.astype(BF16) for mix in _round_robin([tail] + gens)[1:]]
    mix_scr[...] = jnp.concatenate(mixes, axis=0)


def _mixer(proj, sp, tabs, conv_w, sink_rows, x, mrows, norm_g, w_out_bf, layer, w_router, n_ctx):
    cz, rqkv, gates, aq, kv = proj
    dm, qdec = tabs
    b, tu, _ = cz.shape
    d = x.shape[2]
    n_exp = w_router.shape[1]
    wr = jnp.pad(w_router, ((0, 0), (0, LANES - n_exp)))
    wr_hi = wr.astype(BF16)
    wr_lo = (wr - wr_hi.astype(F32)).astype(BF16)
    n_chunks = tu // CHUNK
    mc = MIX_CHUNKS
    rows = mc * CHUNK
    assert n_chunks % mc == 0 and (n_ctx // CHUNK) % mc == 0
    last = n_chunks - 1
    n_steps = n_chunks // mc
    per8 = rows // SUBLANES
    mixed = lambda t: jnp.minimum(t, n_steps - 1)
    projected = lambda t: jnp.maximum(t - 1, 0)
    cur = lambda w: pl.BlockSpec((1, rows, w), lambda i, t: (i, mixed(t), 0))
    out = lambda r, w: pl.BlockSpec((1, r, w), lambda i, t: (i, projected(t), 0))
    full = lambda shape: pl.BlockSpec(shape, lambda i, t: (0,) * len(shape))
    wcz, wkv = cz.shape[2], kv.shape[2]
    in_specs = [
        cur(wcz),
        pl.BlockSpec((1, SUBLANES, wcz), lambda i, t: (i, jnp.maximum(mixed(t) * per8 - 1, 0), 0)),
        pl.BlockSpec((1, SUBLANES, wcz),
                     lambda i, t: (i, jnp.minimum((mixed(t) + 1) * per8, tu // SUBLANES - 1), 0)),
        full(conv_w.shape),
        cur(rqkv.shape[2]), cur(gates.shape[2]),
        pl.BlockSpec((1, 2, mc, RET_WIDTH, RET_WIDTH), lambda i, t: (i, 0, mixed(t), 0, 0)),
        full(dm.shape), full(qdec.shape),
        cur(ATTN_WIDTH),
        pl.BlockSpec((1, CHUNK, wkv), lambda i, t: (i, jnp.maximum(mixed(t) * mc - 1, 0), 0)),
        cur(wkv),
        pl.BlockSpec((1, CHUNK, wkv), lambda i, t: (i, jnp.minimum((mixed(t) + 1) * mc, last), 0)),
        pl.BlockSpec((1, n_ctx, wkv), lambda i, t: (i, 0, 0)),
        full(sink_rows.shape),
        out(rows, d),
        pl.BlockSpec((1, M_ROWS, d), lambda i, t: (i, 0, 0)),
        full((1, d)),
        pl.BlockSpec((None,) + w_out_bf.shape[1:], lambda i, t: (layer, 0, 0)),
        full((d, LANES)), full((d, LANES)),
    ]
    args = [cz, cz, cz, conv_w, rqkv, gates, sp, dm, qdec, aq, kv, kv, kv, kv, sink_rows,
            x, mrows, norm_g.reshape(1, d), w_out_bf, wr_hi, wr_lo]
    return pl.pallas_call(
        functools.partial(_mixer_kernel, n_ctx // CHUNK, n_chunks, n_exp),
        grid=(b, n_steps + 1),
        in_specs=in_specs,
        out_specs=[out(rows, d), out(rows * SUBLANES, LANES), out(rows, LANES)],
        out_shape=[
            jax.ShapeDtypeStruct((b, tu, d), F32),
            jax.ShapeDtypeStruct((b, tu * SUBLANES, LANES), F32),
            jax.ShapeDtypeStruct((b, tu, LANES), F32),
        ],
        scratch_shapes=[pltpu.VMEM((rows, CONV_WIDTH + RET_WIDTH + ATTN_WIDTH), BF16)],
        compiler_params=_params("parallel", "arbitrary"),
        name="mixer",
    )(*args)


def _outproj_tail(n_ctx, n_exp, tile, mix, x_ref, mod_ref, g_ref, w_ref, wrh_ref, wrl_ref, xo_ref, hg_ref, aff_ref):
    tm = x_ref.shape[1]
    row = _mod_rows(mod_ref, tm, n_ctx, tile)

    y = _dot(mix, w_ref[...])
    yield
    x = x_ref[0] + row(M_G1) * y
    xo_ref[0] = x
    h = (_rms(x) * g_ref[...]) * (1.0 + row(M_SC2)) + row(M_SH2)
    for q in range(SUBLANES):
        hg_ref[0, pl.ds(q, tm, stride=SUBLANES), :] = h[:, q * LANES:(q + 1) * LANES]
    yield
    h_hi = h.astype(BF16)
    h_lo = (h - h_hi.astype(F32)).astype(BF16)
    logits = _dot(h_hi, wrh_ref[...]) + (_dot(h_lo, wrh_ref[...]) + _dot(h_hi, wrl_ref[...]))
    yield
    lane = lax.broadcasted_iota(I32, (1, LANES), 1)
    logits = jnp.where(lane < n_exp, logits, NEG_INF)
    e = jnp.exp(logits - jnp.max(logits, axis=1, keepdims=True))
    aff_ref[0] = e / jnp.sum(e, axis=1, keepdims=True)


ROUTE_TABLES = 4
ROUTE_GROUP = 4


def _route_stream_tables(v, k, tab_scr, ce_scr, cnt_scr, stream, thr):
    n_exp, n = v.shape
    lane = lax.broadcasted_iota(I32, (1, LANES), 1)
    tri_r = lax.broadcasted_iota(I32, (LANES, LANES), 0)
    tri_c = lax.broadcasted_iota(I32, (LANES, LANES), 1)
    upper = jnp.where(tri_r <= tri_c, 1.0, 0.0).astype(BF16)
    above = pltpu.bitcast(thr + 1, F32)
    floor = pltpu.bitcast(thr, F32)
    n_gt = jnp.sum(jnp.where(v >= above, 1, 0), axis=1, keepdims=True)
    need = (k - n_gt).astype(F32)
    seen_eq = jnp.zeros((n_exp, 1), F32)
    counts = jnp.zeros((n_exp, LANES), F32)
    for j in range(n // LANES):
        vb = v[:, j * LANES:(j + 1) * LANES]
        gt = vb >= above
        eq = jnp.logical_and(vb >= floor, vb < above)
        both = jnp.concatenate([jnp.where(gt, 1.0, 0.0), jnp.where(eq, 1.0, 0.0)], axis=0).astype(BF16)
        pref = _dot(both, upper)
        rank_eq = pref[n_exp:] + seen_eq
        inc = pref[:n_exp] + jnp.minimum(rank_eq, need) - jnp.minimum(seen_eq, need)
        sel = jnp.logical_or(gt, jnp.logical_and(eq, rank_eq <= need))
        seen_eq = rank_eq[:, LANES - 1:LANES]
        counts = counts + jnp.where(lane == j, inc[:, LANES - 1:LANES], 0.0)
        a0 = jnp.where(sel, vb, 0.0)
        t0 = a0.astype(BF16).astype(F32)
        t1 = (a0 - t0).astype(BF16).astype(F32)
        t2 = a0 - t0 - t1
        for t, val in enumerate((inc, t0, t1, t2)):
            tab_scr[stream * ROUTE_TABLES + t, pl.ds(j, n_exp, stride=LANES), :] = val
    through = _dot(counts.astype(BF16), upper)
    for e in range(n_exp):
        ce_scr[stream, e] = through[e:e + 1, :]
        cnt_scr[stream, e] = counts[e:e + 1, :]


def _route_slots(e, stream, k, lo, tab_scr, ce_scr, cnt_scr):
    lane_f = lax.broadcasted_iota(I32, (1, LANES), 1).astype(F32)
    slot = lax.broadcasted_iota(I32, (k, LANES), 0).astype(F32)
    ones_rows = jnp.ones((LANES, LANES), BF16)
    cnt_rows = jnp.broadcast_to(cnt_scr[stream, e], (LANES, LANES)).astype(BF16)
    rows_e = pl.ds(pl.multiple_of(e * LANES, LANES), LANES)
    tab = jnp.concatenate([tab_scr[stream * ROUTE_TABLES + t, rows_e, :] for t in range(ROUTE_TABLES)],
                          axis=1).astype(BF16)
    before = jnp.where(ce_scr[stream, e] <= slot, 1.0, 0.0).astype(BF16)
    blk = _dot_nt(before, ones_rows)
    base = _dot_nt(before, cnt_rows)
    yield
    row = _dot(jnp.where(lane_f == blk, 1.0, 0.0).astype(BF16), tab)
    yield
    inc = row[:, :LANES]
    aff = row[:, LANES:2 * LANES] + row[:, 2 * LANES:3 * LANES] + row[:, 3 * LANES:]
    local = slot - base
    pos = _dot_nt(jnp.where(inc <= local, 1.0, 0.0).astype(BF16), ones_rows)
    yield
    tok = blk * LANES + pos + float(lo)
    gate = jnp.sum(jnp.where(inc == local + 1.0, aff, 0.0), axis=1, keepdims=True)
    return tok, gate


def _route_kernel(n_exp, streams, aff_ref, idx_ref, gate_ref, tab_scr, ce_scr, cnt_scr):
    a = aff_ref[0].T[:n_exp, :]
    tab_scr[...] = jnp.zeros_like(tab_scr)
    vs = [a[:, lo:lo + n] for lo, n, _, _ in streams]

    def search(i, ts):
        out = []
        for t, v, (_, _, k, _) in zip(ts, vs, streams):
            cand = t | jnp.left_shift(jnp.int32(1), 30 - i)
            cnt = jnp.sum(jnp.where(v >= pltpu.bitcast(cand, F32), 1, 0), axis=1, keepdims=True)
            out.append(jnp.where(cnt >= k, cand, t))
        return tuple(out)

    thrs = lax.fori_loop(0, 31, search, tuple(jnp.zeros((n_exp, 1), I32) for _ in streams))
    for st, (v, (_, _, k, _)) in enumerate(zip(vs, streams)):
        _route_stream_tables(v, k, tab_scr, ce_scr, cnt_scr, st, thrs[st])

    slots = idx_ref.shape[3]

    assert [s[3] for s in streams] == [sum(s[2] for s in streams[:n]) for n in range(len(streams))]

    def expert_group(g, carry):
        experts = [g * ROUTE_GROUP + u for u in range(ROUTE_GROUP)]
        found = _round_robin([_route_slots(e, st, k, lo, tab_scr, ce_scr, cnt_scr)
                              for e in experts for st, (lo, _, k, _) in enumerate(streams)])
        for u, e in enumerate(experts):
            toks = []
            for st, (_, _, k, slot0) in enumerate(streams):
                tok, gate = found[u * len(streams) + st]
                toks.append(tok)
                gate_ref[0, pl.ds(e, 1), slot0:slot0 + k, :] = gate[None]
            toks.append(jnp.zeros((-slots % LANES, LANES), F32))
            row = jnp.concatenate(toks, axis=0).T[0:1, :slots]
            idx_ref[0, pl.ds(e, 1)] = row.astype(I32)[None]
        return carry

    assert n_exp % ROUTE_GROUP == 0
    lax.fori_loop(0, n_exp // ROUTE_GROUP, expert_group, 0)


def _route(aff, n_exp, n_ctx):
    b, tu, _ = aff.shape
    n_lat = tu - n_ctx
    cap_l = CAPACITY_FACTOR * n_lat // n_exp
    cap_c = CAPACITY_FACTOR * n_ctx // n_exp
    slots = cap_l + cap_c
    assert n_lat // LANES <= LANES and n_ctx // LANES <= LANES
    streams = ((n_ctx, n_lat, cap_l, 0), (0, n_ctx, cap_c, cap_l))
    return pl.pallas_call(
        functools.partial(_route_kernel, n_exp, streams),
        grid=(b,),
        in_specs=[pl.BlockSpec((1, tu, LANES), lambda i: (i, 0, 0))],
        out_specs=[pl.BlockSpec((1, n_exp, 1, slots), lambda i: (i, 0, 0, 0)),
                   pl.BlockSpec((1, n_exp, slots, 1), lambda i: (i, 0, 0, 0))],
        out_shape=[jax.ShapeDtypeStruct((b, n_exp, 1, slots), I32),
                   jax.ShapeDtypeStruct((b, n_exp, slots, 1), F32)],
        scratch_shapes=[pltpu.VMEM((len(streams) * ROUTE_TABLES, n_exp * LANES, LANES), F32),
                        pltpu.VMEM((len(streams), n_exp, 1, LANES), F32),
                        pltpu.VMEM((len(streams), n_exp, 1, LANES), F32)],
        compiler_params=_params("parallel"),
        name="route",
    )(aff)


def _slot_pitch(slots):
    return slots + SUBLANES


def _zero_slot_padding(ref, lead, slots, pitch):
    for q in range(SUBLANES):
        ref[lead + (pl.ds(q * pitch + slots, pitch - slots), slice(None))] = jnp.zeros((pitch - slots, LANES), ref.dtype)


BF16_ROWS = 2 * SUBLANES


def _slot_pitch_bf16(slots):
    return -(-slots // BF16_ROWS) * BF16_ROWS + BF16_ROWS


def _gather_kernel(slots, pitch, pitch_out, idx_ref, h_ref, o_ref, tile_scr):
    for mi in range(slots):
        r = pl.multiple_of(idx_ref[0, 0, mi] * SUBLANES, SUBLANES)
        tile_scr[pl.ds(mi, SUBLANES, stride=pitch), :] = h_ref[0, pl.ds(r, SUBLANES), :]
    _zero_slot_padding(o_ref, (0, 0), slots, pitch_out)
    for q in range(SUBLANES):
        o_ref[0, 0, q * pitch_out:q * pitch_out + slots, :] = tile_scr[q * pitch:q * pitch + slots, :].astype(BF16)


def _gather(hg, idx_rows, n_exp, slots):
    b = hg.shape[0]
    pitch = _slot_pitch(slots)
    pitch_out = _slot_pitch_bf16(slots)
    return pl.pallas_call(
        functools.partial(_gather_kernel, slots, pitch, pitch_out),
        grid=(b, n_exp),
        in_specs=[
            pl.BlockSpec((1, 1, slots), lambda i, e: (i * n_exp + e, 0, 0), memory_space=pltpu.SMEM),
            pl.BlockSpec((1,) + hg.shape[1:], lambda i, e: (i, 0, 0)),
        ],
        out_specs=pl.BlockSpec((1, 1, SUBLANES * pitch_out, LANES), lambda i, e: (i, e, 0, 0)),
        out_shape=jax.ShapeDtypeStruct((b, n_exp, SUBLANES * pitch_out, LANES), BF16),
        scratch_shapes=[pltpu.VMEM((SUBLANES * pitch, LANES), F32)],
        compiler_params=_params("parallel", "arbitrary"),
        name="gather",
    )(idx_rows, hg)


FFN_SAMPLES = 2
FFN_VMEM_LIMIT_BYTES = 60 * 1024 * 1024


def _ffn_kernel(slots, pitch_in, pitch, xs_ref, wg_ref, wu_ref, wd_ref, gate_ref, y_ref):
    n = xs_ref.shape[0]
    x = jnp.concatenate(
        [jnp.concatenate([xs_ref[s, 0, q * pitch_in:q * pitch_in + slots, :] for q in range(SUBLANES)], axis=1)
         for s in range(n)], axis=0)
    gate = jnp.concatenate([gate_ref[s, 0] for s in range(n)], axis=0)
    a = _dot(x, wg_ref[0].astype(BF16))
    u = _dot(x, wu_ref[0].astype(BF16))
    y = _dot((_silu(a) * u).astype(BF16), wd_ref[0].astype(BF16)) * gate
    for s in range(n):
        _zero_slot_padding(y_ref, (s, 0), slots, pitch)
        for q in range(SUBLANES):
            y_ref[s, 0, q * pitch:q * pitch + slots, :] = y[s * slots:(s + 1) * slots, q * LANES:(q + 1) * LANES]


def _ffn(xs, w_gate, w_up, w_down, layer, gate, slots):
    b, n_exp = xs.shape[:2]
    pitch = _slot_pitch(slots)
    pitch_in = _slot_pitch_bf16(slots)
    d, f = w_gate.shape[2:]
    ns = FFN_SAMPLES
    assert b % ns == 0
    slot_tile = lambda p: pl.BlockSpec((ns, 1, SUBLANES * p, LANES), lambda e, i: (i, e, 0, 0))
    weight = lambda rows, cols: pl.BlockSpec((None, 1, rows, cols), lambda e, i: (layer, e, 0, 0))
    return pl.pallas_call(
        functools.partial(_ffn_kernel, slots, pitch_in, pitch),
        grid=(n_exp, b // ns),
        in_specs=[
            slot_tile(pitch_in),
            weight(d, f), weight(d, f), weight(f, d),
            pl.BlockSpec((ns, 1, slots, 1), lambda e, i: (i, e, 0, 0)),
        ],
        out_specs=slot_tile(pitch),
        out_shape=jax.ShapeDtypeStruct((b, n_exp, SUBLANES * pitch, LANES), F32),
        compiler_params=_params("parallel", "arbitrary", vmem=FFN_VMEM_LIMIT_BYTES),
        name="ffn",
    )(xs, w_gate, w_up, w_down, gate)


SCATTER_BATCH = 16


def _scatter_kernel(slots, pitch, idx_ref, y_ref, o_ref):
    @pl.when(pl.program_id(1) == 0)
    def _():
        o_ref[...] = jnp.zeros_like(o_ref)

    for m0 in range(0, slots, SCATTER_BATCH):
        rows = [pl.multiple_of(idx_ref[0, 0, m0 + u] * SUBLANES, SUBLANES) for u in range(SCATTER_BATCH)]
        vals = [o_ref[0, pl.ds(rows[u], SUBLANES), :] + y_ref[0, 0, pl.ds(m0 + u, SUBLANES, stride=pitch), :]
                for u in range(SCATTER_BATCH)]
        for u in range(SCATTER_BATCH):
            o_ref[0, pl.ds(rows[u], SUBLANES), :] = vals[u]


def _scatter(y, idx_rows, tu, slots):
    b, n_exp = y.shape[:2]
    pitch = _slot_pitch(slots)
    return pl.pallas_call(
        functools.partial(_scatter_kernel, slots, pitch),
        grid=(b, n_exp),
        in_specs=[
            pl.BlockSpec((1, 1, slots), lambda i, e: (i * n_exp + e, 0, 0), memory_space=pltpu.SMEM),
            pl.BlockSpec((1, 1, SUBLANES * pitch, LANES), lambda i, e: (i, e, 0, 0)),
        ],
        out_specs=pl.BlockSpec((1, tu * SUBLANES, LANES), lambda i, e: (i, 0, 0)),
        out_shape=jax.ShapeDtypeStruct((b, tu * SUBLANES, LANES), F32),
        compiler_params=_params("parallel", "arbitrary"),
        name="scatter",
    )(idx_rows, y)


def _final_kernel(x_ref, f_ref, mod_ref, g_ref, o_ref):
    tm = x_ref.shape[1]
    x = x_ref[0] + mod_ref[0][M_G2:M_G2 + 1, :] * _from_gather_layout(f_ref, tm)
    o_ref[0] = _rms(x) * g_ref[...]


def _final(x, ffn, mrows, final_g, n_ctx):
    b, tu, d = x.shape
    tm = FINAL_TILE
    skip = n_ctx // tm
    return pl.pallas_call(
        _final_kernel,
        grid=(b, (tu - n_ctx) // tm),
        in_specs=[
            pl.BlockSpec((1, tm, d), lambda i, j: (i, j + skip, 0)),
            pl.BlockSpec((1, tm * SUBLANES, LANES), lambda i, j: (i, j + skip, 0)),
            pl.BlockSpec((1, M_ROWS, d), lambda i, j: (i, 0, 0)),
            pl.BlockSpec((1, d), lambda i, j: (0, 0)),
        ],
        out_specs=pl.BlockSpec((1, tm, d), lambda i, j: (i, j, 0)),
        out_shape=jax.ShapeDtypeStruct((b, tu - n_ctx, d), F32),
        compiler_params=_params("parallel", "parallel"),
        name="final_norm",
    )(x, ffn, mrows, final_g.reshape(1, d))


def _rope_tables(n_lat, n_ctx):
    rows = n_lat // GRID_W
    rowp = jnp.repeat(jnp.arange(rows, dtype=F32), GRID_W)
    colp = jnp.tile(jnp.arange(GRID_W, dtype=F32), rows)
    axis_dim = HEAD_DIM // 2
    inv_freq = ROPE_BASE ** (-jnp.arange(0, axis_dim, 2, dtype=F32) / axis_dim)
    ar = rowp[:, None] * inv_freq
    ac = colp[:, None] * inv_freq
    cs = jnp.concatenate([jnp.cos(ar), jnp.cos(ar), jnp.cos(ac), jnp.cos(ac)], axis=1)
    sn = jnp.concatenate([-jnp.sin(ar), jnp.sin(ar), -jnp.sin(ac), jnp.sin(ac)], axis=1)
    reps = LANES // HEAD_DIM
    cs = jnp.concatenate([jnp.ones((n_ctx, LANES), F32), jnp.tile(cs, (1, reps))], axis=0)
    sn = jnp.concatenate([jnp.zeros((n_ctx, LANES), F32), jnp.tile(sn, (1, reps))], axis=0)
    return cs, sn


def kernel(x, c, ctx, c_ctx, w_mod, b_mod, norm1_g, norm2_g, w_in, conv_w, ret_decay_logit, attn_sink,
           w_out, w_router, w_gate, w_up, w_down, final_g):
    b, n_lat, d = x.shape
    n_ctx = ctx.shape[1]
    depth = w_in.shape[0]
    n_exp = w_router.shape[2]
    tu = n_ctx + n_lat
    assert w_in.shape[2] == IN_WIDTH and tu % INPROJ_TILE == 0
    assert n_ctx % FINAL_TILE == 0 and n_lat % FINAL_TILE == 0
    assert b + 1 <= SUBLANES and n_exp == N_EXPERTS

    c_rows = jnp.concatenate([c, c_ctx[None], jnp.zeros((SUBLANES - b - 1, d), F32)], axis=0)
    mods = _mod_vectors(c_rows, w_mod, b_mod).reshape(depth, SUBLANES, 6, d)
    qdec, win, cd, dm = _decay_tables(ret_decay_logit)
    rope_cs, rope_sn = _rope_tables(n_lat, n_ctx)
    sink_rows = jnp.broadcast_to(jnp.repeat(attn_sink.astype(F32), CHUNK, axis=1)[:, :, None],
                                 (depth, ATTN_HEADS * CHUNK, LANES))
    w_in_bf = w_in.astype(BF16)
    w_out_bf = w_out.astype(BF16)

    xu = jnp.concatenate([ctx, x], axis=1)
    ffn = None
    mrows = None
    cap = CAPACITY_FACTOR * n_lat // n_exp + CAPACITY_FACTOR * n_ctx // n_exp
    for l in range(depth):
        prev_mrows = mrows
        mrows = jnp.concatenate([mods[l, :b], jnp.broadcast_to(mods[l, b][None], (b, 6, d)),
                                 jnp.zeros((b, M_ROWS - 12, d), F32)], axis=1)
        xu, proj = _inproj(xu, ffn, mrows if ffn is None else _with_prev_g2(mrows, prev_mrows),
                           norm1_g[l], w_in_bf, l, rope_cs, rope_sn, n_ctx)
        sp = _ret_states(proj[1], win[l], cd[l], n_ctx)
        xu, hg, aff = _mixer(proj, sp, (dm[l], qdec[l]), conv_w[l], sink_rows[l],
                             xu, mrows, norm2_g[l], w_out_bf, l, w_router[l], n_ctx)
        idx, gate = _route(aff, n_exp, n_ctx)
        idx_rows = idx.reshape(b * n_exp, 1, cap)
        xs = _gather(hg, idx_rows, n_exp, cap)
        y = _ffn(xs, w_gate, w_up, w_down, l, gate, cap)
        ffn = _scatter(y, idx_rows, tu, cap)
    return _final(xu, ffn, mrows, final_g, n_ctx)


def _with_prev_g2(mrows, prev_mrows):
    out = mrows.at[:, M_G2].set(prev_mrows[:, M_G2])
    return out.at[:, M_CTX + M_G2].set(prev_mrows[:, M_CTX + M_G2])
```

```python
import functools

import jax
import jax.numpy as jnp
from jax import lax
from jax.experimental import pallas as pl
from jax.experimental.pallas import tpu as pltpu

F32 = jnp.float32
BF16 = jnp.bfloat16
I32 = jnp.int32
HIGHEST = lax.Precision.HIGHEST

HEAD_DIM = 64
CONV_WIDTH = 256
RET_HEADS = 4
RET_WIDTH = RET_HEADS * HEAD_DIM
ATTN_HEADS = 8
ATTN_KV_HEADS = 2
ATTN_GROUP = ATTN_HEADS // ATTN_KV_HEADS
ATTN_WIDTH = ATTN_HEADS * HEAD_DIM
KV_WIDTH = ATTN_KV_HEADS * HEAD_DIM
CHUNK = 128
GRID_W = 64
N_EXPERTS = 16
CAPACITY_FACTOR = 2
ROPE_BASE = 10000.0
EPS = 1e-6
NEG_INF = -1e30
QK_SCALE = HEAD_DIM ** -0.5
LOG2E = 1.4426950408889634

LANES = 128
SUBLANES = 8
VMEM_LIMIT_BYTES = 56 * 1024 * 1024

O_CB = 0
O_CC = O_CB + CONV_WIDTH
O_CX = O_CC + CONV_WIDTH
O_RQ = O_CX + CONV_WIDTH
O_RK = O_RQ + RET_WIDTH
O_RV = O_RK + RET_WIDTH
O_GF = O_RV + RET_WIDTH
O_GB = O_GF + RET_WIDTH
O_AQ = O_GB + RET_WIDTH
O_AK = O_AQ + ATTN_WIDTH
O_AV = O_AK + KV_WIDTH
IN_WIDTH = O_AV + KV_WIDTH

M_SH1, M_SC1, M_G1, M_SH2, M_SC2, M_G2 = range(6)
M_CTX = 6
M_ROWS = 16

INPROJ_TILE = 544
FINAL_TILE = 256


def _params(*sem, vmem=VMEM_LIMIT_BYTES):
    return pltpu.CompilerParams(dimension_semantics=sem, vmem_limit_bytes=vmem)


def _dot(a, b):
    return jnp.dot(a, b, preferred_element_type=F32)


def _dot_nt(a, b):
    return lax.dot_general(a, b, (((1,), (1,)), ((), ())), preferred_element_type=F32)


def _silu(v):
    return v * jax.nn.sigmoid(v)


def _mod_kernel(c_ref, w_ref, b_ref, o_ref):
    s = _silu(c_ref[...])
    o_ref[0] = jnp.dot(s, w_ref[0], precision=HIGHEST, preferred_element_type=F32) + b_ref[0]


def _mod_vectors(c_rows, w_mod, b_mod):
    depth, d_model, width = w_mod.shape
    tn = 1536
    return pl.pallas_call(
        _mod_kernel,
        grid=(depth, width // tn),
        in_specs=[
            pl.BlockSpec((SUBLANES, d_model), lambda l, n: (0, 0)),
            pl.BlockSpec((1, d_model, tn), lambda l, n: (l, 0, n)),
            pl.BlockSpec((1, 1, tn), lambda l, n: (l, 0, n)),
        ],
        out_specs=pl.BlockSpec((1, SUBLANES, tn), lambda l, n: (l, 0, n)),
        out_shape=jax.ShapeDtypeStruct((depth, SUBLANES, width), F32),
        compiler_params=_params("parallel", "parallel"),
        name="mod_vectors",
    )(c_rows, w_mod, b_mod.reshape(depth, 1, width))


def _log_sigmoid(v):
    return -jnp.log(1.0 + jnp.exp(-v))


def _tables_kernel(lgl_ref, lgr_ref, qdec_ref, win_ref, cd_ref, dm_ref):
    pos = lax.broadcasted_iota(I32, (CHUNK, RET_WIDTH), 0).astype(F32)
    ri = lax.broadcasted_iota(I32, (CHUNK, RET_HEADS * CHUNK), 0).astype(F32)
    rj = (lax.broadcasted_iota(I32, (CHUNK, RET_HEADS * CHUNK), 1) & (CHUNK - 1)).astype(F32)
    for d in range(2):
        lg = _log_sigmoid(lgl_ref[0, d])
        lg1 = lg[0:1, :]
        if d == 0:
            qdec_ref[0, d] = jnp.exp(lg1 * (pos + 1.0))
            win_ref[0, d] = jnp.exp(lg1 * (CHUNK - 1.0 - pos))
            diff = ri - rj
        else:
            qdec_ref[0, d] = jnp.exp(lg1 * (CHUNK - pos))
            win_ref[0, d] = jnp.exp(lg1 * pos)
            diff = rj - ri
        cd_ref[0, d] = jnp.exp(lg * float(CHUNK))
        lr = _log_sigmoid(lgr_ref[0, d])
        dm_ref[0, d] = jnp.where(diff >= 0.0, jnp.exp(lr * jnp.maximum(diff, 0.0)), 0.0)


def _decay_tables(ret_decay_logit):
    depth = ret_decay_logit.shape[0]
    lg = ret_decay_logit.astype(F32)
    lgl = jnp.broadcast_to(jnp.repeat(lg, HEAD_DIM, axis=-1)[:, :, None, :], (depth, 2, SUBLANES, RET_WIDTH))
    lgr = jnp.broadcast_to(jnp.repeat(lg, CHUNK, axis=-1)[:, :, None, :], (depth, 2, CHUNK, RET_HEADS * CHUNK))
    return pl.pallas_call(
        _tables_kernel,
        grid=(depth,),
        in_specs=[
            pl.BlockSpec((1, 2, SUBLANES, RET_WIDTH), lambda l: (l, 0, 0, 0)),
            pl.BlockSpec((1, 2, CHUNK, RET_HEADS * CHUNK), lambda l: (l, 0, 0, 0)),
        ],
        out_specs=[
            pl.BlockSpec((1, 2, CHUNK, RET_WIDTH), lambda l: (l, 0, 0, 0)),
            pl.BlockSpec((1, 2, CHUNK, RET_WIDTH), lambda l: (l, 0, 0, 0)),
            pl.BlockSpec((1, 2, SUBLANES, RET_WIDTH), lambda l: (l, 0, 0, 0)),
            pl.BlockSpec((1, 2, CHUNK, RET_HEADS * CHUNK), lambda l: (l, 0, 0, 0)),
        ],
        out_shape=[
            jax.ShapeDtypeStruct((depth, 2, CHUNK, RET_WIDTH), F32),
            jax.ShapeDtypeStruct((depth, 2, CHUNK, RET_WIDTH), F32),
            jax.ShapeDtypeStruct((depth, 2, SUBLANES, RET_WIDTH), F32),
            jax.ShapeDtypeStruct((depth, 2, CHUNK, RET_HEADS * CHUNK), F32),
        ],
        compiler_params=_params("parallel"),
        name="decay_tables",
    )(lgl, lgr)


def _rope(v, cs, sn):
    lane = lax.broadcasted_iota(I32, (1, LANES), 1)
    first = (lane & 31) < 16
    outs = []
    for g in range(v.shape[1] // LANES):
        vg = v[:, g * LANES:(g + 1) * LANES]
        sw = jnp.where(first, pltpu.roll(vg, LANES - 16, 1), pltpu.roll(vg, 16, 1))
        outs.append(vg * cs + sw * sn)
    return outs[0] if len(outs) == 1 else jnp.concatenate(outs, axis=1)


def _swap_halves(v):
    return jnp.concatenate([v[:, HEAD_DIM:], v[:, :HEAD_DIM]], axis=1)


def _from_gather_layout(f_ref, tm):
    return jnp.concatenate(
        [f_ref[0, pl.ds(q, tm, stride=SUBLANES), :] for q in range(SUBLANES)], axis=1)


def _rms(v):
    return v * lax.rsqrt(jnp.mean(v * v, axis=-1, keepdims=True) + EPS)


def _mod_rows(mod_ref, tm, n_ctx, tile):
    m = mod_ref[0]
    if n_ctx % tm == 0:
        is_ctx = tile < n_ctx // tm
    else:
        is_ctx = tile * tm + lax.broadcasted_iota(I32, (tm, 1), 0) < n_ctx

    def row(r):
        return jnp.where(is_ctx, m[M_CTX + r:M_CTX + r + 1, :], m[r:r + 1, :])

    return row


def _inproj_kernel(has_ffn, n_ctx, *refs):
    if has_ffn:
        x_ref, f_ref, mod_ref, g_ref, w_ref, cs_ref, sn_ref = refs[:7]
        outs = refs[7:]
        xo_ref = outs[0]
        outs = outs[1:]
    else:
        x_ref, mod_ref, g_ref, w_ref, cs_ref, sn_ref = refs[:6]
        outs = refs[6:]
    cz_ref, rqkv_ref, gates_ref, aq_ref, kv_ref = outs
    tm = x_ref.shape[1]
    row = _mod_rows(mod_ref, tm, n_ctx, pl.program_id(1))

    x = x_ref[0]
    if has_ffn:
        x = x + row(M_G2) * _from_gather_layout(f_ref, tm)
        xo_ref[0] = x
    h = (_rms(x) * g_ref[...]) * (1.0 + row(M_SC1)) + row(M_SH1)
    h = h.astype(BF16)
    cs = cs_ref[...]
    sn = sn_ref[...]

    def proj(a, b):
        return _dot(h, w_ref[:, a:b])

    cz_ref[0, :, 0:CONV_WIDTH] = proj(O_CB, O_CC)
    cz_ref[0, :, CONV_WIDTH:] = proj(O_CC, O_CX) * proj(O_CX, O_RQ)
    rqkv_ref[0, :, 0:RET_WIDTH] = _rope(proj(O_RQ, O_RK), cs, sn).astype(BF16)
    rqkv_ref[0, :, RET_WIDTH:2 * RET_WIDTH] = _rope(proj(O_RK, O_RV) * QK_SCALE, cs, sn).astype(BF16)
    rqkv_ref[0, :, 2 * RET_WIDTH:] = proj(O_RV, O_GF).astype(BF16)
    gates_ref[0, :, 0:RET_WIDTH] = _silu(proj(O_GF, O_GB))
    gates_ref[0, :, RET_WIDTH:] = _silu(proj(O_GB, O_AQ))
    aq_ref[0] = (_rope(proj(O_AQ, O_AK), cs, sn) * (QK_SCALE * LOG2E)).astype(BF16)
    ak = _rope(proj(O_AK, O_AV), cs, sn)
    av = proj(O_AV, IN_WIDTH)
    kv_ref[0, :, 0:KV_WIDTH] = ak.astype(BF16)
    kv_ref[0, :, KV_WIDTH:2 * KV_WIDTH] = _swap_halves(ak).astype(BF16)
    kv_ref[0, :, 2 * KV_WIDTH:] = av.astype(BF16)


def _inproj(x, ffn, mrows, norm_g, w_in_bf, layer, rope_cs, rope_sn, n_ctx):
    b, tu, d = x.shape
    tm = INPROJ_TILE
    has_ffn = ffn is not None
    tok = lambda width: pl.BlockSpec((1, tm, width), lambda i, j: (i, j, 0))
    in_specs = [tok(d)]
    args = [x]
    if has_ffn:
        in_specs.append(pl.BlockSpec((1, tm * SUBLANES, LANES), lambda i, j: (i, j, 0)))
        args.append(ffn)
    in_specs += [
        pl.BlockSpec((1, M_ROWS, d), lambda i, j: (i, 0, 0)),
        pl.BlockSpec((1, d), lambda i, j: (0, 0)),
        pl.BlockSpec((None, d, IN_WIDTH), lambda i, j: (layer, 0, 0)),
        pl.BlockSpec((tm, LANES), lambda i, j: (j, 0)),
        pl.BlockSpec((tm, LANES), lambda i, j: (j, 0)),
    ]
    args += [mrows, norm_g.reshape(1, d), w_in_bf, rope_cs, rope_sn]
    widths = [(2 * CONV_WIDTH, F32), (3 * RET_WIDTH, BF16), (2 * RET_WIDTH, F32), (ATTN_WIDTH, BF16),
              (3 * KV_WIDTH, BF16)]
    out_specs = [tok(w) for w, _ in widths]
    out_shape = [jax.ShapeDtypeStruct((b, tu, w), dt) for w, dt in widths]
    if has_ffn:
        out_specs = [tok(d)] + out_specs
        out_shape = [jax.ShapeDtypeStruct((b, tu, d), F32)] + out_shape
    res = pl.pallas_call(
        functools.partial(_inproj_kernel, has_ffn, n_ctx),
        grid=(b, tu // tm),
        in_specs=in_specs,
        out_specs=out_specs,
        out_shape=out_shape,
        compiler_params=_params("parallel", "parallel"),
        name="inproj",
    )(*args)
    if has_ffn:
        return res[0], res[1:]
    return x, res


def _head_block_mask(n):
    r = lax.broadcasted_iota(I32, (n, n), 0) // HEAD_DIM
    c = lax.broadcasted_iota(I32, (n, n), 1) // HEAD_DIM
    return r == c


STATE_UNROLL = 2


def _states_kernel(n_ctx_chunks, n_chunks, rqkv_ref, win_ref, cd_ref, sp_ref, s_scr):
    same_head = _head_block_mask(RET_WIDTH)
    s_scr[...] = jnp.zeros_like(s_scr)

    def chunk_update(d, pos):
        if d == 0:
            c = pos
        else:
            c = jnp.where(pos < n_ctx_chunks, n_ctx_chunks - 1 - pos, n_chunks + n_ctx_chunks - 1 - pos)
        off = pl.multiple_of(c * CHUNK, CHUNK)
        kw = rqkv_ref[0, pl.ds(off, CHUNK), RET_WIDTH:2 * RET_WIDTH].astype(F32) * win_ref[d]
        v = rqkv_ref[0, pl.ds(off, CHUNK), 2 * RET_WIDTH:]
        kw_t = kw.T.astype(BF16)
        yield
        u = _dot(kw_t, v)
        yield
        return c, jnp.where(same_head, u, 0.0)

    def body(i, carry):
        todo = [(d, i * STATE_UNROLL + n) for n in range(STATE_UNROLL) for d in range(2)]
        done = _round_robin([chunk_update(d, pos) for d, pos in todo])
        for (d, _), (c, u) in zip(todo, done):
            s = s_scr[d]
            sp_ref[0, d, pl.ds(c, 1)] = s.astype(BF16)[None]
            s_scr[d] = s * cd_ref[d][0:1, :] + u
        return carry

    assert n_chunks % STATE_UNROLL == 0
    lax.fori_loop(0, n_chunks // STATE_UNROLL, body, 0)


def _ret_states(rqkv, win, cd, n_ctx):
    b, tu, _ = rqkv.shape
    w = RET_WIDTH
    n_chunks = tu // CHUNK
    return pl.pallas_call(
        functools.partial(_states_kernel, n_ctx // CHUNK, n_chunks),
        grid=(b,),
        in_specs=[
            pl.BlockSpec((1, tu, 3 * w), lambda i: (i, 0, 0)),
            pl.BlockSpec((2, CHUNK, w), lambda i: (0, 0, 0)),
            pl.BlockSpec((2, SUBLANES, w), lambda i: (0, 0, 0)),
        ],
        out_specs=pl.BlockSpec((1, 2, n_chunks, w, w), lambda i: (i, 0, 0, 0, 0)),
        out_shape=jax.ShapeDtypeStruct((b, 2, n_chunks, w, w), BF16),
        scratch_shapes=[pltpu.VMEM((2, w, w), F32)],
        compiler_params=_params("parallel"),
        name="ret_states",
    )(rqkv, win, cd)


def _group_mean(t, ones_bd):
    hi = t.astype(BF16)
    lo = (t - hi.astype(F32)).astype(BF16)
    s = _dot(jnp.concatenate([hi, lo], axis=0), ones_bd)
    n = t.shape[0]
    return (s[:n] + s[n:]) * (1.0 / HEAD_DIM)


def _mix_chunk(c, n_ctx_chunks, n_chunks, cz, z_last, z_first, cw, rqkv, gates, sp, dm_ref, qdec_ref,
               qa, kvs, sink_ref):
    is_lat = c >= n_ctx_chunks

    z = cz[:, CONV_WIDTH:]
    row = lax.broadcasted_iota(I32, (CHUNK, 1), 0)
    has_prev = jnp.logical_and(c != 0, c != n_ctx_chunks)
    has_next = jnp.logical_and(c != n_ctx_chunks - 1, c != n_chunks - 1)
    z_before = jnp.where(row == 0, jnp.where(has_prev, z_last, 0.0), pltpu.roll(z, 1, 0))
    z_after = jnp.where(row == CHUNK - 1, jnp.where(has_next, z_first, 0.0), pltpu.roll(z, CHUNK - 1, 0))
    conv = cz[:, :CONV_WIDTH] * (z_before * cw[0:1, :] + z * cw[1:2, :] + z_after * cw[2:3, :])

    q = rqkv[:, 0:RET_WIDTH]
    k = rqkv[:, RET_WIDTH:2 * RET_WIDTH]
    v = rqkv[:, 2 * RET_WIDTH:]
    lane_head = lax.broadcasted_iota(I32, (1, RET_WIDTH), 1) // HEAD_DIM
    kz = jnp.zeros_like(k)
    k_heads = jnp.concatenate([jnp.where(lane_head == hh, k, kz) for hh in range(RET_HEADS)], axis=0)
    v_heads = jnp.concatenate([jnp.where(lane_head == hh, v, kz) for hh in range(RET_HEADS)], axis=0)
    scores = _dot_nt(q, k_heads)

    keys, v_t = kvs
    half = lax.broadcasted_iota(I32, (1, LANES), 1) // HEAD_DIM
    qa_z = jnp.zeros((CHUNK, LANES), BF16)
    combos = [(hk, par) for hk in range(ATTN_KV_HEADS) for par in range(2)]
    raw = {}
    for hk, par in combos:
        ja, jb = 2 * hk, 2 * hk + 1
        qst = jnp.concatenate([
            jnp.where(half == par, qa[:, ja * LANES:(ja + 1) * LANES], qa_z),
            jnp.where(half == par, qa[:, jb * LANES:(jb + 1) * LANES], qa_z)], axis=0)
        sel = 0 if par == hk else 1
        raw[hk, par] = _dot_nt(keys[sel], qst)
    yield

    qf = q.astype(F32)
    outs = []
    for d in range(2):
        p = (scores * dm_ref[d]).astype(BF16)
        lhs = jnp.concatenate([p, (qf * qdec_ref[d]).astype(BF16)], axis=1)
        o = _dot(lhs, jnp.concatenate([v_heads, sp[d]], axis=0))
        outs.append(o)
    o2 = jnp.concatenate(outs, axis=0)
    yield

    n_ctx = keys[0].shape[0] - 3 * CHUNK
    cols2 = 2 * CHUNK
    ik = lax.broadcasted_iota(I32, (CHUNK, cols2), 0)
    iq = lax.broadcasted_iota(I32, (CHUNK, cols2), 1) & (CHUNK - 1)
    off = jnp.full((CHUNK, cols2), NEG_INF, F32)
    bias_prev = jnp.where(jnp.logical_and(jnp.logical_and(is_lat, c - 1 >= n_ctx_chunks), ik >= iq), 0.0, off)
    bias_cur = jnp.where(is_lat, 0.0, off)
    bias_next = jnp.where(jnp.logical_and(jnp.logical_and(is_lat, c + 1 <= n_chunks - 1), ik <= iq), 0.0, off)
    assert n_ctx > 0
    att = {}
    for n, (hk, par) in enumerate(combos):
        r = raw[hk, par]
        s = jnp.concatenate([r[:CHUNK] + bias_prev, r[CHUNK:2 * CHUNK] + bias_cur,
                             r[2 * CHUNK:3 * CHUNK] + bias_next, r[3 * CHUNK:]], axis=0)
        ha, hb = ATTN_GROUP * hk + par, ATTN_GROUP * hk + par + 2
        snk = jnp.concatenate([sink_ref[ha * CHUNK:ha * CHUNK + 1, :],
                               sink_ref[hb * CHUNK:hb * CHUNK + 1, :]], axis=1) * LOG2E
        mx = jnp.maximum(jnp.max(s, axis=0, keepdims=True), snk)
        e = jnp.exp2(s - mx)
        den = jnp.sum(e, axis=0, keepdims=True) + jnp.exp2(snk - mx)
        o_t = _dot(v_t[hk * HEAD_DIM:(hk + 1) * HEAD_DIM, :], e.astype(BF16))
        att[hk, par] = o_t * (1.0 / den)
        if n == 0:
            ones_bd = jnp.where(_head_block_mask(RET_WIDTH), 1.0, 0.0).astype(BF16)
            dl = o2 - _group_mean(o2, ones_bd)
            on = dl * lax.rsqrt(_group_mean(dl * dl, ones_bd) + EPS)
            ret = on[:CHUNK] * gates[:, 0:RET_WIDTH] + on[CHUNK:] * gates[:, RET_WIDTH:]
        yield

    cols = []
    for hk in range(ATTN_KV_HEADS):
        cols.append(jnp.concatenate([att[hk, 0][:, :CHUNK], att[hk, 1][:, :CHUNK]], axis=0).T)
        cols.append(jnp.concatenate([att[hk, 0][:, CHUNK:], att[hk, 1][:, CHUNK:]], axis=0).T)
    return jnp.concatenate([conv, ret] + cols, axis=1)


def _round_robin(gens):
    results = [None] * len(gens)
    active = list(range(len(gens)))
    while active:
        for i in list(active):
            try:
                next(gens[i])
            except StopIteration as done:
                results[i] = done.value
                active.remove(i)
    return results


MIX_CHUNKS = 2


def _mixer_kernel(n_ctx_chunks, n_chunks, n_exp,
                  cz_ref, czp_ref, czn_ref, cw_ref, rqkv_ref, gates_ref, sp_ref, dm_ref, qdec_ref,
                  aq_ref, kvp_ref, kvc_ref, kvn_ref, kvx_ref, sink_ref,
                  x_ref, mod_ref, g_ref, w_ref, wrh_ref, wrl_ref, xo_ref, hg_ref, aff_ref, mix_scr):
    t = pl.program_id(1)
    n_steps = pl.num_programs(1) - 1

    @pl.when(t == 0)
    def _():
        mix_scr[...] = jnp.zeros_like(mix_scr)

    tail = _outproj_tail(n_ctx_chunks * CHUNK, n_exp, jnp.maximum(t - 1, 0), mix_scr[...], x_ref, mod_ref, g_ref,
                         w_ref, wrh_ref, wrl_ref, xo_ref, hg_ref, aff_ref)
    j = jnp.minimum(t, n_steps - 1)
    cw = cw_ref[...]

    def values_t(blk):
        return blk[:, 2 * KV_WIDTH:].astype(F32).T.astype(BF16)

    kv_blocks = ([kvp_ref[0]] + [kvc_ref[0, s * CHUNK:(s + 1) * CHUNK, :] for s in range(MIX_CHUNKS)] + [kvn_ref[0]])
    kv_ctx = [kvx_ref[0, t * CHUNK:(t + 1) * CHUNK, :] for t in range(kvx_ref.shape[1] // CHUNK)]
    vt_blocks = [values_t(blk) for blk in kv_blocks]
    vt_ctx = [values_t(blk) for blk in kv_ctx]
    gens = []
    for s in range(MIX_CHUNKS):
        lo, hi = s * CHUNK, (s + 1) * CHUNK
        if s == 0:
            z_last = czp_ref[0, SUBLANES - 1:SUBLANES, CONV_WIDTH:]
        else:
            z_last = cz_ref[0, lo - 1:lo, CONV_WIDTH:]
        if s == MIX_CHUNKS - 1:
            z_first = czn_ref[0, 0:1, CONV_WIDTH:]
        else:
            z_first = cz_ref[0, hi:hi + 1, CONV_WIDTH:]
        window = kv_blocks[s:s + 3] + kv_ctx
        keys = tuple(jnp.concatenate([blk[:, i * KV_WIDTH:(i + 1) * KV_WIDTH] for blk in window], axis=0)
                     for i in (0, 1))
        v_t = jnp.concatenate(vt_blocks[s:s + 3] + vt_ctx, axis=1)
        gens.append(_mix_chunk(j * MIX_CHUNKS + s, n_ctx_chunks, n_chunks, cz_ref[0, lo:hi, :], z_last, z_first,
                               cw, rqkv_ref[0, lo:hi, :], gates_ref[0, lo:hi, :],
                               (sp_ref[0, 0, s], sp_ref[0, 1, s]), dm_ref, qdec_ref, aq_ref[0, lo:hi, :],
                               (keys, v_t), sink_ref))
    mixes = [mix.astype(BF16) for mix in _round_robin([tail] + gens)[1:]]
    mix_scr[...] = jnp.concatenate(mixes, axis=0)


def _mixer(proj, sp, tabs, conv_w, sink_rows, x, mrows, norm_g, w_out_bf, layer, w_router, n_ctx):
    cz, rqkv, gates, aq, kv = proj
    dm, qdec = tabs
    b, tu, _ = cz.shape
    d = x.shape[2]
    n_exp = w_router.shape[1]
    wr = jnp.pad(w_router, ((0, 0), (0, LANES - n_exp)))
    wr_hi = wr.astype(BF16)
    wr_lo = (wr - wr_hi.astype(F32)).astype(BF16)
    n_chunks = tu // CHUNK
    mc = MIX_CHUNKS
    rows = mc * CHUNK
    assert n_chunks % mc == 0 and (n_ctx // CHUNK) % mc == 0
    last = n_chunks - 1
    n_steps = n_chunks // mc
    per8 = rows // SUBLANES
    mixed = lambda t: jnp.minimum(t, n_steps - 1)
    projected = lambda t: jnp.maximum(t - 1, 0)
    cur = lambda w: pl.BlockSpec((1, rows, w), lambda i, t: (i, mixed(t), 0))
    out = lambda r, w: pl.BlockSpec((1, r, w), lambda i, t: (i, projected(t), 0))
    full = lambda shape: pl.BlockSpec(shape, lambda i, t: (0,) * len(shape))
    wcz, wkv = cz.shape[2], kv.shape[2]
    in_specs = [
        cur(wcz),
        pl.BlockSpec((1, SUBLANES, wcz), lambda i, t: (i, jnp.maximum(mixed(t) * per8 - 1, 0), 0)),
        pl.BlockSpec((1, SUBLANES, wcz),
                     lambda i, t: (i, jnp.minimum((mixed(t) + 1) * per8, tu // SUBLANES - 1), 0)),
        full(conv_w.shape),
        cur(rqkv.shape[2]), cur(gates.shape[2]),
        pl.BlockSpec((1, 2, mc, RET_WIDTH, RET_WIDTH), lambda i, t: (i, 0, mixed(t), 0, 0)),
        full(dm.shape), full(qdec.shape),
        cur(ATTN_WIDTH),
        pl.BlockSpec((1, CHUNK, wkv), lambda i, t: (i, jnp.maximum(mixed(t) * mc - 1, 0), 0)),
        cur(wkv),
        pl.BlockSpec((1, CHUNK, wkv), lambda i, t: (i, jnp.minimum((mixed(t) + 1) * mc, last), 0)),
        pl.BlockSpec((1, n_ctx, wkv), lambda i, t: (i, 0, 0)),
        full(sink_rows.shape),
        out(rows, d),
        pl.BlockSpec((1, M_ROWS, d), lambda i, t: (i, 0, 0)),
        full((1, d)),
        pl.BlockSpec((None,) + w_out_bf.shape[1:], lambda i, t: (layer, 0, 0)),
        full((d, LANES)), full((d, LANES)),
    ]
    args = [cz, cz, cz, conv_w, rqkv, gates, sp, dm, qdec, aq, kv, kv, kv, kv, sink_rows,
            x, mrows, norm_g.reshape(1, d), w_out_bf, wr_hi, wr_lo]
    return pl.pallas_call(
        functools.partial(_mixer_kernel, n_ctx // CHUNK, n_chunks, n_exp),
        grid=(b, n_steps + 1),
        in_specs=in_specs,
        out_specs=[out(rows, d), out(rows * SUBLANES, LANES), out(rows, LANES)],
        out_shape=[
            jax.ShapeDtypeStruct((b, tu, d), F32),
            jax.ShapeDtypeStruct((b, tu * SUBLANES, LANES), F32),
            jax.ShapeDtypeStruct((b, tu, LANES), F32),
        ],
        scratch_shapes=[pltpu.VMEM((rows, CONV_WIDTH + RET_WIDTH + ATTN_WIDTH), BF16)],
        compiler_params=_params("parallel", "arbitrary"),
        name="mixer",
    )(*args)


def _outproj_tail(n_ctx, n_exp, tile, mix, x_ref, mod_ref, g_ref, w_ref, wrh_ref, wrl_ref, xo_ref, hg_ref, aff_ref):
    tm = x_ref.shape[1]
    row = _mod_rows(mod_ref, tm, n_ctx, tile)

    y = _dot(mix, w_ref[...])
    yield
    x = x_ref[0] + row(M_G1) * y
    xo_ref[0] = x
    h = (_rms(x) * g_ref[...]) * (1.0 + row(M_SC2)) + row(M_SH2)
    for q in range(SUBLANES):
        hg_ref[0, pl.ds(q, tm, stride=SUBLANES), :] = h[:, q * LANES:(q + 1) * LANES]
    yield
    h_hi = h.astype(BF16)
    h_lo = (h - h_hi.astype(F32)).astype(BF16)
    logits = _dot(h_hi, wrh_ref[...]) + (_dot(h_lo, wrh_ref[...]) + _dot(h_hi, wrl_ref[...]))
    yield
    lane = lax.broadcasted_iota(I32, (1, LANES), 1)
    logits = jnp.where(lane < n_exp, logits, NEG_INF)
    e = jnp.exp(logits - jnp.max(logits, axis=1, keepdims=True))
    aff_ref[0] = e / jnp.sum(e, axis=1, keepdims=True)


ROUTE_TABLES = 4
ROUTE_GROUP = 4


def _route_stream_tables(v, k, tab_scr, ce_scr, cnt_scr, stream, thr):
    n_exp, n = v.shape
    lane = lax.broadcasted_iota(I32, (1, LANES), 1)
    tri_r = lax.broadcasted_iota(I32, (LANES, LANES), 0)
    tri_c = lax.broadcasted_iota(I32, (LANES, LANES), 1)
    upper = jnp.where(tri_r <= tri_c, 1.0, 0.0).astype(BF16)
    above = pltpu.bitcast(thr + 1, F32)
    floor = pltpu.bitcast(thr, F32)
    n_gt = jnp.sum(jnp.where(v >= above, 1, 0), axis=1, keepdims=True)
    need = (k - n_gt).astype(F32)
    seen_eq = jnp.zeros((n_exp, 1), F32)
    counts = jnp.zeros((n_exp, LANES), F32)
    for j in range(n // LANES):
        vb = v[:, j * LANES:(j + 1) * LANES]
        gt = vb >= above
        eq = jnp.logical_and(vb >= floor, vb < above)
        both = jnp.concatenate([jnp.where(gt, 1.0, 0.0), jnp.where(eq, 1.0, 0.0)], axis=0).astype(BF16)
        pref = _dot(both, upper)
        rank_eq = pref[n_exp:] + seen_eq
        inc = pref[:n_exp] + jnp.minimum(rank_eq, need) - jnp.minimum(seen_eq, need)
        sel = jnp.logical_or(gt, jnp.logical_and(eq, rank_eq <= need))
        seen_eq = rank_eq[:, LANES - 1:LANES]
        counts = counts + jnp.where(lane == j, inc[:, LANES - 1:LANES], 0.0)
        a0 = jnp.where(sel, vb, 0.0)
        t0 = a0.astype(BF16).astype(F32)
        t1 = (a0 - t0).astype(BF16).astype(F32)
        t2 = a0 - t0 - t1
        for t, val in enumerate((inc, t0, t1, t2)):
            tab_scr[stream * ROUTE_TABLES + t, pl.ds(j, n_exp, stride=LANES), :] = val
    through = _dot(counts.astype(BF16), upper)
    for e in range(n_exp):
        ce_scr[stream, e] = through[e:e + 1, :]
        cnt_scr[stream, e] = counts[e:e + 1, :]


def _route_slots(e, stream, k, lo, tab_scr, ce_scr, cnt_scr):
    lane_f = lax.broadcasted_iota(I32, (1, LANES), 1).astype(F32)
    slot = lax.broadcasted_iota(I32, (k, LANES), 0).astype(F32)
    ones_rows = jnp.ones((LANES, LANES), BF16)
    cnt_rows = jnp.broadcast_to(cnt_scr[stream, e], (LANES, LANES)).astype(BF16)
    rows_e = pl.ds(pl.multiple_of(e * LANES, LANES), LANES)
    tab = jnp.concatenate([tab_scr[stream * ROUTE_TABLES + t, rows_e, :] for t in range(ROUTE_TABLES)],
                          axis=1).astype(BF16)
    before = jnp.where(ce_scr[stream, e] <= slot, 1.0, 0.0).astype(BF16)
    blk = _dot_nt(before, ones_rows)
    base = _dot_nt(before, cnt_rows)
    yield
    row = _dot(jnp.where(lane_f == blk, 1.0, 0.0).astype(BF16), tab)
    yield
    inc = row[:, :LANES]
    aff = row[:, LANES:2 * LANES] + row[:, 2 * LANES:3 * LANES] + row[:, 3 * LANES:]
    local = slot - base
    pos = _dot_nt(jnp.where(inc <= local, 1.0, 0.0).astype(BF16), ones_rows)
    yield
    tok = blk * LANES + pos + float(lo)
    gate = jnp.sum(jnp.where(inc == local + 1.0, aff, 0.0), axis=1, keepdims=True)
    return tok, gate


def _route_kernel(n_exp, streams, aff_ref, idx_ref, gate_ref, tab_scr, ce_scr, cnt_scr):
    a = aff_ref[0].T[:n_exp, :]
    tab_scr[...] = jnp.zeros_like(tab_scr)
    vs = [a[:, lo:lo + n] for lo, n, _, _ in streams]

    def search(i, ts):
        out = []
        for t, v, (_, _, k, _) in zip(ts, vs, streams):
            cand = t | jnp.left_shift(jnp.int32(1), 30 - i)
            cnt = jnp.sum(jnp.where(v >= pltpu.bitcast(cand, F32), 1, 0), axis=1, keepdims=True)
            out.append(jnp.where(cnt >= k, cand, t))
        return tuple(out)

    thrs = lax.fori_loop(0, 31, search, tuple(jnp.zeros((n_exp, 1), I32) for _ in streams))
    for st, (v, (_, _, k, _)) in enumerate(zip(vs, streams)):
        _route_stream_tables(v, k, tab_scr, ce_scr, cnt_scr, st, thrs[st])

    slots = idx_ref.shape[3]

    assert [s[3] for s in streams] == [sum(s[2] for s in streams[:n]) for n in range(len(streams))]

    def expert_group(g, carry):
        experts = [g * ROUTE_GROUP + u for u in range(ROUTE_GROUP)]
        found = _round_robin([_route_slots(e, st, k, lo, tab_scr, ce_scr, cnt_scr)
                              for e in experts for st, (lo, _, k, _) in enumerate(streams)])
        for u, e in enumerate(experts):
            toks = []
            for st, (_, _, k, slot0) in enumerate(streams):
                tok, gate = found[u * len(streams) + st]
                toks.append(tok)
                gate_ref[0, pl.ds(e, 1), slot0:slot0 + k, :] = gate[None]
            toks.append(jnp.zeros((-slots % LANES, LANES), F32))
            row = jnp.concatenate(toks, axis=0).T[0:1, :slots]
            idx_ref[0, pl.ds(e, 1)] = row.astype(I32)[None]
        return carry

    assert n_exp % ROUTE_GROUP == 0
    lax.fori_loop(0, n_exp // ROUTE_GROUP, expert_group, 0)


def _route(aff, n_exp, n_ctx):
    b, tu, _ = aff.shape
    n_lat = tu - n_ctx
    cap_l = CAPACITY_FACTOR * n_lat // n_exp
    cap_c = CAPACITY_FACTOR * n_ctx // n_exp
    slots = cap_l + cap_c
    assert n_lat // LANES <= LANES and n_ctx // LANES <= LANES
    streams = ((n_ctx, n_lat, cap_l, 0), (0, n_ctx, cap_c, cap_l))
    return pl.pallas_call(
        functools.partial(_route_kernel, n_exp, streams),
        grid=(b,),
        in_specs=[pl.BlockSpec((1, tu, LANES), lambda i: (i, 0, 0))],
        out_specs=[pl.BlockSpec((1, n_exp, 1, slots), lambda i: (i, 0, 0, 0)),
                   pl.BlockSpec((1, n_exp, slots, 1), lambda i: (i, 0, 0, 0))],
        out_shape=[jax.ShapeDtypeStruct((b, n_exp, 1, slots), I32),
                   jax.ShapeDtypeStruct((b, n_exp, slots, 1), F32)],
        scratch_shapes=[pltpu.VMEM((len(streams) * ROUTE_TABLES, n_exp * LANES, LANES), F32),
                        pltpu.VMEM((len(streams), n_exp, 1, LANES), F32),
                        pltpu.VMEM((len(streams), n_exp, 1, LANES), F32)],
        compiler_params=_params("parallel"),
        name="route",
    )(aff)


def _slot_pitch(slots):
    return slots + SUBLANES


def _zero_slot_padding(ref, lead, slots, pitch):
    for q in range(SUBLANES):
        ref[lead + (pl.ds(q * pitch + slots, pitch - slots), slice(None))] = jnp.zeros((pitch - slots, LANES), ref.dtype)


BF16_ROWS = 2 * SUBLANES


def _slot_pitch_bf16(slots):
    return -(-slots // BF16_ROWS) * BF16_ROWS + BF16_ROWS


DISPATCH_EXPERTS = 2


def _gather_kernel(slots, pitch, pitch_out, idx_ref, h_ref, o_ref, tile_scr):
    for g in range(DISPATCH_EXPERTS):
        for mi in range(slots):
            r = pl.multiple_of(idx_ref[g, 0, mi] * SUBLANES, SUBLANES)
            tile_scr[g, pl.ds(mi, SUBLANES, stride=pitch), :] = h_ref[0, pl.ds(r, SUBLANES), :]
        _zero_slot_padding(o_ref, (0, g), slots, pitch_out)
        for q in range(SUBLANES):
            o_ref[0, g, q * pitch_out:q * pitch_out + slots, :] = (
                tile_scr[g, q * pitch:q * pitch + slots, :].astype(BF16))


def _gather(hg, idx_rows, n_exp, slots):
    b = hg.shape[0]
    pitch = _slot_pitch(slots)
    pitch_out = _slot_pitch_bf16(slots)
    ge = DISPATCH_EXPERTS
    assert n_exp % ge == 0
    return pl.pallas_call(
        functools.partial(_gather_kernel, slots, pitch, pitch_out),
        grid=(b, n_exp // ge),
        in_specs=[
            pl.BlockSpec((ge, 1, slots), lambda i, e: (i * (n_exp // ge) + e, 0, 0), memory_space=pltpu.SMEM),
            pl.BlockSpec((1,) + hg.shape[1:], lambda i, e: (i, 0, 0)),
        ],
        out_specs=pl.BlockSpec((1, ge, SUBLANES * pitch_out, LANES), lambda i, e: (i, e, 0, 0)),
        out_shape=jax.ShapeDtypeStruct((b, n_exp, SUBLANES * pitch_out, LANES), BF16),
        scratch_shapes=[pltpu.VMEM((ge, SUBLANES * pitch, LANES), F32)],
        compiler_params=_params("parallel", "arbitrary"),
        name="gather",
    )(idx_rows, hg)


FFN_SAMPLES = 2
FFN_VMEM_LIMIT_BYTES = 60 * 1024 * 1024


def _ffn_kernel(slots, pitch_in, pitch, xs_ref, wg_ref, wu_ref, wd_ref, gate_ref, y_ref):
    n = xs_ref.shape[0]
    x = jnp.concatenate(
        [jnp.concatenate([xs_ref[s, 0, q * pitch_in:q * pitch_in + slots, :] for q in range(SUBLANES)], axis=1)
         for s in range(n)], axis=0)
    gate = jnp.concatenate([gate_ref[s, 0] for s in range(n)], axis=0)
    a = _dot(x, wg_ref[0].astype(BF16))
    u = _dot(x, wu_ref[0].astype(BF16))
    y = _dot((_silu(a) * u).astype(BF16), wd_ref[0].astype(BF16)) * gate
    for s in range(n):
        _zero_slot_padding(y_ref, (s, 0), slots, pitch)
        for q in range(SUBLANES):
            y_ref[s, 0, q * pitch:q * pitch + slots, :] = y[s * slots:(s + 1) * slots, q * LANES:(q + 1) * LANES]


def _ffn(xs, w_gate, w_up, w_down, layer, gate, slots):
    b, n_exp = xs.shape[:2]
    pitch = _slot_pitch(slots)
    pitch_in = _slot_pitch_bf16(slots)
    d, f = w_gate.shape[2:]
    ns = FFN_SAMPLES
    assert b % ns == 0
    slot_tile = lambda p: pl.BlockSpec((ns, 1, SUBLANES * p, LANES), lambda e, i: (i, e, 0, 0))
    weight = lambda rows, cols: pl.BlockSpec((None, 1, rows, cols), lambda e, i: (layer, e, 0, 0))
    return pl.pallas_call(
        functools.partial(_ffn_kernel, slots, pitch_in, pitch),
        grid=(n_exp, b // ns),
        in_specs=[
            slot_tile(pitch_in),
            weight(d, f), weight(d, f), weight(f, d),
            pl.BlockSpec((ns, 1, slots, 1), lambda e, i: (i, e, 0, 0)),
        ],
        out_specs=slot_tile(pitch),
        out_shape=jax.ShapeDtypeStruct((b, n_exp, SUBLANES * pitch, LANES), F32),
        compiler_params=_params("parallel", "arbitrary", vmem=FFN_VMEM_LIMIT_BYTES),
        name="ffn",
    )(xs, w_gate, w_up, w_down, gate)


SCATTER_BATCH = 16


def _scatter_kernel(slots, pitch, idx_ref, y_ref, o_ref):
    @pl.when(pl.program_id(1) == 0)
    def _():
        o_ref[...] = jnp.zeros_like(o_ref)

    for g in range(DISPATCH_EXPERTS):
        for m0 in range(0, slots, SCATTER_BATCH):
            rows = [pl.multiple_of(idx_ref[g, 0, m0 + u] * SUBLANES, SUBLANES) for u in range(SCATTER_BATCH)]
            vals = [o_ref[0, pl.ds(rows[u], SUBLANES), :] + y_ref[0, g, pl.ds(m0 + u, SUBLANES, stride=pitch), :]
                    for u in range(SCATTER_BATCH)]
            for u in range(SCATTER_BATCH):
                o_ref[0, pl.ds(rows[u], SUBLANES), :] = vals[u]


def _scatter(y, idx_rows, tu, slots):
    b, n_exp = y.shape[:2]
    pitch = _slot_pitch(slots)
    ge = DISPATCH_EXPERTS
    assert n_exp % ge == 0 and slots % SCATTER_BATCH == 0
    return pl.pallas_call(
        functools.partial(_scatter_kernel, slots, pitch),
        grid=(b, n_exp // ge),
        in_specs=[
            pl.BlockSpec((ge, 1, slots), lambda i, e: (i * (n_exp // ge) + e, 0, 0), memory_space=pltpu.SMEM),
            pl.BlockSpec((1, ge, SUBLANES * pitch, LANES), lambda i, e: (i, e, 0, 0)),
        ],
        out_specs=pl.BlockSpec((1, tu * SUBLANES, LANES), lambda i, e: (i, 0, 0)),
        out_shape=jax.ShapeDtypeStruct((b, tu * SUBLANES, LANES), F32),
        compiler_params=_params("parallel", "arbitrary"),
        name="scatter",
    )(idx_rows, y)


def _final_kernel(x_ref, f_ref, mod_ref, g_ref, o_ref):
    tm = x_ref.shape[1]
    x = x_ref[0] + mod_ref[0][M_G2:M_G2 + 1, :] * _from_gather_layout(f_ref, tm)
    o_ref[0] = _rms(x) * g_ref[...]


def _final(x, ffn, mrows, final_g, n_ctx):
    b, tu, d = x.shape
    tm = FINAL_TILE
    skip = n_ctx // tm
    return pl.pallas_call(
        _final_kernel,
        grid=(b, (tu - n_ctx) // tm),
        in_specs=[
            pl.BlockSpec((1, tm, d), lambda i, j: (i, j + skip, 0)),
            pl.BlockSpec((1, tm * SUBLANES, LANES), lambda i, j: (i, j + skip, 0)),
            pl.BlockSpec((1, M_ROWS, d), lambda i, j: (i, 0, 0)),
            pl.BlockSpec((1, d), lambda i, j: (0, 0)),
        ],
        out_specs=pl.BlockSpec((1, tm, d), lambda i, j: (i, j, 0)),
        out_shape=jax.ShapeDtypeStruct((b, tu - n_ctx, d), F32),
        compiler_params=_params("parallel", "parallel"),
        name="final_norm",
    )(x, ffn, mrows, final_g.reshape(1, d))


def _rope_tables(n_lat, n_ctx):
    rows = n_lat // GRID_W
    rowp = jnp.repeat(jnp.arange(rows, dtype=F32), GRID_W)
    colp = jnp.tile(jnp.arange(GRID_W, dtype=F32), rows)
    axis_dim = HEAD_DIM // 2
    inv_freq = ROPE_BASE ** (-jnp.arange(0, axis_dim, 2, dtype=F32) / axis_dim)
    ar = rowp[:, None] * inv_freq
    ac = colp[:, None] * inv_freq
    cs = jnp.concatenate([jnp.cos(ar), jnp.cos(ar), jnp.cos(ac), jnp.cos(ac)], axis=1)
    sn = jnp.concatenate([-jnp.sin(ar), jnp.sin(ar), -jnp.sin(ac), jnp.sin(ac)], axis=1)
    reps = LANES // HEAD_DIM
    cs = jnp.concatenate([jnp.ones((n_ctx, LANES), F32), jnp.tile(cs, (1, reps))], axis=0)
    sn = jnp.concatenate([jnp.zeros((n_ctx, LANES), F32), jnp.tile(sn, (1, reps))], axis=0)
    return cs, sn


def kernel(x, c, ctx, c_ctx, w_mod, b_mod, norm1_g, norm2_g, w_in, conv_w, ret_decay_logit, attn_sink,
           w_out, w_router, w_gate, w_up, w_down, final_g):
    b, n_lat, d = x.shape
    n_ctx = ctx.shape[1]
    depth = w_in.shape[0]
    n_exp = w_router.shape[2]
    tu = n_ctx + n_lat
    assert w_in.shape[2] == IN_WIDTH and tu % INPROJ_TILE == 0
    assert n_ctx % FINAL_TILE == 0 and n_lat % FINAL_TILE == 0
    assert b + 1 <= SUBLANES and n_exp == N_EXPERTS

    c_rows = jnp.concatenate([c, c_ctx[None], jnp.zeros((SUBLANES - b - 1, d), F32)], axis=0)
    mods = _mod_vectors(c_rows, w_mod, b_mod).reshape(depth, SUBLANES, 6, d)
    qdec, win, cd, dm = _decay_tables(ret_decay_logit)
    rope_cs, rope_sn = _rope_tables(n_lat, n_ctx)
    sink_rows = jnp.broadcast_to(jnp.repeat(attn_sink.astype(F32), CHUNK, axis=1)[:, :, None],
                                 (depth, ATTN_HEADS * CHUNK, LANES))
    w_in_bf = w_in.astype(BF16)
    w_out_bf = w_out.astype(BF16)

    xu = jnp.concatenate([ctx, x], axis=1)
    ffn = None
    mrows = None
    cap = CAPACITY_FACTOR * n_lat // n_exp + CAPACITY_FACTOR * n_ctx // n_exp
    for l in range(depth):
        prev_mrows = mrows
        mrows = jnp.concatenate([mods[l, :b], jnp.broadcast_to(mods[l, b][None], (b, 6, d)),
                                 jnp.zeros((b, M_ROWS - 12, d), F32)], axis=1)
        xu, proj = _inproj(xu, ffn, mrows if ffn is None else _with_prev_g2(mrows, prev_mrows),
                           norm1_g[l], w_in_bf, l, rope_cs, rope_sn, n_ctx)
        sp = _ret_states(proj[1], win[l], cd[l], n_ctx)
        xu, hg, aff = _mixer(proj, sp, (dm[l], qdec[l]), conv_w[l], sink_rows[l],
                             xu, mrows, norm2_g[l], w_out_bf, l, w_router[l], n_ctx)
        idx, gate = _route(aff, n_exp, n_ctx)
        idx_rows = idx.reshape(b * n_exp, 1, cap)
        xs = _gather(hg, idx_rows, n_exp, cap)
        y = _ffn(xs, w_gate, w_up, w_down, l, gate, cap)
        ffn = _scatter(y, idx_rows, tu, cap)
    return _final(xu, ffn, mrows, final_g, n_ctx)


def _with_prev_g2(mrows, prev_mrows):
    out = mrows.at[:, M_G2].set(prev_mrows[:, M_G2])
    return out.at[:, M_CTX + M_G2].set(prev_mrows[:, M_CTX + M_G2])
```

```python
import functools

import jax
import jax.numpy as jnp
from jax import lax
from jax.experimental import pallas as pl
from jax.experimental.pallas import tpu as pltpu

F32 = jnp.float32
BF16 = jnp.bfloat16
I32 = jnp.int32
HIGHEST = lax.Precision.HIGHEST

HEAD_DIM = 64
CONV_WIDTH = 256
RET_HEADS = 4
RET_WIDTH = RET_HEADS * HEAD_DIM
ATTN_HEADS = 8
ATTN_KV_HEADS = 2
ATTN_GROUP = ATTN_HEADS // ATTN_KV_HEADS
ATTN_WIDTH = ATTN_HEADS * HEAD_DIM
KV_WIDTH = ATTN_KV_HEADS * HEAD_DIM
CHUNK = 128
GRID_W = 64
N_EXPERTS = 16
CAPACITY_FACTOR = 2
ROPE_BASE = 10000.0
EPS = 1e-6
NEG_INF = -1e30
QK_SCALE = HEAD_DIM ** -0.5
LOG2E = 1.4426950408889634

LANES = 128
SUBLANES = 8
VMEM_LIMIT_BYTES = 56 * 1024 * 1024

O_CB = 0
O_CC = O_CB + CONV_WIDTH
O_CX = O_CC + CONV_WIDTH
O_RQ = O_CX + CONV_WIDTH
O_RK = O_RQ + RET_WIDTH
O_RV = O_RK + RET_WIDTH
O_GF = O_RV + RET_WIDTH
O_GB = O_GF + RET_WIDTH
O_AQ = O_GB + RET_WIDTH
O_AK = O_AQ + ATTN_WIDTH
O_AV = O_AK + KV_WIDTH
IN_WIDTH = O_AV + KV_WIDTH

M_SH1, M_SC1, M_G1, M_SH2, M_SC2, M_G2 = range(6)
M_CTX = 6
M_ROWS = 16

INPROJ_TILE = 544
FINAL_TILE = 256


def _params(*sem, vmem=VMEM_LIMIT_BYTES):
    return pltpu.CompilerParams(dimension_semantics=sem, vmem_limit_bytes=vmem)


def _dot(a, b):
    return jnp.dot(a, b, preferred_element_type=F32)


def _dot_nt(a, b):
    return lax.dot_general(a, b, (((1,), (1,)), ((), ())), preferred_element_type=F32)


def _silu(v):
    return v * jax.nn.sigmoid(v)


def _mod_kernel(c_ref, w_ref, b_ref, o_ref):
    s = _silu(c_ref[...])
    o_ref[0] = jnp.dot(s, w_ref[0], precision=HIGHEST, preferred_element_type=F32) + b_ref[0]


def _mod_vectors(c_rows, w_mod, b_mod):
    depth, d_model, width = w_mod.shape
    tn = 1536
    return pl.pallas_call(
        _mod_kernel,
        grid=(depth, width // tn),
        in_specs=[
            pl.BlockSpec((SUBLANES, d_model), lambda l, n: (0, 0)),
            pl.BlockSpec((1, d_model, tn), lambda l, n: (l, 0, n)),
            pl.BlockSpec((1, 1, tn), lambda l, n: (l, 0, n)),
        ],
        out_specs=pl.BlockSpec((1, SUBLANES, tn), lambda l, n: (l, 0, n)),
        out_shape=jax.ShapeDtypeStruct((depth, SUBLANES, width), F32),
        compiler_params=_params("parallel", "parallel"),
        name="mod_vectors",
    )(c_rows, w_mod, b_mod.reshape(depth, 1, width))


def _log_sigmoid(v):
    return -jnp.log(1.0 + jnp.exp(-v))


def _tables_kernel(lgl_ref, lgr_ref, qdec_ref, win_ref, cd_ref, dm_ref):
    pos = lax.broadcasted_iota(I32, (CHUNK, RET_WIDTH), 0).astype(F32)
    ri = lax.broadcasted_iota(I32, (CHUNK, RET_HEADS * CHUNK), 0).astype(F32)
    rj = (lax.broadcasted_iota(I32, (CHUNK, RET_HEADS * CHUNK), 1) & (CHUNK - 1)).astype(F32)
    for d in range(2):
        lg = _log_sigmoid(lgl_ref[0, d])
        lg1 = lg[0:1, :]
        if d == 0:
            qdec_ref[0, d] = jnp.exp(lg1 * (pos + 1.0))
            win_ref[0, d] = jnp.exp(lg1 * (CHUNK - 1.0 - pos))
            diff = ri - rj
        else:
            qdec_ref[0, d] = jnp.exp(lg1 * (CHUNK - pos))
            win_ref[0, d] = jnp.exp(lg1 * pos)
            diff = rj - ri
        cd_ref[0, d] = jnp.exp(lg * float(CHUNK))
        lr = _log_sigmoid(lgr_ref[0, d])
        dm_ref[0, d] = jnp.where(diff >= 0.0, jnp.exp(lr * jnp.maximum(diff, 0.0)), 0.0)


def _decay_tables(ret_decay_logit):
    depth = ret_decay_logit.shape[0]
    lg = ret_decay_logit.astype(F32)
    lgl = jnp.broadcast_to(jnp.repeat(lg, HEAD_DIM, axis=-1)[:, :, None, :], (depth, 2, SUBLANES, RET_WIDTH))
    lgr = jnp.broadcast_to(jnp.repeat(lg, CHUNK, axis=-1)[:, :, None, :], (depth, 2, CHUNK, RET_HEADS * CHUNK))
    return pl.pallas_call(
        _tables_kernel,
        grid=(depth,),
        in_specs=[
            pl.BlockSpec((1, 2, SUBLANES, RET_WIDTH), lambda l: (l, 0, 0, 0)),
            pl.BlockSpec((1, 2, CHUNK, RET_HEADS * CHUNK), lambda l: (l, 0, 0, 0)),
        ],
        out_specs=[
            pl.BlockSpec((1, 2, CHUNK, RET_WIDTH), lambda l: (l, 0, 0, 0)),
            pl.BlockSpec((1, 2, CHUNK, RET_WIDTH), lambda l: (l, 0, 0, 0)),
            pl.BlockSpec((1, 2, SUBLANES, RET_WIDTH), lambda l: (l, 0, 0, 0)),
            pl.BlockSpec((1, 2, CHUNK, RET_HEADS * CHUNK), lambda l: (l, 0, 0, 0)),
        ],
        out_shape=[
            jax.ShapeDtypeStruct((depth, 2, CHUNK, RET_WIDTH), F32),
            jax.ShapeDtypeStruct((depth, 2, CHUNK, RET_WIDTH), F32),
            jax.ShapeDtypeStruct((depth, 2, SUBLANES, RET_WIDTH), F32),
            jax.ShapeDtypeStruct((depth, 2, CHUNK, RET_HEADS * CHUNK), F32),
        ],
        compiler_params=_params("parallel"),
        name="decay_tables",
    )(lgl, lgr)


def _rope(v, cs, sn):
    lane = lax.broadcasted_iota(I32, (1, LANES), 1)
    first = (lane & 31) < 16
    outs = []
    for g in range(v.shape[1] // LANES):
        vg = v[:, g * LANES:(g + 1) * LANES]
        sw = jnp.where(first, pltpu.roll(vg, LANES - 16, 1), pltpu.roll(vg, 16, 1))
        outs.append(vg * cs + sw * sn)
    return outs[0] if len(outs) == 1 else jnp.concatenate(outs, axis=1)


def _swap_halves(v):
    return jnp.concatenate([v[:, HEAD_DIM:], v[:, :HEAD_DIM]], axis=1)


def _from_gather_layout(f_ref, tm):
    return jnp.concatenate(
        [f_ref[0, pl.ds(q, tm, stride=SUBLANES), :] for q in range(SUBLANES)], axis=1)


def _rms(v):
    return v * lax.rsqrt(jnp.mean(v * v, axis=-1, keepdims=True) + EPS)


def _mod_rows(mod_ref, tm, n_ctx, tile):
    m = mod_ref[0]
    if n_ctx % tm == 0:
        is_ctx = tile < n_ctx // tm
    else:
        is_ctx = tile * tm + lax.broadcasted_iota(I32, (tm, 1), 0) < n_ctx

    def row(r):
        return jnp.where(is_ctx, m[M_CTX + r:M_CTX + r + 1, :], m[r:r + 1, :])

    return row


def _inproj_kernel(has_ffn, n_ctx, *refs):
    if has_ffn:
        x_ref, f_ref, mod_ref, g_ref, w_ref, cs_ref, sn_ref = refs[:7]
        outs = refs[7:]
        xo_ref = outs[0]
        outs = outs[1:]
    else:
        x_ref, mod_ref, g_ref, w_ref, cs_ref, sn_ref = refs[:6]
        outs = refs[6:]
    cz_ref, rqkv_ref, gates_ref, aq_ref, kv_ref = outs
    tm = x_ref.shape[1]
    row = _mod_rows(mod_ref, tm, n_ctx, pl.program_id(1))

    x = x_ref[0]
    if has_ffn:
        x = x + row(M_G2) * _from_gather_layout(f_ref, tm)
        xo_ref[0] = x
    h = (_rms(x) * g_ref[...]) * (1.0 + row(M_SC1)) + row(M_SH1)
    h = h.astype(BF16)
    cs = cs_ref[...]
    sn = sn_ref[...]

    def proj(a, b):
        return _dot(h, w_ref[:, a:b])

    bounds = (O_CB, O_CC, O_CX, O_RQ, O_RK, O_RV, O_GF, O_GB, O_AQ, O_AK, O_AV, IN_WIDTH)
    p = [proj(a, b) for a, b in zip(bounds[:-1], bounds[1:])]
    cz_ref[0, :, 0:CONV_WIDTH] = p[0]
    cz_ref[0, :, CONV_WIDTH:] = p[1] * p[2]
    rqkv_ref[0, :, 0:RET_WIDTH] = _rope(p[3], cs, sn).astype(BF16)
    rqkv_ref[0, :, RET_WIDTH:2 * RET_WIDTH] = _rope(p[4] * QK_SCALE, cs, sn).astype(BF16)
    rqkv_ref[0, :, 2 * RET_WIDTH:] = p[5].astype(BF16)
    gates_ref[0, :, 0:RET_WIDTH] = _silu(p[6])
    gates_ref[0, :, RET_WIDTH:] = _silu(p[7])
    aq_ref[0] = (_rope(p[8], cs, sn) * (QK_SCALE * LOG2E)).astype(BF16)
    ak = _rope(p[9], cs, sn)
    av = p[10]
    kv_ref[0, :, 0:KV_WIDTH] = ak.astype(BF16)
    kv_ref[0, :, KV_WIDTH:2 * KV_WIDTH] = _swap_halves(ak).astype(BF16)
    kv_ref[0, :, 2 * KV_WIDTH:] = av.astype(BF16)


def _inproj(x, ffn, mrows, norm_g, w_in_bf, layer, rope_cs, rope_sn, n_ctx):
    b, tu, d = x.shape
    tm = INPROJ_TILE
    has_ffn = ffn is not None
    tok = lambda width: pl.BlockSpec((1, tm, width), lambda i, j: (i, j, 0))
    in_specs = [tok(d)]
    args = [x]
    if has_ffn:
        in_specs.append(pl.BlockSpec((1, tm * SUBLANES, LANES), lambda i, j: (i, j, 0)))
        args.append(ffn)
    in_specs += [
        pl.BlockSpec((1, M_ROWS, d), lambda i, j: (i, 0, 0)),
        pl.BlockSpec((1, d), lambda i, j: (0, 0)),
        pl.BlockSpec((None, d, IN_WIDTH), lambda i, j: (layer, 0, 0)),
        pl.BlockSpec((tm, LANES), lambda i, j: (j, 0)),
        pl.BlockSpec((tm, LANES), lambda i, j: (j, 0)),
    ]
    args += [mrows, norm_g.reshape(1, d), w_in_bf, rope_cs, rope_sn]
    widths = [(2 * CONV_WIDTH, F32), (3 * RET_WIDTH, BF16), (2 * RET_WIDTH, F32), (ATTN_WIDTH, BF16),
              (3 * KV_WIDTH, BF16)]
    out_specs = [tok(w) for w, _ in widths]
    out_shape = [jax.ShapeDtypeStruct((b, tu, w), dt) for w, dt in widths]
    if has_ffn:
        out_specs = [tok(d)] + out_specs
        out_shape = [jax.ShapeDtypeStruct((b, tu, d), F32)] + out_shape
    res = pl.pallas_call(
        functools.partial(_inproj_kernel, has_ffn, n_ctx),
        grid=(b, tu // tm),
        in_specs=in_specs,
        out_specs=out_specs,
        out_shape=out_shape,
        compiler_params=_params("parallel", "parallel"),
        name="inproj",
    )(*args)
    if has_ffn:
        return res[0], res[1:]
    return x, res


def _head_block_mask(n):
    r = lax.broadcasted_iota(I32, (n, n), 0) // HEAD_DIM
    c = lax.broadcasted_iota(I32, (n, n), 1) // HEAD_DIM
    return r == c


STATE_UNROLL = 2


def _states_kernel(n_ctx_chunks, n_chunks, rqkv_ref, win_ref, cd_ref, sp_ref, s_scr):
    same_head = _head_block_mask(RET_WIDTH)
    s_scr[...] = jnp.zeros_like(s_scr)

    def chunk_update(d, pos):
        if d == 0:
            c = pos
        else:
            c = jnp.where(pos < n_ctx_chunks, n_ctx_chunks - 1 - pos, n_chunks + n_ctx_chunks - 1 - pos)
        off = pl.multiple_of(c * CHUNK, CHUNK)
        kw = rqkv_ref[0, pl.ds(off, CHUNK), RET_WIDTH:2 * RET_WIDTH].astype(F32) * win_ref[d]
        v = rqkv_ref[0, pl.ds(off, CHUNK), 2 * RET_WIDTH:]
        kw_t = kw.T.astype(BF16)
        yield
        u = _dot(kw_t, v)
        yield
        return c, jnp.where(same_head, u, 0.0)

    def body(i, carry):
        todo = [(d, i * STATE_UNROLL + n) for n in range(STATE_UNROLL) for d in range(2)]
        done = _round_robin([chunk_update(d, pos) for d, pos in todo])
        for (d, _), (c, u) in zip(todo, done):
            s = s_scr[d]
            sp_ref[0, d, pl.ds(c, 1)] = s.astype(BF16)[None]
            s_scr[d] = s * cd_ref[d][0:1, :] + u
        return carry

    assert n_chunks % STATE_UNROLL == 0
    lax.fori_loop(0, n_chunks // STATE_UNROLL, body, 0)


def _ret_states(rqkv, win, cd, n_ctx):
    b, tu, _ = rqkv.shape
    w = RET_WIDTH
    n_chunks = tu // CHUNK
    return pl.pallas_call(
        functools.partial(_states_kernel, n_ctx // CHUNK, n_chunks),
        grid=(b,),
        in_specs=[
            pl.BlockSpec((1, tu, 3 * w), lambda i: (i, 0, 0)),
            pl.BlockSpec((2, CHUNK, w), lambda i: (0, 0, 0)),
            pl.BlockSpec((2, SUBLANES, w), lambda i: (0, 0, 0)),
        ],
        out_specs=pl.BlockSpec((1, 2, n_chunks, w, w), lambda i: (i, 0, 0, 0, 0)),
        out_shape=jax.ShapeDtypeStruct((b, 2, n_chunks, w, w), BF16),
        scratch_shapes=[pltpu.VMEM((2, w, w), F32)],
        compiler_params=_params("parallel"),
        name="ret_states",
    )(rqkv, win, cd)


def _group_mean(t, ones_bd):
    hi = t.astype(BF16)
    lo = (t - hi.astype(F32)).astype(BF16)
    s = _dot(jnp.concatenate([hi, lo], axis=0), ones_bd)
    n = t.shape[0]
    return (s[:n] + s[n:]) * (1.0 / HEAD_DIM)


def _mix_chunk(c, n_ctx_chunks, n_chunks, cz, z_last, z_first, cw, rqkv, gates, sp, dm_ref, qdec_ref,
               qa, kvs, sink_ref):
    is_lat = c >= n_ctx_chunks

    z = cz[:, CONV_WIDTH:]
    row = lax.broadcasted_iota(I32, (CHUNK, 1), 0)
    has_prev = jnp.logical_and(c != 0, c != n_ctx_chunks)
    has_next = jnp.logical_and(c != n_ctx_chunks - 1, c != n_chunks - 1)
    z_before = jnp.where(row == 0, jnp.where(has_prev, z_last, 0.0), pltpu.roll(z, 1, 0))
    z_after = jnp.where(row == CHUNK - 1, jnp.where(has_next, z_first, 0.0), pltpu.roll(z, CHUNK - 1, 0))
    conv = cz[:, :CONV_WIDTH] * (z_before * cw[0:1, :] + z * cw[1:2, :] + z_after * cw[2:3, :])

    q = rqkv[:, 0:RET_WIDTH]
    k = rqkv[:, RET_WIDTH:2 * RET_WIDTH]
    v = rqkv[:, 2 * RET_WIDTH:]
    lane_head = lax.broadcasted_iota(I32, (1, RET_WIDTH), 1) // HEAD_DIM
    kz = jnp.zeros_like(k)
    k_heads = jnp.concatenate([jnp.where(lane_head == hh, k, kz) for hh in range(RET_HEADS)], axis=0)
    v_heads = jnp.concatenate([jnp.where(lane_head == hh, v, kz) for hh in range(RET_HEADS)], axis=0)
    scores = _dot_nt(q, k_heads)

    keys, v_t = kvs
    half = lax.broadcasted_iota(I32, (1, LANES), 1) // HEAD_DIM
    qa_z = jnp.zeros((CHUNK, LANES), BF16)
    combos = [(hk, par) for hk in range(ATTN_KV_HEADS) for par in range(2)]
    n_ctx = keys[0].shape[0] - 3 * CHUNK
    cols2 = 2 * CHUNK
    ik = lax.broadcasted_iota(I32, (CHUNK, cols2), 0)
    iq = lax.broadcasted_iota(I32, (CHUNK, cols2), 1) & (CHUNK - 1)
    off = jnp.full((CHUNK, cols2), NEG_INF, F32)
    bias_prev = jnp.where(jnp.logical_and(jnp.logical_and(is_lat, c - 1 >= n_ctx_chunks), ik >= iq), 0.0, off)
    bias_cur = jnp.where(is_lat, 0.0, off)
    bias_next = jnp.where(jnp.logical_and(jnp.logical_and(is_lat, c + 1 <= n_chunks - 1), ik <= iq), 0.0, off)
    assert n_ctx > 0
    logits = {}
    for hk, par in combos:
        ja, jb = 2 * hk, 2 * hk + 1
        qst = jnp.concatenate([
            jnp.where(half == par, qa[:, ja * LANES:(ja + 1) * LANES], qa_z),
            jnp.where(half == par, qa[:, jb * LANES:(jb + 1) * LANES], qa_z)], axis=0)
        sel = 0 if par == hk else 1
        r = _dot_nt(keys[sel], qst)
        s = jnp.concatenate([r[:CHUNK] + bias_prev, r[CHUNK:2 * CHUNK] + bias_cur,
                             r[2 * CHUNK:3 * CHUNK] + bias_next, r[3 * CHUNK:]], axis=0)
        ha, hb = ATTN_GROUP * hk + par, ATTN_GROUP * hk + par + 2
        snk = jnp.concatenate([sink_ref[ha * CHUNK:ha * CHUNK + 1, :],
                               sink_ref[hb * CHUNK:hb * CHUNK + 1, :]], axis=1) * LOG2E
        logits[hk, par] = (s, snk, jnp.maximum(jnp.max(s, axis=0, keepdims=True), snk))
    yield

    qf = q.astype(F32)
    outs = []
    for d in range(2):
        p = (scores * dm_ref[d]).astype(BF16)
        lhs = jnp.concatenate([p, (qf * qdec_ref[d]).astype(BF16)], axis=1)
        o = _dot(lhs, jnp.concatenate([v_heads, sp[d]], axis=0))
        outs.append(o)
    o2 = jnp.concatenate(outs, axis=0)
    yield

    att = {}
    for n, (hk, par) in enumerate(combos):
        s, snk, mx = logits[hk, par]
        e = jnp.exp2(s - mx)
        den = jnp.sum(e, axis=0, keepdims=True) + jnp.exp2(snk - mx)
        o_t = _dot(v_t[hk * HEAD_DIM:(hk + 1) * HEAD_DIM, :], e.astype(BF16))
        att[hk, par] = o_t * (1.0 / den)
        if n == 0:
            ones_bd = jnp.where(_head_block_mask(RET_WIDTH), 1.0, 0.0).astype(BF16)
            dl = o2 - _group_mean(o2, ones_bd)
            on = dl * lax.rsqrt(_group_mean(dl * dl, ones_bd) + EPS)
            ret = on[:CHUNK] * gates[:, 0:RET_WIDTH] + on[CHUNK:] * gates[:, RET_WIDTH:]
        yield

    cols = []
    for hk in range(ATTN_KV_HEADS):
        cols.append(jnp.concatenate([att[hk, 0][:, :CHUNK], att[hk, 1][:, :CHUNK]], axis=0).T)
        cols.append(jnp.concatenate([att[hk, 0][:, CHUNK:], att[hk, 1][:, CHUNK:]], axis=0).T)
    return jnp.concatenate([conv, ret] + cols, axis=1)


def _round_robin(gens):
    results = [None] * len(gens)
    active = list(range(len(gens)))
    while active:
        for i in list(active):
            try:
                next(gens[i])
            except StopIteration as done:
                results[i] = done.value
                active.remove(i)
    return results


MIX_CHUNKS = 2


def _mixer_kernel(n_ctx_chunks, n_chunks, n_exp,
                  cz_ref, czp_ref, czn_ref, cw_ref, rqkv_ref, gates_ref, sp_ref, dm_ref, qdec_ref,
                  aq_ref, kvp_ref, kvc_ref, kvn_ref, kvx_ref, sink_ref,
                  x_ref, mod_ref, g_ref, w_ref, wrh_ref, wrl_ref, xo_ref, hg_ref, aff_ref, mix_scr):
    t = pl.program_id(1)
    n_steps = pl.num_programs(1) - 1

    @pl.when(t == 0)
    def _():
        mix_scr[...] = jnp.zeros_like(mix_scr)

    tail = _outproj_tail(n_ctx_chunks * CHUNK, n_exp, jnp.maximum(t - 1, 0), mix_scr[...], x_ref, mod_ref, g_ref,
                         w_ref, wrh_ref, wrl_ref, xo_ref, hg_ref, aff_ref)
    j = jnp.minimum(t, n_steps - 1)
    cw = cw_ref[...]

    def values_t(blk):
        return blk[:, 2 * KV_WIDTH:].astype(F32).T.astype(BF16)

    kv_blocks = ([kvp_ref[0]] + [kvc_ref[0, s * CHUNK:(s + 1) * CHUNK, :] for s in range(MIX_CHUNKS)] + [kvn_ref[0]])
    kv_ctx = [kvx_ref[0, t * CHUNK:(t + 1) * CHUNK, :] for t in range(kvx_ref.shape[1] // CHUNK)]
    vt_blocks = [values_t(blk) for blk in kv_blocks]
    vt_ctx = [values_t(blk) for blk in kv_ctx]
    gens = []
    for s in range(MIX_CHUNKS):
        lo, hi = s * CHUNK, (s + 1) * CHUNK
        if s == 0:
            z_last = czp_ref[0, SUBLANES - 1:SUBLANES, CONV_WIDTH:]
        else:
            z_last = cz_ref[0, lo - 1:lo, CONV_WIDTH:]
        if s == MIX_CHUNKS - 1:
            z_first = czn_ref[0, 0:1, CONV_WIDTH:]
        else:
            z_first = cz_ref[0, hi:hi + 1, CONV_WIDTH:]
        window = kv_blocks[s:s + 3] + kv_ctx
        keys = tuple(jnp.concatenate([blk[:, i * KV_WIDTH:(i + 1) * KV_WIDTH] for blk in window], axis=0)
                     for i in (0, 1))
        v_t = jnp.concatenate(vt_blocks[s:s + 3] + vt_ctx, axis=1)
        gens.append(_mix_chunk(j * MIX_CHUNKS + s, n_ctx_chunks, n_chunks, cz_ref[0, lo:hi, :], z_last, z_first,
                               cw, rqkv_ref[0, lo:hi, :], gates_ref[0, lo:hi, :],
                               (sp_ref[0, 0, s], sp_ref[0, 1, s]), dm_ref, qdec_ref, aq_ref[0, lo:hi, :],
                               (keys, v_t), sink_ref))
    mixes = [mix.astype(BF16) for mix in _round_robin([tail] + gens)[1:]]
    mix_scr[...] = jnp.concatenate(mixes, axis=0)


def _mixer(proj, sp, tabs, conv_w, sink_rows, x, mrows, norm_g, w_out_bf, layer, w_router, n_ctx):
    cz, rqkv, gates, aq, kv = proj
    dm, qdec = tabs
    b, tu, _ = cz.shape
    d = x.shape[2]
    n_exp = w_router.shape[1]
    wr = w_router.T
    wr_hi = wr.astype(BF16)
    wr_lo = (wr - wr_hi.astype(F32)).astype(BF16)
    n_chunks = tu // CHUNK
    mc = MIX_CHUNKS
    rows = mc * CHUNK
    assert n_chunks % mc == 0 and (n_ctx // CHUNK) % mc == 0
    last = n_chunks - 1
    n_steps = n_chunks // mc
    per8 = rows // SUBLANES
    mixed = lambda t: jnp.minimum(t, n_steps - 1)
    projected = lambda t: jnp.maximum(t - 1, 0)
    cur = lambda w: pl.BlockSpec((1, rows, w), lambda i, t: (i, mixed(t), 0))
    out = lambda r, w: pl.BlockSpec((1, r, w), lambda i, t: (i, projected(t), 0))
    full = lambda shape: pl.BlockSpec(shape, lambda i, t: (0,) * len(shape))
    wcz, wkv = cz.shape[2], kv.shape[2]
    in_specs = [
        cur(wcz),
        pl.BlockSpec((1, SUBLANES, wcz), lambda i, t: (i, jnp.maximum(mixed(t) * per8 - 1, 0), 0)),
        pl.BlockSpec((1, SUBLANES, wcz),
                     lambda i, t: (i, jnp.minimum((mixed(t) + 1) * per8, tu // SUBLANES - 1), 0)),
        full(conv_w.shape),
        cur(rqkv.shape[2]), cur(gates.shape[2]),
        pl.BlockSpec((1, 2, mc, RET_WIDTH, RET_WIDTH), lambda i, t: (i, 0, mixed(t), 0, 0)),
        full(dm.shape), full(qdec.shape),
        cur(ATTN_WIDTH),
        pl.BlockSpec((1, CHUNK, wkv), lambda i, t: (i, jnp.maximum(mixed(t) * mc - 1, 0), 0)),
        cur(wkv),
        pl.BlockSpec((1, CHUNK, wkv), lambda i, t: (i, jnp.minimum((mixed(t) + 1) * mc, last), 0)),
        pl.BlockSpec((1, n_ctx, wkv), lambda i, t: (i, 0, 0)),
        full(sink_rows.shape),
        out(rows, d),
        pl.BlockSpec((1, M_ROWS, d), lambda i, t: (i, 0, 0)),
        full((1, d)),
        pl.BlockSpec((None,) + w_out_bf.shape[1:], lambda i, t: (layer, 0, 0)),
        full((n_exp, d)), full((n_exp, d)),
    ]
    args = [cz, cz, cz, conv_w, rqkv, gates, sp, dm, qdec, aq, kv, kv, kv, kv, sink_rows,
            x, mrows, norm_g.reshape(1, d), w_out_bf, wr_hi, wr_lo]
    return pl.pallas_call(
        functools.partial(_mixer_kernel, n_ctx // CHUNK, n_chunks, n_exp),
        grid=(b, n_steps + 1),
        in_specs=in_specs,
        out_specs=[out(rows, d), out(rows * SUBLANES, LANES),
                   pl.BlockSpec((1, n_exp, rows), lambda i, t: (i, 0, projected(t)))],
        out_shape=[
            jax.ShapeDtypeStruct((b, tu, d), F32),
            jax.ShapeDtypeStruct((b, tu * SUBLANES, LANES), F32),
            jax.ShapeDtypeStruct((b, n_exp, tu), F32),
        ],
        scratch_shapes=[pltpu.VMEM((rows, CONV_WIDTH + RET_WIDTH + ATTN_WIDTH), BF16)],
        compiler_params=_params("parallel", "arbitrary"),
        name="mixer",
    )(*args)


def _outproj_tail(n_ctx, n_exp, tile, mix, x_ref, mod_ref, g_ref, w_ref, wrh_ref, wrl_ref, xo_ref, hg_ref, aff_ref):
    tm = x_ref.shape[1]
    row = _mod_rows(mod_ref, tm, n_ctx, tile)

    y = _dot(mix, w_ref[...])
    yield
    x = x_ref[0] + row(M_G1) * y
    xo_ref[0] = x
    h = (_rms(x) * g_ref[...]) * (1.0 + row(M_SC2)) + row(M_SH2)
    for q in range(SUBLANES):
        hg_ref[0, pl.ds(q, tm, stride=SUBLANES), :] = h[:, q * LANES:(q + 1) * LANES]
    yield
    h_hi = h.astype(BF16)
    h_lo = (h - h_hi.astype(F32)).astype(BF16)
    logits = (_dot_nt(wrh_ref[...], h_hi) + (_dot_nt(wrh_ref[...], h_lo) + _dot_nt(wrl_ref[...], h_hi)))
    yield
    e = jnp.exp(logits - jnp.max(logits, axis=0, keepdims=True))
    aff_ref[0] = e / jnp.sum(e, axis=0, keepdims=True)


ROUTE_TABLES = 4
ROUTE_GROUP = 4


def _route_stream_tables(v, k, tab_scr, ce_scr, cnt_scr, stream, thr):
    n_exp, n = v.shape
    lane = lax.broadcasted_iota(I32, (1, LANES), 1)
    tri_r = lax.broadcasted_iota(I32, (LANES, LANES), 0)
    tri_c = lax.broadcasted_iota(I32, (LANES, LANES), 1)
    upper = jnp.where(tri_r <= tri_c, 1.0, 0.0).astype(BF16)
    above = pltpu.bitcast(thr + 1, F32)
    floor = pltpu.bitcast(thr, F32)
    n_gt = jnp.sum(jnp.where(v >= above, 1, 0), axis=1, keepdims=True)
    need = (k - n_gt).astype(F32)
    seen_eq = jnp.zeros((n_exp, 1), F32)
    counts = jnp.zeros((n_exp, LANES), F32)
    for j in range(n // LANES):
        vb = v[:, j * LANES:(j + 1) * LANES]
        gt = vb >= above
        eq = jnp.logical_and(vb >= floor, vb < above)
        both = jnp.concatenate([jnp.where(gt, 1.0, 0.0), jnp.where(eq, 1.0, 0.0)], axis=0).astype(BF16)
        pref = _dot(both, upper)
        rank_eq = pref[n_exp:] + seen_eq
        inc = pref[:n_exp] + jnp.minimum(rank_eq, need) - jnp.minimum(seen_eq, need)
        sel = jnp.logical_or(gt, jnp.logical_and(eq, rank_eq <= need))
        seen_eq = rank_eq[:, LANES - 1:LANES]
        counts = counts + jnp.where(lane == j, inc[:, LANES - 1:LANES], 0.0)
        a0 = jnp.where(sel, vb, 0.0)
        t0 = a0.astype(BF16).astype(F32)
        t1 = (a0 - t0).astype(BF16).astype(F32)
        t2 = a0 - t0 - t1
        for t, val in enumerate((inc, t0, t1, t2)):
            tab_scr[stream * ROUTE_TABLES + t, pl.ds(j, n_exp, stride=LANES), :] = val
    through = _dot(counts.astype(BF16), upper)
    for e in range(n_exp):
        ce_scr[stream, e] = through[e:e + 1, :]
        cnt_scr[stream, e] = counts[e:e + 1, :]


def _route_slots(e, stream, k, lo, tab_scr, ce_scr, cnt_scr):
    lane_f = lax.broadcasted_iota(I32, (1, LANES), 1).astype(F32)
    slot = lax.broadcasted_iota(I32, (k, LANES), 0).astype(F32)
    ones_rows = jnp.ones((LANES, LANES), BF16)
    cnt_rows = jnp.broadcast_to(cnt_scr[stream, e], (LANES, LANES)).astype(BF16)
    rows_e = pl.ds(pl.multiple_of(e * LANES, LANES), LANES)
    tab = jnp.concatenate([tab_scr[stream * ROUTE_TABLES + t, rows_e, :] for t in range(ROUTE_TABLES)],
                          axis=1).astype(BF16)
    before = jnp.where(ce_scr[stream, e] <= slot, 1.0, 0.0).astype(BF16)
    blk = _dot_nt(before, ones_rows)
    base = _dot_nt(before, cnt_rows)
    yield
    row = _dot(jnp.where(lane_f == blk, 1.0, 0.0).astype(BF16), tab)
    yield
    inc = row[:, :LANES]
    aff = row[:, LANES:2 * LANES] + row[:, 2 * LANES:3 * LANES] + row[:, 3 * LANES:]
    local = slot - base
    pos = _dot_nt(jnp.where(inc <= local, 1.0, 0.0).astype(BF16), ones_rows)
    yield
    tok = blk * LANES + pos + float(lo)
    gate = jnp.sum(jnp.where(inc == local + 1.0, aff, 0.0), axis=1, keepdims=True)
    return tok, gate


def _route_kernel(n_exp, streams, aff_ref, idx_ref, gate_ref, tab_scr, ce_scr, cnt_scr):
    a = aff_ref[0]
    tab_scr[...] = jnp.zeros_like(tab_scr)
    vs = [a[:, lo:lo + n] for lo, n, _, _ in streams]

    def search(i, ts):
        out = []
        for t, v, (_, _, k, _) in zip(ts, vs, streams):
            cand = t | jnp.left_shift(jnp.int32(1), 30 - i)
            cnt = jnp.sum(jnp.where(v >= pltpu.bitcast(cand, F32), 1, 0), axis=1, keepdims=True)
            out.append(jnp.where(cnt >= k, cand, t))
        return tuple(out)

    thrs = lax.fori_loop(0, 31, search, tuple(jnp.zeros((n_exp, 1), I32) for _ in streams))
    for st, (v, (_, _, k, _)) in enumerate(zip(vs, streams)):
        _route_stream_tables(v, k, tab_scr, ce_scr, cnt_scr, st, thrs[st])

    slots = idx_ref.shape[3]

    assert [s[3] for s in streams] == [sum(s[2] for s in streams[:n]) for n in range(len(streams))]

    def expert_group(g, carry):
        experts = [g * ROUTE_GROUP + u for u in range(ROUTE_GROUP)]
        found = _round_robin([_route_slots(e, st, k, lo, tab_scr, ce_scr, cnt_scr)
                              for e in experts for st, (lo, _, k, _) in enumerate(streams)])
        for u, e in enumerate(experts):
            toks = []
            for st, (_, _, k, slot0) in enumerate(streams):
                tok, gate = found[u * len(streams) + st]
                toks.append(tok)
                gate_ref[0, pl.ds(e, 1), slot0:slot0 + k, :] = gate[None]
            toks.append(jnp.zeros((-slots % LANES, LANES), F32))
            row = jnp.concatenate(toks, axis=0).T[0:1, :slots]
            idx_ref[0, pl.ds(e, 1)] = row.astype(I32)[None]
        return carry

    assert n_exp % ROUTE_GROUP == 0
    lax.fori_loop(0, n_exp // ROUTE_GROUP, expert_group, 0)


def _route(aff, n_ctx):
    b, n_exp, tu = aff.shape
    n_lat = tu - n_ctx
    cap_l = CAPACITY_FACTOR * n_lat // n_exp
    cap_c = CAPACITY_FACTOR * n_ctx // n_exp
    slots = cap_l + cap_c
    assert n_lat // LANES <= LANES and n_ctx // LANES <= LANES
    streams = ((n_ctx, n_lat, cap_l, 0), (0, n_ctx, cap_c, cap_l))
    return pl.pallas_call(
        functools.partial(_route_kernel, n_exp, streams),
        grid=(b,),
        in_specs=[pl.BlockSpec((1, n_exp, tu), lambda i: (i, 0, 0))],
        out_specs=[pl.BlockSpec((1, n_exp, 1, slots), lambda i: (i, 0, 0, 0)),
                   pl.BlockSpec((1, n_exp, slots, 1), lambda i: (i, 0, 0, 0))],
        out_shape=[jax.ShapeDtypeStruct((b, n_exp, 1, slots), I32),
                   jax.ShapeDtypeStruct((b, n_exp, slots, 1), F32)],
        scratch_shapes=[pltpu.VMEM((len(streams) * ROUTE_TABLES, n_exp * LANES, LANES), F32),
                        pltpu.VMEM((len(streams), n_exp, 1, LANES), F32),
                        pltpu.VMEM((len(streams), n_exp, 1, LANES), F32)],
        compiler_params=_params("parallel"),
        name="route",
    )(aff)


def _slot_pitch(slots):
    return slots + SUBLANES


def _zero_slot_padding(ref, lead, slots, pitch):
    for q in range(SUBLANES):
        ref[lead + (pl.ds(q * pitch + slots, pitch - slots), slice(None))] = jnp.zeros((pitch - slots, LANES), ref.dtype)


BF16_ROWS = 2 * SUBLANES


def _slot_pitch_bf16(slots):
    return -(-slots // BF16_ROWS) * BF16_ROWS + BF16_ROWS


DISPATCH_EXPERTS = 4


def _gather_kernel(slots, pitch, pitch_out, idx_ref, h_ref, o_ref, tile_scr):
    for g in range(DISPATCH_EXPERTS):
        for mi in range(slots):
            r = pl.multiple_of(idx_ref[g, 0, mi] * SUBLANES, SUBLANES)
            tile_scr[g, pl.ds(mi, SUBLANES, stride=pitch), :] = h_ref[0, pl.ds(r, SUBLANES), :]
        _zero_slot_padding(o_ref, (0, g), slots, pitch_out)
        for q in range(SUBLANES):
            o_ref[0, g, q * pitch_out:q * pitch_out + slots, :] = (
                tile_scr[g, q * pitch:q * pitch + slots, :].astype(BF16))


def _gather(hg, idx_rows, n_exp, slots):
    b = hg.shape[0]
    pitch = _slot_pitch(slots)
    pitch_out = _slot_pitch_bf16(slots)
    ge = DISPATCH_EXPERTS
    assert n_exp % ge == 0
    return pl.pallas_call(
        functools.partial(_gather_kernel, slots, pitch, pitch_out),
        grid=(b, n_exp // ge),
        in_specs=[
            pl.BlockSpec((ge, 1, slots), lambda i, e: (i * (n_exp // ge) + e, 0, 0), memory_space=pltpu.SMEM),
            pl.BlockSpec((1,) + hg.shape[1:], lambda i, e: (i, 0, 0)),
        ],
        out_specs=pl.BlockSpec((1, ge, SUBLANES * pitch_out, LANES), lambda i, e: (i, e, 0, 0)),
        out_shape=jax.ShapeDtypeStruct((b, n_exp, SUBLANES * pitch_out, LANES), BF16),
        scratch_shapes=[pltpu.VMEM((ge, SUBLANES * pitch, LANES), F32)],
        compiler_params=_params("parallel", "arbitrary"),
        name="gather",
    )(idx_rows, hg)


FFN_SAMPLES = 2
FFN_VMEM_LIMIT_BYTES = 60 * 1024 * 1024


def _ffn_kernel(slots, pitch_in, pitch, xs_ref, wg_ref, wu_ref, wd_ref, gate_ref, y_ref):
    n = xs_ref.shape[0]
    x = jnp.concatenate(
        [jnp.concatenate([xs_ref[s, 0, q * pitch_in:q * pitch_in + slots, :] for q in range(SUBLANES)], axis=1)
         for s in range(n)], axis=0)
    gate = jnp.concatenate([gate_ref[s, 0] for s in range(n)], axis=0)
    a = _dot(x, wg_ref[0].astype(BF16))
    u = _dot(x, wu_ref[0].astype(BF16))
    y = _dot((_silu(a) * u).astype(BF16), wd_ref[0].astype(BF16)) * gate
    for s in range(n):
        _zero_slot_padding(y_ref, (s, 0), slots, pitch)
        for q in range(SUBLANES):
            y_ref[s, 0, q * pitch:q * pitch + slots, :] = y[s * slots:(s + 1) * slots, q * LANES:(q + 1) * LANES]


def _ffn(xs, w_gate, w_up, w_down, layer, gate, slots):
    b, n_exp = xs.shape[:2]
    pitch = _slot_pitch(slots)
    pitch_in = _slot_pitch_bf16(slots)
    d, f = w_gate.shape[2:]
    ns = FFN_SAMPLES
    assert b % ns == 0
    slot_tile = lambda p: pl.BlockSpec((ns, 1, SUBLANES * p, LANES), lambda e, i: (i, e, 0, 0))
    weight = lambda rows, cols: pl.BlockSpec((None, 1, rows, cols), lambda e, i: (layer, e, 0, 0))
    return pl.pallas_call(
        functools.partial(_ffn_kernel, slots, pitch_in, pitch),
        grid=(n_exp, b // ns),
        in_specs=[
            slot_tile(pitch_in),
            weight(d, f), weight(d, f), weight(f, d),
            pl.BlockSpec((ns, 1, slots, 1), lambda e, i: (i, e, 0, 0)),
        ],
        out_specs=slot_tile(pitch),
        out_shape=jax.ShapeDtypeStruct((b, n_exp, SUBLANES * pitch, LANES), F32),
        compiler_params=_params("parallel", "arbitrary", vmem=FFN_VMEM_LIMIT_BYTES),
        name="ffn",
    )(xs, w_gate, w_up, w_down, gate)


SCATTER_BATCH = 16


def _scatter_kernel(slots, pitch, idx_ref, y_ref, o_ref):
    @pl.when(pl.program_id(1) == 0)
    def _():
        o_ref[...] = jnp.zeros_like(o_ref)

    for g in range(DISPATCH_EXPERTS):
        for m0 in range(0, slots, SCATTER_BATCH):
            rows = [pl.multiple_of(idx_ref[g, 0, m0 + u] * SUBLANES, SUBLANES) for u in range(SCATTER_BATCH)]
            vals = [o_ref[0, pl.ds(rows[u], SUBLANES), :] + y_ref[0, g, pl.ds(m0 + u, SUBLANES, stride=pitch), :]
                    for u in range(SCATTER_BATCH)]
            for u in range(SCATTER_BATCH):
                o_ref[0, pl.ds(rows[u], SUBLANES), :] = vals[u]


def _scatter(y, idx_rows, tu, slots):
    b, n_exp = y.shape[:2]
    pitch = _slot_pitch(slots)
    ge = DISPATCH_EXPERTS
    assert n_exp % ge == 0 and slots % SCATTER_BATCH == 0
    return pl.pallas_call(
        functools.partial(_scatter_kernel, slots, pitch),
        grid=(b, n_exp // ge),
        in_specs=[
            pl.BlockSpec((ge, 1, slots), lambda i, e: (i * (n_exp // ge) + e, 0, 0), memory_space=pltpu.SMEM),
            pl.BlockSpec((1, ge, SUBLANES * pitch, LANES), lambda i, e: (i, e, 0, 0)),
        ],
        out_specs=pl.BlockSpec((1, tu * SUBLANES, LANES), lambda i, e: (i, 0, 0)),
        out_shape=jax.ShapeDtypeStruct((b, tu * SUBLANES, LANES), F32),
        compiler_params=_params("parallel", "arbitrary"),
        name="scatter",
    )(idx_rows, y)


def _final_kernel(x_ref, f_ref, mod_ref, g_ref, o_ref):
    tm = x_ref.shape[1]
    x = x_ref[0] + mod_ref[0][M_G2:M_G2 + 1, :] * _from_gather_layout(f_ref, tm)
    o_ref[0] = _rms(x) * g_ref[...]


def _final(x, ffn, mrows, final_g, n_ctx):
    b, tu, d = x.shape
    tm = FINAL_TILE
    skip = n_ctx // tm
    return pl.pallas_call(
        _final_kernel,
        grid=(b, (tu - n_ctx) // tm),
        in_specs=[
            pl.BlockSpec((1, tm, d), lambda i, j: (i, j + skip, 0)),
            pl.BlockSpec((1, tm * SUBLANES, LANES), lambda i, j: (i, j + skip, 0)),
            pl.BlockSpec((1, M_ROWS, d), lambda i, j: (i, 0, 0)),
            pl.BlockSpec((1, d), lambda i, j: (0, 0)),
        ],
        out_specs=pl.BlockSpec((1, tm, d), lambda i, j: (i, j, 0)),
        out_shape=jax.ShapeDtypeStruct((b, tu - n_ctx, d), F32),
        compiler_params=_params("parallel", "parallel"),
        name="final_norm",
    )(x, ffn, mrows, final_g.reshape(1, d))


def _rope_tables(n_lat, n_ctx):
    rows = n_lat // GRID_W
    rowp = jnp.repeat(jnp.arange(rows, dtype=F32), GRID_W)
    colp = jnp.tile(jnp.arange(GRID_W, dtype=F32), rows)
    axis_dim = HEAD_DIM // 2
    inv_freq = ROPE_BASE ** (-jnp.arange(0, axis_dim, 2, dtype=F32) / axis_dim)
    ar = rowp[:, None] * inv_freq
    ac = colp[:, None] * inv_freq
    cs = jnp.concatenate([jnp.cos(ar), jnp.cos(ar), jnp.cos(ac), jnp.cos(ac)], axis=1)
    sn = jnp.concatenate([-jnp.sin(ar), jnp.sin(ar), -jnp.sin(ac), jnp.sin(ac)], axis=1)
    reps = LANES // HEAD_DIM
    cs = jnp.concatenate([jnp.ones((n_ctx, LANES), F32), jnp.tile(cs, (1, reps))], axis=0)
    sn = jnp.concatenate([jnp.zeros((n_ctx, LANES), F32), jnp.tile(sn, (1, reps))], axis=0)
    return cs, sn


def kernel(x, c, ctx, c_ctx, w_mod, b_mod, norm1_g, norm2_g, w_in, conv_w, ret_decay_logit, attn_sink,
           w_out, w_router, w_gate, w_up, w_down, final_g):
    b, n_lat, d = x.shape
    n_ctx = ctx.shape[1]
    depth = w_in.shape[0]
    n_exp = w_router.shape[2]
    tu = n_ctx + n_lat
    assert w_in.shape[2] == IN_WIDTH and tu % INPROJ_TILE == 0
    assert n_ctx % FINAL_TILE == 0 and n_lat % FINAL_TILE == 0
    assert b + 1 <= SUBLANES and n_exp == N_EXPERTS

    c_rows = jnp.concatenate([c, c_ctx[None], jnp.zeros((SUBLANES - b - 1, d), F32)], axis=0)
    mods = _mod_vectors(c_rows, w_mod, b_mod).reshape(depth, SUBLANES, 6, d)
    qdec, win, cd, dm = _decay_tables(ret_decay_logit)
    rope_cs, rope_sn = _rope_tables(n_lat, n_ctx)
    sink_rows = jnp.broadcast_to(jnp.repeat(attn_sink.astype(F32), CHUNK, axis=1)[:, :, None],
                                 (depth, ATTN_HEADS * CHUNK, LANES))
    w_in_bf = w_in.astype(BF16)
    w_out_bf = w_out.astype(BF16)

    xu = jnp.concatenate([ctx, x], axis=1)
    ffn = None
    mrows = None
    cap = CAPACITY_FACTOR * n_lat // n_exp + CAPACITY_FACTOR * n_ctx // n_exp
    for l in range(depth):
        prev_mrows = mrows
        mrows = jnp.concatenate([mods[l, :b], jnp.broadcast_to(mods[l, b][None], (b, 6, d)),
                                 jnp.zeros((b, M_ROWS - 12, d), F32)], axis=1)
        xu, proj = _inproj(xu, ffn, mrows if ffn is None else _with_prev_g2(mrows, prev_mrows),
                           norm1_g[l], w_in_bf, l, rope_cs, rope_sn, n_ctx)
        sp = _ret_states(proj[1], win[l], cd[l], n_ctx)
        xu, hg, aff = _mixer(proj, sp, (dm[l], qdec[l]), conv_w[l], sink_rows[l],
                             xu, mrows, norm2_g[l], w_out_bf, l, w_router[l], n_ctx)
        idx, gate = _route(aff, n_ctx)
        idx_rows = idx.reshape(b * n_exp, 1, cap)
        xs = _gather(hg, idx_rows, n_exp, cap)
        y = _ffn(xs, w_gate, w_up, w_down, l, gate, cap)
        ffn = _scatter(y, idx_rows, tu, cap)
    return _final(xu, ffn, mrows, final_g, n_ctx)


def _with_prev_g2(mrows, prev_mrows):
    out = mrows.at[:, M_G2].set(prev_mrows[:, M_G2])
    return out.at[:, M_CTX + M_G2].set(prev_mrows[:, M_CTX + M_G2])
```

```python
import functools

import jax
import jax.numpy as jnp
from jax import lax
from jax.experimental import pallas as pl
from jax.experimental.pallas import tpu as pltpu

F32 = jnp.float32
BF16 = jnp.bfloat16
I32 = jnp.int32

HEAD_DIM = 64
CONV_WIDTH = 256
RET_HEADS = 4
RET_WIDTH = RET_HEADS * HEAD_DIM
ATTN_HEADS = 8
ATTN_KV_HEADS = 2
ATTN_GROUP = ATTN_HEADS // ATTN_KV_HEADS
ATTN_WIDTH = ATTN_HEADS * HEAD_DIM
KV_WIDTH = ATTN_KV_HEADS * HEAD_DIM
CHUNK = 128
GRID_W = 64
N_EXPERTS = 16
CAPACITY_FACTOR = 2
ROPE_BASE = 10000.0
EPS = 1e-6
NEG_INF = -1e30
QK_SCALE = HEAD_DIM ** -0.5
LOG2E = 1.4426950408889634

LANES = 128
SUBLANES = 8
VMEM_LIMIT_BYTES = 56 * 1024 * 1024

O_CB = 0
O_CC = O_CB + CONV_WIDTH
O_CX = O_CC + CONV_WIDTH
O_RQ = O_CX + CONV_WIDTH
O_RK = O_RQ + RET_WIDTH
O_RV = O_RK + RET_WIDTH
O_GF = O_RV + RET_WIDTH
O_GB = O_GF + RET_WIDTH
O_AQ = O_GB + RET_WIDTH
O_AK = O_AQ + ATTN_WIDTH
O_AV = O_AK + KV_WIDTH
IN_WIDTH = O_AV + KV_WIDTH

M_SH1, M_SC1, M_G1, M_SH2, M_SC2, M_G2 = range(6)
M_CTX = 6
M_ROWS = 16

INPROJ_TILE = 544
FINAL_TILE = 256


def _params(*sem, vmem=VMEM_LIMIT_BYTES):
    return pltpu.CompilerParams(dimension_semantics=sem, vmem_limit_bytes=vmem)


def _dot(a, b):
    return jnp.dot(a, b, preferred_element_type=F32)


def _dot_nt(a, b):
    return lax.dot_general(a, b, (((1,), (1,)), ((), ())), preferred_element_type=F32)


def _silu(v):
    return v * jax.nn.sigmoid(v)


def _mod_kernel(c_ref, w_ref, b_ref, o_ref):
    s = _silu(c_ref[...])
    s_hi = s.astype(BF16)
    s_lo = (s - s_hi.astype(F32)).astype(BF16)
    w = w_ref[0]
    w_hi = w.astype(BF16)
    w_lo = (w - w_hi.astype(F32)).astype(BF16)
    n = s.shape[0]
    head = _dot(jnp.concatenate([s_hi, s_lo], axis=0), w_hi)
    o_ref[0] = head[:n] + (head[n:] + _dot(s_hi, w_lo)) + b_ref[0]


def _mod_vectors(c_rows, w_mod, b_mod):
    depth, d_model, width = w_mod.shape
    tn = 1536
    return pl.pallas_call(
        _mod_kernel,
        grid=(depth, width // tn),
        in_specs=[
            pl.BlockSpec((SUBLANES, d_model), lambda l, n: (0, 0)),
            pl.BlockSpec((1, d_model, tn), lambda l, n: (l, 0, n)),
            pl.BlockSpec((1, 1, tn), lambda l, n: (l, 0, n)),
        ],
        out_specs=pl.BlockSpec((1, SUBLANES, tn), lambda l, n: (l, 0, n)),
        out_shape=jax.ShapeDtypeStruct((depth, SUBLANES, width), F32),
        compiler_params=_params("parallel", "parallel"),
        name="mod_vectors",
    )(c_rows, w_mod, b_mod.reshape(depth, 1, width))


def _log_sigmoid(v):
    return -jnp.log(1.0 + jnp.exp(-v))


def _tables_kernel(lgl_ref, lgr_ref, qdec_ref, win_ref, cd_ref, dm_ref):
    pos = lax.broadcasted_iota(I32, (CHUNK, RET_WIDTH), 0).astype(F32)
    ri = lax.broadcasted_iota(I32, (CHUNK, RET_HEADS * CHUNK), 0).astype(F32)
    rj = (lax.broadcasted_iota(I32, (CHUNK, RET_HEADS * CHUNK), 1) & (CHUNK - 1)).astype(F32)
    for d in range(2):
        lg = _log_sigmoid(lgl_ref[0, d])
        lg1 = lg[0:1, :]
        if d == 0:
            qdec_ref[0, d] = jnp.exp(lg1 * (pos + 1.0))
            win_ref[0, d] = jnp.exp(lg1 * (CHUNK - 1.0 - pos))
            diff = ri - rj
        else:
            qdec_ref[0, d] = jnp.exp(lg1 * (CHUNK - pos))
            win_ref[0, d] = jnp.exp(lg1 * pos)
            diff = rj - ri
        cd_ref[0, d] = jnp.exp(lg * float(CHUNK))
        lr = _log_sigmoid(lgr_ref[0, d])
        dm_ref[0, d] = jnp.where(diff >= 0.0, jnp.exp(lr * jnp.maximum(diff, 0.0)), 0.0)


def _decay_tables(ret_decay_logit):
    depth = ret_decay_logit.shape[0]
    lg = ret_decay_logit.astype(F32)
    lgl = jnp.broadcast_to(jnp.repeat(lg, HEAD_DIM, axis=-1)[:, :, None, :], (depth, 2, SUBLANES, RET_WIDTH))
    lgr = jnp.broadcast_to(jnp.repeat(lg, CHUNK, axis=-1)[:, :, None, :], (depth, 2, CHUNK, RET_HEADS * CHUNK))
    return pl.pallas_call(
        _tables_kernel,
        grid=(depth,),
        in_specs=[
            pl.BlockSpec((1, 2, SUBLANES, RET_WIDTH), lambda l: (l, 0, 0, 0)),
            pl.BlockSpec((1, 2, CHUNK, RET_HEADS * CHUNK), lambda l: (l, 0, 0, 0)),
        ],
        out_specs=[
            pl.BlockSpec((1, 2, CHUNK, RET_WIDTH), lambda l: (l, 0, 0, 0)),
            pl.BlockSpec((1, 2, CHUNK, RET_WIDTH), lambda l: (l, 0, 0, 0)),
            pl.BlockSpec((1, 2, SUBLANES, RET_WIDTH), lambda l: (l, 0, 0, 0)),
            pl.BlockSpec((1, 2, CHUNK, RET_HEADS * CHUNK), lambda l: (l, 0, 0, 0)),
        ],
        out_shape=[
            jax.ShapeDtypeStruct((depth, 2, CHUNK, RET_WIDTH), F32),
            jax.ShapeDtypeStruct((depth, 2, CHUNK, RET_WIDTH), F32),
            jax.ShapeDtypeStruct((depth, 2, SUBLANES, RET_WIDTH), F32),
            jax.ShapeDtypeStruct((depth, 2, CHUNK, RET_HEADS * CHUNK), F32),
        ],
        compiler_params=_params("parallel"),
        name="decay_tables",
    )(lgl, lgr)


def _rope(v, cs, sn):
    lane = lax.broadcasted_iota(I32, (1, LANES), 1)
    first = (lane & 31) < 16
    outs = []
    for g in range(v.shape[1] // LANES):
        vg = v[:, g * LANES:(g + 1) * LANES]
        sw = jnp.where(first, pltpu.roll(vg, LANES - 16, 1), pltpu.roll(vg, 16, 1))
        outs.append(vg * cs + sw * sn)
    return outs[0] if len(outs) == 1 else jnp.concatenate(outs, axis=1)


def _swap_halves(v):
    return jnp.concatenate([v[:, HEAD_DIM:], v[:, :HEAD_DIM]], axis=1)


def _from_gather_layout(f_ref, tm):
    return jnp.concatenate(
        [f_ref[0, pl.ds(q, tm, stride=SUBLANES), :] for q in range(SUBLANES)], axis=1)


def _rms(v):
    return v * lax.rsqrt(jnp.mean(v * v, axis=-1, keepdims=True) + EPS)


def _mod_rows(mod_ref, tm, n_ctx, tile):
    m = mod_ref[0]
    if n_ctx % tm == 0:
        is_ctx = tile < n_ctx // tm
    else:
        is_ctx = tile * tm + lax.broadcasted_iota(I32, (tm, 1), 0) < n_ctx

    def row(r):
        return jnp.where(is_ctx, m[M_CTX + r:M_CTX + r + 1, :], m[r:r + 1, :])

    return row


def _inproj_kernel(has_ffn, n_ctx, *refs):
    if has_ffn:
        x_ref, f_ref, mod_ref, g_ref, w_ref, cs_ref, sn_ref = refs[:7]
        outs = refs[7:]
        xo_ref = outs[0]
        outs = outs[1:]
    else:
        x_ref, mod_ref, g_ref, w_ref, cs_ref, sn_ref = refs[:6]
        outs = refs[6:]
    cz_ref, rqkv_ref, gates_ref, aq_ref, kv_ref = outs
    tm = x_ref.shape[1]
    row = _mod_rows(mod_ref, tm, n_ctx, pl.program_id(1))

    x = x_ref[0]
    if has_ffn:
        x = x + row(M_G2) * _from_gather_layout(f_ref, tm)
        xo_ref[0] = x
    h = (_rms(x) * g_ref[...]) * (1.0 + row(M_SC1)) + row(M_SH1)
    h = h.astype(BF16)
    cs = cs_ref[...]
    sn = sn_ref[...]

    def proj(a, b):
        return _dot(h, w_ref[:, a:b])

    bounds = (O_CB, O_CC, O_CX, O_RQ, O_RK, O_RV, O_GF, O_GB, O_AQ, O_AK, O_AV, IN_WIDTH)
    p = [proj(a, b) for a, b in zip(bounds[:-1], bounds[1:])]
    cz_ref[0, :, 0:CONV_WIDTH] = p[0]
    cz_ref[0, :, CONV_WIDTH:] = p[1] * p[2]
    rqkv_ref[0, :, 0:RET_WIDTH] = _rope(p[3], cs, sn).astype(BF16)
    rqkv_ref[0, :, RET_WIDTH:2 * RET_WIDTH] = _rope(p[4] * QK_SCALE, cs, sn).astype(BF16)
    rqkv_ref[0, :, 2 * RET_WIDTH:] = p[5].astype(BF16)
    gates_ref[0, :, 0:RET_WIDTH] = _silu(p[6])
    gates_ref[0, :, RET_WIDTH:] = _silu(p[7])
    aq_ref[0] = (_rope(p[8], cs, sn) * (QK_SCALE * LOG2E)).astype(BF16)
    ak = _rope(p[9], cs, sn)
    av = p[10]
    kv_ref[0, :, 0:KV_WIDTH] = ak.astype(BF16)
    kv_ref[0, :, KV_WIDTH:2 * KV_WIDTH] = _swap_halves(ak).astype(BF16)
    kv_ref[0, :, 2 * KV_WIDTH:] = av.astype(BF16)


def _inproj(x, ffn, mrows, norm_g, w_in_bf, layer, rope_cs, rope_sn, n_ctx):
    b, tu, d = x.shape
    tm = INPROJ_TILE
    has_ffn = ffn is not None
    tok = lambda width: pl.BlockSpec((1, tm, width), lambda i, j: (i, j, 0))
    in_specs = [tok(d)]
    args = [x]
    if has_ffn:
        in_specs.append(pl.BlockSpec((1, tm * SUBLANES, LANES), lambda i, j: (i, j, 0)))
        args.append(ffn)
    in_specs += [
        pl.BlockSpec((1, M_ROWS, d), lambda i, j: (i, 0, 0)),
        pl.BlockSpec((1, d), lambda i, j: (0, 0)),
        pl.BlockSpec((None, d, IN_WIDTH), lambda i, j: (layer, 0, 0)),
        pl.BlockSpec((tm, LANES), lambda i, j: (j, 0)),
        pl.BlockSpec((tm, LANES), lambda i, j: (j, 0)),
    ]
    args += [mrows, norm_g.reshape(1, d), w_in_bf, rope_cs, rope_sn]
    widths = [(2 * CONV_WIDTH, F32), (3 * RET_WIDTH, BF16), (2 * RET_WIDTH, F32), (ATTN_WIDTH, BF16),
              (3 * KV_WIDTH, BF16)]
    out_specs = [tok(w) for w, _ in widths]
    out_shape = [jax.ShapeDtypeStruct((b, tu, w), dt) for w, dt in widths]
    if has_ffn:
        out_specs = [tok(d)] + out_specs
        out_shape = [jax.ShapeDtypeStruct((b, tu, d), F32)] + out_shape
    res = pl.pallas_call(
        functools.partial(_inproj_kernel, has_ffn, n_ctx),
        grid=(b, tu // tm),
        in_specs=in_specs,
        out_specs=out_specs,
        out_shape=out_shape,
        compiler_params=_params("parallel", "parallel"),
        name="inproj",
    )(*args)
    if has_ffn:
        return res[0], res[1:]
    return x, res


def _head_block_mask(n):
    r = lax.broadcasted_iota(I32, (n, n), 0) // HEAD_DIM
    c = lax.broadcasted_iota(I32, (n, n), 1) // HEAD_DIM
    return r == c


STATE_UNROLL = 2


def _states_kernel(n_ctx_chunks, n_chunks, rqkv_ref, win_ref, cd_ref, sp_ref, s_scr):
    same_head = _head_block_mask(RET_WIDTH)
    s_scr[...] = jnp.zeros_like(s_scr)

    def chunk_update(d, pos):
        if d == 0:
            c = pos
        else:
            c = jnp.where(pos < n_ctx_chunks, n_ctx_chunks - 1 - pos, n_chunks + n_ctx_chunks - 1 - pos)
        off = pl.multiple_of(c * CHUNK, CHUNK)
        kw = rqkv_ref[0, pl.ds(off, CHUNK), RET_WIDTH:2 * RET_WIDTH].astype(F32) * win_ref[d]
        v = rqkv_ref[0, pl.ds(off, CHUNK), 2 * RET_WIDTH:]
        kw_t = kw.T.astype(BF16)
        yield
        u = _dot(kw_t, v)
        yield
        return c, jnp.where(same_head, u, 0.0)

    def body(i, carry):
        todo = [(d, i * STATE_UNROLL + n) for n in range(STATE_UNROLL) for d in range(2)]
        done = _round_robin([chunk_update(d, pos) for d, pos in todo])
        for (d, _), (c, u) in zip(todo, done):
            s = s_scr[d]
            sp_ref[0, d, pl.ds(c, 1)] = s.astype(BF16)[None]
            s_scr[d] = s * cd_ref[d][0:1, :] + u
        return carry

    assert n_chunks % STATE_UNROLL == 0
    lax.fori_loop(0, n_chunks // STATE_UNROLL, body, 0)


def _ret_states(rqkv, win, cd, n_ctx):
    b, tu, _ = rqkv.shape
    w = RET_WIDTH
    n_chunks = tu // CHUNK
    return pl.pallas_call(
        functools.partial(_states_kernel, n_ctx // CHUNK, n_chunks),
        grid=(b,),
        in_specs=[
            pl.BlockSpec((1, tu, 3 * w), lambda i: (i, 0, 0)),
            pl.BlockSpec((2, CHUNK, w), lambda i: (0, 0, 0)),
            pl.BlockSpec((2, SUBLANES, w), lambda i: (0, 0, 0)),
        ],
        out_specs=pl.BlockSpec((1, 2, n_chunks, w, w), lambda i: (i, 0, 0, 0, 0)),
        out_shape=jax.ShapeDtypeStruct((b, 2, n_chunks, w, w), BF16),
        scratch_shapes=[pltpu.VMEM((2, w, w), F32)],
        compiler_params=_params("parallel"),
        name="ret_states",
    )(rqkv, win, cd)


def _group_mean(t, ones_bd):
    hi = t.astype(BF16)
    lo = (t - hi.astype(F32)).astype(BF16)
    s = _dot(jnp.concatenate([hi, lo], axis=0), ones_bd)
    n = t.shape[0]
    return (s[:n] + s[n:]) * (1.0 / HEAD_DIM)


def _mix_chunk(c, n_ctx_chunks, n_chunks, cz, z_last, z_first, cw, rqkv, gates, sp, dm_ref, qdec_ref,
               qa, kvs, sink_ref):
    is_lat = c >= n_ctx_chunks

    z = cz[:, CONV_WIDTH:]
    row = lax.broadcasted_iota(I32, (CHUNK, 1), 0)
    has_prev = jnp.logical_and(c != 0, c != n_ctx_chunks)
    has_next = jnp.logical_and(c != n_ctx_chunks - 1, c != n_chunks - 1)
    z_before = jnp.where(row == 0, jnp.where(has_prev, z_last, 0.0), pltpu.roll(z, 1, 0))
    z_after = jnp.where(row == CHUNK - 1, jnp.where(has_next, z_first, 0.0), pltpu.roll(z, CHUNK - 1, 0))
    conv = cz[:, :CONV_WIDTH] * (z_before * cw[0:1, :] + z * cw[1:2, :] + z_after * cw[2:3, :])

    q = rqkv[:, 0:RET_WIDTH]
    k = rqkv[:, RET_WIDTH:2 * RET_WIDTH]
    v = rqkv[:, 2 * RET_WIDTH:]
    lane_head = lax.broadcasted_iota(I32, (1, RET_WIDTH), 1) // HEAD_DIM
    kz = jnp.zeros_like(k)
    k_heads = jnp.concatenate([jnp.where(lane_head == hh, k, kz) for hh in range(RET_HEADS)], axis=0)
    v_heads = jnp.concatenate([jnp.where(lane_head == hh, v, kz) for hh in range(RET_HEADS)], axis=0)
    scores = _dot_nt(q, k_heads)

    keys, v_t = kvs
    half = lax.broadcasted_iota(I32, (1, LANES), 1) // HEAD_DIM
    qa_z = jnp.zeros((CHUNK, LANES), BF16)
    combos = [(hk, par) for hk in range(ATTN_KV_HEADS) for par in range(2)]
    n_ctx = keys[0].shape[0] - 3 * CHUNK
    cols2 = 2 * CHUNK
    ik = lax.broadcasted_iota(I32, (CHUNK, cols2), 0)
    iq = lax.broadcasted_iota(I32, (CHUNK, cols2), 1) & (CHUNK - 1)
    off = jnp.full((CHUNK, cols2), NEG_INF, F32)
    bias_prev = jnp.where(jnp.logical_and(jnp.logical_and(is_lat, c - 1 >= n_ctx_chunks), ik >= iq), 0.0, off)
    bias_cur = jnp.where(is_lat, 0.0, off)
    bias_next = jnp.where(jnp.logical_and(jnp.logical_and(is_lat, c + 1 <= n_chunks - 1), ik <= iq), 0.0, off)
    assert n_ctx > 0
    logits = {}
    for hk, par in combos:
        ja, jb = 2 * hk, 2 * hk + 1
        qst = jnp.concatenate([
            jnp.where(half == par, qa[:, ja * LANES:(ja + 1) * LANES], qa_z),
            jnp.where(half == par, qa[:, jb * LANES:(jb + 1) * LANES], qa_z)], axis=0)
        sel = 0 if par == hk else 1
        r = _dot_nt(keys[sel], qst)
        s = jnp.concatenate([r[:CHUNK] + bias_prev, r[CHUNK:2 * CHUNK] + bias_cur,
                             r[2 * CHUNK:3 * CHUNK] + bias_next, r[3 * CHUNK:]], axis=0)
        ha, hb = ATTN_GROUP * hk + par, ATTN_GROUP * hk + par + 2
        snk = jnp.concatenate([sink_ref[ha * CHUNK:ha * CHUNK + 1, :],
                               sink_ref[hb * CHUNK:hb * CHUNK + 1, :]], axis=1) * LOG2E
        logits[hk, par] = (s, snk, jnp.maximum(jnp.max(s, axis=0, keepdims=True), snk))
    yield

    qf = q.astype(F32)
    outs = []
    for d in range(2):
        p = (scores * dm_ref[d]).astype(BF16)
        lhs = jnp.concatenate([p, (qf * qdec_ref[d]).astype(BF16)], axis=1)
        o = _dot(lhs, jnp.concatenate([v_heads, sp[d]], axis=0))
        outs.append(o)
    o2 = jnp.concatenate(outs, axis=0)
    yield

    att = {}
    for n, (hk, par) in enumerate(combos):
        s, snk, mx = logits[hk, par]
        e = jnp.exp2(s - mx)
        den = jnp.sum(e, axis=0, keepdims=True) + jnp.exp2(snk - mx)
        o_t = _dot(v_t[hk * HEAD_DIM:(hk + 1) * HEAD_DIM, :], e.astype(BF16))
        att[hk, par] = o_t * (1.0 / den)
        if n == 0:
            ones_bd = jnp.where(_head_block_mask(RET_WIDTH), 1.0, 0.0).astype(BF16)
            dl = o2 - _group_mean(o2, ones_bd)
            on = dl * lax.rsqrt(_group_mean(dl * dl, ones_bd) + EPS)
            ret = on[:CHUNK] * gates[:, 0:RET_WIDTH] + on[CHUNK:] * gates[:, RET_WIDTH:]
        yield

    cols = []
    for hk in range(ATTN_KV_HEADS):
        cols.append(jnp.concatenate([att[hk, 0][:, :CHUNK], att[hk, 1][:, :CHUNK]], axis=0).T)
        cols.append(jnp.concatenate([att[hk, 0][:, CHUNK:], att[hk, 1][:, CHUNK:]], axis=0).T)
    return jnp.concatenate([conv, ret] + cols, axis=1)


def _round_robin(gens):
    results = [None] * len(gens)
    active = list(range(len(gens)))
    while active:
        for i in list(active):
            try:
                next(gens[i])
            except StopIteration as done:
                results[i] = done.value
                active.remove(i)
    return results


MIX_CHUNKS = 2


def _mixer_kernel(n_ctx_chunks, n_chunks, n_exp,
                  cz_ref, czp_ref, czn_ref, cw_ref, rqkv_ref, gates_ref, sp_ref, dm_ref, qdec_ref,
                  aq_ref, kvp_ref, kvc_ref, kvn_ref, kvx_ref, sink_ref,
                  x_ref, mod_ref, g_ref, w_ref, wrh_ref, wrl_ref, xo_ref, hg_ref, aff_ref, mix_scr):
    t = pl.program_id(1)
    n_steps = pl.num_programs(1) - 1

    @pl.when(t == 0)
    def _():
        mix_scr[...] = jnp.zeros_like(mix_scr)

    tail = _outproj_tail(n_ctx_chunks * CHUNK, n_exp, jnp.maximum(t - 1, 0), mix_scr[...], x_ref, mod_ref, g_ref,
                         w_ref, wrh_ref, wrl_ref, xo_ref, hg_ref, aff_ref)
    j = jnp.minimum(t, n_steps - 1)
    cw = cw_ref[...]

    def values_t(blk):
        return blk[:, 2 * KV_WIDTH:].astype(F32).T.astype(BF16)

    kv_blocks = ([kvp_ref[0]] + [kvc_ref[0, s * CHUNK:(s + 1) * CHUNK, :] for s in range(MIX_CHUNKS)] + [kvn_ref[0]])
    kv_ctx = [kvx_ref[0, t * CHUNK:(t + 1) * CHUNK, :] for t in range(kvx_ref.shape[1] // CHUNK)]
    vt_blocks = [values_t(blk) for blk in kv_blocks]
    vt_ctx = [values_t(blk) for blk in kv_ctx]
    gens = []
    for s in range(MIX_CHUNKS):
        lo, hi = s * CHUNK, (s + 1) * CHUNK
        if s == 0:
            z_last = czp_ref[0, SUBLANES - 1:SUBLANES, CONV_WIDTH:]
        else:
            z_last = cz_ref[0, lo - 1:lo, CONV_WIDTH:]
        if s == MIX_CHUNKS - 1:
            z_first = czn_ref[0, 0:1, CONV_WIDTH:]
        else:
            z_first = cz_ref[0, hi:hi + 1, CONV_WIDTH:]
        window = kv_blocks[s:s + 3] + kv_ctx
        keys = tuple(jnp.concatenate([blk[:, i * KV_WIDTH:(i + 1) * KV_WIDTH] for blk in window], axis=0)
                     for i in (0, 1))
        v_t = jnp.concatenate(vt_blocks[s:s + 3] + vt_ctx, axis=1)
        gens.append(_mix_chunk(j * MIX_CHUNKS + s, n_ctx_chunks, n_chunks, cz_ref[0, lo:hi, :], z_last, z_first,
                               cw, rqkv_ref[0, lo:hi, :], gates_ref[0, lo:hi, :],
                               (sp_ref[0, 0, s], sp_ref[0, 1, s]), dm_ref, qdec_ref, aq_ref[0, lo:hi, :],
                               (keys, v_t), sink_ref))
    mixes = [mix.astype(BF16) for mix in _round_robin([tail] + gens)[1:]]
    mix_scr[...] = jnp.concatenate(mixes, axis=0)


def _mixer(proj, sp, tabs, conv_w, sink_rows, x, mrows, norm_g, w_out_bf, layer, w_router, n_ctx):
    cz, rqkv, gates, aq, kv = proj
    dm, qdec = tabs
    b, tu, _ = cz.shape
    d = x.shape[2]
    n_exp = w_router.shape[1]
    wr = w_router.T
    wr_hi = wr.astype(BF16)
    wr_lo = (wr - wr_hi.astype(F32)).astype(BF16)
    n_chunks = tu // CHUNK
    mc = MIX_CHUNKS
    rows = mc * CHUNK
    assert n_chunks % mc == 0 and (n_ctx // CHUNK) % mc == 0
    last = n_chunks - 1
    n_steps = n_chunks // mc
    per8 = rows // SUBLANES
    mixed = lambda t: jnp.minimum(t, n_steps - 1)
    projected = lambda t: jnp.maximum(t - 1, 0)
    cur = lambda w: pl.BlockSpec((1, rows, w), lambda i, t: (i, mixed(t), 0))
    out = lambda r, w: pl.BlockSpec((1, r, w), lambda i, t: (i, projected(t), 0))
    full = lambda shape: pl.BlockSpec(shape, lambda i, t: (0,) * len(shape))
    wcz, wkv = cz.shape[2], kv.shape[2]
    in_specs = [
        cur(wcz),
        pl.BlockSpec((1, SUBLANES, wcz), lambda i, t: (i, jnp.maximum(mixed(t) * per8 - 1, 0), 0)),
        pl.BlockSpec((1, SUBLANES, wcz),
                     lambda i, t: (i, jnp.minimum((mixed(t) + 1) * per8, tu // SUBLANES - 1), 0)),
        full(conv_w.shape),
        cur(rqkv.shape[2]), cur(gates.shape[2]),
        pl.BlockSpec((1, 2, mc, RET_WIDTH, RET_WIDTH), lambda i, t: (i, 0, mixed(t), 0, 0)),
        full(dm.shape), full(qdec.shape),
        cur(ATTN_WIDTH),
        pl.BlockSpec((1, CHUNK, wkv), lambda i, t: (i, jnp.maximum(mixed(t) * mc - 1, 0), 0)),
        cur(wkv),
        pl.BlockSpec((1, CHUNK, wkv), lambda i, t: (i, jnp.minimum((mixed(t) + 1) * mc, last), 0)),
        pl.BlockSpec((1, n_ctx, wkv), lambda i, t: (i, 0, 0)),
        full(sink_rows.shape),
        out(rows, d),
        pl.BlockSpec((1, M_ROWS, d), lambda i, t: (i, 0, 0)),
        full((1, d)),
        pl.BlockSpec((None,) + w_out_bf.shape[1:], lambda i, t: (layer, 0, 0)),
        full((n_exp, d)), full((n_exp, d)),
    ]
    args = [cz, cz, cz, conv_w, rqkv, gates, sp, dm, qdec, aq, kv, kv, kv, kv, sink_rows,
            x, mrows, norm_g.reshape(1, d), w_out_bf, wr_hi, wr_lo]
    return pl.pallas_call(
        functools.partial(_mixer_kernel, n_ctx // CHUNK, n_chunks, n_exp),
        grid=(b, n_steps + 1),
        in_specs=in_specs,
        out_specs=[out(rows, d), out(rows * SUBLANES, LANES),
                   pl.BlockSpec((1, n_exp, rows), lambda i, t: (i, 0, projected(t)))],
        out_shape=[
            jax.ShapeDtypeStruct((b, tu, d), F32),
            jax.ShapeDtypeStruct((b, tu * SUBLANES, LANES), F32),
            jax.ShapeDtypeStruct((b, n_exp, tu), F32),
        ],
        scratch_shapes=[pltpu.VMEM((rows, CONV_WIDTH + RET_WIDTH + ATTN_WIDTH), BF16)],
        compiler_params=_params("parallel", "arbitrary"),
        name="mixer",
    )(*args)


def _outproj_tail(n_ctx, n_exp, tile, mix, x_ref, mod_ref, g_ref, w_ref, wrh_ref, wrl_ref, xo_ref, hg_ref, aff_ref):
    tm = x_ref.shape[1]
    row = _mod_rows(mod_ref, tm, n_ctx, tile)

    y = _dot(mix, w_ref[...])
    yield
    x = x_ref[0] + row(M_G1) * y
    xo_ref[0] = x
    h = (_rms(x) * g_ref[...]) * (1.0 + row(M_SC2)) + row(M_SH2)
    for q in range(SUBLANES):
        hg_ref[0, pl.ds(q, tm, stride=SUBLANES), :] = h[:, q * LANES:(q + 1) * LANES]
    yield
    h_hi = h.astype(BF16)
    h_lo = (h - h_hi.astype(F32)).astype(BF16)
    logits = (_dot_nt(wrh_ref[...], h_hi) + (_dot_nt(wrh_ref[...], h_lo) + _dot_nt(wrl_ref[...], h_hi)))
    yield
    e = jnp.exp(logits - jnp.max(logits, axis=0, keepdims=True))
    aff_ref[0] = e / jnp.sum(e, axis=0, keepdims=True)


ROUTE_TABLES = 4
ROUTE_GROUP = 4


def _route_stream_tables(v, k, tab_scr, ce_scr, cnt_scr, stream, thr):
    n_exp, n = v.shape
    lane = lax.broadcasted_iota(I32, (1, LANES), 1)
    tri_r = lax.broadcasted_iota(I32, (LANES, LANES), 0)
    tri_c = lax.broadcasted_iota(I32, (LANES, LANES), 1)
    upper = jnp.where(tri_r <= tri_c, 1.0, 0.0).astype(BF16)
    above = pltpu.bitcast(thr + 1, F32)
    floor = pltpu.bitcast(thr, F32)
    n_gt = jnp.sum(jnp.where(v >= above, 1, 0), axis=1, keepdims=True)
    need = (k - n_gt).astype(F32)
    seen_eq = jnp.zeros((n_exp, 1), F32)
    counts = jnp.zeros((n_exp, LANES), F32)
    for j in range(n // LANES):
        vb = v[:, j * LANES:(j + 1) * LANES]
        gt = vb >= above
        eq = jnp.logical_and(vb >= floor, vb < above)
        both = jnp.concatenate([jnp.where(gt, 1.0, 0.0), jnp.where(eq, 1.0, 0.0)], axis=0).astype(BF16)
        pref = _dot(both, upper)
        rank_eq = pref[n_exp:] + seen_eq
        inc = pref[:n_exp] + jnp.minimum(rank_eq, need) - jnp.minimum(seen_eq, need)
        sel = jnp.logical_or(gt, jnp.logical_and(eq, rank_eq <= need))
        seen_eq = rank_eq[:, LANES - 1:LANES]
        counts = counts + jnp.where(lane == j, inc[:, LANES - 1:LANES], 0.0)
        a0 = jnp.where(sel, vb, 0.0)
        t0 = a0.astype(BF16).astype(F32)
        t1 = (a0 - t0).astype(BF16).astype(F32)
        t2 = a0 - t0 - t1
        for t, val in enumerate((inc, t0, t1, t2)):
            tab_scr[stream * ROUTE_TABLES + t, pl.ds(j, n_exp, stride=LANES), :] = val
    through = _dot(counts.astype(BF16), upper)
    for e in range(n_exp):
        ce_scr[stream, e] = through[e:e + 1, :]
        cnt_scr[stream, e] = counts[e:e + 1, :]


def _route_slots(e, stream, k, lo, tab_scr, ce_scr, cnt_scr):
    lane_f = lax.broadcasted_iota(I32, (1, LANES), 1).astype(F32)
    slot = lax.broadcasted_iota(I32, (k, LANES), 0).astype(F32)
    ones_rows = jnp.ones((LANES, LANES), BF16)
    cnt_rows = jnp.broadcast_to(cnt_scr[stream, e], (LANES, LANES)).astype(BF16)
    rows_e = pl.ds(pl.multiple_of(e * LANES, LANES), LANES)
    tab = jnp.concatenate([tab_scr[stream * ROUTE_TABLES + t, rows_e, :] for t in range(ROUTE_TABLES)],
                          axis=1).astype(BF16)
    before = jnp.where(ce_scr[stream, e] <= slot, 1.0, 0.0).astype(BF16)
    blk = _dot_nt(before, ones_rows)
    base = _dot_nt(before, cnt_rows)
    yield
    row = _dot(jnp.where(lane_f == blk, 1.0, 0.0).astype(BF16), tab)
    yield
    inc = row[:, :LANES]
    aff = row[:, LANES:2 * LANES] + row[:, 2 * LANES:3 * LANES] + row[:, 3 * LANES:]
    local = slot - base
    pos = _dot_nt(jnp.where(inc <= local, 1.0, 0.0).astype(BF16), ones_rows)
    yield
    tok = blk * LANES + pos + float(lo)
    gate = jnp.sum(jnp.where(inc == local + 1.0, aff, 0.0), axis=1, keepdims=True)
    return tok, gate


def _route_kernel(n_exp, streams, aff_ref, idx_ref, gate_ref, tab_scr, ce_scr, cnt_scr):
    a = aff_ref[0]
    tab_scr[...] = jnp.zeros_like(tab_scr)
    vs = [a[:, lo:lo + n] for lo, n, _, _ in streams]

    def search(i, ts):
        out = []
        for t, v, (_, _, k, _) in zip(ts, vs, streams):
            cand = t | jnp.left_shift(jnp.int32(1), 30 - i)
            cnt = jnp.sum(jnp.where(v >= pltpu.bitcast(cand, F32), 1, 0), axis=1, keepdims=True)
            out.append(jnp.where(cnt >= k, cand, t))
        return tuple(out)

    thrs = lax.fori_loop(0, 31, search, tuple(jnp.zeros((n_exp, 1), I32) for _ in streams))
    for st, (v, (_, _, k, _)) in enumerate(zip(vs, streams)):
        _route_stream_tables(v, k, tab_scr, ce_scr, cnt_scr, st, thrs[st])

    slots = idx_ref.shape[3]

    assert [s[3] for s in streams] == [sum(s[2] for s in streams[:n]) for n in range(len(streams))]

    def expert_group(g, carry):
        experts = [g * ROUTE_GROUP + u for u in range(ROUTE_GROUP)]
        found = _round_robin([_route_slots(e, st, k, lo, tab_scr, ce_scr, cnt_scr)
                              for e in experts for st, (lo, _, k, _) in enumerate(streams)])
        for u, e in enumerate(experts):
            toks = []
            for st, (_, _, k, slot0) in enumerate(streams):
                tok, gate = found[u * len(streams) + st]
                toks.append(tok)
                gate_ref[0, pl.ds(e, 1), slot0:slot0 + k, :] = gate[None]
            toks.append(jnp.zeros((-slots % LANES, LANES), F32))
            row = jnp.concatenate(toks, axis=0).T[0:1, :slots]
            idx_ref[0, pl.ds(e, 1)] = row.astype(I32)[None]
        return carry

    assert n_exp % ROUTE_GROUP == 0
    lax.fori_loop(0, n_exp // ROUTE_GROUP, expert_group, 0)


def _route(aff, n_ctx):
    b, n_exp, tu = aff.shape
    n_lat = tu - n_ctx
    cap_l = CAPACITY_FACTOR * n_lat // n_exp
    cap_c = CAPACITY_FACTOR * n_ctx // n_exp
    slots = cap_l + cap_c
    assert n_lat // LANES <= LANES and n_ctx // LANES <= LANES
    streams = ((n_ctx, n_lat, cap_l, 0), (0, n_ctx, cap_c, cap_l))
    return pl.pallas_call(
        functools.partial(_route_kernel, n_exp, streams),
        grid=(b,),
        in_specs=[pl.BlockSpec((1, n_exp, tu), lambda i: (i, 0, 0))],
        out_specs=[pl.BlockSpec((1, n_exp, 1, slots), lambda i: (i, 0, 0, 0)),
                   pl.BlockSpec((1, n_exp, slots, 1), lambda i: (i, 0, 0, 0))],
        out_shape=[jax.ShapeDtypeStruct((b, n_exp, 1, slots), I32),
                   jax.ShapeDtypeStruct((b, n_exp, slots, 1), F32)],
        scratch_shapes=[pltpu.VMEM((len(streams) * ROUTE_TABLES, n_exp * LANES, LANES), F32),
                        pltpu.VMEM((len(streams), n_exp, 1, LANES), F32),
                        pltpu.VMEM((len(streams), n_exp, 1, LANES), F32)],
        compiler_params=_params("parallel"),
        name="route",
    )(aff)


def _slot_pitch(slots):
    return slots + SUBLANES


def _zero_slot_padding(ref, lead, slots, pitch):
    for q in range(SUBLANES):
        ref[lead + (pl.ds(q * pitch + slots, pitch - slots), slice(None))] = jnp.zeros((pitch - slots, LANES), ref.dtype)


BF16_ROWS = 2 * SUBLANES


def _slot_pitch_bf16(slots):
    return -(-slots // BF16_ROWS) * BF16_ROWS + BF16_ROWS


DISPATCH_EXPERTS = 4


def _gather_kernel(slots, pitch, pitch_out, idx_ref, h_ref, o_ref, tile_scr):
    for g in range(DISPATCH_EXPERTS):
        for mi in range(slots):
            r = pl.multiple_of(idx_ref[g, 0, mi] * SUBLANES, SUBLANES)
            tile_scr[g, pl.ds(mi, SUBLANES, stride=pitch), :] = h_ref[0, pl.ds(r, SUBLANES), :]
        _zero_slot_padding(o_ref, (0, g), slots, pitch_out)
        for q in range(SUBLANES):
            o_ref[0, g, q * pitch_out:q * pitch_out + slots, :] = (
                tile_scr[g, q * pitch:q * pitch + slots, :].astype(BF16))


def _gather(hg, idx_rows, n_exp, slots):
    b = hg.shape[0]
    pitch = _slot_pitch(slots)
    pitch_out = _slot_pitch_bf16(slots)
    ge = DISPATCH_EXPERTS
    assert n_exp % ge == 0
    return pl.pallas_call(
        functools.partial(_gather_kernel, slots, pitch, pitch_out),
        grid=(b, n_exp // ge),
        in_specs=[
            pl.BlockSpec((ge, 1, slots), lambda i, e: (i * (n_exp // ge) + e, 0, 0), memory_space=pltpu.SMEM),
            pl.BlockSpec((1,) + hg.shape[1:], lambda i, e: (i, 0, 0)),
        ],
        out_specs=pl.BlockSpec((1, ge, SUBLANES * pitch_out, LANES), lambda i, e: (i, e, 0, 0)),
        out_shape=jax.ShapeDtypeStruct((b, n_exp, SUBLANES * pitch_out, LANES), BF16),
        scratch_shapes=[pltpu.VMEM((ge, SUBLANES * pitch, LANES), F32)],
        compiler_params=_params("parallel", "arbitrary"),
        name="gather",
    )(idx_rows, hg)


FFN_SAMPLES = 2
FFN_VMEM_LIMIT_BYTES = 60 * 1024 * 1024


def _ffn_kernel(slots, pitch_in, pitch, xs_ref, wg_ref, wu_ref, wd_ref, gate_ref, y_ref):
    n = xs_ref.shape[0]
    x = jnp.concatenate(
        [jnp.concatenate([xs_ref[s, 0, q * pitch_in:q * pitch_in + slots, :] for q in range(SUBLANES)], axis=1)
         for s in range(n)], axis=0)
    gate = jnp.concatenate([gate_ref[s, 0] for s in range(n)], axis=0)
    a = _dot(x, wg_ref[0].astype(BF16))
    u = _dot(x, wu_ref[0].astype(BF16))
    y = _dot((_silu(a) * u).astype(BF16), wd_ref[0].astype(BF16)) * gate
    for s in range(n):
        _zero_slot_padding(y_ref, (s, 0), slots, pitch)
        for q in range(SUBLANES):
            y_ref[s, 0, q * pitch:q * pitch + slots, :] = y[s * slots:(s + 1) * slots, q * LANES:(q + 1) * LANES]


def _ffn(xs, w_gate, w_up, w_down, layer, gate, slots):
    b, n_exp = xs.shape[:2]
    pitch = _slot_pitch(slots)
    pitch_in = _slot_pitch_bf16(slots)
    d, f = w_gate.shape[2:]
    ns = FFN_SAMPLES
    assert b % ns == 0
    slot_tile = lambda p: pl.BlockSpec((ns, 1, SUBLANES * p, LANES), lambda e, i: (i, e, 0, 0))
    weight = lambda rows, cols: pl.BlockSpec((None, 1, rows, cols), lambda e, i: (layer, e, 0, 0))
    return pl.pallas_call(
        functools.partial(_ffn_kernel, slots, pitch_in, pitch),
        grid=(n_exp, b // ns),
        in_specs=[
            slot_tile(pitch_in),
            weight(d, f), weight(d, f), weight(f, d),
            pl.BlockSpec((ns, 1, slots, 1), lambda e, i: (i, e, 0, 0)),
        ],
        out_specs=slot_tile(pitch),
        out_shape=jax.ShapeDtypeStruct((b, n_exp, SUBLANES * pitch, LANES), F32),
        compiler_params=_params("parallel", "arbitrary", vmem=FFN_VMEM_LIMIT_BYTES),
        name="ffn",
    )(xs, w_gate, w_up, w_down, gate)


SCATTER_BATCH = 16


def _scatter_kernel(slots, pitch, idx_ref, y_ref, o_ref):
    @pl.when(pl.program_id(1) == 0)
    def _():
        o_ref[...] = jnp.zeros_like(o_ref)

    for g in range(DISPATCH_EXPERTS):
        for m0 in range(0, slots, SCATTER_BATCH):
            rows = [pl.multiple_of(idx_ref[g, 0, m0 + u] * SUBLANES, SUBLANES) for u in range(SCATTER_BATCH)]
            vals = [o_ref[0, pl.ds(rows[u], SUBLANES), :] + y_ref[0, g, pl.ds(m0 + u, SUBLANES, stride=pitch), :]
                    for u in range(SCATTER_BATCH)]
            for u in range(SCATTER_BATCH):
                o_ref[0, pl.ds(rows[u], SUBLANES), :] = vals[u]


def _scatter(y, idx_rows, tu, slots):
    b, n_exp = y.shape[:2]
    pitch = _slot_pitch(slots)
    ge = DISPATCH_EXPERTS
    assert n_exp % ge == 0 and slots % SCATTER_BATCH == 0
    return pl.pallas_call(
        functools.partial(_scatter_kernel, slots, pitch),
        grid=(b, n_exp // ge),
        in_specs=[
            pl.BlockSpec((ge, 1, slots), lambda i, e: (i * (n_exp // ge) + e, 0, 0), memory_space=pltpu.SMEM),
            pl.BlockSpec((1, ge, SUBLANES * pitch, LANES), lambda i, e: (i, e, 0, 0)),
        ],
        out_specs=pl.BlockSpec((1, tu * SUBLANES, LANES), lambda i, e: (i, 0, 0)),
        out_shape=jax.ShapeDtypeStruct((b, tu * SUBLANES, LANES), F32),
        compiler_params=_params("parallel", "arbitrary"),
        name="scatter",
    )(idx_rows, y)


FINAL_PARTS = 4


def _final_kernel(mod_ref, g_ref, *refs):
    o_ref = refs[-1]
    tm = FINAL_TILE
    for n in range(FINAL_PARTS):
        x_ref, f_ref = refs[n], refs[FINAL_PARTS + n]
        x = x_ref[0] + mod_ref[0][M_G2:M_G2 + 1, :] * _from_gather_layout(f_ref, tm)
        o_ref[0, n * tm:(n + 1) * tm, :] = _rms(x) * g_ref[...]


def _final(x, ffn, mrows, final_g, n_ctx):
    b, tu, d = x.shape
    tm = FINAL_TILE
    skip = n_ctx // tm
    parts = FINAL_PARTS
    assert (tu - n_ctx) % (tm * parts) == 0
    tile = lambda n: pl.BlockSpec((1, tm, d), lambda i, j: (i, j * parts + n + skip, 0))
    ftile = lambda n: pl.BlockSpec((1, tm * SUBLANES, LANES), lambda i, j: (i, j * parts + n + skip, 0))
    return pl.pallas_call(
        _final_kernel,
        grid=(b, (tu - n_ctx) // (tm * parts)),
        in_specs=([pl.BlockSpec((1, M_ROWS, d), lambda i, j: (i, 0, 0)), pl.BlockSpec((1, d), lambda i, j: (0, 0))]
                  + [tile(n) for n in range(parts)] + [ftile(n) for n in range(parts)]),
        out_specs=pl.BlockSpec((1, tm * parts, d), lambda i, j: (i, j, 0)),
        out_shape=jax.ShapeDtypeStruct((b, tu - n_ctx, d), F32),
        compiler_params=_params("parallel", "parallel"),
        name="final_norm",
    )(mrows, final_g.reshape(1, d), *([x] * parts + [ffn] * parts))


def _rope_tables(n_lat, n_ctx):
    rows = n_lat // GRID_W
    rowp = jnp.repeat(jnp.arange(rows, dtype=F32), GRID_W)
    colp = jnp.tile(jnp.arange(GRID_W, dtype=F32), rows)
    axis_dim = HEAD_DIM // 2
    inv_freq = ROPE_BASE ** (-jnp.arange(0, axis_dim, 2, dtype=F32) / axis_dim)
    ar = rowp[:, None] * inv_freq
    ac = colp[:, None] * inv_freq
    cs = jnp.concatenate([jnp.cos(ar), jnp.cos(ar), jnp.cos(ac), jnp.cos(ac)], axis=1)
    sn = jnp.concatenate([-jnp.sin(ar), jnp.sin(ar), -jnp.sin(ac), jnp.sin(ac)], axis=1)
    reps = LANES // HEAD_DIM
    cs = jnp.concatenate([jnp.ones((n_ctx, LANES), F32), jnp.tile(cs, (1, reps))], axis=0)
    sn = jnp.concatenate([jnp.zeros((n_ctx, LANES), F32), jnp.tile(sn, (1, reps))], axis=0)
    return cs, sn


def kernel(x, c, ctx, c_ctx, w_mod, b_mod, norm1_g, norm2_g, w_in, conv_w, ret_decay_logit, attn_sink,
           w_out, w_router, w_gate, w_up, w_down, final_g):
    b, n_lat, d = x.shape
    n_ctx = ctx.shape[1]
    depth = w_in.shape[0]
    n_exp = w_router.shape[2]
    tu = n_ctx + n_lat
    assert w_in.shape[2] == IN_WIDTH and tu % INPROJ_TILE == 0
    assert n_ctx % FINAL_TILE == 0 and n_lat % FINAL_TILE == 0
    assert b + 1 <= SUBLANES and n_exp == N_EXPERTS

    c_rows = jnp.concatenate([c, c_ctx[None], jnp.zeros((SUBLANES - b - 1, d), F32)], axis=0)
    mods = _mod_vectors(c_rows, w_mod, b_mod).reshape(depth, SUBLANES, 6, d)
    qdec, win, cd, dm = _decay_tables(ret_decay_logit)
    rope_cs, rope_sn = _rope_tables(n_lat, n_ctx)
    sink_rows = jnp.broadcast_to(jnp.repeat(attn_sink.astype(F32), CHUNK, axis=1)[:, :, None],
                                 (depth, ATTN_HEADS * CHUNK, LANES))
    w_in_bf = w_in.astype(BF16)
    w_out_bf = w_out.astype(BF16)

    xu = jnp.concatenate([ctx, x], axis=1)
    ffn = None
    mrows = None
    cap = CAPACITY_FACTOR * n_lat // n_exp + CAPACITY_FACTOR * n_ctx // n_exp
    for l in range(depth):
        prev_mrows = mrows
        mrows = jnp.concatenate([mods[l, :b], jnp.broadcast_to(mods[l, b][None], (b, 6, d)),
                                 jnp.zeros((b, M_ROWS - 12, d), F32)], axis=1)
        xu, proj = _inproj(xu, ffn, mrows if ffn is None else _with_prev_g2(mrows, prev_mrows),
                           norm1_g[l], w_in_bf, l, rope_cs, rope_sn, n_ctx)
        sp = _ret_states(proj[1], win[l], cd[l], n_ctx)
        xu, hg, aff = _mixer(proj, sp, (dm[l], qdec[l]), conv_w[l], sink_rows[l],
                             xu, mrows, norm2_g[l], w_out_bf, l, w_router[l], n_ctx)
        idx, gate = _route(aff, n_ctx)
        idx_rows = idx.reshape(b * n_exp, 1, cap)
        xs = _gather(hg, idx_rows, n_exp, cap)
        y = _ffn(xs, w_gate, w_up, w_down, l, gate, cap)
        ffn = _scatter(y, idx_rows, tu, cap)
    return _final(xu, ffn, mrows, final_g, n_ctx)


def _with_prev_g2(mrows, prev_mrows):
    out = mrows.at[:, M_G2].set(prev_mrows[:, M_G2])
    return out.at[:, M_CTX + M_G2].set(prev_mrows[:, M_CTX + M_G2])
```

```python
import functools

import jax
import jax.numpy as jnp
from jax import lax
from jax.experimental import pallas as pl
from jax.experimental.pallas import tpu as pltpu

F32 = jnp.float32
BF16 = jnp.bfloat16
I32 = jnp.int32

HEAD_DIM = 64
CONV_WIDTH = 256
RET_HEADS = 4
RET_WIDTH = RET_HEADS * HEAD_DIM
ATTN_HEADS = 8
ATTN_KV_HEADS = 2
ATTN_GROUP = ATTN_HEADS // ATTN_KV_HEADS
ATTN_WIDTH = ATTN_HEADS * HEAD_DIM
KV_WIDTH = ATTN_KV_HEADS * HEAD_DIM
CHUNK = 128
GRID_W = 64
N_EXPERTS = 16
CAPACITY_FACTOR = 2
ROPE_BASE = 10000.0
EPS = 1e-6
NEG_INF = -1e30
QK_SCALE = HEAD_DIM ** -0.5
LOG2E = 1.4426950408889634

LANES = 128
SUBLANES = 8
VMEM_LIMIT_BYTES = 56 * 1024 * 1024

O_CB = 0
O_CC = O_CB + CONV_WIDTH
O_CX = O_CC + CONV_WIDTH
O_RQ = O_CX + CONV_WIDTH
O_RK = O_RQ + RET_WIDTH
O_RV = O_RK + RET_WIDTH
O_GF = O_RV + RET_WIDTH
O_GB = O_GF + RET_WIDTH
O_AQ = O_GB + RET_WIDTH
O_AK = O_AQ + ATTN_WIDTH
O_AV = O_AK + KV_WIDTH
IN_WIDTH = O_AV + KV_WIDTH

M_SH1, M_SC1, M_G1, M_SH2, M_SC2, M_G2 = range(6)
M_CTX = 6
M_ROWS = 16

INPROJ_TILE = 1088
FINAL_TILE = 256


def _params(*sem, vmem=VMEM_LIMIT_BYTES):
    return pltpu.CompilerParams(dimension_semantics=sem, vmem_limit_bytes=vmem)


def _dot(a, b):
    return jnp.dot(a, b, preferred_element_type=F32)


def _dot_nt(a, b):
    return lax.dot_general(a, b, (((1,), (1,)), ((), ())), preferred_element_type=F32)


def _silu(v):
    return v * jax.nn.sigmoid(v)


def _mod_kernel(c_ref, w_ref, b_ref, o_ref):
    s = _silu(c_ref[...])
    s_hi = s.astype(BF16)
    s_lo = (s - s_hi.astype(F32)).astype(BF16)
    w = w_ref[0]
    w_hi = w.astype(BF16)
    w_lo = (w - w_hi.astype(F32)).astype(BF16)
    n = s.shape[0]
    head = _dot(jnp.concatenate([s_hi, s_lo], axis=0), w_hi)
    o_ref[0] = head[:n] + (head[n:] + _dot(s_hi, w_lo)) + b_ref[0]


def _mod_vectors(c_rows, w_mod, b_mod):
    depth, d_model, width = w_mod.shape
    tn = 1536
    return pl.pallas_call(
        _mod_kernel,
        grid=(depth, width // tn),
        in_specs=[
            pl.BlockSpec((SUBLANES, d_model), lambda l, n: (0, 0)),
            pl.BlockSpec((1, d_model, tn), lambda l, n: (l, 0, n)),
            pl.BlockSpec((1, 1, tn), lambda l, n: (l, 0, n)),
        ],
        out_specs=pl.BlockSpec((1, SUBLANES, tn), lambda l, n: (l, 0, n)),
        out_shape=jax.ShapeDtypeStruct((depth, SUBLANES, width), F32),
        compiler_params=_params("parallel", "parallel"),
        name="mod_vectors",
    )(c_rows, w_mod, b_mod.reshape(depth, 1, width))


def _log_sigmoid(v):
    return -jnp.log(1.0 + jnp.exp(-v))


def _tables_kernel(lgl_ref, lgr_ref, qdec_ref, win_ref, cd_ref, dm_ref):
    pos = lax.broadcasted_iota(I32, (CHUNK, RET_WIDTH), 0).astype(F32)
    ri = lax.broadcasted_iota(I32, (CHUNK, RET_HEADS * CHUNK), 0).astype(F32)
    rj = (lax.broadcasted_iota(I32, (CHUNK, RET_HEADS * CHUNK), 1) & (CHUNK - 1)).astype(F32)
    for d in range(2):
        lg = _log_sigmoid(lgl_ref[0, d])
        lg1 = lg[0:1, :]
        if d == 0:
            qdec_ref[0, d] = jnp.exp(lg1 * (pos + 1.0))
            win_ref[0, d] = jnp.exp(lg1 * (CHUNK - 1.0 - pos))
            diff = ri - rj
        else:
            qdec_ref[0, d] = jnp.exp(lg1 * (CHUNK - pos))
            win_ref[0, d] = jnp.exp(lg1 * pos)
            diff = rj - ri
        cd_ref[0, d] = jnp.exp(lg * float(CHUNK))
        lr = _log_sigmoid(lgr_ref[0, d])
        dm_ref[0, d] = jnp.where(diff >= 0.0, jnp.exp(lr * jnp.maximum(diff, 0.0)), 0.0)


def _decay_tables(ret_decay_logit):
    depth = ret_decay_logit.shape[0]
    lg = ret_decay_logit.astype(F32)
    lgl = jnp.broadcast_to(jnp.repeat(lg, HEAD_DIM, axis=-1)[:, :, None, :], (depth, 2, SUBLANES, RET_WIDTH))
    lgr = jnp.broadcast_to(jnp.repeat(lg, CHUNK, axis=-1)[:, :, None, :], (depth, 2, CHUNK, RET_HEADS * CHUNK))
    return pl.pallas_call(
        _tables_kernel,
        grid=(depth,),
        in_specs=[
            pl.BlockSpec((1, 2, SUBLANES, RET_WIDTH), lambda l: (l, 0, 0, 0)),
            pl.BlockSpec((1, 2, CHUNK, RET_HEADS * CHUNK), lambda l: (l, 0, 0, 0)),
        ],
        out_specs=[
            pl.BlockSpec((1, 2, CHUNK, RET_WIDTH), lambda l: (l, 0, 0, 0)),
            pl.BlockSpec((1, 2, CHUNK, RET_WIDTH), lambda l: (l, 0, 0, 0)),
            pl.BlockSpec((1, 2, SUBLANES, RET_WIDTH), lambda l: (l, 0, 0, 0)),
            pl.BlockSpec((1, 2, CHUNK, RET_HEADS * CHUNK), lambda l: (l, 0, 0, 0)),
        ],
        out_shape=[
            jax.ShapeDtypeStruct((depth, 2, CHUNK, RET_WIDTH), F32),
            jax.ShapeDtypeStruct((depth, 2, CHUNK, RET_WIDTH), F32),
            jax.ShapeDtypeStruct((depth, 2, SUBLANES, RET_WIDTH), F32),
            jax.ShapeDtypeStruct((depth, 2, CHUNK, RET_HEADS * CHUNK), F32),
        ],
        compiler_params=_params("parallel"),
        name="decay_tables",
    )(lgl, lgr)


def _rope(v, cs, sn):
    lane = lax.broadcasted_iota(I32, (1, LANES), 1)
    first = (lane & 31) < 16
    outs = []
    for g in range(v.shape[1] // LANES):
        vg = v[:, g * LANES:(g + 1) * LANES]
        sw = jnp.where(first, pltpu.roll(vg, LANES - 16, 1), pltpu.roll(vg, 16, 1))
        outs.append(vg * cs + sw * sn)
    return outs[0] if len(outs) == 1 else jnp.concatenate(outs, axis=1)


def _swap_halves(v):
    return jnp.concatenate([v[:, HEAD_DIM:], v[:, :HEAD_DIM]], axis=1)


def _from_gather_layout(f_ref, tm):
    return jnp.concatenate(
        [f_ref[0, pl.ds(q, tm, stride=SUBLANES), :] for q in range(SUBLANES)], axis=1)


def _rms(v):
    return v * lax.rsqrt(jnp.mean(v * v, axis=-1, keepdims=True) + EPS)


def _mod_rows(mod_ref, tm, n_ctx, tile):
    m = mod_ref[0]
    if n_ctx % tm == 0:
        is_ctx = tile < n_ctx // tm
    else:
        is_ctx = tile * tm + lax.broadcasted_iota(I32, (tm, 1), 0) < n_ctx

    def row(r):
        return jnp.where(is_ctx, m[M_CTX + r:M_CTX + r + 1, :], m[r:r + 1, :])

    return row


def _inproj_kernel(has_ffn, n_ctx, *refs):
    if has_ffn:
        x_ref, f_ref, mod_ref, g_ref, w_ref, cs_ref, sn_ref = refs[:7]
        outs = refs[7:]
        xo_ref = outs[0]
        outs = outs[1:]
    else:
        x_ref, mod_ref, g_ref, w_ref, cs_ref, sn_ref = refs[:6]
        outs = refs[6:]
    cz_ref, rqkv_ref, gates_ref, aq_ref, kv_ref = outs
    tm = x_ref.shape[1]
    row = _mod_rows(mod_ref, tm, n_ctx, pl.program_id(1))

    x = x_ref[0]
    if has_ffn:
        x = x + row(M_G2) * _from_gather_layout(f_ref, tm)
        xo_ref[0] = x
    h = (_rms(x) * g_ref[...]) * (1.0 + row(M_SC1)) + row(M_SH1)
    h = h.astype(BF16)
    cs = cs_ref[...]
    sn = sn_ref[...]

    def proj(a, b):
        return _dot(h, w_ref[:, a:b])

    bounds = (O_CB, O_CC, O_CX, O_RQ, O_RK, O_RV, O_GF, O_GB, O_AQ, O_AK, O_AV, IN_WIDTH)
    p = [proj(a, b) for a, b in zip(bounds[:-1], bounds[1:])]
    cz_ref[0, :, 0:CONV_WIDTH] = p[0]
    cz_ref[0, :, CONV_WIDTH:] = p[1] * p[2]
    rqkv_ref[0, :, 0:RET_WIDTH] = _rope(p[3], cs, sn).astype(BF16)
    rqkv_ref[0, :, RET_WIDTH:2 * RET_WIDTH] = _rope(p[4] * QK_SCALE, cs, sn).astype(BF16)
    rqkv_ref[0, :, 2 * RET_WIDTH:] = p[5].astype(BF16)
    gates_ref[0, :, 0:RET_WIDTH] = _silu(p[6])
    gates_ref[0, :, RET_WIDTH:] = _silu(p[7])
    aq_ref[0] = (_rope(p[8], cs, sn) * (QK_SCALE * LOG2E)).astype(BF16)
    ak = _rope(p[9], cs, sn)
    av = p[10]
    kv_ref[0, :, 0:KV_WIDTH] = ak.astype(BF16)
    kv_ref[0, :, KV_WIDTH:2 * KV_WIDTH] = _swap_halves(ak).astype(BF16)
    kv_ref[0, :, 2 * KV_WIDTH:] = av.astype(BF16)


def _inproj(x, ffn, mrows, norm_g, w_in_bf, layer, rope_cs, rope_sn, n_ctx):
    b, tu, d = x.shape
    tm = INPROJ_TILE
    has_ffn = ffn is not None
    tok = lambda width: pl.BlockSpec((1, tm, width), lambda i, j: (i, j, 0))
    in_specs = [tok(d)]
    args = [x]
    if has_ffn:
        in_specs.append(pl.BlockSpec((1, tm * SUBLANES, LANES), lambda i, j: (i, j, 0)))
        args.append(ffn)
    in_specs += [
        pl.BlockSpec((1, M_ROWS, d), lambda i, j: (i, 0, 0)),
        pl.BlockSpec((1, d), lambda i, j: (0, 0)),
        pl.BlockSpec((None, d, IN_WIDTH), lambda i, j: (layer, 0, 0), pipeline_mode=pl.Buffered(1)),
        pl.BlockSpec((tm, LANES), lambda i, j: (j, 0)),
        pl.BlockSpec((tm, LANES), lambda i, j: (j, 0)),
    ]
    args += [mrows, norm_g.reshape(1, d), w_in_bf, rope_cs, rope_sn]
    widths = [(2 * CONV_WIDTH, F32), (3 * RET_WIDTH, BF16), (2 * RET_WIDTH, F32), (ATTN_WIDTH, BF16),
              (3 * KV_WIDTH, BF16)]
    out_specs = [tok(w) for w, _ in widths]
    out_shape = [jax.ShapeDtypeStruct((b, tu, w), dt) for w, dt in widths]
    if has_ffn:
        out_specs = [tok(d)] + out_specs
        out_shape = [jax.ShapeDtypeStruct((b, tu, d), F32)] + out_shape
    res = pl.pallas_call(
        functools.partial(_inproj_kernel, has_ffn, n_ctx),
        grid=(b, tu // tm),
        in_specs=in_specs,
        out_specs=out_specs,
        out_shape=out_shape,
        compiler_params=_params("parallel", "parallel"),
        name="inproj",
    )(*args)
    if has_ffn:
        return res[0], res[1:]
    return x, res


def _head_block_mask(n):
    r = lax.broadcasted_iota(I32, (n, n), 0) // HEAD_DIM
    c = lax.broadcasted_iota(I32, (n, n), 1) // HEAD_DIM
    return r == c


STATE_UNROLL = 2


def _states_kernel(n_ctx_chunks, n_chunks, rqkv_ref, win_ref, cd_ref, sp_ref, s_scr):
    same_head = _head_block_mask(RET_WIDTH)
    s_scr[...] = jnp.zeros_like(s_scr)

    def chunk_update(d, pos):
        if d == 0:
            c = pos
        else:
            c = jnp.where(pos < n_ctx_chunks, n_ctx_chunks - 1 - pos, n_chunks + n_ctx_chunks - 1 - pos)
        off = pl.multiple_of(c * CHUNK, CHUNK)
        kw = rqkv_ref[0, pl.ds(off, CHUNK), RET_WIDTH:2 * RET_WIDTH].astype(F32) * win_ref[d]
        v = rqkv_ref[0, pl.ds(off, CHUNK), 2 * RET_WIDTH:]
        kw_t = kw.T.astype(BF16)
        yield
        u = _dot(kw_t, v)
        yield
        return c, jnp.where(same_head, u, 0.0)

    def body(i, carry):
        todo = [(d, i * STATE_UNROLL + n) for n in range(STATE_UNROLL) for d in range(2)]
        done = _round_robin([chunk_update(d, pos) for d, pos in todo])
        for (d, _), (c, u) in zip(todo, done):
            s = s_scr[d]
            sp_ref[0, d, pl.ds(c, 1)] = s.astype(BF16)[None]
            s_scr[d] = s * cd_ref[d][0:1, :] + u
        return carry

    assert n_chunks % STATE_UNROLL == 0
    lax.fori_loop(0, n_chunks // STATE_UNROLL, body, 0)


def _ret_states(rqkv, win, cd, n_ctx):
    b, tu, _ = rqkv.shape
    w = RET_WIDTH
    n_chunks = tu // CHUNK
    return pl.pallas_call(
        functools.partial(_states_kernel, n_ctx // CHUNK, n_chunks),
        grid=(b,),
        in_specs=[
            pl.BlockSpec((1, tu, 3 * w), lambda i: (i, 0, 0)),
            pl.BlockSpec((2, CHUNK, w), lambda i: (0, 0, 0)),
            pl.BlockSpec((2, SUBLANES, w), lambda i: (0, 0, 0)),
        ],
        out_specs=pl.BlockSpec((1, 2, n_chunks, w, w), lambda i: (i, 0, 0, 0, 0)),
        out_shape=jax.ShapeDtypeStruct((b, 2, n_chunks, w, w), BF16),
        scratch_shapes=[pltpu.VMEM((2, w, w), F32)],
        compiler_params=_params("parallel"),
        name="ret_states",
    )(rqkv, win, cd)


def _group_mean(t, ones_bd):
    hi = t.astype(BF16)
    lo = (t - hi.astype(F32)).astype(BF16)
    s = _dot(jnp.concatenate([hi, lo], axis=0), ones_bd)
    n = t.shape[0]
    return (s[:n] + s[n:]) * (1.0 / HEAD_DIM)


def _mix_chunk(c, n_ctx_chunks, n_chunks, cz, z_last, z_first, cw, rqkv, gates, sp, dm_ref, qdec_ref,
               qa, kvs, sink_ref):
    is_lat = c >= n_ctx_chunks

    z = cz[:, CONV_WIDTH:]
    row = lax.broadcasted_iota(I32, (CHUNK, 1), 0)
    has_prev = jnp.logical_and(c != 0, c != n_ctx_chunks)
    has_next = jnp.logical_and(c != n_ctx_chunks - 1, c != n_chunks - 1)
    z_before = jnp.where(row == 0, jnp.where(has_prev, z_last, 0.0), pltpu.roll(z, 1, 0))
    z_after = jnp.where(row == CHUNK - 1, jnp.where(has_next, z_first, 0.0), pltpu.roll(z, CHUNK - 1, 0))
    conv = cz[:, :CONV_WIDTH] * (z_before * cw[0:1, :] + z * cw[1:2, :] + z_after * cw[2:3, :])

    q = rqkv[:, 0:RET_WIDTH]
    k = rqkv[:, RET_WIDTH:2 * RET_WIDTH]
    v = rqkv[:, 2 * RET_WIDTH:]
    lane_head = lax.broadcasted_iota(I32, (1, RET_WIDTH), 1) // HEAD_DIM
    kz = jnp.zeros_like(k)
    k_heads = jnp.concatenate([jnp.where(lane_head == hh, k, kz) for hh in range(RET_HEADS)], axis=0)
    v_heads = jnp.concatenate([jnp.where(lane_head == hh, v, kz) for hh in range(RET_HEADS)], axis=0)
    scores = _dot_nt(q, k_heads)

    keys, v_t = kvs
    half = lax.broadcasted_iota(I32, (1, LANES), 1) // HEAD_DIM
    qa_z = jnp.zeros((CHUNK, LANES), BF16)
    combos = [(hk, par) for hk in range(ATTN_KV_HEADS) for par in range(2)]
    n_ctx = keys[0].shape[0] - 3 * CHUNK
    cols2 = 2 * CHUNK
    ik = lax.broadcasted_iota(I32, (CHUNK, cols2), 0)
    iq = lax.broadcasted_iota(I32, (CHUNK, cols2), 1) & (CHUNK - 1)
    off = jnp.full((CHUNK, cols2), NEG_INF, F32)
    bias_prev = jnp.where(jnp.logical_and(jnp.logical_and(is_lat, c - 1 >= n_ctx_chunks), ik >= iq), 0.0, off)
    bias_cur = jnp.where(is_lat, 0.0, off)
    bias_next = jnp.where(jnp.logical_and(jnp.logical_and(is_lat, c + 1 <= n_chunks - 1), ik <= iq), 0.0, off)
    assert n_ctx > 0
    logits = {}
    for hk, par in combos:
        ja, jb = 2 * hk, 2 * hk + 1
        qst = jnp.concatenate([
            jnp.where(half == par, qa[:, ja * LANES:(ja + 1) * LANES], qa_z),
            jnp.where(half == par, qa[:, jb * LANES:(jb + 1) * LANES], qa_z)], axis=0)
        sel = 0 if par == hk else 1
        r = _dot_nt(keys[sel], qst)
        s = jnp.concatenate([r[:CHUNK] + bias_prev, r[CHUNK:2 * CHUNK] + bias_cur,
                             r[2 * CHUNK:3 * CHUNK] + bias_next, r[3 * CHUNK:]], axis=0)
        ha, hb = ATTN_GROUP * hk + par, ATTN_GROUP * hk + par + 2
        snk = jnp.concatenate([sink_ref[ha * CHUNK:ha * CHUNK + 1, :],
                               sink_ref[hb * CHUNK:hb * CHUNK + 1, :]], axis=1) * LOG2E
        logits[hk, par] = (s, snk, jnp.maximum(jnp.max(s, axis=0, keepdims=True), snk))
    yield

    qf = q.astype(F32)
    outs = []
    for d in range(2):
        p = (scores * dm_ref[d]).astype(BF16)
        lhs = jnp.concatenate([p, (qf * qdec_ref[d]).astype(BF16)], axis=1)
        o = _dot(lhs, jnp.concatenate([v_heads, sp[d]], axis=0))
        outs.append(o)
    o2 = jnp.concatenate(outs, axis=0)
    yield

    att = {}
    for n, (hk, par) in enumerate(combos):
        s, snk, mx = logits[hk, par]
        e = jnp.exp2(s - mx)
        den = jnp.sum(e, axis=0, keepdims=True) + jnp.exp2(snk - mx)
        o_t = _dot(v_t[hk * HEAD_DIM:(hk + 1) * HEAD_DIM, :], e.astype(BF16))
        att[hk, par] = o_t * (1.0 / den)
        if n == 0:
            ones_bd = jnp.where(_head_block_mask(RET_WIDTH), 1.0, 0.0).astype(BF16)
            dl = o2 - _group_mean(o2, ones_bd)
            on = dl * lax.rsqrt(_group_mean(dl * dl, ones_bd) + EPS)
            ret = on[:CHUNK] * gates[:, 0:RET_WIDTH] + on[CHUNK:] * gates[:, RET_WIDTH:]
        yield

    cols = []
    for hk in range(ATTN_KV_HEADS):
        cols.append(jnp.concatenate([att[hk, 0][:, :CHUNK], att[hk, 1][:, :CHUNK]], axis=0).T)
        cols.append(jnp.concatenate([att[hk, 0][:, CHUNK:], att[hk, 1][:, CHUNK:]], axis=0).T)
    return jnp.concatenate([conv, ret] + cols, axis=1)


def _round_robin(gens):
    results = [None] * len(gens)
    active = list(range(len(gens)))
    while active:
        for i in list(active):
            try:
                next(gens[i])
            except StopIteration as done:
                results[i] = done.value
                active.remove(i)
    return results


MIX_CHUNKS = 2


def _mixer_kernel(n_ctx_chunks, n_chunks, n_exp,
                  cz_ref, czp_ref, czn_ref, cw_ref, rqkv_ref, gates_ref, sp_ref, dm_ref, qdec_ref,
                  aq_ref, kvp_ref, kvc_ref, kvn_ref, kvx_ref, sink_ref,
                  x_ref, mod_ref, g_ref, w_ref, wrh_ref, wrl_ref, xo_ref, hg_ref, aff_ref, mix_scr):
    t = pl.program_id(1)
    n_steps = pl.num_programs(1) - 1

    @pl.when(t == 0)
    def _():
        mix_scr[...] = jnp.zeros_like(mix_scr)

    tail = _outproj_tail(n_ctx_chunks * CHUNK, n_exp, jnp.maximum(t - 1, 0), mix_scr[...], x_ref, mod_ref, g_ref,
                         w_ref, wrh_ref, wrl_ref, xo_ref, hg_ref, aff_ref)
    j = jnp.minimum(t, n_steps - 1)
    cw = cw_ref[...]

    def values_t(blk):
        return blk[:, 2 * KV_WIDTH:].astype(F32).T.astype(BF16)

    kv_blocks = ([kvp_ref[0]] + [kvc_ref[0, s * CHUNK:(s + 1) * CHUNK, :] for s in range(MIX_CHUNKS)] + [kvn_ref[0]])
    kv_ctx = [kvx_ref[0, t * CHUNK:(t + 1) * CHUNK, :] for t in range(kvx_ref.shape[1] // CHUNK)]
    vt_blocks = [values_t(blk) for blk in kv_blocks]
    vt_ctx = [values_t(blk) for blk in kv_ctx]
    gens = []
    for s in range(MIX_CHUNKS):
        lo, hi = s * CHUNK, (s + 1) * CHUNK
        if s == 0:
            z_last = czp_ref[0, SUBLANES - 1:SUBLANES, CONV_WIDTH:]
        else:
            z_last = cz_ref[0, lo - 1:lo, CONV_WIDTH:]
        if s == MIX_CHUNKS - 1:
            z_first = czn_ref[0, 0:1, CONV_WIDTH:]
        else:
            z_first = cz_ref[0, hi:hi + 1, CONV_WIDTH:]
        window = kv_blocks[s:s + 3] + kv_ctx
        keys = tuple(jnp.concatenate([blk[:, i * KV_WIDTH:(i + 1) * KV_WIDTH] for blk in window], axis=0)
                     for i in (0, 1))
        v_t = jnp.concatenate(vt_blocks[s:s + 3] + vt_ctx, axis=1)
        gens.append(_mix_chunk(j * MIX_CHUNKS + s, n_ctx_chunks, n_chunks, cz_ref[0, lo:hi, :], z_last, z_first,
                               cw, rqkv_ref[0, lo:hi, :], gates_ref[0, lo:hi, :],
                               (sp_ref[0, 0, s], sp_ref[0, 1, s]), dm_ref, qdec_ref, aq_ref[0, lo:hi, :],
                               (keys, v_t), sink_ref))
    mixes = [mix.astype(BF16) for mix in _round_robin([tail] + gens)[1:]]
    mix_scr[...] = jnp.concatenate(mixes, axis=0)


def _mixer(proj, sp, tabs, conv_w, sink_rows, x, mrows, norm_g, w_out_bf, layer, w_router, n_ctx):
    cz, rqkv, gates, aq, kv = proj
    dm, qdec = tabs
    b, tu, _ = cz.shape
    d = x.shape[2]
    n_exp = w_router.shape[1]
    wr = w_router.T
    wr_hi = wr.astype(BF16)
    wr_lo = (wr - wr_hi.astype(F32)).astype(BF16)
    n_chunks = tu // CHUNK
    mc = MIX_CHUNKS
    rows = mc * CHUNK
    assert n_chunks % mc == 0 and (n_ctx // CHUNK) % mc == 0
    last = n_chunks - 1
    n_steps = n_chunks // mc
    per8 = rows // SUBLANES
    mixed = lambda t: jnp.minimum(t, n_steps - 1)
    projected = lambda t: jnp.maximum(t - 1, 0)
    cur = lambda w: pl.BlockSpec((1, rows, w), lambda i, t: (i, mixed(t), 0))
    out = lambda r, w: pl.BlockSpec((1, r, w), lambda i, t: (i, projected(t), 0))
    full = lambda shape: pl.BlockSpec(shape, lambda i, t: (0,) * len(shape))
    wcz, wkv = cz.shape[2], kv.shape[2]
    in_specs = [
        cur(wcz),
        pl.BlockSpec((1, SUBLANES, wcz), lambda i, t: (i, jnp.maximum(mixed(t) * per8 - 1, 0), 0)),
        pl.BlockSpec((1, SUBLANES, wcz),
                     lambda i, t: (i, jnp.minimum((mixed(t) + 1) * per8, tu // SUBLANES - 1), 0)),
        full(conv_w.shape),
        cur(rqkv.shape[2]), cur(gates.shape[2]),
        pl.BlockSpec((1, 2, mc, RET_WIDTH, RET_WIDTH), lambda i, t: (i, 0, mixed(t), 0, 0)),
        full(dm.shape), full(qdec.shape),
        cur(ATTN_WIDTH),
        pl.BlockSpec((1, CHUNK, wkv), lambda i, t: (i, jnp.maximum(mixed(t) * mc - 1, 0), 0)),
        cur(wkv),
        pl.BlockSpec((1, CHUNK, wkv), lambda i, t: (i, jnp.minimum((mixed(t) + 1) * mc, last), 0)),
        pl.BlockSpec((1, n_ctx, wkv), lambda i, t: (i, 0, 0)),
        full(sink_rows.shape),
        out(rows, d),
        pl.BlockSpec((1, M_ROWS, d), lambda i, t: (i, 0, 0)),
        full((1, d)),
        pl.BlockSpec((None,) + w_out_bf.shape[1:], lambda i, t: (layer, 0, 0)),
        full((n_exp, d)), full((n_exp, d)),
    ]
    args = [cz, cz, cz, conv_w, rqkv, gates, sp, dm, qdec, aq, kv, kv, kv, kv, sink_rows,
            x, mrows, norm_g.reshape(1, d), w_out_bf, wr_hi, wr_lo]
    return pl.pallas_call(
        functools.partial(_mixer_kernel, n_ctx // CHUNK, n_chunks, n_exp),
        grid=(b, n_steps + 1),
        in_specs=in_specs,
        out_specs=[out(rows, d), out(rows * SUBLANES, LANES),
                   pl.BlockSpec((1, n_exp, rows), lambda i, t: (i, 0, projected(t)))],
        out_shape=[
            jax.ShapeDtypeStruct((b, tu, d), F32),
            jax.ShapeDtypeStruct((b, tu * SUBLANES, LANES), F32),
            jax.ShapeDtypeStruct((b, n_exp, tu), F32),
        ],
        scratch_shapes=[pltpu.VMEM((rows, CONV_WIDTH + RET_WIDTH + ATTN_WIDTH), BF16)],
        compiler_params=_params("parallel", "arbitrary"),
        name="mixer",
    )(*args)


def _outproj_tail(n_ctx, n_exp, tile, mix, x_ref, mod_ref, g_ref, w_ref, wrh_ref, wrl_ref, xo_ref, hg_ref, aff_ref):
    tm = x_ref.shape[1]
    row = _mod_rows(mod_ref, tm, n_ctx, tile)

    y = _dot(mix, w_ref[...])
    yield
    x = x_ref[0] + row(M_G1) * y
    xo_ref[0] = x
    h = (_rms(x) * g_ref[...]) * (1.0 + row(M_SC2)) + row(M_SH2)
    for q in range(SUBLANES):
        hg_ref[0, pl.ds(q, tm, stride=SUBLANES), :] = h[:, q * LANES:(q + 1) * LANES]
    yield
    h_hi = h.astype(BF16)
    h_lo = (h - h_hi.astype(F32)).astype(BF16)
    logits = (_dot_nt(wrh_ref[...], h_hi) + (_dot_nt(wrh_ref[...], h_lo) + _dot_nt(wrl_ref[...], h_hi)))
    yield
    e = jnp.exp(logits - jnp.max(logits, axis=0, keepdims=True))
    aff_ref[0] = e / jnp.sum(e, axis=0, keepdims=True)


MIN_NORMAL_F32_BITS = 0x00800000
ROUTE_TABLES = 4
ROUTE_GROUP = 4


def _route_stream_tables(v, k, tab_scr, ce_scr, cnt_scr, stream, thr):
    n_exp, n = v.shape
    lane = lax.broadcasted_iota(I32, (1, LANES), 1)
    tri_r = lax.broadcasted_iota(I32, (LANES, LANES), 0)
    tri_c = lax.broadcasted_iota(I32, (LANES, LANES), 1)
    upper = jnp.where(tri_r <= tri_c, 1.0, 0.0).astype(BF16)
    above = pltpu.bitcast(jnp.maximum(thr + 1, MIN_NORMAL_F32_BITS), F32)
    floor = pltpu.bitcast(thr, F32)
    n_gt = jnp.sum(jnp.where(v >= above, 1, 0), axis=1, keepdims=True)
    need = (k - n_gt).astype(F32)
    seen_eq = jnp.zeros((n_exp, 1), F32)
    counts = jnp.zeros((n_exp, LANES), F32)
    for j in range(n // LANES):
        vb = v[:, j * LANES:(j + 1) * LANES]
        gt = vb >= above
        eq = jnp.logical_and(vb >= floor, vb < above)
        both = jnp.concatenate([jnp.where(gt, 1.0, 0.0), jnp.where(eq, 1.0, 0.0)], axis=0).astype(BF16)
        pref = _dot(both, upper)
        rank_eq = pref[n_exp:] + seen_eq
        inc = pref[:n_exp] + jnp.minimum(rank_eq, need) - jnp.minimum(seen_eq, need)
        sel = jnp.logical_or(gt, jnp.logical_and(eq, rank_eq <= need))
        seen_eq = rank_eq[:, LANES - 1:LANES]
        counts = counts + jnp.where(lane == j, inc[:, LANES - 1:LANES], 0.0)
        a0 = jnp.where(sel, vb, 0.0)
        t0 = a0.astype(BF16).astype(F32)
        t1 = (a0 - t0).astype(BF16).astype(F32)
        t2 = a0 - t0 - t1
        for t, val in enumerate((inc, t0, t1, t2)):
            tab_scr[stream * ROUTE_TABLES + t, pl.ds(j, n_exp, stride=LANES), :] = val
    through = _dot(counts.astype(BF16), upper)
    for e in range(n_exp):
        ce_scr[stream, e] = through[e:e + 1, :]
        cnt_scr[stream, e] = counts[e:e + 1, :]


def _route_slots(e, stream, k, lo, tab_scr, ce_scr, cnt_scr):
    lane_f = lax.broadcasted_iota(I32, (1, LANES), 1).astype(F32)
    slot = lax.broadcasted_iota(I32, (k, LANES), 0).astype(F32)
    ones_rows = jnp.ones((LANES, LANES), BF16)
    cnt_rows = jnp.broadcast_to(cnt_scr[stream, e], (LANES, LANES)).astype(BF16)
    rows_e = pl.ds(pl.multiple_of(e * LANES, LANES), LANES)
    tab = jnp.concatenate([tab_scr[stream * ROUTE_TABLES + t, rows_e, :] for t in range(ROUTE_TABLES)],
                          axis=1).astype(BF16)
    before = jnp.where(ce_scr[stream, e] <= slot, 1.0, 0.0).astype(BF16)
    blk = _dot_nt(before, ones_rows)
    base = _dot_nt(before, cnt_rows)
    yield
    row = _dot(jnp.where(lane_f == blk, 1.0, 0.0).astype(BF16), tab)
    yield
    inc = row[:, :LANES]
    aff = row[:, LANES:2 * LANES] + row[:, 2 * LANES:3 * LANES] + row[:, 3 * LANES:]
    local = slot - base
    pos = _dot_nt(jnp.where(inc <= local, 1.0, 0.0).astype(BF16), ones_rows)
    yield
    tok = blk * LANES + pos + float(lo)
    gate = jnp.sum(jnp.where(inc == local + 1.0, aff, 0.0), axis=1, keepdims=True)
    return tok, gate


def _route_kernel(n_exp, streams, aff_ref, idx_ref, gate_ref, tab_scr, ce_scr, cnt_scr):
    a = aff_ref[0]
    tab_scr[...] = jnp.zeros_like(tab_scr)
    vs = [a[:, lo:lo + n] for lo, n, _, _ in streams]

    def search(i, ts):
        out = []
        for t, v, (_, _, k, _) in zip(ts, vs, streams):
            cand = t | jnp.left_shift(jnp.int32(1), 30 - i)
            cnt = jnp.sum(jnp.where(v >= pltpu.bitcast(cand, F32), 1, 0), axis=1, keepdims=True)
            out.append(jnp.where(cnt >= k, cand, t))
        return tuple(out)

    thrs = lax.fori_loop(0, 31, search, tuple(jnp.zeros((n_exp, 1), I32) for _ in streams))
    for st, (v, (_, _, k, _)) in enumerate(zip(vs, streams)):
        _route_stream_tables(v, k, tab_scr, ce_scr, cnt_scr, st, thrs[st])

    slots = idx_ref.shape[3]

    assert [s[3] for s in streams] == [sum(s[2] for s in streams[:n]) for n in range(len(streams))]

    def expert_group(g, carry):
        experts = [g * ROUTE_GROUP + u for u in range(ROUTE_GROUP)]
        found = _round_robin([_route_slots(e, st, k, lo, tab_scr, ce_scr, cnt_scr)
                              for e in experts for st, (lo, _, k, _) in enumerate(streams)])
        for u, e in enumerate(experts):
            toks = []
            for st, (_, _, k, slot0) in enumerate(streams):
                tok, gate = found[u * len(streams) + st]
                toks.append(tok)
                gate_ref[0, pl.ds(e, 1), slot0:slot0 + k, :] = gate[None]
            toks.append(jnp.zeros((-slots % LANES, LANES), F32))
            row = jnp.concatenate(toks, axis=0).T[0:1, :slots]
            idx_ref[0, pl.ds(e, 1)] = row.astype(I32)[None]
        return carry

    assert n_exp % ROUTE_GROUP == 0
    lax.fori_loop(0, n_exp // ROUTE_GROUP, expert_group, 0)


def _route(aff, n_ctx):
    b, n_exp, tu = aff.shape
    n_lat = tu - n_ctx
    cap_l = CAPACITY_FACTOR * n_lat // n_exp
    cap_c = CAPACITY_FACTOR * n_ctx // n_exp
    slots = cap_l + cap_c
    assert n_lat // LANES <= LANES and n_ctx // LANES <= LANES
    streams = ((n_ctx, n_lat, cap_l, 0), (0, n_ctx, cap_c, cap_l))
    return pl.pallas_call(
        functools.partial(_route_kernel, n_exp, streams),
        grid=(b,),
        in_specs=[pl.BlockSpec((1, n_exp, tu), lambda i: (i, 0, 0))],
        out_specs=[pl.BlockSpec((1, n_exp, 1, slots), lambda i: (i, 0, 0, 0)),
                   pl.BlockSpec((1, n_exp, slots, 1), lambda i: (i, 0, 0, 0))],
        out_shape=[jax.ShapeDtypeStruct((b, n_exp, 1, slots), I32),
                   jax.ShapeDtypeStruct((b, n_exp, slots, 1), F32)],
        scratch_shapes=[pltpu.VMEM((len(streams) * ROUTE_TABLES, n_exp * LANES, LANES), F32),
                        pltpu.VMEM((len(streams), n_exp, 1, LANES), F32),
                        pltpu.VMEM((len(streams), n_exp, 1, LANES), F32)],
        compiler_params=_params("parallel"),
        name="route",
    )(aff)


def _slot_pitch(slots):
    return slots + SUBLANES


def _zero_slot_padding(ref, lead, slots, pitch):
    for q in range(SUBLANES):
        ref[lead + (pl.ds(q * pitch + slots, pitch - slots), slice(None))] = jnp.zeros((pitch - slots, LANES), ref.dtype)


BF16_ROWS = 2 * SUBLANES


def _slot_pitch_bf16(slots):
    return -(-slots // BF16_ROWS) * BF16_ROWS + BF16_ROWS


DISPATCH_EXPERTS = 4


def _gather_kernel(slots, pitch, pitch_out, idx_ref, h_ref, o_ref, tile_scr):
    for g in range(DISPATCH_EXPERTS):
        for mi in range(slots):
            r = pl.multiple_of(idx_ref[g, 0, mi] * SUBLANES, SUBLANES)
            tile_scr[g, pl.ds(mi, SUBLANES, stride=pitch), :] = h_ref[0, pl.ds(r, SUBLANES), :]
        _zero_slot_padding(o_ref, (0, g), slots, pitch_out)
        for q in range(SUBLANES):
            o_ref[0, g, q * pitch_out:q * pitch_out + slots, :] = (
                tile_scr[g, q * pitch:q * pitch + slots, :].astype(BF16))


def _gather(hg, idx_rows, n_exp, slots):
    b = hg.shape[0]
    pitch = _slot_pitch(slots)
    pitch_out = _slot_pitch_bf16(slots)
    ge = DISPATCH_EXPERTS
    assert n_exp % ge == 0
    return pl.pallas_call(
        functools.partial(_gather_kernel, slots, pitch, pitch_out),
        grid=(b, n_exp // ge),
        in_specs=[
            pl.BlockSpec((ge, 1, slots), lambda i, e: (i * (n_exp // ge) + e, 0, 0), memory_space=pltpu.SMEM),
            pl.BlockSpec((1,) + hg.shape[1:], lambda i, e: (i, 0, 0)),
        ],
        out_specs=pl.BlockSpec((1, ge, SUBLANES * pitch_out, LANES), lambda i, e: (i, e, 0, 0)),
        out_shape=jax.ShapeDtypeStruct((b, n_exp, SUBLANES * pitch_out, LANES), BF16),
        scratch_shapes=[pltpu.VMEM((ge, SUBLANES * pitch, LANES), F32)],
        compiler_params=_params("parallel", "arbitrary"),
        name="gather",
    )(idx_rows, hg)


FFN_SAMPLES = 2
FFN_VMEM_LIMIT_BYTES = 60 * 1024 * 1024


def _ffn_kernel(slots, pitch_in, pitch, xs_ref, wg_ref, wu_ref, wd_ref, gate_ref, y_ref):
    n = xs_ref.shape[0]
    x = jnp.concatenate(
        [jnp.concatenate([xs_ref[s, 0, q * pitch_in:q * pitch_in + slots, :] for q in range(SUBLANES)], axis=1)
         for s in range(n)], axis=0)
    gate = jnp.concatenate([gate_ref[s, 0] for s in range(n)], axis=0)
    a = _dot(x, wg_ref[0].astype(BF16))
    u = _dot(x, wu_ref[0].astype(BF16))
    y = _dot((_silu(a) * u).astype(BF16), wd_ref[0].astype(BF16)) * gate
    for s in range(n):
        _zero_slot_padding(y_ref, (s, 0), slots, pitch)
        for q in range(SUBLANES):
            y_ref[s, 0, q * pitch:q * pitch + slots, :] = y[s * slots:(s + 1) * slots, q * LANES:(q + 1) * LANES]


def _ffn(xs, w_gate, w_up, w_down, layer, gate, slots):
    b, n_exp = xs.shape[:2]
    pitch = _slot_pitch(slots)
    pitch_in = _slot_pitch_bf16(slots)
    d, f = w_gate.shape[2:]
    ns = FFN_SAMPLES
    assert b % ns == 0
    slot_tile = lambda p: pl.BlockSpec((ns, 1, SUBLANES * p, LANES), lambda e, i: (i, e, 0, 0))
    weight = lambda rows, cols: pl.BlockSpec((None, 1, rows, cols), lambda e, i: (layer, e, 0, 0))
    return pl.pallas_call(
        functools.partial(_ffn_kernel, slots, pitch_in, pitch),
        grid=(n_exp, b // ns),
        in_specs=[
            slot_tile(pitch_in),
            weight(d, f), weight(d, f), weight(f, d),
            pl.BlockSpec((ns, 1, slots, 1), lambda e, i: (i, e, 0, 0)),
        ],
        out_specs=slot_tile(pitch),
        out_shape=jax.ShapeDtypeStruct((b, n_exp, SUBLANES * pitch, LANES), F32),
        compiler_params=_params("parallel", "arbitrary", vmem=FFN_VMEM_LIMIT_BYTES),
        name="ffn",
    )(xs, w_gate, w_up, w_down, gate)


SCATTER_BATCH = 16


def _scatter_kernel(slots, pitch, idx_ref, y_ref, o_ref):
    @pl.when(pl.program_id(1) == 0)
    def _():
        o_ref[...] = jnp.zeros_like(o_ref)

    for g in range(DISPATCH_EXPERTS):
        for m0 in range(0, slots, SCATTER_BATCH):
            rows = [pl.multiple_of(idx_ref[g, 0, m0 + u] * SUBLANES, SUBLANES) for u in range(SCATTER_BATCH)]
            vals = [o_ref[0, pl.ds(rows[u], SUBLANES), :] + y_ref[0, g, pl.ds(m0 + u, SUBLANES, stride=pitch), :]
                    for u in range(SCATTER_BATCH)]
            for u in range(SCATTER_BATCH):
                o_ref[0, pl.ds(rows[u], SUBLANES), :] = vals[u]


def _scatter(y, idx_rows, tu, slots):
    b, n_exp = y.shape[:2]
    pitch = _slot_pitch(slots)
    ge = DISPATCH_EXPERTS
    assert n_exp % ge == 0 and slots % SCATTER_BATCH == 0
    return pl.pallas_call(
        functools.partial(_scatter_kernel, slots, pitch),
        grid=(b, n_exp // ge),
        in_specs=[
            pl.BlockSpec((ge, 1, slots), lambda i, e: (i * (n_exp // ge) + e, 0, 0), memory_space=pltpu.SMEM),
            pl.BlockSpec((1, ge, SUBLANES * pitch, LANES), lambda i, e: (i, e, 0, 0)),
        ],
        out_specs=pl.BlockSpec((1, tu * SUBLANES, LANES), lambda i, e: (i, 0, 0)),
        out_shape=jax.ShapeDtypeStruct((b, tu * SUBLANES, LANES), F32),
        compiler_params=_params("parallel", "arbitrary"),
        name="scatter",
    )(idx_rows, y)


FINAL_PARTS = 4


def _final_kernel(mod_ref, g_ref, *refs):
    o_ref = refs[-1]
    tm = FINAL_TILE
    for n in range(FINAL_PARTS):
        x_ref, f_ref = refs[n], refs[FINAL_PARTS + n]
        x = x_ref[0] + mod_ref[0][M_G2:M_G2 + 1, :] * _from_gather_layout(f_ref, tm)
        o_ref[0, n * tm:(n + 1) * tm, :] = _rms(x) * g_ref[...]


def _final(x, ffn, mrows, final_g, n_ctx):
    b, tu, d = x.shape
    tm = FINAL_TILE
    skip = n_ctx // tm
    parts = FINAL_PARTS
    assert (tu - n_ctx) % (tm * parts) == 0
    tile = lambda n: pl.BlockSpec((1, tm, d), lambda i, j: (i, j * parts + n + skip, 0))
    ftile = lambda n: pl.BlockSpec((1, tm * SUBLANES, LANES), lambda i, j: (i, j * parts + n + skip, 0))
    return pl.pallas_call(
        _final_kernel,
        grid=(b, (tu - n_ctx) // (tm * parts)),
        in_specs=([pl.BlockSpec((1, M_ROWS, d), lambda i, j: (i, 0, 0)), pl.BlockSpec((1, d), lambda i, j: (0, 0))]
                  + [tile(n) for n in range(parts)] + [ftile(n) for n in range(parts)]),
        out_specs=pl.BlockSpec((1, tm * parts, d), lambda i, j: (i, j, 0)),
        out_shape=jax.ShapeDtypeStruct((b, tu - n_ctx, d), F32),
        compiler_params=_params("parallel", "parallel"),
        name="final_norm",
    )(mrows, final_g.reshape(1, d), *([x] * parts + [ffn] * parts))


def _rope_tables(n_lat, n_ctx):
    rows = n_lat // GRID_W
    rowp = jnp.repeat(jnp.arange(rows, dtype=F32), GRID_W)
    colp = jnp.tile(jnp.arange(GRID_W, dtype=F32), rows)
    axis_dim = HEAD_DIM // 2
    inv_freq = ROPE_BASE ** (-jnp.arange(0, axis_dim, 2, dtype=F32) / axis_dim)
    ar = rowp[:, None] * inv_freq
    ac = colp[:, None] * inv_freq
    cs = jnp.concatenate([jnp.cos(ar), jnp.cos(ar), jnp.cos(ac), jnp.cos(ac)], axis=1)
    sn = jnp.concatenate([-jnp.sin(ar), jnp.sin(ar), -jnp.sin(ac), jnp.sin(ac)], axis=1)
    reps = LANES // HEAD_DIM
    cs = jnp.concatenate([jnp.ones((n_ctx, LANES), F32), jnp.tile(cs, (1, reps))], axis=0)
    sn = jnp.concatenate([jnp.zeros((n_ctx, LANES), F32), jnp.tile(sn, (1, reps))], axis=0)
    return cs, sn


def kernel(x, c, ctx, c_ctx, w_mod, b_mod, norm1_g, norm2_g, w_in, conv_w, ret_decay_logit, attn_sink,
           w_out, w_router, w_gate, w_up, w_down, final_g):
    b, n_lat, d = x.shape
    n_ctx = ctx.shape[1]
    depth = w_in.shape[0]
    n_exp = w_router.shape[2]
    tu = n_ctx + n_lat
    assert w_in.shape[2] == IN_WIDTH and tu % INPROJ_TILE == 0
    assert n_ctx % FINAL_TILE == 0 and n_lat % FINAL_TILE == 0
    assert b + 1 <= SUBLANES and n_exp == N_EXPERTS

    c_rows = jnp.concatenate([c, c_ctx[None], jnp.zeros((SUBLANES - b - 1, d), F32)], axis=0)
    mods = _mod_vectors(c_rows, w_mod, b_mod).reshape(depth, SUBLANES, 6, d)
    qdec, win, cd, dm = _decay_tables(ret_decay_logit)
    rope_cs, rope_sn = _rope_tables(n_lat, n_ctx)
    sink_rows = jnp.broadcast_to(jnp.repeat(attn_sink.astype(F32), CHUNK, axis=1)[:, :, None],
                                 (depth, ATTN_HEADS * CHUNK, LANES))
    w_in_bf = w_in.astype(BF16)
    w_out_bf = w_out.astype(BF16)

    xu = jnp.concatenate([ctx, x], axis=1)
    ffn = None
    mrows = None
    cap = CAPACITY_FACTOR * n_lat // n_exp + CAPACITY_FACTOR * n_ctx // n_exp
    for l in range(depth):
        prev_mrows = mrows
        mrows = jnp.concatenate([mods[l, :b], jnp.broadcast_to(mods[l, b][None], (b, 6, d)),
                                 jnp.zeros((b, M_ROWS - 12, d), F32)], axis=1)
        xu, proj = _inproj(xu, ffn, mrows if ffn is None else _with_prev_g2(mrows, prev_mrows),
                           norm1_g[l], w_in_bf, l, rope_cs, rope_sn, n_ctx)
        sp = _ret_states(proj[1], win[l], cd[l], n_ctx)
        xu, hg, aff = _mixer(proj, sp, (dm[l], qdec[l]), conv_w[l], sink_rows[l],
                             xu, mrows, norm2_g[l], w_out_bf, l, w_router[l], n_ctx)
        idx, gate = _route(aff, n_ctx)
        idx_rows = idx.reshape(b * n_exp, 1, cap)
        xs = _gather(hg, idx_rows, n_exp, cap)
        y = _ffn(xs, w_gate, w_up, w_down, l, gate, cap)
        ffn = _scatter(y, idx_rows, tu, cap)
    return _final(xu, ffn, mrows, final_g, n_ctx)


def _with_prev_g2(mrows, prev_mrows):
    out = mrows.at[:, M_G2].set(prev_mrows[:, M_G2])
    return out.at[:, M_CTX + M_G2].set(prev_mrows[:, M_CTX + M_G2])
```

```python
import functools

import jax
import jax.numpy as jnp
from jax import lax
from jax.experimental import pallas as pl
from jax.experimental.pallas import tpu as pltpu

F32 = jnp.float32
BF16 = jnp.bfloat16
I32 = jnp.int32

HEAD_DIM = 64
CONV_WIDTH = 256
RET_HEADS = 4
RET_WIDTH = RET_HEADS * HEAD_DIM
ATTN_HEADS = 8
ATTN_KV_HEADS = 2
ATTN_GROUP = ATTN_HEADS // ATTN_KV_HEADS
ATTN_WIDTH = ATTN_HEADS * HEAD_DIM
KV_WIDTH = ATTN_KV_HEADS * HEAD_DIM
CHUNK = 128
GRID_W = 64
N_EXPERTS = 16
CAPACITY_FACTOR = 2
ROPE_BASE = 10000.0
EPS = 1e-6
NEG_INF = -1e30
QK_SCALE = HEAD_DIM ** -0.5
LOG2E = 1.4426950408889634

LANES = 128
SUBLANES = 8
BF16_ROWS = 2 * SUBLANES
VMEM_LIMIT_BYTES = 56 * 1024 * 1024

O_CB = 0
O_CC = O_CB + CONV_WIDTH
O_CX = O_CC + CONV_WIDTH
O_RQ = O_CX + CONV_WIDTH
O_RK = O_RQ + RET_WIDTH
O_RV = O_RK + RET_WIDTH
O_GF = O_RV + RET_WIDTH
O_GB = O_GF + RET_WIDTH
O_AQ = O_GB + RET_WIDTH
O_AK = O_AQ + ATTN_WIDTH
O_AV = O_AK + KV_WIDTH
IN_WIDTH = O_AV + KV_WIDTH

M_SH1, M_SC1, M_G1, M_SH2, M_SC2, M_G2 = range(6)
M_CTX = 6
M_ROWS = 16

INPROJ_TILE = 1088
FINAL_TILE = 256


def _params(*sem, vmem=VMEM_LIMIT_BYTES):
    return pltpu.CompilerParams(dimension_semantics=sem, vmem_limit_bytes=vmem)


def _dot(a, b):
    return jnp.dot(a, b, preferred_element_type=F32)


def _dot_nt(a, b):
    return lax.dot_general(a, b, (((1,), (1,)), ((), ())), preferred_element_type=F32)


def _silu(v):
    return v * jax.nn.sigmoid(v)


def _mod_kernel(c_ref, w_ref, b_ref, o_ref):
    s = _silu(c_ref[...])
    s_hi = s.astype(BF16)
    s_lo = (s - s_hi.astype(F32)).astype(BF16)
    w = w_ref[0]
    w_hi = w.astype(BF16)
    w_lo = (w - w_hi.astype(F32)).astype(BF16)
    n = s.shape[0]
    head = _dot(jnp.concatenate([s_hi, s_lo], axis=0), w_hi)
    o_ref[0] = head[:n] + (head[n:] + _dot(s_hi, w_lo)) + b_ref[0]


def _mod_vectors(c_rows, w_mod, b_mod):
    depth, d_model, width = w_mod.shape
    tn = 1536
    return pl.pallas_call(
        _mod_kernel,
        grid=(depth, width // tn),
        in_specs=[
            pl.BlockSpec((SUBLANES, d_model), lambda l, n: (0, 0)),
            pl.BlockSpec((1, d_model, tn), lambda l, n: (l, 0, n)),
            pl.BlockSpec((1, 1, tn), lambda l, n: (l, 0, n)),
        ],
        out_specs=pl.BlockSpec((1, SUBLANES, tn), lambda l, n: (l, 0, n)),
        out_shape=jax.ShapeDtypeStruct((depth, SUBLANES, width), F32),
        compiler_params=_params("parallel", "parallel"),
        name="mod_vectors",
    )(c_rows, w_mod, b_mod.reshape(depth, 1, width))


def _log_sigmoid(v):
    return -jnp.log(1.0 + jnp.exp(-v))


def _tables_kernel(lgl_ref, lgr_ref, qdec_ref, win_ref, cd_ref, dm_ref):
    pos = lax.broadcasted_iota(I32, (CHUNK, RET_WIDTH), 0).astype(F32)
    ri = lax.broadcasted_iota(I32, (CHUNK, RET_HEADS * CHUNK), 0).astype(F32)
    rj = (lax.broadcasted_iota(I32, (CHUNK, RET_HEADS * CHUNK), 1) & (CHUNK - 1)).astype(F32)
    for d in range(2):
        lg = _log_sigmoid(lgl_ref[0, d])
        lg1 = lg[0:1, :]
        if d == 0:
            qdec_ref[0, d] = jnp.exp(lg1 * (pos + 1.0))
            win_ref[0, d] = jnp.exp(lg1 * (CHUNK - 1.0 - pos))
            diff = ri - rj
        else:
            qdec_ref[0, d] = jnp.exp(lg1 * (CHUNK - pos))
            win_ref[0, d] = jnp.exp(lg1 * pos)
            diff = rj - ri
        cd_ref[0, d] = jnp.exp(lg * float(CHUNK))
        lr = _log_sigmoid(lgr_ref[0, d])
        dm_ref[0, d] = jnp.where(diff >= 0.0, jnp.exp(lr * jnp.maximum(diff, 0.0)), 0.0)


def _decay_tables(ret_decay_logit):
    depth = ret_decay_logit.shape[0]
    lg = ret_decay_logit.astype(F32)
    lgl = jnp.broadcast_to(jnp.repeat(lg, HEAD_DIM, axis=-1)[:, :, None, :], (depth, 2, SUBLANES, RET_WIDTH))
    lgr = jnp.broadcast_to(jnp.repeat(lg, CHUNK, axis=-1)[:, :, None, :], (depth, 2, CHUNK, RET_HEADS * CHUNK))
    return pl.pallas_call(
        _tables_kernel,
        grid=(depth,),
        in_specs=[
            pl.BlockSpec((1, 2, SUBLANES, RET_WIDTH), lambda l: (l, 0, 0, 0)),
            pl.BlockSpec((1, 2, CHUNK, RET_HEADS * CHUNK), lambda l: (l, 0, 0, 0)),
        ],
        out_specs=[
            pl.BlockSpec((1, 2, CHUNK, RET_WIDTH), lambda l: (l, 0, 0, 0)),
            pl.BlockSpec((1, 2, CHUNK, RET_WIDTH), lambda l: (l, 0, 0, 0)),
            pl.BlockSpec((1, 2, SUBLANES, RET_WIDTH), lambda l: (l, 0, 0, 0)),
            pl.BlockSpec((1, 2, CHUNK, RET_HEADS * CHUNK), lambda l: (l, 0, 0, 0)),
        ],
        out_shape=[
            jax.ShapeDtypeStruct((depth, 2, CHUNK, RET_WIDTH), F32),
            jax.ShapeDtypeStruct((depth, 2, CHUNK, RET_WIDTH), F32),
            jax.ShapeDtypeStruct((depth, 2, SUBLANES, RET_WIDTH), F32),
            jax.ShapeDtypeStruct((depth, 2, CHUNK, RET_HEADS * CHUNK), F32),
        ],
        compiler_params=_params("parallel"),
        name="decay_tables",
    )(lgl, lgr)


def _rope(v, cs, sn):
    lane = lax.broadcasted_iota(I32, (1, LANES), 1)
    first = (lane & 31) < 16
    outs = []
    for g in range(v.shape[1] // LANES):
        vg = v[:, g * LANES:(g + 1) * LANES]
        sw = jnp.where(first, pltpu.roll(vg, LANES - 16, 1), pltpu.roll(vg, 16, 1))
        outs.append(vg * cs + sw * sn)
    return outs[0] if len(outs) == 1 else jnp.concatenate(outs, axis=1)


def _swap_halves(v):
    return jnp.concatenate([v[:, HEAD_DIM:], v[:, :HEAD_DIM]], axis=1)


def _from_gather_layout(f_ref, tm):
    return jnp.concatenate(
        [f_ref[0, pl.ds(q, tm, stride=SUBLANES), :] for q in range(SUBLANES)], axis=1)


def _rms(v):
    return v * lax.rsqrt(jnp.mean(v * v, axis=-1, keepdims=True) + EPS)


def _mod_rows(mod_ref, tm, n_ctx, tile):
    m = mod_ref[0]
    if n_ctx % tm == 0:
        is_ctx = tile < n_ctx // tm
    else:
        is_ctx = tile * tm + lax.broadcasted_iota(I32, (tm, 1), 0) < n_ctx

    def row(r):
        return jnp.where(is_ctx, m[M_CTX + r:M_CTX + r + 1, :], m[r:r + 1, :])

    return row


def _inproj_kernel(has_ffn, n_ctx, *refs):
    if has_ffn:
        x_ref, f_ref, mod_ref, g_ref, w_ref, cs_ref, sn_ref = refs[:7]
        outs = refs[7:]
        xo_ref = outs[0]
        outs = outs[1:]
    else:
        x_ref, mod_ref, g_ref, w_ref, cs_ref, sn_ref = refs[:6]
        outs = refs[6:]
    cz_ref, rqkv_ref, gates_ref, aq_ref, kv_ref = outs
    tm = x_ref.shape[1]
    row = _mod_rows(mod_ref, tm, n_ctx, pl.program_id(1))

    x = x_ref[0]
    if has_ffn:
        x = x + row(M_G2) * _from_gather_layout(f_ref, tm)
        xo_ref[0] = x
    h = (_rms(x) * g_ref[...]) * (1.0 + row(M_SC1)) + row(M_SH1)
    h = h.astype(BF16)
    cs = cs_ref[...]
    sn = sn_ref[...]

    def proj(a, b):
        return _dot(h, w_ref[:, a:b])

    bounds = (O_CB, O_CC, O_CX, O_RQ, O_RK, O_RV, O_GF, O_GB, O_AQ, O_AK, O_AV, IN_WIDTH)
    p = [proj(a, b) for a, b in zip(bounds[:-1], bounds[1:])]
    cz_ref[0, :, 0:CONV_WIDTH] = p[0]
    cz_ref[0, :, CONV_WIDTH:] = p[1] * p[2]
    rqkv_ref[0, :, 0:RET_WIDTH] = _rope(p[3], cs, sn).astype(BF16)
    rqkv_ref[0, :, RET_WIDTH:2 * RET_WIDTH] = _rope(p[4] * QK_SCALE, cs, sn).astype(BF16)
    rqkv_ref[0, :, 2 * RET_WIDTH:] = p[5].astype(BF16)
    gates_ref[0, :, 0:RET_WIDTH] = _silu(p[6])
    gates_ref[0, :, RET_WIDTH:] = _silu(p[7])
    aq_ref[0] = (_rope(p[8], cs, sn) * (QK_SCALE * LOG2E)).astype(BF16)
    ak = _rope(p[9], cs, sn)
    av = p[10]
    kv_ref[0, :, 0:KV_WIDTH] = ak.astype(BF16)
    kv_ref[0, :, KV_WIDTH:2 * KV_WIDTH] = _swap_halves(ak).astype(BF16)
    kv_ref[0, :, 2 * KV_WIDTH:] = av.astype(BF16)


def _inproj(x, ffn, mrows, norm_g, w_in_bf, layer, rope_cs, rope_sn, n_ctx):
    b, tu, d = x.shape
    tm = INPROJ_TILE
    has_ffn = ffn is not None
    tok = lambda width: pl.BlockSpec((1, tm, width), lambda i, j: (i, j, 0))
    in_specs = [tok(d)]
    args = [x]
    if has_ffn:
        in_specs.append(pl.BlockSpec((1, tm * SUBLANES, LANES), lambda i, j: (i, j, 0)))
        args.append(ffn)
    in_specs += [
        pl.BlockSpec((1, M_ROWS, d), lambda i, j: (i, 0, 0)),
        pl.BlockSpec((1, d), lambda i, j: (0, 0)),
        pl.BlockSpec((None, d, IN_WIDTH), lambda i, j: (layer, 0, 0), pipeline_mode=pl.Buffered(1)),
        pl.BlockSpec((tm, LANES), lambda i, j: (j, 0)),
        pl.BlockSpec((tm, LANES), lambda i, j: (j, 0)),
    ]
    args += [mrows, norm_g.reshape(1, d), w_in_bf, rope_cs, rope_sn]
    widths = [(2 * CONV_WIDTH, F32), (3 * RET_WIDTH, BF16), (2 * RET_WIDTH, F32), (ATTN_WIDTH, BF16),
              (3 * KV_WIDTH, BF16)]
    out_specs = [tok(w) for w, _ in widths]
    out_shape = [jax.ShapeDtypeStruct((b, tu, w), dt) for w, dt in widths]
    if has_ffn:
        out_specs = [tok(d)] + out_specs
        out_shape = [jax.ShapeDtypeStruct((b, tu, d), F32)] + out_shape
    res = pl.pallas_call(
        functools.partial(_inproj_kernel, has_ffn, n_ctx),
        grid=(b, tu // tm),
        in_specs=in_specs,
        out_specs=out_specs,
        out_shape=out_shape,
        compiler_params=_params("parallel", "parallel"),
        name="inproj",
    )(*args)
    if has_ffn:
        return res[0], res[1:]
    return x, res


def _head_block_mask(n):
    r = lax.broadcasted_iota(I32, (n, n), 0) // HEAD_DIM
    c = lax.broadcasted_iota(I32, (n, n), 1) // HEAD_DIM
    return r == c


STATE_UNROLL = 2


def _states_kernel(n_ctx_chunks, n_chunks, rqkv_ref, win_ref, cd_ref, sp_ref, s_scr):
    same_head = _head_block_mask(RET_WIDTH)
    s_scr[...] = jnp.zeros_like(s_scr)

    def chunk_update(d, pos):
        if d == 0:
            c = pos
        else:
            c = jnp.where(pos < n_ctx_chunks, n_ctx_chunks - 1 - pos, n_chunks + n_ctx_chunks - 1 - pos)
        off = pl.multiple_of(c * CHUNK, CHUNK)
        kw = rqkv_ref[0, pl.ds(off, CHUNK), RET_WIDTH:2 * RET_WIDTH].astype(F32) * win_ref[d]
        v = rqkv_ref[0, pl.ds(off, CHUNK), 2 * RET_WIDTH:]
        kw_t = kw.T.astype(BF16)
        yield
        u = _dot(kw_t, v)
        yield
        return c, jnp.where(same_head, u, 0.0)

    def body(i, carry):
        todo = [(d, i * STATE_UNROLL + n) for n in range(STATE_UNROLL) for d in range(2)]
        done = _round_robin([chunk_update(d, pos) for d, pos in todo])
        for (d, _), (c, u) in zip(todo, done):
            s = s_scr[d]
            sp_ref[0, d, pl.ds(c, 1)] = s.astype(BF16)[None]
            s_scr[d] = s * cd_ref[d][0:1, :] + u
        return carry

    assert n_chunks % STATE_UNROLL == 0
    lax.fori_loop(0, n_chunks // STATE_UNROLL, body, 0)


def _ret_states(rqkv, win, cd, n_ctx):
    b, tu, _ = rqkv.shape
    w = RET_WIDTH
    n_chunks = tu // CHUNK
    return pl.pallas_call(
        functools.partial(_states_kernel, n_ctx // CHUNK, n_chunks),
        grid=(b,),
        in_specs=[
            pl.BlockSpec((1, tu, 3 * w), lambda i: (i, 0, 0)),
            pl.BlockSpec((2, CHUNK, w), lambda i: (0, 0, 0)),
            pl.BlockSpec((2, SUBLANES, w), lambda i: (0, 0, 0)),
        ],
        out_specs=pl.BlockSpec((1, 2, n_chunks, w, w), lambda i: (i, 0, 0, 0, 0)),
        out_shape=jax.ShapeDtypeStruct((b, 2, n_chunks, w, w), BF16),
        scratch_shapes=[pltpu.VMEM((2, w, w), F32)],
        compiler_params=_params("parallel"),
        name="ret_states",
    )(rqkv, win, cd)


def _group_mean(t, ones_bd):
    hi = t.astype(BF16)
    lo = (t - hi.astype(F32)).astype(BF16)
    s = _dot(jnp.concatenate([hi, lo], axis=0), ones_bd)
    n = t.shape[0]
    return (s[:n] + s[n:]) * (1.0 / HEAD_DIM)


def _mix_chunk(c, n_ctx_chunks, n_chunks, cz, z_last, z_first, cw, rqkv, gates, sp, dm_ref, qdec_ref,
               qa, kvs, sink_ref):
    is_lat = c >= n_ctx_chunks

    z = cz[:, CONV_WIDTH:]
    row = lax.broadcasted_iota(I32, (CHUNK, 1), 0)
    has_prev = jnp.logical_and(c != 0, c != n_ctx_chunks)
    has_next = jnp.logical_and(c != n_ctx_chunks - 1, c != n_chunks - 1)
    z_before = jnp.where(row == 0, jnp.where(has_prev, z_last, 0.0), pltpu.roll(z, 1, 0))
    z_after = jnp.where(row == CHUNK - 1, jnp.where(has_next, z_first, 0.0), pltpu.roll(z, CHUNK - 1, 0))
    conv = cz[:, :CONV_WIDTH] * (z_before * cw[0:1, :] + z * cw[1:2, :] + z_after * cw[2:3, :])

    q = rqkv[:, 0:RET_WIDTH]
    k = rqkv[:, RET_WIDTH:2 * RET_WIDTH]
    v = rqkv[:, 2 * RET_WIDTH:]
    lane_head = lax.broadcasted_iota(I32, (1, RET_WIDTH), 1) // HEAD_DIM
    kz = jnp.zeros_like(k)
    k_heads = jnp.concatenate([jnp.where(lane_head == hh, k, kz) for hh in range(RET_HEADS)], axis=0)
    v_heads = jnp.concatenate([jnp.where(lane_head == hh, v, kz) for hh in range(RET_HEADS)], axis=0)
    scores = _dot_nt(q, k_heads)

    keys, v_t = kvs
    half = lax.broadcasted_iota(I32, (1, LANES), 1) // HEAD_DIM
    qa_z = jnp.zeros((CHUNK, LANES), BF16)
    combos = [(hk, par) for hk in range(ATTN_KV_HEADS) for par in range(2)]
    n_ctx = keys[0].shape[0] - 3 * CHUNK
    cols2 = 2 * CHUNK
    ik = lax.broadcasted_iota(I32, (CHUNK, cols2), 0)
    iq = lax.broadcasted_iota(I32, (CHUNK, cols2), 1) & (CHUNK - 1)
    off = jnp.full((CHUNK, cols2), NEG_INF, F32)
    bias_prev = jnp.where(jnp.logical_and(jnp.logical_and(is_lat, c - 1 >= n_ctx_chunks), ik >= iq), 0.0, off)
    bias_cur = jnp.where(is_lat, 0.0, off)
    bias_next = jnp.where(jnp.logical_and(jnp.logical_and(is_lat, c + 1 <= n_chunks - 1), ik <= iq), 0.0, off)
    assert n_ctx > 0
    logits = {}
    for hk, par in combos:
        ja, jb = 2 * hk, 2 * hk + 1
        qst = jnp.concatenate([
            jnp.where(half == par, qa[:, ja * LANES:(ja + 1) * LANES], qa_z),
            jnp.where(half == par, qa[:, jb * LANES:(jb + 1) * LANES], qa_z)], axis=0)
        sel = 0 if par == hk else 1
        r = _dot_nt(keys[sel], qst)
        s = jnp.concatenate([r[:CHUNK] + bias_prev, r[CHUNK:2 * CHUNK] + bias_cur,
                             r[2 * CHUNK:3 * CHUNK] + bias_next, r[3 * CHUNK:]], axis=0)
        ha, hb = ATTN_GROUP * hk + par, ATTN_GROUP * hk + par + 2
        snk = jnp.concatenate([sink_ref[ha * CHUNK:ha * CHUNK + 1, :],
                               sink_ref[hb * CHUNK:hb * CHUNK + 1, :]], axis=1) * LOG2E
        logits[hk, par] = (s, snk, jnp.maximum(jnp.max(s, axis=0, keepdims=True), snk))
    yield

    qf = q.astype(F32)
    outs = []
    for d in range(2):
        p = (scores * dm_ref[d]).astype(BF16)
        lhs = jnp.concatenate([p, (qf * qdec_ref[d]).astype(BF16)], axis=1)
        o = _dot(lhs, jnp.concatenate([v_heads, sp[d]], axis=0))
        outs.append(o)
    o2 = jnp.concatenate(outs, axis=0)
    yield

    att = {}
    ones_keys = jnp.ones((BF16_ROWS, keys[0].shape[0]), BF16)
    for n, (hk, par) in enumerate(combos):
        s, snk, mx = logits[hk, par]
        e = jnp.exp2(s - mx).astype(BF16)
        o_t = _dot(jnp.concatenate([v_t[hk * HEAD_DIM:(hk + 1) * HEAD_DIM, :], ones_keys], axis=0), e)
        den = o_t[HEAD_DIM:HEAD_DIM + 1, :] + jnp.exp2(snk - mx)
        att[hk, par] = o_t[:HEAD_DIM, :] * (1.0 / den)
        if n == 0:
            ones_bd = jnp.where(_head_block_mask(RET_WIDTH), 1.0, 0.0).astype(BF16)
            dl = o2 - _group_mean(o2, ones_bd)
            on = dl * lax.rsqrt(_group_mean(dl * dl, ones_bd) + EPS)
            ret = on[:CHUNK] * gates[:, 0:RET_WIDTH] + on[CHUNK:] * gates[:, RET_WIDTH:]
        yield

    cols = []
    for hk in range(ATTN_KV_HEADS):
        cols.append(jnp.concatenate([att[hk, 0][:, :CHUNK], att[hk, 1][:, :CHUNK]], axis=0).T)
        cols.append(jnp.concatenate([att[hk, 0][:, CHUNK:], att[hk, 1][:, CHUNK:]], axis=0).T)
    return jnp.concatenate([conv, ret] + cols, axis=1)


def _round_robin(gens):
    results = [None] * len(gens)
    active = list(range(len(gens)))
    while active:
        for i in list(active):
            try:
                next(gens[i])
            except StopIteration as done:
                results[i] = done.value
                active.remove(i)
    return results


MIX_CHUNKS = 2


def _mixer_kernel(n_ctx_chunks, n_chunks, n_exp,
                  cz_ref, czp_ref, czn_ref, cw_ref, rqkv_ref, gates_ref, sp_ref, dm_ref, qdec_ref,
                  aq_ref, kvp_ref, kvc_ref, kvn_ref, kvx_ref, sink_ref,
                  x_ref, mod_ref, g_ref, w_ref, wrh_ref, wrl_ref, xo_ref, hg_ref, aff_ref, mix_scr):
    t = pl.program_id(1)
    n_steps = pl.num_programs(1) - 1

    @pl.when(t == 0)
    def _():
        mix_scr[...] = jnp.zeros_like(mix_scr)

    tail = _outproj_tail(n_ctx_chunks * CHUNK, n_exp, jnp.maximum(t - 1, 0), mix_scr[...], x_ref, mod_ref, g_ref,
                         w_ref, wrh_ref, wrl_ref, xo_ref, hg_ref, aff_ref)
    j = jnp.minimum(t, n_steps - 1)
    cw = cw_ref[...]

    def values_t(blk):
        return blk[:, 2 * KV_WIDTH:].astype(F32).T.astype(BF16)

    kv_blocks = ([kvp_ref[0]] + [kvc_ref[0, s * CHUNK:(s + 1) * CHUNK, :] for s in range(MIX_CHUNKS)] + [kvn_ref[0]])
    kv_ctx = [kvx_ref[0, t * CHUNK:(t + 1) * CHUNK, :] for t in range(kvx_ref.shape[1] // CHUNK)]
    vt_blocks = [values_t(blk) for blk in kv_blocks]
    vt_ctx = [values_t(blk) for blk in kv_ctx]
    gens = []
    for s in range(MIX_CHUNKS):
        lo, hi = s * CHUNK, (s + 1) * CHUNK
        if s == 0:
            z_last = czp_ref[0, SUBLANES - 1:SUBLANES, CONV_WIDTH:]
        else:
            z_last = cz_ref[0, lo - 1:lo, CONV_WIDTH:]
        if s == MIX_CHUNKS - 1:
            z_first = czn_ref[0, 0:1, CONV_WIDTH:]
        else:
            z_first = cz_ref[0, hi:hi + 1, CONV_WIDTH:]
        window = kv_blocks[s:s + 3] + kv_ctx
        keys = tuple(jnp.concatenate([blk[:, i * KV_WIDTH:(i + 1) * KV_WIDTH] for blk in window], axis=0)
                     for i in (0, 1))
        v_t = jnp.concatenate(vt_blocks[s:s + 3] + vt_ctx, axis=1)
        gens.append(_mix_chunk(j * MIX_CHUNKS + s, n_ctx_chunks, n_chunks, cz_ref[0, lo:hi, :], z_last, z_first,
                               cw, rqkv_ref[0, lo:hi, :], gates_ref[0, lo:hi, :],
                               (sp_ref[0, 0, s], sp_ref[0, 1, s]), dm_ref, qdec_ref, aq_ref[0, lo:hi, :],
                               (keys, v_t), sink_ref))
    mixes = [mix.astype(BF16) for mix in _round_robin([tail] + gens)[1:]]
    mix_scr[...] = jnp.concatenate(mixes, axis=0)


def _mixer(proj, sp, tabs, conv_w, sink_rows, x, mrows, norm_g, w_out_bf, layer, w_router, n_ctx):
    cz, rqkv, gates, aq, kv = proj
    dm, qdec = tabs
    b, tu, _ = cz.shape
    d = x.shape[2]
    n_exp = w_router.shape[1]
    wr = w_router.T
    wr_hi = wr.astype(BF16)
    wr_lo = (wr - wr_hi.astype(F32)).astype(BF16)
    n_chunks = tu // CHUNK
    mc = MIX_CHUNKS
    rows = mc * CHUNK
    assert n_chunks % mc == 0 and (n_ctx // CHUNK) % mc == 0
    last = n_chunks - 1
    n_steps = n_chunks // mc
    per8 = rows // SUBLANES
    mixed = lambda t: jnp.minimum(t, n_steps - 1)
    projected = lambda t: jnp.maximum(t - 1, 0)
    cur = lambda w: pl.BlockSpec((1, rows, w), lambda i, t: (i, mixed(t), 0))
    out = lambda r, w: pl.BlockSpec((1, r, w), lambda i, t: (i, projected(t), 0))
    full = lambda shape: pl.BlockSpec(shape, lambda i, t: (0,) * len(shape))
    wcz, wkv = cz.shape[2], kv.shape[2]
    in_specs = [
        cur(wcz),
        pl.BlockSpec((1, SUBLANES, wcz), lambda i, t: (i, jnp.maximum(mixed(t) * per8 - 1, 0), 0)),
        pl.BlockSpec((1, SUBLANES, wcz),
                     lambda i, t: (i, jnp.minimum((mixed(t) + 1) * per8, tu // SUBLANES - 1), 0)),
        full(conv_w.shape),
        cur(rqkv.shape[2]), cur(gates.shape[2]),
        pl.BlockSpec((1, 2, mc, RET_WIDTH, RET_WIDTH), lambda i, t: (i, 0, mixed(t), 0, 0)),
        full(dm.shape), full(qdec.shape),
        cur(ATTN_WIDTH),
        pl.BlockSpec((1, CHUNK, wkv), lambda i, t: (i, jnp.maximum(mixed(t) * mc - 1, 0), 0)),
        cur(wkv),
        pl.BlockSpec((1, CHUNK, wkv), lambda i, t: (i, jnp.minimum((mixed(t) + 1) * mc, last), 0)),
        pl.BlockSpec((1, n_ctx, wkv), lambda i, t: (i, 0, 0)),
        full(sink_rows.shape),
        out(rows, d),
        pl.BlockSpec((1, M_ROWS, d), lambda i, t: (i, 0, 0)),
        full((1, d)),
        pl.BlockSpec((None,) + w_out_bf.shape[1:], lambda i, t: (layer, 0, 0)),
        full((n_exp, d)), full((n_exp, d)),
    ]
    args = [cz, cz, cz, conv_w, rqkv, gates, sp, dm, qdec, aq, kv, kv, kv, kv, sink_rows,
            x, mrows, norm_g.reshape(1, d), w_out_bf, wr_hi, wr_lo]
    return pl.pallas_call(
        functools.partial(_mixer_kernel, n_ctx // CHUNK, n_chunks, n_exp),
        grid=(b, n_steps + 1),
        in_specs=in_specs,
        out_specs=[out(rows, d), out(rows * SUBLANES, LANES),
                   pl.BlockSpec((1, n_exp, rows), lambda i, t: (i, 0, projected(t)))],
        out_shape=[
            jax.ShapeDtypeStruct((b, tu, d), F32),
            jax.ShapeDtypeStruct((b, tu * SUBLANES, LANES), F32),
            jax.ShapeDtypeStruct((b, n_exp, tu), F32),
        ],
        scratch_shapes=[pltpu.VMEM((rows, CONV_WIDTH + RET_WIDTH + ATTN_WIDTH), BF16)],
        compiler_params=_params("parallel", "arbitrary"),
        name="mixer",
    )(*args)


def _outproj_tail(n_ctx, n_exp, tile, mix, x_ref, mod_ref, g_ref, w_ref, wrh_ref, wrl_ref, xo_ref, hg_ref, aff_ref):
    tm = x_ref.shape[1]
    row = _mod_rows(mod_ref, tm, n_ctx, tile)

    y = _dot(mix, w_ref[...])
    yield
    x = x_ref[0] + row(M_G1) * y
    xo_ref[0] = x
    h = (_rms(x) * g_ref[...]) * (1.0 + row(M_SC2)) + row(M_SH2)
    for q in range(SUBLANES):
        hg_ref[0, pl.ds(q, tm, stride=SUBLANES), :] = h[:, q * LANES:(q + 1) * LANES]
    yield
    h_hi = h.astype(BF16)
    h_lo = (h - h_hi.astype(F32)).astype(BF16)
    logits = (_dot_nt(wrh_ref[...], h_hi) + (_dot_nt(wrh_ref[...], h_lo) + _dot_nt(wrl_ref[...], h_hi)))
    yield
    e = jnp.exp(logits - jnp.max(logits, axis=0, keepdims=True))
    aff_ref[0] = e / jnp.sum(e, axis=0, keepdims=True)


MIN_NORMAL_F32_BITS = 0x00800000
ROUTE_TABLES = 4
ROUTE_GROUP = 4


def _route_stream_tables(v, k, tab_scr, ce_scr, cnt_scr, stream, thr):
    n_exp, n = v.shape
    lane = lax.broadcasted_iota(I32, (1, LANES), 1)
    tri_r = lax.broadcasted_iota(I32, (LANES, LANES), 0)
    tri_c = lax.broadcasted_iota(I32, (LANES, LANES), 1)
    upper = jnp.where(tri_r <= tri_c, 1.0, 0.0).astype(BF16)
    above = pltpu.bitcast(jnp.maximum(thr + 1, MIN_NORMAL_F32_BITS), F32)
    floor = pltpu.bitcast(thr, F32)
    n_gt = jnp.sum(jnp.where(v >= above, 1, 0), axis=1, keepdims=True)
    need = (k - n_gt).astype(F32)
    seen_eq = jnp.zeros((n_exp, 1), F32)
    counts = jnp.zeros((n_exp, LANES), F32)
    for j in range(n // LANES):
        vb = v[:, j * LANES:(j + 1) * LANES]
        gt = vb >= above
        eq = jnp.logical_and(vb >= floor, vb < above)
        both = jnp.concatenate([jnp.where(gt, 1.0, 0.0), jnp.where(eq, 1.0, 0.0)], axis=0).astype(BF16)
        pref = _dot(both, upper)
        rank_eq = pref[n_exp:] + seen_eq
        inc = pref[:n_exp] + jnp.minimum(rank_eq, need) - jnp.minimum(seen_eq, need)
        sel = jnp.logical_or(gt, jnp.logical_and(eq, rank_eq <= need))
        seen_eq = rank_eq[:, LANES - 1:LANES]
        counts = counts + jnp.where(lane == j, inc[:, LANES - 1:LANES], 0.0)
        a0 = jnp.where(sel, vb, 0.0)
        t0 = a0.astype(BF16).astype(F32)
        t1 = (a0 - t0).astype(BF16).astype(F32)
        t2 = a0 - t0 - t1
        for t, val in enumerate((inc, t0, t1, t2)):
            tab_scr[stream * ROUTE_TABLES + t, pl.ds(j, n_exp, stride=LANES), :] = val
    through = _dot(counts.astype(BF16), upper)
    for e in range(n_exp):
        ce_scr[stream, e] = through[e:e + 1, :]
        cnt_scr[stream, e] = counts[e:e + 1, :]


def _route_slots(e, stream, k, lo, tab_scr, ce_scr, cnt_scr):
    lane_f = lax.broadcasted_iota(I32, (1, LANES), 1).astype(F32)
    slot = lax.broadcasted_iota(I32, (k, LANES), 0).astype(F32)
    ones_rows = jnp.ones((LANES, LANES), BF16)
    cnt_rows = jnp.broadcast_to(cnt_scr[stream, e], (LANES, LANES)).astype(BF16)
    rows_e = pl.ds(pl.multiple_of(e * LANES, LANES), LANES)
    tab = jnp.concatenate([tab_scr[stream * ROUTE_TABLES + t, rows_e, :] for t in range(ROUTE_TABLES)],
                          axis=1).astype(BF16)
    before = jnp.where(ce_scr[stream, e] <= slot, 1.0, 0.0).astype(BF16)
    blk = _dot_nt(before, ones_rows)
    base = _dot_nt(before, cnt_rows)
    yield
    row = _dot(jnp.where(lane_f == blk, 1.0, 0.0).astype(BF16), tab)
    yield
    inc = row[:, :LANES]
    aff = row[:, LANES:2 * LANES] + row[:, 2 * LANES:3 * LANES] + row[:, 3 * LANES:]
    local = slot - base
    pos = _dot_nt(jnp.where(inc <= local, 1.0, 0.0).astype(BF16), ones_rows)
    yield
    tok = blk * LANES + pos + float(lo)
    gate = jnp.sum(jnp.where(inc == local + 1.0, aff, 0.0), axis=1, keepdims=True)
    return tok, gate


def _route_kernel(n_exp, streams, aff_ref, idx_ref, gate_ref, tab_scr, ce_scr, cnt_scr):
    a = aff_ref[0]
    tab_scr[...] = jnp.zeros_like(tab_scr)
    vs = [a[:, lo:lo + n] for lo, n, _, _ in streams]

    def search(i, ts):
        out = []
        for t, v, (_, _, k, _) in zip(ts, vs, streams):
            cand = t | jnp.left_shift(jnp.int32(1), 30 - i)
            cnt = jnp.sum(jnp.where(v >= pltpu.bitcast(cand, F32), 1, 0), axis=1, keepdims=True)
            out.append(jnp.where(cnt >= k, cand, t))
        return tuple(out)

    thrs = lax.fori_loop(0, 31, search, tuple(jnp.zeros((n_exp, 1), I32) for _ in streams))
    for st, (v, (_, _, k, _)) in enumerate(zip(vs, streams)):
        _route_stream_tables(v, k, tab_scr, ce_scr, cnt_scr, st, thrs[st])

    slots = idx_ref.shape[3]

    assert [s[3] for s in streams] == [sum(s[2] for s in streams[:n]) for n in range(len(streams))]

    def expert_group(g, carry):
        experts = [g * ROUTE_GROUP + u for u in range(ROUTE_GROUP)]
        found = _round_robin([_route_slots(e, st, k, lo, tab_scr, ce_scr, cnt_scr)
                              for e in experts for st, (lo, _, k, _) in enumerate(streams)])
        for u, e in enumerate(experts):
            toks = []
            for st, (_, _, k, slot0) in enumerate(streams):
                tok, gate = found[u * len(streams) + st]
                toks.append(tok)
                gate_ref[0, pl.ds(e, 1), slot0:slot0 + k, :] = gate[None]
            toks.append(jnp.zeros((-slots % LANES, LANES), F32))
            row = jnp.concatenate(toks, axis=0).T[0:1, :slots]
            idx_ref[0, pl.ds(e, 1)] = row.astype(I32)[None]
        return carry

    assert n_exp % ROUTE_GROUP == 0
    lax.fori_loop(0, n_exp // ROUTE_GROUP, expert_group, 0)


def _route(aff, n_ctx):
    b, n_exp, tu = aff.shape
    n_lat = tu - n_ctx
    cap_l = CAPACITY_FACTOR * n_lat // n_exp
    cap_c = CAPACITY_FACTOR * n_ctx // n_exp
    slots = cap_l + cap_c
    assert n_lat // LANES <= LANES and n_ctx // LANES <= LANES
    streams = ((n_ctx, n_lat, cap_l, 0), (0, n_ctx, cap_c, cap_l))
    return pl.pallas_call(
        functools.partial(_route_kernel, n_exp, streams),
        grid=(b,),
        in_specs=[pl.BlockSpec((1, n_exp, tu), lambda i: (i, 0, 0))],
        out_specs=[pl.BlockSpec((1, n_exp, 1, slots), lambda i: (i, 0, 0, 0)),
                   pl.BlockSpec((1, n_exp, slots, 1), lambda i: (i, 0, 0, 0))],
        out_shape=[jax.ShapeDtypeStruct((b, n_exp, 1, slots), I32),
                   jax.ShapeDtypeStruct((b, n_exp, slots, 1), F32)],
        scratch_shapes=[pltpu.VMEM((len(streams) * ROUTE_TABLES, n_exp * LANES, LANES), F32),
                        pltpu.VMEM((len(streams), n_exp, 1, LANES), F32),
                        pltpu.VMEM((len(streams), n_exp, 1, LANES), F32)],
        compiler_params=_params("parallel"),
        name="route",
    )(aff)


def _slot_pitch(slots):
    return slots + SUBLANES


def _zero_slot_padding(ref, lead, slots, pitch):
    for q in range(SUBLANES):
        ref[lead + (pl.ds(q * pitch + slots, pitch - slots), slice(None))] = jnp.zeros((pitch - slots, LANES), ref.dtype)


def _slot_pitch_bf16(slots):
    return -(-slots // BF16_ROWS) * BF16_ROWS + BF16_ROWS


DISPATCH_EXPERTS = 4


def _gather_kernel(slots, pitch, pitch_out, idx_ref, h_ref, o_ref, tile_scr):
    for g in range(DISPATCH_EXPERTS):
        for mi in range(slots):
            r = pl.multiple_of(idx_ref[g, 0, mi] * SUBLANES, SUBLANES)
            tile_scr[g, pl.ds(mi, SUBLANES, stride=pitch), :] = h_ref[0, pl.ds(r, SUBLANES), :]
        _zero_slot_padding(o_ref, (0, g), slots, pitch_out)
        for q in range(SUBLANES):
            o_ref[0, g, q * pitch_out:q * pitch_out + slots, :] = (
                tile_scr[g, q * pitch:q * pitch + slots, :].astype(BF16))


def _gather(hg, idx_rows, n_exp, slots):
    b = hg.shape[0]
    pitch = _slot_pitch(slots)
    pitch_out = _slot_pitch_bf16(slots)
    ge = DISPATCH_EXPERTS
    assert n_exp % ge == 0
    return pl.pallas_call(
        functools.partial(_gather_kernel, slots, pitch, pitch_out),
        grid=(b, n_exp // ge),
        in_specs=[
            pl.BlockSpec((ge, 1, slots), lambda i, e: (i * (n_exp // ge) + e, 0, 0), memory_space=pltpu.SMEM),
            pl.BlockSpec((1,) + hg.shape[1:], lambda i, e: (i, 0, 0)),
        ],
        out_specs=pl.BlockSpec((1, ge, SUBLANES * pitch_out, LANES), lambda i, e: (i, e, 0, 0)),
        out_shape=jax.ShapeDtypeStruct((b, n_exp, SUBLANES * pitch_out, LANES), BF16),
        scratch_shapes=[pltpu.VMEM((ge, SUBLANES * pitch, LANES), F32)],
        compiler_params=_params("parallel", "arbitrary"),
        name="gather",
    )(idx_rows, hg)


FFN_SAMPLES = 2
FFN_VMEM_LIMIT_BYTES = 60 * 1024 * 1024


def _ffn_kernel(slots, pitch_in, pitch, xs_ref, wg_ref, wu_ref, wd_ref, gate_ref, y_ref):
    n = xs_ref.shape[0]
    x = jnp.concatenate(
        [jnp.concatenate([xs_ref[s, 0, q * pitch_in:q * pitch_in + slots, :] for q in range(SUBLANES)], axis=1)
         for s in range(n)], axis=0)
    gate = jnp.concatenate([gate_ref[s, 0] for s in range(n)], axis=0)
    a = _dot(x, wg_ref[0].astype(BF16))
    u = _dot(x, wu_ref[0].astype(BF16))
    y = _dot((_silu(a) * u).astype(BF16), wd_ref[0].astype(BF16)) * gate
    for s in range(n):
        _zero_slot_padding(y_ref, (s, 0), slots, pitch)
        for q in range(SUBLANES):
            y_ref[s, 0, q * pitch:q * pitch + slots, :] = y[s * slots:(s + 1) * slots, q * LANES:(q + 1) * LANES]


def _ffn(xs, w_gate, w_up, w_down, layer, gate, slots):
    b, n_exp = xs.shape[:2]
    pitch = _slot_pitch(slots)
    pitch_in = _slot_pitch_bf16(slots)
    d, f = w_gate.shape[2:]
    ns = FFN_SAMPLES
    assert b % ns == 0
    slot_tile = lambda p: pl.BlockSpec((ns, 1, SUBLANES * p, LANES), lambda e, i: (i, e, 0, 0))
    weight = lambda rows, cols: pl.BlockSpec((None, 1, rows, cols), lambda e, i: (layer, e, 0, 0))
    return pl.pallas_call(
        functools.partial(_ffn_kernel, slots, pitch_in, pitch),
        grid=(n_exp, b // ns),
        in_specs=[
            slot_tile(pitch_in),
            weight(d, f), weight(d, f), weight(f, d),
            pl.BlockSpec((ns, 1, slots, 1), lambda e, i: (i, e, 0, 0)),
        ],
        out_specs=slot_tile(pitch),
        out_shape=jax.ShapeDtypeStruct((b, n_exp, SUBLANES * pitch, LANES), F32),
        compiler_params=_params("parallel", "arbitrary", vmem=FFN_VMEM_LIMIT_BYTES),
        name="ffn",
    )(xs, w_gate, w_up, w_down, gate)


SCATTER_BATCH = 16


def _scatter_kernel(slots, pitch, idx_ref, y_ref, o_ref):
    @pl.when(pl.program_id(1) == 0)
    def _():
        o_ref[...] = jnp.zeros_like(o_ref)

    for g in range(DISPATCH_EXPERTS):
        for m0 in range(0, slots, SCATTER_BATCH):
            rows = [pl.multiple_of(idx_ref[g, 0, m0 + u] * SUBLANES, SUBLANES) for u in range(SCATTER_BATCH)]
            vals = [o_ref[0, pl.ds(rows[u], SUBLANES), :] + y_ref[0, g, pl.ds(m0 + u, SUBLANES, stride=pitch), :]
                    for u in range(SCATTER_BATCH)]
            for u in range(SCATTER_BATCH):
                o_ref[0, pl.ds(rows[u], SUBLANES), :] = vals[u]


def _scatter(y, idx_rows, tu, slots):
    b, n_exp = y.shape[:2]
    pitch = _slot_pitch(slots)
    ge = DISPATCH_EXPERTS
    assert n_exp % ge == 0 and slots % SCATTER_BATCH == 0
    return pl.pallas_call(
        functools.partial(_scatter_kernel, slots, pitch),
        grid=(b, n_exp // ge),
        in_specs=[
            pl.BlockSpec((ge, 1, slots), lambda i, e: (i * (n_exp // ge) + e, 0, 0), memory_space=pltpu.SMEM),
            pl.BlockSpec((1, ge, SUBLANES * pitch, LANES), lambda i, e: (i, e, 0, 0)),
        ],
        out_specs=pl.BlockSpec((1, tu * SUBLANES, LANES), lambda i, e: (i, 0, 0)),
        out_shape=jax.ShapeDtypeStruct((b, tu * SUBLANES, LANES), F32),
        compiler_params=_params("parallel", "arbitrary"),
        name="scatter",
    )(idx_rows, y)


FINAL_PARTS = 4


def _final_kernel(mod_ref, g_ref, *refs):
    o_ref = refs[-1]
    tm = FINAL_TILE
    for n in range(FINAL_PARTS):
        x_ref, f_ref = refs[n], refs[FINAL_PARTS + n]
        x = x_ref[0] + mod_ref[0][M_G2:M_G2 + 1, :] * _from_gather_layout(f_ref, tm)
        o_ref[0, n * tm:(n + 1) * tm, :] = _rms(x) * g_ref[...]


def _final(x, ffn, mrows, final_g, n_ctx):
    b, tu, d = x.shape
    tm = FINAL_TILE
    skip = n_ctx // tm
    parts = FINAL_PARTS
    assert (tu - n_ctx) % (tm * parts) == 0
    tile = lambda n: pl.BlockSpec((1, tm, d), lambda i, j: (i, j * parts + n + skip, 0))
    ftile = lambda n: pl.BlockSpec((1, tm * SUBLANES, LANES), lambda i, j: (i, j * parts + n + skip, 0))
    return pl.pallas_call(
        _final_kernel,
        grid=(b, (tu - n_ctx) // (tm * parts)),
        in_specs=([pl.BlockSpec((1, M_ROWS, d), lambda i, j: (i, 0, 0)), pl.BlockSpec((1, d), lambda i, j: (0, 0))]
                  + [tile(n) for n in range(parts)] + [ftile(n) for n in range(parts)]),
        out_specs=pl.BlockSpec((1, tm * parts, d), lambda i, j: (i, j, 0)),
        out_shape=jax.ShapeDtypeStruct((b, tu - n_ctx, d), F32),
        compiler_params=_params("parallel", "parallel"),
        name="final_norm",
    )(mrows, final_g.reshape(1, d), *([x] * parts + [ffn] * parts))


def _rope_tables(n_lat, n_ctx):
    rows = n_lat // GRID_W
    rowp = jnp.repeat(jnp.arange(rows, dtype=F32), GRID_W)
    colp = jnp.tile(jnp.arange(GRID_W, dtype=F32), rows)
    axis_dim = HEAD_DIM // 2
    inv_freq = ROPE_BASE ** (-jnp.arange(0, axis_dim, 2, dtype=F32) / axis_dim)
    ar = rowp[:, None] * inv_freq
    ac = colp[:, None] * inv_freq
    cs = jnp.concatenate([jnp.cos(ar), jnp.cos(ar), jnp.cos(ac), jnp.cos(ac)], axis=1)
    sn = jnp.concatenate([-jnp.sin(ar), jnp.sin(ar), -jnp.sin(ac), jnp.sin(ac)], axis=1)
    reps = LANES // HEAD_DIM
    cs = jnp.concatenate([jnp.ones((n_ctx, LANES), F32), jnp.tile(cs, (1, reps))], axis=0)
    sn = jnp.concatenate([jnp.zeros((n_ctx, LANES), F32), jnp.tile(sn, (1, reps))], axis=0)
    return cs, sn


def kernel(x, c, ctx, c_ctx, w_mod, b_mod, norm1_g, norm2_g, w_in, conv_w, ret_decay_logit, attn_sink,
           w_out, w_router, w_gate, w_up, w_down, final_g):
    b, n_lat, d = x.shape
    n_ctx = ctx.shape[1]
    depth = w_in.shape[0]
    n_exp = w_router.shape[2]
    tu = n_ctx + n_lat
    assert w_in.shape[2] == IN_WIDTH and tu % INPROJ_TILE == 0
    assert n_ctx % FINAL_TILE == 0 and n_lat % FINAL_TILE == 0
    assert b + 1 <= SUBLANES and n_exp == N_EXPERTS

    c_rows = jnp.concatenate([c, c_ctx[None], jnp.zeros((SUBLANES - b - 1, d), F32)], axis=0)
    mods = _mod_vectors(c_rows, w_mod, b_mod).reshape(depth, SUBLANES, 6, d)
    qdec, win, cd, dm = _decay_tables(ret_decay_logit)
    rope_cs, rope_sn = _rope_tables(n_lat, n_ctx)
    sink_rows = jnp.broadcast_to(jnp.repeat(attn_sink.astype(F32), CHUNK, axis=1)[:, :, None],
                                 (depth, ATTN_HEADS * CHUNK, LANES))
    w_in_bf = w_in.astype(BF16)
    w_out_bf = w_out.astype(BF16)

    xu = jnp.concatenate([ctx, x], axis=1)
    ffn = None
    mrows = None
    cap = CAPACITY_FACTOR * n_lat // n_exp + CAPACITY_FACTOR * n_ctx // n_exp
    for l in range(depth):
        prev_mrows = mrows
        mrows = jnp.concatenate([mods[l, :b], jnp.broadcast_to(mods[l, b][None], (b, 6, d)),
                                 jnp.zeros((b, M_ROWS - 12, d), F32)], axis=1)
        xu, proj = _inproj(xu, ffn, mrows if ffn is None else _with_prev_g2(mrows, prev_mrows),
                           norm1_g[l], w_in_bf, l, rope_cs, rope_sn, n_ctx)
        sp = _ret_states(proj[1], win[l], cd[l], n_ctx)
        xu, hg, aff = _mixer(proj, sp, (dm[l], qdec[l]), conv_w[l], sink_rows[l],
                             xu, mrows, norm2_g[l], w_out_bf, l, w_router[l], n_ctx)
        idx, gate = _route(aff, n_ctx)
        idx_rows = idx.reshape(b * n_exp, 1, cap)
        xs = _gather(hg, idx_rows, n_exp, cap)
        y = _ffn(xs, w_gate, w_up, w_down, l, gate, cap)
        ffn = _scatter(y, idx_rows, tu, cap)
    return _final(xu, ffn, mrows, final_g, n_ctx)


def _with_prev_g2(mrows, prev_mrows):
    out = mrows.at[:, M_G2].set(prev_mrows[:, M_G2])
    return out.at[:, M_CTX + M_G2].set(prev_mrows[:, M_CTX + M_G2])
```

```python
import functools

import jax
import jax.numpy as jnp
from jax import lax
from jax.experimental import pallas as pl
from jax.experimental.pallas import tpu as pltpu

F32 = jnp.float32
BF16 = jnp.bfloat16
I32 = jnp.int32

HEAD_DIM = 64
CONV_WIDTH = 256
RET_HEADS = 4
RET_WIDTH = RET_HEADS * HEAD_DIM
ATTN_HEADS = 8
ATTN_KV_HEADS = 2
ATTN_GROUP = ATTN_HEADS // ATTN_KV_HEADS
ATTN_WIDTH = ATTN_HEADS * HEAD_DIM
KV_WIDTH = ATTN_KV_HEADS * HEAD_DIM
CHUNK = 128
GRID_W = 64
N_EXPERTS = 16
CAPACITY_FACTOR = 2
ROPE_BASE = 10000.0
EPS = 1e-6
NEG_INF = -1e30
QK_SCALE = HEAD_DIM ** -0.5
LOG2E = 1.4426950408889634

LANES = 128
SUBLANES = 8
BF16_ROWS = 2 * SUBLANES
VMEM_LIMIT_BYTES = 56 * 1024 * 1024

O_CB = 0
O_CC = O_CB + CONV_WIDTH
O_CX = O_CC + CONV_WIDTH
O_RQ = O_CX + CONV_WIDTH
O_RK = O_RQ + RET_WIDTH
O_RV = O_RK + RET_WIDTH
O_GF = O_RV + RET_WIDTH
O_GB = O_GF + RET_WIDTH
O_AQ = O_GB + RET_WIDTH
O_AK = O_AQ + ATTN_WIDTH
O_AV = O_AK + KV_WIDTH
IN_WIDTH = O_AV + KV_WIDTH

M_SH1, M_SC1, M_G1, M_SH2, M_SC2, M_G2 = range(6)
M_CTX = 6
M_ROWS = 16

INPROJ_TILE = 1088
FINAL_TILE = 256


def _params(*sem, vmem=VMEM_LIMIT_BYTES):
    return pltpu.CompilerParams(dimension_semantics=sem, vmem_limit_bytes=vmem)


def _dot(a, b):
    return jnp.dot(a, b, preferred_element_type=F32)


def _dot_nt(a, b):
    return lax.dot_general(a, b, (((1,), (1,)), ((), ())), preferred_element_type=F32)


def _silu(v):
    return v * jax.nn.sigmoid(v)


def _mod_kernel(c_ref, w_ref, b_ref, o_ref):
    s = _silu(c_ref[...])
    s_hi = s.astype(BF16)
    s_lo = (s - s_hi.astype(F32)).astype(BF16)
    w = w_ref[0]
    w_hi = w.astype(BF16)
    w_lo = (w - w_hi.astype(F32)).astype(BF16)
    n = s.shape[0]
    head = _dot(jnp.concatenate([s_hi, s_lo], axis=0), w_hi)
    o_ref[0] = head[:n] + (head[n:] + _dot(s_hi, w_lo)) + b_ref[0]


def _mod_vectors(c_rows, w_mod, b_mod):
    depth, d_model, width = w_mod.shape
    tn = 1536
    return pl.pallas_call(
        _mod_kernel,
        grid=(depth, width // tn),
        in_specs=[
            pl.BlockSpec((SUBLANES, d_model), lambda l, n: (0, 0)),
            pl.BlockSpec((1, d_model, tn), lambda l, n: (l, 0, n)),
            pl.BlockSpec((1, 1, tn), lambda l, n: (l, 0, n)),
        ],
        out_specs=pl.BlockSpec((1, SUBLANES, tn), lambda l, n: (l, 0, n)),
        out_shape=jax.ShapeDtypeStruct((depth, SUBLANES, width), F32),
        compiler_params=_params("parallel", "parallel"),
        name="mod_vectors",
    )(c_rows, w_mod, b_mod.reshape(depth, 1, width))


def _log_sigmoid(v):
    return -jnp.log(1.0 + jnp.exp(-v))


def _tables_kernel(lgl_ref, lgr_ref, qdec_ref, win_ref, cd_ref, dm_ref):
    pos = lax.broadcasted_iota(I32, (CHUNK, RET_WIDTH), 0).astype(F32)
    ri = lax.broadcasted_iota(I32, (CHUNK, RET_HEADS * CHUNK), 0).astype(F32)
    rj = (lax.broadcasted_iota(I32, (CHUNK, RET_HEADS * CHUNK), 1) & (CHUNK - 1)).astype(F32)
    for d in range(2):
        lg = _log_sigmoid(lgl_ref[0, d])
        lg1 = lg[0:1, :]
        if d == 0:
            qdec_ref[0, d] = jnp.exp(lg1 * (pos + 1.0))
            win_ref[0, d] = jnp.exp(lg1 * (CHUNK - 1.0 - pos))
            diff = ri - rj
        else:
            qdec_ref[0, d] = jnp.exp(lg1 * (CHUNK - pos))
            win_ref[0, d] = jnp.exp(lg1 * pos)
            diff = rj - ri
        cd_ref[0, d] = jnp.exp(lg * float(CHUNK))
        lr = _log_sigmoid(lgr_ref[0, d])
        dm_ref[0, d] = jnp.where(diff >= 0.0, jnp.exp(lr * jnp.maximum(diff, 0.0)), 0.0)


def _decay_tables(ret_decay_logit):
    depth = ret_decay_logit.shape[0]
    lg = ret_decay_logit.astype(F32)
    lgl = jnp.broadcast_to(jnp.repeat(lg, HEAD_DIM, axis=-1)[:, :, None, :], (depth, 2, SUBLANES, RET_WIDTH))
    lgr = jnp.broadcast_to(jnp.repeat(lg, CHUNK, axis=-1)[:, :, None, :], (depth, 2, CHUNK, RET_HEADS * CHUNK))
    return pl.pallas_call(
        _tables_kernel,
        grid=(depth,),
        in_specs=[
            pl.BlockSpec((1, 2, SUBLANES, RET_WIDTH), lambda l: (l, 0, 0, 0)),
            pl.BlockSpec((1, 2, CHUNK, RET_HEADS * CHUNK), lambda l: (l, 0, 0, 0)),
        ],
        out_specs=[
            pl.BlockSpec((1, 2, CHUNK, RET_WIDTH), lambda l: (l, 0, 0, 0)),
            pl.BlockSpec((1, 2, CHUNK, RET_WIDTH), lambda l: (l, 0, 0, 0)),
            pl.BlockSpec((1, 2, SUBLANES, RET_WIDTH), lambda l: (l, 0, 0, 0)),
            pl.BlockSpec((1, 2, CHUNK, RET_HEADS * CHUNK), lambda l: (l, 0, 0, 0)),
        ],
        out_shape=[
            jax.ShapeDtypeStruct((depth, 2, CHUNK, RET_WIDTH), F32),
            jax.ShapeDtypeStruct((depth, 2, CHUNK, RET_WIDTH), F32),
            jax.ShapeDtypeStruct((depth, 2, SUBLANES, RET_WIDTH), F32),
            jax.ShapeDtypeStruct((depth, 2, CHUNK, RET_HEADS * CHUNK), F32),
        ],
        compiler_params=_params("parallel"),
        name="decay_tables",
    )(lgl, lgr)


def _rope(v, cs, sn):
    lane = lax.broadcasted_iota(I32, (1, LANES), 1)
    first = (lane & 31) < 16
    outs = []
    for g in range(v.shape[1] // LANES):
        vg = v[:, g * LANES:(g + 1) * LANES]
        sw = jnp.where(first, pltpu.roll(vg, LANES - 16, 1), pltpu.roll(vg, 16, 1))
        outs.append(vg * cs + sw * sn)
    return outs[0] if len(outs) == 1 else jnp.concatenate(outs, axis=1)


def _swap_halves(v):
    return jnp.concatenate([v[:, HEAD_DIM:], v[:, :HEAD_DIM]], axis=1)


def _from_gather_layout(f_ref, tm):
    return jnp.concatenate(
        [f_ref[0, pl.ds(q, tm, stride=SUBLANES), :] for q in range(SUBLANES)], axis=1)


def _rms(v):
    return v * lax.rsqrt(jnp.mean(v * v, axis=-1, keepdims=True) + EPS)


def _mod_rows(mod_ref, tm, n_ctx, tile):
    m = mod_ref[0]
    if n_ctx % tm == 0:
        is_ctx = tile < n_ctx // tm
    else:
        is_ctx = tile * tm + lax.broadcasted_iota(I32, (tm, 1), 0) < n_ctx

    def row(r):
        return jnp.where(is_ctx, m[M_CTX + r:M_CTX + r + 1, :], m[r:r + 1, :])

    return row


def _inproj_kernel(has_ffn, n_ctx, *refs):
    if has_ffn:
        x_ref, f_ref, mod_ref, g_ref, w_ref, cs_ref, sn_ref = refs[:7]
        outs = refs[7:]
        xo_ref = outs[0]
        outs = outs[1:]
    else:
        x_ref, mod_ref, g_ref, w_ref, cs_ref, sn_ref = refs[:6]
        outs = refs[6:]
    cz_ref, rqkv_ref, gates_ref, aq_ref, kv_ref = outs
    tm = x_ref.shape[1]
    row = _mod_rows(mod_ref, tm, n_ctx, pl.program_id(1))

    x = x_ref[0]
    if has_ffn:
        x = x + row(M_G2) * _from_gather_layout(f_ref, tm)
        xo_ref[0] = x
    h = (_rms(x) * g_ref[...]) * (1.0 + row(M_SC1)) + row(M_SH1)
    h = h.astype(BF16)
    cs = cs_ref[...]
    sn = sn_ref[...]

    def proj(a, b):
        return _dot(h, w_ref[:, a:b])

    bounds = (O_CB, O_CC, O_CX, O_RQ, O_RK, O_RV, O_GF, O_GB, O_AQ, O_AK, O_AV, IN_WIDTH)
    p = [proj(a, b) for a, b in zip(bounds[:-1], bounds[1:])]
    cz_ref[0, :, 0:CONV_WIDTH] = p[0]
    cz_ref[0, :, CONV_WIDTH:] = p[1] * p[2]
    rqkv_ref[0, :, 0:RET_WIDTH] = _rope(p[3], cs, sn).astype(BF16)
    rqkv_ref[0, :, RET_WIDTH:2 * RET_WIDTH] = _rope(p[4] * QK_SCALE, cs, sn).astype(BF16)
    rqkv_ref[0, :, 2 * RET_WIDTH:] = p[5].astype(BF16)
    gates_ref[0, :, 0:RET_WIDTH] = _silu(p[6])
    gates_ref[0, :, RET_WIDTH:] = _silu(p[7])
    aq_ref[0] = (_rope(p[8], cs, sn) * (QK_SCALE * LOG2E)).astype(BF16)
    ak = _rope(p[9], cs, sn)
    av = p[10]
    kv_ref[0, :, 0:KV_WIDTH] = ak.astype(BF16)
    kv_ref[0, :, KV_WIDTH:2 * KV_WIDTH] = _swap_halves(ak).astype(BF16)
    kv_ref[0, :, 2 * KV_WIDTH:] = av.astype(BF16)


def _layer_block(stacked, layer):
    rest = stacked.shape[1:]
    return pl.BlockSpec((None,) + rest, lambda *_: (layer,) + (0,) * len(rest))


def _inproj(x, ffn, mrows, norm_g, w_in_bf, layer, rope_cs, rope_sn, n_ctx):
    b, tu, d = x.shape
    tm = INPROJ_TILE
    has_ffn = ffn is not None
    tok = lambda width: pl.BlockSpec((1, tm, width), lambda i, j: (i, j, 0))
    in_specs = [tok(d)]
    args = [x]
    if has_ffn:
        in_specs.append(pl.BlockSpec((1, tm * SUBLANES, LANES), lambda i, j: (i, j, 0)))
        args.append(ffn)
    in_specs += [
        pl.BlockSpec((1, M_ROWS, d), lambda i, j: (i, 0, 0)),
        _layer_block(norm_g, layer),
        pl.BlockSpec((None, d, IN_WIDTH), lambda i, j: (layer, 0, 0), pipeline_mode=pl.Buffered(1)),
        pl.BlockSpec((tm, LANES), lambda i, j: (j, 0)),
        pl.BlockSpec((tm, LANES), lambda i, j: (j, 0)),
    ]
    args += [mrows, norm_g, w_in_bf, rope_cs, rope_sn]
    widths = [(2 * CONV_WIDTH, F32), (3 * RET_WIDTH, BF16), (2 * RET_WIDTH, F32), (ATTN_WIDTH, BF16),
              (3 * KV_WIDTH, BF16)]
    out_specs = [tok(w) for w, _ in widths]
    out_shape = [jax.ShapeDtypeStruct((b, tu, w), dt) for w, dt in widths]
    if has_ffn:
        out_specs = [tok(d)] + out_specs
        out_shape = [jax.ShapeDtypeStruct((b, tu, d), F32)] + out_shape
    res = pl.pallas_call(
        functools.partial(_inproj_kernel, has_ffn, n_ctx),
        grid=(b, tu // tm),
        in_specs=in_specs,
        out_specs=out_specs,
        out_shape=out_shape,
        compiler_params=_params("parallel", "parallel"),
        name="inproj",
    )(*args)
    if has_ffn:
        return res[0], res[1:]
    return x, res


def _head_block_mask(n):
    r = lax.broadcasted_iota(I32, (n, n), 0) // HEAD_DIM
    c = lax.broadcasted_iota(I32, (n, n), 1) // HEAD_DIM
    return r == c


STATE_UNROLL = 2


def _states_kernel(n_ctx_chunks, n_chunks, rqkv_ref, win_ref, cd_ref, sp_ref, s_scr):
    same_head = _head_block_mask(RET_WIDTH)
    s_scr[...] = jnp.zeros_like(s_scr)

    def chunk_update(d, pos):
        if d == 0:
            c = pos
        else:
            c = jnp.where(pos < n_ctx_chunks, n_ctx_chunks - 1 - pos, n_chunks + n_ctx_chunks - 1 - pos)
        off = pl.multiple_of(c * CHUNK, CHUNK)
        kw = rqkv_ref[0, pl.ds(off, CHUNK), RET_WIDTH:2 * RET_WIDTH].astype(F32) * win_ref[d]
        v = rqkv_ref[0, pl.ds(off, CHUNK), 2 * RET_WIDTH:]
        kw_t = kw.T.astype(BF16)
        yield
        u = _dot(kw_t, v)
        yield
        return c, jnp.where(same_head, u, 0.0)

    def body(i, carry):
        todo = [(d, i * STATE_UNROLL + n) for n in range(STATE_UNROLL) for d in range(2)]
        done = _round_robin([chunk_update(d, pos) for d, pos in todo])
        for (d, _), (c, u) in zip(todo, done):
            s = s_scr[d]
            sp_ref[0, d, pl.ds(c, 1)] = s.astype(BF16)[None]
            s_scr[d] = s * cd_ref[d][0:1, :] + u
        return carry

    assert n_chunks % STATE_UNROLL == 0
    lax.fori_loop(0, n_chunks // STATE_UNROLL, body, 0)


def _ret_states(rqkv, win, cd, layer, n_ctx):
    b, tu, _ = rqkv.shape
    w = RET_WIDTH
    n_chunks = tu // CHUNK
    return pl.pallas_call(
        functools.partial(_states_kernel, n_ctx // CHUNK, n_chunks),
        grid=(b,),
        in_specs=[
            pl.BlockSpec((1, tu, 3 * w), lambda i: (i, 0, 0)),
            _layer_block(win, layer),
            _layer_block(cd, layer),
        ],
        out_specs=pl.BlockSpec((1, 2, n_chunks, w, w), lambda i: (i, 0, 0, 0, 0)),
        out_shape=jax.ShapeDtypeStruct((b, 2, n_chunks, w, w), BF16),
        scratch_shapes=[pltpu.VMEM((2, w, w), F32)],
        compiler_params=_params("parallel"),
        name="ret_states",
    )(rqkv, win, cd)


def _group_mean(t, ones_bd):
    hi = t.astype(BF16)
    lo = (t - hi.astype(F32)).astype(BF16)
    s = _dot(jnp.concatenate([hi, lo], axis=0), ones_bd)
    n = t.shape[0]
    return (s[:n] + s[n:]) * (1.0 / HEAD_DIM)


def _mix_chunk(c, n_ctx_chunks, n_chunks, cz, z_last, z_first, cw, rqkv, gates, sp, dm_ref, qdec_ref,
               qa, kvs, sink_ref):
    is_lat = c >= n_ctx_chunks

    z = cz[:, CONV_WIDTH:]
    row = lax.broadcasted_iota(I32, (CHUNK, 1), 0)
    has_prev = jnp.logical_and(c != 0, c != n_ctx_chunks)
    has_next = jnp.logical_and(c != n_ctx_chunks - 1, c != n_chunks - 1)
    z_before = jnp.where(row == 0, jnp.where(has_prev, z_last, 0.0), pltpu.roll(z, 1, 0))
    z_after = jnp.where(row == CHUNK - 1, jnp.where(has_next, z_first, 0.0), pltpu.roll(z, CHUNK - 1, 0))
    conv = cz[:, :CONV_WIDTH] * (z_before * cw[0:1, :] + z * cw[1:2, :] + z_after * cw[2:3, :])

    q = rqkv[:, 0:RET_WIDTH]
    k = rqkv[:, RET_WIDTH:2 * RET_WIDTH]
    v = rqkv[:, 2 * RET_WIDTH:]
    lane_head = lax.broadcasted_iota(I32, (1, RET_WIDTH), 1) // HEAD_DIM
    kz = jnp.zeros_like(k)
    k_heads = jnp.concatenate([jnp.where(lane_head == hh, k, kz) for hh in range(RET_HEADS)], axis=0)
    v_heads = jnp.concatenate([jnp.where(lane_head == hh, v, kz) for hh in range(RET_HEADS)], axis=0)
    scores = _dot_nt(q, k_heads)

    keys, v_t = kvs
    half = lax.broadcasted_iota(I32, (1, LANES), 1) // HEAD_DIM
    qa_z = jnp.zeros((CHUNK, LANES), BF16)
    combos = [(hk, par) for hk in range(ATTN_KV_HEADS) for par in range(2)]
    n_ctx = keys[0].shape[0] - 3 * CHUNK
    cols2 = 2 * CHUNK
    ik = lax.broadcasted_iota(I32, (CHUNK, cols2), 0)
    iq = lax.broadcasted_iota(I32, (CHUNK, cols2), 1) & (CHUNK - 1)
    off = jnp.full((CHUNK, cols2), NEG_INF, F32)
    bias_prev = jnp.where(jnp.logical_and(jnp.logical_and(is_lat, c - 1 >= n_ctx_chunks), ik >= iq), 0.0, off)
    bias_cur = jnp.where(is_lat, 0.0, off)
    bias_next = jnp.where(jnp.logical_and(jnp.logical_and(is_lat, c + 1 <= n_chunks - 1), ik <= iq), 0.0, off)
    assert n_ctx > 0
    logits = {}
    for hk, par in combos:
        ja, jb = 2 * hk, 2 * hk + 1
        qst = jnp.concatenate([
            jnp.where(half == par, qa[:, ja * LANES:(ja + 1) * LANES], qa_z),
            jnp.where(half == par, qa[:, jb * LANES:(jb + 1) * LANES], qa_z)], axis=0)
        sel = 0 if par == hk else 1
        r = _dot_nt(keys[sel], qst)
        s = jnp.concatenate([r[:CHUNK] + bias_prev, r[CHUNK:2 * CHUNK] + bias_cur,
                             r[2 * CHUNK:3 * CHUNK] + bias_next, r[3 * CHUNK:]], axis=0)
        ha, hb = ATTN_GROUP * hk + par, ATTN_GROUP * hk + par + 2
        snk = jnp.concatenate([sink_ref[ha * CHUNK:ha * CHUNK + 1, :],
                               sink_ref[hb * CHUNK:hb * CHUNK + 1, :]], axis=1) * LOG2E
        logits[hk, par] = (s, snk, jnp.maximum(jnp.max(s, axis=0, keepdims=True), snk))
    yield

    qf = q.astype(F32)
    outs = []
    for d in range(2):
        p = (scores * dm_ref[d]).astype(BF16)
        lhs = jnp.concatenate([p, (qf * qdec_ref[d]).astype(BF16)], axis=1)
        o = _dot(lhs, jnp.concatenate([v_heads, sp[d]], axis=0))
        outs.append(o)
    o2 = jnp.concatenate(outs, axis=0)
    yield

    att = {}
    ones_keys = jnp.ones((BF16_ROWS, keys[0].shape[0]), BF16)
    for n, (hk, par) in enumerate(combos):
        s, snk, mx = logits[hk, par]
        e = jnp.exp2(s - mx).astype(BF16)
        o_t = _dot(jnp.concatenate([v_t[hk * HEAD_DIM:(hk + 1) * HEAD_DIM, :], ones_keys], axis=0), e)
        den = o_t[HEAD_DIM:HEAD_DIM + 1, :] + jnp.exp2(snk - mx)
        att[hk, par] = o_t[:HEAD_DIM, :] * (1.0 / den)
        if n == 0:
            ones_bd = jnp.where(_head_block_mask(RET_WIDTH), 1.0, 0.0).astype(BF16)
            dl = o2 - _group_mean(o2, ones_bd)
            on = dl * lax.rsqrt(_group_mean(dl * dl, ones_bd) + EPS)
            ret = on[:CHUNK] * gates[:, 0:RET_WIDTH] + on[CHUNK:] * gates[:, RET_WIDTH:]
        yield

    cols = []
    for hk in range(ATTN_KV_HEADS):
        cols.append(jnp.concatenate([att[hk, 0][:, :CHUNK], att[hk, 1][:, :CHUNK]], axis=0).T)
        cols.append(jnp.concatenate([att[hk, 0][:, CHUNK:], att[hk, 1][:, CHUNK:]], axis=0).T)
    return jnp.concatenate([conv, ret] + cols, axis=1)


def _round_robin(gens):
    results = [None] * len(gens)
    active = list(range(len(gens)))
    while active:
        for i in list(active):
            try:
                next(gens[i])
            except StopIteration as done:
                results[i] = done.value
                active.remove(i)
    return results


MIX_CHUNKS = 2


def _mixer_kernel(n_ctx_chunks, n_chunks, n_exp,
                  cz_ref, czp_ref, czn_ref, cw_ref, rqkv_ref, gates_ref, sp_ref, dm_ref, qdec_ref,
                  aq_ref, kvp_ref, kvc_ref, kvn_ref, kvx_ref, sink_ref,
                  x_ref, mod_ref, g_ref, w_ref, wrh_ref, wrl_ref, xo_ref, hg_ref, aff_ref, mix_scr):
    t = pl.program_id(1)
    n_steps = pl.num_programs(1) - 1

    @pl.when(t == 0)
    def _():
        mix_scr[...] = jnp.zeros_like(mix_scr)

    tail = _outproj_tail(n_ctx_chunks * CHUNK, n_exp, jnp.maximum(t - 1, 0), mix_scr[...], x_ref, mod_ref, g_ref,
                         w_ref, wrh_ref, wrl_ref, xo_ref, hg_ref, aff_ref)
    j = jnp.minimum(t, n_steps - 1)
    cw = cw_ref[...]

    def values_t(blk):
        return blk[:, 2 * KV_WIDTH:].astype(F32).T.astype(BF16)

    kv_blocks = ([kvp_ref[0]] + [kvc_ref[0, s * CHUNK:(s + 1) * CHUNK, :] for s in range(MIX_CHUNKS)] + [kvn_ref[0]])
    kv_ctx = [kvx_ref[0, t * CHUNK:(t + 1) * CHUNK, :] for t in range(kvx_ref.shape[1] // CHUNK)]
    vt_blocks = [values_t(blk) for blk in kv_blocks]
    vt_ctx = [values_t(blk) for blk in kv_ctx]
    gens = []
    for s in range(MIX_CHUNKS):
        lo, hi = s * CHUNK, (s + 1) * CHUNK
        if s == 0:
            z_last = czp_ref[0, SUBLANES - 1:SUBLANES, CONV_WIDTH:]
        else:
            z_last = cz_ref[0, lo - 1:lo, CONV_WIDTH:]
        if s == MIX_CHUNKS - 1:
            z_first = czn_ref[0, 0:1, CONV_WIDTH:]
        else:
            z_first = cz_ref[0, hi:hi + 1, CONV_WIDTH:]
        window = kv_blocks[s:s + 3] + kv_ctx
        keys = tuple(jnp.concatenate([blk[:, i * KV_WIDTH:(i + 1) * KV_WIDTH] for blk in window], axis=0)
                     for i in (0, 1))
        v_t = jnp.concatenate(vt_blocks[s:s + 3] + vt_ctx, axis=1)
        gens.append(_mix_chunk(j * MIX_CHUNKS + s, n_ctx_chunks, n_chunks, cz_ref[0, lo:hi, :], z_last, z_first,
                               cw, rqkv_ref[0, lo:hi, :], gates_ref[0, lo:hi, :],
                               (sp_ref[0, 0, s], sp_ref[0, 1, s]), dm_ref, qdec_ref, aq_ref[0, lo:hi, :],
                               (keys, v_t), sink_ref))
    mixes = [mix.astype(BF16) for mix in _round_robin([tail] + gens)[1:]]
    mix_scr[...] = jnp.concatenate(mixes, axis=0)


def _mixer(proj, sp, tabs, conv_w, sink_rows, x, mrows, norm_g, w_out_bf, layer, wr_hi, wr_lo, n_ctx):
    cz, rqkv, gates, aq, kv = proj
    dm, qdec = tabs
    b, tu, _ = cz.shape
    d = x.shape[2]
    n_exp = wr_hi.shape[1]
    n_chunks = tu // CHUNK
    mc = MIX_CHUNKS
    rows = mc * CHUNK
    assert n_chunks % mc == 0 and (n_ctx // CHUNK) % mc == 0
    last = n_chunks - 1
    n_steps = n_chunks // mc
    per8 = rows // SUBLANES
    mixed = lambda t: jnp.minimum(t, n_steps - 1)
    projected = lambda t: jnp.maximum(t - 1, 0)
    cur = lambda w: pl.BlockSpec((1, rows, w), lambda i, t: (i, mixed(t), 0))
    out = lambda r, w: pl.BlockSpec((1, r, w), lambda i, t: (i, projected(t), 0))
    wcz, wkv = cz.shape[2], kv.shape[2]
    in_specs = [
        cur(wcz),
        pl.BlockSpec((1, SUBLANES, wcz), lambda i, t: (i, jnp.maximum(mixed(t) * per8 - 1, 0), 0)),
        pl.BlockSpec((1, SUBLANES, wcz),
                     lambda i, t: (i, jnp.minimum((mixed(t) + 1) * per8, tu // SUBLANES - 1), 0)),
        _layer_block(conv_w, layer),
        cur(rqkv.shape[2]), cur(gates.shape[2]),
        pl.BlockSpec((1, 2, mc, RET_WIDTH, RET_WIDTH), lambda i, t: (i, 0, mixed(t), 0, 0)),
        _layer_block(dm, layer), _layer_block(qdec, layer),
        cur(ATTN_WIDTH),
        pl.BlockSpec((1, CHUNK, wkv), lambda i, t: (i, jnp.maximum(mixed(t) * mc - 1, 0), 0)),
        cur(wkv),
        pl.BlockSpec((1, CHUNK, wkv), lambda i, t: (i, jnp.minimum((mixed(t) + 1) * mc, last), 0)),
        pl.BlockSpec((1, n_ctx, wkv), lambda i, t: (i, 0, 0)),
        _layer_block(sink_rows, layer),
        out(rows, d),
        pl.BlockSpec((1, M_ROWS, d), lambda i, t: (i, 0, 0)),
        _layer_block(norm_g, layer),
        _layer_block(w_out_bf, layer),
        _layer_block(wr_hi, layer), _layer_block(wr_lo, layer),
    ]
    args = [cz, cz, cz, conv_w, rqkv, gates, sp, dm, qdec, aq, kv, kv, kv, kv, sink_rows,
            x, mrows, norm_g, w_out_bf, wr_hi, wr_lo]
    return pl.pallas_call(
        functools.partial(_mixer_kernel, n_ctx // CHUNK, n_chunks, n_exp),
        grid=(b, n_steps + 1),
        in_specs=in_specs,
        out_specs=[out(rows, d), out(rows * SUBLANES, LANES),
                   pl.BlockSpec((1, n_exp, rows), lambda i, t: (i, 0, projected(t)))],
        out_shape=[
            jax.ShapeDtypeStruct((b, tu, d), F32),
            jax.ShapeDtypeStruct((b, tu * SUBLANES, LANES), F32),
            jax.ShapeDtypeStruct((b, n_exp, tu), F32),
        ],
        scratch_shapes=[pltpu.VMEM((rows, CONV_WIDTH + RET_WIDTH + ATTN_WIDTH), BF16)],
        compiler_params=_params("parallel", "arbitrary"),
        name="mixer",
    )(*args)


def _outproj_tail(n_ctx, n_exp, tile, mix, x_ref, mod_ref, g_ref, w_ref, wrh_ref, wrl_ref, xo_ref, hg_ref, aff_ref):
    tm = x_ref.shape[1]
    row = _mod_rows(mod_ref, tm, n_ctx, tile)

    y = _dot(mix, w_ref[...])
    yield
    x = x_ref[0] + row(M_G1) * y
    xo_ref[0] = x
    h = (_rms(x) * g_ref[...]) * (1.0 + row(M_SC2)) + row(M_SH2)
    for q in range(SUBLANES):
        hg_ref[0, pl.ds(q, tm, stride=SUBLANES), :] = h[:, q * LANES:(q + 1) * LANES]
    yield
    h_hi = h.astype(BF16)
    h_lo = (h - h_hi.astype(F32)).astype(BF16)
    logits = (_dot_nt(wrh_ref[...], h_hi) + (_dot_nt(wrh_ref[...], h_lo) + _dot_nt(wrl_ref[...], h_hi)))
    yield
    e = jnp.exp(logits - jnp.max(logits, axis=0, keepdims=True))
    aff_ref[0] = e / jnp.sum(e, axis=0, keepdims=True)


MIN_NORMAL_F32_BITS = 0x00800000
ROUTE_TABLES = 4
ROUTE_GROUP = 4


def _route_stream_tables(v, k, tab_scr, ce_scr, cnt_scr, stream, thr):
    n_exp, n = v.shape
    lane = lax.broadcasted_iota(I32, (1, LANES), 1)
    tri_r = lax.broadcasted_iota(I32, (LANES, LANES), 0)
    tri_c = lax.broadcasted_iota(I32, (LANES, LANES), 1)
    upper = jnp.where(tri_r <= tri_c, 1.0, 0.0).astype(BF16)
    above = pltpu.bitcast(jnp.maximum(thr + 1, MIN_NORMAL_F32_BITS), F32)
    floor = pltpu.bitcast(thr, F32)
    n_gt = jnp.sum(jnp.where(v >= above, 1, 0), axis=1, keepdims=True)
    need = (k - n_gt).astype(F32)
    seen_eq = jnp.zeros((n_exp, 1), F32)
    counts = jnp.zeros((n_exp, LANES), F32)
    for j in range(n // LANES):
        vb = v[:, j * LANES:(j + 1) * LANES]
        gt = vb >= above
        eq = jnp.logical_and(vb >= floor, vb < above)
        both = jnp.concatenate([jnp.where(gt, 1.0, 0.0), jnp.where(eq, 1.0, 0.0)], axis=0).astype(BF16)
        pref = _dot(both, upper)
        rank_eq = pref[n_exp:] + seen_eq
        inc = pref[:n_exp] + jnp.minimum(rank_eq, need) - jnp.minimum(seen_eq, need)
        sel = jnp.logical_or(gt, jnp.logical_and(eq, rank_eq <= need))
        seen_eq = rank_eq[:, LANES - 1:LANES]
        counts = counts + jnp.where(lane == j, inc[:, LANES - 1:LANES], 0.0)
        a0 = jnp.where(sel, vb, 0.0)
        t0 = a0.astype(BF16).astype(F32)
        t1 = (a0 - t0).astype(BF16).astype(F32)
        t2 = a0 - t0 - t1
        for t, val in enumerate((inc, t0, t1, t2)):
            tab_scr[stream * ROUTE_TABLES + t, pl.ds(j, n_exp, stride=LANES), :] = val
    through = _dot(counts.astype(BF16), upper)
    for e in range(n_exp):
        ce_scr[stream, e] = through[e:e + 1, :]
        cnt_scr[stream, e] = counts[e:e + 1, :]


def _route_slots(e, stream, k, lo, tab_scr, ce_scr, cnt_scr):
    lane_f = lax.broadcasted_iota(I32, (1, LANES), 1).astype(F32)
    slot = lax.broadcasted_iota(I32, (k, LANES), 0).astype(F32)
    ones_rows = jnp.ones((LANES, LANES), BF16)
    cnt_rows = jnp.broadcast_to(cnt_scr[stream, e], (LANES, LANES)).astype(BF16)
    rows_e = pl.ds(pl.multiple_of(e * LANES, LANES), LANES)
    tab = jnp.concatenate([tab_scr[stream * ROUTE_TABLES + t, rows_e, :] for t in range(ROUTE_TABLES)],
                          axis=1).astype(BF16)
    before = jnp.where(ce_scr[stream, e] <= slot, 1.0, 0.0).astype(BF16)
    blk = _dot_nt(before, ones_rows)
    base = _dot_nt(before, cnt_rows)
    yield
    row = _dot(jnp.where(lane_f == blk, 1.0, 0.0).astype(BF16), tab)
    yield
    inc = row[:, :LANES]
    aff = row[:, LANES:2 * LANES] + row[:, 2 * LANES:3 * LANES] + row[:, 3 * LANES:]
    local = slot - base
    pos = _dot_nt(jnp.where(inc <= local, 1.0, 0.0).astype(BF16), ones_rows)
    yield
    tok = blk * LANES + pos + float(lo)
    gate = jnp.sum(jnp.where(inc == local + 1.0, aff, 0.0), axis=1, keepdims=True)
    return tok, gate


def _route_kernel(n_exp, streams, aff_ref, idx_ref, gate_ref, tab_scr, ce_scr, cnt_scr):
    a = aff_ref[0]
    tab_scr[...] = jnp.zeros_like(tab_scr)
    vs = [a[:, lo:lo + n] for lo, n, _, _ in streams]

    def search(i, ts):
        out = []
        for t, v, (_, _, k, _) in zip(ts, vs, streams):
            cand = t | jnp.left_shift(jnp.int32(1), 30 - i)
            cnt = jnp.sum(jnp.where(v >= pltpu.bitcast(cand, F32), 1, 0), axis=1, keepdims=True)
            out.append(jnp.where(cnt >= k, cand, t))
        return tuple(out)

    thrs = lax.fori_loop(0, 31, search, tuple(jnp.zeros((n_exp, 1), I32) for _ in streams))
    for st, (v, (_, _, k, _)) in enumerate(zip(vs, streams)):
        _route_stream_tables(v, k, tab_scr, ce_scr, cnt_scr, st, thrs[st])

    slots = idx_ref.shape[3]

    assert [s[3] for s in streams] == [sum(s[2] for s in streams[:n]) for n in range(len(streams))]

    def expert_group(g, carry):
        experts = [g * ROUTE_GROUP + u for u in range(ROUTE_GROUP)]
        found = _round_robin([_route_slots(e, st, k, lo, tab_scr, ce_scr, cnt_scr)
                              for e in experts for st, (lo, _, k, _) in enumerate(streams)])
        for u, e in enumerate(experts):
            toks = []
            for st, (_, _, k, slot0) in enumerate(streams):
                tok, gate = found[u * len(streams) + st]
                toks.append(tok)
                gate_ref[0, pl.ds(e, 1), slot0:slot0 + k, :] = gate[None]
            toks.append(jnp.zeros((-slots % LANES, LANES), F32))
            row = jnp.concatenate(toks, axis=0).T[0:1, :slots]
            idx_ref[0, pl.ds(e, 1)] = row.astype(I32)[None]
        return carry

    assert n_exp % ROUTE_GROUP == 0
    lax.fori_loop(0, n_exp // ROUTE_GROUP, expert_group, 0)


def _route(aff, n_ctx):
    b, n_exp, tu = aff.shape
    n_lat = tu - n_ctx
    cap_l = CAPACITY_FACTOR * n_lat // n_exp
    cap_c = CAPACITY_FACTOR * n_ctx // n_exp
    slots = cap_l + cap_c
    assert n_lat // LANES <= LANES and n_ctx // LANES <= LANES
    streams = ((n_ctx, n_lat, cap_l, 0), (0, n_ctx, cap_c, cap_l))
    return pl.pallas_call(
        functools.partial(_route_kernel, n_exp, streams),
        grid=(b,),
        in_specs=[pl.BlockSpec((1, n_exp, tu), lambda i: (i, 0, 0))],
        out_specs=[pl.BlockSpec((1, n_exp, 1, slots), lambda i: (i, 0, 0, 0)),
                   pl.BlockSpec((1, n_exp, slots, 1), lambda i: (i, 0, 0, 0))],
        out_shape=[jax.ShapeDtypeStruct((b, n_exp, 1, slots), I32),
                   jax.ShapeDtypeStruct((b, n_exp, slots, 1), F32)],
        scratch_shapes=[pltpu.VMEM((len(streams) * ROUTE_TABLES, n_exp * LANES, LANES), F32),
                        pltpu.VMEM((len(streams), n_exp, 1, LANES), F32),
                        pltpu.VMEM((len(streams), n_exp, 1, LANES), F32)],
        compiler_params=_params("parallel"),
        name="route",
    )(aff)


def _slot_pitch(slots):
    return slots + SUBLANES


def _zero_slot_padding(ref, lead, slots, pitch):
    for q in range(SUBLANES):
        ref[lead + (pl.ds(q * pitch + slots, pitch - slots), slice(None))] = jnp.zeros((pitch - slots, LANES), ref.dtype)


def _slot_pitch_bf16(slots):
    return -(-slots // BF16_ROWS) * BF16_ROWS + BF16_ROWS


DISPATCH_EXPERTS = 4


def _gather_kernel(slots, pitch, pitch_out, idx_ref, h_ref, o_ref, tile_scr):
    for g in range(DISPATCH_EXPERTS):
        for mi in range(slots):
            r = pl.multiple_of(idx_ref[g, 0, mi] * SUBLANES, SUBLANES)
            tile_scr[g, pl.ds(mi, SUBLANES, stride=pitch), :] = h_ref[0, pl.ds(r, SUBLANES), :]
        _zero_slot_padding(o_ref, (0, g), slots, pitch_out)
        for q in range(SUBLANES):
            o_ref[0, g, q * pitch_out:q * pitch_out + slots, :] = (
                tile_scr[g, q * pitch:q * pitch + slots, :].astype(BF16))


def _gather(hg, idx_rows, n_exp, slots):
    b = hg.shape[0]
    pitch = _slot_pitch(slots)
    pitch_out = _slot_pitch_bf16(slots)
    ge = DISPATCH_EXPERTS
    assert n_exp % ge == 0
    return pl.pallas_call(
        functools.partial(_gather_kernel, slots, pitch, pitch_out),
        grid=(b, n_exp // ge),
        in_specs=[
            pl.BlockSpec((ge, 1, slots), lambda i, e: (i * (n_exp // ge) + e, 0, 0), memory_space=pltpu.SMEM),
            pl.BlockSpec((1,) + hg.shape[1:], lambda i, e: (i, 0, 0)),
        ],
        out_specs=pl.BlockSpec((1, ge, SUBLANES * pitch_out, LANES), lambda i, e: (i, e, 0, 0)),
        out_shape=jax.ShapeDtypeStruct((b, n_exp, SUBLANES * pitch_out, LANES), BF16),
        scratch_shapes=[pltpu.VMEM((ge, SUBLANES * pitch, LANES), F32)],
        compiler_params=_params("parallel", "arbitrary"),
        name="gather",
    )(idx_rows, hg)


FFN_SAMPLES = 2
FFN_VMEM_LIMIT_BYTES = 60 * 1024 * 1024


def _ffn_kernel(slots, pitch_in, pitch, xs_ref, wg_ref, wu_ref, wd_ref, gate_ref, y_ref):
    n = xs_ref.shape[0]
    x = jnp.concatenate(
        [jnp.concatenate([xs_ref[s, 0, q * pitch_in:q * pitch_in + slots, :] for q in range(SUBLANES)], axis=1)
         for s in range(n)], axis=0)
    gate = jnp.concatenate([gate_ref[s, 0] for s in range(n)], axis=0)
    a = _dot(x, wg_ref[0].astype(BF16))
    u = _dot(x, wu_ref[0].astype(BF16))
    y = _dot((_silu(a) * u).astype(BF16), wd_ref[0].astype(BF16)) * gate
    for s in range(n):
        _zero_slot_padding(y_ref, (s, 0), slots, pitch)
        for q in range(SUBLANES):
            y_ref[s, 0, q * pitch:q * pitch + slots, :] = y[s * slots:(s + 1) * slots, q * LANES:(q + 1) * LANES]


def _ffn(xs, w_gate, w_up, w_down, layer, gate, slots):
    b, n_exp = xs.shape[:2]
    pitch = _slot_pitch(slots)
    pitch_in = _slot_pitch_bf16(slots)
    d, f = w_gate.shape[2:]
    ns = FFN_SAMPLES
    assert b % ns == 0
    slot_tile = lambda p: pl.BlockSpec((ns, 1, SUBLANES * p, LANES), lambda e, i: (i, e, 0, 0))
    weight = lambda rows, cols: pl.BlockSpec((None, 1, rows, cols), lambda e, i: (layer, e, 0, 0))
    return pl.pallas_call(
        functools.partial(_ffn_kernel, slots, pitch_in, pitch),
        grid=(n_exp, b // ns),
        in_specs=[
            slot_tile(pitch_in),
            weight(d, f), weight(d, f), weight(f, d),
            pl.BlockSpec((ns, 1, slots, 1), lambda e, i: (i, e, 0, 0)),
        ],
        out_specs=slot_tile(pitch),
        out_shape=jax.ShapeDtypeStruct((b, n_exp, SUBLANES * pitch, LANES), F32),
        compiler_params=_params("parallel", "arbitrary", vmem=FFN_VMEM_LIMIT_BYTES),
        name="ffn",
    )(xs, w_gate, w_up, w_down, gate)


SCATTER_BATCH = 16


def _scatter_kernel(slots, pitch, idx_ref, y_ref, o_ref):
    @pl.when(pl.program_id(1) == 0)
    def _():
        o_ref[...] = jnp.zeros_like(o_ref)

    for g in range(DISPATCH_EXPERTS):
        for m0 in range(0, slots, SCATTER_BATCH):
            rows = [pl.multiple_of(idx_ref[g, 0, m0 + u] * SUBLANES, SUBLANES) for u in range(SCATTER_BATCH)]
            vals = [o_ref[0, pl.ds(rows[u], SUBLANES), :] + y_ref[0, g, pl.ds(m0 + u, SUBLANES, stride=pitch), :]
                    for u in range(SCATTER_BATCH)]
            for u in range(SCATTER_BATCH):
                o_ref[0, pl.ds(rows[u], SUBLANES), :] = vals[u]


def _scatter(y, idx_rows, tu, slots):
    b, n_exp = y.shape[:2]
    pitch = _slot_pitch(slots)
    ge = DISPATCH_EXPERTS
    assert n_exp % ge == 0 and slots % SCATTER_BATCH == 0
    return pl.pallas_call(
        functools.partial(_scatter_kernel, slots, pitch),
        grid=(b, n_exp // ge),
        in_specs=[
            pl.BlockSpec((ge, 1, slots), lambda i, e: (i * (n_exp // ge) + e, 0, 0), memory_space=pltpu.SMEM),
            pl.BlockSpec((1, ge, SUBLANES * pitch, LANES), lambda i, e: (i, e, 0, 0)),
        ],
        out_specs=pl.BlockSpec((1, tu * SUBLANES, LANES), lambda i, e: (i, 0, 0)),
        out_shape=jax.ShapeDtypeStruct((b, tu * SUBLANES, LANES), F32),
        compiler_params=_params("parallel", "arbitrary"),
        name="scatter",
    )(idx_rows, y)


FINAL_PARTS = 4


def _final_kernel(mod_ref, g_ref, *refs):
    o_ref = refs[-1]
    tm = FINAL_TILE
    for n in range(FINAL_PARTS):
        x_ref, f_ref = refs[n], refs[FINAL_PARTS + n]
        x = x_ref[0] + mod_ref[0][M_G2:M_G2 + 1, :] * _from_gather_layout(f_ref, tm)
        o_ref[0, n * tm:(n + 1) * tm, :] = _rms(x) * g_ref[...]


def _final(x, ffn, mrows, final_g, n_ctx):
    b, tu, d = x.shape
    tm = FINAL_TILE
    skip = n_ctx // tm
    parts = FINAL_PARTS
    assert (tu - n_ctx) % (tm * parts) == 0
    tile = lambda n: pl.BlockSpec((1, tm, d), lambda i, j: (i, j * parts + n + skip, 0))
    ftile = lambda n: pl.BlockSpec((1, tm * SUBLANES, LANES), lambda i, j: (i, j * parts + n + skip, 0))
    return pl.pallas_call(
        _final_kernel,
        grid=(b, (tu - n_ctx) // (tm * parts)),
        in_specs=([pl.BlockSpec((1, M_ROWS, d), lambda i, j: (i, 0, 0)), pl.BlockSpec((1, d), lambda i, j: (0, 0))]
                  + [tile(n) for n in range(parts)] + [ftile(n) for n in range(parts)]),
        out_specs=pl.BlockSpec((1, tm * parts, d), lambda i, j: (i, j, 0)),
        out_shape=jax.ShapeDtypeStruct((b, tu - n_ctx, d), F32),
        compiler_params=_params("parallel", "parallel"),
        name="final_norm",
    )(mrows, final_g.reshape(1, d), *([x] * parts + [ffn] * parts))


def _rope_tables(n_lat, n_ctx):
    rows = n_lat // GRID_W
    rowp = jnp.repeat(jnp.arange(rows, dtype=F32), GRID_W)
    colp = jnp.tile(jnp.arange(GRID_W, dtype=F32), rows)
    axis_dim = HEAD_DIM // 2
    inv_freq = ROPE_BASE ** (-jnp.arange(0, axis_dim, 2, dtype=F32) / axis_dim)
    ar = rowp[:, None] * inv_freq
    ac = colp[:, None] * inv_freq
    cs = jnp.concatenate([jnp.cos(ar), jnp.cos(ar), jnp.cos(ac), jnp.cos(ac)], axis=1)
    sn = jnp.concatenate([-jnp.sin(ar), jnp.sin(ar), -jnp.sin(ac), jnp.sin(ac)], axis=1)
    reps = LANES // HEAD_DIM
    cs = jnp.concatenate([jnp.ones((n_ctx, LANES), F32), jnp.tile(cs, (1, reps))], axis=0)
    sn = jnp.concatenate([jnp.zeros((n_ctx, LANES), F32), jnp.tile(sn, (1, reps))], axis=0)
    return cs, sn


def kernel(x, c, ctx, c_ctx, w_mod, b_mod, norm1_g, norm2_g, w_in, conv_w, ret_decay_logit, attn_sink,
           w_out, w_router, w_gate, w_up, w_down, final_g):
    b, n_lat, d = x.shape
    n_ctx = ctx.shape[1]
    depth = w_in.shape[0]
    n_exp = w_router.shape[2]
    tu = n_ctx + n_lat
    assert w_in.shape[2] == IN_WIDTH and tu % INPROJ_TILE == 0
    assert n_ctx % FINAL_TILE == 0 and n_lat % FINAL_TILE == 0
    assert b + 1 <= SUBLANES and n_exp == N_EXPERTS

    c_rows = jnp.concatenate([c, c_ctx[None], jnp.zeros((SUBLANES - b - 1, d), F32)], axis=0)
    mods = _mod_vectors(c_rows, w_mod, b_mod).reshape(depth, SUBLANES, 6, d)
    qdec, win, cd, dm = _decay_tables(ret_decay_logit)
    rope_cs, rope_sn = _rope_tables(n_lat, n_ctx)
    sink_rows = jnp.broadcast_to(jnp.repeat(attn_sink.astype(F32), CHUNK, axis=1)[:, :, None],
                                 (depth, ATTN_HEADS * CHUNK, LANES))
    w_in_bf = w_in.astype(BF16)
    w_out_bf = w_out.astype(BF16)
    wr = jnp.swapaxes(w_router, 1, 2)
    wr_hi = wr.astype(BF16)
    wr_lo = (wr - wr_hi.astype(F32)).astype(BF16)
    g1 = norm1_g.reshape(depth, 1, d)
    g2 = norm2_g.reshape(depth, 1, d)
    mrows = jnp.concatenate([mods[:, :b], jnp.broadcast_to(mods[:, b][:, None], (depth, b, 6, d)),
                             jnp.zeros((depth, b, M_ROWS - 12, d), F32)], axis=2)
    mrows_in = mrows.at[1:, :, M_G2].set(mrows[:-1, :, M_G2])
    mrows_in = mrows_in.at[1:, :, M_CTX + M_G2].set(mrows[:-1, :, M_CTX + M_G2])

    xu = jnp.concatenate([ctx, x], axis=1)
    ffn = None
    cap = CAPACITY_FACTOR * n_lat // n_exp + CAPACITY_FACTOR * n_ctx // n_exp
    for l in range(depth):
        xu, proj = _inproj(xu, ffn, mrows_in[l], g1, w_in_bf, l, rope_cs, rope_sn, n_ctx)
        sp = _ret_states(proj[1], win, cd, l, n_ctx)
        xu, hg, aff = _mixer(proj, sp, (dm, qdec), conv_w, sink_rows, xu, mrows[l], g2, w_out_bf, l,
                             wr_hi, wr_lo, n_ctx)
        idx, gate = _route(aff, n_ctx)
        idx_rows = idx.reshape(b * n_exp, 1, cap)
        xs = _gather(hg, idx_rows, n_exp, cap)
        y = _ffn(xs, w_gate, w_up, w_down, l, gate, cap)
        ffn = _scatter(y, idx_rows, tu, cap)
    return _final(xu, ffn, mrows[depth - 1], final_g, n_ctx)
```

```python
import functools

import jax
import jax.numpy as jnp
from jax import lax
from jax.experimental import pallas as pl
from jax.experimental.pallas import tpu as pltpu

F32 = jnp.float32
BF16 = jnp.bfloat16
I32 = jnp.int32

HEAD_DIM = 64
CONV_WIDTH = 256
RET_HEADS = 4
RET_WIDTH = RET_HEADS * HEAD_DIM
ATTN_HEADS = 8
ATTN_KV_HEADS = 2
ATTN_GROUP = ATTN_HEADS // ATTN_KV_HEADS
ATTN_WIDTH = ATTN_HEADS * HEAD_DIM
KV_WIDTH = ATTN_KV_HEADS * HEAD_DIM
CHUNK = 128
GRID_W = 64
N_EXPERTS = 16
CAPACITY_FACTOR = 2
ROPE_BASE = 10000.0
EPS = 1e-6
NEG_INF = -1e30
QK_SCALE = HEAD_DIM ** -0.5
LOG2E = 1.4426950408889634

LANES = 128
SUBLANES = 8
BF16_ROWS = 2 * SUBLANES
VMEM_LIMIT_BYTES = 56 * 1024 * 1024

O_CB = 0
O_CC = O_CB + CONV_WIDTH
O_CX = O_CC + CONV_WIDTH
O_RQ = O_CX + CONV_WIDTH
O_RK = O_RQ + RET_WIDTH
O_RV = O_RK + RET_WIDTH
O_GF = O_RV + RET_WIDTH
O_GB = O_GF + RET_WIDTH
O_AQ = O_GB + RET_WIDTH
O_AK = O_AQ + ATTN_WIDTH
O_AV = O_AK + KV_WIDTH
IN_WIDTH = O_AV + KV_WIDTH

M_SH1, M_SC1, M_G1, M_SH2, M_SC2, M_G2 = range(6)
M_CTX = 6
M_ROWS = 16

INPROJ_TILE = 1088
FINAL_TILE = 256


def _params(*sem, vmem=VMEM_LIMIT_BYTES):
    return pltpu.CompilerParams(dimension_semantics=sem, vmem_limit_bytes=vmem)


def _dot(a, b):
    return jnp.dot(a, b, preferred_element_type=F32)


def _dot_nt(a, b):
    return lax.dot_general(a, b, (((1,), (1,)), ((), ())), preferred_element_type=F32)


def _silu(v):
    return v * jax.nn.sigmoid(v)


def _mod_kernel(c_ref, w_ref, b_ref, o_ref):
    s = _silu(c_ref[...])
    s_hi = s.astype(BF16)
    s_lo = (s - s_hi.astype(F32)).astype(BF16)
    w = w_ref[0]
    w_hi = w.astype(BF16)
    w_lo = (w - w_hi.astype(F32)).astype(BF16)
    n = s.shape[0]
    head = _dot(jnp.concatenate([s_hi, s_lo], axis=0), w_hi)
    o_ref[0] = head[:n] + (head[n:] + _dot(s_hi, w_lo)) + b_ref[0]


def _mod_vectors(c_rows, w_mod, b_mod):
    depth, d_model, width = w_mod.shape
    tn = 1536
    return pl.pallas_call(
        _mod_kernel,
        grid=(depth, width // tn),
        in_specs=[
            pl.BlockSpec((SUBLANES, d_model), lambda l, n: (0, 0)),
            pl.BlockSpec((1, d_model, tn), lambda l, n: (l, 0, n)),
            pl.BlockSpec((1, 1, tn), lambda l, n: (l, 0, n)),
        ],
        out_specs=pl.BlockSpec((1, SUBLANES, tn), lambda l, n: (l, 0, n)),
        out_shape=jax.ShapeDtypeStruct((depth, SUBLANES, width), F32),
        compiler_params=_params("parallel", "parallel"),
        name="mod_vectors",
    )(c_rows, w_mod, b_mod.reshape(depth, 1, width))


def _log_sigmoid(v):
    return -jnp.log(1.0 + jnp.exp(-v))


def _tables_kernel(lgl_ref, lgr_ref, qdec_ref, win_ref, cd_ref, dm_ref):
    pos = lax.broadcasted_iota(I32, (CHUNK, RET_WIDTH), 0).astype(F32)
    ri = lax.broadcasted_iota(I32, (CHUNK, RET_HEADS * CHUNK), 0).astype(F32)
    rj = (lax.broadcasted_iota(I32, (CHUNK, RET_HEADS * CHUNK), 1) & (CHUNK - 1)).astype(F32)
    for d in range(2):
        lg = _log_sigmoid(lgl_ref[0, d])
        lg1 = lg[0:1, :]
        if d == 0:
            qdec_ref[0, d] = jnp.exp(lg1 * (pos + 1.0))
            win_ref[0, d] = jnp.exp(lg1 * (CHUNK - 1.0 - pos))
            diff = ri - rj
        else:
            qdec_ref[0, d] = jnp.exp(lg1 * (CHUNK - pos))
            win_ref[0, d] = jnp.exp(lg1 * pos)
            diff = rj - ri
        cd_ref[0, d] = jnp.exp(lg * float(CHUNK))
        lr = _log_sigmoid(lgr_ref[0, d])
        dm_ref[0, d] = jnp.where(diff >= 0.0, jnp.exp(lr * jnp.maximum(diff, 0.0)), 0.0)


def _decay_tables(ret_decay_logit):
    depth = ret_decay_logit.shape[0]
    lg = ret_decay_logit.astype(F32)
    lgl = jnp.broadcast_to(jnp.repeat(lg, HEAD_DIM, axis=-1)[:, :, None, :], (depth, 2, SUBLANES, RET_WIDTH))
    lgr = jnp.broadcast_to(jnp.repeat(lg, CHUNK, axis=-1)[:, :, None, :], (depth, 2, CHUNK, RET_HEADS * CHUNK))
    return pl.pallas_call(
        _tables_kernel,
        grid=(depth,),
        in_specs=[
            pl.BlockSpec((1, 2, SUBLANES, RET_WIDTH), lambda l: (l, 0, 0, 0)),
            pl.BlockSpec((1, 2, CHUNK, RET_HEADS * CHUNK), lambda l: (l, 0, 0, 0)),
        ],
        out_specs=[
            pl.BlockSpec((1, 2, CHUNK, RET_WIDTH), lambda l: (l, 0, 0, 0)),
            pl.BlockSpec((1, 2, CHUNK, RET_WIDTH), lambda l: (l, 0, 0, 0)),
            pl.BlockSpec((1, 2, SUBLANES, RET_WIDTH), lambda l: (l, 0, 0, 0)),
            pl.BlockSpec((1, 2, CHUNK, RET_HEADS * CHUNK), lambda l: (l, 0, 0, 0)),
        ],
        out_shape=[
            jax.ShapeDtypeStruct((depth, 2, CHUNK, RET_WIDTH), F32),
            jax.ShapeDtypeStruct((depth, 2, CHUNK, RET_WIDTH), F32),
            jax.ShapeDtypeStruct((depth, 2, SUBLANES, RET_WIDTH), F32),
            jax.ShapeDtypeStruct((depth, 2, CHUNK, RET_HEADS * CHUNK), F32),
        ],
        compiler_params=_params("parallel"),
        name="decay_tables",
    )(lgl, lgr)


def _rope(v, cs, sn):
    lane = lax.broadcasted_iota(I32, (1, LANES), 1)
    first = (lane & 31) < 16
    outs = []
    for g in range(v.shape[1] // LANES):
        vg = v[:, g * LANES:(g + 1) * LANES]
        sw = jnp.where(first, pltpu.roll(vg, LANES - 16, 1), pltpu.roll(vg, 16, 1))
        outs.append(vg * cs + sw * sn)
    return outs[0] if len(outs) == 1 else jnp.concatenate(outs, axis=1)


def _swap_halves(v):
    return jnp.concatenate([v[:, HEAD_DIM:], v[:, :HEAD_DIM]], axis=1)


def _from_gather_layout(f_ref, tm):
    return jnp.concatenate(
        [f_ref[0, pl.ds(q, tm, stride=SUBLANES), :] for q in range(SUBLANES)], axis=1)


def _rms(v):
    return v * lax.rsqrt(jnp.mean(v * v, axis=-1, keepdims=True) + EPS)


def _mod_rows(mod_ref, tm, n_ctx, tile):
    m = mod_ref[0]
    if n_ctx % tm == 0:
        is_ctx = tile < n_ctx // tm
    else:
        is_ctx = tile * tm + lax.broadcasted_iota(I32, (tm, 1), 0) < n_ctx

    def row(r):
        return jnp.where(is_ctx, m[M_CTX + r:M_CTX + r + 1, :], m[r:r + 1, :])

    return row


def _inproj_kernel(has_ffn, n_ctx, *refs):
    if has_ffn:
        x_ref, f_ref, mod_ref, g_ref, w_ref, cs_ref, sn_ref = refs[:7]
        outs = refs[7:]
        xo_ref = outs[0]
        outs = outs[1:]
    else:
        x_ref, mod_ref, g_ref, w_ref, cs_ref, sn_ref = refs[:6]
        outs = refs[6:]
    cz_ref, rqkv_ref, gates_ref, aq_ref, kv_ref = outs
    tm = x_ref.shape[1]
    row = _mod_rows(mod_ref, tm, n_ctx, pl.program_id(1))

    x = x_ref[0]
    if has_ffn:
        x = x + row(M_G2) * _from_gather_layout(f_ref, tm)
        xo_ref[0] = x
    h = (_rms(x) * g_ref[...]) * (1.0 + row(M_SC1)) + row(M_SH1)
    h = h.astype(BF16)
    cs = cs_ref[...]
    sn = sn_ref[...]

    def proj(a, b):
        return _dot(h, w_ref[:, a:b])

    bounds = (O_CB, O_CC, O_CX, O_RQ, O_RK, O_RV, O_GF, O_GB, O_AQ, O_AK, O_AV, IN_WIDTH)
    p = [proj(a, b) for a, b in zip(bounds[:-1], bounds[1:])]
    cz_ref[0, :, 0:CONV_WIDTH] = p[0]
    cz_ref[0, :, CONV_WIDTH:] = p[1] * p[2]
    rqkv_ref[0, :, 0:RET_WIDTH] = _rope(p[3], cs, sn).astype(BF16)
    rqkv_ref[0, :, RET_WIDTH:2 * RET_WIDTH] = _rope(p[4] * QK_SCALE, cs, sn).astype(BF16)
    rqkv_ref[0, :, 2 * RET_WIDTH:] = p[5].astype(BF16)
    gates_ref[0, :, 0:RET_WIDTH] = _silu(p[6])
    gates_ref[0, :, RET_WIDTH:] = _silu(p[7])
    aq_ref[0] = (_rope(p[8], cs, sn) * (QK_SCALE * LOG2E)).astype(BF16)
    ak = _rope(p[9], cs, sn)
    av = p[10]
    kv_ref[0, :, 0:KV_WIDTH] = ak.astype(BF16)
    kv_ref[0, :, KV_WIDTH:2 * KV_WIDTH] = _swap_halves(ak).astype(BF16)
    kv_ref[0, :, 2 * KV_WIDTH:] = av.astype(BF16)


def _layer_block(stacked, layer):
    rest = stacked.shape[1:]
    return pl.BlockSpec((None,) + rest, lambda *_: (layer,) + (0,) * len(rest))


def _inproj(x, ffn, mrows, norm_g, w_in_bf, layer, rope_cs, rope_sn, n_ctx):
    b, tu, d = x.shape
    tm = INPROJ_TILE
    has_ffn = ffn is not None
    tok = lambda width: pl.BlockSpec((1, tm, width), lambda i, j: (i, j, 0))
    in_specs = [tok(d)]
    args = [x]
    if has_ffn:
        in_specs.append(pl.BlockSpec((1, tm * SUBLANES, LANES), lambda i, j: (i, j, 0)))
        args.append(ffn)
    in_specs += [
        pl.BlockSpec((1, M_ROWS, d), lambda i, j: (i, 0, 0)),
        _layer_block(norm_g, layer),
        pl.BlockSpec((None, d, IN_WIDTH), lambda i, j: (layer, 0, 0), pipeline_mode=pl.Buffered(1)),
        pl.BlockSpec((tm, LANES), lambda i, j: (j, 0)),
        pl.BlockSpec((tm, LANES), lambda i, j: (j, 0)),
    ]
    args += [mrows, norm_g, w_in_bf, rope_cs, rope_sn]
    widths = [(2 * CONV_WIDTH, F32), (3 * RET_WIDTH, BF16), (2 * RET_WIDTH, F32), (ATTN_WIDTH, BF16),
              (3 * KV_WIDTH, BF16)]
    out_specs = [tok(w) for w, _ in widths]
    out_shape = [jax.ShapeDtypeStruct((b, tu, w), dt) for w, dt in widths]
    if has_ffn:
        out_specs = [tok(d)] + out_specs
        out_shape = [jax.ShapeDtypeStruct((b, tu, d), F32)] + out_shape
    res = pl.pallas_call(
        functools.partial(_inproj_kernel, has_ffn, n_ctx),
        grid=(b, tu // tm),
        in_specs=in_specs,
        out_specs=out_specs,
        out_shape=out_shape,
        compiler_params=_params("parallel", "parallel"),
        name="inproj",
    )(*args)
    if has_ffn:
        return res[0], res[1:]
    return x, res


def _head_block_mask(n):
    r = lax.broadcasted_iota(I32, (n, n), 0) // HEAD_DIM
    c = lax.broadcasted_iota(I32, (n, n), 1) // HEAD_DIM
    return r == c


STATE_UNROLL = 17


def _states_kernel(n_ctx_chunks, n_chunks, rqkv_ref, win_ref, cd_ref, sp_ref, s_scr):
    same_head = _head_block_mask(RET_WIDTH)
    s_scr[...] = jnp.zeros_like(s_scr)

    def chunk_update(d, pos):
        if d == 0:
            c = pos
        else:
            c = jnp.where(pos < n_ctx_chunks, n_ctx_chunks - 1 - pos, n_chunks + n_ctx_chunks - 1 - pos)
        off = pl.multiple_of(c * CHUNK, CHUNK)
        kw = rqkv_ref[0, pl.ds(off, CHUNK), RET_WIDTH:2 * RET_WIDTH].astype(F32) * win_ref[d]
        v = rqkv_ref[0, pl.ds(off, CHUNK), 2 * RET_WIDTH:]
        kw_t = kw.T.astype(BF16)
        yield
        u = _dot(kw_t, v)
        yield
        return c, jnp.where(same_head, u, 0.0)

    def body(i, carry):
        todo = [(d, i * STATE_UNROLL + n) for n in range(STATE_UNROLL) for d in range(2)]
        done = _round_robin([chunk_update(d, pos) for d, pos in todo])
        for (d, _), (c, u) in zip(todo, done):
            s = s_scr[d]
            sp_ref[0, d, pl.ds(c, 1)] = s.astype(BF16)[None]
            s_scr[d] = s * cd_ref[d][0:1, :] + u
        return carry

    assert n_chunks % STATE_UNROLL == 0
    lax.fori_loop(0, n_chunks // STATE_UNROLL, body, 0)


def _ret_states(rqkv, win, cd, layer, n_ctx):
    b, tu, _ = rqkv.shape
    w = RET_WIDTH
    n_chunks = tu // CHUNK
    return pl.pallas_call(
        functools.partial(_states_kernel, n_ctx // CHUNK, n_chunks),
        grid=(b,),
        in_specs=[
            pl.BlockSpec((1, tu, 3 * w), lambda i: (i, 0, 0)),
            _layer_block(win, layer),
            _layer_block(cd, layer),
        ],
        out_specs=pl.BlockSpec((1, 2, n_chunks, w, w), lambda i: (i, 0, 0, 0, 0)),
        out_shape=jax.ShapeDtypeStruct((b, 2, n_chunks, w, w), BF16),
        scratch_shapes=[pltpu.VMEM((2, w, w), F32)],
        compiler_params=_params("parallel"),
        name="ret_states",
    )(rqkv, win, cd)


def _group_mean(t, ones_bd):
    hi = t.astype(BF16)
    lo = (t - hi.astype(F32)).astype(BF16)
    s = _dot(jnp.concatenate([hi, lo], axis=0), ones_bd)
    n = t.shape[0]
    return (s[:n] + s[n:]) * (1.0 / HEAD_DIM)


def _mix_chunk(c, n_ctx_chunks, n_chunks, cz, z_last, z_first, cw, rqkv, gates, sp, dm_ref, qdec_ref,
               qa, kvs, sink_ref):
    is_lat = c >= n_ctx_chunks

    z = cz[:, CONV_WIDTH:]
    row = lax.broadcasted_iota(I32, (CHUNK, 1), 0)
    has_prev = jnp.logical_and(c != 0, c != n_ctx_chunks)
    has_next = jnp.logical_and(c != n_ctx_chunks - 1, c != n_chunks - 1)
    z_before = jnp.where(row == 0, jnp.where(has_prev, z_last, 0.0), pltpu.roll(z, 1, 0))
    z_after = jnp.where(row == CHUNK - 1, jnp.where(has_next, z_first, 0.0), pltpu.roll(z, CHUNK - 1, 0))
    conv = cz[:, :CONV_WIDTH] * (z_before * cw[0:1, :] + z * cw[1:2, :] + z_after * cw[2:3, :])

    q = rqkv[:, 0:RET_WIDTH]
    k = rqkv[:, RET_WIDTH:2 * RET_WIDTH]
    v = rqkv[:, 2 * RET_WIDTH:]
    lane_head = lax.broadcasted_iota(I32, (1, RET_WIDTH), 1) // HEAD_DIM
    kz = jnp.zeros_like(k)
    k_heads = jnp.concatenate([jnp.where(lane_head == hh, k, kz) for hh in range(RET_HEADS)], axis=0)
    v_heads = jnp.concatenate([jnp.where(lane_head == hh, v, kz) for hh in range(RET_HEADS)], axis=0)
    scores = _dot_nt(q, k_heads)

    keys, v_t = kvs
    half = lax.broadcasted_iota(I32, (1, LANES), 1) // HEAD_DIM
    qa_z = jnp.zeros((CHUNK, LANES), BF16)
    combos = [(hk, par) for hk in range(ATTN_KV_HEADS) for par in range(2)]
    n_ctx = keys[0].shape[0] - 3 * CHUNK
    cols2 = 2 * CHUNK
    ik = lax.broadcasted_iota(I32, (CHUNK, cols2), 0)
    iq = lax.broadcasted_iota(I32, (CHUNK, cols2), 1) & (CHUNK - 1)
    off = jnp.full((CHUNK, cols2), NEG_INF, F32)
    bias_prev = jnp.where(jnp.logical_and(jnp.logical_and(is_lat, c - 1 >= n_ctx_chunks), ik >= iq), 0.0, off)
    bias_cur = jnp.where(is_lat, 0.0, off)
    bias_next = jnp.where(jnp.logical_and(jnp.logical_and(is_lat, c + 1 <= n_chunks - 1), ik <= iq), 0.0, off)
    assert n_ctx > 0
    logits = {}
    for hk, par in combos:
        ja, jb = 2 * hk, 2 * hk + 1
        qst = jnp.concatenate([
            jnp.where(half == par, qa[:, ja * LANES:(ja + 1) * LANES], qa_z),
            jnp.where(half == par, qa[:, jb * LANES:(jb + 1) * LANES], qa_z)], axis=0)
        sel = 0 if par == hk else 1
        r = _dot_nt(keys[sel], qst)
        s = jnp.concatenate([r[:CHUNK] + bias_prev, r[CHUNK:2 * CHUNK] + bias_cur,
                             r[2 * CHUNK:3 * CHUNK] + bias_next, r[3 * CHUNK:]], axis=0)
        ha, hb = ATTN_GROUP * hk + par, ATTN_GROUP * hk + par + 2
        snk = jnp.concatenate([sink_ref[ha * CHUNK:ha * CHUNK + 1, :],
                               sink_ref[hb * CHUNK:hb * CHUNK + 1, :]], axis=1) * LOG2E
        logits[hk, par] = (s, snk, jnp.maximum(jnp.max(s, axis=0, keepdims=True), snk))
    yield

    qf = q.astype(F32)
    outs = []
    for d in range(2):
        p = (scores * dm_ref[d]).astype(BF16)
        lhs = jnp.concatenate([p, (qf * qdec_ref[d]).astype(BF16)], axis=1)
        o = _dot(lhs, jnp.concatenate([v_heads, sp[d]], axis=0))
        outs.append(o)
    o2 = jnp.concatenate(outs, axis=0)
    yield

    att = {}
    ones_keys = jnp.ones((BF16_ROWS, keys[0].shape[0]), BF16)
    for n, (hk, par) in enumerate(combos):
        s, snk, mx = logits[hk, par]
        e = jnp.exp2(s - mx).astype(BF16)
        o_t = _dot(jnp.concatenate([v_t[hk * HEAD_DIM:(hk + 1) * HEAD_DIM, :], ones_keys], axis=0), e)
        den = o_t[HEAD_DIM:HEAD_DIM + 1, :] + jnp.exp2(snk - mx)
        att[hk, par] = o_t[:HEAD_DIM, :] * (1.0 / den)
        if n == 0:
            ones_bd = jnp.where(_head_block_mask(RET_WIDTH), 1.0, 0.0).astype(BF16)
            dl = o2 - _group_mean(o2, ones_bd)
            on = dl * lax.rsqrt(_group_mean(dl * dl, ones_bd) + EPS)
            ret = on[:CHUNK] * gates[:, 0:RET_WIDTH] + on[CHUNK:] * gates[:, RET_WIDTH:]
        yield

    cols = []
    for hk in range(ATTN_KV_HEADS):
        cols.append(jnp.concatenate([att[hk, 0][:, :CHUNK], att[hk, 1][:, :CHUNK]], axis=0).T)
        cols.append(jnp.concatenate([att[hk, 0][:, CHUNK:], att[hk, 1][:, CHUNK:]], axis=0).T)
    return jnp.concatenate([conv, ret] + cols, axis=1)


def _round_robin(gens):
    results = [None] * len(gens)
    active = list(range(len(gens)))
    while active:
        for i in list(active):
            try:
                next(gens[i])
            except StopIteration as done:
                results[i] = done.value
                active.remove(i)
    return results


MIX_CHUNKS = 2


def _mixer_kernel(n_ctx_chunks, n_chunks, n_exp,
                  cz_ref, czp_ref, czn_ref, cw_ref, rqkv_ref, gates_ref, sp_ref, dm_ref, qdec_ref,
                  aq_ref, kvp_ref, kvc_ref, kvn_ref, kvx_ref, sink_ref,
                  x_ref, mod_ref, g_ref, w_ref, wrh_ref, wrl_ref, xo_ref, hg_ref, aff_ref, mix_scr):
    t = pl.program_id(1)
    n_steps = pl.num_programs(1) - 1

    @pl.when(t == 0)
    def _():
        mix_scr[...] = jnp.zeros_like(mix_scr)

    tail = _outproj_tail(n_ctx_chunks * CHUNK, n_exp, jnp.maximum(t - 1, 0), mix_scr[...], x_ref, mod_ref, g_ref,
                         w_ref, wrh_ref, wrl_ref, xo_ref, hg_ref, aff_ref)
    j = jnp.minimum(t, n_steps - 1)
    cw = cw_ref[...]

    def values_t(blk):
        return blk[:, 2 * KV_WIDTH:].astype(F32).T.astype(BF16)

    kv_blocks = ([kvp_ref[0]] + [kvc_ref[0, s * CHUNK:(s + 1) * CHUNK, :] for s in range(MIX_CHUNKS)] + [kvn_ref[0]])
    kv_ctx = [kvx_ref[0, t * CHUNK:(t + 1) * CHUNK, :] for t in range(kvx_ref.shape[1] // CHUNK)]
    vt_blocks = [values_t(blk) for blk in kv_blocks]
    vt_ctx = [values_t(blk) for blk in kv_ctx]
    gens = []
    for s in range(MIX_CHUNKS):
        lo, hi = s * CHUNK, (s + 1) * CHUNK
        if s == 0:
            z_last = czp_ref[0, SUBLANES - 1:SUBLANES, CONV_WIDTH:]
        else:
            z_last = cz_ref[0, lo - 1:lo, CONV_WIDTH:]
        if s == MIX_CHUNKS - 1:
            z_first = czn_ref[0, 0:1, CONV_WIDTH:]
        else:
            z_first = cz_ref[0, hi:hi + 1, CONV_WIDTH:]
        window = kv_blocks[s:s + 3] + kv_ctx
        keys = tuple(jnp.concatenate([blk[:, i * KV_WIDTH:(i + 1) * KV_WIDTH] for blk in window], axis=0)
                     for i in (0, 1))
        v_t = jnp.concatenate(vt_blocks[s:s + 3] + vt_ctx, axis=1)
        gens.append(_mix_chunk(j * MIX_CHUNKS + s, n_ctx_chunks, n_chunks, cz_ref[0, lo:hi, :], z_last, z_first,
                               cw, rqkv_ref[0, lo:hi, :], gates_ref[0, lo:hi, :],
                               (sp_ref[0, 0, s], sp_ref[0, 1, s]), dm_ref, qdec_ref, aq_ref[0, lo:hi, :],
                               (keys, v_t), sink_ref))
    mixes = [mix.astype(BF16) for mix in _round_robin([tail] + gens)[1:]]
    mix_scr[...] = jnp.concatenate(mixes, axis=0)


def _mixer(proj, sp, tabs, conv_w, sink_rows, x, mrows, norm_g, w_out_bf, layer, wr_hi, wr_lo, n_ctx):
    cz, rqkv, gates, aq, kv = proj
    dm, qdec = tabs
    b, tu, _ = cz.shape
    d = x.shape[2]
    n_exp = wr_hi.shape[1]
    n_chunks = tu // CHUNK
    mc = MIX_CHUNKS
    rows = mc * CHUNK
    assert n_chunks % mc == 0 and (n_ctx // CHUNK) % mc == 0
    last = n_chunks - 1
    n_steps = n_chunks // mc
    per8 = rows // SUBLANES
    mixed = lambda t: jnp.minimum(t, n_steps - 1)
    projected = lambda t: jnp.maximum(t - 1, 0)
    cur = lambda w: pl.BlockSpec((1, rows, w), lambda i, t: (i, mixed(t), 0))
    out = lambda r, w: pl.BlockSpec((1, r, w), lambda i, t: (i, projected(t), 0))
    wcz, wkv = cz.shape[2], kv.shape[2]
    in_specs = [
        cur(wcz),
        pl.BlockSpec((1, SUBLANES, wcz), lambda i, t: (i, jnp.maximum(mixed(t) * per8 - 1, 0), 0)),
        pl.BlockSpec((1, SUBLANES, wcz),
                     lambda i, t: (i, jnp.minimum((mixed(t) + 1) * per8, tu // SUBLANES - 1), 0)),
        _layer_block(conv_w, layer),
        cur(rqkv.shape[2]), cur(gates.shape[2]),
        pl.BlockSpec((1, 2, mc, RET_WIDTH, RET_WIDTH), lambda i, t: (i, 0, mixed(t), 0, 0)),
        _layer_block(dm, layer), _layer_block(qdec, layer),
        cur(ATTN_WIDTH),
        pl.BlockSpec((1, CHUNK, wkv), lambda i, t: (i, jnp.maximum(mixed(t) * mc - 1, 0), 0)),
        cur(wkv),
        pl.BlockSpec((1, CHUNK, wkv), lambda i, t: (i, jnp.minimum((mixed(t) + 1) * mc, last), 0)),
        pl.BlockSpec((1, n_ctx, wkv), lambda i, t: (i, 0, 0)),
        _layer_block(sink_rows, layer),
        out(rows, d),
        pl.BlockSpec((1, M_ROWS, d), lambda i, t: (i, 0, 0)),
        _layer_block(norm_g, layer),
        _layer_block(w_out_bf, layer),
        _layer_block(wr_hi, layer), _layer_block(wr_lo, layer),
    ]
    args = [cz, cz, cz, conv_w, rqkv, gates, sp, dm, qdec, aq, kv, kv, kv, kv, sink_rows,
            x, mrows, norm_g, w_out_bf, wr_hi, wr_lo]
    return pl.pallas_call(
        functools.partial(_mixer_kernel, n_ctx // CHUNK, n_chunks, n_exp),
        grid=(b, n_steps + 1),
        in_specs=in_specs,
        out_specs=[out(rows, d), out(rows * SUBLANES, LANES),
                   pl.BlockSpec((1, n_exp, rows), lambda i, t: (i, 0, projected(t)))],
        out_shape=[
            jax.ShapeDtypeStruct((b, tu, d), F32),
            jax.ShapeDtypeStruct((b, tu * SUBLANES, LANES), F32),
            jax.ShapeDtypeStruct((b, n_exp, tu), F32),
        ],
        scratch_shapes=[pltpu.VMEM((rows, CONV_WIDTH + RET_WIDTH + ATTN_WIDTH), BF16)],
        compiler_params=_params("parallel", "arbitrary"),
        name="mixer",
    )(*args)


def _outproj_tail(n_ctx, n_exp, tile, mix, x_ref, mod_ref, g_ref, w_ref, wrh_ref, wrl_ref, xo_ref, hg_ref, aff_ref):
    tm = x_ref.shape[1]
    row = _mod_rows(mod_ref, tm, n_ctx, tile)

    y = _dot(mix, w_ref[...])
    yield
    x = x_ref[0] + row(M_G1) * y
    xo_ref[0] = x
    h = (_rms(x) * g_ref[...]) * (1.0 + row(M_SC2)) + row(M_SH2)
    for q in range(SUBLANES):
        hg_ref[0, pl.ds(q, tm, stride=SUBLANES), :] = h[:, q * LANES:(q + 1) * LANES]
    yield
    h_hi = h.astype(BF16)
    h_lo = (h - h_hi.astype(F32)).astype(BF16)
    logits = (_dot_nt(wrh_ref[...], h_hi) + (_dot_nt(wrh_ref[...], h_lo) + _dot_nt(wrl_ref[...], h_hi)))
    yield
    e = jnp.exp(logits - jnp.max(logits, axis=0, keepdims=True))
    aff_ref[0] = e / jnp.sum(e, axis=0, keepdims=True)


MIN_NORMAL_F32_BITS = 0x00800000
ROUTE_TABLES = 4
ROUTE_GROUP = 4


def _route_stream_tables(v, k, tab_scr, ce_scr, cnt_scr, stream, thr):
    n_exp, n = v.shape
    lane = lax.broadcasted_iota(I32, (1, LANES), 1)
    tri_r = lax.broadcasted_iota(I32, (LANES, LANES), 0)
    tri_c = lax.broadcasted_iota(I32, (LANES, LANES), 1)
    upper = jnp.where(tri_r <= tri_c, 1.0, 0.0).astype(BF16)
    above = pltpu.bitcast(jnp.maximum(thr + 1, MIN_NORMAL_F32_BITS), F32)
    floor = pltpu.bitcast(thr, F32)
    n_gt = jnp.sum(jnp.where(v >= above, 1, 0), axis=1, keepdims=True)
    need = (k - n_gt).astype(F32)
    seen_eq = jnp.zeros((n_exp, 1), F32)
    counts = jnp.zeros((n_exp, LANES), F32)
    for j in range(n // LANES):
        vb = v[:, j * LANES:(j + 1) * LANES]
        gt = vb >= above
        eq = jnp.logical_and(vb >= floor, vb < above)
        both = jnp.concatenate([jnp.where(gt, 1.0, 0.0), jnp.where(eq, 1.0, 0.0)], axis=0).astype(BF16)
        pref = _dot(both, upper)
        rank_eq = pref[n_exp:] + seen_eq
        inc = pref[:n_exp] + jnp.minimum(rank_eq, need) - jnp.minimum(seen_eq, need)
        sel = jnp.logical_or(gt, jnp.logical_and(eq, rank_eq <= need))
        seen_eq = rank_eq[:, LANES - 1:LANES]
        counts = counts + jnp.where(lane == j, inc[:, LANES - 1:LANES], 0.0)
        a0 = jnp.where(sel, vb, 0.0)
        t0 = a0.astype(BF16).astype(F32)
        t1 = (a0 - t0).astype(BF16).astype(F32)
        t2 = a0 - t0 - t1
        for t, val in enumerate((inc, t0, t1, t2)):
            tab_scr[stream * ROUTE_TABLES + t, pl.ds(j, n_exp, stride=LANES), :] = val
    through = _dot(counts.astype(BF16), upper)
    for e in range(n_exp):
        ce_scr[stream, e] = through[e:e + 1, :]
        cnt_scr[stream, e] = counts[e:e + 1, :]


def _route_slots(e, stream, k, lo, tab_scr, ce_scr, cnt_scr):
    lane_f = lax.broadcasted_iota(I32, (1, LANES), 1).astype(F32)
    slot = lax.broadcasted_iota(I32, (k, LANES), 0).astype(F32)
    ones_rows = jnp.ones((LANES, LANES), BF16)
    cnt_rows = jnp.broadcast_to(cnt_scr[stream, e], (LANES, LANES)).astype(BF16)
    rows_e = pl.ds(pl.multiple_of(e * LANES, LANES), LANES)
    tab = jnp.concatenate([tab_scr[stream * ROUTE_TABLES + t, rows_e, :] for t in range(ROUTE_TABLES)],
                          axis=1).astype(BF16)
    before = jnp.where(ce_scr[stream, e] <= slot, 1.0, 0.0).astype(BF16)
    blk = _dot_nt(before, ones_rows)
    base = _dot_nt(before, cnt_rows)
    yield
    row = _dot(jnp.where(lane_f == blk, 1.0, 0.0).astype(BF16), tab)
    yield
    inc = row[:, :LANES]
    aff = row[:, LANES:2 * LANES] + row[:, 2 * LANES:3 * LANES] + row[:, 3 * LANES:]
    local = slot - base
    pos = _dot_nt(jnp.where(inc <= local, 1.0, 0.0).astype(BF16), ones_rows)
    yield
    tok = blk * LANES + pos + float(lo)
    gate = jnp.sum(jnp.where(inc == local + 1.0, aff, 0.0), axis=1, keepdims=True)
    return tok, gate


def _route_kernel(n_exp, streams, aff_ref, idx_ref, gate_ref, tab_scr, ce_scr, cnt_scr):
    a = aff_ref[0]
    tab_scr[...] = jnp.zeros_like(tab_scr)
    vs = [a[:, lo:lo + n] for lo, n, _, _ in streams]

    def enough(v, cand, k):
        return jnp.sum(jnp.where(v >= pltpu.bitcast(cand, F32), 1, 0), axis=1, keepdims=True) >= k

    def search(i, ts):
        out = []
        for t, v, (_, _, k, _) in zip(ts, vs, streams):
            hi = t | jnp.left_shift(jnp.int32(1), 30 - 2 * i)
            lo = jnp.left_shift(jnp.int32(1), 29 - 2 * i)
            out.append(jnp.where(enough(v, hi, k), jnp.where(enough(v, hi | lo, k), hi | lo, hi),
                                 jnp.where(enough(v, t | lo, k), t | lo, t)))
        return tuple(out)

    thrs = lax.fori_loop(0, 15, search, tuple(jnp.zeros((n_exp, 1), I32) for _ in streams))
    thrs = tuple(jnp.where(enough(v, t | 1, k), t | 1, t) for t, v, (_, _, k, _) in zip(thrs, vs, streams))
    for st, (v, (_, _, k, _)) in enumerate(zip(vs, streams)):
        _route_stream_tables(v, k, tab_scr, ce_scr, cnt_scr, st, thrs[st])

    slots = idx_ref.shape[3]

    assert [s[3] for s in streams] == [sum(s[2] for s in streams[:n]) for n in range(len(streams))]

    def expert_group(g, carry):
        experts = [g * ROUTE_GROUP + u for u in range(ROUTE_GROUP)]
        found = _round_robin([_route_slots(e, st, k, lo, tab_scr, ce_scr, cnt_scr)
                              for e in experts for st, (lo, _, k, _) in enumerate(streams)])
        for u, e in enumerate(experts):
            toks = []
            for st, (_, _, k, slot0) in enumerate(streams):
                tok, gate = found[u * len(streams) + st]
                toks.append(tok)
                gate_ref[0, pl.ds(e, 1), slot0:slot0 + k, :] = gate[None]
            toks.append(jnp.zeros((-slots % LANES, LANES), F32))
            row = jnp.concatenate(toks, axis=0).T[0:1, :slots]
            idx_ref[0, pl.ds(e, 1)] = row.astype(I32)[None]
        return carry

    assert n_exp % ROUTE_GROUP == 0
    lax.fori_loop(0, n_exp // ROUTE_GROUP, expert_group, 0)


def _route(aff, n_ctx):
    b, n_exp, tu = aff.shape
    n_lat = tu - n_ctx
    cap_l = CAPACITY_FACTOR * n_lat // n_exp
    cap_c = CAPACITY_FACTOR * n_ctx // n_exp
    slots = cap_l + cap_c
    assert n_lat // LANES <= LANES and n_ctx // LANES <= LANES
    streams = ((n_ctx, n_lat, cap_l, 0), (0, n_ctx, cap_c, cap_l))
    return pl.pallas_call(
        functools.partial(_route_kernel, n_exp, streams),
        grid=(b,),
        in_specs=[pl.BlockSpec((1, n_exp, tu), lambda i: (i, 0, 0))],
        out_specs=[pl.BlockSpec((1, n_exp, 1, slots), lambda i: (i, 0, 0, 0)),
                   pl.BlockSpec((1, n_exp, slots, 1), lambda i: (i, 0, 0, 0))],
        out_shape=[jax.ShapeDtypeStruct((b, n_exp, 1, slots), I32),
                   jax.ShapeDtypeStruct((b, n_exp, slots, 1), F32)],
        scratch_shapes=[pltpu.VMEM((len(streams) * ROUTE_TABLES, n_exp * LANES, LANES), F32),
                        pltpu.VMEM((len(streams), n_exp, 1, LANES), F32),
                        pltpu.VMEM((len(streams), n_exp, 1, LANES), F32)],
        compiler_params=_params("parallel"),
        name="route",
    )(aff)


def _slot_pitch(slots):
    return slots + SUBLANES


def _zero_slot_padding(ref, lead, slots, pitch):
    for q in range(SUBLANES):
        ref[lead + (pl.ds(q * pitch + slots, pitch - slots), slice(None))] = jnp.zeros((pitch - slots, LANES), ref.dtype)


def _slot_pitch_bf16(slots):
    return -(-slots // BF16_ROWS) * BF16_ROWS + BF16_ROWS


DISPATCH_EXPERTS = 4


def _gather_kernel(slots, pitch, pitch_out, idx_ref, h_ref, o_ref, tile_scr):
    for g in range(DISPATCH_EXPERTS):
        for mi in range(slots):
            r = pl.multiple_of(idx_ref[g, 0, mi] * SUBLANES, SUBLANES)
            tile_scr[g, pl.ds(mi, SUBLANES, stride=pitch), :] = h_ref[0, pl.ds(r, SUBLANES), :]
        _zero_slot_padding(o_ref, (0, g), slots, pitch_out)
        for q in range(SUBLANES):
            o_ref[0, g, q * pitch_out:q * pitch_out + slots, :] = (
                tile_scr[g, q * pitch:q * pitch + slots, :].astype(BF16))


def _gather(hg, idx_rows, n_exp, slots):
    b = hg.shape[0]
    pitch = _slot_pitch(slots)
    pitch_out = _slot_pitch_bf16(slots)
    ge = DISPATCH_EXPERTS
    assert n_exp % ge == 0
    return pl.pallas_call(
        functools.partial(_gather_kernel, slots, pitch, pitch_out),
        grid=(b, n_exp // ge),
        in_specs=[
            pl.BlockSpec((ge, 1, slots), lambda i, e: (i * (n_exp // ge) + e, 0, 0), memory_space=pltpu.SMEM),
            pl.BlockSpec((1,) + hg.shape[1:], lambda i, e: (i, 0, 0)),
        ],
        out_specs=pl.BlockSpec((1, ge, SUBLANES * pitch_out, LANES), lambda i, e: (i, e, 0, 0)),
        out_shape=jax.ShapeDtypeStruct((b, n_exp, SUBLANES * pitch_out, LANES), BF16),
        scratch_shapes=[pltpu.VMEM((ge, SUBLANES * pitch, LANES), F32)],
        compiler_params=_params("parallel", "arbitrary"),
        name="gather",
    )(idx_rows, hg)


FFN_SAMPLES = 2
FFN_VMEM_LIMIT_BYTES = 60 * 1024 * 1024


def _ffn_kernel(slots, pitch_in, pitch, xs_ref, wg_ref, wu_ref, wd_ref, gate_ref, y_ref):
    n = xs_ref.shape[0]
    x = jnp.concatenate(
        [jnp.concatenate([xs_ref[s, 0, q * pitch_in:q * pitch_in + slots, :] for q in range(SUBLANES)], axis=1)
         for s in range(n)], axis=0)
    gate = jnp.concatenate([gate_ref[s, 0] for s in range(n)], axis=0)
    a = _dot(x, wg_ref[0].astype(BF16))
    u = _dot(x, wu_ref[0].astype(BF16))
    y = _dot((_silu(a) * u).astype(BF16), wd_ref[0].astype(BF16)) * gate
    for s in range(n):
        _zero_slot_padding(y_ref, (s, 0), slots, pitch)
        for q in range(SUBLANES):
            y_ref[s, 0, q * pitch:q * pitch + slots, :] = y[s * slots:(s + 1) * slots, q * LANES:(q + 1) * LANES]


def _ffn(xs, w_gate, w_up, w_down, layer, gate, slots):
    b, n_exp = xs.shape[:2]
    pitch = _slot_pitch(slots)
    pitch_in = _slot_pitch_bf16(slots)
    d, f = w_gate.shape[2:]
    ns = FFN_SAMPLES
    assert b % ns == 0
    slot_tile = lambda p: pl.BlockSpec((ns, 1, SUBLANES * p, LANES), lambda e, i: (i, e, 0, 0))
    weight = lambda rows, cols: pl.BlockSpec((None, 1, rows, cols), lambda e, i: (layer, e, 0, 0))
    return pl.pallas_call(
        functools.partial(_ffn_kernel, slots, pitch_in, pitch),
        grid=(n_exp, b // ns),
        in_specs=[
            slot_tile(pitch_in),
            weight(d, f), weight(d, f), weight(f, d),
            pl.BlockSpec((ns, 1, slots, 1), lambda e, i: (i, e, 0, 0)),
        ],
        out_specs=slot_tile(pitch),
        out_shape=jax.ShapeDtypeStruct((b, n_exp, SUBLANES * pitch, LANES), F32),
        compiler_params=_params("parallel", "arbitrary", vmem=FFN_VMEM_LIMIT_BYTES),
        name="ffn",
    )(xs, w_gate, w_up, w_down, gate)


SCATTER_BATCH = 16


def _scatter_kernel(slots, pitch, idx_ref, y_ref, o_ref):
    @pl.when(pl.program_id(1) == 0)
    def _():
        o_ref[...] = jnp.zeros_like(o_ref)

    for g in range(DISPATCH_EXPERTS):
        for m0 in range(0, slots, SCATTER_BATCH):
            rows = [pl.multiple_of(idx_ref[g, 0, m0 + u] * SUBLANES, SUBLANES) for u in range(SCATTER_BATCH)]
            vals = [o_ref[0, pl.ds(rows[u], SUBLANES), :] + y_ref[0, g, pl.ds(m0 + u, SUBLANES, stride=pitch), :]
                    for u in range(SCATTER_BATCH)]
            for u in range(SCATTER_BATCH):
                o_ref[0, pl.ds(rows[u], SUBLANES), :] = vals[u]


def _scatter(y, idx_rows, tu, slots):
    b, n_exp = y.shape[:2]
    pitch = _slot_pitch(slots)
    ge = DISPATCH_EXPERTS
    assert n_exp % ge == 0 and slots % SCATTER_BATCH == 0
    return pl.pallas_call(
        functools.partial(_scatter_kernel, slots, pitch),
        grid=(b, n_exp // ge),
        in_specs=[
            pl.BlockSpec((ge, 1, slots), lambda i, e: (i * (n_exp // ge) + e, 0, 0), memory_space=pltpu.SMEM),
            pl.BlockSpec((1, ge, SUBLANES * pitch, LANES), lambda i, e: (i, e, 0, 0)),
        ],
        out_specs=pl.BlockSpec((1, tu * SUBLANES, LANES), lambda i, e: (i, 0, 0)),
        out_shape=jax.ShapeDtypeStruct((b, tu * SUBLANES, LANES), F32),
        compiler_params=_params("parallel", "arbitrary"),
        name="scatter",
    )(idx_rows, y)


FINAL_PARTS = 4


def _final_kernel(mod_ref, g_ref, *refs):
    o_ref = refs[-1]
    tm = FINAL_TILE
    for n in range(FINAL_PARTS):
        x_ref, f_ref = refs[n], refs[FINAL_PARTS + n]
        x = x_ref[0] + mod_ref[0][M_G2:M_G2 + 1, :] * _from_gather_layout(f_ref, tm)
        o_ref[0, n * tm:(n + 1) * tm, :] = _rms(x) * g_ref[...]


def _final(x, ffn, mrows, final_g, n_ctx):
    b, tu, d = x.shape
    tm = FINAL_TILE
    skip = n_ctx // tm
    parts = FINAL_PARTS
    assert (tu - n_ctx) % (tm * parts) == 0
    tile = lambda n: pl.BlockSpec((1, tm, d), lambda i, j: (i, j * parts + n + skip, 0))
    ftile = lambda n: pl.BlockSpec((1, tm * SUBLANES, LANES), lambda i, j: (i, j * parts + n + skip, 0))
    return pl.pallas_call(
        _final_kernel,
        grid=(b, (tu - n_ctx) // (tm * parts)),
        in_specs=([pl.BlockSpec((1, M_ROWS, d), lambda i, j: (i, 0, 0)), pl.BlockSpec((1, d), lambda i, j: (0, 0))]
                  + [tile(n) for n in range(parts)] + [ftile(n) for n in range(parts)]),
        out_specs=pl.BlockSpec((1, tm * parts, d), lambda i, j: (i, j, 0)),
        out_shape=jax.ShapeDtypeStruct((b, tu - n_ctx, d), F32),
        compiler_params=_params("parallel", "parallel"),
        name="final_norm",
    )(mrows, final_g.reshape(1, d), *([x] * parts + [ffn] * parts))


def _rope_tables(n_lat, n_ctx):
    rows = n_lat // GRID_W
    rowp = jnp.repeat(jnp.arange(rows, dtype=F32), GRID_W)
    colp = jnp.tile(jnp.arange(GRID_W, dtype=F32), rows)
    axis_dim = HEAD_DIM // 2
    inv_freq = ROPE_BASE ** (-jnp.arange(0, axis_dim, 2, dtype=F32) / axis_dim)
    ar = rowp[:, None] * inv_freq
    ac = colp[:, None] * inv_freq
    cs = jnp.concatenate([jnp.cos(ar), jnp.cos(ar), jnp.cos(ac), jnp.cos(ac)], axis=1)
    sn = jnp.concatenate([-jnp.sin(ar), jnp.sin(ar), -jnp.sin(ac), jnp.sin(ac)], axis=1)
    reps = LANES // HEAD_DIM
    cs = jnp.concatenate([jnp.ones((n_ctx, LANES), F32), jnp.tile(cs, (1, reps))], axis=0)
    sn = jnp.concatenate([jnp.zeros((n_ctx, LANES), F32), jnp.tile(sn, (1, reps))], axis=0)
    return cs, sn


def kernel(x, c, ctx, c_ctx, w_mod, b_mod, norm1_g, norm2_g, w_in, conv_w, ret_decay_logit, attn_sink,
           w_out, w_router, w_gate, w_up, w_down, final_g):
    b, n_lat, d = x.shape
    n_ctx = ctx.shape[1]
    depth = w_in.shape[0]
    n_exp = w_router.shape[2]
    tu = n_ctx + n_lat
    assert w_in.shape[2] == IN_WIDTH and tu % INPROJ_TILE == 0
    assert n_ctx % FINAL_TILE == 0 and n_lat % FINAL_TILE == 0
    assert b + 1 <= SUBLANES and n_exp == N_EXPERTS

    c_rows = jnp.concatenate([c, c_ctx[None], jnp.zeros((SUBLANES - b - 1, d), F32)], axis=0)
    mods = _mod_vectors(c_rows, w_mod, b_mod).reshape(depth, SUBLANES, 6, d)
    qdec, win, cd, dm = _decay_tables(ret_decay_logit)
    rope_cs, rope_sn = _rope_tables(n_lat, n_ctx)
    sink_rows = jnp.broadcast_to(jnp.repeat(attn_sink.astype(F32), CHUNK, axis=1)[:, :, None],
                                 (depth, ATTN_HEADS * CHUNK, LANES))
    w_in_bf = w_in.astype(BF16)
    w_out_bf = w_out.astype(BF16)
    wr = jnp.swapaxes(w_router, 1, 2)
    wr_hi = wr.astype(BF16)
    wr_lo = (wr - wr_hi.astype(F32)).astype(BF16)
    g1 = norm1_g.reshape(depth, 1, d)
    g2 = norm2_g.reshape(depth, 1, d)
    mrows = jnp.concatenate([mods[:, :b], jnp.broadcast_to(mods[:, b][:, None], (depth, b, 6, d)),
                             jnp.zeros((depth, b, M_ROWS - 12, d), F32)], axis=2)
    mrows_in = mrows.at[1:, :, M_G2].set(mrows[:-1, :, M_G2])
    mrows_in = mrows_in.at[1:, :, M_CTX + M_G2].set(mrows[:-1, :, M_CTX + M_G2])

    xu = jnp.concatenate([ctx, x], axis=1)
    ffn = None
    cap = CAPACITY_FACTOR * n_lat // n_exp + CAPACITY_FACTOR * n_ctx // n_exp
    for l in range(depth):
        xu, proj = _inproj(xu, ffn, mrows_in[l], g1, w_in_bf, l, rope_cs, rope_sn, n_ctx)
        sp = _ret_states(proj[1], win, cd, l, n_ctx)
        xu, hg, aff = _mixer(proj, sp, (dm, qdec), conv_w, sink_rows, xu, mrows[l], g2, w_out_bf, l,
                             wr_hi, wr_lo, n_ctx)
        idx, gate = _route(aff, n_ctx)
        idx_rows = idx.reshape(b * n_exp, 1, cap)
        xs = _gather(hg, idx_rows, n_exp, cap)
        y = _ffn(xs, w_gate, w_up, w_down, l, gate, cap)
        ffn = _scatter(y, idx_rows, tu, cap)
    return _final(xu, ffn, mrows[depth - 1], final_g, n_ctx)
```

```python
import functools

import jax
import jax.numpy as jnp
from jax import lax
from jax.experimental import pallas as pl
from jax.experimental.pallas import tpu as pltpu

F32 = jnp.float32
BF16 = jnp.bfloat16
I32 = jnp.int32

HEAD_DIM = 64
CONV_WIDTH = 256
RET_HEADS = 4
RET_WIDTH = RET_HEADS * HEAD_DIM
ATTN_HEADS = 8
ATTN_KV_HEADS = 2
ATTN_GROUP = ATTN_HEADS // ATTN_KV_HEADS
ATTN_WIDTH = ATTN_HEADS * HEAD_DIM
KV_WIDTH = ATTN_KV_HEADS * HEAD_DIM
CHUNK = 128
GRID_W = 64
N_EXPERTS = 16
CAPACITY_FACTOR = 2
ROPE_BASE = 10000.0
EPS = 1e-6
NEG_INF = -1e30
QK_SCALE = HEAD_DIM ** -0.5
LOG2E = 1.4426950408889634

LANES = 128
SUBLANES = 8
BF16_ROWS = 2 * SUBLANES
VMEM_LIMIT_BYTES = 56 * 1024 * 1024

O_CB = 0
O_CC = O_CB + CONV_WIDTH
O_CX = O_CC + CONV_WIDTH
O_RQ = O_CX + CONV_WIDTH
O_RK = O_RQ + RET_WIDTH
O_RV = O_RK + RET_WIDTH
O_GF = O_RV + RET_WIDTH
O_GB = O_GF + RET_WIDTH
O_AQ = O_GB + RET_WIDTH
O_AK = O_AQ + ATTN_WIDTH
O_AV = O_AK + KV_WIDTH
IN_WIDTH = O_AV + KV_WIDTH

M_SH1, M_SC1, M_G1, M_SH2, M_SC2, M_G2 = range(6)
M_CTX = 6
M_ROWS = 16

INPROJ_TILE = 1088
FINAL_TILE = 256


def _params(*sem, vmem=VMEM_LIMIT_BYTES):
    return pltpu.CompilerParams(dimension_semantics=sem, vmem_limit_bytes=vmem)


def _dot(a, b):
    return jnp.dot(a, b, preferred_element_type=F32)


def _dot_nt(a, b):
    return lax.dot_general(a, b, (((1,), (1,)), ((), ())), preferred_element_type=F32)


def _silu(v):
    return v * jax.nn.sigmoid(v)


def _mod_kernel(c_ref, w_ref, b_ref, o_ref):
    s = _silu(c_ref[...])
    s_hi = s.astype(BF16)
    s_lo = (s - s_hi.astype(F32)).astype(BF16)
    w = w_ref[0]
    w_hi = w.astype(BF16)
    w_lo = (w - w_hi.astype(F32)).astype(BF16)
    n = s.shape[0]
    head = _dot(jnp.concatenate([s_hi, s_lo], axis=0), w_hi)
    o_ref[0] = head[:n] + (head[n:] + _dot(s_hi, w_lo)) + b_ref[0]


def _mod_vectors(c_rows, w_mod, b_mod):
    depth, d_model, width = w_mod.shape
    tn = 1536
    return pl.pallas_call(
        _mod_kernel,
        grid=(depth, width // tn),
        in_specs=[
            pl.BlockSpec((SUBLANES, d_model), lambda l, n: (0, 0)),
            pl.BlockSpec((1, d_model, tn), lambda l, n: (l, 0, n)),
            pl.BlockSpec((1, 1, tn), lambda l, n: (l, 0, n)),
        ],
        out_specs=pl.BlockSpec((1, SUBLANES, tn), lambda l, n: (l, 0, n)),
        out_shape=jax.ShapeDtypeStruct((depth, SUBLANES, width), F32),
        compiler_params=_params("parallel", "parallel"),
        name="mod_vectors",
    )(c_rows, w_mod, b_mod.reshape(depth, 1, width))


def _log_sigmoid(v):
    return -jnp.log(1.0 + jnp.exp(-v))


def _tables_kernel(lgl_ref, lgr_ref, qdec_ref, win_ref, cd_ref, dm_ref):
    pos = lax.broadcasted_iota(I32, (CHUNK, RET_WIDTH), 0).astype(F32)
    ri = lax.broadcasted_iota(I32, (CHUNK, RET_HEADS * CHUNK), 0).astype(F32)
    rj = (lax.broadcasted_iota(I32, (CHUNK, RET_HEADS * CHUNK), 1) & (CHUNK - 1)).astype(F32)
    for d in range(2):
        lg = _log_sigmoid(lgl_ref[0, d])
        lg1 = lg[0:1, :]
        if d == 0:
            qdec_ref[0, d] = jnp.exp(lg1 * (pos + 1.0))
            win_ref[0, d] = jnp.exp(lg1 * (CHUNK - 1.0 - pos))
            diff = ri - rj
        else:
            qdec_ref[0, d] = jnp.exp(lg1 * (CHUNK - pos))
            win_ref[0, d] = jnp.exp(lg1 * pos)
            diff = rj - ri
        cd_ref[0, d] = jnp.exp(lg * float(CHUNK))
        lr = _log_sigmoid(lgr_ref[0, d])
        dm_ref[0, d] = jnp.where(diff >= 0.0, jnp.exp(lr * jnp.maximum(diff, 0.0)), 0.0)


def _decay_tables(ret_decay_logit):
    depth = ret_decay_logit.shape[0]
    lg = ret_decay_logit.astype(F32)
    lgl = jnp.broadcast_to(jnp.repeat(lg, HEAD_DIM, axis=-1)[:, :, None, :], (depth, 2, SUBLANES, RET_WIDTH))
    lgr = jnp.broadcast_to(jnp.repeat(lg, CHUNK, axis=-1)[:, :, None, :], (depth, 2, CHUNK, RET_HEADS * CHUNK))
    return pl.pallas_call(
        _tables_kernel,
        grid=(depth,),
        in_specs=[
            pl.BlockSpec((1, 2, SUBLANES, RET_WIDTH), lambda l: (l, 0, 0, 0)),
            pl.BlockSpec((1, 2, CHUNK, RET_HEADS * CHUNK), lambda l: (l, 0, 0, 0)),
        ],
        out_specs=[
            pl.BlockSpec((1, 2, CHUNK, RET_WIDTH), lambda l: (l, 0, 0, 0)),
            pl.BlockSpec((1, 2, CHUNK, RET_WIDTH), lambda l: (l, 0, 0, 0)),
            pl.BlockSpec((1, 2, SUBLANES, RET_WIDTH), lambda l: (l, 0, 0, 0)),
            pl.BlockSpec((1, 2, CHUNK, RET_HEADS * CHUNK), lambda l: (l, 0, 0, 0)),
        ],
        out_shape=[
            jax.ShapeDtypeStruct((depth, 2, CHUNK, RET_WIDTH), F32),
            jax.ShapeDtypeStruct((depth, 2, CHUNK, RET_WIDTH), F32),
            jax.ShapeDtypeStruct((depth, 2, SUBLANES, RET_WIDTH), F32),
            jax.ShapeDtypeStruct((depth, 2, CHUNK, RET_HEADS * CHUNK), F32),
        ],
        compiler_params=_params("parallel"),
        name="decay_tables",
    )(lgl, lgr)


def _rope(v, cs, sn):
    lane = lax.broadcasted_iota(I32, (1, LANES), 1)
    first = (lane & 31) < 16
    outs = []
    for g in range(v.shape[1] // LANES):
        vg = v[:, g * LANES:(g + 1) * LANES]
        sw = jnp.where(first, pltpu.roll(vg, LANES - 16, 1), pltpu.roll(vg, 16, 1))
        outs.append(vg * cs + sw * sn)
    return outs[0] if len(outs) == 1 else jnp.concatenate(outs, axis=1)


def _swap_halves(v):
    return jnp.concatenate([v[:, HEAD_DIM:], v[:, :HEAD_DIM]], axis=1)


def _from_gather_layout(f_ref, tm):
    return jnp.concatenate(
        [f_ref[0, pl.ds(q, tm, stride=SUBLANES), :] for q in range(SUBLANES)], axis=1)


def _rms(v):
    return v * lax.rsqrt(jnp.mean(v * v, axis=-1, keepdims=True) + EPS)


def _mod_rows(mod_ref, tm, n_ctx, tile):
    m = mod_ref[0]
    if n_ctx % tm == 0:
        is_ctx = tile < n_ctx // tm
    else:
        is_ctx = tile * tm + lax.broadcasted_iota(I32, (tm, 1), 0) < n_ctx

    def row(r):
        return jnp.where(is_ctx, m[M_CTX + r:M_CTX + r + 1, :], m[r:r + 1, :])

    return row


def _inproj_kernel(has_ffn, n_ctx, *refs):
    if has_ffn:
        x_ref, f_ref, mod_ref, g_ref, w_ref, cs_ref, sn_ref = refs[:7]
        outs = refs[7:]
        xo_ref = outs[0]
        outs = outs[1:]
    else:
        x_ref, mod_ref, g_ref, w_ref, cs_ref, sn_ref = refs[:6]
        outs = refs[6:]
    cz_ref, rqkv_ref, gates_ref, aq_ref, kv_ref = outs
    tm = x_ref.shape[1]
    row = _mod_rows(mod_ref, tm, n_ctx, pl.program_id(1))

    x = x_ref[0]
    if has_ffn:
        x = x + row(M_G2) * _from_gather_layout(f_ref, tm)
        xo_ref[0] = x
    h = (_rms(x) * g_ref[...]) * (1.0 + row(M_SC1)) + row(M_SH1)
    h = h.astype(BF16)
    cs = cs_ref[...]
    sn = sn_ref[...]

    def proj(a, b):
        return _dot(h, w_ref[:, a:b])

    bounds = (O_CB, O_CC, O_CX, O_RQ, O_RK, O_RV, O_GF, O_GB, O_AQ, O_AK, O_AV, IN_WIDTH)
    p = [proj(a, b) for a, b in zip(bounds[:-1], bounds[1:])]
    cz_ref[0, :, 0:CONV_WIDTH] = p[0]
    cz_ref[0, :, CONV_WIDTH:] = p[1] * p[2]
    rqkv_ref[0, :, 0:RET_WIDTH] = _rope(p[3], cs, sn).astype(BF16)
    rqkv_ref[0, :, RET_WIDTH:2 * RET_WIDTH] = _rope(p[4] * QK_SCALE, cs, sn).astype(BF16)
    rqkv_ref[0, :, 2 * RET_WIDTH:] = p[5].astype(BF16)
    gates_ref[0, :, 0:RET_WIDTH] = _silu(p[6])
    gates_ref[0, :, RET_WIDTH:] = _silu(p[7])
    aq_ref[0] = (_rope(p[8], cs, sn) * (QK_SCALE * LOG2E)).astype(BF16)
    ak = _rope(p[9], cs, sn)
    av = p[10]
    kv_ref[0, :, 0:KV_WIDTH] = ak.astype(BF16)
    kv_ref[0, :, KV_WIDTH:2 * KV_WIDTH] = _swap_halves(ak).astype(BF16)
    kv_ref[0, :, 2 * KV_WIDTH:] = av.astype(BF16)


def _layer_block(stacked, layer):
    rest = stacked.shape[1:]
    return pl.BlockSpec((None,) + rest, lambda *_: (layer,) + (0,) * len(rest))


def _inproj(x, ffn, mrows, norm_g, w_in_bf, layer, rope_cs, rope_sn, n_ctx):
    b, tu, d = x.shape
    tm = INPROJ_TILE
    has_ffn = ffn is not None
    tok = lambda width: pl.BlockSpec((1, tm, width), lambda i, j: (i, j, 0))
    in_specs = [tok(d)]
    args = [x]
    if has_ffn:
        in_specs.append(pl.BlockSpec((1, tm * SUBLANES, LANES), lambda i, j: (i, j, 0)))
        args.append(ffn)
    in_specs += [
        pl.BlockSpec((1, M_ROWS, d), lambda i, j: (i, 0, 0)),
        _layer_block(norm_g, layer),
        pl.BlockSpec((None, d, IN_WIDTH), lambda i, j: (layer, 0, 0), pipeline_mode=pl.Buffered(1)),
        pl.BlockSpec((tm, LANES), lambda i, j: (j, 0)),
        pl.BlockSpec((tm, LANES), lambda i, j: (j, 0)),
    ]
    args += [mrows, norm_g, w_in_bf, rope_cs, rope_sn]
    widths = [(2 * CONV_WIDTH, F32), (3 * RET_WIDTH, BF16), (2 * RET_WIDTH, F32), (ATTN_WIDTH, BF16),
              (3 * KV_WIDTH, BF16)]
    out_specs = [tok(w) for w, _ in widths]
    out_shape = [jax.ShapeDtypeStruct((b, tu, w), dt) for w, dt in widths]
    if has_ffn:
        out_specs = [tok(d)] + out_specs
        out_shape = [jax.ShapeDtypeStruct((b, tu, d), F32)] + out_shape
    res = pl.pallas_call(
        functools.partial(_inproj_kernel, has_ffn, n_ctx),
        grid=(b, tu // tm),
        in_specs=in_specs,
        out_specs=out_specs,
        out_shape=out_shape,
        compiler_params=_params("parallel", "parallel"),
        name="inproj",
    )(*args)
    if has_ffn:
        return res[0], res[1:]
    return x, res


def _head_block_mask(n):
    r = lax.broadcasted_iota(I32, (n, n), 0) // HEAD_DIM
    c = lax.broadcasted_iota(I32, (n, n), 1) // HEAD_DIM
    return r == c


STATE_UNROLL = 17


def _states_kernel(n_ctx_chunks, n_chunks, rqkv_ref, win_ref, cd_ref, sp_ref, s_scr):
    same_head = _head_block_mask(RET_WIDTH)
    s_scr[...] = jnp.zeros_like(s_scr)

    def chunk_update(d, pos):
        if d == 0:
            c = pos
        else:
            c = jnp.where(pos < n_ctx_chunks, n_ctx_chunks - 1 - pos, n_chunks + n_ctx_chunks - 1 - pos)
        off = pl.multiple_of(c * CHUNK, CHUNK)
        kw = rqkv_ref[0, pl.ds(off, CHUNK), RET_WIDTH:2 * RET_WIDTH].astype(F32) * win_ref[d]
        v = rqkv_ref[0, pl.ds(off, CHUNK), 2 * RET_WIDTH:]
        kw_t = kw.T.astype(BF16)
        yield
        u = _dot(kw_t, v)
        yield
        return c, jnp.where(same_head, u, 0.0)

    def body(i, carry):
        todo = [(d, i * STATE_UNROLL + n) for n in range(STATE_UNROLL) for d in range(2)]
        done = _round_robin([chunk_update(d, pos) for d, pos in todo])
        for (d, _), (c, u) in zip(todo, done):
            s = s_scr[d]
            sp_ref[0, d, pl.ds(c, 1)] = s.astype(BF16)[None]
            s_scr[d] = s * cd_ref[d][0:1, :] + u
        return carry

    assert n_chunks % STATE_UNROLL == 0
    lax.fori_loop(0, n_chunks // STATE_UNROLL, body, 0)


def _ret_states(rqkv, win, cd, layer, n_ctx):
    b, tu, _ = rqkv.shape
    w = RET_WIDTH
    n_chunks = tu // CHUNK
    return pl.pallas_call(
        functools.partial(_states_kernel, n_ctx // CHUNK, n_chunks),
        grid=(b,),
        in_specs=[
            pl.BlockSpec((1, tu, 3 * w), lambda i: (i, 0, 0)),
            _layer_block(win, layer),
            _layer_block(cd, layer),
        ],
        out_specs=pl.BlockSpec((1, 2, n_chunks, w, w), lambda i: (i, 0, 0, 0, 0)),
        out_shape=jax.ShapeDtypeStruct((b, 2, n_chunks, w, w), BF16),
        scratch_shapes=[pltpu.VMEM((2, w, w), F32)],
        compiler_params=_params("parallel"),
        name="ret_states",
    )(rqkv, win, cd)


def _group_mean(t, ones_bd):
    hi = t.astype(BF16)
    lo = (t - hi.astype(F32)).astype(BF16)
    s = _dot(jnp.concatenate([hi, lo], axis=0), ones_bd)
    n = t.shape[0]
    return (s[:n] + s[n:]) * (1.0 / HEAD_DIM)


def _mix_chunk(c, n_ctx_chunks, n_chunks, cz, z_last, z_first, cw, rqkv, gates, sp, dm_ref, qdec_ref,
               qa, kvs, sink_ref):
    is_lat = c >= n_ctx_chunks

    z = cz[:, CONV_WIDTH:]
    row = lax.broadcasted_iota(I32, (CHUNK, 1), 0)
    has_prev = jnp.logical_and(c != 0, c != n_ctx_chunks)
    has_next = jnp.logical_and(c != n_ctx_chunks - 1, c != n_chunks - 1)
    z_before = jnp.where(row == 0, jnp.where(has_prev, z_last, 0.0), pltpu.roll(z, 1, 0))
    z_after = jnp.where(row == CHUNK - 1, jnp.where(has_next, z_first, 0.0), pltpu.roll(z, CHUNK - 1, 0))
    conv = cz[:, :CONV_WIDTH] * (z_before * cw[0:1, :] + z * cw[1:2, :] + z_after * cw[2:3, :])

    q = rqkv[:, 0:RET_WIDTH]
    k = rqkv[:, RET_WIDTH:2 * RET_WIDTH]
    v = rqkv[:, 2 * RET_WIDTH:]
    lane_head = lax.broadcasted_iota(I32, (1, RET_WIDTH), 1) // HEAD_DIM
    kz = jnp.zeros_like(k)
    k_heads = jnp.concatenate([jnp.where(lane_head == hh, k, kz) for hh in range(RET_HEADS)], axis=0)
    v_heads = jnp.concatenate([jnp.where(lane_head == hh, v, kz) for hh in range(RET_HEADS)], axis=0)
    scores = _dot_nt(q, k_heads)

    keys, v_t = kvs
    half = lax.broadcasted_iota(I32, (1, LANES), 1) // HEAD_DIM
    qa_z = jnp.zeros((CHUNK, LANES), BF16)
    combos = [(hk, par) for hk in range(ATTN_KV_HEADS) for par in range(2)]
    n_ctx = keys[0].shape[0] - 3 * CHUNK
    cols2 = 2 * CHUNK
    ik = lax.broadcasted_iota(I32, (CHUNK, cols2), 0)
    iq = lax.broadcasted_iota(I32, (CHUNK, cols2), 1) & (CHUNK - 1)
    off = jnp.full((CHUNK, cols2), NEG_INF, F32)
    bias_prev = jnp.where(jnp.logical_and(jnp.logical_and(is_lat, c - 1 >= n_ctx_chunks), ik >= iq), 0.0, off)
    bias_cur = jnp.where(is_lat, 0.0, off)
    bias_next = jnp.where(jnp.logical_and(jnp.logical_and(is_lat, c + 1 <= n_chunks - 1), ik <= iq), 0.0, off)
    assert n_ctx > 0
    logits = {}
    for hk, par in combos:
        ja, jb = 2 * hk, 2 * hk + 1
        qst = jnp.concatenate([
            jnp.where(half == par, qa[:, ja * LANES:(ja + 1) * LANES], qa_z),
            jnp.where(half == par, qa[:, jb * LANES:(jb + 1) * LANES], qa_z)], axis=0)
        sel = 0 if par == hk else 1
        r = _dot_nt(keys[sel], qst)
        s = jnp.concatenate([r[:CHUNK] + bias_prev, r[CHUNK:2 * CHUNK] + bias_cur,
                             r[2 * CHUNK:3 * CHUNK] + bias_next, r[3 * CHUNK:]], axis=0)
        ha, hb = ATTN_GROUP * hk + par, ATTN_GROUP * hk + par + 2
        snk = jnp.concatenate([sink_ref[ha * CHUNK:ha * CHUNK + 1, :],
                               sink_ref[hb * CHUNK:hb * CHUNK + 1, :]], axis=1) * LOG2E
        logits[hk, par] = (s, snk, jnp.maximum(jnp.max(s, axis=0, keepdims=True), snk))
    yield

    qf = q.astype(F32)
    outs = []
    for d in range(2):
        p = (scores * dm_ref[d]).astype(BF16)
        lhs = jnp.concatenate([p, (qf * qdec_ref[d]).astype(BF16)], axis=1)
        o = _dot(lhs, jnp.concatenate([v_heads, sp[d]], axis=0))
        outs.append(o)
    o2 = jnp.concatenate(outs, axis=0)
    yield

    att = {}
    ones_keys = jnp.ones((BF16_ROWS, keys[0].shape[0]), BF16)
    for n, (hk, par) in enumerate(combos):
        s, snk, mx = logits[hk, par]
        e = jnp.exp2(s - mx).astype(BF16)
        o_t = _dot(jnp.concatenate([v_t[hk * HEAD_DIM:(hk + 1) * HEAD_DIM, :], ones_keys], axis=0), e)
        den = o_t[HEAD_DIM:HEAD_DIM + 1, :] + jnp.exp2(snk - mx)
        att[hk, par] = o_t[:HEAD_DIM, :] * (1.0 / den)
        if n == 0:
            ones_bd = jnp.where(_head_block_mask(RET_WIDTH), 1.0, 0.0).astype(BF16)
            dl = o2 - _group_mean(o2, ones_bd)
            on = dl * lax.rsqrt(_group_mean(dl * dl, ones_bd) + EPS)
            ret = on[:CHUNK] * gates[:, 0:RET_WIDTH] + on[CHUNK:] * gates[:, RET_WIDTH:]
        yield

    cols = []
    for hk in range(ATTN_KV_HEADS):
        cols.append(jnp.concatenate([att[hk, 0][:, :CHUNK], att[hk, 1][:, :CHUNK]], axis=0).T)
        cols.append(jnp.concatenate([att[hk, 0][:, CHUNK:], att[hk, 1][:, CHUNK:]], axis=0).T)
    return jnp.concatenate([conv, ret] + cols, axis=1)


def _round_robin(gens):
    results = [None] * len(gens)
    active = list(range(len(gens)))
    while active:
        for i in list(active):
            try:
                next(gens[i])
            except StopIteration as done:
                results[i] = done.value
                active.remove(i)
    return results


MIX_CHUNKS = 2


def _mixer_kernel(n_ctx_chunks, n_chunks, n_exp,
                  cz_ref, czp_ref, czn_ref, cw_ref, rqkv_ref, gates_ref, sp_ref, dm_ref, qdec_ref,
                  aq_ref, kvp_ref, kvc_ref, kvn_ref, kvx_ref, sink_ref,
                  x_ref, mod_ref, g_ref, w_ref, wrh_ref, wrl_ref, xo_ref, hg_ref, aff_ref, mix_scr):
    t = pl.program_id(0)
    last = pl.num_programs(0) - 2
    tiles = n_chunks // MIX_CHUNKS

    @pl.when(t == 0)
    def _():
        mix_scr[...] = jnp.zeros_like(mix_scr)

    tail = _outproj_tail(n_ctx_chunks * CHUNK, n_exp, lax.rem(jnp.maximum(t - 1, 0), tiles), mix_scr[...], x_ref,
                         mod_ref, g_ref, w_ref, wrh_ref, wrl_ref, xo_ref, hg_ref, aff_ref)
    j = lax.rem(jnp.minimum(t, last), tiles)
    cw = cw_ref[...]

    def values_t(blk):
        return blk[:, 2 * KV_WIDTH:].astype(F32).T.astype(BF16)

    kv_blocks = ([kvp_ref[0]] + [kvc_ref[0, s * CHUNK:(s + 1) * CHUNK, :] for s in range(MIX_CHUNKS)] + [kvn_ref[0]])
    kv_ctx = [kvx_ref[0, t * CHUNK:(t + 1) * CHUNK, :] for t in range(kvx_ref.shape[1] // CHUNK)]
    vt_blocks = [values_t(blk) for blk in kv_blocks]
    vt_ctx = [values_t(blk) for blk in kv_ctx]
    gens = []
    for s in range(MIX_CHUNKS):
        lo, hi = s * CHUNK, (s + 1) * CHUNK
        if s == 0:
            z_last = czp_ref[0, SUBLANES - 1:SUBLANES, CONV_WIDTH:]
        else:
            z_last = cz_ref[0, lo - 1:lo, CONV_WIDTH:]
        if s == MIX_CHUNKS - 1:
            z_first = czn_ref[0, 0:1, CONV_WIDTH:]
        else:
            z_first = cz_ref[0, hi:hi + 1, CONV_WIDTH:]
        window = kv_blocks[s:s + 3] + kv_ctx
        keys = tuple(jnp.concatenate([blk[:, i * KV_WIDTH:(i + 1) * KV_WIDTH] for blk in window], axis=0)
                     for i in (0, 1))
        v_t = jnp.concatenate(vt_blocks[s:s + 3] + vt_ctx, axis=1)
        gens.append(_mix_chunk(j * MIX_CHUNKS + s, n_ctx_chunks, n_chunks, cz_ref[0, lo:hi, :], z_last, z_first,
                               cw, rqkv_ref[0, lo:hi, :], gates_ref[0, lo:hi, :],
                               (sp_ref[0, 0, s], sp_ref[0, 1, s]), dm_ref, qdec_ref, aq_ref[0, lo:hi, :],
                               (keys, v_t), sink_ref))
    mixes = [mix.astype(BF16) for mix in _round_robin([tail] + gens)[1:]]
    mix_scr[...] = jnp.concatenate(mixes, axis=0)


def _mixer(proj, sp, tabs, conv_w, sink_rows, x, mrows, norm_g, w_out_bf, layer, wr_hi, wr_lo, n_ctx):
    cz, rqkv, gates, aq, kv = proj
    dm, qdec = tabs
    b, tu, _ = cz.shape
    d = x.shape[2]
    n_exp = wr_hi.shape[1]
    n_chunks = tu // CHUNK
    mc = MIX_CHUNKS
    rows = mc * CHUNK
    assert n_chunks % mc == 0 and (n_ctx // CHUNK) % mc == 0
    last = n_chunks - 1
    tiles = n_chunks // mc
    per8 = rows // SUBLANES

    def mixed(t):
        m = jnp.minimum(t, b * tiles - 1)
        return m // tiles, lax.rem(m, tiles)

    def projected(t):
        p = jnp.maximum(t - 1, 0)
        return p // tiles, lax.rem(p, tiles)

    def at_mixed(block, tile_index):
        def index(t):
            i, j = mixed(t)
            return (i, tile_index(j), 0)
        return pl.BlockSpec(block, index)

    cur = lambda w: at_mixed((1, rows, w), lambda j: j)

    def out(r, w):
        return pl.BlockSpec((1, r, w), lambda t: projected(t) + (0,))

    wcz, wkv = cz.shape[2], kv.shape[2]
    in_specs = [
        cur(wcz),
        at_mixed((1, SUBLANES, wcz), lambda j: jnp.maximum(j * per8 - 1, 0)),
        at_mixed((1, SUBLANES, wcz), lambda j: jnp.minimum((j + 1) * per8, tu // SUBLANES - 1)),
        _layer_block(conv_w, layer),
        cur(rqkv.shape[2]), cur(gates.shape[2]),
        pl.BlockSpec((1, 2, mc, RET_WIDTH, RET_WIDTH), lambda t: (mixed(t)[0], 0, mixed(t)[1], 0, 0)),
        _layer_block(dm, layer), _layer_block(qdec, layer),
        cur(ATTN_WIDTH),
        at_mixed((1, CHUNK, wkv), lambda j: jnp.maximum(j * mc - 1, 0)),
        cur(wkv),
        at_mixed((1, CHUNK, wkv), lambda j: jnp.minimum((j + 1) * mc, last)),
        at_mixed((1, n_ctx, wkv), lambda j: 0),
        _layer_block(sink_rows, layer),
        out(rows, d),
        pl.BlockSpec((1, M_ROWS, d), lambda t: (projected(t)[0], 0, 0)),
        _layer_block(norm_g, layer),
        _layer_block(w_out_bf, layer),
        _layer_block(wr_hi, layer), _layer_block(wr_lo, layer),
    ]
    args = [cz, cz, cz, conv_w, rqkv, gates, sp, dm, qdec, aq, kv, kv, kv, kv, sink_rows,
            x, mrows, norm_g, w_out_bf, wr_hi, wr_lo]
    return pl.pallas_call(
        functools.partial(_mixer_kernel, n_ctx // CHUNK, n_chunks, n_exp),
        grid=(b * tiles + 1,),
        in_specs=in_specs,
        out_specs=[out(rows, d), out(rows * SUBLANES, LANES),
                   pl.BlockSpec((1, n_exp, rows), lambda t: (projected(t)[0], 0, projected(t)[1]))],
        out_shape=[
            jax.ShapeDtypeStruct((b, tu, d), F32),
            jax.ShapeDtypeStruct((b, tu * SUBLANES, LANES), F32),
            jax.ShapeDtypeStruct((b, n_exp, tu), F32),
        ],
        scratch_shapes=[pltpu.VMEM((rows, CONV_WIDTH + RET_WIDTH + ATTN_WIDTH), BF16)],
        compiler_params=_params("arbitrary"),
        name="mixer",
    )(*args)


def _outproj_tail(n_ctx, n_exp, tile, mix, x_ref, mod_ref, g_ref, w_ref, wrh_ref, wrl_ref, xo_ref, hg_ref, aff_ref):
    tm = x_ref.shape[1]
    row = _mod_rows(mod_ref, tm, n_ctx, tile)

    y = _dot(mix, w_ref[...])
    yield
    x = x_ref[0] + row(M_G1) * y
    xo_ref[0] = x
    h = (_rms(x) * g_ref[...]) * (1.0 + row(M_SC2)) + row(M_SH2)
    for q in range(SUBLANES):
        hg_ref[0, pl.ds(q, tm, stride=SUBLANES), :] = h[:, q * LANES:(q + 1) * LANES]
    yield
    h_hi = h.astype(BF16)
    h_lo = (h - h_hi.astype(F32)).astype(BF16)
    logits = (_dot_nt(wrh_ref[...], h_hi) + (_dot_nt(wrh_ref[...], h_lo) + _dot_nt(wrl_ref[...], h_hi)))
    yield
    e = jnp.exp(logits - jnp.max(logits, axis=0, keepdims=True))
    aff_ref[0] = e / jnp.sum(e, axis=0, keepdims=True)


MIN_NORMAL_F32_BITS = 0x00800000
ROUTE_TABLES = 4
ROUTE_GROUP = 4


def _route_stream_tables(v, k, tab_scr, ce_scr, cnt_scr, stream, thr):
    n_exp, n = v.shape
    lane = lax.broadcasted_iota(I32, (1, LANES), 1)
    tri_r = lax.broadcasted_iota(I32, (LANES, LANES), 0)
    tri_c = lax.broadcasted_iota(I32, (LANES, LANES), 1)
    upper = jnp.where(tri_r <= tri_c, 1.0, 0.0).astype(BF16)
    above = pltpu.bitcast(jnp.maximum(thr + 1, MIN_NORMAL_F32_BITS), F32)
    floor = pltpu.bitcast(thr, F32)
    n_gt = jnp.sum(jnp.where(v >= above, 1, 0), axis=1, keepdims=True)
    need = (k - n_gt).astype(F32)
    seen_eq = jnp.zeros((n_exp, 1), F32)
    counts = jnp.zeros((n_exp, LANES), F32)
    for j in range(n // LANES):
        vb = v[:, j * LANES:(j + 1) * LANES]
        gt = vb >= above
        eq = jnp.logical_and(vb >= floor, vb < above)
        both = jnp.concatenate([jnp.where(gt, 1.0, 0.0), jnp.where(eq, 1.0, 0.0)], axis=0).astype(BF16)
        pref = _dot(both, upper)
        rank_eq = pref[n_exp:] + seen_eq
        inc = pref[:n_exp] + jnp.minimum(rank_eq, need) - jnp.minimum(seen_eq, need)
        sel = jnp.logical_or(gt, jnp.logical_and(eq, rank_eq <= need))
        seen_eq = rank_eq[:, LANES - 1:LANES]
        counts = counts + jnp.where(lane == j, inc[:, LANES - 1:LANES], 0.0)
        a0 = jnp.where(sel, vb, 0.0)
        t0 = a0.astype(BF16).astype(F32)
        t1 = (a0 - t0).astype(BF16).astype(F32)
        t2 = a0 - t0 - t1
        for t, val in enumerate((inc, t0, t1, t2)):
            tab_scr[stream * ROUTE_TABLES + t, pl.ds(j, n_exp, stride=LANES), :] = val
    through = _dot(counts.astype(BF16), upper)
    for e in range(n_exp):
        ce_scr[stream, e] = through[e:e + 1, :]
        cnt_scr[stream, e] = counts[e:e + 1, :]


def _route_slots(e, stream, k, lo, tab_scr, ce_scr, cnt_scr):
    lane_f = lax.broadcasted_iota(I32, (1, LANES), 1).astype(F32)
    slot = lax.broadcasted_iota(I32, (k, LANES), 0).astype(F32)
    ones_rows = jnp.ones((LANES, LANES), BF16)
    cnt_rows = jnp.broadcast_to(cnt_scr[stream, e], (LANES, LANES)).astype(BF16)
    rows_e = pl.ds(pl.multiple_of(e * LANES, LANES), LANES)
    tab = jnp.concatenate([tab_scr[stream * ROUTE_TABLES + t, rows_e, :] for t in range(ROUTE_TABLES)],
                          axis=1).astype(BF16)
    before = jnp.where(ce_scr[stream, e] <= slot, 1.0, 0.0).astype(BF16)
    blk = _dot_nt(before, ones_rows)
    base = _dot_nt(before, cnt_rows)
    yield
    row = _dot(jnp.where(lane_f == blk, 1.0, 0.0).astype(BF16), tab)
    yield
    inc = row[:, :LANES]
    aff = row[:, LANES:2 * LANES] + row[:, 2 * LANES:3 * LANES] + row[:, 3 * LANES:]
    local = slot - base
    pos = _dot_nt(jnp.where(inc <= local, 1.0, 0.0).astype(BF16), ones_rows)
    yield
    tok = blk * LANES + pos + float(lo)
    gate = jnp.sum(jnp.where(inc == local + 1.0, aff, 0.0), axis=1, keepdims=True)
    return tok, gate


def _route_kernel(n_exp, streams, aff_ref, idx_ref, gate_ref, tab_scr, ce_scr, cnt_scr):
    a = aff_ref[0]
    tab_scr[...] = jnp.zeros_like(tab_scr)
    vs = [a[:, lo:lo + n] for lo, n, _, _ in streams]

    def enough(v, cand, k):
        return jnp.sum(jnp.where(v >= pltpu.bitcast(cand, F32), 1, 0), axis=1, keepdims=True) >= k

    def search(i, ts):
        out = []
        for t, v, (_, _, k, _) in zip(ts, vs, streams):
            hi = t | jnp.left_shift(jnp.int32(1), 30 - 2 * i)
            lo = jnp.left_shift(jnp.int32(1), 29 - 2 * i)
            out.append(jnp.where(enough(v, hi, k), jnp.where(enough(v, hi | lo, k), hi | lo, hi),
                                 jnp.where(enough(v, t | lo, k), t | lo, t)))
        return tuple(out)

    thrs = lax.fori_loop(0, 15, search, tuple(jnp.zeros((n_exp, 1), I32) for _ in streams))
    thrs = tuple(jnp.where(enough(v, t | 1, k), t | 1, t) for t, v, (_, _, k, _) in zip(thrs, vs, streams))
    for st, (v, (_, _, k, _)) in enumerate(zip(vs, streams)):
        _route_stream_tables(v, k, tab_scr, ce_scr, cnt_scr, st, thrs[st])

    slots = idx_ref.shape[3]

    assert [s[3] for s in streams] == [sum(s[2] for s in streams[:n]) for n in range(len(streams))]

    def expert_group(g, carry):
        experts = [g * ROUTE_GROUP + u for u in range(ROUTE_GROUP)]
        found = _round_robin([_route_slots(e, st, k, lo, tab_scr, ce_scr, cnt_scr)
                              for e in experts for st, (lo, _, k, _) in enumerate(streams)])
        for u, e in enumerate(experts):
            toks = []
            for st, (_, _, k, slot0) in enumerate(streams):
                tok, gate = found[u * len(streams) + st]
                toks.append(tok)
                gate_ref[0, pl.ds(e, 1), slot0:slot0 + k, :] = gate[None]
            toks.append(jnp.zeros((-slots % LANES, LANES), F32))
            row = jnp.concatenate(toks, axis=0).T[0:1, :slots]
            idx_ref[0, pl.ds(e, 1)] = row.astype(I32)[None]
        return carry

    assert n_exp % ROUTE_GROUP == 0
    lax.fori_loop(0, n_exp // ROUTE_GROUP, expert_group, 0)


def _route(aff, n_ctx):
    b, n_exp, tu = aff.shape
    n_lat = tu - n_ctx
    cap_l = CAPACITY_FACTOR * n_lat // n_exp
    cap_c = CAPACITY_FACTOR * n_ctx // n_exp
    slots = cap_l + cap_c
    assert n_lat // LANES <= LANES and n_ctx // LANES <= LANES
    streams = ((n_ctx, n_lat, cap_l, 0), (0, n_ctx, cap_c, cap_l))
    return pl.pallas_call(
        functools.partial(_route_kernel, n_exp, streams),
        grid=(b,),
        in_specs=[pl.BlockSpec((1, n_exp, tu), lambda i: (i, 0, 0))],
        out_specs=[pl.BlockSpec((1, n_exp, 1, slots), lambda i: (i, 0, 0, 0)),
                   pl.BlockSpec((1, n_exp, slots, 1), lambda i: (i, 0, 0, 0))],
        out_shape=[jax.ShapeDtypeStruct((b, n_exp, 1, slots), I32),
                   jax.ShapeDtypeStruct((b, n_exp, slots, 1), F32)],
        scratch_shapes=[pltpu.VMEM((len(streams) * ROUTE_TABLES, n_exp * LANES, LANES), F32),
                        pltpu.VMEM((len(streams), n_exp, 1, LANES), F32),
                        pltpu.VMEM((len(streams), n_exp, 1, LANES), F32)],
        compiler_params=_params("parallel"),
        name="route",
    )(aff)


def _slot_pitch(slots):
    return slots + SUBLANES


def _zero_slot_padding(ref, lead, slots, pitch):
    for q in range(SUBLANES):
        ref[lead + (pl.ds(q * pitch + slots, pitch - slots), slice(None))] = jnp.zeros((pitch - slots, LANES), ref.dtype)


def _slot_pitch_bf16(slots):
    return -(-slots // BF16_ROWS) * BF16_ROWS + BF16_ROWS


DISPATCH_EXPERTS = 4


def _gather_kernel(slots, pitch, pitch_out, idx_ref, h_ref, o_ref, tile_scr):
    for g in range(DISPATCH_EXPERTS):
        for mi in range(slots):
            r = pl.multiple_of(idx_ref[g, 0, mi] * SUBLANES, SUBLANES)
            tile_scr[g, pl.ds(mi, SUBLANES, stride=pitch), :] = h_ref[0, pl.ds(r, SUBLANES), :]
        _zero_slot_padding(o_ref, (0, g), slots, pitch_out)
        for q in range(SUBLANES):
            o_ref[0, g, q * pitch_out:q * pitch_out + slots, :] = (
                tile_scr[g, q * pitch:q * pitch + slots, :].astype(BF16))


def _gather(hg, idx_rows, n_exp, slots):
    b = hg.shape[0]
    pitch = _slot_pitch(slots)
    pitch_out = _slot_pitch_bf16(slots)
    ge = DISPATCH_EXPERTS
    assert n_exp % ge == 0
    return pl.pallas_call(
        functools.partial(_gather_kernel, slots, pitch, pitch_out),
        grid=(b, n_exp // ge),
        in_specs=[
            pl.BlockSpec((ge, 1, slots), lambda i, e: (i * (n_exp // ge) + e, 0, 0), memory_space=pltpu.SMEM),
            pl.BlockSpec((1,) + hg.shape[1:], lambda i, e: (i, 0, 0)),
        ],
        out_specs=pl.BlockSpec((1, ge, SUBLANES * pitch_out, LANES), lambda i, e: (i, e, 0, 0)),
        out_shape=jax.ShapeDtypeStruct((b, n_exp, SUBLANES * pitch_out, LANES), BF16),
        scratch_shapes=[pltpu.VMEM((ge, SUBLANES * pitch, LANES), F32)],
        compiler_params=_params("parallel", "arbitrary"),
        name="gather",
    )(idx_rows, hg)


FFN_SAMPLES = 2
FFN_VMEM_LIMIT_BYTES = 60 * 1024 * 1024


def _ffn_kernel(slots, pitch_in, pitch, xs_ref, wg_ref, wu_ref, wd_ref, gate_ref, y_ref):
    n = xs_ref.shape[0]
    x = jnp.concatenate(
        [jnp.concatenate([xs_ref[s, 0, q * pitch_in:q * pitch_in + slots, :] for q in range(SUBLANES)], axis=1)
         for s in range(n)], axis=0)
    gate = jnp.concatenate([gate_ref[s, 0] for s in range(n)], axis=0)
    a = _dot(x, wg_ref[0].astype(BF16))
    u = _dot(x, wu_ref[0].astype(BF16))
    y = _dot((_silu(a) * u).astype(BF16), wd_ref[0].astype(BF16)) * gate
    for s in range(n):
        _zero_slot_padding(y_ref, (s, 0), slots, pitch)
        for q in range(SUBLANES):
            y_ref[s, 0, q * pitch:q * pitch + slots, :] = y[s * slots:(s + 1) * slots, q * LANES:(q + 1) * LANES]


def _ffn(xs, w_gate, w_up, w_down, layer, gate, slots):
    b, n_exp = xs.shape[:2]
    pitch = _slot_pitch(slots)
    pitch_in = _slot_pitch_bf16(slots)
    d, f = w_gate.shape[2:]
    ns = FFN_SAMPLES
    assert b % ns == 0
    slot_tile = lambda p: pl.BlockSpec((ns, 1, SUBLANES * p, LANES), lambda e, i: (i, e, 0, 0))
    weight = lambda rows, cols: pl.BlockSpec((None, 1, rows, cols), lambda e, i: (layer, e, 0, 0))
    return pl.pallas_call(
        functools.partial(_ffn_kernel, slots, pitch_in, pitch),
        grid=(n_exp, b // ns),
        in_specs=[
            slot_tile(pitch_in),
            weight(d, f), weight(d, f), weight(f, d),
            pl.BlockSpec((ns, 1, slots, 1), lambda e, i: (i, e, 0, 0)),
        ],
        out_specs=slot_tile(pitch),
        out_shape=jax.ShapeDtypeStruct((b, n_exp, SUBLANES * pitch, LANES), F32),
        compiler_params=_params("parallel", "arbitrary", vmem=FFN_VMEM_LIMIT_BYTES),
        name="ffn",
    )(xs, w_gate, w_up, w_down, gate)


SCATTER_BATCH = 16


def _scatter_kernel(slots, pitch, idx_ref, y_ref, o_ref):
    @pl.when(pl.program_id(1) == 0)
    def _():
        o_ref[...] = jnp.zeros_like(o_ref)

    for g in range(DISPATCH_EXPERTS):
        for m0 in range(0, slots, SCATTER_BATCH):
            rows = [pl.multiple_of(idx_ref[g, 0, m0 + u] * SUBLANES, SUBLANES) for u in range(SCATTER_BATCH)]
            vals = [o_ref[0, pl.ds(rows[u], SUBLANES), :] + y_ref[0, g, pl.ds(m0 + u, SUBLANES, stride=pitch), :]
                    for u in range(SCATTER_BATCH)]
            for u in range(SCATTER_BATCH):
                o_ref[0, pl.ds(rows[u], SUBLANES), :] = vals[u]


def _scatter(y, idx_rows, tu, slots):
    b, n_exp = y.shape[:2]
    pitch = _slot_pitch(slots)
    ge = DISPATCH_EXPERTS
    assert n_exp % ge == 0 and slots % SCATTER_BATCH == 0
    return pl.pallas_call(
        functools.partial(_scatter_kernel, slots, pitch),
        grid=(b, n_exp // ge),
        in_specs=[
            pl.BlockSpec((ge, 1, slots), lambda i, e: (i * (n_exp // ge) + e, 0, 0), memory_space=pltpu.SMEM),
            pl.BlockSpec((1, ge, SUBLANES * pitch, LANES), lambda i, e: (i, e, 0, 0)),
        ],
        out_specs=pl.BlockSpec((1, tu * SUBLANES, LANES), lambda i, e: (i, 0, 0)),
        out_shape=jax.ShapeDtypeStruct((b, tu * SUBLANES, LANES), F32),
        compiler_params=_params("parallel", "arbitrary"),
        name="scatter",
    )(idx_rows, y)


FINAL_PARTS = 4


def _final_kernel(mod_ref, g_ref, *refs):
    o_ref = refs[-1]
    tm = FINAL_TILE
    for n in range(FINAL_PARTS):
        x_ref, f_ref = refs[n], refs[FINAL_PARTS + n]
        x = x_ref[0] + mod_ref[0][M_G2:M_G2 + 1, :] * _from_gather_layout(f_ref, tm)
        o_ref[0, n * tm:(n + 1) * tm, :] = _rms(x) * g_ref[...]


def _final(x, ffn, mrows, final_g, n_ctx):
    b, tu, d = x.shape
    tm = FINAL_TILE
    skip = n_ctx // tm
    parts = FINAL_PARTS
    assert (tu - n_ctx) % (tm * parts) == 0
    tile = lambda n: pl.BlockSpec((1, tm, d), lambda i, j: (i, j * parts + n + skip, 0))
    ftile = lambda n: pl.BlockSpec((1, tm * SUBLANES, LANES), lambda i, j: (i, j * parts + n + skip, 0))
    return pl.pallas_call(
        _final_kernel,
        grid=(b, (tu - n_ctx) // (tm * parts)),
        in_specs=([pl.BlockSpec((1, M_ROWS, d), lambda i, j: (i, 0, 0)), pl.BlockSpec((1, d), lambda i, j: (0, 0))]
                  + [tile(n) for n in range(parts)] + [ftile(n) for n in range(parts)]),
        out_specs=pl.BlockSpec((1, tm * parts, d), lambda i, j: (i, j, 0)),
        out_shape=jax.ShapeDtypeStruct((b, tu - n_ctx, d), F32),
        compiler_params=_params("parallel", "parallel"),
        name="final_norm",
    )(mrows, final_g.reshape(1, d), *([x] * parts + [ffn] * parts))


def _rope_tables(n_lat, n_ctx):
    rows = n_lat // GRID_W
    rowp = jnp.repeat(jnp.arange(rows, dtype=F32), GRID_W)
    colp = jnp.tile(jnp.arange(GRID_W, dtype=F32), rows)
    axis_dim = HEAD_DIM // 2
    inv_freq = ROPE_BASE ** (-jnp.arange(0, axis_dim, 2, dtype=F32) / axis_dim)
    ar = rowp[:, None] * inv_freq
    ac = colp[:, None] * inv_freq
    cs = jnp.concatenate([jnp.cos(ar), jnp.cos(ar), jnp.cos(ac), jnp.cos(ac)], axis=1)
    sn = jnp.concatenate([-jnp.sin(ar), jnp.sin(ar), -jnp.sin(ac), jnp.sin(ac)], axis=1)
    reps = LANES // HEAD_DIM
    cs = jnp.concatenate([jnp.ones((n_ctx, LANES), F32), jnp.tile(cs, (1, reps))], axis=0)
    sn = jnp.concatenate([jnp.zeros((n_ctx, LANES), F32), jnp.tile(sn, (1, reps))], axis=0)
    return cs, sn


def kernel(x, c, ctx, c_ctx, w_mod, b_mod, norm1_g, norm2_g, w_in, conv_w, ret_decay_logit, attn_sink,
           w_out, w_router, w_gate, w_up, w_down, final_g):
    b, n_lat, d = x.shape
    n_ctx = ctx.shape[1]
    depth = w_in.shape[0]
    n_exp = w_router.shape[2]
    tu = n_ctx + n_lat
    assert w_in.shape[2] == IN_WIDTH and tu % INPROJ_TILE == 0
    assert n_ctx % FINAL_TILE == 0 and n_lat % FINAL_TILE == 0
    assert b + 1 <= SUBLANES and n_exp == N_EXPERTS

    c_rows = jnp.concatenate([c, c_ctx[None], jnp.zeros((SUBLANES - b - 1, d), F32)], axis=0)
    mods = _mod_vectors(c_rows, w_mod, b_mod).reshape(depth, SUBLANES, 6, d)
    qdec, win, cd, dm = _decay_tables(ret_decay_logit)
    rope_cs, rope_sn = _rope_tables(n_lat, n_ctx)
    sink_rows = jnp.broadcast_to(jnp.repeat(attn_sink.astype(F32), CHUNK, axis=1)[:, :, None],
                                 (depth, ATTN_HEADS * CHUNK, LANES))
    w_in_bf = w_in.astype(BF16)
    w_out_bf = w_out.astype(BF16)
    wr = jnp.swapaxes(w_router, 1, 2)
    wr_hi = wr.astype(BF16)
    wr_lo = (wr - wr_hi.astype(F32)).astype(BF16)
    g1 = norm1_g.reshape(depth, 1, d)
    g2 = norm2_g.reshape(depth, 1, d)
    mrows = jnp.concatenate([mods[:, :b], jnp.broadcast_to(mods[:, b][:, None], (depth, b, 6, d)),
                             jnp.zeros((depth, b, M_ROWS - 12, d), F32)], axis=2)
    mrows_in = mrows.at[1:, :, M_G2].set(mrows[:-1, :, M_G2])
    mrows_in = mrows_in.at[1:, :, M_CTX + M_G2].set(mrows[:-1, :, M_CTX + M_G2])

    xu = jnp.concatenate([ctx, x], axis=1)
    ffn = None
    cap = CAPACITY_FACTOR * n_lat // n_exp + CAPACITY_FACTOR * n_ctx // n_exp
    for l in range(depth):
        xu, proj = _inproj(xu, ffn, mrows_in[l], g1, w_in_bf, l, rope_cs, rope_sn, n_ctx)
        sp = _ret_states(proj[1], win, cd, l, n_ctx)
        xu, hg, aff = _mixer(proj, sp, (dm, qdec), conv_w, sink_rows, xu, mrows[l], g2, w_out_bf, l,
                             wr_hi, wr_lo, n_ctx)
        idx, gate = _route(aff, n_ctx)
        idx_rows = idx.reshape(b * n_exp, 1, cap)
        xs = _gather(hg, idx_rows, n_exp, cap)
        y = _ffn(xs, w_gate, w_up, w_down, l, gate, cap)
        ffn = _scatter(y, idx_rows, tu, cap)
    return _final(xu, ffn, mrows[depth - 1], final_g, n_ctx)
```

```python
import functools

import jax
import jax.numpy as jnp
from jax import lax
from jax.experimental import pallas as pl
from jax.experimental.pallas import tpu as pltpu

F32 = jnp.float32
BF16 = jnp.bfloat16
I32 = jnp.int32

HEAD_DIM = 64
CONV_WIDTH = 256
RET_HEADS = 4
RET_WIDTH = RET_HEADS * HEAD_DIM
ATTN_HEADS = 8
ATTN_KV_HEADS = 2
ATTN_GROUP = ATTN_HEADS // ATTN_KV_HEADS
ATTN_WIDTH = ATTN_HEADS * HEAD_DIM
KV_WIDTH = ATTN_KV_HEADS * HEAD_DIM
CHUNK = 128
GRID_W = 64
N_EXPERTS = 16
CAPACITY_FACTOR = 2
ROPE_BASE = 10000.0
EPS = 1e-6
NEG_INF = -1e30
QK_SCALE = HEAD_DIM ** -0.5
LOG2E = 1.4426950408889634

LANES = 128
SUBLANES = 8
BF16_ROWS = 2 * SUBLANES
VMEM_LIMIT_BYTES = 56 * 1024 * 1024

O_CB = 0
O_CC = O_CB + CONV_WIDTH
O_CX = O_CC + CONV_WIDTH
O_RQ = O_CX + CONV_WIDTH
O_RK = O_RQ + RET_WIDTH
O_RV = O_RK + RET_WIDTH
O_GF = O_RV + RET_WIDTH
O_GB = O_GF + RET_WIDTH
O_AQ = O_GB + RET_WIDTH
O_AK = O_AQ + ATTN_WIDTH
O_AV = O_AK + KV_WIDTH
IN_WIDTH = O_AV + KV_WIDTH

M_SH1, M_SC1, M_G1, M_SH2, M_SC2, M_G2 = range(6)
M_CTX = 6
M_ROWS = 16

MOD_TILE = 3072
INPROJ_TILE = 1088
FINAL_TILE = 256


def _params(*sem, vmem=VMEM_LIMIT_BYTES):
    return pltpu.CompilerParams(dimension_semantics=sem, vmem_limit_bytes=vmem)


def _dot(a, b):
    return jnp.dot(a, b, preferred_element_type=F32)


def _dot_nt(a, b):
    return lax.dot_general(a, b, (((1,), (1,)), ((), ())), preferred_element_type=F32)


def _silu(v):
    return v * jax.nn.sigmoid(v)


def _mod_kernel(c_ref, w_ref, b_ref, o_ref):
    s = _silu(c_ref[...])
    s_hi = s.astype(BF16)
    s_lo = (s - s_hi.astype(F32)).astype(BF16)
    w = w_ref[0]
    w_hi = w.astype(BF16)
    w_lo = (w - w_hi.astype(F32)).astype(BF16)
    n = s.shape[0]
    head = _dot(jnp.concatenate([s_hi, s_lo], axis=0), w_hi)
    o_ref[0] = head[:n] + (head[n:] + _dot(s_hi, w_lo)) + b_ref[0]


def _mod_vectors(c_rows, w_mod, b_mod):
    depth, d_model, width = w_mod.shape
    tn = MOD_TILE
    assert width % tn == 0
    return pl.pallas_call(
        _mod_kernel,
        grid=(depth, width // tn),
        in_specs=[
            pl.BlockSpec((SUBLANES, d_model), lambda l, n: (0, 0)),
            pl.BlockSpec((1, d_model, tn), lambda l, n: (l, 0, n)),
            pl.BlockSpec((1, 1, tn), lambda l, n: (l, 0, n)),
        ],
        out_specs=pl.BlockSpec((1, SUBLANES, tn), lambda l, n: (l, 0, n)),
        out_shape=jax.ShapeDtypeStruct((depth, SUBLANES, width), F32),
        compiler_params=_params("parallel", "parallel"),
        name="mod_vectors",
    )(c_rows, w_mod, b_mod.reshape(depth, 1, width))


def _log_sigmoid(v):
    return -jnp.log(1.0 + jnp.exp(-v))


def _tables_kernel(lgl_ref, lgr_ref, qdec_ref, win_ref, cd_ref, dm_ref):
    pos = lax.broadcasted_iota(I32, (CHUNK, RET_WIDTH), 0).astype(F32)
    ri = lax.broadcasted_iota(I32, (CHUNK, RET_HEADS * CHUNK), 0).astype(F32)
    rj = (lax.broadcasted_iota(I32, (CHUNK, RET_HEADS * CHUNK), 1) & (CHUNK - 1)).astype(F32)
    for d in range(2):
        lg = _log_sigmoid(lgl_ref[0, d])
        lg1 = lg[0:1, :]
        if d == 0:
            qdec_ref[0, d] = jnp.exp(lg1 * (pos + 1.0))
            win_ref[0, d] = jnp.exp(lg1 * (CHUNK - 1.0 - pos))
            diff = ri - rj
        else:
            qdec_ref[0, d] = jnp.exp(lg1 * (CHUNK - pos))
            win_ref[0, d] = jnp.exp(lg1 * pos)
            diff = rj - ri
        cd_ref[0, d] = jnp.exp(lg * float(CHUNK))
        lr = _log_sigmoid(lgr_ref[0, d])
        dm_ref[0, d] = jnp.where(diff >= 0.0, jnp.exp(lr * jnp.maximum(diff, 0.0)), 0.0)


def _decay_tables(ret_decay_logit):
    depth = ret_decay_logit.shape[0]
    lg = ret_decay_logit.astype(F32)
    lgl = jnp.broadcast_to(jnp.repeat(lg, HEAD_DIM, axis=-1)[:, :, None, :], (depth, 2, SUBLANES, RET_WIDTH))
    lgr = jnp.broadcast_to(jnp.repeat(lg, CHUNK, axis=-1)[:, :, None, :], (depth, 2, CHUNK, RET_HEADS * CHUNK))
    return pl.pallas_call(
        _tables_kernel,
        grid=(depth,),
        in_specs=[
            pl.BlockSpec((1, 2, SUBLANES, RET_WIDTH), lambda l: (l, 0, 0, 0)),
            pl.BlockSpec((1, 2, CHUNK, RET_HEADS * CHUNK), lambda l: (l, 0, 0, 0)),
        ],
        out_specs=[
            pl.BlockSpec((1, 2, CHUNK, RET_WIDTH), lambda l: (l, 0, 0, 0)),
            pl.BlockSpec((1, 2, CHUNK, RET_WIDTH), lambda l: (l, 0, 0, 0)),
            pl.BlockSpec((1, 2, SUBLANES, RET_WIDTH), lambda l: (l, 0, 0, 0)),
            pl.BlockSpec((1, 2, CHUNK, RET_HEADS * CHUNK), lambda l: (l, 0, 0, 0)),
        ],
        out_shape=[
            jax.ShapeDtypeStruct((depth, 2, CHUNK, RET_WIDTH), F32),
            jax.ShapeDtypeStruct((depth, 2, CHUNK, RET_WIDTH), F32),
            jax.ShapeDtypeStruct((depth, 2, SUBLANES, RET_WIDTH), F32),
            jax.ShapeDtypeStruct((depth, 2, CHUNK, RET_HEADS * CHUNK), F32),
        ],
        compiler_params=_params("parallel"),
        name="decay_tables",
    )(lgl, lgr)


def _rope(v, cs, sn):
    lane = lax.broadcasted_iota(I32, (1, LANES), 1)
    first = (lane & 31) < 16
    outs = []
    for g in range(v.shape[1] // LANES):
        vg = v[:, g * LANES:(g + 1) * LANES]
        sw = jnp.where(first, pltpu.roll(vg, LANES - 16, 1), pltpu.roll(vg, 16, 1))
        outs.append(vg * cs + sw * sn)
    return outs[0] if len(outs) == 1 else jnp.concatenate(outs, axis=1)


def _swap_halves(v):
    return jnp.concatenate([v[:, HEAD_DIM:], v[:, :HEAD_DIM]], axis=1)


def _from_gather_layout(f_ref, tm):
    return jnp.concatenate(
        [f_ref[0, pl.ds(q, tm, stride=SUBLANES), :] for q in range(SUBLANES)], axis=1)


def _rms(v):
    return v * lax.rsqrt(jnp.mean(v * v, axis=-1, keepdims=True) + EPS)


def _mod_rows(mod_ref, tm, n_ctx, tile):
    m = mod_ref[0]
    if n_ctx % tm == 0:
        is_ctx = tile < n_ctx // tm
    else:
        is_ctx = tile * tm + lax.broadcasted_iota(I32, (tm, 1), 0) < n_ctx

    def row(r):
        return jnp.where(is_ctx, m[M_CTX + r:M_CTX + r + 1, :], m[r:r + 1, :])

    return row


def _inproj_kernel(has_ffn, n_ctx, *refs):
    if has_ffn:
        x_ref, f_ref, mod_ref, g_ref, w_ref, cs_ref, sn_ref = refs[:7]
        outs = refs[7:]
        xo_ref = outs[0]
        outs = outs[1:]
    else:
        x_ref, mod_ref, g_ref, w_ref, cs_ref, sn_ref = refs[:6]
        outs = refs[6:]
    cz_ref, rqkv_ref, gates_ref, aq_ref, kv_ref = outs
    tm = x_ref.shape[1]
    row = _mod_rows(mod_ref, tm, n_ctx, pl.program_id(1))

    x = x_ref[0]
    if has_ffn:
        x = x + row(M_G2) * _from_gather_layout(f_ref, tm)
        xo_ref[0] = x
    h = (_rms(x) * g_ref[...]) * (1.0 + row(M_SC1)) + row(M_SH1)
    h = h.astype(BF16)
    cs = cs_ref[...]
    sn = sn_ref[...]

    def proj(a, b):
        return _dot(h, w_ref[:, a:b])

    bounds = (O_CB, O_CC, O_CX, O_RQ, O_RK, O_RV, O_GF, O_GB, O_AQ, O_AK, O_AV, IN_WIDTH)
    p = [proj(a, b) for a, b in zip(bounds[:-1], bounds[1:])]
    cz_ref[0, :, 0:CONV_WIDTH] = p[0]
    cz_ref[0, :, CONV_WIDTH:] = p[1] * p[2]
    rqkv_ref[0, :, 0:RET_WIDTH] = _rope(p[3], cs, sn).astype(BF16)
    rqkv_ref[0, :, RET_WIDTH:2 * RET_WIDTH] = _rope(p[4] * QK_SCALE, cs, sn).astype(BF16)
    rqkv_ref[0, :, 2 * RET_WIDTH:] = p[5].astype(BF16)
    gates_ref[0, :, 0:RET_WIDTH] = _silu(p[6])
    gates_ref[0, :, RET_WIDTH:] = _silu(p[7])
    aq_ref[0] = (_rope(p[8], cs, sn) * (QK_SCALE * LOG2E)).astype(BF16)
    ak = _rope(p[9], cs, sn)
    av = p[10]
    kv_ref[0, :, 0:KV_WIDTH] = ak.astype(BF16)
    kv_ref[0, :, KV_WIDTH:2 * KV_WIDTH] = _swap_halves(ak).astype(BF16)
    kv_ref[0, :, 2 * KV_WIDTH:] = av.astype(BF16)


def _layer_block(stacked, layer):
    rest = stacked.shape[1:]
    return pl.BlockSpec((None,) + rest, lambda *_: (layer,) + (0,) * len(rest))


def _inproj(x, ffn, mrows, norm_g, w_in_bf, layer, rope_cs, rope_sn, n_ctx):
    b, tu, d = x.shape
    tm = INPROJ_TILE
    has_ffn = ffn is not None
    tok = lambda width: pl.BlockSpec((1, tm, width), lambda i, j: (i, j, 0))
    in_specs = [tok(d)]
    args = [x]
    if has_ffn:
        in_specs.append(pl.BlockSpec((1, tm * SUBLANES, LANES), lambda i, j: (i, j, 0)))
        args.append(ffn)
    in_specs += [
        pl.BlockSpec((1, M_ROWS, d), lambda i, j: (i, 0, 0)),
        _layer_block(norm_g, layer),
        pl.BlockSpec((None, d, IN_WIDTH), lambda i, j: (layer, 0, 0), pipeline_mode=pl.Buffered(1)),
        pl.BlockSpec((tm, LANES), lambda i, j: (j, 0)),
        pl.BlockSpec((tm, LANES), lambda i, j: (j, 0)),
    ]
    args += [mrows, norm_g, w_in_bf, rope_cs, rope_sn]
    widths = [(2 * CONV_WIDTH, F32), (3 * RET_WIDTH, BF16), (2 * RET_WIDTH, F32), (ATTN_WIDTH, BF16),
              (3 * KV_WIDTH, BF16)]
    out_specs = [tok(w) for w, _ in widths]
    out_shape = [jax.ShapeDtypeStruct((b, tu, w), dt) for w, dt in widths]
    if has_ffn:
        out_specs = [tok(d)] + out_specs
        out_shape = [jax.ShapeDtypeStruct((b, tu, d), F32)] + out_shape
    res = pl.pallas_call(
        functools.partial(_inproj_kernel, has_ffn, n_ctx),
        grid=(b, tu // tm),
        in_specs=in_specs,
        out_specs=out_specs,
        out_shape=out_shape,
        compiler_params=_params("parallel", "parallel"),
        name="inproj",
    )(*args)
    if has_ffn:
        return res[0], res[1:]
    return x, res


def _head_block_mask(n):
    r = lax.broadcasted_iota(I32, (n, n), 0) // HEAD_DIM
    c = lax.broadcasted_iota(I32, (n, n), 1) // HEAD_DIM
    return r == c


STATE_UNROLL = 17


def _states_kernel(n_ctx_chunks, n_chunks, rqkv_ref, win_ref, cd_ref, sp_ref, s_scr):
    same_head = _head_block_mask(RET_WIDTH)
    s_scr[...] = jnp.zeros_like(s_scr)

    def chunk_update(d, pos):
        if d == 0:
            c = pos
        else:
            c = jnp.where(pos < n_ctx_chunks, n_ctx_chunks - 1 - pos, n_chunks + n_ctx_chunks - 1 - pos)
        off = pl.multiple_of(c * CHUNK, CHUNK)
        kw = rqkv_ref[0, pl.ds(off, CHUNK), RET_WIDTH:2 * RET_WIDTH].astype(F32) * win_ref[d]
        v = rqkv_ref[0, pl.ds(off, CHUNK), 2 * RET_WIDTH:]
        kw_t = kw.T.astype(BF16)
        yield
        u = _dot(kw_t, v)
        yield
        return c, jnp.where(same_head, u, 0.0)

    def body(i, carry):
        todo = [(d, i * STATE_UNROLL + n) for n in range(STATE_UNROLL) for d in range(2)]
        done = _round_robin([chunk_update(d, pos) for d, pos in todo])
        for (d, _), (c, u) in zip(todo, done):
            s = s_scr[d]
            sp_ref[0, d, pl.ds(c, 1)] = s.astype(BF16)[None]
            s_scr[d] = s * cd_ref[d][0:1, :] + u
        return carry

    assert n_chunks % STATE_UNROLL == 0
    lax.fori_loop(0, n_chunks // STATE_UNROLL, body, 0)


def _ret_states(rqkv, win, cd, layer, n_ctx):
    b, tu, _ = rqkv.shape
    w = RET_WIDTH
    n_chunks = tu // CHUNK
    return pl.pallas_call(
        functools.partial(_states_kernel, n_ctx // CHUNK, n_chunks),
        grid=(b,),
        in_specs=[
            pl.BlockSpec((1, tu, 3 * w), lambda i: (i, 0, 0)),
            _layer_block(win, layer),
            _layer_block(cd, layer),
        ],
        out_specs=pl.BlockSpec((1, 2, n_chunks, w, w), lambda i: (i, 0, 0, 0, 0)),
        out_shape=jax.ShapeDtypeStruct((b, 2, n_chunks, w, w), BF16),
        scratch_shapes=[pltpu.VMEM((2, w, w), F32)],
        compiler_params=_params("parallel"),
        name="ret_states",
    )(rqkv, win, cd)


def _group_mean(t, ones_bd):
    hi = t.astype(BF16)
    lo = (t - hi.astype(F32)).astype(BF16)
    s = _dot(jnp.concatenate([hi, lo], axis=0), ones_bd)
    n = t.shape[0]
    return (s[:n] + s[n:]) * (1.0 / HEAD_DIM)


def _mix_chunk(c, n_ctx_chunks, n_chunks, cz, z_last, z_first, cw, rqkv, gates, sp, dm_ref, qdec_ref,
               qa, kvs, sink_ref):
    is_lat = c >= n_ctx_chunks

    z = cz[:, CONV_WIDTH:]
    row = lax.broadcasted_iota(I32, (CHUNK, 1), 0)
    has_prev = jnp.logical_and(c != 0, c != n_ctx_chunks)
    has_next = jnp.logical_and(c != n_ctx_chunks - 1, c != n_chunks - 1)
    z_before = jnp.where(row == 0, jnp.where(has_prev, z_last, 0.0), pltpu.roll(z, 1, 0))
    z_after = jnp.where(row == CHUNK - 1, jnp.where(has_next, z_first, 0.0), pltpu.roll(z, CHUNK - 1, 0))
    conv = cz[:, :CONV_WIDTH] * (z_before * cw[0:1, :] + z * cw[1:2, :] + z_after * cw[2:3, :])

    q = rqkv[:, 0:RET_WIDTH]
    k = rqkv[:, RET_WIDTH:2 * RET_WIDTH]
    v = rqkv[:, 2 * RET_WIDTH:]
    lane_head = lax.broadcasted_iota(I32, (1, RET_WIDTH), 1) // HEAD_DIM
    kz = jnp.zeros_like(k)
    k_heads = jnp.concatenate([jnp.where(lane_head == hh, k, kz) for hh in range(RET_HEADS)], axis=0)
    v_heads = jnp.concatenate([jnp.where(lane_head == hh, v, kz) for hh in range(RET_HEADS)], axis=0)
    scores = _dot_nt(q, k_heads)

    keys, v_t = kvs
    half = lax.broadcasted_iota(I32, (1, LANES), 1) // HEAD_DIM
    qa_z = jnp.zeros((CHUNK, LANES), BF16)
    combos = [(hk, par) for hk in range(ATTN_KV_HEADS) for par in range(2)]
    n_ctx = keys[0].shape[0] - 3 * CHUNK
    cols2 = 2 * CHUNK
    ik = lax.broadcasted_iota(I32, (CHUNK, cols2), 0)
    iq = lax.broadcasted_iota(I32, (CHUNK, cols2), 1) & (CHUNK - 1)
    off = jnp.full((CHUNK, cols2), NEG_INF, F32)
    bias_prev = jnp.where(jnp.logical_and(jnp.logical_and(is_lat, c - 1 >= n_ctx_chunks), ik >= iq), 0.0, off)
    bias_cur = jnp.where(is_lat, 0.0, off)
    bias_next = jnp.where(jnp.logical_and(jnp.logical_and(is_lat, c + 1 <= n_chunks - 1), ik <= iq), 0.0, off)
    assert n_ctx > 0
    logits = {}
    for hk, par in combos:
        ja, jb = 2 * hk, 2 * hk + 1
        qst = jnp.concatenate([
            jnp.where(half == par, qa[:, ja * LANES:(ja + 1) * LANES], qa_z),
            jnp.where(half == par, qa[:, jb * LANES:(jb + 1) * LANES], qa_z)], axis=0)
        sel = 0 if par == hk else 1
        r = _dot_nt(keys[sel], qst)
        s = jnp.concatenate([r[:CHUNK] + bias_prev, r[CHUNK:2 * CHUNK] + bias_cur,
                             r[2 * CHUNK:3 * CHUNK] + bias_next, r[3 * CHUNK:]], axis=0)
        ha, hb = ATTN_GROUP * hk + par, ATTN_GROUP * hk + par + 2
        snk = jnp.concatenate([sink_ref[ha * CHUNK:ha * CHUNK + 1, :],
                               sink_ref[hb * CHUNK:hb * CHUNK + 1, :]], axis=1) * LOG2E
        logits[hk, par] = (s, snk, jnp.maximum(jnp.max(s, axis=0, keepdims=True), snk))
    yield

    qf = q.astype(F32)
    outs = []
    for d in range(2):
        p = (scores * dm_ref[d]).astype(BF16)
        lhs = jnp.concatenate([p, (qf * qdec_ref[d]).astype(BF16)], axis=1)
        o = _dot(lhs, jnp.concatenate([v_heads, sp[d]], axis=0))
        outs.append(o)
    o2 = jnp.concatenate(outs, axis=0)
    yield

    att = {}
    ones_keys = jnp.ones((BF16_ROWS, keys[0].shape[0]), BF16)
    for n, (hk, par) in enumerate(combos):
        s, snk, mx = logits[hk, par]
        e = jnp.exp2(s - mx).astype(BF16)
        o_t = _dot(jnp.concatenate([v_t[hk * HEAD_DIM:(hk + 1) * HEAD_DIM, :], ones_keys], axis=0), e)
        den = o_t[HEAD_DIM:HEAD_DIM + 1, :] + jnp.exp2(snk - mx)
        att[hk, par] = o_t[:HEAD_DIM, :] * (1.0 / den)
        if n == 0:
            ones_bd = jnp.where(_head_block_mask(RET_WIDTH), 1.0, 0.0).astype(BF16)
            dl = o2 - _group_mean(o2, ones_bd)
            on = dl * lax.rsqrt(_group_mean(dl * dl, ones_bd) + EPS)
            ret = on[:CHUNK] * gates[:, 0:RET_WIDTH] + on[CHUNK:] * gates[:, RET_WIDTH:]
        yield

    cols = []
    for hk in range(ATTN_KV_HEADS):
        cols.append(jnp.concatenate([att[hk, 0][:, :CHUNK], att[hk, 1][:, :CHUNK]], axis=0).T)
        cols.append(jnp.concatenate([att[hk, 0][:, CHUNK:], att[hk, 1][:, CHUNK:]], axis=0).T)
    return jnp.concatenate([conv, ret] + cols, axis=1)


def _round_robin(gens):
    results = [None] * len(gens)
    active = list(range(len(gens)))
    while active:
        for i in list(active):
            try:
                next(gens[i])
            except StopIteration as done:
                results[i] = done.value
                active.remove(i)
    return results


MIX_CHUNKS = 2


def _mixer_kernel(n_ctx_chunks, n_chunks, n_exp,
                  cz_ref, czp_ref, czn_ref, cw_ref, rqkv_ref, gates_ref, sp_ref, dm_ref, qdec_ref,
                  aq_ref, kvp_ref, kvc_ref, kvn_ref, kvx_ref, sink_ref,
                  x_ref, mod_ref, g_ref, w_ref, wrh_ref, wrl_ref, xo_ref, hg_ref, aff_ref, mix_scr):
    t = pl.program_id(0)
    last = pl.num_programs(0) - 2
    tiles = n_chunks // MIX_CHUNKS

    @pl.when(t == 0)
    def _():
        mix_scr[...] = jnp.zeros_like(mix_scr)

    tail = _outproj_tail(n_ctx_chunks * CHUNK, n_exp, lax.rem(jnp.maximum(t - 1, 0), tiles), mix_scr[...], x_ref,
                         mod_ref, g_ref, w_ref, wrh_ref, wrl_ref, xo_ref, hg_ref, aff_ref)
    j = lax.rem(jnp.minimum(t, last), tiles)
    cw = cw_ref[...]

    def values_t(blk):
        return blk[:, 2 * KV_WIDTH:].astype(F32).T.astype(BF16)

    kv_blocks = ([kvp_ref[0]] + [kvc_ref[0, s * CHUNK:(s + 1) * CHUNK, :] for s in range(MIX_CHUNKS)] + [kvn_ref[0]])
    kv_ctx = [kvx_ref[0, t * CHUNK:(t + 1) * CHUNK, :] for t in range(kvx_ref.shape[1] // CHUNK)]
    vt_blocks = [values_t(blk) for blk in kv_blocks]
    vt_ctx = [values_t(blk) for blk in kv_ctx]
    gens = []
    for s in range(MIX_CHUNKS):
        lo, hi = s * CHUNK, (s + 1) * CHUNK
        if s == 0:
            z_last = czp_ref[0, SUBLANES - 1:SUBLANES, CONV_WIDTH:]
        else:
            z_last = cz_ref[0, lo - 1:lo, CONV_WIDTH:]
        if s == MIX_CHUNKS - 1:
            z_first = czn_ref[0, 0:1, CONV_WIDTH:]
        else:
            z_first = cz_ref[0, hi:hi + 1, CONV_WIDTH:]
        window = kv_blocks[s:s + 3] + kv_ctx
        keys = tuple(jnp.concatenate([blk[:, i * KV_WIDTH:(i + 1) * KV_WIDTH] for blk in window], axis=0)
                     for i in (0, 1))
        v_t = jnp.concatenate(vt_blocks[s:s + 3] + vt_ctx, axis=1)
        gens.append(_mix_chunk(j * MIX_CHUNKS + s, n_ctx_chunks, n_chunks, cz_ref[0, lo:hi, :], z_last, z_first,
                               cw, rqkv_ref[0, lo:hi, :], gates_ref[0, lo:hi, :],
                               (sp_ref[0, 0, s], sp_ref[0, 1, s]), dm_ref, qdec_ref, aq_ref[0, lo:hi, :],
                               (keys, v_t), sink_ref))
    mixes = [mix.astype(BF16) for mix in _round_robin([tail] + gens)[1:]]
    mix_scr[...] = jnp.concatenate(mixes, axis=0)


def _mixer(proj, sp, tabs, conv_w, sink_rows, x, mrows, norm_g, w_out_bf, layer, wr_hi, wr_lo, n_ctx):
    cz, rqkv, gates, aq, kv = proj
    dm, qdec = tabs
    b, tu, _ = cz.shape
    d = x.shape[2]
    n_exp = wr_hi.shape[1]
    n_chunks = tu // CHUNK
    mc = MIX_CHUNKS
    rows = mc * CHUNK
    assert n_chunks % mc == 0 and (n_ctx // CHUNK) % mc == 0
    last = n_chunks - 1
    tiles = n_chunks // mc
    per8 = rows // SUBLANES

    def mixed(t):
        m = jnp.minimum(t, b * tiles - 1)
        return m // tiles, lax.rem(m, tiles)

    def projected(t):
        p = jnp.maximum(t - 1, 0)
        return p // tiles, lax.rem(p, tiles)

    def at_mixed(block, tile_index):
        def index(t):
            i, j = mixed(t)
            return (i, tile_index(j), 0)
        return pl.BlockSpec(block, index)

    cur = lambda w: at_mixed((1, rows, w), lambda j: j)

    def out(r, w):
        return pl.BlockSpec((1, r, w), lambda t: projected(t) + (0,))

    wcz, wkv = cz.shape[2], kv.shape[2]
    in_specs = [
        cur(wcz),
        at_mixed((1, SUBLANES, wcz), lambda j: jnp.maximum(j * per8 - 1, 0)),
        at_mixed((1, SUBLANES, wcz), lambda j: jnp.minimum((j + 1) * per8, tu // SUBLANES - 1)),
        _layer_block(conv_w, layer),
        cur(rqkv.shape[2]), cur(gates.shape[2]),
        pl.BlockSpec((1, 2, mc, RET_WIDTH, RET_WIDTH), lambda t: (mixed(t)[0], 0, mixed(t)[1], 0, 0)),
        _layer_block(dm, layer), _layer_block(qdec, layer),
        cur(ATTN_WIDTH),
        at_mixed((1, CHUNK, wkv), lambda j: jnp.maximum(j * mc - 1, 0)),
        cur(wkv),
        at_mixed((1, CHUNK, wkv), lambda j: jnp.minimum((j + 1) * mc, last)),
        at_mixed((1, n_ctx, wkv), lambda j: 0),
        _layer_block(sink_rows, layer),
        out(rows, d),
        pl.BlockSpec((1, M_ROWS, d), lambda t: (projected(t)[0], 0, 0)),
        _layer_block(norm_g, layer),
        _layer_block(w_out_bf, layer),
        _layer_block(wr_hi, layer), _layer_block(wr_lo, layer),
    ]
    args = [cz, cz, cz, conv_w, rqkv, gates, sp, dm, qdec, aq, kv, kv, kv, kv, sink_rows,
            x, mrows, norm_g, w_out_bf, wr_hi, wr_lo]
    return pl.pallas_call(
        functools.partial(_mixer_kernel, n_ctx // CHUNK, n_chunks, n_exp),
        grid=(b * tiles + 1,),
        in_specs=in_specs,
        out_specs=[out(rows, d), out(rows * SUBLANES, LANES),
                   pl.BlockSpec((1, n_exp, rows), lambda t: (projected(t)[0], 0, projected(t)[1]))],
        out_shape=[
            jax.ShapeDtypeStruct((b, tu, d), F32),
            jax.ShapeDtypeStruct((b, tu * SUBLANES, LANES), F32),
            jax.ShapeDtypeStruct((b, n_exp, tu), F32),
        ],
        scratch_shapes=[pltpu.VMEM((rows, CONV_WIDTH + RET_WIDTH + ATTN_WIDTH), BF16)],
        compiler_params=_params("arbitrary"),
        name="mixer",
    )(*args)


def _outproj_tail(n_ctx, n_exp, tile, mix, x_ref, mod_ref, g_ref, w_ref, wrh_ref, wrl_ref, xo_ref, hg_ref, aff_ref):
    tm = x_ref.shape[1]
    row = _mod_rows(mod_ref, tm, n_ctx, tile)

    y = _dot(mix, w_ref[...])
    yield
    x = x_ref[0] + row(M_G1) * y
    xo_ref[0] = x
    h = (_rms(x) * g_ref[...]) * (1.0 + row(M_SC2)) + row(M_SH2)
    for q in range(SUBLANES):
        hg_ref[0, pl.ds(q, tm, stride=SUBLANES), :] = h[:, q * LANES:(q + 1) * LANES]
    yield
    h_hi = h.astype(BF16)
    h_lo = (h - h_hi.astype(F32)).astype(BF16)
    logits = (_dot_nt(wrh_ref[...], h_hi) + (_dot_nt(wrh_ref[...], h_lo) + _dot_nt(wrl_ref[...], h_hi)))
    yield
    e = jnp.exp(logits - jnp.max(logits, axis=0, keepdims=True))
    aff_ref[0] = e / jnp.sum(e, axis=0, keepdims=True)


MIN_NORMAL_F32_BITS = 0x00800000
ROUTE_TABLES = 4
ROUTE_GROUP = 4


def _route_stream_tables(v, k, tab_scr, ce_scr, cnt_scr, stream, thr):
    n_exp, n = v.shape
    lane = lax.broadcasted_iota(I32, (1, LANES), 1)
    tri_r = lax.broadcasted_iota(I32, (LANES, LANES), 0)
    tri_c = lax.broadcasted_iota(I32, (LANES, LANES), 1)
    upper = jnp.where(tri_r <= tri_c, 1.0, 0.0).astype(BF16)
    above = pltpu.bitcast(jnp.maximum(thr + 1, MIN_NORMAL_F32_BITS), F32)
    floor = pltpu.bitcast(thr, F32)
    n_gt = jnp.sum(jnp.where(v >= above, 1, 0), axis=1, keepdims=True)
    need = (k - n_gt).astype(F32)
    seen_eq = jnp.zeros((n_exp, 1), F32)
    counts = jnp.zeros((n_exp, LANES), F32)
    for j in range(n // LANES):
        vb = v[:, j * LANES:(j + 1) * LANES]
        gt = vb >= above
        eq = jnp.logical_and(vb >= floor, vb < above)
        both = jnp.concatenate([jnp.where(gt, 1.0, 0.0), jnp.where(eq, 1.0, 0.0)], axis=0).astype(BF16)
        pref = _dot(both, upper)
        rank_eq = pref[n_exp:] + seen_eq
        inc = pref[:n_exp] + jnp.minimum(rank_eq, need) - jnp.minimum(seen_eq, need)
        sel = jnp.logical_or(gt, jnp.logical_and(eq, rank_eq <= need))
        seen_eq = rank_eq[:, LANES - 1:LANES]
        counts = counts + jnp.where(lane == j, inc[:, LANES - 1:LANES], 0.0)
        a0 = jnp.where(sel, vb, 0.0)
        t0 = a0.astype(BF16).astype(F32)
        t1 = (a0 - t0).astype(BF16).astype(F32)
        t2 = a0 - t0 - t1
        for t, val in enumerate((inc, t0, t1, t2)):
            tab_scr[stream * ROUTE_TABLES + t, pl.ds(j, n_exp, stride=LANES), :] = val
    through = _dot(counts.astype(BF16), upper)
    for e in range(n_exp):
        ce_scr[stream, e] = through[e:e + 1, :]
        cnt_scr[stream, e] = counts[e:e + 1, :]


def _route_slots(e, stream, k, lo, tab_scr, ce_scr, cnt_scr):
    lane_f = lax.broadcasted_iota(I32, (1, LANES), 1).astype(F32)
    slot = lax.broadcasted_iota(I32, (k, LANES), 0).astype(F32)
    ones_rows = jnp.ones((LANES, LANES), BF16)
    cnt_rows = jnp.broadcast_to(cnt_scr[stream, e], (LANES, LANES)).astype(BF16)
    rows_e = pl.ds(pl.multiple_of(e * LANES, LANES), LANES)
    tab = jnp.concatenate([tab_scr[stream * ROUTE_TABLES + t, rows_e, :] for t in range(ROUTE_TABLES)],
                          axis=1).astype(BF16)
    before = jnp.where(ce_scr[stream, e] <= slot, 1.0, 0.0).astype(BF16)
    blk = _dot_nt(before, ones_rows)
    base = _dot_nt(before, cnt_rows)
    yield
    row = _dot(jnp.where(lane_f == blk, 1.0, 0.0).astype(BF16), tab)
    yield
    inc = row[:, :LANES]
    aff = row[:, LANES:2 * LANES] + row[:, 2 * LANES:3 * LANES] + row[:, 3 * LANES:]
    local = slot - base
    pos = _dot_nt(jnp.where(inc <= local, 1.0, 0.0).astype(BF16), ones_rows)
    yield
    tok = blk * LANES + pos + float(lo)
    gate = jnp.sum(jnp.where(inc == local + 1.0, aff, 0.0), axis=1, keepdims=True)
    return tok, gate


def _route_kernel(n_exp, streams, aff_ref, idx_ref, gate_ref, tab_scr, ce_scr, cnt_scr):
    a = aff_ref[0]
    tab_scr[...] = jnp.zeros_like(tab_scr)
    vs = [a[:, lo:lo + n] for lo, n, _, _ in streams]

    def enough(v, cand, k):
        return jnp.sum(jnp.where(v >= pltpu.bitcast(cand, F32), 1, 0), axis=1, keepdims=True) >= k

    def search(i, ts):
        out = []
        for t, v, (_, _, k, _) in zip(ts, vs, streams):
            hi = t | jnp.left_shift(jnp.int32(1), 30 - 2 * i)
            lo = jnp.left_shift(jnp.int32(1), 29 - 2 * i)
            out.append(jnp.where(enough(v, hi, k), jnp.where(enough(v, hi | lo, k), hi | lo, hi),
                                 jnp.where(enough(v, t | lo, k), t | lo, t)))
        return tuple(out)

    thrs = lax.fori_loop(0, 15, search, tuple(jnp.zeros((n_exp, 1), I32) for _ in streams))
    thrs = tuple(jnp.where(enough(v, t | 1, k), t | 1, t) for t, v, (_, _, k, _) in zip(thrs, vs, streams))
    for st, (v, (_, _, k, _)) in enumerate(zip(vs, streams)):
        _route_stream_tables(v, k, tab_scr, ce_scr, cnt_scr, st, thrs[st])

    slots = idx_ref.shape[3]

    assert [s[3] for s in streams] == [sum(s[2] for s in streams[:n]) for n in range(len(streams))]

    def expert_group(g, carry):
        experts = [g * ROUTE_GROUP + u for u in range(ROUTE_GROUP)]
        found = _round_robin([_route_slots(e, st, k, lo, tab_scr, ce_scr, cnt_scr)
                              for e in experts for st, (lo, _, k, _) in enumerate(streams)])
        for u, e in enumerate(experts):
            toks = []
            for st, (_, _, k, slot0) in enumerate(streams):
                tok, gate = found[u * len(streams) + st]
                toks.append(tok)
                gate_ref[0, pl.ds(e, 1), slot0:slot0 + k, :] = gate[None]
            toks.append(jnp.zeros((-slots % LANES, LANES), F32))
            row = jnp.concatenate(toks, axis=0).T[0:1, :slots]
            idx_ref[0, pl.ds(e, 1)] = row.astype(I32)[None]
        return carry

    assert n_exp % ROUTE_GROUP == 0
    lax.fori_loop(0, n_exp // ROUTE_GROUP, expert_group, 0)


def _route(aff, n_ctx):
    b, n_exp, tu = aff.shape
    n_lat = tu - n_ctx
    cap_l = CAPACITY_FACTOR * n_lat // n_exp
    cap_c = CAPACITY_FACTOR * n_ctx // n_exp
    slots = cap_l + cap_c
    assert n_lat // LANES <= LANES and n_ctx // LANES <= LANES
    streams = ((n_ctx, n_lat, cap_l, 0), (0, n_ctx, cap_c, cap_l))
    return pl.pallas_call(
        functools.partial(_route_kernel, n_exp, streams),
        grid=(b,),
        in_specs=[pl.BlockSpec((1, n_exp, tu), lambda i: (i, 0, 0))],
        out_specs=[pl.BlockSpec((1, n_exp, 1, slots), lambda i: (i, 0, 0, 0)),
                   pl.BlockSpec((1, n_exp, slots, 1), lambda i: (i, 0, 0, 0))],
        out_shape=[jax.ShapeDtypeStruct((b, n_exp, 1, slots), I32),
                   jax.ShapeDtypeStruct((b, n_exp, slots, 1), F32)],
        scratch_shapes=[pltpu.VMEM((len(streams) * ROUTE_TABLES, n_exp * LANES, LANES), F32),
                        pltpu.VMEM((len(streams), n_exp, 1, LANES), F32),
                        pltpu.VMEM((len(streams), n_exp, 1, LANES), F32)],
        compiler_params=_params("parallel"),
        name="route",
    )(aff)


def _slot_pitch(slots):
    return slots + SUBLANES


def _zero_slot_padding(ref, lead, slots, pitch):
    for q in range(SUBLANES):
        ref[lead + (pl.ds(q * pitch + slots, pitch - slots), slice(None))] = jnp.zeros((pitch - slots, LANES), ref.dtype)


def _slot_pitch_bf16(slots):
    return -(-slots // BF16_ROWS) * BF16_ROWS + BF16_ROWS


DISPATCH_EXPERTS = 4


def _gather_kernel(slots, pitch, pitch_out, idx_ref, h_ref, o_ref, tile_scr):
    for g in range(DISPATCH_EXPERTS):
        for mi in range(slots):
            r = pl.multiple_of(idx_ref[g, 0, mi] * SUBLANES, SUBLANES)
            tile_scr[g, pl.ds(mi, SUBLANES, stride=pitch), :] = h_ref[0, pl.ds(r, SUBLANES), :]
        _zero_slot_padding(o_ref, (0, g), slots, pitch_out)
        for q in range(SUBLANES):
            o_ref[0, g, q * pitch_out:q * pitch_out + slots, :] = (
                tile_scr[g, q * pitch:q * pitch + slots, :].astype(BF16))


def _gather(hg, idx_rows, n_exp, slots):
    b = hg.shape[0]
    pitch = _slot_pitch(slots)
    pitch_out = _slot_pitch_bf16(slots)
    ge = DISPATCH_EXPERTS
    assert n_exp % ge == 0
    return pl.pallas_call(
        functools.partial(_gather_kernel, slots, pitch, pitch_out),
        grid=(b, n_exp // ge),
        in_specs=[
            pl.BlockSpec((ge, 1, slots), lambda i, e: (i * (n_exp // ge) + e, 0, 0), memory_space=pltpu.SMEM),
            pl.BlockSpec((1,) + hg.shape[1:], lambda i, e: (i, 0, 0)),
        ],
        out_specs=pl.BlockSpec((1, ge, SUBLANES * pitch_out, LANES), lambda i, e: (i, e, 0, 0)),
        out_shape=jax.ShapeDtypeStruct((b, n_exp, SUBLANES * pitch_out, LANES), BF16),
        scratch_shapes=[pltpu.VMEM((ge, SUBLANES * pitch, LANES), F32)],
        compiler_params=_params("parallel", "arbitrary"),
        name="gather",
    )(idx_rows, hg)


FFN_SAMPLES = 2
FFN_VMEM_LIMIT_BYTES = 60 * 1024 * 1024


def _ffn_kernel(slots, pitch_in, pitch, xs_ref, wg_ref, wu_ref, wd_ref, gate_ref, y_ref):
    n = xs_ref.shape[0]
    x = jnp.concatenate(
        [jnp.concatenate([xs_ref[s, 0, q * pitch_in:q * pitch_in + slots, :] for q in range(SUBLANES)], axis=1)
         for s in range(n)], axis=0)
    gate = jnp.concatenate([gate_ref[s, 0] for s in range(n)], axis=0)
    a = _dot(x, wg_ref[0].astype(BF16))
    u = _dot(x, wu_ref[0].astype(BF16))
    y = _dot((_silu(a) * u).astype(BF16), wd_ref[0].astype(BF16)) * gate
    for s in range(n):
        _zero_slot_padding(y_ref, (s, 0), slots, pitch)
        for q in range(SUBLANES):
            y_ref[s, 0, q * pitch:q * pitch + slots, :] = y[s * slots:(s + 1) * slots, q * LANES:(q + 1) * LANES]


def _ffn(xs, w_gate, w_up, w_down, layer, gate, slots):
    b, n_exp = xs.shape[:2]
    pitch = _slot_pitch(slots)
    pitch_in = _slot_pitch_bf16(slots)
    d, f = w_gate.shape[2:]
    ns = FFN_SAMPLES
    assert b % ns == 0
    slot_tile = lambda p: pl.BlockSpec((ns, 1, SUBLANES * p, LANES), lambda e, i: (i, e, 0, 0))
    weight = lambda rows, cols: pl.BlockSpec((None, 1, rows, cols), lambda e, i: (layer, e, 0, 0))
    return pl.pallas_call(
        functools.partial(_ffn_kernel, slots, pitch_in, pitch),
        grid=(n_exp, b // ns),
        in_specs=[
            slot_tile(pitch_in),
            weight(d, f), weight(d, f), weight(f, d),
            pl.BlockSpec((ns, 1, slots, 1), lambda e, i: (i, e, 0, 0)),
        ],
        out_specs=slot_tile(pitch),
        out_shape=jax.ShapeDtypeStruct((b, n_exp, SUBLANES * pitch, LANES), F32),
        compiler_params=_params("parallel", "arbitrary", vmem=FFN_VMEM_LIMIT_BYTES),
        name="ffn",
    )(xs, w_gate, w_up, w_down, gate)


SCATTER_BATCH = 16


def _scatter_kernel(slots, pitch, idx_ref, y_ref, o_ref):
    @pl.when(pl.program_id(1) == 0)
    def _():
        o_ref[...] = jnp.zeros_like(o_ref)

    for g in range(DISPATCH_EXPERTS):
        for m0 in range(0, slots, SCATTER_BATCH):
            rows = [pl.multiple_of(idx_ref[g, 0, m0 + u] * SUBLANES, SUBLANES) for u in range(SCATTER_BATCH)]
            vals = [o_ref[0, pl.ds(rows[u], SUBLANES), :] + y_ref[0, g, pl.ds(m0 + u, SUBLANES, stride=pitch), :]
                    for u in range(SCATTER_BATCH)]
            for u in range(SCATTER_BATCH):
                o_ref[0, pl.ds(rows[u], SUBLANES), :] = vals[u]


def _scatter(y, idx_rows, tu, slots):
    b, n_exp = y.shape[:2]
    pitch = _slot_pitch(slots)
    ge = DISPATCH_EXPERTS
    assert n_exp % ge == 0 and slots % SCATTER_BATCH == 0
    return pl.pallas_call(
        functools.partial(_scatter_kernel, slots, pitch),
        grid=(b, n_exp // ge),
        in_specs=[
            pl.BlockSpec((ge, 1, slots), lambda i, e: (i * (n_exp // ge) + e, 0, 0), memory_space=pltpu.SMEM),
            pl.BlockSpec((1, ge, SUBLANES * pitch, LANES), lambda i, e: (i, e, 0, 0)),
        ],
        out_specs=pl.BlockSpec((1, tu * SUBLANES, LANES), lambda i, e: (i, 0, 0)),
        out_shape=jax.ShapeDtypeStruct((b, tu * SUBLANES, LANES), F32),
        compiler_params=_params("parallel", "arbitrary"),
        name="scatter",
    )(idx_rows, y)


FINAL_PARTS = 8


def _final_kernel(mod_ref, g_ref, *refs):
    o_ref = refs[-1]
    tm = FINAL_TILE
    for n in range(FINAL_PARTS):
        x_ref, f_ref = refs[n], refs[FINAL_PARTS + n]
        x = x_ref[0] + mod_ref[0][M_G2:M_G2 + 1, :] * _from_gather_layout(f_ref, tm)
        o_ref[0, n * tm:(n + 1) * tm, :] = _rms(x) * g_ref[...]


def _final(x, ffn, mrows, final_g, n_ctx):
    b, tu, d = x.shape
    tm = FINAL_TILE
    skip = n_ctx // tm
    parts = FINAL_PARTS
    assert (tu - n_ctx) % (tm * parts) == 0
    tile = lambda n: pl.BlockSpec((1, tm, d), lambda i, j: (i, j * parts + n + skip, 0))
    ftile = lambda n: pl.BlockSpec((1, tm * SUBLANES, LANES), lambda i, j: (i, j * parts + n + skip, 0))
    return pl.pallas_call(
        _final_kernel,
        grid=(b, (tu - n_ctx) // (tm * parts)),
        in_specs=([pl.BlockSpec((1, M_ROWS, d), lambda i, j: (i, 0, 0)), pl.BlockSpec((1, d), lambda i, j: (0, 0))]
                  + [tile(n) for n in range(parts)] + [ftile(n) for n in range(parts)]),
        out_specs=pl.BlockSpec((1, tm * parts, d), lambda i, j: (i, j, 0)),
        out_shape=jax.ShapeDtypeStruct((b, tu - n_ctx, d), F32),
        compiler_params=_params("parallel", "parallel"),
        name="final_norm",
    )(mrows, final_g.reshape(1, d), *([x] * parts + [ffn] * parts))


def _rope_tables(n_lat, n_ctx):
    rows = n_lat // GRID_W
    rowp = jnp.repeat(jnp.arange(rows, dtype=F32), GRID_W)
    colp = jnp.tile(jnp.arange(GRID_W, dtype=F32), rows)
    axis_dim = HEAD_DIM // 2
    inv_freq = ROPE_BASE ** (-jnp.arange(0, axis_dim, 2, dtype=F32) / axis_dim)
    ar = rowp[:, None] * inv_freq
    ac = colp[:, None] * inv_freq
    cs = jnp.concatenate([jnp.cos(ar), jnp.cos(ar), jnp.cos(ac), jnp.cos(ac)], axis=1)
    sn = jnp.concatenate([-jnp.sin(ar), jnp.sin(ar), -jnp.sin(ac), jnp.sin(ac)], axis=1)
    reps = LANES // HEAD_DIM
    cs = jnp.concatenate([jnp.ones((n_ctx, LANES), F32), jnp.tile(cs, (1, reps))], axis=0)
    sn = jnp.concatenate([jnp.zeros((n_ctx, LANES), F32), jnp.tile(sn, (1, reps))], axis=0)
    return cs, sn


def kernel(x, c, ctx, c_ctx, w_mod, b_mod, norm1_g, norm2_g, w_in, conv_w, ret_decay_logit, attn_sink,
           w_out, w_router, w_gate, w_up, w_down, final_g):
    b, n_lat, d = x.shape
    n_ctx = ctx.shape[1]
    depth = w_in.shape[0]
    n_exp = w_router.shape[2]
    tu = n_ctx + n_lat
    assert w_in.shape[2] == IN_WIDTH and tu % INPROJ_TILE == 0
    assert n_ctx % FINAL_TILE == 0 and n_lat % FINAL_TILE == 0
    assert b + 1 <= SUBLANES and n_exp == N_EXPERTS

    c_rows = jnp.concatenate([c, c_ctx[None], jnp.zeros((SUBLANES - b - 1, d), F32)], axis=0)
    mods = _mod_vectors(c_rows, w_mod, b_mod).reshape(depth, SUBLANES, 6, d)
    qdec, win, cd, dm = _decay_tables(ret_decay_logit)
    rope_cs, rope_sn = _rope_tables(n_lat, n_ctx)
    sink_rows = jnp.broadcast_to(jnp.repeat(attn_sink.astype(F32), CHUNK, axis=1)[:, :, None],
                                 (depth, ATTN_HEADS * CHUNK, LANES))
    w_in_bf = w_in.astype(BF16)
    w_out_bf = w_out.astype(BF16)
    wr = jnp.swapaxes(w_router, 1, 2)
    wr_hi = wr.astype(BF16)
    wr_lo = (wr - wr_hi.astype(F32)).astype(BF16)
    g1 = norm1_g.reshape(depth, 1, d)
    g2 = norm2_g.reshape(depth, 1, d)
    mrows = jnp.concatenate([mods[:, :b], jnp.broadcast_to(mods[:, b][:, None], (depth, b, 6, d)),
                             jnp.zeros((depth, b, M_ROWS - 12, d), F32)], axis=2)
    mrows_in = mrows.at[1:, :, M_G2].set(mrows[:-1, :, M_G2])
    mrows_in = mrows_in.at[1:, :, M_CTX + M_G2].set(mrows[:-1, :, M_CTX + M_G2])

    xu = jnp.concatenate([ctx, x], axis=1)
    ffn = None
    cap = CAPACITY_FACTOR * n_lat // n_exp + CAPACITY_FACTOR * n_ctx // n_exp
    for l in range(depth):
        xu, proj = _inproj(xu, ffn, mrows_in[l], g1, w_in_bf, l, rope_cs, rope_sn, n_ctx)
        sp = _ret_states(proj[1], win, cd, l, n_ctx)
        xu, hg, aff = _mixer(proj, sp, (dm, qdec), conv_w, sink_rows, xu, mrows[l], g2, w_out_bf, l,
                             wr_hi, wr_lo, n_ctx)
        idx, gate = _route(aff, n_ctx)
        idx_rows = idx.reshape(b * n_exp, 1, cap)
        xs = _gather(hg, idx_rows, n_exp, cap)
        y = _ffn(xs, w_gate, w_up, w_down, l, gate, cap)
        ffn = _scatter(y, idx_rows, tu, cap)
    return _final(xu, ffn, mrows[depth - 1], final_g, n_ctx)
```

```python
import functools

import jax
import jax.numpy as jnp
from jax import lax
from jax.experimental import pallas as pl
from jax.experimental.pallas import tpu as pltpu

F32 = jnp.float32
BF16 = jnp.bfloat16
I32 = jnp.int32

HEAD_DIM = 64
CONV_WIDTH = 256
RET_HEADS = 4
RET_WIDTH = RET_HEADS * HEAD_DIM
ATTN_HEADS = 8
ATTN_KV_HEADS = 2
ATTN_GROUP = ATTN_HEADS // ATTN_KV_HEADS
ATTN_WIDTH = ATTN_HEADS * HEAD_DIM
KV_WIDTH = ATTN_KV_HEADS * HEAD_DIM
CHUNK = 128
GRID_W = 64
N_EXPERTS = 16
CAPACITY_FACTOR = 2
ROPE_BASE = 10000.0
EPS = 1e-6
NEG_INF = -1e30
QK_SCALE = HEAD_DIM ** -0.5
LOG2E = 1.4426950408889634

LANES = 128
SUBLANES = 8
BF16_ROWS = 2 * SUBLANES
VMEM_LIMIT_BYTES = 56 * 1024 * 1024

O_CB = 0
O_CC = O_CB + CONV_WIDTH
O_CX = O_CC + CONV_WIDTH
O_RQ = O_CX + CONV_WIDTH
O_RK = O_RQ + RET_WIDTH
O_RV = O_RK + RET_WIDTH
O_GF = O_RV + RET_WIDTH
O_GB = O_GF + RET_WIDTH
O_AQ = O_GB + RET_WIDTH
O_AK = O_AQ + ATTN_WIDTH
O_AV = O_AK + KV_WIDTH
IN_WIDTH = O_AV + KV_WIDTH

M_SH1, M_SC1, M_G1, M_SH2, M_SC2, M_G2 = range(6)
M_CTX = 6
M_ROWS = 16

INPROJ_TILE = 1088
FINAL_TILE = 256


def _params(*sem, vmem=VMEM_LIMIT_BYTES):
    return pltpu.CompilerParams(dimension_semantics=sem, vmem_limit_bytes=vmem)


def _dot(a, b):
    return jnp.dot(a, b, preferred_element_type=F32)


def _dot_nt(a, b):
    return lax.dot_general(a, b, (((1,), (1,)), ((), ())), preferred_element_type=F32)


def _silu(v):
    return v * jax.nn.sigmoid(v)


def _mod_kernel(c_ref, w_ref, b_ref, o_ref):
    s = _silu(c_ref[...])
    s_hi = s.astype(BF16)
    s_lo = (s - s_hi.astype(F32)).astype(BF16)
    w = w_ref[0]
    w_hi = w.astype(BF16)
    w_lo = (w - w_hi.astype(F32)).astype(BF16)
    n = s.shape[0]
    head = _dot(jnp.concatenate([s_hi, s_lo], axis=0), w_hi)
    o_ref[0] = head[:n] + (head[n:] + _dot(s_hi, w_lo)) + b_ref[0]


def _mod_vectors(c_rows, w_mod, b_mod):
    depth, d_model, width = w_mod.shape
    tn = 1536
    return pl.pallas_call(
        _mod_kernel,
        grid=(depth, width // tn),
        in_specs=[
            pl.BlockSpec((SUBLANES, d_model), lambda l, n: (0, 0)),
            pl.BlockSpec((1, d_model, tn), lambda l, n: (l, 0, n)),
            pl.BlockSpec((1, 1, tn), lambda l, n: (l, 0, n)),
        ],
        out_specs=pl.BlockSpec((1, SUBLANES, tn), lambda l, n: (l, 0, n)),
        out_shape=jax.ShapeDtypeStruct((depth, SUBLANES, width), F32),
        compiler_params=_params("parallel", "parallel"),
        name="mod_vectors",
    )(c_rows, w_mod, b_mod.reshape(depth, 1, width))


def _log_sigmoid(v):
    return -jnp.log(1.0 + jnp.exp(-v))


def _tables_kernel(lgl_ref, lgr_ref, qdec_ref, win_ref, cd_ref, dm_ref):
    pos = lax.broadcasted_iota(I32, (CHUNK, RET_WIDTH), 0).astype(F32)
    ri = lax.broadcasted_iota(I32, (CHUNK, RET_HEADS * CHUNK), 0).astype(F32)
    rj = (lax.broadcasted_iota(I32, (CHUNK, RET_HEADS * CHUNK), 1) & (CHUNK - 1)).astype(F32)
    for d in range(2):
        lg = _log_sigmoid(lgl_ref[0, d])
        lg1 = lg[0:1, :]
        if d == 0:
            qdec_ref[0, d] = jnp.exp(lg1 * (pos + 1.0))
            win_ref[0, d] = jnp.exp(lg1 * (CHUNK - 1.0 - pos))
            diff = ri - rj
        else:
            qdec_ref[0, d] = jnp.exp(lg1 * (CHUNK - pos))
            win_ref[0, d] = jnp.exp(lg1 * pos)
            diff = rj - ri
        cd_ref[0, d] = jnp.exp(lg * float(CHUNK))
        lr = _log_sigmoid(lgr_ref[0, d])
        dm_ref[0, d] = jnp.where(diff >= 0.0, jnp.exp(lr * jnp.maximum(diff, 0.0)), 0.0)


def _decay_tables(ret_decay_logit):
    depth = ret_decay_logit.shape[0]
    lg = ret_decay_logit.astype(F32)
    lgl = jnp.broadcast_to(jnp.repeat(lg, HEAD_DIM, axis=-1)[:, :, None, :], (depth, 2, SUBLANES, RET_WIDTH))
    lgr = jnp.broadcast_to(jnp.repeat(lg, CHUNK, axis=-1)[:, :, None, :], (depth, 2, CHUNK, RET_HEADS * CHUNK))
    return pl.pallas_call(
        _tables_kernel,
        grid=(depth,),
        in_specs=[
            pl.BlockSpec((1, 2, SUBLANES, RET_WIDTH), lambda l: (l, 0, 0, 0)),
            pl.BlockSpec((1, 2, CHUNK, RET_HEADS * CHUNK), lambda l: (l, 0, 0, 0)),
        ],
        out_specs=[
            pl.BlockSpec((1, 2, CHUNK, RET_WIDTH), lambda l: (l, 0, 0, 0)),
            pl.BlockSpec((1, 2, CHUNK, RET_WIDTH), lambda l: (l, 0, 0, 0)),
            pl.BlockSpec((1, 2, SUBLANES, RET_WIDTH), lambda l: (l, 0, 0, 0)),
            pl.BlockSpec((1, 2, CHUNK, RET_HEADS * CHUNK), lambda l: (l, 0, 0, 0)),
        ],
        out_shape=[
            jax.ShapeDtypeStruct((depth, 2, CHUNK, RET_WIDTH), F32),
            jax.ShapeDtypeStruct((depth, 2, CHUNK, RET_WIDTH), F32),
            jax.ShapeDtypeStruct((depth, 2, SUBLANES, RET_WIDTH), F32),
            jax.ShapeDtypeStruct((depth, 2, CHUNK, RET_HEADS * CHUNK), F32),
        ],
        compiler_params=_params("parallel"),
        name="decay_tables",
    )(lgl, lgr)


def _rope(v, cs, sn):
    lane = lax.broadcasted_iota(I32, (1, LANES), 1)
    first = (lane & 31) < 16
    outs = []
    for g in range(v.shape[1] // LANES):
        vg = v[:, g * LANES:(g + 1) * LANES]
        sw = jnp.where(first, pltpu.roll(vg, LANES - 16, 1), pltpu.roll(vg, 16, 1))
        outs.append(vg * cs + sw * sn)
    return outs[0] if len(outs) == 1 else jnp.concatenate(outs, axis=1)


def _swap_halves(v):
    return jnp.concatenate([v[:, HEAD_DIM:], v[:, :HEAD_DIM]], axis=1)


def _from_gather_layout(f_ref, tm):
    return jnp.concatenate(
        [f_ref[0, pl.ds(q, tm, stride=SUBLANES), :] for q in range(SUBLANES)], axis=1)


def _rms(v):
    return v * lax.rsqrt(jnp.mean(v * v, axis=-1, keepdims=True) + EPS)


def _mod_rows(mod_ref, tm, n_ctx, tile):
    m = mod_ref[0]
    if n_ctx % tm == 0:
        is_ctx = tile < n_ctx // tm
    else:
        is_ctx = tile * tm + lax.broadcasted_iota(I32, (tm, 1), 0) < n_ctx

    def row(r):
        return jnp.where(is_ctx, m[M_CTX + r:M_CTX + r + 1, :], m[r:r + 1, :])

    return row


def _inproj_kernel(has_ffn, n_ctx, *refs):
    if has_ffn:
        x_ref, f_ref, mod_ref, g_ref, w_ref, cs_ref, sn_ref = refs[:7]
        outs = refs[7:]
        xo_ref = outs[0]
        outs = outs[1:]
    else:
        x_ref, mod_ref, g_ref, w_ref, cs_ref, sn_ref = refs[:6]
        outs = refs[6:]
    cz_ref, rqkv_ref, gates_ref, aq_ref, kv_ref = outs
    tm = x_ref.shape[1]
    row = _mod_rows(mod_ref, tm, n_ctx, pl.program_id(1))

    x = x_ref[0]
    if has_ffn:
        x = x + row(M_G2) * _from_gather_layout(f_ref, tm)
        xo_ref[0] = x
    h = (_rms(x) * g_ref[...]) * (1.0 + row(M_SC1)) + row(M_SH1)
    h = h.astype(BF16)
    cs = cs_ref[...]
    sn = sn_ref[...]

    def proj(a, b):
        return _dot(h, w_ref[:, a:b])

    bounds = (O_CB, O_CC, O_CX, O_RQ, O_RK, O_RV, O_GF, O_GB, O_AQ, O_AK, O_AV, IN_WIDTH)
    p = [proj(a, b) for a, b in zip(bounds[:-1], bounds[1:])]
    cz_ref[0, :, 0:CONV_WIDTH] = p[0]
    cz_ref[0, :, CONV_WIDTH:] = p[1] * p[2]
    rqkv_ref[0, :, 0:RET_WIDTH] = _rope(p[3], cs, sn).astype(BF16)
    rqkv_ref[0, :, RET_WIDTH:2 * RET_WIDTH] = _rope(p[4] * QK_SCALE, cs, sn).astype(BF16)
    rqkv_ref[0, :, 2 * RET_WIDTH:] = p[5].astype(BF16)
    gates_ref[0, :, 0:RET_WIDTH] = _silu(p[6])
    gates_ref[0, :, RET_WIDTH:] = _silu(p[7])
    aq_ref[0] = (_rope(p[8], cs, sn) * (QK_SCALE * LOG2E)).astype(BF16)
    ak = _rope(p[9], cs, sn)
    av = p[10]
    kv_ref[0, :, 0:KV_WIDTH] = ak.astype(BF16)
    kv_ref[0, :, KV_WIDTH:2 * KV_WIDTH] = _swap_halves(ak).astype(BF16)
    kv_ref[0, :, 2 * KV_WIDTH:] = av.astype(BF16)


def _layer_block(stacked, layer):
    rest = stacked.shape[1:]
    return pl.BlockSpec((None,) + rest, lambda *_: (layer,) + (0,) * len(rest))


def _inproj(x, ffn, mrows, norm_g, w_in_bf, layer, rope_cs, rope_sn, n_ctx):
    b, tu, d = x.shape
    tm = INPROJ_TILE
    has_ffn = ffn is not None
    tok = lambda width: pl.BlockSpec((1, tm, width), lambda i, j: (i, j, 0))
    in_specs = [tok(d)]
    args = [x]
    if has_ffn:
        in_specs.append(pl.BlockSpec((1, tm * SUBLANES, LANES), lambda i, j: (i, j, 0)))
        args.append(ffn)
    in_specs += [
        pl.BlockSpec((1, M_ROWS, d), lambda i, j: (i, 0, 0)),
        _layer_block(norm_g, layer),
        pl.BlockSpec((None, d, IN_WIDTH), lambda i, j: (layer, 0, 0), pipeline_mode=pl.Buffered(1)),
        pl.BlockSpec((tm, LANES), lambda i, j: (j, 0)),
        pl.BlockSpec((tm, LANES), lambda i, j: (j, 0)),
    ]
    args += [mrows, norm_g, w_in_bf, rope_cs, rope_sn]
    widths = [(2 * CONV_WIDTH, F32), (3 * RET_WIDTH, BF16), (2 * RET_WIDTH, F32), (ATTN_WIDTH, BF16),
              (3 * KV_WIDTH, BF16)]
    out_specs = [tok(w) for w, _ in widths]
    out_shape = [jax.ShapeDtypeStruct((b, tu, w), dt) for w, dt in widths]
    if has_ffn:
        out_specs = [tok(d)] + out_specs
        out_shape = [jax.ShapeDtypeStruct((b, tu, d), F32)] + out_shape
    res = pl.pallas_call(
        functools.partial(_inproj_kernel, has_ffn, n_ctx),
        grid=(b, tu // tm),
        in_specs=in_specs,
        out_specs=out_specs,
        out_shape=out_shape,
        compiler_params=_params("parallel", "parallel"),
        name="inproj",
    )(*args)
    if has_ffn:
        return res[0], res[1:]
    return x, res


def _head_block_mask(n):
    r = lax.broadcasted_iota(I32, (n, n), 0) // HEAD_DIM
    c = lax.broadcasted_iota(I32, (n, n), 1) // HEAD_DIM
    return r == c


STATE_UNROLL = 17


def _states_kernel(n_ctx_chunks, n_chunks, rqkv_ref, win_ref, cd_ref, sp_ref, s_scr):
    same_head = _head_block_mask(RET_WIDTH)
    s_scr[...] = jnp.zeros_like(s_scr)

    def chunk_update(d, pos):
        if d == 0:
            c = pos
        else:
            c = jnp.where(pos < n_ctx_chunks, n_ctx_chunks - 1 - pos, n_chunks + n_ctx_chunks - 1 - pos)
        off = pl.multiple_of(c * CHUNK, CHUNK)
        kw = rqkv_ref[0, pl.ds(off, CHUNK), RET_WIDTH:2 * RET_WIDTH].astype(F32) * win_ref[d]
        v = rqkv_ref[0, pl.ds(off, CHUNK), 2 * RET_WIDTH:]
        kw_t = kw.T.astype(BF16)
        yield
        u = _dot(kw_t, v)
        yield
        return c, jnp.where(same_head, u, 0.0)

    def body(i, carry):
        todo = [(d, i * STATE_UNROLL + n) for n in range(STATE_UNROLL) for d in range(2)]
        done = _round_robin([chunk_update(d, pos) for d, pos in todo])
        for (d, _), (c, u) in zip(todo, done):
            s = s_scr[d]
            sp_ref[0, d, pl.ds(c, 1)] = s.astype(BF16)[None]
            s_scr[d] = s * cd_ref[d][0:1, :] + u
        return carry

    assert n_chunks % STATE_UNROLL == 0
    lax.fori_loop(0, n_chunks // STATE_UNROLL, body, 0)


def _ret_states(rqkv, win, cd, layer, n_ctx):
    b, tu, _ = rqkv.shape
    w = RET_WIDTH
    n_chunks = tu // CHUNK
    return pl.pallas_call(
        functools.partial(_states_kernel, n_ctx // CHUNK, n_chunks),
        grid=(b,),
        in_specs=[
            pl.BlockSpec((1, tu, 3 * w), lambda i: (i, 0, 0)),
            _layer_block(win, layer),
            _layer_block(cd, layer),
        ],
        out_specs=pl.BlockSpec((1, 2, n_chunks, w, w), lambda i: (i, 0, 0, 0, 0)),
        out_shape=jax.ShapeDtypeStruct((b, 2, n_chunks, w, w), BF16),
        scratch_shapes=[pltpu.VMEM((2, w, w), F32)],
        compiler_params=_params("parallel"),
        name="ret_states",
    )(rqkv, win, cd)


def _group_mean(t, ones_bd):
    hi = t.astype(BF16)
    lo = (t - hi.astype(F32)).astype(BF16)
    s = _dot(jnp.concatenate([hi, lo], axis=0), ones_bd)
    n = t.shape[0]
    return (s[:n] + s[n:]) * (1.0 / HEAD_DIM)


def _mix_chunk(c, n_ctx_chunks, n_chunks, cz, z_last, z_first, cw, rqkv, gates, sp, dm_ref, qdec_ref,
               qa, kvs, sink_ref):
    is_lat = c >= n_ctx_chunks

    z = cz[:, CONV_WIDTH:]
    row = lax.broadcasted_iota(I32, (CHUNK, 1), 0)
    has_prev = jnp.logical_and(c != 0, c != n_ctx_chunks)
    has_next = jnp.logical_and(c != n_ctx_chunks - 1, c != n_chunks - 1)
    z_before = jnp.where(row == 0, jnp.where(has_prev, z_last, 0.0), pltpu.roll(z, 1, 0))
    z_after = jnp.where(row == CHUNK - 1, jnp.where(has_next, z_first, 0.0), pltpu.roll(z, CHUNK - 1, 0))
    conv = cz[:, :CONV_WIDTH] * (z_before * cw[0:1, :] + z * cw[1:2, :] + z_after * cw[2:3, :])

    q = rqkv[:, 0:RET_WIDTH]
    k = rqkv[:, RET_WIDTH:2 * RET_WIDTH]
    v = rqkv[:, 2 * RET_WIDTH:]
    lane_head = lax.broadcasted_iota(I32, (1, RET_WIDTH), 1) // HEAD_DIM
    kz = jnp.zeros_like(k)
    k_heads = jnp.concatenate([jnp.where(lane_head == hh, k, kz) for hh in range(RET_HEADS)], axis=0)
    v_heads = jnp.concatenate([jnp.where(lane_head == hh, v, kz) for hh in range(RET_HEADS)], axis=0)
    scores = _dot_nt(q, k_heads)

    keys, v_t = kvs
    half = lax.broadcasted_iota(I32, (1, LANES), 1) // HEAD_DIM
    qa_z = jnp.zeros((CHUNK, LANES), BF16)
    combos = [(hk, par) for hk in range(ATTN_KV_HEADS) for par in range(2)]
    n_ctx = keys[0].shape[0] - 3 * CHUNK
    cols2 = 2 * CHUNK
    ik = lax.broadcasted_iota(I32, (CHUNK, cols2), 0)
    iq = lax.broadcasted_iota(I32, (CHUNK, cols2), 1) & (CHUNK - 1)
    off = jnp.full((CHUNK, cols2), NEG_INF, F32)
    bias_prev = jnp.where(jnp.logical_and(jnp.logical_and(is_lat, c - 1 >= n_ctx_chunks), ik >= iq), 0.0, off)
    bias_cur = jnp.where(is_lat, 0.0, off)
    bias_next = jnp.where(jnp.logical_and(jnp.logical_and(is_lat, c + 1 <= n_chunks - 1), ik <= iq), 0.0, off)
    assert n_ctx > 0
    logits = {}
    for hk, par in combos:
        ja, jb = 2 * hk, 2 * hk + 1
        qst = jnp.concatenate([
            jnp.where(half == par, qa[:, ja * LANES:(ja + 1) * LANES], qa_z),
            jnp.where(half == par, qa[:, jb * LANES:(jb + 1) * LANES], qa_z)], axis=0)
        sel = 0 if par == hk else 1
        r = _dot_nt(keys[sel], qst)
        s = jnp.concatenate([r[:CHUNK] + bias_prev, r[CHUNK:2 * CHUNK] + bias_cur,
                             r[2 * CHUNK:3 * CHUNK] + bias_next, r[3 * CHUNK:]], axis=0)
        ha, hb = ATTN_GROUP * hk + par, ATTN_GROUP * hk + par + 2
        snk = jnp.concatenate([sink_ref[ha * CHUNK:ha * CHUNK + 1, :],
                               sink_ref[hb * CHUNK:hb * CHUNK + 1, :]], axis=1) * LOG2E
        logits[hk, par] = (s, snk, jnp.maximum(jnp.max(s, axis=0, keepdims=True), snk))
    yield

    qf = q.astype(F32)
    outs = []
    for d in range(2):
        p = (scores * dm_ref[d]).astype(BF16)
        lhs = jnp.concatenate([p, (qf * qdec_ref[d]).astype(BF16)], axis=1)
        o = _dot(lhs, jnp.concatenate([v_heads, sp[d]], axis=0))
        outs.append(o)
    o2 = jnp.concatenate(outs, axis=0)
    yield

    att = {}
    ones_keys = jnp.ones((BF16_ROWS, keys[0].shape[0]), BF16)
    for n, (hk, par) in enumerate(combos):
        s, snk, mx = logits[hk, par]
        e = jnp.exp2(s - mx).astype(BF16)
        o_t = _dot(jnp.concatenate([v_t[hk * HEAD_DIM:(hk + 1) * HEAD_DIM, :], ones_keys], axis=0), e)
        den = o_t[HEAD_DIM:HEAD_DIM + 1, :] + jnp.exp2(snk - mx)
        att[hk, par] = o_t[:HEAD_DIM, :] * (1.0 / den)
        if n == 0:
            ones_bd = jnp.where(_head_block_mask(RET_WIDTH), 1.0, 0.0).astype(BF16)
            dl = o2 - _group_mean(o2, ones_bd)
            on = dl * lax.rsqrt(_group_mean(dl * dl, ones_bd) + EPS)
            ret = on[:CHUNK] * gates[:, 0:RET_WIDTH] + on[CHUNK:] * gates[:, RET_WIDTH:]
        yield

    cols = []
    for hk in range(ATTN_KV_HEADS):
        cols.append(jnp.concatenate([att[hk, 0][:, :CHUNK], att[hk, 1][:, :CHUNK]], axis=0).T)
        cols.append(jnp.concatenate([att[hk, 0][:, CHUNK:], att[hk, 1][:, CHUNK:]], axis=0).T)
    return jnp.concatenate([conv, ret] + cols, axis=1)


def _round_robin(gens):
    results = [None] * len(gens)
    active = list(range(len(gens)))
    while active:
        for i in list(active):
            try:
                next(gens[i])
            except StopIteration as done:
                results[i] = done.value
                active.remove(i)
    return results


MIX_CHUNKS = 2


def _mixer_kernel(n_ctx_chunks, n_chunks, n_exp,
                  cz_ref, czp_ref, czn_ref, cw_ref, rqkv_ref, gates_ref, sp_ref, dm_ref, qdec_ref,
                  aq_ref, kvp_ref, kvc_ref, kvn_ref, kvx_ref, sink_ref,
                  x_ref, mod_ref, g_ref, w_ref, wrh_ref, wrl_ref, xo_ref, hg_ref, aff_ref, mix_scr):
    t = pl.program_id(0)
    last = pl.num_programs(0) - 2
    tiles = n_chunks // MIX_CHUNKS

    @pl.when(t == 0)
    def _():
        mix_scr[...] = jnp.zeros_like(mix_scr)

    tail = _outproj_tail(n_ctx_chunks * CHUNK, n_exp, lax.rem(jnp.maximum(t - 1, 0), tiles), mix_scr[...], x_ref,
                         mod_ref, g_ref, w_ref, wrh_ref, wrl_ref, xo_ref, hg_ref, aff_ref)
    j = lax.rem(jnp.minimum(t, last), tiles)
    cw = cw_ref[...]

    def values_t(blk):
        return blk[:, 2 * KV_WIDTH:].astype(F32).T.astype(BF16)

    kv_blocks = ([kvp_ref[0]] + [kvc_ref[0, s * CHUNK:(s + 1) * CHUNK, :] for s in range(MIX_CHUNKS)] + [kvn_ref[0]])
    kv_ctx = [kvx_ref[0, t * CHUNK:(t + 1) * CHUNK, :] for t in range(kvx_ref.shape[1] // CHUNK)]
    vt_blocks = [values_t(blk) for blk in kv_blocks]
    vt_ctx = [values_t(blk) for blk in kv_ctx]
    gens = []
    for s in range(MIX_CHUNKS):
        lo, hi = s * CHUNK, (s + 1) * CHUNK
        if s == 0:
            z_last = czp_ref[0, SUBLANES - 1:SUBLANES, CONV_WIDTH:]
        else:
            z_last = cz_ref[0, lo - 1:lo, CONV_WIDTH:]
        if s == MIX_CHUNKS - 1:
            z_first = czn_ref[0, 0:1, CONV_WIDTH:]
        else:
            z_first = cz_ref[0, hi:hi + 1, CONV_WIDTH:]
        window = kv_blocks[s:s + 3] + kv_ctx
        keys = tuple(jnp.concatenate([blk[:, i * KV_WIDTH:(i + 1) * KV_WIDTH] for blk in window], axis=0)
                     for i in (0, 1))
        v_t = jnp.concatenate(vt_blocks[s:s + 3] + vt_ctx, axis=1)
        gens.append(_mix_chunk(j * MIX_CHUNKS + s, n_ctx_chunks, n_chunks, cz_ref[0, lo:hi, :], z_last, z_first,
                               cw, rqkv_ref[0, lo:hi, :], gates_ref[0, lo:hi, :],
                               (sp_ref[0, 0, s], sp_ref[0, 1, s]), dm_ref, qdec_ref, aq_ref[0, lo:hi, :],
                               (keys, v_t), sink_ref))
    mixes = [mix.astype(BF16) for mix in _round_robin([tail] + gens)[1:]]
    mix_scr[...] = jnp.concatenate(mixes, axis=0)


def _mixer(proj, sp, tabs, conv_w, sink_rows, x, mrows, norm_g, w_out_bf, layer, wr_hi, wr_lo, n_ctx):
    cz, rqkv, gates, aq, kv = proj
    dm, qdec = tabs
    b, tu, _ = cz.shape
    d = x.shape[2]
    n_exp = wr_hi.shape[1]
    n_chunks = tu // CHUNK
    mc = MIX_CHUNKS
    rows = mc * CHUNK
    assert n_chunks % mc == 0 and (n_ctx // CHUNK) % mc == 0
    last = n_chunks - 1
    tiles = n_chunks // mc
    per8 = rows // SUBLANES

    def mixed(t):
        m = jnp.minimum(t, b * tiles - 1)
        return m // tiles, lax.rem(m, tiles)

    def projected(t):
        p = jnp.maximum(t - 1, 0)
        return p // tiles, lax.rem(p, tiles)

    def at_mixed(block, tile_index):
        def index(t):
            i, j = mixed(t)
            return (i, tile_index(j), 0)
        return pl.BlockSpec(block, index)

    cur = lambda w: at_mixed((1, rows, w), lambda j: j)

    def out(r, w):
        return pl.BlockSpec((1, r, w), lambda t: projected(t) + (0,))

    wcz, wkv = cz.shape[2], kv.shape[2]
    in_specs = [
        cur(wcz),
        at_mixed((1, SUBLANES, wcz), lambda j: jnp.maximum(j * per8 - 1, 0)),
        at_mixed((1, SUBLANES, wcz), lambda j: jnp.minimum((j + 1) * per8, tu // SUBLANES - 1)),
        _layer_block(conv_w, layer),
        cur(rqkv.shape[2]), cur(gates.shape[2]),
        pl.BlockSpec((1, 2, mc, RET_WIDTH, RET_WIDTH), lambda t: (mixed(t)[0], 0, mixed(t)[1], 0, 0)),
        _layer_block(dm, layer), _layer_block(qdec, layer),
        cur(ATTN_WIDTH),
        at_mixed((1, CHUNK, wkv), lambda j: jnp.maximum(j * mc - 1, 0)),
        cur(wkv),
        at_mixed((1, CHUNK, wkv), lambda j: jnp.minimum((j + 1) * mc, last)),
        at_mixed((1, n_ctx, wkv), lambda j: 0),
        _layer_block(sink_rows, layer),
        out(rows, d),
        pl.BlockSpec((1, M_ROWS, d), lambda t: (projected(t)[0], 0, 0)),
        _layer_block(norm_g, layer),
        _layer_block(w_out_bf, layer),
        _layer_block(wr_hi, layer), _layer_block(wr_lo, layer),
    ]
    args = [cz, cz, cz, conv_w, rqkv, gates, sp, dm, qdec, aq, kv, kv, kv, kv, sink_rows,
            x, mrows, norm_g, w_out_bf, wr_hi, wr_lo]
    return pl.pallas_call(
        functools.partial(_mixer_kernel, n_ctx // CHUNK, n_chunks, n_exp),
        grid=(b * tiles + 1,),
        in_specs=in_specs,
        out_specs=[out(rows, d), out(rows * SUBLANES, LANES),
                   pl.BlockSpec((1, n_exp, rows), lambda t: (projected(t)[0], 0, projected(t)[1]))],
        out_shape=[
            jax.ShapeDtypeStruct((b, tu, d), F32),
            jax.ShapeDtypeStruct((b, tu * SUBLANES, LANES), F32),
            jax.ShapeDtypeStruct((b, n_exp, tu), F32),
        ],
        scratch_shapes=[pltpu.VMEM((rows, CONV_WIDTH + RET_WIDTH + ATTN_WIDTH), BF16)],
        compiler_params=_params("arbitrary"),
        name="mixer",
    )(*args)


def _outproj_tail(n_ctx, n_exp, tile, mix, x_ref, mod_ref, g_ref, w_ref, wrh_ref, wrl_ref, xo_ref, hg_ref, aff_ref):
    tm = x_ref.shape[1]
    row = _mod_rows(mod_ref, tm, n_ctx, tile)

    y = _dot(mix, w_ref[...])
    yield
    x = x_ref[0] + row(M_G1) * y
    xo_ref[0] = x
    h = (_rms(x) * g_ref[...]) * (1.0 + row(M_SC2)) + row(M_SH2)
    for q in range(SUBLANES):
        hg_ref[0, pl.ds(q, tm, stride=SUBLANES), :] = h[:, q * LANES:(q + 1) * LANES]
    yield
    h_hi = h.astype(BF16)
    h_lo = (h - h_hi.astype(F32)).astype(BF16)
    logits = (_dot_nt(wrh_ref[...], h_hi) + (_dot_nt(wrh_ref[...], h_lo) + _dot_nt(wrl_ref[...], h_hi)))
    yield
    e = jnp.exp(logits - jnp.max(logits, axis=0, keepdims=True))
    aff_ref[0] = e / jnp.sum(e, axis=0, keepdims=True)


MIN_NORMAL_F32_BITS = 0x00800000
ROUTE_TABLES = 4
ROUTE_GROUP = 4


def _route_stream_tables(v, k, tab_scr, ce_scr, cnt_scr, stream, thr):
    n_exp, n = v.shape
    lane = lax.broadcasted_iota(I32, (1, LANES), 1)
    tri_r = lax.broadcasted_iota(I32, (LANES, LANES), 0)
    tri_c = lax.broadcasted_iota(I32, (LANES, LANES), 1)
    upper = jnp.where(tri_r <= tri_c, 1.0, 0.0).astype(BF16)
    above = pltpu.bitcast(jnp.maximum(thr + 1, MIN_NORMAL_F32_BITS), F32)
    floor = pltpu.bitcast(thr, F32)
    n_gt = jnp.sum(jnp.where(v >= above, 1, 0), axis=1, keepdims=True)
    need = (k - n_gt).astype(F32)
    seen_eq = jnp.zeros((n_exp, 1), F32)
    counts = jnp.zeros((n_exp, LANES), F32)
    for j in range(n // LANES):
        vb = v[:, j * LANES:(j + 1) * LANES]
        gt = vb >= above
        eq = jnp.logical_and(vb >= floor, vb < above)
        both = jnp.concatenate([jnp.where(gt, 1.0, 0.0), jnp.where(eq, 1.0, 0.0)], axis=0).astype(BF16)
        pref = _dot(both, upper)
        rank_eq = pref[n_exp:] + seen_eq
        inc = pref[:n_exp] + jnp.minimum(rank_eq, need) - jnp.minimum(seen_eq, need)
        sel = jnp.logical_or(gt, jnp.logical_and(eq, rank_eq <= need))
        seen_eq = rank_eq[:, LANES - 1:LANES]
        counts = counts + jnp.where(lane == j, inc[:, LANES - 1:LANES], 0.0)
        a0 = jnp.where(sel, vb, 0.0)
        t0 = a0.astype(BF16).astype(F32)
        t1 = (a0 - t0).astype(BF16).astype(F32)
        t2 = a0 - t0 - t1
        for t, val in enumerate((inc, t0, t1, t2)):
            tab_scr[stream * ROUTE_TABLES + t, pl.ds(j, n_exp, stride=LANES), :] = val
    through = _dot(counts.astype(BF16), upper)
    for e in range(n_exp):
        ce_scr[stream, e] = through[e:e + 1, :]
        cnt_scr[stream, e] = counts[e:e + 1, :]


def _route_slots(e, stream, k, lo, tab_scr, ce_scr, cnt_scr):
    lane_f = lax.broadcasted_iota(I32, (1, LANES), 1).astype(F32)
    slot = lax.broadcasted_iota(I32, (k, LANES), 0).astype(F32)
    ones_rows = jnp.ones((LANES, LANES), BF16)
    cnt_rows = jnp.broadcast_to(cnt_scr[stream, e], (LANES, LANES)).astype(BF16)
    rows_e = pl.ds(pl.multiple_of(e * LANES, LANES), LANES)
    tab = jnp.concatenate([tab_scr[stream * ROUTE_TABLES + t, rows_e, :] for t in range(ROUTE_TABLES)],
                          axis=1).astype(BF16)
    before = jnp.where(ce_scr[stream, e] <= slot, 1.0, 0.0).astype(BF16)
    blk = _dot_nt(before, ones_rows)
    base = _dot_nt(before, cnt_rows)
    yield
    row = _dot(jnp.where(lane_f == blk, 1.0, 0.0).astype(BF16), tab)
    yield
    inc = row[:, :LANES]
    aff = row[:, LANES:2 * LANES] + row[:, 2 * LANES:3 * LANES] + row[:, 3 * LANES:]
    local = slot - base
    pos = _dot_nt(jnp.where(inc <= local, 1.0, 0.0).astype(BF16), ones_rows)
    yield
    tok = blk * LANES + pos + float(lo)
    gate = jnp.sum(jnp.where(inc == local + 1.0, aff, 0.0), axis=1, keepdims=True)
    return tok, gate


def _route_kernel(n_exp, streams, aff_ref, idx_ref, gate_ref, tab_scr, ce_scr, cnt_scr):
    a = aff_ref[0]
    tab_scr[...] = jnp.zeros_like(tab_scr)
    vs = [a[:, lo:lo + n] for lo, n, _, _ in streams]

    def enough(v, cand, k):
        return jnp.sum(jnp.where(v >= pltpu.bitcast(cand, F32), 1, 0), axis=1, keepdims=True) >= k

    def search(i, ts):
        out = []
        for t, v, (_, _, k, _) in zip(ts, vs, streams):
            hi = t | jnp.left_shift(jnp.int32(1), 30 - 2 * i)
            lo = jnp.left_shift(jnp.int32(1), 29 - 2 * i)
            out.append(jnp.where(enough(v, hi, k), jnp.where(enough(v, hi | lo, k), hi | lo, hi),
                                 jnp.where(enough(v, t | lo, k), t | lo, t)))
        return tuple(out)

    thrs = lax.fori_loop(0, 15, search, tuple(jnp.zeros((n_exp, 1), I32) for _ in streams))
    thrs = tuple(jnp.where(enough(v, t | 1, k), t | 1, t) for t, v, (_, _, k, _) in zip(thrs, vs, streams))
    for st, (v, (_, _, k, _)) in enumerate(zip(vs, streams)):
        _route_stream_tables(v, k, tab_scr, ce_scr, cnt_scr, st, thrs[st])

    slots = idx_ref.shape[3]

    assert [s[3] for s in streams] == [sum(s[2] for s in streams[:n]) for n in range(len(streams))]

    def expert_group(g, carry):
        experts = [g * ROUTE_GROUP + u for u in range(ROUTE_GROUP)]
        found = _round_robin([_route_slots(e, st, k, lo, tab_scr, ce_scr, cnt_scr)
                              for e in experts for st, (lo, _, k, _) in enumerate(streams)])
        for u, e in enumerate(experts):
            toks = []
            for st, (_, _, k, slot0) in enumerate(streams):
                tok, gate = found[u * len(streams) + st]
                toks.append(tok)
                gate_ref[0, pl.ds(e, 1), slot0:slot0 + k, :] = gate[None]
            if slots % LANES:
                toks.append(jnp.zeros((-slots % LANES, LANES), F32))
            row = jnp.concatenate(toks, axis=0).T[0:1, :slots]
            idx_ref[0, pl.ds(e, 1)] = row.astype(I32)[None]
        return carry

    assert n_exp % ROUTE_GROUP == 0
    lax.fori_loop(0, n_exp // ROUTE_GROUP, expert_group, 0)


def _route(aff, n_ctx, with_ctx):
    b, n_exp, tu = aff.shape
    n_lat = tu - n_ctx
    cap_l = CAPACITY_FACTOR * n_lat // n_exp
    cap_c = CAPACITY_FACTOR * n_ctx // n_exp
    assert n_lat // LANES <= LANES and n_ctx // LANES <= LANES
    streams = ((n_ctx, n_lat, cap_l, 0),) + (((0, n_ctx, cap_c, cap_l),) if with_ctx else ())
    slots = sum(s[2] for s in streams)
    return pl.pallas_call(
        functools.partial(_route_kernel, n_exp, streams),
        grid=(b,),
        in_specs=[pl.BlockSpec((1, n_exp, tu), lambda i: (i, 0, 0))],
        out_specs=[pl.BlockSpec((1, n_exp, 1, slots), lambda i: (i, 0, 0, 0)),
                   pl.BlockSpec((1, n_exp, slots, 1), lambda i: (i, 0, 0, 0))],
        out_shape=[jax.ShapeDtypeStruct((b, n_exp, 1, slots), I32),
                   jax.ShapeDtypeStruct((b, n_exp, slots, 1), F32)],
        scratch_shapes=[pltpu.VMEM((len(streams) * ROUTE_TABLES, n_exp * LANES, LANES), F32),
                        pltpu.VMEM((len(streams), n_exp, 1, LANES), F32),
                        pltpu.VMEM((len(streams), n_exp, 1, LANES), F32)],
        compiler_params=_params("parallel"),
        name="route",
    )(aff)


def _slot_pitch(slots):
    return slots + SUBLANES


def _zero_slot_padding(ref, lead, slots, pitch):
    for q in range(SUBLANES):
        ref[lead + (pl.ds(q * pitch + slots, pitch - slots), slice(None))] = jnp.zeros((pitch - slots, LANES), ref.dtype)


def _slot_pitch_bf16(slots):
    return -(-slots // BF16_ROWS) * BF16_ROWS + BF16_ROWS


DISPATCH_EXPERTS = 4


def _gather_kernel(slots, pitch, pitch_out, idx_ref, h_ref, o_ref, tile_scr):
    for g in range(DISPATCH_EXPERTS):
        for mi in range(slots):
            r = pl.multiple_of(idx_ref[g, 0, mi] * SUBLANES, SUBLANES)
            tile_scr[g, pl.ds(mi, SUBLANES, stride=pitch), :] = h_ref[0, pl.ds(r, SUBLANES), :]
        _zero_slot_padding(o_ref, (0, g), slots, pitch_out)
        for q in range(SUBLANES):
            o_ref[0, g, q * pitch_out:q * pitch_out + slots, :] = (
                tile_scr[g, q * pitch:q * pitch + slots, :].astype(BF16))


def _gather(hg, idx_rows, n_exp, slots):
    b = hg.shape[0]
    pitch = _slot_pitch(slots)
    pitch_out = _slot_pitch_bf16(slots)
    ge = DISPATCH_EXPERTS
    assert n_exp % ge == 0
    return pl.pallas_call(
        functools.partial(_gather_kernel, slots, pitch, pitch_out),
        grid=(b, n_exp // ge),
        in_specs=[
            pl.BlockSpec((ge, 1, slots), lambda i, e: (i * (n_exp // ge) + e, 0, 0), memory_space=pltpu.SMEM),
            pl.BlockSpec((1,) + hg.shape[1:], lambda i, e: (i, 0, 0)),
        ],
        out_specs=pl.BlockSpec((1, ge, SUBLANES * pitch_out, LANES), lambda i, e: (i, e, 0, 0)),
        out_shape=jax.ShapeDtypeStruct((b, n_exp, SUBLANES * pitch_out, LANES), BF16),
        scratch_shapes=[pltpu.VMEM((ge, SUBLANES * pitch, LANES), F32)],
        compiler_params=_params("parallel", "arbitrary"),
        name="gather",
    )(idx_rows, hg)


FFN_SAMPLES = 2
FFN_VMEM_LIMIT_BYTES = 60 * 1024 * 1024


def _ffn_kernel(slots, pitch_in, pitch, xs_ref, wg_ref, wu_ref, wd_ref, gate_ref, y_ref):
    n = xs_ref.shape[0]
    x = jnp.concatenate(
        [jnp.concatenate([xs_ref[s, 0, q * pitch_in:q * pitch_in + slots, :] for q in range(SUBLANES)], axis=1)
         for s in range(n)], axis=0)
    gate = jnp.concatenate([gate_ref[s, 0] for s in range(n)], axis=0)
    a = _dot(x, wg_ref[0].astype(BF16))
    u = _dot(x, wu_ref[0].astype(BF16))
    y = _dot((_silu(a) * u).astype(BF16), wd_ref[0].astype(BF16)) * gate
    for s in range(n):
        _zero_slot_padding(y_ref, (s, 0), slots, pitch)
        for q in range(SUBLANES):
            y_ref[s, 0, q * pitch:q * pitch + slots, :] = y[s * slots:(s + 1) * slots, q * LANES:(q + 1) * LANES]


def _ffn(xs, w_gate, w_up, w_down, layer, gate, slots):
    b, n_exp = xs.shape[:2]
    pitch = _slot_pitch(slots)
    pitch_in = _slot_pitch_bf16(slots)
    d, f = w_gate.shape[2:]
    ns = FFN_SAMPLES
    assert b % ns == 0
    slot_tile = lambda p: pl.BlockSpec((ns, 1, SUBLANES * p, LANES), lambda e, i: (i, e, 0, 0))
    weight = lambda rows, cols: pl.BlockSpec((None, 1, rows, cols), lambda e, i: (layer, e, 0, 0))
    return pl.pallas_call(
        functools.partial(_ffn_kernel, slots, pitch_in, pitch),
        grid=(n_exp, b // ns),
        in_specs=[
            slot_tile(pitch_in),
            weight(d, f), weight(d, f), weight(f, d),
            pl.BlockSpec((ns, 1, slots, 1), lambda e, i: (i, e, 0, 0)),
        ],
        out_specs=slot_tile(pitch),
        out_shape=jax.ShapeDtypeStruct((b, n_exp, SUBLANES * pitch, LANES), F32),
        compiler_params=_params("parallel", "arbitrary", vmem=FFN_VMEM_LIMIT_BYTES),
        name="ffn",
    )(xs, w_gate, w_up, w_down, gate)


SCATTER_BATCH = 16


def _scatter_kernel(slots, pitch, idx_ref, y_ref, o_ref):
    @pl.when(pl.program_id(1) == 0)
    def _():
        o_ref[...] = jnp.zeros_like(o_ref)

    for g in range(DISPATCH_EXPERTS):
        for m0 in range(0, slots, SCATTER_BATCH):
            rows = [pl.multiple_of(idx_ref[g, 0, m0 + u] * SUBLANES, SUBLANES) for u in range(SCATTER_BATCH)]
            vals = [o_ref[0, pl.ds(rows[u], SUBLANES), :] + y_ref[0, g, pl.ds(m0 + u, SUBLANES, stride=pitch), :]
                    for u in range(SCATTER_BATCH)]
            for u in range(SCATTER_BATCH):
                o_ref[0, pl.ds(rows[u], SUBLANES), :] = vals[u]


def _scatter(y, idx_rows, tu, slots):
    b, n_exp = y.shape[:2]
    pitch = _slot_pitch(slots)
    ge = DISPATCH_EXPERTS
    assert n_exp % ge == 0 and slots % SCATTER_BATCH == 0
    return pl.pallas_call(
        functools.partial(_scatter_kernel, slots, pitch),
        grid=(b, n_exp // ge),
        in_specs=[
            pl.BlockSpec((ge, 1, slots), lambda i, e: (i * (n_exp // ge) + e, 0, 0), memory_space=pltpu.SMEM),
            pl.BlockSpec((1, ge, SUBLANES * pitch, LANES), lambda i, e: (i, e, 0, 0)),
        ],
        out_specs=pl.BlockSpec((1, tu * SUBLANES, LANES), lambda i, e: (i, 0, 0)),
        out_shape=jax.ShapeDtypeStruct((b, tu * SUBLANES, LANES), F32),
        compiler_params=_params("parallel", "arbitrary"),
        name="scatter",
    )(idx_rows, y)


FINAL_PARTS = 4


def _final_kernel(mod_ref, g_ref, *refs):
    o_ref = refs[-1]
    tm = FINAL_TILE
    for n in range(FINAL_PARTS):
        x_ref, f_ref = refs[n], refs[FINAL_PARTS + n]
        x = x_ref[0] + mod_ref[0][M_G2:M_G2 + 1, :] * _from_gather_layout(f_ref, tm)
        o_ref[0, n * tm:(n + 1) * tm, :] = _rms(x) * g_ref[...]


def _final(x, ffn, mrows, final_g, n_ctx):
    b, tu, d = x.shape
    tm = FINAL_TILE
    skip = n_ctx // tm
    parts = FINAL_PARTS
    assert (tu - n_ctx) % (tm * parts) == 0
    tile = lambda n: pl.BlockSpec((1, tm, d), lambda i, j: (i, j * parts + n + skip, 0))
    ftile = lambda n: pl.BlockSpec((1, tm * SUBLANES, LANES), lambda i, j: (i, j * parts + n + skip, 0))
    return pl.pallas_call(
        _final_kernel,
        grid=(b, (tu - n_ctx) // (tm * parts)),
        in_specs=([pl.BlockSpec((1, M_ROWS, d), lambda i, j: (i, 0, 0)), pl.BlockSpec((1, d), lambda i, j: (0, 0))]
                  + [tile(n) for n in range(parts)] + [ftile(n) for n in range(parts)]),
        out_specs=pl.BlockSpec((1, tm * parts, d), lambda i, j: (i, j, 0)),
        out_shape=jax.ShapeDtypeStruct((b, tu - n_ctx, d), F32),
        compiler_params=_params("parallel", "parallel"),
        name="final_norm",
    )(mrows, final_g.reshape(1, d), *([x] * parts + [ffn] * parts))


def _rope_tables(n_lat, n_ctx):
    rows = n_lat // GRID_W
    rowp = jnp.repeat(jnp.arange(rows, dtype=F32), GRID_W)
    colp = jnp.tile(jnp.arange(GRID_W, dtype=F32), rows)
    axis_dim = HEAD_DIM // 2
    inv_freq = ROPE_BASE ** (-jnp.arange(0, axis_dim, 2, dtype=F32) / axis_dim)
    ar = rowp[:, None] * inv_freq
    ac = colp[:, None] * inv_freq
    cs = jnp.concatenate([jnp.cos(ar), jnp.cos(ar), jnp.cos(ac), jnp.cos(ac)], axis=1)
    sn = jnp.concatenate([-jnp.sin(ar), jnp.sin(ar), -jnp.sin(ac), jnp.sin(ac)], axis=1)
    reps = LANES // HEAD_DIM
    cs = jnp.concatenate([jnp.ones((n_ctx, LANES), F32), jnp.tile(cs, (1, reps))], axis=0)
    sn = jnp.concatenate([jnp.zeros((n_ctx, LANES), F32), jnp.tile(sn, (1, reps))], axis=0)
    return cs, sn


def kernel(x, c, ctx, c_ctx, w_mod, b_mod, norm1_g, norm2_g, w_in, conv_w, ret_decay_logit, attn_sink,
           w_out, w_router, w_gate, w_up, w_down, final_g):
    b, n_lat, d = x.shape
    n_ctx = ctx.shape[1]
    depth = w_in.shape[0]
    n_exp = w_router.shape[2]
    tu = n_ctx + n_lat
    assert w_in.shape[2] == IN_WIDTH and tu % INPROJ_TILE == 0
    assert n_ctx % FINAL_TILE == 0 and n_lat % FINAL_TILE == 0
    assert b + 1 <= SUBLANES and n_exp == N_EXPERTS

    c_rows = jnp.concatenate([c, c_ctx[None], jnp.zeros((SUBLANES - b - 1, d), F32)], axis=0)
    mods = _mod_vectors(c_rows, w_mod, b_mod).reshape(depth, SUBLANES, 6, d)
    qdec, win, cd, dm = _decay_tables(ret_decay_logit)
    rope_cs, rope_sn = _rope_tables(n_lat, n_ctx)
    sink_rows = jnp.broadcast_to(jnp.repeat(attn_sink.astype(F32), CHUNK, axis=1)[:, :, None],
                                 (depth, ATTN_HEADS * CHUNK, LANES))
    w_in_bf = w_in.astype(BF16)
    w_out_bf = w_out.astype(BF16)
    wr = jnp.swapaxes(w_router, 1, 2)
    wr_hi = wr.astype(BF16)
    wr_lo = (wr - wr_hi.astype(F32)).astype(BF16)
    g1 = norm1_g.reshape(depth, 1, d)
    g2 = norm2_g.reshape(depth, 1, d)
    mrows = jnp.concatenate([mods[:, :b], jnp.broadcast_to(mods[:, b][:, None], (depth, b, 6, d)),
                             jnp.zeros((depth, b, M_ROWS - 12, d), F32)], axis=2)
    mrows_in = mrows.at[1:, :, M_G2].set(mrows[:-1, :, M_G2])
    mrows_in = mrows_in.at[1:, :, M_CTX + M_G2].set(mrows[:-1, :, M_CTX + M_G2])

    xu = jnp.concatenate([ctx, x], axis=1)
    ffn = None
    for l in range(depth):
        xu, proj = _inproj(xu, ffn, mrows_in[l], g1, w_in_bf, l, rope_cs, rope_sn, n_ctx)
        sp = _ret_states(proj[1], win, cd, l, n_ctx)
        xu, hg, aff = _mixer(proj, sp, (dm, qdec), conv_w, sink_rows, xu, mrows[l], g2, w_out_bf, l,
                             wr_hi, wr_lo, n_ctx)
        idx, gate = _route(aff, n_ctx, with_ctx=l < depth - 1)
        cap = idx.shape[3]
        idx_rows = idx.reshape(b * n_exp, 1, cap)
        xs = _gather(hg, idx_rows, n_exp, cap)
        y = _ffn(xs, w_gate, w_up, w_down, l, gate, cap)
        ffn = _scatter(y, idx_rows, tu, cap)
    return _final(xu, ffn, mrows[depth - 1], final_g, n_ctx)
```

```python
import functools

import jax
import jax.numpy as jnp
from jax import lax
from jax.experimental import pallas as pl
from jax.experimental.pallas import tpu as pltpu

F32 = jnp.float32
BF16 = jnp.bfloat16
I32 = jnp.int32

HEAD_DIM = 64
CONV_WIDTH = 256
RET_HEADS = 4
RET_WIDTH = RET_HEADS * HEAD_DIM
ATTN_HEADS = 8
ATTN_KV_HEADS = 2
ATTN_GROUP = ATTN_HEADS // ATTN_KV_HEADS
ATTN_WIDTH = ATTN_HEADS * HEAD_DIM
KV_WIDTH = ATTN_KV_HEADS * HEAD_DIM
CHUNK = 128
GRID_W = 64
N_EXPERTS = 16
CAPACITY_FACTOR = 2
ROPE_BASE = 10000.0
EPS = 1e-6
NEG_INF = -1e30
QK_SCALE = HEAD_DIM ** -0.5
LOG2E = 1.4426950408889634

LANES = 128
SUBLANES = 8
BF16_ROWS = 2 * SUBLANES
VMEM_LIMIT_BYTES = 56 * 1024 * 1024

O_CB = 0
O_CC = O_CB + CONV_WIDTH
O_CX = O_CC + CONV_WIDTH
O_RQ = O_CX + CONV_WIDTH
O_RK = O_RQ + RET_WIDTH
O_RV = O_RK + RET_WIDTH
O_GF = O_RV + RET_WIDTH
O_GB = O_GF + RET_WIDTH
O_AQ = O_GB + RET_WIDTH
O_AK = O_AQ + ATTN_WIDTH
O_AV = O_AK + KV_WIDTH
IN_WIDTH = O_AV + KV_WIDTH

M_SH1, M_SC1, M_G1, M_SH2, M_SC2, M_G2 = range(6)
M_CTX = 6
M_ROWS = 16

INPROJ_TILE = 1088
FINAL_TILE = 256


def _params(*sem, vmem=VMEM_LIMIT_BYTES):
    return pltpu.CompilerParams(dimension_semantics=sem, vmem_limit_bytes=vmem)


def _dot(a, b):
    return jnp.dot(a, b, preferred_element_type=F32)


def _dot_nt(a, b):
    return lax.dot_general(a, b, (((1,), (1,)), ((), ())), preferred_element_type=F32)


def _silu(v):
    return v * jax.nn.sigmoid(v)


def _mod_kernel(c_ref, w_ref, b_ref, o_ref):
    s = _silu(c_ref[...])
    s_hi = s.astype(BF16)
    s_lo = (s - s_hi.astype(F32)).astype(BF16)
    w = w_ref[0]
    w_hi = w.astype(BF16)
    w_lo = (w - w_hi.astype(F32)).astype(BF16)
    n = s.shape[0]
    head = _dot(jnp.concatenate([s_hi, s_lo], axis=0), w_hi)
    o_ref[0] = head[:n] + (head[n:] + _dot(s_hi, w_lo)) + b_ref[0]


def _mod_vectors(c_rows, w_mod, b_mod):
    depth, d_model, width = w_mod.shape
    tn = 1536
    return pl.pallas_call(
        _mod_kernel,
        grid=(depth, width // tn),
        in_specs=[
            pl.BlockSpec((SUBLANES, d_model), lambda l, n: (0, 0)),
            pl.BlockSpec((1, d_model, tn), lambda l, n: (l, 0, n)),
            pl.BlockSpec((1, 1, tn), lambda l, n: (l, 0, n)),
        ],
        out_specs=pl.BlockSpec((1, SUBLANES, tn), lambda l, n: (l, 0, n)),
        out_shape=jax.ShapeDtypeStruct((depth, SUBLANES, width), F32),
        compiler_params=_params("parallel", "parallel"),
        name="mod_vectors",
    )(c_rows, w_mod, b_mod.reshape(depth, 1, width))


def _log_sigmoid(v):
    return -jnp.log(1.0 + jnp.exp(-v))


def _tables_kernel(lgl_ref, lgr_ref, qdec_ref, win_ref, cd_ref, dm_ref):
    pos = lax.broadcasted_iota(I32, (CHUNK, RET_WIDTH), 0).astype(F32)
    ri = lax.broadcasted_iota(I32, (CHUNK, RET_HEADS * CHUNK), 0).astype(F32)
    rj = (lax.broadcasted_iota(I32, (CHUNK, RET_HEADS * CHUNK), 1) & (CHUNK - 1)).astype(F32)
    for d in range(2):
        lg = _log_sigmoid(lgl_ref[0, d])
        lg1 = lg[0:1, :]
        if d == 0:
            qdec_ref[0, d] = jnp.exp(lg1 * (pos + 1.0))
            win_ref[0, d] = jnp.exp(lg1 * (CHUNK - 1.0 - pos))
            diff = ri - rj
        else:
            qdec_ref[0, d] = jnp.exp(lg1 * (CHUNK - pos))
            win_ref[0, d] = jnp.exp(lg1 * pos)
            diff = rj - ri
        cd_ref[0, d] = jnp.exp(lg * float(CHUNK))
        lr = _log_sigmoid(lgr_ref[0, d])
        dm_ref[0, d] = jnp.where(diff >= 0.0, jnp.exp(lr * jnp.maximum(diff, 0.0)), 0.0)


def _decay_tables(ret_decay_logit):
    depth = ret_decay_logit.shape[0]
    lg = ret_decay_logit.astype(F32)
    lgl = jnp.broadcast_to(jnp.repeat(lg, HEAD_DIM, axis=-1)[:, :, None, :], (depth, 2, SUBLANES, RET_WIDTH))
    lgr = jnp.broadcast_to(jnp.repeat(lg, CHUNK, axis=-1)[:, :, None, :], (depth, 2, CHUNK, RET_HEADS * CHUNK))
    return pl.pallas_call(
        _tables_kernel,
        grid=(depth,),
        in_specs=[
            pl.BlockSpec((1, 2, SUBLANES, RET_WIDTH), lambda l: (l, 0, 0, 0)),
            pl.BlockSpec((1, 2, CHUNK, RET_HEADS * CHUNK), lambda l: (l, 0, 0, 0)),
        ],
        out_specs=[
            pl.BlockSpec((1, 2, CHUNK, RET_WIDTH), lambda l: (l, 0, 0, 0)),
            pl.BlockSpec((1, 2, CHUNK, RET_WIDTH), lambda l: (l, 0, 0, 0)),
            pl.BlockSpec((1, 2, SUBLANES, RET_WIDTH), lambda l: (l, 0, 0, 0)),
            pl.BlockSpec((1, 2, CHUNK, RET_HEADS * CHUNK), lambda l: (l, 0, 0, 0)),
        ],
        out_shape=[
            jax.ShapeDtypeStruct((depth, 2, CHUNK, RET_WIDTH), F32),
            jax.ShapeDtypeStruct((depth, 2, CHUNK, RET_WIDTH), F32),
            jax.ShapeDtypeStruct((depth, 2, SUBLANES, RET_WIDTH), F32),
            jax.ShapeDtypeStruct((depth, 2, CHUNK, RET_HEADS * CHUNK), F32),
        ],
        compiler_params=_params("parallel"),
        name="decay_tables",
    )(lgl, lgr)


def _rope(v, cs, sn):
    lane = lax.broadcasted_iota(I32, (1, LANES), 1)
    first = (lane & 31) < 16
    outs = []
    for g in range(v.shape[1] // LANES):
        vg = v[:, g * LANES:(g + 1) * LANES]
        sw = jnp.where(first, pltpu.roll(vg, LANES - 16, 1), pltpu.roll(vg, 16, 1))
        outs.append(vg * cs + sw * sn)
    return outs[0] if len(outs) == 1 else jnp.concatenate(outs, axis=1)


def _swap_halves(v):
    return jnp.concatenate([v[:, HEAD_DIM:], v[:, :HEAD_DIM]], axis=1)


def _from_gather_layout(f_ref, tm):
    return jnp.concatenate(
        [f_ref[0, pl.ds(q, tm, stride=SUBLANES), :] for q in range(SUBLANES)], axis=1)


def _rms(v):
    return v * lax.rsqrt(jnp.mean(v * v, axis=-1, keepdims=True) + EPS)


def _mod_rows(mod_ref, tm, n_ctx, tile):
    m = mod_ref[0]
    if n_ctx % tm == 0:
        is_ctx = tile < n_ctx // tm
    else:
        is_ctx = tile * tm + lax.broadcasted_iota(I32, (tm, 1), 0) < n_ctx

    def row(r):
        return jnp.where(is_ctx, m[M_CTX + r:M_CTX + r + 1, :], m[r:r + 1, :])

    return row


def _inproj_kernel(has_ffn, n_ctx, *refs):
    if has_ffn:
        x_ref, f_ref, mod_ref, g_ref, w_ref, cs_ref, sn_ref = refs[:7]
        outs = refs[7:]
        xo_ref = outs[0]
        outs = outs[1:]
    else:
        x_ref, mod_ref, g_ref, w_ref, cs_ref, sn_ref = refs[:6]
        outs = refs[6:]
    cz_ref, rqkv_ref, gates_ref, aq_ref, kv_ref = outs
    tm = x_ref.shape[1]
    row = _mod_rows(mod_ref, tm, n_ctx, pl.program_id(1))

    x = x_ref[0]
    if has_ffn:
        x = x + row(M_G2) * _from_gather_layout(f_ref, tm)
        xo_ref[0] = x
    h = (_rms(x) * g_ref[...]) * (1.0 + row(M_SC1)) + row(M_SH1)
    h = h.astype(BF16)
    cs = cs_ref[...]
    sn = sn_ref[...]

    def proj(a, b):
        return _dot(h, w_ref[:, a:b])

    bounds = (O_CB, O_CC, O_CX, O_RQ, O_RK, O_RV, O_GF, O_GB, O_AQ, O_AK, O_AV, IN_WIDTH)
    p = [proj(a, b) for a, b in zip(bounds[:-1], bounds[1:])]
    cz_ref[0, :, 0:CONV_WIDTH] = p[0].astype(BF16)
    cz_ref[0, :, CONV_WIDTH:] = (p[1] * p[2]).astype(BF16)
    rqkv_ref[0, :, 0:RET_WIDTH] = _rope(p[3], cs, sn).astype(BF16)
    rqkv_ref[0, :, RET_WIDTH:2 * RET_WIDTH] = _rope(p[4] * QK_SCALE, cs, sn).astype(BF16)
    rqkv_ref[0, :, 2 * RET_WIDTH:] = p[5].astype(BF16)
    gates_ref[0, :, 0:RET_WIDTH] = _silu(p[6]).astype(BF16)
    gates_ref[0, :, RET_WIDTH:] = _silu(p[7]).astype(BF16)
    aq_ref[0] = (_rope(p[8], cs, sn) * (QK_SCALE * LOG2E)).astype(BF16)
    ak = _rope(p[9], cs, sn)
    av = p[10]
    kv_ref[0, :, 0:KV_WIDTH] = ak.astype(BF16)
    kv_ref[0, :, KV_WIDTH:2 * KV_WIDTH] = _swap_halves(ak).astype(BF16)
    kv_ref[0, :, 2 * KV_WIDTH:] = av.astype(BF16)


def _layer_block(stacked, layer):
    rest = stacked.shape[1:]
    return pl.BlockSpec((None,) + rest, lambda *_: (layer,) + (0,) * len(rest))


def _inproj(x, ffn, mrows, norm_g, w_in_bf, layer, rope_cs, rope_sn, n_ctx):
    b, tu, d = x.shape
    tm = INPROJ_TILE
    has_ffn = ffn is not None
    tok = lambda width: pl.BlockSpec((1, tm, width), lambda i, j: (i, j, 0))
    in_specs = [tok(d)]
    args = [x]
    if has_ffn:
        in_specs.append(pl.BlockSpec((1, tm * SUBLANES, LANES), lambda i, j: (i, j, 0)))
        args.append(ffn)
    in_specs += [
        pl.BlockSpec((1, M_ROWS, d), lambda i, j: (i, 0, 0)),
        _layer_block(norm_g, layer),
        pl.BlockSpec((None, d, IN_WIDTH), lambda i, j: (layer, 0, 0), pipeline_mode=pl.Buffered(1)),
        pl.BlockSpec((tm, LANES), lambda i, j: (j, 0)),
        pl.BlockSpec((tm, LANES), lambda i, j: (j, 0)),
    ]
    args += [mrows, norm_g, w_in_bf, rope_cs, rope_sn]
    widths = [(2 * CONV_WIDTH, BF16), (3 * RET_WIDTH, BF16), (2 * RET_WIDTH, BF16), (ATTN_WIDTH, BF16),
              (3 * KV_WIDTH, BF16)]
    out_specs = [tok(w) for w, _ in widths]
    out_shape = [jax.ShapeDtypeStruct((b, tu, w), dt) for w, dt in widths]
    if has_ffn:
        out_specs = [tok(d)] + out_specs
        out_shape = [jax.ShapeDtypeStruct((b, tu, d), F32)] + out_shape
    res = pl.pallas_call(
        functools.partial(_inproj_kernel, has_ffn, n_ctx),
        grid=(b, tu // tm),
        in_specs=in_specs,
        out_specs=out_specs,
        out_shape=out_shape,
        compiler_params=_params("parallel", "parallel"),
        name="inproj",
    )(*args)
    if has_ffn:
        return res[0], res[1:]
    return x, res


def _head_block_mask(n):
    r = lax.broadcasted_iota(I32, (n, n), 0) // HEAD_DIM
    c = lax.broadcasted_iota(I32, (n, n), 1) // HEAD_DIM
    return r == c


STATE_UNROLL = 17


def _states_kernel(n_ctx_chunks, n_chunks, rqkv_ref, win_ref, cd_ref, sp_ref, s_scr):
    same_head = _head_block_mask(RET_WIDTH)
    s_scr[...] = jnp.zeros_like(s_scr)

    def chunk_update(d, pos):
        if d == 0:
            c = pos
        else:
            c = jnp.where(pos < n_ctx_chunks, n_ctx_chunks - 1 - pos, n_chunks + n_ctx_chunks - 1 - pos)
        off = pl.multiple_of(c * CHUNK, CHUNK)
        kw = rqkv_ref[0, pl.ds(off, CHUNK), RET_WIDTH:2 * RET_WIDTH].astype(F32) * win_ref[d]
        v = rqkv_ref[0, pl.ds(off, CHUNK), 2 * RET_WIDTH:]
        kw_t = kw.T.astype(BF16)
        yield
        u = _dot(kw_t, v)
        yield
        return c, jnp.where(same_head, u, 0.0)

    def body(i, carry):
        todo = [(d, i * STATE_UNROLL + n) for n in range(STATE_UNROLL) for d in range(2)]
        done = _round_robin([chunk_update(d, pos) for d, pos in todo])
        for (d, _), (c, u) in zip(todo, done):
            s = s_scr[d]
            sp_ref[0, d, pl.ds(c, 1)] = s.astype(BF16)[None]
            s_scr[d] = s * cd_ref[d][0:1, :] + u
        return carry

    assert n_chunks % STATE_UNROLL == 0
    lax.fori_loop(0, n_chunks // STATE_UNROLL, body, 0)


def _ret_states(rqkv, win, cd, layer, n_ctx):
    b, tu, _ = rqkv.shape
    w = RET_WIDTH
    n_chunks = tu // CHUNK
    return pl.pallas_call(
        functools.partial(_states_kernel, n_ctx // CHUNK, n_chunks),
        grid=(b,),
        in_specs=[
            pl.BlockSpec((1, tu, 3 * w), lambda i: (i, 0, 0)),
            _layer_block(win, layer),
            _layer_block(cd, layer),
        ],
        out_specs=pl.BlockSpec((1, 2, n_chunks, w, w), lambda i: (i, 0, 0, 0, 0)),
        out_shape=jax.ShapeDtypeStruct((b, 2, n_chunks, w, w), BF16),
        scratch_shapes=[pltpu.VMEM((2, w, w), F32)],
        compiler_params=_params("parallel"),
        name="ret_states",
    )(rqkv, win, cd)


def _group_mean(t, ones_bd):
    hi = t.astype(BF16)
    lo = (t - hi.astype(F32)).astype(BF16)
    s = _dot(jnp.concatenate([hi, lo], axis=0), ones_bd)
    n = t.shape[0]
    return (s[:n] + s[n:]) * (1.0 / HEAD_DIM)


def _mix_chunk(c, n_ctx_chunks, n_chunks, cz, z_last, z_first, cw, rqkv, gates, sp, dm_ref, qdec_ref,
               qa, kvs, sink_ref):
    is_lat = c >= n_ctx_chunks

    cz = cz.astype(F32)
    z = cz[:, CONV_WIDTH:]
    row = lax.broadcasted_iota(I32, (CHUNK, 1), 0)
    has_prev = jnp.logical_and(c != 0, c != n_ctx_chunks)
    has_next = jnp.logical_and(c != n_ctx_chunks - 1, c != n_chunks - 1)
    z_before = jnp.where(row == 0, jnp.where(has_prev, z_last, 0.0), pltpu.roll(z, 1, 0))
    z_after = jnp.where(row == CHUNK - 1, jnp.where(has_next, z_first, 0.0), pltpu.roll(z, CHUNK - 1, 0))
    conv = cz[:, :CONV_WIDTH] * (z_before * cw[0:1, :] + z * cw[1:2, :] + z_after * cw[2:3, :])

    q = rqkv[:, 0:RET_WIDTH]
    k = rqkv[:, RET_WIDTH:2 * RET_WIDTH]
    v = rqkv[:, 2 * RET_WIDTH:]
    lane_head = lax.broadcasted_iota(I32, (1, RET_WIDTH), 1) // HEAD_DIM
    kz = jnp.zeros_like(k)
    k_heads = jnp.concatenate([jnp.where(lane_head == hh, k, kz) for hh in range(RET_HEADS)], axis=0)
    v_heads = jnp.concatenate([jnp.where(lane_head == hh, v, kz) for hh in range(RET_HEADS)], axis=0)
    scores = _dot_nt(q, k_heads)

    keys, v_t = kvs
    half = lax.broadcasted_iota(I32, (1, LANES), 1) // HEAD_DIM
    qa_z = jnp.zeros((CHUNK, LANES), BF16)
    combos = [(hk, par) for hk in range(ATTN_KV_HEADS) for par in range(2)]
    n_ctx = keys[0].shape[0] - 3 * CHUNK
    cols2 = 2 * CHUNK
    ik = lax.broadcasted_iota(I32, (CHUNK, cols2), 0)
    iq = lax.broadcasted_iota(I32, (CHUNK, cols2), 1) & (CHUNK - 1)
    off = jnp.full((CHUNK, cols2), NEG_INF, F32)
    bias_prev = jnp.where(jnp.logical_and(jnp.logical_and(is_lat, c - 1 >= n_ctx_chunks), ik >= iq), 0.0, off)
    bias_cur = jnp.where(is_lat, 0.0, off)
    bias_next = jnp.where(jnp.logical_and(jnp.logical_and(is_lat, c + 1 <= n_chunks - 1), ik <= iq), 0.0, off)
    assert n_ctx > 0
    logits = {}
    for hk, par in combos:
        ja, jb = 2 * hk, 2 * hk + 1
        qst = jnp.concatenate([
            jnp.where(half == par, qa[:, ja * LANES:(ja + 1) * LANES], qa_z),
            jnp.where(half == par, qa[:, jb * LANES:(jb + 1) * LANES], qa_z)], axis=0)
        sel = 0 if par == hk else 1
        r = _dot_nt(keys[sel], qst)
        s = jnp.concatenate([r[:CHUNK] + bias_prev, r[CHUNK:2 * CHUNK] + bias_cur,
                             r[2 * CHUNK:3 * CHUNK] + bias_next, r[3 * CHUNK:]], axis=0)
        ha, hb = ATTN_GROUP * hk + par, ATTN_GROUP * hk + par + 2
        snk = jnp.concatenate([sink_ref[ha * CHUNK:ha * CHUNK + 1, :],
                               sink_ref[hb * CHUNK:hb * CHUNK + 1, :]], axis=1) * LOG2E
        logits[hk, par] = (s, snk, jnp.maximum(jnp.max(s, axis=0, keepdims=True), snk))
    yield

    qf = q.astype(F32)
    outs = []
    for d in range(2):
        p = (scores * dm_ref[d]).astype(BF16)
        lhs = jnp.concatenate([p, (qf * qdec_ref[d]).astype(BF16)], axis=1)
        o = _dot(lhs, jnp.concatenate([v_heads, sp[d]], axis=0))
        outs.append(o)
    o2 = jnp.concatenate(outs, axis=0)
    yield

    att = {}
    ones_keys = jnp.ones((BF16_ROWS, keys[0].shape[0]), BF16)
    for n, (hk, par) in enumerate(combos):
        s, snk, mx = logits[hk, par]
        e = jnp.exp2(s - mx).astype(BF16)
        o_t = _dot(jnp.concatenate([v_t[hk * HEAD_DIM:(hk + 1) * HEAD_DIM, :], ones_keys], axis=0), e)
        den = o_t[HEAD_DIM:HEAD_DIM + 1, :] + jnp.exp2(snk - mx)
        att[hk, par] = o_t[:HEAD_DIM, :] * (1.0 / den)
        if n == 0:
            ones_bd = jnp.where(_head_block_mask(RET_WIDTH), 1.0, 0.0).astype(BF16)
            dl = o2 - _group_mean(o2, ones_bd)
            on = dl * lax.rsqrt(_group_mean(dl * dl, ones_bd) + EPS)
            gts = gates.astype(F32)
            ret = on[:CHUNK] * gts[:, 0:RET_WIDTH] + on[CHUNK:] * gts[:, RET_WIDTH:]
        yield

    cols = []
    for hk in range(ATTN_KV_HEADS):
        cols.append(jnp.concatenate([att[hk, 0][:, :CHUNK], att[hk, 1][:, :CHUNK]], axis=0).T)
        cols.append(jnp.concatenate([att[hk, 0][:, CHUNK:], att[hk, 1][:, CHUNK:]], axis=0).T)
    return jnp.concatenate([conv, ret] + cols, axis=1)


def _round_robin(gens):
    results = [None] * len(gens)
    active = list(range(len(gens)))
    while active:
        for i in list(active):
            try:
                next(gens[i])
            except StopIteration as done:
                results[i] = done.value
                active.remove(i)
    return results


MIX_CHUNKS = 2


def _mixer_kernel(n_ctx_chunks, n_chunks, n_exp,
                  cz_ref, czp_ref, czn_ref, cw_ref, rqkv_ref, gates_ref, sp_ref, dm_ref, qdec_ref,
                  aq_ref, kvp_ref, kvc_ref, kvn_ref, kvx_ref, sink_ref,
                  x_ref, mod_ref, g_ref, w_ref, wrh_ref, wrl_ref, xo_ref, hg_ref, aff_ref, mix_scr):
    t = pl.program_id(0)
    last = pl.num_programs(0) - 2
    tiles = n_chunks // MIX_CHUNKS

    @pl.when(t == 0)
    def _():
        mix_scr[...] = jnp.zeros_like(mix_scr)

    tail = _outproj_tail(n_ctx_chunks * CHUNK, n_exp, lax.rem(jnp.maximum(t - 1, 0), tiles), mix_scr[...], x_ref,
                         mod_ref, g_ref, w_ref, wrh_ref, wrl_ref, xo_ref, hg_ref, aff_ref)
    j = lax.rem(jnp.minimum(t, last), tiles)
    cw = cw_ref[...]

    def values_t(blk):
        return blk[:, 2 * KV_WIDTH:].astype(F32).T.astype(BF16)

    kv_blocks = ([kvp_ref[0]] + [kvc_ref[0, s * CHUNK:(s + 1) * CHUNK, :] for s in range(MIX_CHUNKS)] + [kvn_ref[0]])
    kv_ctx = [kvx_ref[0, t * CHUNK:(t + 1) * CHUNK, :] for t in range(kvx_ref.shape[1] // CHUNK)]
    vt_blocks = [values_t(blk) for blk in kv_blocks]
    vt_ctx = [values_t(blk) for blk in kv_ctx]
    gens = []
    for s in range(MIX_CHUNKS):
        lo, hi = s * CHUNK, (s + 1) * CHUNK
        if s == 0:
            before = czp_ref[0, :, CONV_WIDTH:]
        else:
            before = cz_ref[0, lo - BF16_ROWS:lo, CONV_WIDTH:]
        if s == MIX_CHUNKS - 1:
            after = czn_ref[0, :, CONV_WIDTH:]
        else:
            after = cz_ref[0, hi:hi + BF16_ROWS, CONV_WIDTH:]
        z_last = before.astype(F32)[BF16_ROWS - 1:BF16_ROWS, :]
        z_first = after.astype(F32)[0:1, :]
        window = kv_blocks[s:s + 3] + kv_ctx
        keys = tuple(jnp.concatenate([blk[:, i * KV_WIDTH:(i + 1) * KV_WIDTH] for blk in window], axis=0)
                     for i in (0, 1))
        v_t = jnp.concatenate(vt_blocks[s:s + 3] + vt_ctx, axis=1)
        gens.append(_mix_chunk(j * MIX_CHUNKS + s, n_ctx_chunks, n_chunks, cz_ref[0, lo:hi, :], z_last, z_first,
                               cw, rqkv_ref[0, lo:hi, :], gates_ref[0, lo:hi, :],
                               (sp_ref[0, 0, s], sp_ref[0, 1, s]), dm_ref, qdec_ref, aq_ref[0, lo:hi, :],
                               (keys, v_t), sink_ref))
    mixes = [mix.astype(BF16) for mix in _round_robin([tail] + gens)[1:]]
    mix_scr[...] = jnp.concatenate(mixes, axis=0)


def _mixer(proj, sp, tabs, conv_w, sink_rows, x, mrows, norm_g, w_out_bf, layer, wr_hi, wr_lo, n_ctx):
    cz, rqkv, gates, aq, kv = proj
    dm, qdec = tabs
    b, tu, _ = cz.shape
    d = x.shape[2]
    n_exp = wr_hi.shape[1]
    n_chunks = tu // CHUNK
    mc = MIX_CHUNKS
    rows = mc * CHUNK
    assert n_chunks % mc == 0 and (n_ctx // CHUNK) % mc == 0
    last = n_chunks - 1
    tiles = n_chunks // mc
    halo = rows // BF16_ROWS

    def mixed(t):
        m = jnp.minimum(t, b * tiles - 1)
        return m // tiles, lax.rem(m, tiles)

    def projected(t):
        p = jnp.maximum(t - 1, 0)
        return p // tiles, lax.rem(p, tiles)

    def at_mixed(block, tile_index):
        def index(t):
            i, j = mixed(t)
            return (i, tile_index(j), 0)
        return pl.BlockSpec(block, index)

    cur = lambda w: at_mixed((1, rows, w), lambda j: j)

    def out(r, w):
        return pl.BlockSpec((1, r, w), lambda t: projected(t) + (0,))

    wcz, wkv = cz.shape[2], kv.shape[2]
    in_specs = [
        cur(wcz),
        at_mixed((1, BF16_ROWS, wcz), lambda j: jnp.maximum(j * halo - 1, 0)),
        at_mixed((1, BF16_ROWS, wcz), lambda j: jnp.minimum((j + 1) * halo, tu // BF16_ROWS - 1)),
        _layer_block(conv_w, layer),
        cur(rqkv.shape[2]), cur(gates.shape[2]),
        pl.BlockSpec((1, 2, mc, RET_WIDTH, RET_WIDTH), lambda t: (mixed(t)[0], 0, mixed(t)[1], 0, 0)),
        _layer_block(dm, layer), _layer_block(qdec, layer),
        cur(ATTN_WIDTH),
        at_mixed((1, CHUNK, wkv), lambda j: jnp.maximum(j * mc - 1, 0)),
        cur(wkv),
        at_mixed((1, CHUNK, wkv), lambda j: jnp.minimum((j + 1) * mc, last)),
        at_mixed((1, n_ctx, wkv), lambda j: 0),
        _layer_block(sink_rows, layer),
        out(rows, d),
        pl.BlockSpec((1, M_ROWS, d), lambda t: (projected(t)[0], 0, 0)),
        _layer_block(norm_g, layer),
        _layer_block(w_out_bf, layer),
        _layer_block(wr_hi, layer), _layer_block(wr_lo, layer),
    ]
    args = [cz, cz, cz, conv_w, rqkv, gates, sp, dm, qdec, aq, kv, kv, kv, kv, sink_rows,
            x, mrows, norm_g, w_out_bf, wr_hi, wr_lo]
    return pl.pallas_call(
        functools.partial(_mixer_kernel, n_ctx // CHUNK, n_chunks, n_exp),
        grid=(b * tiles + 1,),
        in_specs=in_specs,
        out_specs=[out(rows, d), out(rows * SUBLANES, LANES),
                   pl.BlockSpec((1, n_exp, rows), lambda t: (projected(t)[0], 0, projected(t)[1]))],
        out_shape=[
            jax.ShapeDtypeStruct((b, tu, d), F32),
            jax.ShapeDtypeStruct((b, tu * SUBLANES, LANES), F32),
            jax.ShapeDtypeStruct((b, n_exp, tu), F32),
        ],
        scratch_shapes=[pltpu.VMEM((rows, CONV_WIDTH + RET_WIDTH + ATTN_WIDTH), BF16)],
        compiler_params=_params("arbitrary"),
        name="mixer",
    )(*args)


def _outproj_tail(n_ctx, n_exp, tile, mix, x_ref, mod_ref, g_ref, w_ref, wrh_ref, wrl_ref, xo_ref, hg_ref, aff_ref):
    tm = x_ref.shape[1]
    row = _mod_rows(mod_ref, tm, n_ctx, tile)

    y = _dot(mix, w_ref[...])
    yield
    x = x_ref[0] + row(M_G1) * y
    xo_ref[0] = x
    h = (_rms(x) * g_ref[...]) * (1.0 + row(M_SC2)) + row(M_SH2)
    for q in range(SUBLANES):
        hg_ref[0, pl.ds(q, tm, stride=SUBLANES), :] = h[:, q * LANES:(q + 1) * LANES]
    yield
    h_hi = h.astype(BF16)
    h_lo = (h - h_hi.astype(F32)).astype(BF16)
    logits = (_dot_nt(wrh_ref[...], h_hi) + (_dot_nt(wrh_ref[...], h_lo) + _dot_nt(wrl_ref[...], h_hi)))
    yield
    e = jnp.exp(logits - jnp.max(logits, axis=0, keepdims=True))
    aff_ref[0] = e / jnp.sum(e, axis=0, keepdims=True)


MIN_NORMAL_F32_BITS = 0x00800000
ROUTE_TABLES = 4
ROUTE_GROUP = 4


def _route_stream_tables(v, k, tab_scr, ce_scr, cnt_scr, stream, thr):
    n_exp, n = v.shape
    lane = lax.broadcasted_iota(I32, (1, LANES), 1)
    tri_r = lax.broadcasted_iota(I32, (LANES, LANES), 0)
    tri_c = lax.broadcasted_iota(I32, (LANES, LANES), 1)
    upper = jnp.where(tri_r <= tri_c, 1.0, 0.0).astype(BF16)
    above = pltpu.bitcast(jnp.maximum(thr + 1, MIN_NORMAL_F32_BITS), F32)
    floor = pltpu.bitcast(thr, F32)
    n_gt = jnp.sum(jnp.where(v >= above, 1, 0), axis=1, keepdims=True)
    need = (k - n_gt).astype(F32)
    seen_eq = jnp.zeros((n_exp, 1), F32)
    counts = jnp.zeros((n_exp, LANES), F32)
    for j in range(n // LANES):
        vb = v[:, j * LANES:(j + 1) * LANES]
        gt = vb >= above
        eq = jnp.logical_and(vb >= floor, vb < above)
        both = jnp.concatenate([jnp.where(gt, 1.0, 0.0), jnp.where(eq, 1.0, 0.0)], axis=0).astype(BF16)
        pref = _dot(both, upper)
        rank_eq = pref[n_exp:] + seen_eq
        inc = pref[:n_exp] + jnp.minimum(rank_eq, need) - jnp.minimum(seen_eq, need)
        sel = jnp.logical_or(gt, jnp.logical_and(eq, rank_eq <= need))
        seen_eq = rank_eq[:, LANES - 1:LANES]
        counts = counts + jnp.where(lane == j, inc[:, LANES - 1:LANES], 0.0)
        a0 = jnp.where(sel, vb, 0.0)
        t0 = a0.astype(BF16).astype(F32)
        t1 = (a0 - t0).astype(BF16).astype(F32)
        t2 = a0 - t0 - t1
        for t, val in enumerate((inc, t0, t1, t2)):
            tab_scr[stream * ROUTE_TABLES + t, pl.ds(j, n_exp, stride=LANES), :] = val
    through = _dot(counts.astype(BF16), upper)
    for e in range(n_exp):
        ce_scr[stream, e] = through[e:e + 1, :]
        cnt_scr[stream, e] = counts[e:e + 1, :]


def _route_slots(e, stream, k, lo, tab_scr, ce_scr, cnt_scr):
    lane_f = lax.broadcasted_iota(I32, (1, LANES), 1).astype(F32)
    slot = lax.broadcasted_iota(I32, (k, LANES), 0).astype(F32)
    ones_rows = jnp.ones((LANES, LANES), BF16)
    cnt_rows = jnp.broadcast_to(cnt_scr[stream, e], (LANES, LANES)).astype(BF16)
    rows_e = pl.ds(pl.multiple_of(e * LANES, LANES), LANES)
    tab = jnp.concatenate([tab_scr[stream * ROUTE_TABLES + t, rows_e, :] for t in range(ROUTE_TABLES)],
                          axis=1).astype(BF16)
    before = jnp.where(ce_scr[stream, e] <= slot, 1.0, 0.0).astype(BF16)
    blk = _dot_nt(before, ones_rows)
    base = _dot_nt(before, cnt_rows)
    yield
    row = _dot(jnp.where(lane_f == blk, 1.0, 0.0).astype(BF16), tab)
    yield
    inc = row[:, :LANES]
    aff = row[:, LANES:2 * LANES] + row[:, 2 * LANES:3 * LANES] + row[:, 3 * LANES:]
    local = slot - base
    pos = _dot_nt(jnp.where(inc <= local, 1.0, 0.0).astype(BF16), ones_rows)
    yield
    tok = blk * LANES + pos + float(lo)
    gate = jnp.sum(jnp.where(inc == local + 1.0, aff, 0.0), axis=1, keepdims=True)
    return tok, gate


def _route_kernel(n_exp, streams, aff_ref, idx_ref, gate_ref, tab_scr, ce_scr, cnt_scr):
    a = aff_ref[0]
    tab_scr[...] = jnp.zeros_like(tab_scr)
    vs = [a[:, lo:lo + n] for lo, n, _, _ in streams]

    def enough(v, cand, k):
        return jnp.sum(jnp.where(v >= pltpu.bitcast(cand, F32), 1, 0), axis=1, keepdims=True) >= k

    def search(i, ts):
        out = []
        for t, v, (_, _, k, _) in zip(ts, vs, streams):
            hi = t | jnp.left_shift(jnp.int32(1), 30 - 2 * i)
            lo = jnp.left_shift(jnp.int32(1), 29 - 2 * i)
            out.append(jnp.where(enough(v, hi, k), jnp.where(enough(v, hi | lo, k), hi | lo, hi),
                                 jnp.where(enough(v, t | lo, k), t | lo, t)))
        return tuple(out)

    thrs = lax.fori_loop(0, 15, search, tuple(jnp.zeros((n_exp, 1), I32) for _ in streams))
    thrs = tuple(jnp.where(enough(v, t | 1, k), t | 1, t) for t, v, (_, _, k, _) in zip(thrs, vs, streams))
    for st, (v, (_, _, k, _)) in enumerate(zip(vs, streams)):
        _route_stream_tables(v, k, tab_scr, ce_scr, cnt_scr, st, thrs[st])

    slots = idx_ref.shape[3]

    assert [s[3] for s in streams] == [sum(s[2] for s in streams[:n]) for n in range(len(streams))]

    def expert_group(g, carry):
        experts = [g * ROUTE_GROUP + u for u in range(ROUTE_GROUP)]
        found = _round_robin([_route_slots(e, st, k, lo, tab_scr, ce_scr, cnt_scr)
                              for e in experts for st, (lo, _, k, _) in enumerate(streams)])
        for u, e in enumerate(experts):
            toks = []
            for st, (_, _, k, slot0) in enumerate(streams):
                tok, gate = found[u * len(streams) + st]
                toks.append(tok)
                gate_ref[0, pl.ds(e, 1), slot0:slot0 + k, :] = gate[None]
            if slots % LANES:
                toks.append(jnp.zeros((-slots % LANES, LANES), F32))
            row = jnp.concatenate(toks, axis=0).T[0:1, :slots]
            idx_ref[0, pl.ds(e, 1)] = row.astype(I32)[None]
        return carry

    assert n_exp % ROUTE_GROUP == 0
    lax.fori_loop(0, n_exp // ROUTE_GROUP, expert_group, 0)


def _route(aff, n_ctx, with_ctx):
    b, n_exp, tu = aff.shape
    n_lat = tu - n_ctx
    cap_l = CAPACITY_FACTOR * n_lat // n_exp
    cap_c = CAPACITY_FACTOR * n_ctx // n_exp
    assert n_lat // LANES <= LANES and n_ctx // LANES <= LANES
    streams = ((n_ctx, n_lat, cap_l, 0),) + (((0, n_ctx, cap_c, cap_l),) if with_ctx else ())
    slots = sum(s[2] for s in streams)
    return pl.pallas_call(
        functools.partial(_route_kernel, n_exp, streams),
        grid=(b,),
        in_specs=[pl.BlockSpec((1, n_exp, tu), lambda i: (i, 0, 0))],
        out_specs=[pl.BlockSpec((1, n_exp, 1, slots), lambda i: (i, 0, 0, 0)),
                   pl.BlockSpec((1, n_exp, slots, 1), lambda i: (i, 0, 0, 0))],
        out_shape=[jax.ShapeDtypeStruct((b, n_exp, 1, slots), I32),
                   jax.ShapeDtypeStruct((b, n_exp, slots, 1), F32)],
        scratch_shapes=[pltpu.VMEM((len(streams) * ROUTE_TABLES, n_exp * LANES, LANES), F32),
                        pltpu.VMEM((len(streams), n_exp, 1, LANES), F32),
                        pltpu.VMEM((len(streams), n_exp, 1, LANES), F32)],
        compiler_params=_params("parallel"),
        name="route",
    )(aff)


def _slot_pitch(slots):
    return slots + SUBLANES


def _zero_slot_padding(ref, lead, slots, pitch):
    for q in range(SUBLANES):
        ref[lead + (pl.ds(q * pitch + slots, pitch - slots), slice(None))] = jnp.zeros((pitch - slots, LANES), ref.dtype)


def _slot_pitch_bf16(slots):
    return -(-slots // BF16_ROWS) * BF16_ROWS + BF16_ROWS


DISPATCH_EXPERTS = 4


def _gather_kernel(slots, pitch, pitch_out, idx_ref, h_ref, o_ref, tile_scr):
    for g in range(DISPATCH_EXPERTS):
        for mi in range(slots):
            r = pl.multiple_of(idx_ref[g, 0, mi] * SUBLANES, SUBLANES)
            tile_scr[g, pl.ds(mi, SUBLANES, stride=pitch), :] = h_ref[0, pl.ds(r, SUBLANES), :]
        _zero_slot_padding(o_ref, (0, g), slots, pitch_out)
        for q in range(SUBLANES):
            o_ref[0, g, q * pitch_out:q * pitch_out + slots, :] = (
                tile_scr[g, q * pitch:q * pitch + slots, :].astype(BF16))


def _gather(hg, idx_rows, n_exp, slots):
    b = hg.shape[0]
    pitch = _slot_pitch(slots)
    pitch_out = _slot_pitch_bf16(slots)
    ge = DISPATCH_EXPERTS
    assert n_exp % ge == 0
    return pl.pallas_call(
        functools.partial(_gather_kernel, slots, pitch, pitch_out),
        grid=(b, n_exp // ge),
        in_specs=[
            pl.BlockSpec((ge, 1, slots), lambda i, e: (i * (n_exp // ge) + e, 0, 0), memory_space=pltpu.SMEM),
            pl.BlockSpec((1,) + hg.shape[1:], lambda i, e: (i, 0, 0)),
        ],
        out_specs=pl.BlockSpec((1, ge, SUBLANES * pitch_out, LANES), lambda i, e: (i, e, 0, 0)),
        out_shape=jax.ShapeDtypeStruct((b, n_exp, SUBLANES * pitch_out, LANES), BF16),
        scratch_shapes=[pltpu.VMEM((ge, SUBLANES * pitch, LANES), F32)],
        compiler_params=_params("parallel", "arbitrary"),
        name="gather",
    )(idx_rows, hg)


FFN_SAMPLES = 2
FFN_VMEM_LIMIT_BYTES = 60 * 1024 * 1024


def _ffn_kernel(slots, pitch_in, pitch, xs_ref, wg_ref, wu_ref, wd_ref, gate_ref, y_ref):
    n = xs_ref.shape[0]
    x = jnp.concatenate(
        [jnp.concatenate([xs_ref[s, 0, q * pitch_in:q * pitch_in + slots, :] for q in range(SUBLANES)], axis=1)
         for s in range(n)], axis=0)
    gate = jnp.concatenate([gate_ref[s, 0] for s in range(n)], axis=0)
    a = _dot(x, wg_ref[0].astype(BF16))
    u = _dot(x, wu_ref[0].astype(BF16))
    y = _dot((_silu(a) * u).astype(BF16), wd_ref[0].astype(BF16)) * gate
    for s in range(n):
        _zero_slot_padding(y_ref, (s, 0), slots, pitch)
        for q in range(SUBLANES):
            y_ref[s, 0, q * pitch:q * pitch + slots, :] = y[s * slots:(s + 1) * slots, q * LANES:(q + 1) * LANES]


def _ffn(xs, w_gate, w_up, w_down, layer, gate, slots):
    b, n_exp = xs.shape[:2]
    pitch = _slot_pitch(slots)
    pitch_in = _slot_pitch_bf16(slots)
    d, f = w_gate.shape[2:]
    ns = FFN_SAMPLES
    assert b % ns == 0
    slot_tile = lambda p: pl.BlockSpec((ns, 1, SUBLANES * p, LANES), lambda e, i: (i, e, 0, 0))
    weight = lambda rows, cols: pl.BlockSpec((None, 1, rows, cols), lambda e, i: (layer, e, 0, 0))
    return pl.pallas_call(
        functools.partial(_ffn_kernel, slots, pitch_in, pitch),
        grid=(n_exp, b // ns),
        in_specs=[
            slot_tile(pitch_in),
            weight(d, f), weight(d, f), weight(f, d),
            pl.BlockSpec((ns, 1, slots, 1), lambda e, i: (i, e, 0, 0)),
        ],
        out_specs=slot_tile(pitch),
        out_shape=jax.ShapeDtypeStruct((b, n_exp, SUBLANES * pitch, LANES), F32),
        compiler_params=_params("parallel", "arbitrary", vmem=FFN_VMEM_LIMIT_BYTES),
        name="ffn",
    )(xs, w_gate, w_up, w_down, gate)


SCATTER_BATCH = 16


def _scatter_kernel(slots, pitch, idx_ref, y_ref, o_ref):
    @pl.when(pl.program_id(1) == 0)
    def _():
        o_ref[...] = jnp.zeros_like(o_ref)

    for g in range(DISPATCH_EXPERTS):
        for m0 in range(0, slots, SCATTER_BATCH):
            rows = [pl.multiple_of(idx_ref[g, 0, m0 + u] * SUBLANES, SUBLANES) for u in range(SCATTER_BATCH)]
            vals = [o_ref[0, pl.ds(rows[u], SUBLANES), :] + y_ref[0, g, pl.ds(m0 + u, SUBLANES, stride=pitch), :]
                    for u in range(SCATTER_BATCH)]
            for u in range(SCATTER_BATCH):
                o_ref[0, pl.ds(rows[u], SUBLANES), :] = vals[u]


def _scatter(y, idx_rows, tu, slots):
    b, n_exp = y.shape[:2]
    pitch = _slot_pitch(slots)
    ge = DISPATCH_EXPERTS
    assert n_exp % ge == 0 and slots % SCATTER_BATCH == 0
    return pl.pallas_call(
        functools.partial(_scatter_kernel, slots, pitch),
        grid=(b, n_exp // ge),
        in_specs=[
            pl.BlockSpec((ge, 1, slots), lambda i, e: (i * (n_exp // ge) + e, 0, 0), memory_space=pltpu.SMEM),
            pl.BlockSpec((1, ge, SUBLANES * pitch, LANES), lambda i, e: (i, e, 0, 0)),
        ],
        out_specs=pl.BlockSpec((1, tu * SUBLANES, LANES), lambda i, e: (i, 0, 0)),
        out_shape=jax.ShapeDtypeStruct((b, tu * SUBLANES, LANES), F32),
        compiler_params=_params("parallel", "arbitrary"),
        name="scatter",
    )(idx_rows, y)


FINAL_PARTS = 4


def _final_kernel(mod_ref, g_ref, *refs):
    o_ref = refs[-1]
    tm = FINAL_TILE
    for n in range(FINAL_PARTS):
        x_ref, f_ref = refs[n], refs[FINAL_PARTS + n]
        x = x_ref[0] + mod_ref[0][M_G2:M_G2 + 1, :] * _from_gather_layout(f_ref, tm)
        o_ref[0, n * tm:(n + 1) * tm, :] = _rms(x) * g_ref[...]


def _final(x, ffn, mrows, final_g, n_ctx):
    b, tu, d = x.shape
    tm = FINAL_TILE
    skip = n_ctx // tm
    parts = FINAL_PARTS
    assert (tu - n_ctx) % (tm * parts) == 0
    tile = lambda n: pl.BlockSpec((1, tm, d), lambda i, j: (i, j * parts + n + skip, 0))
    ftile = lambda n: pl.BlockSpec((1, tm * SUBLANES, LANES), lambda i, j: (i, j * parts + n + skip, 0))
    return pl.pallas_call(
        _final_kernel,
        grid=(b, (tu - n_ctx) // (tm * parts)),
        in_specs=([pl.BlockSpec((1, M_ROWS, d), lambda i, j: (i, 0, 0)), pl.BlockSpec((1, d), lambda i, j: (0, 0))]
                  + [tile(n) for n in range(parts)] + [ftile(n) for n in range(parts)]),
        out_specs=pl.BlockSpec((1, tm * parts, d), lambda i, j: (i, j, 0)),
        out_shape=jax.ShapeDtypeStruct((b, tu - n_ctx, d), F32),
        compiler_params=_params("parallel", "parallel"),
        name="final_norm",
    )(mrows, final_g.reshape(1, d), *([x] * parts + [ffn] * parts))


def _rope_tables(n_lat, n_ctx):
    rows = n_lat // GRID_W
    rowp = jnp.repeat(jnp.arange(rows, dtype=F32), GRID_W)
    colp = jnp.tile(jnp.arange(GRID_W, dtype=F32), rows)
    axis_dim = HEAD_DIM // 2
    inv_freq = ROPE_BASE ** (-jnp.arange(0, axis_dim, 2, dtype=F32) / axis_dim)
    ar = rowp[:, None] * inv_freq
    ac = colp[:, None] * inv_freq
    cs = jnp.concatenate([jnp.cos(ar), jnp.cos(ar), jnp.cos(ac), jnp.cos(ac)], axis=1)
    sn = jnp.concatenate([-jnp.sin(ar), jnp.sin(ar), -jnp.sin(ac), jnp.sin(ac)], axis=1)
    reps = LANES // HEAD_DIM
    cs = jnp.concatenate([jnp.ones((n_ctx, LANES), F32), jnp.tile(cs, (1, reps))], axis=0)
    sn = jnp.concatenate([jnp.zeros((n_ctx, LANES), F32), jnp.tile(sn, (1, reps))], axis=0)
    return cs, sn


def kernel(x, c, ctx, c_ctx, w_mod, b_mod, norm1_g, norm2_g, w_in, conv_w, ret_decay_logit, attn_sink,
           w_out, w_router, w_gate, w_up, w_down, final_g):
    b, n_lat, d = x.shape
    n_ctx = ctx.shape[1]
    depth = w_in.shape[0]
    n_exp = w_router.shape[2]
    tu = n_ctx + n_lat
    assert w_in.shape[2] == IN_WIDTH and tu % INPROJ_TILE == 0
    assert n_ctx % FINAL_TILE == 0 and n_lat % FINAL_TILE == 0
    assert b + 1 <= SUBLANES and n_exp == N_EXPERTS

    c_rows = jnp.concatenate([c, c_ctx[None], jnp.zeros((SUBLANES - b - 1, d), F32)], axis=0)
    mods = _mod_vectors(c_rows, w_mod, b_mod).reshape(depth, SUBLANES, 6, d)
    qdec, win, cd, dm = _decay_tables(ret_decay_logit)
    rope_cs, rope_sn = _rope_tables(n_lat, n_ctx)
    sink_rows = jnp.broadcast_to(jnp.repeat(attn_sink.astype(F32), CHUNK, axis=1)[:, :, None],
                                 (depth, ATTN_HEADS * CHUNK, LANES))
    w_in_bf = w_in.astype(BF16)
    w_out_bf = w_out.astype(BF16)
    wr = jnp.swapaxes(w_router, 1, 2)
    wr_hi = wr.astype(BF16)
    wr_lo = (wr - wr_hi.astype(F32)).astype(BF16)
    g1 = norm1_g.reshape(depth, 1, d)
    g2 = norm2_g.reshape(depth, 1, d)
    mrows = jnp.concatenate([mods[:, :b], jnp.broadcast_to(mods[:, b][:, None], (depth, b, 6, d)),
                             jnp.zeros((depth, b, M_ROWS - 12, d), F32)], axis=2)
    mrows_in = mrows.at[1:, :, M_G2].set(mrows[:-1, :, M_G2])
    mrows_in = mrows_in.at[1:, :, M_CTX + M_G2].set(mrows[:-1, :, M_CTX + M_G2])

    xu = jnp.concatenate([ctx, x], axis=1)
    ffn = None
    for l in range(depth):
        xu, proj = _inproj(xu, ffn, mrows_in[l], g1, w_in_bf, l, rope_cs, rope_sn, n_ctx)
        sp = _ret_states(proj[1], win, cd, l, n_ctx)
        xu, hg, aff = _mixer(proj, sp, (dm, qdec), conv_w, sink_rows, xu, mrows[l], g2, w_out_bf, l,
                             wr_hi, wr_lo, n_ctx)
        idx, gate = _route(aff, n_ctx, with_ctx=l < depth - 1)
        cap = idx.shape[3]
        idx_rows = idx.reshape(b * n_exp, 1, cap)
        xs = _gather(hg, idx_rows, n_exp, cap)
        y = _ffn(xs, w_gate, w_up, w_down, l, gate, cap)
        ffn = _scatter(y, idx_rows, tu, cap)
    return _final(xu, ffn, mrows[depth - 1], final_g, n_ctx)
```

```python
import functools

import jax
import jax.numpy as jnp
from jax import lax
from jax.experimental import pallas as pl
from jax.experimental.pallas import tpu as pltpu

F32 = jnp.float32
BF16 = jnp.bfloat16
I32 = jnp.int32

HEAD_DIM = 64
CONV_WIDTH = 256
RET_HEADS = 4
RET_WIDTH = RET_HEADS * HEAD_DIM
ATTN_HEADS = 8
ATTN_KV_HEADS = 2
ATTN_GROUP = ATTN_HEADS // ATTN_KV_HEADS
ATTN_WIDTH = ATTN_HEADS * HEAD_DIM
KV_WIDTH = ATTN_KV_HEADS * HEAD_DIM
CHUNK = 128
GRID_W = 64
N_EXPERTS = 16
CAPACITY_FACTOR = 2
ROPE_BASE = 10000.0
EPS = 1e-6
NEG_INF = -1e30
QK_SCALE = HEAD_DIM ** -0.5
LOG2E = 1.4426950408889634

LANES = 128
SUBLANES = 8
BF16_ROWS = 2 * SUBLANES
VMEM_LIMIT_BYTES = 56 * 1024 * 1024

O_CB = 0
O_CC = O_CB + CONV_WIDTH
O_CX = O_CC + CONV_WIDTH
O_RQ = O_CX + CONV_WIDTH
O_RK = O_RQ + RET_WIDTH
O_RV = O_RK + RET_WIDTH
O_GF = O_RV + RET_WIDTH
O_GB = O_GF + RET_WIDTH
O_AQ = O_GB + RET_WIDTH
O_AK = O_AQ + ATTN_WIDTH
O_AV = O_AK + KV_WIDTH
IN_WIDTH = O_AV + KV_WIDTH

M_SH1, M_SC1, M_G1, M_SH2, M_SC2, M_G2 = range(6)
M_CTX = 6
M_ROWS = 16

INPROJ_TILE = 1088
FINAL_TILE = 256


def _params(*sem, vmem=VMEM_LIMIT_BYTES):
    return pltpu.CompilerParams(dimension_semantics=sem, vmem_limit_bytes=vmem)


def _dot(a, b):
    return jnp.dot(a, b, preferred_element_type=F32)


def _dot_nt(a, b):
    return lax.dot_general(a, b, (((1,), (1,)), ((), ())), preferred_element_type=F32)


def _silu(v):
    return v * jax.nn.sigmoid(v)


def _mod_kernel(c_ref, w_ref, b_ref, o_ref):
    s = _silu(c_ref[...])
    s_hi = s.astype(BF16)
    s_lo = (s - s_hi.astype(F32)).astype(BF16)
    w = w_ref[0]
    w_hi = w.astype(BF16)
    w_lo = (w - w_hi.astype(F32)).astype(BF16)
    n = s.shape[0]
    head = _dot(jnp.concatenate([s_hi, s_lo], axis=0), w_hi)
    o_ref[0] = head[:n] + (head[n:] + _dot(s_hi, w_lo)) + b_ref[0]


def _mod_vectors(c_rows, w_mod, b_mod):
    depth, d_model, width = w_mod.shape
    tn = 1536
    return pl.pallas_call(
        _mod_kernel,
        grid=(depth, width // tn),
        in_specs=[
            pl.BlockSpec((SUBLANES, d_model), lambda l, n: (0, 0)),
            pl.BlockSpec((1, d_model, tn), lambda l, n: (l, 0, n)),
            pl.BlockSpec((1, 1, tn), lambda l, n: (l, 0, n)),
        ],
        out_specs=pl.BlockSpec((1, SUBLANES, tn), lambda l, n: (l, 0, n)),
        out_shape=jax.ShapeDtypeStruct((depth, SUBLANES, width), F32),
        compiler_params=_params("parallel", "parallel"),
        name="mod_vectors",
    )(c_rows, w_mod, b_mod.reshape(depth, 1, width))


def _log_sigmoid(v):
    return -jnp.log(1.0 + jnp.exp(-v))


def _tables_kernel(lgl_ref, lgr_ref, qdec_ref, win_ref, cd_ref, dm_ref):
    pos = lax.broadcasted_iota(I32, (CHUNK, RET_WIDTH), 0).astype(F32)
    ri = lax.broadcasted_iota(I32, (CHUNK, RET_HEADS * CHUNK), 0).astype(F32)
    rj = (lax.broadcasted_iota(I32, (CHUNK, RET_HEADS * CHUNK), 1) & (CHUNK - 1)).astype(F32)
    for d in range(2):
        lg = _log_sigmoid(lgl_ref[0, d])
        lg1 = lg[0:1, :]
        if d == 0:
            qdec_ref[0, d] = jnp.exp(lg1 * (pos + 1.0))
            win_ref[0, d] = jnp.exp(lg1 * (CHUNK - 1.0 - pos))
            diff = ri - rj
        else:
            qdec_ref[0, d] = jnp.exp(lg1 * (CHUNK - pos))
            win_ref[0, d] = jnp.exp(lg1 * pos)
            diff = rj - ri
        cd_ref[0, d] = jnp.exp(lg * float(CHUNK))
        lr = _log_sigmoid(lgr_ref[0, d])
        dm_ref[0, d] = jnp.where(diff >= 0.0, jnp.exp(lr * jnp.maximum(diff, 0.0)), 0.0)


def _decay_tables(ret_decay_logit):
    depth = ret_decay_logit.shape[0]
    lg = ret_decay_logit.astype(F32)
    lgl = jnp.broadcast_to(jnp.repeat(lg, HEAD_DIM, axis=-1)[:, :, None, :], (depth, 2, SUBLANES, RET_WIDTH))
    lgr = jnp.broadcast_to(jnp.repeat(lg, CHUNK, axis=-1)[:, :, None, :], (depth, 2, CHUNK, RET_HEADS * CHUNK))
    return pl.pallas_call(
        _tables_kernel,
        grid=(depth,),
        in_specs=[
            pl.BlockSpec((1, 2, SUBLANES, RET_WIDTH), lambda l: (l, 0, 0, 0)),
            pl.BlockSpec((1, 2, CHUNK, RET_HEADS * CHUNK), lambda l: (l, 0, 0, 0)),
        ],
        out_specs=[
            pl.BlockSpec((1, 2, CHUNK, RET_WIDTH), lambda l: (l, 0, 0, 0)),
            pl.BlockSpec((1, 2, CHUNK, RET_WIDTH), lambda l: (l, 0, 0, 0)),
            pl.BlockSpec((1, 2, SUBLANES, RET_WIDTH), lambda l: (l, 0, 0, 0)),
            pl.BlockSpec((1, 2, CHUNK, RET_HEADS * CHUNK), lambda l: (l, 0, 0, 0)),
        ],
        out_shape=[
            jax.ShapeDtypeStruct((depth, 2, CHUNK, RET_WIDTH), F32),
            jax.ShapeDtypeStruct((depth, 2, CHUNK, RET_WIDTH), F32),
            jax.ShapeDtypeStruct((depth, 2, SUBLANES, RET_WIDTH), F32),
            jax.ShapeDtypeStruct((depth, 2, CHUNK, RET_HEADS * CHUNK), F32),
        ],
        compiler_params=_params("parallel"),
        name="decay_tables",
    )(lgl, lgr)


def _rope(v, cs, sn):
    lane = lax.broadcasted_iota(I32, (1, LANES), 1)
    first = (lane & 31) < 16
    outs = []
    for g in range(v.shape[1] // LANES):
        vg = v[:, g * LANES:(g + 1) * LANES]
        sw = jnp.where(first, pltpu.roll(vg, LANES - 16, 1), pltpu.roll(vg, 16, 1))
        outs.append(vg * cs + sw * sn)
    return outs[0] if len(outs) == 1 else jnp.concatenate(outs, axis=1)


def _swap_halves(v):
    return jnp.concatenate([v[:, HEAD_DIM:], v[:, :HEAD_DIM]], axis=1)


def _from_gather_layout(f_ref, tm):
    return jnp.concatenate(
        [f_ref[0, pl.ds(q, tm, stride=SUBLANES), :] for q in range(SUBLANES)], axis=1)


def _rms(v):
    return v * lax.rsqrt(jnp.mean(v * v, axis=-1, keepdims=True) + EPS)


def _mod_rows(mod_ref, tm, n_ctx, tile):
    m = mod_ref[0]
    if n_ctx % tm == 0:
        is_ctx = tile < n_ctx // tm
    else:
        is_ctx = tile * tm + lax.broadcasted_iota(I32, (tm, 1), 0) < n_ctx

    def row(r):
        return jnp.where(is_ctx, m[M_CTX + r:M_CTX + r + 1, :], m[r:r + 1, :])

    return row


def _inproj_kernel(has_ffn, has_ctx, n_ctx, *refs):
    if has_ffn:
        x_ref, f_ref, mod_ref, g_ref, w_ref, cs_ref, sn_ref = refs[:7]
        outs = refs[7:]
        xo_ref = outs[0]
        outs = outs[1:]
    elif has_ctx:
        x_ref, ctx_ref, mod_ref, g_ref, w_ref, cs_ref, sn_ref = refs[:7]
        outs = refs[7:]
    else:
        x_ref, mod_ref, g_ref, w_ref, cs_ref, sn_ref = refs[:6]
        outs = refs[6:]
    cz_ref, rqkv_ref, gates_ref, aq_ref, kv_ref = outs
    tm = x_ref.shape[1]
    row = _mod_rows(mod_ref, tm, n_ctx, pl.program_id(1))

    x = x_ref[0]
    if has_ctx:
        x = jnp.where(pl.program_id(1) == 0, ctx_ref[0], x)
    if has_ffn:
        x = x + row(M_G2) * _from_gather_layout(f_ref, tm)
        xo_ref[0] = x
    h = (_rms(x) * g_ref[...]) * (1.0 + row(M_SC1)) + row(M_SH1)
    h = h.astype(BF16)
    cs = cs_ref[...]
    sn = sn_ref[...]

    def proj(a, b):
        return _dot(h, w_ref[:, a:b])

    bounds = (O_CB, O_CC, O_CX, O_RQ, O_RK, O_RV, O_GF, O_GB, O_AQ, O_AK, O_AV, IN_WIDTH)
    p = [proj(a, b) for a, b in zip(bounds[:-1], bounds[1:])]
    cz_ref[0, :, 0:CONV_WIDTH] = p[0]
    cz_ref[0, :, CONV_WIDTH:] = p[1] * p[2]
    rqkv_ref[0, :, 0:RET_WIDTH] = _rope(p[3], cs, sn).astype(BF16)
    rqkv_ref[0, :, RET_WIDTH:2 * RET_WIDTH] = _rope(p[4] * QK_SCALE, cs, sn).astype(BF16)
    rqkv_ref[0, :, 2 * RET_WIDTH:] = p[5].astype(BF16)
    gates_ref[0, :, 0:RET_WIDTH] = _silu(p[6])
    gates_ref[0, :, RET_WIDTH:] = _silu(p[7])
    aq_ref[0] = (_rope(p[8], cs, sn) * (QK_SCALE * LOG2E)).astype(BF16)
    ak = _rope(p[9], cs, sn)
    av = p[10]
    kv_ref[0, :, 0:KV_WIDTH] = ak.astype(BF16)
    kv_ref[0, :, KV_WIDTH:2 * KV_WIDTH] = _swap_halves(ak).astype(BF16)
    kv_ref[0, :, 2 * KV_WIDTH:] = av.astype(BF16)


def _layer_block(stacked, layer):
    rest = stacked.shape[1:]
    return pl.BlockSpec((None,) + rest, lambda *_: (layer,) + (0,) * len(rest))


def _inproj(x, ffn, mrows, norm_g, w_in_bf, layer, rope_cs, rope_sn, n_ctx, ctx=None):
    b, tu, d = x.shape
    tm = INPROJ_TILE
    has_ffn = ffn is not None
    has_ctx = ctx is not None
    assert not (has_ffn and has_ctx)
    tok = lambda width: pl.BlockSpec((1, tm, width), lambda i, j: (i, j, 0))
    in_specs = [tok(d)]
    args = [x]
    if has_ctx:
        tm = ctx.shape[1]
        tu = tu + tm
        assert tm == n_ctx and x.shape[1] % tm == 0
        in_specs = [pl.BlockSpec((1, tm, d), lambda i, j: (i, jnp.maximum(j - 1, 0), 0)),
                    pl.BlockSpec((1, tm, d), lambda i, j: (i, 0, 0))]
        args = [x, ctx]
    if has_ffn:
        in_specs.append(pl.BlockSpec((1, tm * SUBLANES, LANES), lambda i, j: (i, j, 0)))
        args.append(ffn)
    in_specs += [
        pl.BlockSpec((1, M_ROWS, d), lambda i, j: (i, 0, 0)),
        _layer_block(norm_g, layer),
        pl.BlockSpec((None, d, IN_WIDTH), lambda i, j: (layer, 0, 0), pipeline_mode=pl.Buffered(1)),
        pl.BlockSpec((tm, LANES), lambda i, j: (j, 0)),
        pl.BlockSpec((tm, LANES), lambda i, j: (j, 0)),
    ]
    args += [mrows, norm_g, w_in_bf, rope_cs, rope_sn]
    widths = [(2 * CONV_WIDTH, F32), (3 * RET_WIDTH, BF16), (2 * RET_WIDTH, F32), (ATTN_WIDTH, BF16),
              (3 * KV_WIDTH, BF16)]
    out_specs = [tok(w) for w, _ in widths]
    out_shape = [jax.ShapeDtypeStruct((b, tu, w), dt) for w, dt in widths]
    if has_ffn:
        out_specs = [tok(d)] + out_specs
        out_shape = [jax.ShapeDtypeStruct((b, tu, d), F32)] + out_shape
    res = pl.pallas_call(
        functools.partial(_inproj_kernel, has_ffn, has_ctx, n_ctx),
        grid=(b, tu // tm),
        in_specs=in_specs,
        out_specs=out_specs,
        out_shape=out_shape,
        compiler_params=_params("parallel", "parallel"),
        name="inproj",
    )(*args)
    if has_ffn:
        return res[0], res[1:]
    return x, res


def _head_block_mask(n):
    r = lax.broadcasted_iota(I32, (n, n), 0) // HEAD_DIM
    c = lax.broadcasted_iota(I32, (n, n), 1) // HEAD_DIM
    return r == c


STATE_UNROLL = 17


def _states_kernel(n_ctx_chunks, n_chunks, rqkv_ref, win_ref, cd_ref, sp_ref, s_scr):
    same_head = _head_block_mask(RET_WIDTH)
    s_scr[...] = jnp.zeros_like(s_scr)

    def chunk_update(d, pos):
        if d == 0:
            c = pos
        else:
            c = jnp.where(pos < n_ctx_chunks, n_ctx_chunks - 1 - pos, n_chunks + n_ctx_chunks - 1 - pos)
        off = pl.multiple_of(c * CHUNK, CHUNK)
        kw = rqkv_ref[0, pl.ds(off, CHUNK), RET_WIDTH:2 * RET_WIDTH].astype(F32) * win_ref[d]
        v = rqkv_ref[0, pl.ds(off, CHUNK), 2 * RET_WIDTH:]
        kw_t = kw.T.astype(BF16)
        yield
        u = _dot(kw_t, v)
        yield
        return c, jnp.where(same_head, u, 0.0)

    def body(i, carry):
        todo = [(d, i * STATE_UNROLL + n) for n in range(STATE_UNROLL) for d in range(2)]
        done = _round_robin([chunk_update(d, pos) for d, pos in todo])
        for (d, _), (c, u) in zip(todo, done):
            s = s_scr[d]
            sp_ref[0, d, pl.ds(c, 1)] = s.astype(BF16)[None]
            s_scr[d] = s * cd_ref[d][0:1, :] + u
        return carry

    assert n_chunks % STATE_UNROLL == 0
    lax.fori_loop(0, n_chunks // STATE_UNROLL, body, 0)


def _ret_states(rqkv, win, cd, layer, n_ctx):
    b, tu, _ = rqkv.shape
    w = RET_WIDTH
    n_chunks = tu // CHUNK
    return pl.pallas_call(
        functools.partial(_states_kernel, n_ctx // CHUNK, n_chunks),
        grid=(b,),
        in_specs=[
            pl.BlockSpec((1, tu, 3 * w), lambda i: (i, 0, 0)),
            _layer_block(win, layer),
            _layer_block(cd, layer),
        ],
        out_specs=pl.BlockSpec((1, 2, n_chunks, w, w), lambda i: (i, 0, 0, 0, 0)),
        out_shape=jax.ShapeDtypeStruct((b, 2, n_chunks, w, w), BF16),
        scratch_shapes=[pltpu.VMEM((2, w, w), F32)],
        compiler_params=_params("parallel"),
        name="ret_states",
    )(rqkv, win, cd)


def _group_mean(t, ones_bd):
    hi = t.astype(BF16)
    lo = (t - hi.astype(F32)).astype(BF16)
    s = _dot(jnp.concatenate([hi, lo], axis=0), ones_bd)
    n = t.shape[0]
    return (s[:n] + s[n:]) * (1.0 / HEAD_DIM)


def _mix_chunk(c, n_ctx_chunks, n_chunks, cz, z_last, z_first, cw, rqkv, gates, sp, dm_ref, qdec_ref,
               qa, kvs, sink_ref):
    is_lat = c >= n_ctx_chunks

    z = cz[:, CONV_WIDTH:]
    row = lax.broadcasted_iota(I32, (CHUNK, 1), 0)
    has_prev = jnp.logical_and(c != 0, c != n_ctx_chunks)
    has_next = jnp.logical_and(c != n_ctx_chunks - 1, c != n_chunks - 1)
    z_before = jnp.where(row == 0, jnp.where(has_prev, z_last, 0.0), pltpu.roll(z, 1, 0))
    z_after = jnp.where(row == CHUNK - 1, jnp.where(has_next, z_first, 0.0), pltpu.roll(z, CHUNK - 1, 0))
    conv = cz[:, :CONV_WIDTH] * (z_before * cw[0:1, :] + z * cw[1:2, :] + z_after * cw[2:3, :])

    q = rqkv[:, 0:RET_WIDTH]
    k = rqkv[:, RET_WIDTH:2 * RET_WIDTH]
    v = rqkv[:, 2 * RET_WIDTH:]
    lane_head = lax.broadcasted_iota(I32, (1, RET_WIDTH), 1) // HEAD_DIM
    kz = jnp.zeros_like(k)
    k_heads = jnp.concatenate([jnp.where(lane_head == hh, k, kz) for hh in range(RET_HEADS)], axis=0)
    v_heads = jnp.concatenate([jnp.where(lane_head == hh, v, kz) for hh in range(RET_HEADS)], axis=0)
    scores = _dot_nt(q, k_heads)

    keys, v_t = kvs
    half = lax.broadcasted_iota(I32, (1, LANES), 1) // HEAD_DIM
    qa_z = jnp.zeros((CHUNK, LANES), BF16)
    combos = [(hk, par) for hk in range(ATTN_KV_HEADS) for par in range(2)]
    n_ctx = keys[0].shape[0] - 3 * CHUNK
    cols2 = 2 * CHUNK
    ik = lax.broadcasted_iota(I32, (CHUNK, cols2), 0)
    iq = lax.broadcasted_iota(I32, (CHUNK, cols2), 1) & (CHUNK - 1)
    off = jnp.full((CHUNK, cols2), NEG_INF, F32)
    bias_prev = jnp.where(jnp.logical_and(jnp.logical_and(is_lat, c - 1 >= n_ctx_chunks), ik >= iq), 0.0, off)
    bias_cur = jnp.where(is_lat, 0.0, off)
    bias_next = jnp.where(jnp.logical_and(jnp.logical_and(is_lat, c + 1 <= n_chunks - 1), ik <= iq), 0.0, off)
    assert n_ctx > 0
    logits = {}
    for hk, par in combos:
        ja, jb = 2 * hk, 2 * hk + 1
        qst = jnp.concatenate([
            jnp.where(half == par, qa[:, ja * LANES:(ja + 1) * LANES], qa_z),
            jnp.where(half == par, qa[:, jb * LANES:(jb + 1) * LANES], qa_z)], axis=0)
        sel = 0 if par == hk else 1
        r = _dot_nt(keys[sel], qst)
        s = jnp.concatenate([r[:CHUNK] + bias_prev, r[CHUNK:2 * CHUNK] + bias_cur,
                             r[2 * CHUNK:3 * CHUNK] + bias_next, r[3 * CHUNK:]], axis=0)
        ha, hb = ATTN_GROUP * hk + par, ATTN_GROUP * hk + par + 2
        snk = jnp.concatenate([sink_ref[ha * CHUNK:ha * CHUNK + 1, :],
                               sink_ref[hb * CHUNK:hb * CHUNK + 1, :]], axis=1) * LOG2E
        logits[hk, par] = (s, snk, jnp.maximum(jnp.max(s, axis=0, keepdims=True), snk))
    yield

    qf = q.astype(F32)
    outs = []
    for d in range(2):
        p = (scores * dm_ref[d]).astype(BF16)
        lhs = jnp.concatenate([p, (qf * qdec_ref[d]).astype(BF16)], axis=1)
        o = _dot(lhs, jnp.concatenate([v_heads, sp[d]], axis=0))
        outs.append(o)
    o2 = jnp.concatenate(outs, axis=0)
    yield

    att = {}
    ones_keys = jnp.ones((BF16_ROWS, keys[0].shape[0]), BF16)
    for n, (hk, par) in enumerate(combos):
        s, snk, mx = logits[hk, par]
        e = jnp.exp2(s - mx).astype(BF16)
        o_t = _dot(jnp.concatenate([v_t[hk * HEAD_DIM:(hk + 1) * HEAD_DIM, :], ones_keys], axis=0), e)
        den = o_t[HEAD_DIM:HEAD_DIM + 1, :] + jnp.exp2(snk - mx)
        att[hk, par] = o_t[:HEAD_DIM, :] * (1.0 / den)
        if n == 0:
            ones_bd = jnp.where(_head_block_mask(RET_WIDTH), 1.0, 0.0).astype(BF16)
            dl = o2 - _group_mean(o2, ones_bd)
            on = dl * lax.rsqrt(_group_mean(dl * dl, ones_bd) + EPS)
            ret = on[:CHUNK] * gates[:, 0:RET_WIDTH] + on[CHUNK:] * gates[:, RET_WIDTH:]
        yield

    cols = []
    for hk in range(ATTN_KV_HEADS):
        cols.append(jnp.concatenate([att[hk, 0][:, :CHUNK], att[hk, 1][:, :CHUNK]], axis=0).T)
        cols.append(jnp.concatenate([att[hk, 0][:, CHUNK:], att[hk, 1][:, CHUNK:]], axis=0).T)
    return jnp.concatenate([conv, ret] + cols, axis=1)


def _round_robin(gens):
    results = [None] * len(gens)
    active = list(range(len(gens)))
    while active:
        for i in list(active):
            try:
                next(gens[i])
            except StopIteration as done:
                results[i] = done.value
                active.remove(i)
    return results


MIX_CHUNKS = 2


def _mixer_kernel(n_ctx_chunks, n_chunks, n_exp, has_ctx,
                  cz_ref, czp_ref, czn_ref, cw_ref, rqkv_ref, gates_ref, sp_ref, dm_ref, qdec_ref,
                  aq_ref, kvp_ref, kvc_ref, kvn_ref, kvx_ref, sink_ref, x_ref, *rest):
    ctx_ref, rest = (rest[0], rest[1:]) if has_ctx else (None, rest)
    mod_ref, g_ref, w_ref, wrh_ref, wrl_ref, xo_ref, hg_ref, aff_ref, mix_scr = rest
    t = pl.program_id(0)
    last = pl.num_programs(0) - 2
    tiles = n_chunks // MIX_CHUNKS

    @pl.when(t == 0)
    def _():
        mix_scr[...] = jnp.zeros_like(mix_scr)

    tail = _outproj_tail(n_ctx_chunks * CHUNK, n_exp, lax.rem(jnp.maximum(t - 1, 0), tiles), mix_scr[...], x_ref,
                         mod_ref, g_ref, w_ref, wrh_ref, wrl_ref, xo_ref, hg_ref, aff_ref, ctx_ref)
    j = lax.rem(jnp.minimum(t, last), tiles)
    cw = cw_ref[...]

    def values_t(blk):
        return blk[:, 2 * KV_WIDTH:].astype(F32).T.astype(BF16)

    kv_blocks = ([kvp_ref[0]] + [kvc_ref[0, s * CHUNK:(s + 1) * CHUNK, :] for s in range(MIX_CHUNKS)] + [kvn_ref[0]])
    kv_ctx = [kvx_ref[0, t * CHUNK:(t + 1) * CHUNK, :] for t in range(kvx_ref.shape[1] // CHUNK)]
    vt_blocks = [values_t(blk) for blk in kv_blocks]
    vt_ctx = [values_t(blk) for blk in kv_ctx]
    gens = []
    for s in range(MIX_CHUNKS):
        lo, hi = s * CHUNK, (s + 1) * CHUNK
        if s == 0:
            z_last = czp_ref[0, SUBLANES - 1:SUBLANES, CONV_WIDTH:]
        else:
            z_last = cz_ref[0, lo - 1:lo, CONV_WIDTH:]
        if s == MIX_CHUNKS - 1:
            z_first = czn_ref[0, 0:1, CONV_WIDTH:]
        else:
            z_first = cz_ref[0, hi:hi + 1, CONV_WIDTH:]
        window = kv_blocks[s:s + 3] + kv_ctx
        keys = tuple(jnp.concatenate([blk[:, i * KV_WIDTH:(i + 1) * KV_WIDTH] for blk in window], axis=0)
                     for i in (0, 1))
        v_t = jnp.concatenate(vt_blocks[s:s + 3] + vt_ctx, axis=1)
        gens.append(_mix_chunk(j * MIX_CHUNKS + s, n_ctx_chunks, n_chunks, cz_ref[0, lo:hi, :], z_last, z_first,
                               cw, rqkv_ref[0, lo:hi, :], gates_ref[0, lo:hi, :],
                               (sp_ref[0, 0, s], sp_ref[0, 1, s]), dm_ref, qdec_ref, aq_ref[0, lo:hi, :],
                               (keys, v_t), sink_ref))
    mixes = [mix.astype(BF16) for mix in _round_robin([tail] + gens)[1:]]
    mix_scr[...] = jnp.concatenate(mixes, axis=0)


def _mixer(proj, sp, tabs, conv_w, sink_rows, x, mrows, norm_g, w_out_bf, layer, wr_hi, wr_lo, n_ctx, ctx=None):
    cz, rqkv, gates, aq, kv = proj
    dm, qdec = tabs
    b, tu, _ = cz.shape
    d = x.shape[2]
    n_exp = wr_hi.shape[1]
    n_chunks = tu // CHUNK
    mc = MIX_CHUNKS
    rows = mc * CHUNK
    assert n_chunks % mc == 0 and (n_ctx // CHUNK) % mc == 0
    last = n_chunks - 1
    tiles = n_chunks // mc
    per8 = rows // SUBLANES

    def mixed(t):
        m = jnp.minimum(t, b * tiles - 1)
        return m // tiles, lax.rem(m, tiles)

    def projected(t):
        p = jnp.maximum(t - 1, 0)
        return p // tiles, lax.rem(p, tiles)

    def at_mixed(block, tile_index):
        def index(t):
            i, j = mixed(t)
            return (i, tile_index(j), 0)
        return pl.BlockSpec(block, index)

    cur = lambda w: at_mixed((1, rows, w), lambda j: j)

    def out(r, w):
        return pl.BlockSpec((1, r, w), lambda t: projected(t) + (0,))

    if ctx is None:
        residual_specs, residual_args = [out(rows, d)], [x]
    else:
        assert ctx.shape[1] == rows and x.shape[1] == tu - rows

        def latent_tile(t):
            i, j = projected(t)
            return (i, jnp.maximum(j - 1, 0), 0)

        residual_specs = [pl.BlockSpec((1, rows, d), latent_tile),
                          pl.BlockSpec((1, rows, d), lambda t: (projected(t)[0], 0, 0))]
        residual_args = [x, ctx]
    wcz, wkv = cz.shape[2], kv.shape[2]
    in_specs = [
        cur(wcz),
        at_mixed((1, SUBLANES, wcz), lambda j: jnp.maximum(j * per8 - 1, 0)),
        at_mixed((1, SUBLANES, wcz), lambda j: jnp.minimum((j + 1) * per8, tu // SUBLANES - 1)),
        _layer_block(conv_w, layer),
        cur(rqkv.shape[2]), cur(gates.shape[2]),
        pl.BlockSpec((1, 2, mc, RET_WIDTH, RET_WIDTH), lambda t: (mixed(t)[0], 0, mixed(t)[1], 0, 0)),
        _layer_block(dm, layer), _layer_block(qdec, layer),
        cur(ATTN_WIDTH),
        at_mixed((1, CHUNK, wkv), lambda j: jnp.maximum(j * mc - 1, 0)),
        cur(wkv),
        at_mixed((1, CHUNK, wkv), lambda j: jnp.minimum((j + 1) * mc, last)),
        at_mixed((1, n_ctx, wkv), lambda j: 0),
        _layer_block(sink_rows, layer),
        *residual_specs,
        pl.BlockSpec((1, M_ROWS, d), lambda t: (projected(t)[0], 0, 0)),
        _layer_block(norm_g, layer),
        _layer_block(w_out_bf, layer),
        _layer_block(wr_hi, layer), _layer_block(wr_lo, layer),
    ]
    args = [cz, cz, cz, conv_w, rqkv, gates, sp, dm, qdec, aq, kv, kv, kv, kv, sink_rows,
            *residual_args, mrows, norm_g, w_out_bf, wr_hi, wr_lo]
    return pl.pallas_call(
        functools.partial(_mixer_kernel, n_ctx // CHUNK, n_chunks, n_exp, ctx is not None),
        grid=(b * tiles + 1,),
        in_specs=in_specs,
        out_specs=[out(rows, d), out(rows * SUBLANES, LANES),
                   pl.BlockSpec((1, n_exp, rows), lambda t: (projected(t)[0], 0, projected(t)[1]))],
        out_shape=[
            jax.ShapeDtypeStruct((b, tu, d), F32),
            jax.ShapeDtypeStruct((b, tu * SUBLANES, LANES), F32),
            jax.ShapeDtypeStruct((b, n_exp, tu), F32),
        ],
        scratch_shapes=[pltpu.VMEM((rows, CONV_WIDTH + RET_WIDTH + ATTN_WIDTH), BF16)],
        compiler_params=_params("arbitrary"),
        name="mixer",
    )(*args)


def _outproj_tail(n_ctx, n_exp, tile, mix, x_ref, mod_ref, g_ref, w_ref, wrh_ref, wrl_ref, xo_ref, hg_ref, aff_ref,
                  ctx_ref=None):
    tm = x_ref.shape[1]
    row = _mod_rows(mod_ref, tm, n_ctx, tile)

    y = _dot(mix, w_ref[...])
    yield
    x_in = x_ref[0] if ctx_ref is None else jnp.where(tile == 0, ctx_ref[0], x_ref[0])
    x = x_in + row(M_G1) * y
    xo_ref[0] = x
    h = (_rms(x) * g_ref[...]) * (1.0 + row(M_SC2)) + row(M_SH2)
    for q in range(SUBLANES):
        hg_ref[0, pl.ds(q, tm, stride=SUBLANES), :] = h[:, q * LANES:(q + 1) * LANES]
    yield
    h_hi = h.astype(BF16)
    h_lo = (h - h_hi.astype(F32)).astype(BF16)
    logits = (_dot_nt(wrh_ref[...], h_hi) + (_dot_nt(wrh_ref[...], h_lo) + _dot_nt(wrl_ref[...], h_hi)))
    yield
    e = jnp.exp(logits - jnp.max(logits, axis=0, keepdims=True))
    aff_ref[0] = e / jnp.sum(e, axis=0, keepdims=True)


MIN_NORMAL_F32_BITS = 0x00800000
ROUTE_TABLES = 4
ROUTE_GROUP = 4


def _route_stream_tables(v, k, tab_scr, ce_scr, cnt_scr, stream, thr):
    n_exp, n = v.shape
    lane = lax.broadcasted_iota(I32, (1, LANES), 1)
    tri_r = lax.broadcasted_iota(I32, (LANES, LANES), 0)
    tri_c = lax.broadcasted_iota(I32, (LANES, LANES), 1)
    upper = jnp.where(tri_r <= tri_c, 1.0, 0.0).astype(BF16)
    above = pltpu.bitcast(jnp.maximum(thr + 1, MIN_NORMAL_F32_BITS), F32)
    floor = pltpu.bitcast(thr, F32)
    n_gt = jnp.sum(jnp.where(v >= above, 1, 0), axis=1, keepdims=True)
    need = (k - n_gt).astype(F32)
    seen_eq = jnp.zeros((n_exp, 1), F32)
    counts = jnp.zeros((n_exp, LANES), F32)
    for j in range(n // LANES):
        vb = v[:, j * LANES:(j + 1) * LANES]
        gt = vb >= above
        eq = jnp.logical_and(vb >= floor, vb < above)
        both = jnp.concatenate([jnp.where(gt, 1.0, 0.0), jnp.where(eq, 1.0, 0.0)], axis=0).astype(BF16)
        pref = _dot(both, upper)
        rank_eq = pref[n_exp:] + seen_eq
        inc = pref[:n_exp] + jnp.minimum(rank_eq, need) - jnp.minimum(seen_eq, need)
        sel = jnp.logical_or(gt, jnp.logical_and(eq, rank_eq <= need))
        seen_eq = rank_eq[:, LANES - 1:LANES]
        counts = counts + jnp.where(lane == j, inc[:, LANES - 1:LANES], 0.0)
        a0 = jnp.where(sel, vb, 0.0)
        t0 = a0.astype(BF16).astype(F32)
        t1 = (a0 - t0).astype(BF16).astype(F32)
        t2 = a0 - t0 - t1
        for t, val in enumerate((inc, t0, t1, t2)):
            tab_scr[stream * ROUTE_TABLES + t, pl.ds(j, n_exp, stride=LANES), :] = val
    through = _dot(counts.astype(BF16), upper)
    for e in range(n_exp):
        ce_scr[stream, e] = through[e:e + 1, :]
        cnt_scr[stream, e] = counts[e:e + 1, :]


def _route_slots(e, stream, k, lo, tab_scr, ce_scr, cnt_scr):
    lane_f = lax.broadcasted_iota(I32, (1, LANES), 1).astype(F32)
    slot = lax.broadcasted_iota(I32, (k, LANES), 0).astype(F32)
    ones_rows = jnp.ones((LANES, LANES), BF16)
    cnt_rows = jnp.broadcast_to(cnt_scr[stream, e], (LANES, LANES)).astype(BF16)
    rows_e = pl.ds(pl.multiple_of(e * LANES, LANES), LANES)
    tab = jnp.concatenate([tab_scr[stream * ROUTE_TABLES + t, rows_e, :] for t in range(ROUTE_TABLES)],
                          axis=1).astype(BF16)
    before = jnp.where(ce_scr[stream, e] <= slot, 1.0, 0.0).astype(BF16)
    blk = _dot_nt(before, ones_rows)
    base = _dot_nt(before, cnt_rows)
    yield
    row = _dot(jnp.where(lane_f == blk, 1.0, 0.0).astype(BF16), tab)
    yield
    inc = row[:, :LANES]
    aff = row[:, LANES:2 * LANES] + row[:, 2 * LANES:3 * LANES] + row[:, 3 * LANES:]
    local = slot - base
    pos = _dot_nt(jnp.where(inc <= local, 1.0, 0.0).astype(BF16), ones_rows)
    yield
    tok = blk * LANES + pos + float(lo)
    gate = jnp.sum(jnp.where(inc == local + 1.0, aff, 0.0), axis=1, keepdims=True)
    return tok, gate


def _route_kernel(n_exp, streams, aff_ref, idx_ref, gate_ref, tab_scr, ce_scr, cnt_scr):
    a = aff_ref[0]
    tab_scr[...] = jnp.zeros_like(tab_scr)
    vs = [a[:, lo:lo + n] for lo, n, _, _ in streams]

    def enough(v, cand, k):
        return jnp.sum(jnp.where(v >= pltpu.bitcast(cand, F32), 1, 0), axis=1, keepdims=True) >= k

    def search(i, ts):
        out = []
        for t, v, (_, _, k, _) in zip(ts, vs, streams):
            hi = t | jnp.left_shift(jnp.int32(1), 30 - 2 * i)
            lo = jnp.left_shift(jnp.int32(1), 29 - 2 * i)
            out.append(jnp.where(enough(v, hi, k), jnp.where(enough(v, hi | lo, k), hi | lo, hi),
                                 jnp.where(enough(v, t | lo, k), t | lo, t)))
        return tuple(out)

    thrs = lax.fori_loop(0, 15, search, tuple(jnp.zeros((n_exp, 1), I32) for _ in streams))
    thrs = tuple(jnp.where(enough(v, t | 1, k), t | 1, t) for t, v, (_, _, k, _) in zip(thrs, vs, streams))
    for st, (v, (_, _, k, _)) in enumerate(zip(vs, streams)):
        _route_stream_tables(v, k, tab_scr, ce_scr, cnt_scr, st, thrs[st])

    slots = idx_ref.shape[3]

    assert [s[3] for s in streams] == [sum(s[2] for s in streams[:n]) for n in range(len(streams))]

    def expert_group(g, carry):
        experts = [g * ROUTE_GROUP + u for u in range(ROUTE_GROUP)]
        found = _round_robin([_route_slots(e, st, k, lo, tab_scr, ce_scr, cnt_scr)
                              for e in experts for st, (lo, _, k, _) in enumerate(streams)])
        for u, e in enumerate(experts):
            toks = []
            for st, (_, _, k, slot0) in enumerate(streams):
                tok, gate = found[u * len(streams) + st]
                toks.append(tok)
                gate_ref[0, pl.ds(e, 1), slot0:slot0 + k, :] = gate[None]
            if slots % LANES:
                toks.append(jnp.zeros((-slots % LANES, LANES), F32))
            row = jnp.concatenate(toks, axis=0).T[0:1, :slots]
            idx_ref[0, pl.ds(e, 1)] = row.astype(I32)[None]
        return carry

    assert n_exp % ROUTE_GROUP == 0
    lax.fori_loop(0, n_exp // ROUTE_GROUP, expert_group, 0)


def _route(aff, n_ctx, with_ctx):
    b, n_exp, tu = aff.shape
    n_lat = tu - n_ctx
    cap_l = CAPACITY_FACTOR * n_lat // n_exp
    cap_c = CAPACITY_FACTOR * n_ctx // n_exp
    assert n_lat // LANES <= LANES and n_ctx // LANES <= LANES
    streams = ((n_ctx, n_lat, cap_l, 0),) + (((0, n_ctx, cap_c, cap_l),) if with_ctx else ())
    slots = sum(s[2] for s in streams)
    return pl.pallas_call(
        functools.partial(_route_kernel, n_exp, streams),
        grid=(b,),
        in_specs=[pl.BlockSpec((1, n_exp, tu), lambda i: (i, 0, 0))],
        out_specs=[pl.BlockSpec((1, n_exp, 1, slots), lambda i: (i, 0, 0, 0)),
                   pl.BlockSpec((1, n_exp, slots, 1), lambda i: (i, 0, 0, 0))],
        out_shape=[jax.ShapeDtypeStruct((b, n_exp, 1, slots), I32),
                   jax.ShapeDtypeStruct((b, n_exp, slots, 1), F32)],
        scratch_shapes=[pltpu.VMEM((len(streams) * ROUTE_TABLES, n_exp * LANES, LANES), F32),
                        pltpu.VMEM((len(streams), n_exp, 1, LANES), F32),
                        pltpu.VMEM((len(streams), n_exp, 1, LANES), F32)],
        compiler_params=_params("parallel"),
        name="route",
    )(aff)


def _slot_pitch(slots):
    return slots + SUBLANES


def _zero_slot_padding(ref, lead, slots, pitch):
    for q in range(SUBLANES):
        ref[lead + (pl.ds(q * pitch + slots, pitch - slots), slice(None))] = jnp.zeros((pitch - slots, LANES), ref.dtype)


def _slot_pitch_bf16(slots):
    return -(-slots // BF16_ROWS) * BF16_ROWS + BF16_ROWS


DISPATCH_EXPERTS = 4


def _gather_kernel(slots, pitch, pitch_out, idx_ref, h_ref, o_ref, tile_scr):
    for g in range(DISPATCH_EXPERTS):
        for mi in range(slots):
            r = pl.multiple_of(idx_ref[g, 0, mi] * SUBLANES, SUBLANES)
            tile_scr[g, pl.ds(mi, SUBLANES, stride=pitch), :] = h_ref[0, pl.ds(r, SUBLANES), :]
        _zero_slot_padding(o_ref, (0, g), slots, pitch_out)
        for q in range(SUBLANES):
            o_ref[0, g, q * pitch_out:q * pitch_out + slots, :] = (
                tile_scr[g, q * pitch:q * pitch + slots, :].astype(BF16))


def _gather(hg, idx_rows, n_exp, slots):
    b = hg.shape[0]
    pitch = _slot_pitch(slots)
    pitch_out = _slot_pitch_bf16(slots)
    ge = DISPATCH_EXPERTS
    assert n_exp % ge == 0
    return pl.pallas_call(
        functools.partial(_gather_kernel, slots, pitch, pitch_out),
        grid=(b, n_exp // ge),
        in_specs=[
            pl.BlockSpec((ge, 1, slots), lambda i, e: (i * (n_exp // ge) + e, 0, 0), memory_space=pltpu.SMEM),
            pl.BlockSpec((1,) + hg.shape[1:], lambda i, e: (i, 0, 0)),
        ],
        out_specs=pl.BlockSpec((1, ge, SUBLANES * pitch_out, LANES), lambda i, e: (i, e, 0, 0)),
        out_shape=jax.ShapeDtypeStruct((b, n_exp, SUBLANES * pitch_out, LANES), BF16),
        scratch_shapes=[pltpu.VMEM((ge, SUBLANES * pitch, LANES), F32)],
        compiler_params=_params("parallel", "arbitrary"),
        name="gather",
    )(idx_rows, hg)


FFN_SAMPLES = 2
FFN_VMEM_LIMIT_BYTES = 60 * 1024 * 1024


def _ffn_kernel(slots, pitch_in, pitch, xs_ref, wg_ref, wu_ref, wd_ref, gate_ref, y_ref):
    n = xs_ref.shape[0]
    x = jnp.concatenate(
        [jnp.concatenate([xs_ref[s, 0, q * pitch_in:q * pitch_in + slots, :] for q in range(SUBLANES)], axis=1)
         for s in range(n)], axis=0)
    gate = jnp.concatenate([gate_ref[s, 0] for s in range(n)], axis=0)
    a = _dot(x, wg_ref[0].astype(BF16))
    u = _dot(x, wu_ref[0].astype(BF16))
    y = _dot((_silu(a) * u).astype(BF16), wd_ref[0].astype(BF16)) * gate
    for s in range(n):
        _zero_slot_padding(y_ref, (s, 0), slots, pitch)
        for q in range(SUBLANES):
            y_ref[s, 0, q * pitch:q * pitch + slots, :] = y[s * slots:(s + 1) * slots, q * LANES:(q + 1) * LANES]


def _ffn(xs, w_gate, w_up, w_down, layer, gate, slots):
    b, n_exp = xs.shape[:2]
    pitch = _slot_pitch(slots)
    pitch_in = _slot_pitch_bf16(slots)
    d, f = w_gate.shape[2:]
    ns = FFN_SAMPLES
    assert b % ns == 0
    slot_tile = lambda p: pl.BlockSpec((ns, 1, SUBLANES * p, LANES), lambda e, i: (i, e, 0, 0))
    weight = lambda rows, cols: pl.BlockSpec((None, 1, rows, cols), lambda e, i: (layer, e, 0, 0))
    return pl.pallas_call(
        functools.partial(_ffn_kernel, slots, pitch_in, pitch),
        grid=(n_exp, b // ns),
        in_specs=[
            slot_tile(pitch_in),
            weight(d, f), weight(d, f), weight(f, d),
            pl.BlockSpec((ns, 1, slots, 1), lambda e, i: (i, e, 0, 0)),
        ],
        out_specs=slot_tile(pitch),
        out_shape=jax.ShapeDtypeStruct((b, n_exp, SUBLANES * pitch, LANES), F32),
        compiler_params=_params("parallel", "arbitrary", vmem=FFN_VMEM_LIMIT_BYTES),
        name="ffn",
    )(xs, w_gate, w_up, w_down, gate)


SCATTER_BATCH = 16


def _scatter_kernel(slots, pitch, idx_ref, y_ref, o_ref):
    @pl.when(pl.program_id(1) == 0)
    def _():
        o_ref[...] = jnp.zeros_like(o_ref)

    for g in range(DISPATCH_EXPERTS):
        for m0 in range(0, slots, SCATTER_BATCH):
            rows = [pl.multiple_of(idx_ref[g, 0, m0 + u] * SUBLANES, SUBLANES) for u in range(SCATTER_BATCH)]
            vals = [o_ref[0, pl.ds(rows[u], SUBLANES), :] + y_ref[0, g, pl.ds(m0 + u, SUBLANES, stride=pitch), :]
                    for u in range(SCATTER_BATCH)]
            for u in range(SCATTER_BATCH):
                o_ref[0, pl.ds(rows[u], SUBLANES), :] = vals[u]


def _scatter(y, idx_rows, tu, slots):
    b, n_exp = y.shape[:2]
    pitch = _slot_pitch(slots)
    ge = DISPATCH_EXPERTS
    assert n_exp % ge == 0 and slots % SCATTER_BATCH == 0
    return pl.pallas_call(
        functools.partial(_scatter_kernel, slots, pitch),
        grid=(b, n_exp // ge),
        in_specs=[
            pl.BlockSpec((ge, 1, slots), lambda i, e: (i * (n_exp // ge) + e, 0, 0), memory_space=pltpu.SMEM),
            pl.BlockSpec((1, ge, SUBLANES * pitch, LANES), lambda i, e: (i, e, 0, 0)),
        ],
        out_specs=pl.BlockSpec((1, tu * SUBLANES, LANES), lambda i, e: (i, 0, 0)),
        out_shape=jax.ShapeDtypeStruct((b, tu * SUBLANES, LANES), F32),
        compiler_params=_params("parallel", "arbitrary"),
        name="scatter",
    )(idx_rows, y)


FINAL_PARTS = 4


def _final_kernel(mod_ref, g_ref, *refs):
    o_ref = refs[-1]
    tm = FINAL_TILE
    for n in range(FINAL_PARTS):
        x_ref, f_ref = refs[n], refs[FINAL_PARTS + n]
        x = x_ref[0] + mod_ref[0][M_G2:M_G2 + 1, :] * _from_gather_layout(f_ref, tm)
        o_ref[0, n * tm:(n + 1) * tm, :] = _rms(x) * g_ref[...]


def _final(x, ffn, mrows, final_g, n_ctx):
    b, tu, d = x.shape
    tm = FINAL_TILE
    skip = n_ctx // tm
    parts = FINAL_PARTS
    assert (tu - n_ctx) % (tm * parts) == 0
    tile = lambda n: pl.BlockSpec((1, tm, d), lambda i, j: (i, j * parts + n + skip, 0))
    ftile = lambda n: pl.BlockSpec((1, tm * SUBLANES, LANES), lambda i, j: (i, j * parts + n + skip, 0))
    return pl.pallas_call(
        _final_kernel,
        grid=(b, (tu - n_ctx) // (tm * parts)),
        in_specs=([pl.BlockSpec((1, M_ROWS, d), lambda i, j: (i, 0, 0)), pl.BlockSpec((1, d), lambda i, j: (0, 0))]
                  + [tile(n) for n in range(parts)] + [ftile(n) for n in range(parts)]),
        out_specs=pl.BlockSpec((1, tm * parts, d), lambda i, j: (i, j, 0)),
        out_shape=jax.ShapeDtypeStruct((b, tu - n_ctx, d), F32),
        compiler_params=_params("parallel", "parallel"),
        name="final_norm",
    )(mrows, final_g.reshape(1, d), *([x] * parts + [ffn] * parts))


def _rope_tables(n_lat, n_ctx):
    rows = n_lat // GRID_W
    rowp = jnp.repeat(jnp.arange(rows, dtype=F32), GRID_W)
    colp = jnp.tile(jnp.arange(GRID_W, dtype=F32), rows)
    axis_dim = HEAD_DIM // 2
    inv_freq = ROPE_BASE ** (-jnp.arange(0, axis_dim, 2, dtype=F32) / axis_dim)
    ar = rowp[:, None] * inv_freq
    ac = colp[:, None] * inv_freq
    cs = jnp.concatenate([jnp.cos(ar), jnp.cos(ar), jnp.cos(ac), jnp.cos(ac)], axis=1)
    sn = jnp.concatenate([-jnp.sin(ar), jnp.sin(ar), -jnp.sin(ac), jnp.sin(ac)], axis=1)
    reps = LANES // HEAD_DIM
    cs = jnp.concatenate([jnp.ones((n_ctx, LANES), F32), jnp.tile(cs, (1, reps))], axis=0)
    sn = jnp.concatenate([jnp.zeros((n_ctx, LANES), F32), jnp.tile(sn, (1, reps))], axis=0)
    return cs, sn


def kernel(x, c, ctx, c_ctx, w_mod, b_mod, norm1_g, norm2_g, w_in, conv_w, ret_decay_logit, attn_sink,
           w_out, w_router, w_gate, w_up, w_down, final_g):
    b, n_lat, d = x.shape
    n_ctx = ctx.shape[1]
    depth = w_in.shape[0]
    n_exp = w_router.shape[2]
    tu = n_ctx + n_lat
    assert w_in.shape[2] == IN_WIDTH and tu % INPROJ_TILE == 0
    assert n_ctx % FINAL_TILE == 0 and n_lat % FINAL_TILE == 0
    assert b + 1 <= SUBLANES and n_exp == N_EXPERTS

    c_rows = jnp.concatenate([c, c_ctx[None], jnp.zeros((SUBLANES - b - 1, d), F32)], axis=0)
    mods = _mod_vectors(c_rows, w_mod, b_mod).reshape(depth, SUBLANES, 6, d)
    qdec, win, cd, dm = _decay_tables(ret_decay_logit)
    rope_cs, rope_sn = _rope_tables(n_lat, n_ctx)
    sink_rows = jnp.broadcast_to(jnp.repeat(attn_sink.astype(F32), CHUNK, axis=1)[:, :, None],
                                 (depth, ATTN_HEADS * CHUNK, LANES))
    w_in_bf = w_in.astype(BF16)
    w_out_bf = w_out.astype(BF16)
    wr = jnp.swapaxes(w_router, 1, 2)
    wr_hi = wr.astype(BF16)
    wr_lo = (wr - wr_hi.astype(F32)).astype(BF16)
    g1 = norm1_g.reshape(depth, 1, d)
    g2 = norm2_g.reshape(depth, 1, d)
    mrows = jnp.concatenate([mods[:, :b], jnp.broadcast_to(mods[:, b][:, None], (depth, b, 6, d)),
                             jnp.zeros((depth, b, M_ROWS - 12, d), F32)], axis=2)
    mrows_in = mrows.at[1:, :, M_G2].set(mrows[:-1, :, M_G2])
    mrows_in = mrows_in.at[1:, :, M_CTX + M_G2].set(mrows[:-1, :, M_CTX + M_G2])

    xu = x
    ffn = None
    for l in range(depth):
        first = ctx if l == 0 else None
        xu, proj = _inproj(xu, ffn, mrows_in[l], g1, w_in_bf, l, rope_cs, rope_sn, n_ctx, ctx=first)
        sp = _ret_states(proj[1], win, cd, l, n_ctx)
        xu, hg, aff = _mixer(proj, sp, (dm, qdec), conv_w, sink_rows, xu, mrows[l], g2, w_out_bf, l,
                             wr_hi, wr_lo, n_ctx, ctx=first)
        idx, gate = _route(aff, n_ctx, with_ctx=l < depth - 1)
        cap = idx.shape[3]
        idx_rows = idx.reshape(b * n_exp, 1, cap)
        xs = _gather(hg, idx_rows, n_exp, cap)
        y = _ffn(xs, w_gate, w_up, w_down, l, gate, cap)
        ffn = _scatter(y, idx_rows, tu, cap)
    return _final(xu, ffn, mrows[depth - 1], final_g, n_ctx)
```

```python
import functools

import jax
import jax.numpy as jnp
from jax import lax
from jax.experimental import pallas as pl
from jax.experimental.pallas import tpu as pltpu

F32 = jnp.float32
BF16 = jnp.bfloat16
I32 = jnp.int32

HEAD_DIM = 64
CONV_WIDTH = 256
RET_HEADS = 4
RET_WIDTH = RET_HEADS * HEAD_DIM
ATTN_HEADS = 8
ATTN_KV_HEADS = 2
ATTN_GROUP = ATTN_HEADS // ATTN_KV_HEADS
ATTN_WIDTH = ATTN_HEADS * HEAD_DIM
KV_WIDTH = ATTN_KV_HEADS * HEAD_DIM
CHUNK = 128
GRID_W = 64
N_EXPERTS = 16
CAPACITY_FACTOR = 2
ROPE_BASE = 10000.0
EPS = 1e-6
NEG_INF = -1e30
QK_SCALE = HEAD_DIM ** -0.5
LOG2E = 1.4426950408889634

LANES = 128
SUBLANES = 8
BF16_ROWS = 2 * SUBLANES
VMEM_LIMIT_BYTES = 56 * 1024 * 1024

O_CB = 0
O_CC = O_CB + CONV_WIDTH
O_CX = O_CC + CONV_WIDTH
O_RQ = O_CX + CONV_WIDTH
O_RK = O_RQ + RET_WIDTH
O_RV = O_RK + RET_WIDTH
O_GF = O_RV + RET_WIDTH
O_GB = O_GF + RET_WIDTH
O_AQ = O_GB + RET_WIDTH
O_AK = O_AQ + ATTN_WIDTH
O_AV = O_AK + KV_WIDTH
IN_WIDTH = O_AV + KV_WIDTH

M_SH1, M_SC1, M_G1, M_SH2, M_SC2, M_G2 = range(6)
M_CTX = 6
M_ROWS = 16

INPROJ_TILE = 1088
FINAL_TILE = 256


def _params(*sem, vmem=VMEM_LIMIT_BYTES):
    return pltpu.CompilerParams(dimension_semantics=sem, vmem_limit_bytes=vmem)


def _dot(a, b):
    return jnp.dot(a, b, preferred_element_type=F32)


def _dot_nt(a, b):
    return lax.dot_general(a, b, (((1,), (1,)), ((), ())), preferred_element_type=F32)


def _silu(v):
    return v * jax.nn.sigmoid(v)


def _mod_kernel(c_ref, w_ref, b_ref, o_ref):
    s = _silu(c_ref[...])
    s_hi = s.astype(BF16)
    s_lo = (s - s_hi.astype(F32)).astype(BF16)
    w = w_ref[0]
    w_hi = w.astype(BF16)
    w_lo = (w - w_hi.astype(F32)).astype(BF16)
    n = s.shape[0]
    head = _dot(jnp.concatenate([s_hi, s_lo], axis=0), w_hi)
    o_ref[0] = head[:n] + (head[n:] + _dot(s_hi, w_lo)) + b_ref[0]


def _mod_vectors(c_rows, w_mod, b_mod):
    depth, d_model, width = w_mod.shape
    tn = 1536
    return pl.pallas_call(
        _mod_kernel,
        grid=(depth, width // tn),
        in_specs=[
            pl.BlockSpec((SUBLANES, d_model), lambda l, n: (0, 0)),
            pl.BlockSpec((1, d_model, tn), lambda l, n: (l, 0, n)),
            pl.BlockSpec((1, 1, tn), lambda l, n: (l, 0, n)),
        ],
        out_specs=pl.BlockSpec((1, SUBLANES, tn), lambda l, n: (l, 0, n)),
        out_shape=jax.ShapeDtypeStruct((depth, SUBLANES, width), F32),
        compiler_params=_params("parallel", "parallel"),
        name="mod_vectors",
    )(c_rows, w_mod, b_mod.reshape(depth, 1, width))


def _log_sigmoid(v):
    return -jnp.log(1.0 + jnp.exp(-v))


def _tables_kernel(lgl_ref, lgr_ref, qdec_ref, win_ref, cd_ref, dm_ref):
    pos = lax.broadcasted_iota(I32, (CHUNK, RET_WIDTH), 0).astype(F32)
    ri = lax.broadcasted_iota(I32, (CHUNK, RET_HEADS * CHUNK), 0).astype(F32)
    rj = (lax.broadcasted_iota(I32, (CHUNK, RET_HEADS * CHUNK), 1) & (CHUNK - 1)).astype(F32)
    for d in range(2):
        lg = _log_sigmoid(lgl_ref[0, d])
        lg1 = lg[0:1, :]
        if d == 0:
            qdec_ref[0, d] = jnp.exp(lg1 * (pos + 1.0))
            win_ref[0, d] = jnp.exp(lg1 * (CHUNK - 1.0 - pos))
            diff = ri - rj
        else:
            qdec_ref[0, d] = jnp.exp(lg1 * (CHUNK - pos))
            win_ref[0, d] = jnp.exp(lg1 * pos)
            diff = rj - ri
        cd_ref[0, d] = jnp.exp(lg * float(CHUNK))
        lr = _log_sigmoid(lgr_ref[0, d])
        dm_ref[0, d] = jnp.where(diff >= 0.0, jnp.exp(lr * jnp.maximum(diff, 0.0)), 0.0)


def _decay_tables(ret_decay_logit):
    depth = ret_decay_logit.shape[0]
    lg = ret_decay_logit.astype(F32)
    lgl = jnp.broadcast_to(jnp.repeat(lg, HEAD_DIM, axis=-1)[:, :, None, :], (depth, 2, SUBLANES, RET_WIDTH))
    lgr = jnp.broadcast_to(jnp.repeat(lg, CHUNK, axis=-1)[:, :, None, :], (depth, 2, CHUNK, RET_HEADS * CHUNK))
    return pl.pallas_call(
        _tables_kernel,
        grid=(depth,),
        in_specs=[
            pl.BlockSpec((1, 2, SUBLANES, RET_WIDTH), lambda l: (l, 0, 0, 0)),
            pl.BlockSpec((1, 2, CHUNK, RET_HEADS * CHUNK), lambda l: (l, 0, 0, 0)),
        ],
        out_specs=[
            pl.BlockSpec((1, 2, CHUNK, RET_WIDTH), lambda l: (l, 0, 0, 0)),
            pl.BlockSpec((1, 2, CHUNK, RET_WIDTH), lambda l: (l, 0, 0, 0)),
            pl.BlockSpec((1, 2, SUBLANES, RET_WIDTH), lambda l: (l, 0, 0, 0)),
            pl.BlockSpec((1, 2, CHUNK, RET_HEADS * CHUNK), lambda l: (l, 0, 0, 0)),
        ],
        out_shape=[
            jax.ShapeDtypeStruct((depth, 2, CHUNK, RET_WIDTH), F32),
            jax.ShapeDtypeStruct((depth, 2, CHUNK, RET_WIDTH), F32),
            jax.ShapeDtypeStruct((depth, 2, SUBLANES, RET_WIDTH), F32),
            jax.ShapeDtypeStruct((depth, 2, CHUNK, RET_HEADS * CHUNK), F32),
        ],
        compiler_params=_params("parallel"),
        name="decay_tables",
    )(lgl, lgr)


def _rope(v, cs, sn):
    lane = lax.broadcasted_iota(I32, (1, LANES), 1)
    first = (lane & 31) < 16
    outs = []
    for g in range(v.shape[1] // LANES):
        vg = v[:, g * LANES:(g + 1) * LANES]
        sw = jnp.where(first, pltpu.roll(vg, LANES - 16, 1), pltpu.roll(vg, 16, 1))
        outs.append(vg * cs + sw * sn)
    return outs[0] if len(outs) == 1 else jnp.concatenate(outs, axis=1)


def _swap_halves(v):
    return jnp.concatenate([v[:, HEAD_DIM:], v[:, :HEAD_DIM]], axis=1)


def _from_gather_layout(f_ref, tm):
    return jnp.concatenate(
        [f_ref[0, pl.ds(q, tm, stride=SUBLANES), :] for q in range(SUBLANES)], axis=1)


def _rms(v):
    return v * lax.rsqrt(jnp.mean(v * v, axis=-1, keepdims=True) + EPS)


def _mod_rows(mod_ref, tm, n_ctx, tile):
    m = mod_ref[0]
    if n_ctx % tm == 0:
        is_ctx = tile < n_ctx // tm
    else:
        is_ctx = tile * tm + lax.broadcasted_iota(I32, (tm, 1), 0) < n_ctx

    def row(r):
        return jnp.where(is_ctx, m[M_CTX + r:M_CTX + r + 1, :], m[r:r + 1, :])

    return row


def _inproj_kernel(has_ffn, has_ctx, n_ctx, *refs):
    if has_ffn:
        x_ref, f_ref, mod_ref, g_ref, w_ref, cs_ref, sn_ref = refs[:7]
        outs = refs[7:]
        xo_ref = outs[0]
        outs = outs[1:]
    elif has_ctx:
        x_ref, ctx_ref, mod_ref, g_ref, w_ref, cs_ref, sn_ref = refs[:7]
        outs = refs[7:]
    else:
        x_ref, mod_ref, g_ref, w_ref, cs_ref, sn_ref = refs[:6]
        outs = refs[6:]
    cz_ref, rqkv_ref, gates_ref, aq_ref, kv_ref = outs
    tm = x_ref.shape[1]
    row = _mod_rows(mod_ref, tm, n_ctx, pl.program_id(1))

    x = x_ref[0]
    if has_ctx:
        x = jnp.where(pl.program_id(1) == 0, ctx_ref[0], x)
    if has_ffn:
        x = x + row(M_G2) * _from_gather_layout(f_ref, tm)
        xo_ref[0] = x
    h = (_rms(x) * g_ref[...]) * (1.0 + row(M_SC1)) + row(M_SH1)
    h = h.astype(BF16)
    cs = cs_ref[...]
    sn = sn_ref[...]

    def proj(a, b):
        return _dot(h, w_ref[:, a:b])

    bounds = (O_CB, O_CC, O_CX, O_RQ, O_RK, O_RV, O_GF, O_GB, O_AQ, O_AK, O_AV, IN_WIDTH)
    p = [proj(a, b) for a, b in zip(bounds[:-1], bounds[1:])]
    cz_ref[0, :, 0:CONV_WIDTH] = p[0]
    cz_ref[0, :, CONV_WIDTH:] = p[1] * p[2]
    rqkv_ref[0, :, 0:RET_WIDTH] = _rope(p[3], cs, sn).astype(BF16)
    rqkv_ref[0, :, RET_WIDTH:2 * RET_WIDTH] = _rope(p[4] * QK_SCALE, cs, sn).astype(BF16)
    rqkv_ref[0, :, 2 * RET_WIDTH:] = p[5].astype(BF16)
    gates_ref[0, :, 0:RET_WIDTH] = _silu(p[6])
    gates_ref[0, :, RET_WIDTH:] = _silu(p[7])
    aq_ref[0] = (_rope(p[8], cs, sn) * (QK_SCALE * LOG2E)).astype(BF16)
    ak = _rope(p[9], cs, sn)
    av = p[10]
    kv_ref[0, :, 0:KV_WIDTH] = ak.astype(BF16)
    kv_ref[0, :, KV_WIDTH:2 * KV_WIDTH] = _swap_halves(ak).astype(BF16)
    kv_ref[0, :, 2 * KV_WIDTH:] = av.astype(BF16)


def _layer_block(stacked, layer):
    rest = stacked.shape[1:]
    return pl.BlockSpec((None,) + rest, lambda *_: (layer,) + (0,) * len(rest))


def _inproj(x, ffn, mrows, norm_g, w_in_bf, layer, rope_cs, rope_sn, n_ctx, ctx=None):
    b, tu, d = x.shape
    tm = INPROJ_TILE
    has_ffn = ffn is not None
    has_ctx = ctx is not None
    assert not (has_ffn and has_ctx)
    tok = lambda width: pl.BlockSpec((1, tm, width), lambda i, j: (i, j, 0))
    in_specs = [tok(d)]
    args = [x]
    if has_ctx:
        tm = ctx.shape[1]
        tu = tu + tm
        assert tm == n_ctx and x.shape[1] % tm == 0
        in_specs = [pl.BlockSpec((1, tm, d), lambda i, j: (i, jnp.maximum(j - 1, 0), 0)),
                    pl.BlockSpec((1, tm, d), lambda i, j: (i, 0, 0))]
        args = [x, ctx]
    if has_ffn:
        in_specs.append(pl.BlockSpec((1, tm * SUBLANES, LANES), lambda i, j: (i, j, 0)))
        args.append(ffn)
    in_specs += [
        pl.BlockSpec((1, M_ROWS, d), lambda i, j: (i, 0, 0)),
        _layer_block(norm_g, layer),
        pl.BlockSpec((None, d, IN_WIDTH), lambda i, j: (layer, 0, 0), pipeline_mode=pl.Buffered(1)),
        pl.BlockSpec((tm, LANES), lambda i, j: (j, 0)),
        pl.BlockSpec((tm, LANES), lambda i, j: (j, 0)),
    ]
    args += [mrows, norm_g, w_in_bf, rope_cs, rope_sn]
    widths = [(2 * CONV_WIDTH, F32), (3 * RET_WIDTH, BF16), (2 * RET_WIDTH, F32), (ATTN_WIDTH, BF16),
              (3 * KV_WIDTH, BF16)]
    out_specs = [tok(w) for w, _ in widths]
    out_shape = [jax.ShapeDtypeStruct((b, tu, w), dt) for w, dt in widths]
    if has_ffn:
        out_specs = [tok(d)] + out_specs
        out_shape = [jax.ShapeDtypeStruct((b, tu, d), F32)] + out_shape
    res = pl.pallas_call(
        functools.partial(_inproj_kernel, has_ffn, has_ctx, n_ctx),
        grid=(b, tu // tm),
        in_specs=in_specs,
        out_specs=out_specs,
        out_shape=out_shape,
        compiler_params=_params("parallel", "parallel"),
        name="inproj",
    )(*args)
    if has_ffn:
        return res[0], res[1:]
    return x, res


def _head_block_mask(n):
    r = lax.broadcasted_iota(I32, (n, n), 0) // HEAD_DIM
    c = lax.broadcasted_iota(I32, (n, n), 1) // HEAD_DIM
    return r == c


STATE_UNROLL = 17


def _states_kernel(n_ctx_chunks, n_chunks, rqkv_ref, win_ref, cd_ref, sp_ref, s_scr):
    same_head = _head_block_mask(RET_WIDTH)
    s_scr[...] = jnp.zeros_like(s_scr)

    def chunk_update(d, pos):
        if d == 0:
            c = pos
        else:
            c = jnp.where(pos < n_ctx_chunks, n_ctx_chunks - 1 - pos, n_chunks + n_ctx_chunks - 1 - pos)
        off = pl.multiple_of(c * CHUNK, CHUNK)
        kw = rqkv_ref[0, pl.ds(off, CHUNK), RET_WIDTH:2 * RET_WIDTH].astype(F32) * win_ref[d]
        v = rqkv_ref[0, pl.ds(off, CHUNK), 2 * RET_WIDTH:]
        kw_t = kw.T.astype(BF16)
        yield
        u = _dot(kw_t, v)
        yield
        return c, jnp.where(same_head, u, 0.0)

    def body(i, carry):
        todo = [(d, i * STATE_UNROLL + n) for n in range(STATE_UNROLL) for d in range(2)]
        done = _round_robin([chunk_update(d, pos) for d, pos in todo])
        for (d, _), (c, u) in zip(todo, done):
            s = s_scr[d]
            sp_ref[0, d, pl.ds(c, 1)] = s.astype(BF16)[None]
            s_scr[d] = s * cd_ref[d][0:1, :] + u
        return carry

    assert n_chunks % STATE_UNROLL == 0
    lax.fori_loop(0, n_chunks // STATE_UNROLL, body, 0)


def _ret_states(rqkv, win, cd, layer, n_ctx):
    b, tu, _ = rqkv.shape
    w = RET_WIDTH
    n_chunks = tu // CHUNK
    return pl.pallas_call(
        functools.partial(_states_kernel, n_ctx // CHUNK, n_chunks),
        grid=(b,),
        in_specs=[
            pl.BlockSpec((1, tu, 3 * w), lambda i: (i, 0, 0)),
            _layer_block(win, layer),
            _layer_block(cd, layer),
        ],
        out_specs=pl.BlockSpec((1, 2, n_chunks, w, w), lambda i: (i, 0, 0, 0, 0)),
        out_shape=jax.ShapeDtypeStruct((b, 2, n_chunks, w, w), BF16),
        scratch_shapes=[pltpu.VMEM((2, w, w), F32)],
        compiler_params=_params("parallel"),
        name="ret_states",
    )(rqkv, win, cd)


def _group_mean(t, ones_bd):
    hi = t.astype(BF16)
    lo = (t - hi.astype(F32)).astype(BF16)
    s = _dot(jnp.concatenate([hi, lo], axis=0), ones_bd)
    n = t.shape[0]
    return (s[:n] + s[n:]) * (1.0 / HEAD_DIM)


def _mix_chunk(c, n_ctx_chunks, n_chunks, cz, z_last, z_first, cw, rqkv, gates, sp, dm_ref, qdec_ref,
               qa, kvs, sink_ref):
    is_lat = c >= n_ctx_chunks

    z = cz[:, CONV_WIDTH:]
    row = lax.broadcasted_iota(I32, (CHUNK, 1), 0)
    has_prev = jnp.logical_and(c != 0, c != n_ctx_chunks)
    has_next = jnp.logical_and(c != n_ctx_chunks - 1, c != n_chunks - 1)
    z_before = jnp.where(row == 0, jnp.where(has_prev, z_last, 0.0), pltpu.roll(z, 1, 0))
    z_after = jnp.where(row == CHUNK - 1, jnp.where(has_next, z_first, 0.0), pltpu.roll(z, CHUNK - 1, 0))
    conv = cz[:, :CONV_WIDTH] * (z_before * cw[0:1, :] + z * cw[1:2, :] + z_after * cw[2:3, :])

    q = rqkv[:, 0:RET_WIDTH]
    k = rqkv[:, RET_WIDTH:2 * RET_WIDTH]
    v = rqkv[:, 2 * RET_WIDTH:]
    lane_head = lax.broadcasted_iota(I32, (1, RET_WIDTH), 1) // HEAD_DIM
    kz = jnp.zeros_like(k)
    k_heads = jnp.concatenate([jnp.where(lane_head == hh, k, kz) for hh in range(RET_HEADS)], axis=0)
    v_heads = jnp.concatenate([jnp.where(lane_head == hh, v, kz) for hh in range(RET_HEADS)], axis=0)
    scores = _dot_nt(q, k_heads)

    keys, v_t = kvs
    half = lax.broadcasted_iota(I32, (1, LANES), 1) // HEAD_DIM
    qa_z = jnp.zeros((CHUNK, LANES), BF16)
    combos = [(hk, par) for hk in range(ATTN_KV_HEADS) for par in range(2)]
    n_ctx = keys[0].shape[0] - 3 * CHUNK
    cols2 = 2 * CHUNK
    ik = lax.broadcasted_iota(I32, (CHUNK, cols2), 0)
    iq = lax.broadcasted_iota(I32, (CHUNK, cols2), 1) & (CHUNK - 1)
    off = jnp.full((CHUNK, cols2), NEG_INF, F32)
    bias_prev = jnp.where(jnp.logical_and(jnp.logical_and(is_lat, c - 1 >= n_ctx_chunks), ik >= iq), 0.0, off)
    bias_cur = jnp.where(is_lat, 0.0, off)
    bias_next = jnp.where(jnp.logical_and(jnp.logical_and(is_lat, c + 1 <= n_chunks - 1), ik <= iq), 0.0, off)
    assert n_ctx > 0
    logits = {}
    for hk, par in combos:
        ja, jb = 2 * hk, 2 * hk + 1
        qst = jnp.concatenate([
            jnp.where(half == par, qa[:, ja * LANES:(ja + 1) * LANES], qa_z),
            jnp.where(half == par, qa[:, jb * LANES:(jb + 1) * LANES], qa_z)], axis=0)
        sel = 0 if par == hk else 1
        r = _dot_nt(keys[sel], qst)
        s = jnp.concatenate([r[:CHUNK] + bias_prev, r[CHUNK:2 * CHUNK] + bias_cur,
                             r[2 * CHUNK:3 * CHUNK] + bias_next, r[3 * CHUNK:]], axis=0)
        ha, hb = ATTN_GROUP * hk + par, ATTN_GROUP * hk + par + 2
        snk = jnp.concatenate([sink_ref[ha * CHUNK:ha * CHUNK + 1, :],
                               sink_ref[hb * CHUNK:hb * CHUNK + 1, :]], axis=1) * LOG2E
        logits[hk, par] = (s, snk, jnp.maximum(jnp.max(s, axis=0, keepdims=True), snk))
    yield

    qf = q.astype(F32)
    outs = []
    for d in range(2):
        p = (scores * dm_ref[d]).astype(BF16)
        lhs = jnp.concatenate([p, (qf * qdec_ref[d]).astype(BF16)], axis=1)
        o = _dot(lhs, jnp.concatenate([v_heads, sp[d]], axis=0))
        outs.append(o)
    o2 = jnp.concatenate(outs, axis=0)
    yield

    att = {}
    ones_keys = jnp.ones((BF16_ROWS, keys[0].shape[0]), BF16)
    for n, (hk, par) in enumerate(combos):
        s, snk, mx = logits[hk, par]
        e = jnp.exp2(s - mx).astype(BF16)
        o_t = _dot(jnp.concatenate([v_t[hk * HEAD_DIM:(hk + 1) * HEAD_DIM, :], ones_keys], axis=0), e)
        den = o_t[HEAD_DIM:HEAD_DIM + 1, :] + jnp.exp2(snk - mx)
        att[hk, par] = o_t[:HEAD_DIM, :] * (1.0 / den)
        if n == 0:
            ones_bd = jnp.where(_head_block_mask(RET_WIDTH), 1.0, 0.0).astype(BF16)
            dl = o2 - _group_mean(o2, ones_bd)
            on = dl * lax.rsqrt(_group_mean(dl * dl, ones_bd) + EPS)
            ret = on[:CHUNK] * gates[:, 0:RET_WIDTH] + on[CHUNK:] * gates[:, RET_WIDTH:]
        yield

    cols = []
    for hk in range(ATTN_KV_HEADS):
        cols.append(jnp.concatenate([att[hk, 0][:, :CHUNK], att[hk, 1][:, :CHUNK]], axis=0).T)
        cols.append(jnp.concatenate([att[hk, 0][:, CHUNK:], att[hk, 1][:, CHUNK:]], axis=0).T)
    return jnp.concatenate([conv, ret] + cols, axis=1)


def _round_robin(gens):
    results = [None] * len(gens)
    active = list(range(len(gens)))
    while active:
        for i in list(active):
            try:
                next(gens[i])
            except StopIteration as done:
                results[i] = done.value
                active.remove(i)
    return results


MIX_CHUNKS = 2


def _mixer_kernel(n_ctx_chunks, n_chunks, n_exp, has_ctx,
                  cz_ref, czp_ref, czn_ref, cw_ref, rqkv_ref, gates_ref, sp_ref, dm_ref, qdec_ref,
                  aq_ref, kvp_ref, kvc_ref, kvn_ref, kvx_ref, sink_ref, x_ref, *rest):
    ctx_ref, rest = (rest[0], rest[1:]) if has_ctx else (None, rest)
    mod_ref, g_ref, w_ref, wrh_ref, wrl_ref, xo_ref, hg_ref, aff_ref, mix_scr = rest
    t = pl.program_id(0)
    last = pl.num_programs(0) - 2
    tiles = n_chunks // MIX_CHUNKS

    @pl.when(t == 0)
    def _():
        mix_scr[...] = jnp.zeros_like(mix_scr)

    tail = _outproj_tail(n_ctx_chunks * CHUNK, n_exp, lax.rem(jnp.maximum(t - 1, 0), tiles), mix_scr[...], x_ref,
                         mod_ref, g_ref, w_ref, wrh_ref, wrl_ref, xo_ref, hg_ref, aff_ref, ctx_ref)
    j = lax.rem(jnp.minimum(t, last), tiles)
    cw = cw_ref[...]

    def values_t(blk):
        return blk[:, 2 * KV_WIDTH:].astype(F32).T.astype(BF16)

    kv_blocks = ([kvp_ref[0]] + [kvc_ref[0, s * CHUNK:(s + 1) * CHUNK, :] for s in range(MIX_CHUNKS)] + [kvn_ref[0]])
    kv_ctx = [kvx_ref[0, t * CHUNK:(t + 1) * CHUNK, :] for t in range(kvx_ref.shape[1] // CHUNK)]
    vt_blocks = [values_t(blk) for blk in kv_blocks]
    vt_ctx = [values_t(blk) for blk in kv_ctx]
    gens = []
    for s in range(MIX_CHUNKS):
        lo, hi = s * CHUNK, (s + 1) * CHUNK
        if s == 0:
            z_last = czp_ref[0, SUBLANES - 1:SUBLANES, CONV_WIDTH:]
        else:
            z_last = cz_ref[0, lo - 1:lo, CONV_WIDTH:]
        if s == MIX_CHUNKS - 1:
            z_first = czn_ref[0, 0:1, CONV_WIDTH:]
        else:
            z_first = cz_ref[0, hi:hi + 1, CONV_WIDTH:]
        window = kv_blocks[s:s + 3] + kv_ctx
        keys = tuple(jnp.concatenate([blk[:, i * KV_WIDTH:(i + 1) * KV_WIDTH] for blk in window], axis=0)
                     for i in (0, 1))
        v_t = jnp.concatenate(vt_blocks[s:s + 3] + vt_ctx, axis=1)
        gens.append(_mix_chunk(j * MIX_CHUNKS + s, n_ctx_chunks, n_chunks, cz_ref[0, lo:hi, :], z_last, z_first,
                               cw, rqkv_ref[0, lo:hi, :], gates_ref[0, lo:hi, :],
                               (sp_ref[0, 0, s], sp_ref[0, 1, s]), dm_ref, qdec_ref, aq_ref[0, lo:hi, :],
                               (keys, v_t), sink_ref))
    mixes = [mix.astype(BF16) for mix in _round_robin([tail] + gens)[1:]]
    mix_scr[...] = jnp.concatenate(mixes, axis=0)


def _mixer(proj, sp, tabs, conv_w, sink_rows, x, mrows, norm_g, w_out_bf, layer, wr_hi, wr_lo, n_ctx, ctx=None):
    cz, rqkv, gates, aq, kv = proj
    dm, qdec = tabs
    b, tu, _ = cz.shape
    d = x.shape[2]
    n_exp = wr_hi.shape[1]
    n_chunks = tu // CHUNK
    mc = MIX_CHUNKS
    rows = mc * CHUNK
    assert n_chunks % mc == 0 and (n_ctx // CHUNK) % mc == 0
    last = n_chunks - 1
    tiles = n_chunks // mc
    per8 = rows // SUBLANES

    def mixed(t):
        m = jnp.minimum(t, b * tiles - 1)
        return m // tiles, lax.rem(m, tiles)

    def projected(t):
        p = jnp.maximum(t - 1, 0)
        return p // tiles, lax.rem(p, tiles)

    def at_mixed(block, tile_index):
        def index(t):
            i, j = mixed(t)
            return (i, tile_index(j), 0)
        return pl.BlockSpec(block, index)

    cur = lambda w: at_mixed((1, rows, w), lambda j: j)

    def out(r, w):
        return pl.BlockSpec((1, r, w), lambda t: projected(t) + (0,))

    if ctx is None:
        residual_specs, residual_args = [out(rows, d)], [x]
    else:
        assert ctx.shape[1] == rows and x.shape[1] == tu - rows

        def latent_tile(t):
            i, j = projected(t)
            return (i, jnp.maximum(j - 1, 0), 0)

        residual_specs = [pl.BlockSpec((1, rows, d), latent_tile),
                          pl.BlockSpec((1, rows, d), lambda t: (projected(t)[0], 0, 0))]
        residual_args = [x, ctx]
    wcz, wkv = cz.shape[2], kv.shape[2]
    in_specs = [
        cur(wcz),
        at_mixed((1, SUBLANES, wcz), lambda j: jnp.maximum(j * per8 - 1, 0)),
        at_mixed((1, SUBLANES, wcz), lambda j: jnp.minimum((j + 1) * per8, tu // SUBLANES - 1)),
        _layer_block(conv_w, layer),
        cur(rqkv.shape[2]), cur(gates.shape[2]),
        pl.BlockSpec((1, 2, mc, RET_WIDTH, RET_WIDTH), lambda t: (mixed(t)[0], 0, mixed(t)[1], 0, 0)),
        _layer_block(dm, layer), _layer_block(qdec, layer),
        cur(ATTN_WIDTH),
        at_mixed((1, CHUNK, wkv), lambda j: jnp.maximum(j * mc - 1, 0)),
        cur(wkv),
        at_mixed((1, CHUNK, wkv), lambda j: jnp.minimum((j + 1) * mc, last)),
        at_mixed((1, n_ctx, wkv), lambda j: 0),
        _layer_block(sink_rows, layer),
        *residual_specs,
        pl.BlockSpec((1, M_ROWS, d), lambda t: (projected(t)[0], 0, 0)),
        _layer_block(norm_g, layer),
        _layer_block(w_out_bf, layer),
        _layer_block(wr_hi, layer), _layer_block(wr_lo, layer),
    ]
    args = [cz, cz, cz, conv_w, rqkv, gates, sp, dm, qdec, aq, kv, kv, kv, kv, sink_rows,
            *residual_args, mrows, norm_g, w_out_bf, wr_hi, wr_lo]
    return pl.pallas_call(
        functools.partial(_mixer_kernel, n_ctx // CHUNK, n_chunks, n_exp, ctx is not None),
        grid=(b * tiles + 1,),
        in_specs=in_specs,
        out_specs=[out(rows, d), out(rows * SUBLANES, LANES),
                   pl.BlockSpec((1, n_exp, rows), lambda t: (projected(t)[0], 0, projected(t)[1]))],
        out_shape=[
            jax.ShapeDtypeStruct((b, tu, d), F32),
            jax.ShapeDtypeStruct((b, tu * SUBLANES, LANES), F32),
            jax.ShapeDtypeStruct((b, n_exp, tu), F32),
        ],
        scratch_shapes=[pltpu.VMEM((rows, CONV_WIDTH + RET_WIDTH + ATTN_WIDTH), BF16)],
        compiler_params=_params("arbitrary"),
        name="mixer",
    )(*args)


def _outproj_tail(n_ctx, n_exp, tile, mix, x_ref, mod_ref, g_ref, w_ref, wrh_ref, wrl_ref, xo_ref, hg_ref, aff_ref,
                  ctx_ref=None):
    tm = x_ref.shape[1]
    row = _mod_rows(mod_ref, tm, n_ctx, tile)

    y = _dot(mix, w_ref[...])
    yield
    x_in = x_ref[0] if ctx_ref is None else jnp.where(tile == 0, ctx_ref[0], x_ref[0])
    x = x_in + row(M_G1) * y
    xo_ref[0] = x
    h = (_rms(x) * g_ref[...]) * (1.0 + row(M_SC2)) + row(M_SH2)
    for q in range(SUBLANES):
        hg_ref[0, pl.ds(q, tm, stride=SUBLANES), :] = h[:, q * LANES:(q + 1) * LANES]
    yield
    h_hi = h.astype(BF16)
    h_lo = (h - h_hi.astype(F32)).astype(BF16)
    logits = (_dot_nt(wrh_ref[...], h_hi) + (_dot_nt(wrh_ref[...], h_lo) + _dot_nt(wrl_ref[...], h_hi)))
    yield
    e = jnp.exp(logits - jnp.max(logits, axis=0, keepdims=True))
    aff_ref[0] = e / jnp.sum(e, axis=0, keepdims=True)


MIN_NORMAL_F32_BITS = 0x00800000
ROUTE_TABLES = 4
ROUTE_GROUP = 4


def _route_stream_tables(v, k, tab_scr, ce_scr, cnt_scr, stream, thr):
    n_exp, n = v.shape
    lane = lax.broadcasted_iota(I32, (1, LANES), 1)
    tri_r = lax.broadcasted_iota(I32, (LANES, LANES), 0)
    tri_c = lax.broadcasted_iota(I32, (LANES, LANES), 1)
    upper = jnp.where(tri_r <= tri_c, 1.0, 0.0).astype(BF16)
    above = pltpu.bitcast(jnp.maximum(thr + 1, MIN_NORMAL_F32_BITS), F32)
    floor = pltpu.bitcast(thr, F32)
    n_gt = jnp.sum(jnp.where(v >= above, 1, 0), axis=1, keepdims=True)
    need = (k - n_gt).astype(F32)
    seen_eq = jnp.zeros((n_exp, 1), F32)
    counts = jnp.zeros((n_exp, LANES), F32)
    for j in range(n // LANES):
        vb = v[:, j * LANES:(j + 1) * LANES]
        gt = vb >= above
        eq = jnp.logical_and(vb >= floor, vb < above)
        both = jnp.concatenate([jnp.where(gt, 1.0, 0.0), jnp.where(eq, 1.0, 0.0)], axis=0).astype(BF16)
        pref = _dot(both, upper)
        rank_eq = pref[n_exp:] + seen_eq
        inc = pref[:n_exp] + jnp.minimum(rank_eq, need) - jnp.minimum(seen_eq, need)
        sel = jnp.logical_or(gt, jnp.logical_and(eq, rank_eq <= need))
        seen_eq = rank_eq[:, LANES - 1:LANES]
        counts = counts + jnp.where(lane == j, inc[:, LANES - 1:LANES], 0.0)
        a0 = jnp.where(sel, vb, 0.0)
        t0 = a0.astype(BF16).astype(F32)
        t1 = (a0 - t0).astype(BF16).astype(F32)
        t2 = a0 - t0 - t1
        for t, val in enumerate((inc, t0, t1, t2)):
            tab_scr[stream * ROUTE_TABLES + t, pl.ds(j, n_exp, stride=LANES), :] = val
    through = _dot(counts.astype(BF16), upper)
    for e in range(n_exp):
        ce_scr[stream, e] = through[e:e + 1, :]
        cnt_scr[stream, e] = counts[e:e + 1, :]


def _route_slots(e, stream, k, lo, tab_scr, ce_scr, cnt_scr):
    lane_f = lax.broadcasted_iota(I32, (1, LANES), 1).astype(F32)
    slot = lax.broadcasted_iota(I32, (k, LANES), 0).astype(F32)
    ones_rows = jnp.ones((LANES, LANES), BF16)
    cnt_rows = jnp.broadcast_to(cnt_scr[stream, e], (LANES, LANES)).astype(BF16)
    rows_e = pl.ds(pl.multiple_of(e * LANES, LANES), LANES)
    tab = jnp.concatenate([tab_scr[stream * ROUTE_TABLES + t, rows_e, :] for t in range(ROUTE_TABLES)],
                          axis=1).astype(BF16)
    before = jnp.where(ce_scr[stream, e] <= slot, 1.0, 0.0).astype(BF16)
    blk = _dot_nt(before, ones_rows)
    base = _dot_nt(before, cnt_rows)
    yield
    row = _dot(jnp.where(lane_f == blk, 1.0, 0.0).astype(BF16), tab)
    yield
    inc = row[:, :LANES]
    aff = row[:, LANES:2 * LANES] + row[:, 2 * LANES:3 * LANES] + row[:, 3 * LANES:]
    local = slot - base
    pos = _dot_nt(jnp.where(inc <= local, 1.0, 0.0).astype(BF16), ones_rows)
    yield
    tok = blk * LANES + pos + float(lo)
    gate = jnp.sum(jnp.where(inc == local + 1.0, aff, 0.0), axis=1, keepdims=True)
    return tok, gate


def _route_kernel(n_exp, streams, aff_ref, idx_ref, gate_ref, tab_scr, ce_scr, cnt_scr):
    a = aff_ref[0]
    tab_scr[...] = jnp.zeros_like(tab_scr)
    vs = [a[:, lo:lo + n] for lo, n, _, _ in streams]

    def enough(v, cand, k):
        return jnp.sum(jnp.where(v >= pltpu.bitcast(cand, F32), 1, 0), axis=1, keepdims=True) >= k

    def search(i, ts):
        out = []
        for t, v, (_, _, k, _) in zip(ts, vs, streams):
            hi = t | jnp.left_shift(jnp.int32(1), 30 - 2 * i)
            lo = jnp.left_shift(jnp.int32(1), 29 - 2 * i)
            out.append(jnp.where(enough(v, hi, k), jnp.where(enough(v, hi | lo, k), hi | lo, hi),
                                 jnp.where(enough(v, t | lo, k), t | lo, t)))
        return tuple(out)

    thrs = lax.fori_loop(0, 15, search, tuple(jnp.zeros((n_exp, 1), I32) for _ in streams))
    thrs = tuple(jnp.where(enough(v, t | 1, k), t | 1, t) for t, v, (_, _, k, _) in zip(thrs, vs, streams))
    for st, (v, (_, _, k, _)) in enumerate(zip(vs, streams)):
        _route_stream_tables(v, k, tab_scr, ce_scr, cnt_scr, st, thrs[st])

    slots = idx_ref.shape[3]

    assert [s[3] for s in streams] == [sum(s[2] for s in streams[:n]) for n in range(len(streams))]

    def expert_group(g, carry):
        experts = [g * ROUTE_GROUP + u for u in range(ROUTE_GROUP)]
        found = _round_robin([_route_slots(e, st, k, lo, tab_scr, ce_scr, cnt_scr)
                              for e in experts for st, (lo, _, k, _) in enumerate(streams)])
        for u, e in enumerate(experts):
            toks = []
            for st, (_, _, k, slot0) in enumerate(streams):
                tok, gate = found[u * len(streams) + st]
                toks.append(tok)
                gate_ref[0, pl.ds(e, 1), slot0:slot0 + k, :] = gate[None]
            if slots % LANES:
                toks.append(jnp.zeros((-slots % LANES, LANES), F32))
            row = jnp.concatenate(toks, axis=0).T[0:1, :slots]
            idx_ref[0, pl.ds(e, 1)] = row.astype(I32)[None]
        return carry

    assert n_exp % ROUTE_GROUP == 0
    lax.fori_loop(0, n_exp // ROUTE_GROUP, expert_group, 0)


def _route(aff, n_ctx, with_ctx):
    b, n_exp, tu = aff.shape
    n_lat = tu - n_ctx
    cap_l = CAPACITY_FACTOR * n_lat // n_exp
    cap_c = CAPACITY_FACTOR * n_ctx // n_exp
    assert n_lat // LANES <= LANES and n_ctx // LANES <= LANES
    streams = ((n_ctx, n_lat, cap_l, 0),) + (((0, n_ctx, cap_c, cap_l),) if with_ctx else ())
    slots = sum(s[2] for s in streams)
    return pl.pallas_call(
        functools.partial(_route_kernel, n_exp, streams),
        grid=(b,),
        in_specs=[pl.BlockSpec((1, n_exp, tu), lambda i: (i, 0, 0))],
        out_specs=[pl.BlockSpec((1, n_exp, 1, slots), lambda i: (i, 0, 0, 0)),
                   pl.BlockSpec((1, n_exp, slots, 1), lambda i: (i, 0, 0, 0))],
        out_shape=[jax.ShapeDtypeStruct((b, n_exp, 1, slots), I32),
                   jax.ShapeDtypeStruct((b, n_exp, slots, 1), F32)],
        scratch_shapes=[pltpu.VMEM((len(streams) * ROUTE_TABLES, n_exp * LANES, LANES), F32),
                        pltpu.VMEM((len(streams), n_exp, 1, LANES), F32),
                        pltpu.VMEM((len(streams), n_exp, 1, LANES), F32)],
        compiler_params=_params("parallel"),
        name="route",
    )(aff)


def _slot_pitch(slots):
    return slots + SUBLANES


def _zero_slot_padding(ref, lead, slots, pitch):
    for q in range(SUBLANES):
        ref[lead + (pl.ds(q * pitch + slots, pitch - slots), slice(None))] = jnp.zeros((pitch - slots, LANES), ref.dtype)


def _slot_pitch_bf16(slots):
    return -(-slots // BF16_ROWS) * BF16_ROWS + BF16_ROWS


DISPATCH_EXPERTS = 4


def _gather_kernel(slots, pitch, pitch_out, idx_ref, h_ref, o_ref, tile_scr):
    for g in range(DISPATCH_EXPERTS):
        for mi in range(slots):
            r = pl.multiple_of(idx_ref[g, 0, mi] * SUBLANES, SUBLANES)
            tile_scr[g, pl.ds(mi, SUBLANES, stride=pitch), :] = h_ref[0, pl.ds(r, SUBLANES), :]
        _zero_slot_padding(o_ref, (0, g), slots, pitch_out)
        for q in range(SUBLANES):
            o_ref[0, g, q * pitch_out:q * pitch_out + slots, :] = (
                tile_scr[g, q * pitch:q * pitch + slots, :].astype(BF16))


def _gather(hg, idx_rows, n_exp, slots):
    b = hg.shape[0]
    pitch = _slot_pitch(slots)
    pitch_out = _slot_pitch_bf16(slots)
    ge = DISPATCH_EXPERTS
    assert n_exp % ge == 0
    return pl.pallas_call(
        functools.partial(_gather_kernel, slots, pitch, pitch_out),
        grid=(b, n_exp // ge),
        in_specs=[
            pl.BlockSpec((ge, 1, slots), lambda i, e: (i * (n_exp // ge) + e, 0, 0), memory_space=pltpu.SMEM),
            pl.BlockSpec((1,) + hg.shape[1:], lambda i, e: (i, 0, 0)),
        ],
        out_specs=pl.BlockSpec((1, ge, SUBLANES * pitch_out, LANES), lambda i, e: (i, e, 0, 0)),
        out_shape=jax.ShapeDtypeStruct((b, n_exp, SUBLANES * pitch_out, LANES), BF16),
        scratch_shapes=[pltpu.VMEM((ge, SUBLANES * pitch, LANES), F32)],
        compiler_params=_params("parallel", "arbitrary"),
        name="gather",
    )(idx_rows, hg)


FFN_SAMPLES = 2
FFN_VMEM_LIMIT_BYTES = 60 * 1024 * 1024


def _ffn_kernel(slots, pitch_in, pitch, xs_ref, wg_ref, wu_ref, wd_ref, gate_ref, y_ref, wg_bf, wu_bf, wd_bf):
    @pl.when(pl.program_id(1) == 0)
    def _():
        wg_bf[...] = wg_ref[0].astype(BF16)
        wu_bf[...] = wu_ref[0].astype(BF16)
        wd_bf[...] = wd_ref[0].astype(BF16)

    n = xs_ref.shape[0]
    x = jnp.concatenate(
        [jnp.concatenate([xs_ref[s, 0, q * pitch_in:q * pitch_in + slots, :] for q in range(SUBLANES)], axis=1)
         for s in range(n)], axis=0)
    gate = jnp.concatenate([gate_ref[s, 0] for s in range(n)], axis=0)
    a = _dot(x, wg_bf[...])
    u = _dot(x, wu_bf[...])
    y = _dot((_silu(a) * u).astype(BF16), wd_bf[...]) * gate
    for s in range(n):
        _zero_slot_padding(y_ref, (s, 0), slots, pitch)
        for q in range(SUBLANES):
            y_ref[s, 0, q * pitch:q * pitch + slots, :] = y[s * slots:(s + 1) * slots, q * LANES:(q + 1) * LANES]


def _ffn(xs, w_gate, w_up, w_down, layer, gate, slots):
    b, n_exp = xs.shape[:2]
    pitch = _slot_pitch(slots)
    pitch_in = _slot_pitch_bf16(slots)
    d, f = w_gate.shape[2:]
    ns = FFN_SAMPLES
    assert b % ns == 0
    slot_tile = lambda p: pl.BlockSpec((ns, 1, SUBLANES * p, LANES), lambda e, i: (i, e, 0, 0))
    weight = lambda rows, cols: pl.BlockSpec((None, 1, rows, cols), lambda e, i: (layer, e, 0, 0))
    return pl.pallas_call(
        functools.partial(_ffn_kernel, slots, pitch_in, pitch),
        grid=(n_exp, b // ns),
        in_specs=[
            slot_tile(pitch_in),
            weight(d, f), weight(d, f), weight(f, d),
            pl.BlockSpec((ns, 1, slots, 1), lambda e, i: (i, e, 0, 0)),
        ],
        out_specs=slot_tile(pitch),
        out_shape=jax.ShapeDtypeStruct((b, n_exp, SUBLANES * pitch, LANES), F32),
        scratch_shapes=[pltpu.VMEM((d, f), BF16), pltpu.VMEM((d, f), BF16), pltpu.VMEM((f, d), BF16)],
        compiler_params=_params("parallel", "arbitrary", vmem=FFN_VMEM_LIMIT_BYTES),
        name="ffn",
    )(xs, w_gate, w_up, w_down, gate)


SCATTER_BATCH = 16


def _scatter_kernel(slots, pitch, idx_ref, y_ref, o_ref):
    @pl.when(pl.program_id(1) == 0)
    def _():
        o_ref[...] = jnp.zeros_like(o_ref)

    for g in range(DISPATCH_EXPERTS):
        for m0 in range(0, slots, SCATTER_BATCH):
            rows = [pl.multiple_of(idx_ref[g, 0, m0 + u] * SUBLANES, SUBLANES) for u in range(SCATTER_BATCH)]
            vals = [o_ref[0, pl.ds(rows[u], SUBLANES), :] + y_ref[0, g, pl.ds(m0 + u, SUBLANES, stride=pitch), :]
                    for u in range(SCATTER_BATCH)]
            for u in range(SCATTER_BATCH):
                o_ref[0, pl.ds(rows[u], SUBLANES), :] = vals[u]


def _scatter(y, idx_rows, tu, slots):
    b, n_exp = y.shape[:2]
    pitch = _slot_pitch(slots)
    ge = DISPATCH_EXPERTS
    assert n_exp % ge == 0 and slots % SCATTER_BATCH == 0
    return pl.pallas_call(
        functools.partial(_scatter_kernel, slots, pitch),
        grid=(b, n_exp // ge),
        in_specs=[
            pl.BlockSpec((ge, 1, slots), lambda i, e: (i * (n_exp // ge) + e, 0, 0), memory_space=pltpu.SMEM),
            pl.BlockSpec((1, ge, SUBLANES * pitch, LANES), lambda i, e: (i, e, 0, 0)),
        ],
        out_specs=pl.BlockSpec((1, tu * SUBLANES, LANES), lambda i, e: (i, 0, 0)),
        out_shape=jax.ShapeDtypeStruct((b, tu * SUBLANES, LANES), F32),
        compiler_params=_params("parallel", "arbitrary"),
        name="scatter",
    )(idx_rows, y)


FINAL_PARTS = 4


def _final_kernel(mod_ref, g_ref, *refs):
    o_ref = refs[-1]
    tm = FINAL_TILE
    for n in range(FINAL_PARTS):
        x_ref, f_ref = refs[n], refs[FINAL_PARTS + n]
        x = x_ref[0] + mod_ref[0][M_G2:M_G2 + 1, :] * _from_gather_layout(f_ref, tm)
        o_ref[0, n * tm:(n + 1) * tm, :] = _rms(x) * g_ref[...]


def _final(x, ffn, mrows, final_g, n_ctx):
    b, tu, d = x.shape
    tm = FINAL_TILE
    skip = n_ctx // tm
    parts = FINAL_PARTS
    assert (tu - n_ctx) % (tm * parts) == 0
    tile = lambda n: pl.BlockSpec((1, tm, d), lambda i, j: (i, j * parts + n + skip, 0))
    ftile = lambda n: pl.BlockSpec((1, tm * SUBLANES, LANES), lambda i, j: (i, j * parts + n + skip, 0))
    return pl.pallas_call(
        _final_kernel,
        grid=(b, (tu - n_ctx) // (tm * parts)),
        in_specs=([pl.BlockSpec((1, M_ROWS, d), lambda i, j: (i, 0, 0)), pl.BlockSpec((1, d), lambda i, j: (0, 0))]
                  + [tile(n) for n in range(parts)] + [ftile(n) for n in range(parts)]),
        out_specs=pl.BlockSpec((1, tm * parts, d), lambda i, j: (i, j, 0)),
        out_shape=jax.ShapeDtypeStruct((b, tu - n_ctx, d), F32),
        compiler_params=_params("parallel", "parallel"),
        name="final_norm",
    )(mrows, final_g.reshape(1, d), *([x] * parts + [ffn] * parts))


def _rope_tables(n_lat, n_ctx):
    rows = n_lat // GRID_W
    rowp = jnp.repeat(jnp.arange(rows, dtype=F32), GRID_W)
    colp = jnp.tile(jnp.arange(GRID_W, dtype=F32), rows)
    axis_dim = HEAD_DIM // 2
    inv_freq = ROPE_BASE ** (-jnp.arange(0, axis_dim, 2, dtype=F32) / axis_dim)
    ar = rowp[:, None] * inv_freq
    ac = colp[:, None] * inv_freq
    cs = jnp.concatenate([jnp.cos(ar), jnp.cos(ar), jnp.cos(ac), jnp.cos(ac)], axis=1)
    sn = jnp.concatenate([-jnp.sin(ar), jnp.sin(ar), -jnp.sin(ac), jnp.sin(ac)], axis=1)
    reps = LANES // HEAD_DIM
    cs = jnp.concatenate([jnp.ones((n_ctx, LANES), F32), jnp.tile(cs, (1, reps))], axis=0)
    sn = jnp.concatenate([jnp.zeros((n_ctx, LANES), F32), jnp.tile(sn, (1, reps))], axis=0)
    return cs, sn


def kernel(x, c, ctx, c_ctx, w_mod, b_mod, norm1_g, norm2_g, w_in, conv_w, ret_decay_logit, attn_sink,
           w_out, w_router, w_gate, w_up, w_down, final_g):
    b, n_lat, d = x.shape
    n_ctx = ctx.shape[1]
    depth = w_in.shape[0]
    n_exp = w_router.shape[2]
    tu = n_ctx + n_lat
    assert w_in.shape[2] == IN_WIDTH and tu % INPROJ_TILE == 0
    assert n_ctx % FINAL_TILE == 0 and n_lat % FINAL_TILE == 0
    assert b + 1 <= SUBLANES and n_exp == N_EXPERTS

    c_rows = jnp.concatenate([c, c_ctx[None], jnp.zeros((SUBLANES - b - 1, d), F32)], axis=0)
    mods = _mod_vectors(c_rows, w_mod, b_mod).reshape(depth, SUBLANES, 6, d)
    qdec, win, cd, dm = _decay_tables(ret_decay_logit)
    rope_cs, rope_sn = _rope_tables(n_lat, n_ctx)
    sink_rows = jnp.broadcast_to(jnp.repeat(attn_sink.astype(F32), CHUNK, axis=1)[:, :, None],
                                 (depth, ATTN_HEADS * CHUNK, LANES))
    w_in_bf = w_in.astype(BF16)
    w_out_bf = w_out.astype(BF16)
    wr = jnp.swapaxes(w_router, 1, 2)
    wr_hi = wr.astype(BF16)
    wr_lo = (wr - wr_hi.astype(F32)).astype(BF16)
    g1 = norm1_g.reshape(depth, 1, d)
    g2 = norm2_g.reshape(depth, 1, d)
    mrows = jnp.concatenate([mods[:, :b], jnp.broadcast_to(mods[:, b][:, None], (depth, b, 6, d)),
                             jnp.zeros((depth, b, M_ROWS - 12, d), F32)], axis=2)
    mrows_in = mrows.at[1:, :, M_G2].set(mrows[:-1, :, M_G2])
    mrows_in = mrows_in.at[1:, :, M_CTX + M_G2].set(mrows[:-1, :, M_CTX + M_G2])

    xu = x
    ffn = None
    for l in range(depth):
        first = ctx if l == 0 else None
        xu, proj = _inproj(xu, ffn, mrows_in[l], g1, w_in_bf, l, rope_cs, rope_sn, n_ctx, ctx=first)
        sp = _ret_states(proj[1], win, cd, l, n_ctx)
        xu, hg, aff = _mixer(proj, sp, (dm, qdec), conv_w, sink_rows, xu, mrows[l], g2, w_out_bf, l,
                             wr_hi, wr_lo, n_ctx, ctx=first)
        idx, gate = _route(aff, n_ctx, with_ctx=l < depth - 1)
        cap = idx.shape[3]
        idx_rows = idx.reshape(b * n_exp, 1, cap)
        xs = _gather(hg, idx_rows, n_exp, cap)
        y = _ffn(xs, w_gate, w_up, w_down, l, gate, cap)
        ffn = _scatter(y, idx_rows, tu, cap)
    return _final(xu, ffn, mrows[depth - 1], final_g, n_ctx)
```
